```python
import jax, jax.numpy as jnp
from jax import lax
import numpy as np

D_MODEL = 1024
BATCH = 8
SEQ = 8192
DEPTH = 2

HEAD_DIM = 64
DSW_GROUPS = ((128, 1), (512, 4), (2048, 16))
DSW_HEADS_PER_GROUP = 4
DSW_HEADS = DSW_HEADS_PER_GROUP * len(DSW_GROUPS)
SB_HEADS = D_MODEL // (4 * HEAD_DIM)
W_A = DSW_HEADS * HEAD_DIM
W_B = SB_HEADS * HEAD_DIM
OUT_A = DSW_HEADS_PER_GROUP * HEAD_DIM
OUT_B = W_B
N_IN = 3 * W_A + 3 * W_B + 2 * D_MODEL
BLOCK = 128
D_FF = 2816
CONV_WIDTH = 3
RMS_EPS = 1e-6

kernel_name = "hybrid_dilated_stickbreaking_gated_block"


def rmsnorm(x, g):
    xf = x.astype(jnp.float32)
    y = xf * lax.rsqrt(jnp.mean(xf * xf, axis=-1, keepdims=True) + RMS_EPS)
    return (y * g.astype(jnp.float32)).astype(x.dtype)


def alibi_slopes(n):
    return jnp.asarray(2.0 ** (-8.0 * np.arange(1, n + 1) / n), dtype=jnp.float32)


def dilated_window_attention(q, k, v, slopes, window, dilation):
    B, S, H, hd = q.shape
    W = window // dilation
    L = S // dilation
    nb = -(-L // W)
    Lp = nb * W

    def to_sub(a):
        a = a.reshape(B, L, dilation, H, hd).transpose(0, 2, 3, 1, 4)
        return jnp.pad(a, ((0, 0), (0, 0), (0, 0), (0, Lp - L), (0, 0)))

    def windows(a):
        ap = jnp.pad(to_sub(a), ((0, 0), (0, 0), (0, 0), (W, 0), (0, 0))).reshape(B, dilation, H, nb + 1, W, hd)
        return jnp.concatenate([ap[:, :, :, :-1], ap[:, :, :, 1:]], axis=-2)

    qs = to_sub(q).reshape(B, dilation, H, nb, W, hd)
    kw = windows(k)
    vw = windows(v).astype(jnp.float32)
    scores = jnp.einsum('brhnqc,brhnkc->brhnqk', qs, kw).astype(jnp.float32) * (hd ** -0.5)
    qi = jnp.arange(W)[:, None]
    kj = jnp.arange(2 * W)[None, :]
    delta = qi + W - kj
    key_sub = (jnp.arange(nb) * W - W)[:, None, None] + kj[None]
    valid = (delta >= 0) & (delta <= W) & (key_sub >= 0)
    bias = -slopes[:, None, None, None] * (delta * dilation).astype(jnp.float32)
    scores = jnp.where(valid, scores + bias, -jnp.inf)
    mx = jnp.max(scores, axis=-1, keepdims=True)
    p = jnp.exp(scores - mx)
    den = jnp.sum(p, axis=-1, keepdims=True)
    o = jnp.einsum('brhnqk,brhnkc->brhnqc', p, vw) / den
    lse = (mx + jnp.log(den))[..., 0]
    o = o.reshape(B, dilation, H, Lp, hd)[:, :, :, :L].transpose(0, 3, 1, 2, 4).reshape(B, S, H, hd)
    lse = lse.reshape(B, dilation, H, Lp)[:, :, :, :L].transpose(0, 3, 1, 2).reshape(B, S, H)
    return o, lse


def dilated_mixture(q, k, v):
    B, S = q.shape[:2]
    slopes = alibi_slopes(DSW_HEADS)
    outs, lses = [], []
    for g, (window, dilation) in enumerate(DSW_GROUPS):
        hs = slice(g * DSW_HEADS_PER_GROUP, (g + 1) * DSW_HEADS_PER_GROUP)
        o, lse = dilated_window_attention(q[:, :, hs], k[:, :, hs], v[:, :, hs], slopes[hs], window, dilation)
        outs.append(o)
        lses.append(lse)
    alpha = jax.nn.softmax(jnp.stack(lses, axis=0), axis=0)
    o = jnp.sum(alpha[..., None] * jnp.stack(outs, axis=0), axis=0)
    return o.reshape(B, S, OUT_A).astype(q.dtype)


def stick_breaking_attention(q, k, v):
    B, S, H, hd = q.shape
    nq = S // BLOCK
    scale = hd ** -0.5
    qb = q.reshape(B, nq, BLOCK, H, hd).transpose(1, 0, 3, 2, 4)
    kt = k.transpose(0, 2, 1, 3)
    vt = v.transpose(0, 2, 1, 3).astype(jnp.float32)
    key_pos = jnp.arange(S)

    def block(args):
        q_blk, i = args
        z = jnp.einsum('bhqc,bhkc->bhqk', q_blk, kt).astype(jnp.float32) * scale
        qpos = i * BLOCK + jnp.arange(BLOCK)
        causal = key_pos[None, :] < qpos[:, None]
        log_stay = jnp.where(causal, jax.nn.log_sigmoid(-z), 0.0)
        later = lax.cumsum(log_stay, axis=3, reverse=True) - log_stay
        attn = jnp.where(causal, jnp.exp(jax.nn.log_sigmoid(z) + later), 0.0)
        return jnp.einsum('bhqk,bhkc->bhqc', attn, vt)

    o = lax.map(block, (qb, jnp.arange(nq)))
    return o.transpose(1, 0, 3, 2, 4).reshape(B, S, H * hd).astype(q.dtype)


def causal_depthwise_conv(a, w, b):
    S = a.shape[1]
    ap = jnp.pad(a, ((0, 0), (CONV_WIDTH - 1, 0), (0, 0)))
    y = b
    for i in range(CONV_WIDTH):
        y = y + ap[:, i:i + S] * w[i]
    return y


def _fwd_setup_inputs(seed: int = 0) -> dict:
    key = jax.random.key(seed)
    ks = jax.random.split(key, 12)
    f32 = jnp.float32
    x = jax.random.normal(ks[0], (BATCH, SEQ, D_MODEL), f32)
    norm1 = 1.0 + 0.05 * jax.random.normal(ks[1], (DEPTH, D_MODEL), f32)
    w_in = jax.random.normal(ks[2], (DEPTH, D_MODEL, N_IN), f32) * D_MODEL ** -0.5
    b_gate = 0.01 * jax.random.normal(ks[3], (DEPTH, 2 * D_MODEL), f32)
    w_br = jax.random.normal(ks[4], (DEPTH, OUT_A + OUT_B, D_MODEL), f32) * OUT_A ** -0.5
    w_o = jax.random.normal(ks[5], (DEPTH, D_MODEL, D_MODEL), f32) * D_MODEL ** -0.5
    norm2 = 1.0 + 0.05 * jax.random.normal(ks[6], (DEPTH, D_MODEL), f32)
    w_up = jax.random.normal(ks[7], (DEPTH, D_MODEL, 2 * D_FF), f32) * D_MODEL ** -0.5
    conv_w = jax.random.normal(ks[8], (DEPTH, CONV_WIDTH, D_FF), f32) * CONV_WIDTH ** -0.5
    conv_b = 0.01 * jax.random.normal(ks[9], (DEPTH, D_FF), f32)
    w_down = jax.random.normal(ks[10], (DEPTH, D_FF, D_MODEL), f32) * D_FF ** -0.5
    norm_f = 1.0 + 0.05 * jax.random.normal(ks[11], (D_MODEL,), f32)
    return {"x": x, "norm1": norm1, "w_in": w_in, "b_gate": b_gate, "w_br": w_br, "w_o": w_o,
            "norm2": norm2, "w_up": w_up, "conv_w": conv_w, "conv_b": conv_b, "w_down": w_down,
            "norm_f": norm_f}


def _fwd_reference(x, norm1, w_in, b_gate, w_br, w_o, norm2, w_up, conv_w, conv_b, w_down, norm_f):
    B, S, _ = x.shape
    splits = [W_A, 2 * W_A, 3 * W_A, 3 * W_A + W_B, 3 * W_A + 2 * W_B, 3 * W_A + 3 * W_B]
    for l in range(DEPTH):
        h = rmsnorm(x, norm1[l])
        proj = h @ w_in[l]
        qa, ka, va, qb, kb, vb, gate_pre = jnp.split(proj, splits, axis=-1)
        hd4 = lambda t, n: t.reshape(B, S, n, HEAD_DIM)
        o_a = dilated_mixture(hd4(qa, DSW_HEADS), hd4(ka, DSW_HEADS), hd4(va, DSW_HEADS))
        o_b = stick_breaking_attention(hd4(qb, SB_HEADS), hd4(kb, SB_HEADS), hd4(vb, SB_HEADS))
        gates = jax.nn.sigmoid(gate_pre + b_gate[l])
        g_a, g_b = gates[..., :D_MODEL], gates[..., D_MODEL:]
        merged = g_a * (o_a @ w_br[l, :OUT_A]) + g_b * (o_b @ w_br[l, OUT_A:])
        x = x + merged @ w_o[l]
        h2 = rmsnorm(x, norm2[l])
        up = h2 @ w_up[l]
        a, v = up[..., :D_FF], up[..., D_FF:]
        a = causal_depthwise_conv(a, conv_w[l], conv_b[l])
        x = x + (jax.nn.gelu(a, approximate=False) * v) @ w_down[l]
    return rmsnorm(x, norm_f)


import jax as _jax
import jax.numpy as _jnp

TWIN_FORMAT = 'train_step'
FWD_PARAMS = ['x', 'norm1', 'w_in', 'b_gate', 'w_br', 'w_o', 'norm2', 'w_up', 'conv_w', 'conv_b', 'w_down', 'norm_f']
TWIN_WEIGHTS = ['norm1', 'w_in', 'b_gate', 'w_br', 'w_o', 'norm2', 'w_up', 'conv_w', 'conv_b', 'w_down', 'norm_f']
TWIN_DIFF_INPUT = 'x'
TWIN_INPUTS = ['x', 'norm1', 'w_in', 'b_gate', 'w_br', 'w_o', 'norm2', 'w_up', 'conv_w', 'conv_b', 'w_down', 'norm_f', 'loss_target', 'm_norm1', 'm_w_in', 'm_b_gate', 'm_w_br', 'm_w_o', 'm_norm2', 'm_w_up', 'm_conv_w', 'm_conv_b', 'm_w_down', 'm_norm_f', 'v_norm1', 'v_w_in', 'v_b_gate', 'v_w_br', 'v_w_o', 'v_norm2', 'v_w_up', 'v_conv_w', 'v_conv_b', 'v_w_down', 'v_norm_f']
TWIN_OUTPUTS = ['loss', 'grad_x', 'grad_norm1', 'grad_w_in', 'grad_b_gate', 'grad_w_br', 'grad_w_o', 'grad_norm2', 'grad_w_up', 'grad_conv_w', 'grad_conv_b', 'grad_w_down', 'grad_norm_f', 'delta_norm1', 'delta_w_in', 'delta_b_gate', 'delta_w_br', 'delta_w_o', 'delta_norm2', 'delta_w_up', 'delta_conv_w', 'delta_conv_b', 'delta_w_down', 'delta_norm_f', 'new_m_norm1', 'new_m_w_in', 'new_m_b_gate', 'new_m_w_br', 'new_m_w_o', 'new_m_norm2', 'new_m_w_up', 'new_m_conv_w', 'new_m_conv_b', 'new_m_w_down', 'new_m_norm_f', 'new_v_norm1', 'new_v_w_in', 'new_v_b_gate', 'new_v_w_br', 'new_v_w_o', 'new_v_norm2', 'new_v_w_up', 'new_v_conv_w', 'new_v_conv_b', 'new_v_w_down', 'new_v_norm_f']
TWIN_LEAF_KINDS = {'loss': 'loss', 'grad_x': 'grad_x', 'grad_norm1': 'grad_w', 'grad_w_in': 'grad_w', 'grad_b_gate': 'grad_w', 'grad_w_br': 'grad_w', 'grad_w_o': 'grad_w', 'grad_norm2': 'grad_w', 'grad_w_up': 'grad_w', 'grad_conv_w': 'grad_w', 'grad_conv_b': 'grad_w', 'grad_w_down': 'grad_w', 'grad_norm_f': 'grad_w', 'delta_norm1': 'delta_w', 'delta_w_in': 'delta_w', 'delta_b_gate': 'delta_w', 'delta_w_br': 'delta_w', 'delta_w_o': 'delta_w', 'delta_norm2': 'delta_w', 'delta_w_up': 'delta_w', 'delta_conv_w': 'delta_w', 'delta_conv_b': 'delta_w', 'delta_w_down': 'delta_w', 'delta_norm_f': 'delta_w', 'new_m_norm1': 'new_m', 'new_m_w_in': 'new_m', 'new_m_b_gate': 'new_m', 'new_m_w_br': 'new_m', 'new_m_w_o': 'new_m', 'new_m_norm2': 'new_m', 'new_m_w_up': 'new_m', 'new_m_conv_w': 'new_m', 'new_m_conv_b': 'new_m', 'new_m_w_down': 'new_m', 'new_m_norm_f': 'new_m', 'new_v_norm1': 'new_v', 'new_v_w_in': 'new_v', 'new_v_b_gate': 'new_v', 'new_v_w_br': 'new_v', 'new_v_w_o': 'new_v', 'new_v_norm2': 'new_v', 'new_v_w_up': 'new_v', 'new_v_conv_w': 'new_v', 'new_v_conv_b': 'new_v', 'new_v_w_down': 'new_v', 'new_v_norm_f': 'new_v'}


def _forward(args):
    return _fwd_reference(*[args[k] for k in FWD_PARAMS])


def _output_shape():
    def fwd():
        inp = _fwd_setup_inputs(0)
        return _fwd_reference(*[inp[k] for k in FWD_PARAMS])
    out = _jax.eval_shape(fwd)
    return out.shape, out.dtype

N_MICROBATCH = 1
ADAM_LR = 0.001
ADAM_B1 = 0.9
ADAM_B2 = 0.999
ADAM_EPS = 1e-08
ADAM_WD = 0.01
ADAM_STEP = 10
PER_EXAMPLE_BATCH_AXIS = {'x': 0, 'loss_target': 0}
SHARED_INPUTS = []
_WEIGHT_DTYPES = {'norm1': _jnp.float32, 'w_in': _jnp.float32, 'b_gate': _jnp.float32, 'w_br': _jnp.float32, 'w_o': _jnp.float32, 'norm2': _jnp.float32, 'w_up': _jnp.float32, 'conv_w': _jnp.float32, 'conv_b': _jnp.float32, 'w_down': _jnp.float32, 'norm_f': _jnp.float32}
MOMENT_SCALE = {'norm1': 1.388325e-01, 'w_in': 6.088851e-02, 'b_gate': 2.817159e-02, 'w_br': 7.261422e-02, 'w_o': 1.027101e-01, 'norm2': 1.963732e-01, 'w_up': 8.176622e-02, 'conv_w': 8.662611e-02, 'conv_b': 8.285448e-02, 'w_down': 1.350048e-01, 'norm_f': 6.413642e+01}


def _to_microbatches(a, axis):
    t = _jnp.moveaxis(a, axis, 0)
    t = t.reshape((N_MICROBATCH, t.shape[0] // N_MICROBATCH) + t.shape[1:])
    return _jnp.moveaxis(t, 1, axis + 1)


def setup_inputs(seed: int = 0) -> dict:
    inp = _fwd_setup_inputs(seed)
    key = _jax.random.fold_in(_jax.random.key(seed), 7919)
    shape, _ = _output_shape()
    out = dict(inp)
    out["loss_target"] = _jax.random.normal(_jax.random.fold_in(key, 0), shape, _jnp.float32)
    for i, name in enumerate(TWIN_WEIGHTS):
        w = inp[name].astype(_jnp.float32)
        if MOMENT_SCALE is None:
            s = _jnp.sqrt(_jnp.mean(_jnp.square(w)) + 1e-30)
        else:
            s = MOMENT_SCALE[name]
        km, kv = _jax.random.split(_jax.random.fold_in(key, i + 1))
        out[name] = w
        out["m_" + name] = s * _jax.random.normal(km, w.shape, _jnp.float32)
        out["v_" + name] = (s * s) * _jax.random.uniform(kv, w.shape, _jnp.float32, 0.5, 1.5)
    if N_MICROBATCH > 1:
        for name, axis in PER_EXAMPLE_BATCH_AXIS.items():
            out[name] = _to_microbatches(out[name], axis)
    return {'x': out['x'], 'norm1': out['norm1'], 'w_in': out['w_in'], 'b_gate': out['b_gate'], 'w_br': out['w_br'], 'w_o': out['w_o'], 'norm2': out['norm2'], 'w_up': out['w_up'], 'conv_w': out['conv_w'], 'conv_b': out['conv_b'], 'w_down': out['w_down'], 'norm_f': out['norm_f'], 'loss_target': out['loss_target'], 'm_norm1': out['m_norm1'], 'm_w_in': out['m_w_in'], 'm_b_gate': out['m_b_gate'], 'm_w_br': out['m_w_br'], 'm_w_o': out['m_w_o'], 'm_norm2': out['m_norm2'], 'm_w_up': out['m_w_up'], 'm_conv_w': out['m_conv_w'], 'm_conv_b': out['m_conv_b'], 'm_w_down': out['m_w_down'], 'm_norm_f': out['m_norm_f'], 'v_norm1': out['v_norm1'], 'v_w_in': out['v_w_in'], 'v_b_gate': out['v_b_gate'], 'v_w_br': out['v_w_br'], 'v_w_o': out['v_w_o'], 'v_norm2': out['v_norm2'], 'v_w_up': out['v_w_up'], 'v_conv_w': out['v_conv_w'], 'v_conv_b': out['v_conv_b'], 'v_w_down': out['v_w_down'], 'v_norm_f': out['v_norm_f']}


def _loss(weights, diff, rest, loss_target):
    with _jax.named_scope("forward"):
        args = {**rest, TWIN_DIFF_INPUT: diff, **{k: w.astype(_WEIGHT_DTYPES[k]) for k, w in weights.items()}}
        y = _forward(args)
    with _jax.named_scope("loss_head"):
        err = _jnp.square(y.astype(_jnp.float32) - loss_target)
        return 0.5 * _jnp.sum(_jnp.mean(err, axis=-1)) if err.ndim else 0.5 * err


def _adamw(w, g, m, v):
    m = ADAM_B1 * m + (1.0 - ADAM_B1) * g
    v = ADAM_B2 * v + (1.0 - ADAM_B2) * _jnp.square(g)
    m_hat = m / (1.0 - ADAM_B1 ** ADAM_STEP)
    v_hat = v / (1.0 - ADAM_B2 ** ADAM_STEP)
    delta = -ADAM_LR * (m_hat / (_jnp.sqrt(v_hat) + ADAM_EPS) + ADAM_WD * w)
    return delta, m, v


def reference(x, norm1, w_in, b_gate, w_br, w_o, norm2, w_up, conv_w, conv_b, w_down, norm_f, loss_target, m_norm1, m_w_in, m_b_gate, m_w_br, m_w_o, m_norm2, m_w_up, m_conv_w, m_conv_b, m_w_down, m_norm_f, v_norm1, v_w_in, v_b_gate, v_w_br, v_w_o, v_norm2, v_w_up, v_conv_w, v_conv_b, v_w_down, v_norm_f):
    given = dict(x=x, norm1=norm1, w_in=w_in, b_gate=b_gate, w_br=w_br, w_o=w_o, norm2=norm2, w_up=w_up, conv_w=conv_w, conv_b=conv_b, w_down=w_down, norm_f=norm_f, loss_target=loss_target, m_norm1=m_norm1, m_w_in=m_w_in, m_b_gate=m_b_gate, m_w_br=m_w_br, m_w_o=m_w_o, m_norm2=m_norm2, m_w_up=m_w_up, m_conv_w=m_conv_w, m_conv_b=m_conv_b, m_w_down=m_w_down, m_norm_f=m_norm_f, v_norm1=v_norm1, v_w_in=v_w_in, v_b_gate=v_b_gate, v_w_br=v_w_br, v_w_o=v_w_o, v_norm2=v_norm2, v_w_up=v_w_up, v_conv_w=v_conv_w, v_conv_b=v_conv_b, v_w_down=v_w_down, v_norm_f=v_norm_f)
    weights = {n: given[n] for n in TWIN_WEIGHTS}
    shared = {n: given[n] for n in SHARED_INPUTS}
    per_example = {n: given[n] for n in ['x']}
    grad_fn = _jax.value_and_grad(_loss, argnums=(0, 1))

    def one_microbatch(ex, loss_target):
        ex = dict(ex)
        diff = ex.pop(TWIN_DIFF_INPUT)
        return grad_fn(weights, diff, {**shared, **ex}, loss_target)

    if N_MICROBATCH == 1:
        loss, (grad_w, grad_x) = one_microbatch(per_example, given["loss_target"])
    else:
        def body(carry, xs):
            loss_sum, grad_sum = carry
            l_k, (gw_k, gx_k) = one_microbatch(xs[0], xs[1])
            with _jax.named_scope("update"):
                return (loss_sum + l_k, _jax.tree.map(_jnp.add, grad_sum, gw_k)), gx_k

        init = (_jnp.zeros((), _jnp.float32), _jax.tree.map(_jnp.zeros_like, weights))
        (loss, grad_w), grad_x = _jax.lax.scan(body, init, (per_example, given["loss_target"]))
    with _jax.named_scope("update"):
        delta_w, new_m, new_v = {}, {}, {}
        for n in TWIN_WEIGHTS:
            delta_w[n], new_m[n], new_v[n] = _adamw(weights[n], grad_w[n], given["m_" + n], given["v_" + n])
    return (loss, grad_x, *[grad_w[n] for n in TWIN_WEIGHTS], *[delta_w[n] for n in TWIN_WEIGHTS],
            *[new_m[n] for n in TWIN_WEIGHTS], *[new_v[n] for n in TWIN_WEIGHTS])
```

```python
import functools
import math

import jax
import jax.numpy as jnp
from jax import lax
from jax.experimental import pallas as pl
from jax.experimental.pallas import tpu as pltpu

BF = jnp.bfloat16
F32 = jnp.float32

N_DEV = 8
D_MODEL = 1024
HEAD_DIM = 64
DSW_GROUPS = ((128, 1), (512, 4), (2048, 16))
HEADS_PER_GROUP = 4
N_GROUPS = len(DSW_GROUPS)
DSW_HEADS = HEADS_PER_GROUP * N_GROUPS
SB_HEADS = 4
W_A = DSW_HEADS * HEAD_DIM
W_B = SB_HEADS * HEAD_DIM
OUT_A = HEADS_PER_GROUP * HEAD_DIM
N_IN = 3 * W_A + 3 * W_B + 2 * D_MODEL
GATE_OFF = 3 * W_A + 3 * W_B
D_FF = 2816
SB_BLOCK = 128
RMS_EPS = 1e-6
ATT_SCALE = HEAD_DIM ** -0.5
NEG = -1e30
SB_EXIT = -110.0

ADAM_LR = 0.001
ADAM_B1 = 0.9
ADAM_B2 = 0.999
ADAM_EPS = 1e-08
ADAM_WD = 0.01
ADAM_STEP = 10

HBM_SPEC = pl.BlockSpec(memory_space=pltpu.HBM)
MESH = pl.DeviceIdType.MESH

_NN = (((1,), (0,)), ((), ()))
_NT = (((1,), (1,)), ((), ()))
_TN = (((0,), (0,)), ((), ()))


def _dot(a, b, dn=_NN):
    return lax.dot_general(a, b, dn, preferred_element_type=F32)


def _pick(dim, pref):
    if dim <= pref:
        return dim
    t = (pref // 128) * 128
    while t >= 128:
        if dim % t == 0:
            return t
        t -= 128
    return dim


def _params(*sem):
    return pltpu.CompilerParams(dimension_semantics=sem)


def _peer(k):
    x, y, c = lax.axis_index("x"), lax.axis_index("y"), lax.axis_index("c")
    px = 1 - x if (k >> 2) & 1 else x
    py = 1 - y if (k >> 1) & 1 else y
    pc = 1 - c if k & 1 else c
    return (px, py, pc), 4 * px + 2 * py + pc


def _all_gather(x, name):
    def body(x_ref, out_ref, send_sems, recv_sems, local_sem):
        _, me = _peer(0)
        mine = pltpu.make_async_copy(x_ref, out_ref.at[me], local_sem)
        mine.start()
        sends = []
        for k in range(1, N_DEV):
            peer, _ = _peer(k)
            cp = pltpu.make_async_remote_copy(
                src_ref=x_ref, dst_ref=out_ref.at[me], send_sem=send_sems.at[k - 1],
                recv_sem=recv_sems.at[k - 1], device_id=peer, device_id_type=MESH)
            cp.start()
            sends.append(cp)
        for k in range(1, N_DEV):
            peer, pidx = _peer(k)
            pltpu.make_async_remote_copy(
                src_ref=x_ref, dst_ref=out_ref.at[pidx], send_sem=send_sems.at[k - 1],
                recv_sem=recv_sems.at[k - 1], device_id=peer, device_id_type=MESH).wait_recv()
        for cp in sends:
            cp.wait_send()
        mine.wait()

    return pl.pallas_call(
        body, name=name,
        out_shape=jax.ShapeDtypeStruct((N_DEV,) + x.shape, x.dtype),
        in_specs=[HBM_SPEC], out_specs=HBM_SPEC,
        scratch_shapes=[pltpu.SemaphoreType.DMA((N_DEV - 1,)), pltpu.SemaphoreType.DMA((N_DEV - 1,)),
                        pltpu.SemaphoreType.DMA],
    )(x)


def _all_to_all(x, name):
    def body(x_ref, out_ref, send_sems, recv_sems, local_sem):
        _, me = _peer(0)
        mine = pltpu.make_async_copy(x_ref.at[me], out_ref.at[me], local_sem)
        mine.start()
        sends = []
        for k in range(1, N_DEV):
            peer, pidx = _peer(k)
            cp = pltpu.make_async_remote_copy(
                src_ref=x_ref.at[pidx], dst_ref=out_ref.at[me], send_sem=send_sems.at[k - 1],
                recv_sem=recv_sems.at[k - 1], device_id=peer, device_id_type=MESH)
            cp.start()
            sends.append(cp)
        for k in range(1, N_DEV):
            peer, pidx = _peer(k)
            pltpu.make_async_remote_copy(
                src_ref=x_ref.at[pidx], dst_ref=out_ref.at[pidx], send_sem=send_sems.at[k - 1],
                recv_sem=recv_sems.at[k - 1], device_id=peer, device_id_type=MESH).wait_recv()
        for cp in sends:
            cp.wait_send()
        mine.wait()

    return pl.pallas_call(
        body, name=name,
        out_shape=jax.ShapeDtypeStruct(x.shape, x.dtype),
        in_specs=[HBM_SPEC], out_specs=HBM_SPEC,
        scratch_shapes=[pltpu.SemaphoreType.DMA((N_DEV - 1,)), pltpu.SemaphoreType.DMA((N_DEV - 1,)),
                        pltpu.SemaphoreType.DMA],
    )(x)


def _matmul(a, b, *, mode, out_dtype, name, tm=512, tn=1024, tk=1024, res=None):
    if mode == "nn":
        (M, K), (_, N) = a.shape, b.shape
    elif mode == "nt":
        (M, K), (N, _) = a.shape, b.shape
    else:
        (K, M), (_, N) = a.shape, b.shape
    tm, tn, tk = _pick(M, tm), _pick(N, tn), _pick(K, tk)
    nk = K // tk
    dn = {"nn": _NN, "nt": _NT, "tn": _TN}[mode]

    def body(*refs):
        if res is not None:
            a_ref, b_ref, r_ref, o_ref, acc = refs
        else:
            a_ref, b_ref, o_ref, acc = refs
        k = pl.program_id(2)

        @pl.when(k == 0)
        def _():
            acc[...] = jnp.zeros_like(acc)

        acc[...] += _dot(a_ref[...].astype(BF), b_ref[...].astype(BF), dn)

        @pl.when(k == nk - 1)
        def _():
            r = acc[...]
            if res is not None:
                r = r + r_ref[...].astype(F32)
            o_ref[...] = r.astype(out_dtype)

    if mode == "tn":
        a_spec = pl.BlockSpec((tk, tm), lambda i, j, k: (k, i))
    else:
        a_spec = pl.BlockSpec((tm, tk), lambda i, j, k: (i, k))
    if mode == "nt":
        b_spec = pl.BlockSpec((tn, tk), lambda i, j, k: (j, k))
    else:
        b_spec = pl.BlockSpec((tk, tn), lambda i, j, k: (k, j))
    o_spec = pl.BlockSpec((tm, tn), lambda i, j, k: (i, j))
    in_specs = [a_spec, b_spec] + ([o_spec] if res is not None else [])
    args = (a, b) + ((res,) if res is not None else ())
    return pl.pallas_call(
        body, name=name,
        out_shape=jax.ShapeDtypeStruct((M, N), out_dtype),
        grid=(M // tm, N // tn, nk),
        in_specs=in_specs, out_specs=o_spec,
        scratch_shapes=[pltpu.VMEM((tm, tn), F32)],
        compiler_params=_params("parallel", "parallel", "arbitrary"),
    )(*args)


def _rms_fwd(x, g, name):
    S, D = x.shape
    tm = _pick(S, 512)

    def body(x_ref, g_ref, h_ref):
        xf = x_ref[...]
        r = lax.rsqrt(jnp.mean(xf * xf, axis=-1, keepdims=True) + RMS_EPS)
        h_ref[...] = (xf * r * g_ref[...]).astype(BF)

    return pl.pallas_call(
        body, name=name,
        out_shape=jax.ShapeDtypeStruct((S, D), BF),
        grid=(S // tm,),
        in_specs=[pl.BlockSpec((tm, D), lambda i: (i, 0)), pl.BlockSpec((1, D), lambda i: (0, 0))],
        out_specs=pl.BlockSpec((tm, D), lambda i: (i, 0)),
        compiler_params=_params("parallel"),
    )(x, g.reshape(1, D))


def _rms_bwd(x, g, dh, dres, name):
    S, D = x.shape
    tm = _pick(S, 512)

    def body(x_ref, g_ref, dh_ref, dres_ref, dx_ref, dg_ref):
        @pl.when(pl.program_id(0) == 0)
        def _():
            dg_ref[...] = jnp.zeros_like(dg_ref)

        xf = x_ref[...]
        r = lax.rsqrt(jnp.mean(xf * xf, axis=-1, keepdims=True) + RMS_EPS)
        xh = xf * r
        dy = dh_ref[...].astype(F32)
        dg_ref[...] += jnp.sum(dy * xh, axis=0, keepdims=True)
        dxh = dy * g_ref[...]
        dx = r * (dxh - xh * jnp.mean(dxh * xh, axis=-1, keepdims=True))
        dx_ref[...] = dres_ref[...] + dx

    row = pl.BlockSpec((tm, D), lambda i: (i, 0))
    vec = pl.BlockSpec((1, D), lambda i: (0, 0))
    return pl.pallas_call(
        body, name=name,
        out_shape=(jax.ShapeDtypeStruct((S, D), F32), jax.ShapeDtypeStruct((1, D), F32)),
        grid=(S // tm,),
        in_specs=[row, vec, row, row], out_specs=(row, vec),
        compiler_params=_params("arbitrary"),
    )(x, g.reshape(1, D), dh, dres)


def _loss_head(x, g, target):
    S, D = x.shape
    tm = _pick(S, 512)

    def body(x_ref, g_ref, t_ref, loss_ref, dx_ref, dg_ref):
        @pl.when(pl.program_id(0) == 0)
        def _():
            dg_ref[...] = jnp.zeros_like(dg_ref)
            loss_ref[...] = jnp.zeros_like(loss_ref)

        xf = x_ref[...]
        gg = g_ref[...]
        r = lax.rsqrt(jnp.mean(xf * xf, axis=-1, keepdims=True) + RMS_EPS)
        xh = xf * r
        err = xh * gg - t_ref[...]
        per_tok = jnp.mean(err * err, axis=-1, keepdims=True)
        loss_ref[...] += 0.5 * jnp.sum(per_tok, axis=0, keepdims=True)
        dy = err * (1.0 / D)
        dg_ref[...] += jnp.sum(dy * xh, axis=0, keepdims=True)
        dxh = dy * gg
        dx_ref[...] = r * (dxh - xh * jnp.mean(dxh * xh, axis=-1, keepdims=True))

    row = pl.BlockSpec((tm, D), lambda i: (i, 0))
    vec = pl.BlockSpec((1, D), lambda i: (0, 0))
    one = pl.BlockSpec((1, 1), lambda i: (0, 0))
    return pl.pallas_call(
        body, name="loss_head",
        out_shape=(jax.ShapeDtypeStruct((1, 1), F32), jax.ShapeDtypeStruct((S, D), F32),
                   jax.ShapeDtypeStruct((1, D), F32)),
        grid=(S // tm,),
        in_specs=[row, vec, row], out_specs=(one, row, vec),
        compiler_params=_params("arbitrary"),
    )(x, g.reshape(1, D), target)


def _slopes(g):
    return [2.0 ** (-8.0 * (HEADS_PER_GROUP * g + j + 1) / DSW_HEADS) for j in range(HEADS_PER_GROUP)]


def _band_masks(W):
    row = lax.broadcasted_iota(jnp.int32, (W, W), 0)
    col = lax.broadcasted_iota(jnp.int32, (W, W), 1)
    d_cur = row - col
    d_prev = d_cur + W
    return d_cur, d_prev, d_cur >= 0, d_cur <= 0


def _attn_a_fwd(q, k, v, g):
    win, dil = DSW_GROUPS[g]
    W = win // dil
    H, d, L, hd = q.shape
    nb = L // W
    slopes = _slopes(g)

    def body(q_ref, kp_ref, kc_ref, vp_ref, vc_ref, o_ref, l_ref):
        n = pl.program_id(1)
        d_cur, d_prev, m_cur, m_prev = _band_masks(W)
        m_prev = jnp.logical_and(m_prev, n > 0)
        for j in range(H):
            sl = slopes[j] * dil
            qh = q_ref[j]
            s_c = _dot(qh, kc_ref[j], _NT) * ATT_SCALE - sl * d_cur.astype(F32)
            s_p = _dot(qh, kp_ref[j], _NT) * ATT_SCALE - sl * d_prev.astype(F32)
            s_c = jnp.where(m_cur, s_c, NEG)
            s_p = jnp.where(m_prev, s_p, NEG)
            m = jnp.maximum(jnp.max(s_c, axis=1, keepdims=True), jnp.max(s_p, axis=1, keepdims=True))
            p_c = jnp.exp(s_c - m)
            p_p = jnp.exp(s_p - m)
            den = jnp.sum(p_c, axis=1, keepdims=True) + jnp.sum(p_p, axis=1, keepdims=True)
            o = _dot(p_c.astype(BF), vc_ref[j]) + _dot(p_p.astype(BF), vp_ref[j])
            o_ref[j] = o / den
            l_ref[j] = jnp.broadcast_to(m + jnp.log(den), (W, hd))

    cur = pl.BlockSpec((H, None, W, hd), lambda r, n: (0, r, n, 0))
    prev = pl.BlockSpec((H, None, W, hd), lambda r, n: (0, r, jnp.maximum(n - 1, 0), 0))
    return pl.pallas_call(
        body, name=f"attn_a_fwd_g{g}",
        out_shape=(jax.ShapeDtypeStruct(q.shape, F32), jax.ShapeDtypeStruct(q.shape, F32)),
        grid=(d, nb),
        in_specs=[cur, prev, cur, prev, cur], out_specs=(cur, cur),
        compiler_params=_params("parallel", "parallel"),
    )(q, k, k, v, v)


def _attn_a_bwd(q, k, v, do, lse, dsum, g):
    win, dil = DSW_GROUPS[g]
    W = win // dil
    H, d, L, hd = q.shape
    nb = L // W
    slopes = _slopes(g)

    def body(q_ref, qn_ref, kp_ref, kc_ref, vp_ref, vc_ref, do_ref, don_ref, l_ref, ln_ref,
             ds_ref, dsn_ref, dq_ref, dk_ref, dv_ref):
        n = pl.program_id(1)
        d_cur, d_prev, m_cur, m_prev = _band_masks(W)
        m_p = jnp.logical_and(m_prev, n > 0)
        m_n = jnp.logical_and(m_prev, n < nb - 1)
        for j in range(H):
            sl = slopes[j] * dil
            b_cur = sl * d_cur.astype(F32)
            b_prev = sl * d_prev.astype(F32)
            qh, qn = q_ref[j], qn_ref[j]
            kc, kp, vc, vp = kc_ref[j], kp_ref[j], vc_ref[j], vp_ref[j]
            dob, donb = do_ref[j].astype(BF), don_ref[j].astype(BF)
            lse_c, lse_n = l_ref[j][:, 0:1], ln_ref[j][:, 0:1]
            dsum_c, dsum_n = ds_ref[j][:, 0:1], dsn_ref[j][:, 0:1]
            p_cc = jnp.exp(jnp.where(m_cur, _dot(qh, kc, _NT) * ATT_SCALE - b_cur, NEG) - lse_c)
            p_cp = jnp.exp(jnp.where(m_p, _dot(qh, kp, _NT) * ATT_SCALE - b_prev, NEG) - lse_c)
            p_nc = jnp.exp(jnp.where(m_n, _dot(qn, kc, _NT) * ATT_SCALE - b_prev, NEG) - lse_n)
            ds_cc = (p_cc * (_dot(dob, vc, _NT) - dsum_c) * ATT_SCALE).astype(BF)
            ds_cp = (p_cp * (_dot(dob, vp, _NT) - dsum_c) * ATT_SCALE).astype(BF)
            ds_nc = (p_nc * (_dot(donb, vc, _NT) - dsum_n) * ATT_SCALE).astype(BF)
            dq_ref[j] = (_dot(ds_cc, kc) + _dot(ds_cp, kp)).astype(BF)
            dk_ref[j] = (_dot(ds_cc, qh, _TN) + _dot(ds_nc, qn, _TN)).astype(BF)
            dv_ref[j] = (_dot(p_cc.astype(BF), dob, _TN) + _dot(p_nc.astype(BF), donb, _TN)).astype(BF)

    cur = pl.BlockSpec((H, None, W, hd), lambda r, n: (0, r, n, 0))
    prev = pl.BlockSpec((H, None, W, hd), lambda r, n: (0, r, jnp.maximum(n - 1, 0), 0))
    nxt = pl.BlockSpec((H, None, W, hd), lambda r, n: (0, r, jnp.minimum(n + 1, nb - 1), 0))
    out = jax.ShapeDtypeStruct(q.shape, BF)
    return pl.pallas_call(
        body, name=f"attn_a_bwd_g{g}",
        out_shape=(out, out, out),
        grid=(d, nb),
        in_specs=[cur, nxt, prev, cur, prev, cur, cur, nxt, cur, nxt, cur, nxt],
        out_specs=(cur, cur, cur),
        compiler_params=_params("parallel", "parallel"),
    )(q, q, k, k, v, v, do, do, lse, lse, dsum, dsum)


def _softplus_parts(z):
    e = jnp.exp(-jnp.abs(z))
    log1p_e = jnp.where(e < 1e-3, e * (1.0 - e * (0.5 - e * (1.0 / 3.0))), jnp.log(1.0 + e))
    return e, jnp.maximum(z, 0.0) + log1p_e


def _split_dot(x, t):
    hi = x.astype(BF)
    lo = (x - hi.astype(F32)).astype(BF)
    return _dot(hi, t) + _dot(lo, t)


def _sb_block(qh, kk, i, kb, r_run, tri_incl, row, col):
    z = _dot(qh, kk, _NT) * ATT_SCALE
    causal = jnp.logical_or(col < row, kb < i)
    e, sp = _softplus_parts(z)
    ls = jnp.where(causal, -sp, 0.0)
    cin = _split_dot(ls, tri_incl)
    a = jnp.where(causal, jnp.exp(z + cin + r_run), 0.0)
    return z, e, causal, cin, a


def _sb_fwd(q, k, v):
    H, S, hd = q.shape
    B = SB_BLOCK
    nq = S // B

    def body(q_ref, k_ref, v_ref, o_ref):
        i = pl.program_id(1)
        qh = q_ref[...]
        row = lax.broadcasted_iota(jnp.int32, (B, B), 0)
        col = lax.broadcasted_iota(jnp.int32, (B, B), 1)
        tri_incl = (row >= col).astype(BF)

        def cond(c):
            kb, _, _, rmax = c
            return jnp.logical_and(kb >= 0, rmax > SB_EXIT)

        def step(c):
            kb, r_run, acc, _ = c
            off = pl.multiple_of(kb * B, B)
            kk = k_ref[pl.ds(off, B), :]
            vv = v_ref[pl.ds(off, B), :]
            _, _, _, cin, a = _sb_block(qh, kk, i, kb, r_run, tri_incl, row, col)
            acc = acc + _dot(a.astype(BF), vv)
            r_run = r_run + cin[:, 0:1]
            return kb - 1, r_run, acc, jnp.max(r_run)

        init = (i, jnp.zeros((B, 1), F32), jnp.zeros((B, hd), F32), jnp.float32(0.0))
        _, _, acc, _ = lax.while_loop(cond, step, init)
        o_ref[...] = acc

    qs = pl.BlockSpec((None, B, hd), lambda h, i: (h, i, 0))
    full = pl.BlockSpec((None, S, hd), lambda h, i: (h, 0, 0))
    return pl.pallas_call(
        body, name="sb_fwd",
        out_shape=jax.ShapeDtypeStruct((H, S, hd), F32),
        grid=(H, nq),
        in_specs=[qs, full, full], out_specs=qs,
        compiler_params=_params("parallel", "parallel"),
    )(q, k, v)


def _sb_bwd(q, k, v, do, o):
    H, S, hd = q.shape
    B = SB_BLOCK
    nq = S // B

    def body(q_ref, k_ref, v_ref, do_ref, o_ref, dq_ref, dk_ref, dv_ref):
        i = pl.program_id(1)

        @pl.when(i == 0)
        def _():
            dk_ref[...] = jnp.zeros_like(dk_ref)
            dv_ref[...] = jnp.zeros_like(dv_ref)

        qh = q_ref[...]
        dob = do_ref[...].astype(BF)
        dsum = jnp.sum(dob.astype(F32) * o_ref[...], axis=1, keepdims=True)
        row = lax.broadcasted_iota(jnp.int32, (B, B), 0)
        col = lax.broadcasted_iota(jnp.int32, (B, B), 1)
        tri_incl = (row >= col).astype(BF)
        tri_strict = (row > col).astype(BF)

        def cond(c):
            kb, _, _, _, rmax = c
            return jnp.logical_and(kb >= 0, rmax > SB_EXIT)

        def step(c):
            kb, r_run, g_run, dq, _ = c
            off = pl.multiple_of(kb * B, B)
            kk = k_ref[pl.ds(off, B), :]
            vv = v_ref[pl.ds(off, B), :]
            z, e, causal, cin, a = _sb_block(qh, kk, i, kb, r_run, tri_incl, row, col)
            a16 = a.astype(BF)
            gmat = a16.astype(F32) * _dot(dob, vv, _NT)
            later = _split_dot(gmat, tri_strict)
            pfx = dsum - g_run - later
            sig = jnp.where(z >= 0, 1.0, e) / (1.0 + e)
            dz = (jnp.where(causal, gmat - sig * pfx, 0.0) * ATT_SCALE).astype(BF)
            dq = dq + _dot(dz, kk)
            dk_ref[pl.ds(off, B), :] += _dot(dz, qh, _TN)
            dv_ref[pl.ds(off, B), :] += _dot(a16, dob, _TN)
            g_run = g_run + jnp.sum(gmat, axis=1, keepdims=True)
            r_run = r_run + cin[:, 0:1]
            return kb - 1, r_run, g_run, dq, jnp.max(r_run)

        init = (i, jnp.zeros((B, 1), F32), jnp.zeros((B, 1), F32), jnp.zeros((B, hd), F32), jnp.float32(0.0))
        _, _, _, dq, _ = lax.while_loop(cond, step, init)
        dq_ref[...] = dq

    qs = pl.BlockSpec((None, B, hd), lambda h, i: (h, i, 0))
    full = pl.BlockSpec((None, S, hd), lambda h, i: (h, 0, 0))
    out = jax.ShapeDtypeStruct((H, S, hd), F32)
    return pl.pallas_call(
        body, name="sb_bwd",
        out_shape=(out, out, out),
        grid=(H, nq),
        in_specs=[qs, full, full, qs, qs], out_specs=(qs, full, full),
        compiler_params=_params("arbitrary", "arbitrary"),
    )(q, k, v, do, o)


def _merge_fwd(o_g, l_g, o_b, proj, b_gate, w_br):
    S = o_b.shape[0]
    D = D_MODEL
    tm = _pick(S, 256)
    gcol = GATE_OFF // D

    def body(o0, o1, o2, l0, l1, l2, ob_ref, ga_ref, gb_ref, bg_ref, w_ref, mg_ref, oa_ref, lse_ref):
        la, lb, lc = l0[...], l1[...], l2[...]
        mx = jnp.maximum(jnp.maximum(la, lb), lc)
        ea, eb, ec = jnp.exp(la - mx), jnp.exp(lb - mx), jnp.exp(lc - mx)
        den = ea + eb + ec
        oa = (ea * o0[...] + eb * o1[...] + ec * o2[...]) / den
        oa_ref[...] = oa
        lse_ref[...] = mx + jnp.log(den)
        ya = _dot(oa.astype(BF), w_ref[0:OUT_A, :])
        yb = _dot(ob_ref[...].astype(BF), w_ref[OUT_A:OUT_A + W_B, :])
        bg = bg_ref[...]
        g_a = jax.nn.sigmoid(ga_ref[...].astype(F32) + bg[:, 0:D])
        g_b = jax.nn.sigmoid(gb_ref[...].astype(F32) + bg[:, D:2 * D])
        mg_ref[...] = (g_a * ya + g_b * yb).astype(BF)

    nar = pl.BlockSpec((tm, OUT_A), lambda i: (i, 0))
    wide = pl.BlockSpec((tm, D), lambda i: (i, 0))
    return pl.pallas_call(
        body, name="merge_fwd",
        out_shape=(jax.ShapeDtypeStruct((S, D), BF), jax.ShapeDtypeStruct((S, OUT_A), F32),
                   jax.ShapeDtypeStruct((S, OUT_A), F32)),
        grid=(S // tm,),
        in_specs=[nar] * 7 + [pl.BlockSpec((tm, D), lambda i: (i, gcol)),
                              pl.BlockSpec((tm, D), lambda i: (i, gcol + 1)),
                              pl.BlockSpec((1, 2 * D), lambda i: (0, 0)),
                              pl.BlockSpec((OUT_A + W_B, D), lambda i: (0, 0))],
        out_specs=(wide, nar, nar),
        compiler_params=_params("parallel"),
    )(*o_g, *l_g, o_b, proj, proj, b_gate.reshape(1, 2 * D), w_br)


def _merge_bwd(dm, oa, ob, proj, b_gate, w_br):
    S = ob.shape[0]
    D = D_MODEL
    tm = _pick(S, 256)
    gcol = GATE_OFF // D

    def body(dm_ref, oa_ref, ob_ref, ga_ref, gb_ref, bg_ref, w_ref,
             dya_ref, dyb_ref, doa_ref, dob_ref, ds_ref, dg_ref, dbg_ref):
        @pl.when(pl.program_id(0) == 0)
        def _():
            dbg_ref[...] = jnp.zeros_like(dbg_ref)

        wa = w_ref[0:OUT_A, :]
        wb = w_ref[OUT_A:OUT_A + W_B, :]
        oa = oa_ref[...]
        ya = _dot(oa.astype(BF), wa)
        yb = _dot(ob_ref[...].astype(BF), wb)
        bg = bg_ref[...]
        g_a = jax.nn.sigmoid(ga_ref[...].astype(F32) + bg[:, 0:D])
        g_b = jax.nn.sigmoid(gb_ref[...].astype(F32) + bg[:, D:2 * D])
        dm = dm_ref[...].astype(F32)
        dga = dm * ya * g_a * (1.0 - g_a)
        dgb = dm * yb * g_b * (1.0 - g_b)
        dg_ref[:, 0:D] = dga.astype(BF)
        dg_ref[:, D:2 * D] = dgb.astype(BF)
        dbg_ref[:, 0:D] += jnp.sum(dga, axis=0, keepdims=True)
        dbg_ref[:, D:2 * D] += jnp.sum(dgb, axis=0, keepdims=True)
        dya = (dm * g_a).astype(BF)
        dyb = (dm * g_b).astype(BF)
        dya_ref[...] = dya
        dyb_ref[...] = dyb
        doa = _dot(dya, wa, _NT)
        doa_ref[...] = doa
        dob_ref[...] = _dot(dyb, wb, _NT)
        r = lax.broadcasted_iota(jnp.int32, (OUT_A, OUT_A), 0) // HEAD_DIM
        c = lax.broadcasted_iota(jnp.int32, (OUT_A, OUT_A), 1) // HEAD_DIM
        ds_ref[...] = _split_dot(doa * oa, (r == c).astype(BF))

    nar = pl.BlockSpec((tm, OUT_A), lambda i: (i, 0))
    wide = pl.BlockSpec((tm, D), lambda i: (i, 0))
    wide2 = pl.BlockSpec((tm, 2 * D), lambda i: (i, 0))
    vec2 = pl.BlockSpec((1, 2 * D), lambda i: (0, 0))
    return pl.pallas_call(
        body, name="merge_bwd",
        out_shape=(jax.ShapeDtypeStruct((S, D), BF), jax.ShapeDtypeStruct((S, D), BF),
                   jax.ShapeDtypeStruct((S, OUT_A), F32), jax.ShapeDtypeStruct((S, W_B), F32),
                   jax.ShapeDtypeStruct((S, OUT_A), F32), jax.ShapeDtypeStruct((S, 2 * D), BF),
                   jax.ShapeDtypeStruct((1, 2 * D), F32)),
        grid=(S // tm,),
        in_specs=[wide, nar, nar, pl.BlockSpec((tm, D), lambda i: (i, gcol)),
                  pl.BlockSpec((tm, D), lambda i: (i, gcol + 1)), vec2,
                  pl.BlockSpec((OUT_A + W_B, D), lambda i: (0, 0))],
        out_specs=(wide, wide, nar, nar, nar, wide2, vec2),
        compiler_params=_params("arbitrary"),
    )(dm, oa, ob, proj, proj, b_gate.reshape(1, 2 * D), w_br)


_SQRT_HALF = 0.7071067811865476
_INV_SQRT_2PI = 0.3989422804014327


def _gelu_parts(a):
    cdf = 0.5 * (1.0 + lax.erf(a * _SQRT_HALF))
    pdf = _INV_SQRT_2PI * jnp.exp(-0.5 * a * a)
    return cdf, pdf


def _shift_down(a, halo, k):
    rows = lax.broadcasted_iota(jnp.int32, a.shape, 0)
    out = pltpu.roll(a, k, 0)
    for r in range(k):
        out = jnp.where(rows == r, halo[8 - k + r:8 - k + r + 1, :], out)
    return out


def _shift_up(a, halo, k):
    n = a.shape[0]
    rows = lax.broadcasted_iota(jnp.int32, a.shape, 0)
    out = pltpu.roll(a, n - k, 0)
    for r in range(k):
        out = jnp.where(rows == n - k + r, halo[r:r + 1, :], out)
    return out


def _conv_in(a_ref, h_ref, first):
    a = a_ref[...].astype(F32)
    halo = jnp.where(first, 0.0, h_ref[...].astype(F32))
    return a, _shift_down(a, halo, 1), _shift_down(a, halo, 2)


def _ffn_specs(S, tm):
    F = D_FF
    t8 = tm // 8
    a_spec = pl.BlockSpec((tm, F), lambda i: (i, 0))
    v_spec = pl.BlockSpec((tm, F), lambda i: (i, 1))
    halo_prev = pl.BlockSpec((8, F), lambda i: (jnp.maximum(i * t8 - 1, 0), 0))
    return a_spec, v_spec, halo_prev


def _ffn_act_fwd(up, conv_w, conv_b):
    S = up.shape[0]
    F = D_FF
    tm = _pick(S, 256)
    a_spec, v_spec, halo_prev = _ffn_specs(S, tm)

    def body(a_ref, h_ref, v_ref, w_ref, b_ref, act_ref):
        a0, a1, a2 = _conv_in(a_ref, h_ref, pl.program_id(0) == 0)
        w = w_ref[...]
        ac = b_ref[...] + w[0:1, :] * a2 + w[1:2, :] * a1 + w[2:3, :] * a0
        cdf, _ = _gelu_parts(ac)
        act_ref[...] = (ac * cdf * v_ref[...].astype(F32)).astype(BF)

    return pl.pallas_call(
        body, name="ffn_act_fwd",
        out_shape=jax.ShapeDtypeStruct((S, F), BF),
        grid=(S // tm,),
        in_specs=[a_spec, halo_prev, v_spec, pl.BlockSpec((3, F), lambda i: (0, 0)),
                  pl.BlockSpec((1, F), lambda i: (0, 0))],
        out_specs=a_spec,
        compiler_params=_params("parallel"),
    )(up, up, up, conv_w, conv_b.reshape(1, F))


def _ffn_act_bwd1(dact, up, conv_w, conv_b):
    S = up.shape[0]
    F = D_FF
    tm = _pick(S, 256)
    a_spec, v_spec, halo_prev = _ffn_specs(S, tm)

    def body(da_ref, a_ref, h_ref, v_ref, w_ref, b_ref, dac_ref, dv_ref, dw_ref, db_ref):
        @pl.when(pl.program_id(0) == 0)
        def _():
            dw_ref[...] = jnp.zeros_like(dw_ref)
            db_ref[...] = jnp.zeros_like(db_ref)

        a0, a1, a2 = _conv_in(a_ref, h_ref, pl.program_id(0) == 0)
        w = w_ref[...]
        ac = b_ref[...] + w[0:1, :] * a2 + w[1:2, :] * a1 + w[2:3, :] * a0
        cdf, pdf = _gelu_parts(ac)
        dact = da_ref[...].astype(F32)
        dv_ref[...] = (dact * ac * cdf).astype(BF)
        dac = dact * v_ref[...].astype(F32) * (cdf + ac * pdf)
        dac_ref[...] = dac.astype(BF)
        db_ref[...] += jnp.sum(dac, axis=0, keepdims=True)
        dw_ref[0:1, :] += jnp.sum(dac * a2, axis=0, keepdims=True)
        dw_ref[1:2, :] += jnp.sum(dac * a1, axis=0, keepdims=True)
        dw_ref[2:3, :] += jnp.sum(dac * a0, axis=0, keepdims=True)

    w_spec = pl.BlockSpec((3, F), lambda i: (0, 0))
    b_spec = pl.BlockSpec((1, F), lambda i: (0, 0))
    return pl.pallas_call(
        body, name="ffn_act_bwd1",
        out_shape=(jax.ShapeDtypeStruct((S, F), BF), jax.ShapeDtypeStruct((S, F), BF),
                   jax.ShapeDtypeStruct((3, F), F32), jax.ShapeDtypeStruct((1, F), F32)),
        grid=(S // tm,),
        in_specs=[a_spec, a_spec, halo_prev, v_spec, w_spec, b_spec],
        out_specs=(a_spec, a_spec, w_spec, b_spec),
        compiler_params=_params("arbitrary"),
    )(dact, up, up, up, conv_w, conv_b.reshape(1, F))


def _ffn_act_bwd2(dac, dv, conv_w):
    S = dac.shape[0]
    F = D_FF
    tm = _pick(S, 256)
    t8 = tm // 8
    nt = S // tm
    cur = pl.BlockSpec((tm, F), lambda i: (i, 0))
    halo_next = pl.BlockSpec((8, F), lambda i: (jnp.minimum((i + 1) * t8, S // 8 - 1), 0))

    def body(d_ref, h_ref, dv_ref, w_ref, o_ref):
        d0 = d_ref[...].astype(F32)
        halo = jnp.where(pl.program_id(0) == nt - 1, 0.0, h_ref[...].astype(F32))
        d1 = _shift_up(d0, halo, 1)
        d2 = _shift_up(d0, halo, 2)
        w = w_ref[...]
        o_ref[:, 0:F] = (w[2:3, :] * d0 + w[1:2, :] * d1 + w[0:1, :] * d2).astype(BF)
        o_ref[:, F:2 * F] = dv_ref[...]

    return pl.pallas_call(
        body, name="ffn_act_bwd2",
        out_shape=jax.ShapeDtypeStruct((S, 2 * F), BF),
        grid=(nt,),
        in_specs=[cur, halo_next, cur, pl.BlockSpec((3, F), lambda i: (0, 0))],
        out_specs=pl.BlockSpec((tm, 2 * F), lambda i: (i, 0)),
        compiler_params=_params("parallel"),
    )(dac, dac, dv, conv_w)


def _adamw(parts, w, m, v, name):
    R, C = w.shape
    tr = R
    for cand in (512, 256, 128, 64, 32, 16):
        if R % cand == 0 and cand * C * 4 <= (1 << 21):
            tr = cand
            break
    c1 = 1.0 / (1.0 - ADAM_B1 ** ADAM_STEP)
    c2 = 1.0 / (1.0 - ADAM_B2 ** ADAM_STEP)

    def body(p_ref, w_ref, m_ref, v_ref, g_ref, d_ref, nm_ref, nv_ref):
        g = p_ref[0].astype(F32)
        for j in range(1, N_DEV):
            g = g + p_ref[j].astype(F32)
        nm = ADAM_B1 * m_ref[...] + (1.0 - ADAM_B1) * g
        nv = ADAM_B2 * v_ref[...] + (1.0 - ADAM_B2) * (g * g)
        g_ref[...] = g
        nm_ref[...] = nm
        nv_ref[...] = nv
        d_ref[...] = -ADAM_LR * ((nm * c1) / (jnp.sqrt(nv * c2) + ADAM_EPS) + ADAM_WD * w_ref[...])

    blk = pl.BlockSpec((tr, C), lambda i: (i, 0))
    out = jax.ShapeDtypeStruct((R, C), F32)
    return pl.pallas_call(
        body, name=name,
        out_shape=(out, out, out, out),
        grid=(R // tr,),
        in_specs=[pl.BlockSpec((N_DEV, tr, C), lambda i: (0, i, 0)), blk, blk, blk],
        out_specs=(blk, blk, blk, blk),
        compiler_params=_params("parallel"),
    )(parts, w, m, v)


def _to_heads(t, nheads):
    S = t.shape[0]
    return t.reshape(S, nheads, HEAD_DIM).transpose(1, 0, 2)


def _from_heads(t):
    H, S, hd = t.shape
    return t.transpose(1, 0, 2).reshape(S, H * hd)


def _to_dilated(t, dil):
    S = t.shape[0]
    H = t.shape[1] // HEAD_DIM
    return t.reshape(S // dil, dil, H, HEAD_DIM).transpose(2, 1, 0, 3)


def _from_dilated(t):
    H, dil, L, hd = t.shape
    return t.transpose(2, 1, 0, 3).reshape(L * dil, H * hd)


def _layer_fwd(x, p, l):
    S = x.shape[0]
    h = _rms_fwd(x, p["norm1"][l], f"rms1_fwd_{l}")
    proj = _matmul(h, p["w_in"][l], mode="nn", out_dtype=BF, name=f"proj_{l}", tm=512, tn=1024, tk=1024)
    o_g, l_g = [], []
    for g, (_, dil) in enumerate(DSW_GROUPS):
        c0 = g * OUT_A
        qg = _to_dilated(proj[:, c0:c0 + OUT_A], dil)
        kg = _to_dilated(proj[:, W_A + c0:W_A + c0 + OUT_A], dil)
        vg = _to_dilated(proj[:, 2 * W_A + c0:2 * W_A + c0 + OUT_A], dil)
        og, lg = _attn_a_fwd(qg, kg, vg, g)
        o_g.append(_from_dilated(og))
        l_g.append(_from_dilated(lg))
    b0 = 3 * W_A
    qb = _to_heads(proj[:, b0:b0 + W_B], SB_HEADS)
    kb = _to_heads(proj[:, b0 + W_B:b0 + 2 * W_B], SB_HEADS)
    vb = _to_heads(proj[:, b0 + 2 * W_B:b0 + 3 * W_B], SB_HEADS)
    ob = _from_heads(_sb_fwd(qb, kb, vb))
    merged, oa, lse = _merge_fwd(o_g, l_g, ob, proj, p["b_gate"][l], p["w_br"][l])
    x1 = _matmul(merged, p["w_o"][l], mode="nn", out_dtype=F32, name=f"wo_{l}", tm=512, tn=1024, tk=1024, res=x)
    h2 = _rms_fwd(x1, p["norm2"][l], f"rms2_fwd_{l}")
    up = _matmul(h2, p["w_up"][l], mode="nn", out_dtype=BF, name=f"up_{l}", tm=512, tn=1408, tk=1024)
    act = _ffn_act_fwd(up, p["conv_w"][l], p["conv_b"][l])
    x2 = _matmul(act, p["w_down"][l], mode="nn", out_dtype=F32, name=f"down_{l}", tm=512, tn=1024, tk=D_FF, res=x1)
    saved = dict(x=x, h=h, proj=proj, oa=oa, ob=ob, lse=lse, merged=merged, x1=x1, h2=h2, up=up, act=act)
    return x2, saved


def _layer_bwd(dx2, sv, p, l):
    gr = {}
    dact = _matmul(dx2, p["w_down"][l], mode="nt", out_dtype=BF, name=f"d_act_{l}", tm=512, tn=1408, tk=1024)
    gr["w_down"] = _matmul(sv["act"], dx2, mode="tn", out_dtype=BF, name=f"dw_down_{l}", tm=1408, tn=1024, tk=512)
    dac, dv, gr["conv_w"], dcb = _ffn_act_bwd1(dact, sv["up"], p["conv_w"][l], p["conv_b"][l])
    gr["conv_b"] = dcb[0]
    dup = _ffn_act_bwd2(dac, dv, p["conv_w"][l])
    dh2 = _matmul(dup, p["w_up"][l], mode="nt", out_dtype=BF, name=f"d_h2_{l}", tm=512, tn=1024, tk=1408)
    gr["w_up"] = _matmul(sv["h2"], dup, mode="tn", out_dtype=BF, name=f"dw_up_{l}", tm=1024, tn=1408, tk=512)
    dx1, dn2 = _rms_bwd(sv["x1"], p["norm2"][l], dh2, dx2, f"rms2_bwd_{l}")
    gr["norm2"] = dn2[0]
    dm = _matmul(dx1, p["w_o"][l], mode="nt", out_dtype=BF, name=f"d_merged_{l}", tm=512, tn=1024, tk=1024)
    gr["w_o"] = _matmul(sv["merged"], dx1, mode="tn", out_dtype=BF, name=f"dw_o_{l}", tm=1024, tn=1024, tk=512)
    dya, dyb, doa, dob, dsum, dgate, dbg = _merge_bwd(dm, sv["oa"], sv["ob"], sv["proj"], p["b_gate"][l], p["w_br"][l])
    gr["b_gate"] = dbg[0]
    dwa = _matmul(sv["oa"], dya, mode="tn", out_dtype=BF, name=f"dw_bra_{l}", tm=256, tn=1024, tk=512)
    dwb = _matmul(sv["ob"], dyb, mode="tn", out_dtype=BF, name=f"dw_brb_{l}", tm=256, tn=1024, tk=512)
    gr["w_br"] = jnp.concatenate([dwa, dwb], axis=0)
    proj = sv["proj"]
    dq_a, dk_a, dv_a = [], [], []
    for g, (_, dil) in enumerate(DSW_GROUPS):
        c0 = g * OUT_A
        qg = _to_dilated(proj[:, c0:c0 + OUT_A], dil)
        kg = _to_dilated(proj[:, W_A + c0:W_A + c0 + OUT_A], dil)
        vg = _to_dilated(proj[:, 2 * W_A + c0:2 * W_A + c0 + OUT_A], dil)
        dqg, dkg, dvg = _attn_a_bwd(qg, kg, vg, _to_dilated(doa, dil), _to_dilated(sv["lse"], dil),
                                    _to_dilated(dsum, dil), g)
        dq_a.append(_from_dilated(dqg))
        dk_a.append(_from_dilated(dkg))
        dv_a.append(_from_dilated(dvg))
    b0 = 3 * W_A
    qb = _to_heads(proj[:, b0:b0 + W_B], SB_HEADS)
    kb = _to_heads(proj[:, b0 + W_B:b0 + 2 * W_B], SB_HEADS)
    vb = _to_heads(proj[:, b0 + 2 * W_B:b0 + 3 * W_B], SB_HEADS)
    dqb, dkb, dvb = _sb_bwd(qb, kb, vb, _to_heads(dob, SB_HEADS), _to_heads(sv["ob"], SB_HEADS))
    dproj = jnp.concatenate(dq_a + dk_a + dv_a + [_from_heads(dqb).astype(BF), _from_heads(dkb).astype(BF),
                                                  _from_heads(dvb).astype(BF), dgate], axis=1)
    dh = _matmul(dproj, p["w_in"][l], mode="nt", out_dtype=BF, name=f"d_h_{l}", tm=512, tn=1024, tk=1280)
    gr["w_in"] = _matmul(sv["h"], dproj, mode="tn", out_dtype=BF, name=f"dw_in_{l}", tm=1024, tn=1280, tk=512)
    dx, dn1 = _rms_bwd(sv["x"], p["norm1"][l], dh, dx1, f"rms1_bwd_{l}")
    gr["norm1"] = dn1[0]
    return dx, gr


_COL_SHARDED = ("w_in", "w_br", "w_up")
_ROW_SHARDED = ("w_o", "w_down")


def _gather_cols(shard, name):
    g = _all_gather(shard, name)
    _, L, K, n = g.shape
    return g.transpose(1, 2, 0, 3).reshape(L, K, N_DEV * n)


def _gather_rows(shard, name):
    g = _all_gather(shard, name)
    _, L, k, N = g.shape
    return g.transpose(1, 0, 2, 3).reshape(L, N_DEV * k, N)


def _scatter_cols(full, name):
    L, K, N = full.shape
    n = N // N_DEV
    blocks = full.reshape(L, K, N_DEV, n).transpose(2, 0, 1, 3)
    return _all_to_all(blocks, name).reshape(N_DEV, L * K, n)


def _scatter_rows(full, name):
    L, K, N = full.shape
    k = K // N_DEV
    blocks = full.reshape(L, N_DEV, k, N).transpose(1, 0, 2, 3)
    return _all_to_all(blocks, name).reshape(N_DEV, L * k, N)


def kernel(x, norm1, w_in, b_gate, w_br, w_o, norm2, w_up, conv_w, conv_b, w_down, norm_f, loss_target, m_norm1, m_w_in, m_b_gate, m_w_br, m_w_o, m_norm2, m_w_up, m_conv_w, m_conv_b, m_w_down, m_norm_f, v_norm1, v_w_in, v_b_gate, v_w_br, v_w_o, v_norm2, v_w_up, v_conv_w, v_conv_b, v_w_down, v_norm_f):
    depth = norm1.shape[0]
    me = 4 * lax.axis_index("x") + 2 * lax.axis_index("y") + lax.axis_index("c")
    shards = dict(w_in=w_in, w_br=w_br, w_o=w_o, w_up=w_up, w_down=w_down)
    moments_m = dict(norm1=m_norm1, w_in=m_w_in, b_gate=m_b_gate, w_br=m_w_br, w_o=m_w_o, norm2=m_norm2,
                     w_up=m_w_up, conv_w=m_conv_w, conv_b=m_conv_b, w_down=m_w_down, norm_f=m_norm_f)
    moments_v = dict(norm1=v_norm1, w_in=v_w_in, b_gate=v_b_gate, w_br=v_w_br, w_o=v_w_o, norm2=v_norm2,
                     w_up=v_w_up, conv_w=v_conv_w, conv_b=v_conv_b, w_down=v_w_down, norm_f=v_norm_f)

    p = dict(norm1=norm1, b_gate=b_gate, norm2=norm2, conv_b=conv_b)
    for n in _COL_SHARDED:
        p[n] = _gather_cols(shards[n].astype(BF), f"gather_{n}")
    for n in _ROW_SHARDED:
        p[n] = _gather_rows(shards[n].astype(BF), f"gather_{n}")
    cw = _all_gather(conv_w, "gather_conv_w")
    p["conv_w"] = cw.transpose(1, 2, 0, 3).reshape(depth, 3, D_FF)

    xs = x[0]
    saved = []
    for l in range(depth):
        xs, sv = _layer_fwd(xs, p, l)
        saved.append(sv)
    loss_part, dx, dnf = _loss_head(xs, norm_f, loss_target[0])
    loss = lax.psum(loss_part[0, 0], ("x", "y", "c"))

    grads = [None] * depth
    for l in reversed(range(depth)):
        dx, grads[l] = _layer_bwd(dx, saved[l], p, l)
    grad_x = dx[None]

    out_g, out_d, out_m, out_v = {}, {}, {}, {}
    for n in _COL_SHARDED + _ROW_SHARDED:
        full = jnp.stack([grads[l][n] for l in range(depth)])
        parts = _scatter_cols(full, f"scatter_{n}") if n in _COL_SHARDED else _scatter_rows(full, f"scatter_{n}")
        shp = shards[n].shape
        flat = (shp[0] * shp[1], shp[2])
        res = _adamw(parts, shards[n].reshape(flat), moments_m[n].reshape(flat), moments_v[n].reshape(flat),
                     f"adamw_{n}")
        out_g[n], out_d[n], out_m[n], out_v[n] = [r.reshape(shp) for r in res]

    small = ("norm1", "b_gate", "norm2", "conv_b")
    vecs = [jnp.stack([grads[l][n] for l in range(depth)]).reshape(-1) for n in small]
    vecs.append(dnf.reshape(-1))
    vecs.append(jnp.stack([grads[l]["conv_w"] for l in range(depth)]).reshape(-1))
    sizes = [v.shape[0] for v in vecs]
    flat = jnp.concatenate(vecs)
    n_small = sum(sizes[:-1])
    pad = (-flat.shape[0]) % 1024
    flat = jnp.pad(flat, (0, pad)).reshape(-1, 128)
    allp = _all_gather(flat, "gather_small_grads").reshape(N_DEV, -1)
    rep_w = jnp.concatenate([norm1.reshape(-1), b_gate.reshape(-1), norm2.reshape(-1), conv_b.reshape(-1), norm_f])
    rep_m = jnp.concatenate([moments_m[n].reshape(-1) for n in small] + [m_norm_f])
    rep_v = jnp.concatenate([moments_v[n].reshape(-1) for n in small] + [v_norm_f])
    rows = n_small // 128
    res = _adamw(allp[:, :n_small].reshape(N_DEV, rows, 128), rep_w.reshape(rows, 128), rep_m.reshape(rows, 128),
                 rep_v.reshape(rows, 128), "adamw_small")
    off = 0
    for n, sz in zip(small + ("norm_f",), sizes[:-1]):
        shp = norm_f.shape if n == "norm_f" else p[n].shape
        out_g[n], out_d[n], out_m[n], out_v[n] = [r.reshape(-1)[off:off + sz].reshape(shp) for r in res]
        off += sz
    f = conv_w.shape[2]
    cwp = allp[:, n_small:n_small + sizes[-1]].reshape(N_DEV, depth * 3, D_FF)
    cwp = lax.dynamic_slice_in_dim(cwp, me * f, f, axis=2)
    res = _adamw(cwp, conv_w.reshape(depth * 3, f), m_conv_w.reshape(depth * 3, f), v_conv_w.reshape(depth * 3, f),
                 "adamw_conv_w")
    out_g["conv_w"], out_d["conv_w"], out_m["conv_w"], out_v["conv_w"] = [r.reshape(conv_w.shape) for r in res]

    order = ("norm1", "w_in", "b_gate", "w_br", "w_o", "norm2", "w_up", "conv_w", "conv_b", "w_down", "norm_f")
    return (loss, grad_x, *[out_g[n] for n in order], *[out_d[n] for n in order],
            *[out_m[n] for n in order], *[out_v[n] for n in order])
```

```python
import functools

import jax
import jax.numpy as jnp
from jax import lax
from jax.experimental import pallas as pl
from jax.experimental.pallas import tpu as pltpu

BF = jnp.bfloat16
F32 = jnp.float32

N_DEV = 8
D_MODEL = 1024
HEAD_DIM = 64
DSW_GROUPS = ((128, 1), (512, 4), (2048, 16))
HEADS_PER_GROUP = 4
N_GROUPS = len(DSW_GROUPS)
DSW_HEADS = HEADS_PER_GROUP * N_GROUPS
SB_HEADS = 4
W_A = DSW_HEADS * HEAD_DIM
W_B = SB_HEADS * HEAD_DIM
OUT_A = HEADS_PER_GROUP * HEAD_DIM
N_IN = 3 * W_A + 3 * W_B + 2 * D_MODEL
GATE_OFF = 3 * W_A + 3 * W_B
D_FF = 2816
SB_BLOCK = 128
RMS_EPS = 1e-6
ATT_SCALE = HEAD_DIM ** -0.5
NEG = -1e30
SB_EXIT = -110.0

ADAM_LR = 0.001
ADAM_B1 = 0.9
ADAM_B2 = 0.999
ADAM_EPS = 1e-08
ADAM_WD = 0.01
ADAM_STEP = 10

HBM_SPEC = pl.BlockSpec(memory_space=pltpu.HBM)
MESH = pl.DeviceIdType.MESH

_NN = (((1,), (0,)), ((), ()))
_NT = (((1,), (1,)), ((), ()))
_TN = (((0,), (0,)), ((), ()))


def _dot(a, b, dn=_NN):
    return lax.dot_general(a, b, dn, preferred_element_type=F32)


def _pick(dim, pref):
    if dim <= pref:
        return dim
    t = (pref // 128) * 128
    while t >= 128:
        if dim % t == 0:
            return t
        t -= 128
    return dim


def _params(*sem):
    return pltpu.CompilerParams(dimension_semantics=sem)


def _peer(k):
    x, y, c = lax.axis_index("x"), lax.axis_index("y"), lax.axis_index("c")
    px = 1 - x if (k >> 2) & 1 else x
    py = 1 - y if (k >> 1) & 1 else y
    pc = 1 - c if k & 1 else c
    return (px, py, pc), 4 * px + 2 * py + pc


def _all_gather(x, name):
    def body(x_ref, out_ref, send_sems, recv_sems, local_sem):
        _, me = _peer(0)
        mine = pltpu.make_async_copy(x_ref, out_ref.at[me], local_sem)
        mine.start()
        sends = []
        for k in range(1, N_DEV):
            peer, _ = _peer(k)
            cp = pltpu.make_async_remote_copy(
                src_ref=x_ref, dst_ref=out_ref.at[me], send_sem=send_sems.at[k - 1],
                recv_sem=recv_sems.at[k - 1], device_id=peer, device_id_type=MESH)
            cp.start()
            sends.append(cp)
        for k in range(1, N_DEV):
            peer, pidx = _peer(k)
            pltpu.make_async_remote_copy(
                src_ref=x_ref, dst_ref=out_ref.at[pidx], send_sem=send_sems.at[k - 1],
                recv_sem=recv_sems.at[k - 1], device_id=peer, device_id_type=MESH).wait_recv()
        for cp in sends:
            cp.wait_send()
        mine.wait()

    return pl.pallas_call(
        body, name=name,
        out_shape=jax.ShapeDtypeStruct((N_DEV,) + x.shape, x.dtype),
        in_specs=[HBM_SPEC], out_specs=HBM_SPEC,
        scratch_shapes=[pltpu.SemaphoreType.DMA((N_DEV - 1,)), pltpu.SemaphoreType.DMA((N_DEV - 1,)),
                        pltpu.SemaphoreType.DMA],
    )(x)


def _all_to_all(x, name):
    def body(x_ref, out_ref, send_sems, recv_sems, local_sem):
        _, me = _peer(0)
        mine = pltpu.make_async_copy(x_ref.at[me], out_ref.at[me], local_sem)
        mine.start()
        sends = []
        for k in range(1, N_DEV):
            peer, pidx = _peer(k)
            cp = pltpu.make_async_remote_copy(
                src_ref=x_ref.at[pidx], dst_ref=out_ref.at[me], send_sem=send_sems.at[k - 1],
                recv_sem=recv_sems.at[k - 1], device_id=peer, device_id_type=MESH)
            cp.start()
            sends.append(cp)
        for k in range(1, N_DEV):
            peer, pidx = _peer(k)
            pltpu.make_async_remote_copy(
                src_ref=x_ref.at[pidx], dst_ref=out_ref.at[pidx], send_sem=send_sems.at[k - 1],
                recv_sem=recv_sems.at[k - 1], device_id=peer, device_id_type=MESH).wait_recv()
        for cp in sends:
            cp.wait_send()
        mine.wait()

    return pl.pallas_call(
        body, name=name,
        out_shape=jax.ShapeDtypeStruct(x.shape, x.dtype),
        in_specs=[HBM_SPEC], out_specs=HBM_SPEC,
        scratch_shapes=[pltpu.SemaphoreType.DMA((N_DEV - 1,)), pltpu.SemaphoreType.DMA((N_DEV - 1,)),
                        pltpu.SemaphoreType.DMA],
    )(x)


def _matmul(a, b, *, mode, out_dtype, name, tm=512, tn=1024, tk=1024, res=None):
    if mode == "nn":
        (M, K), (_, N) = a.shape, b.shape
    elif mode == "nt":
        (M, K), (N, _) = a.shape, b.shape
    else:
        (K, M), (_, N) = a.shape, b.shape
    tm, tn, tk = _pick(M, tm), _pick(N, tn), _pick(K, tk)
    nk = K // tk
    dn = {"nn": _NN, "nt": _NT, "tn": _TN}[mode]

    def body(*refs):
        a_ref, b_ref = refs[0], refs[1]
        r_ref = refs[2] if res is not None else None
        o_ref = refs[3] if res is not None else refs[2]

        def finish(r):
            if res is not None:
                r = r + r_ref[...].astype(F32)
            o_ref[...] = r.astype(out_dtype)

        part = _dot(a_ref[...].astype(BF), b_ref[...].astype(BF), dn)
        if nk == 1:
            finish(part)
            return
        acc = refs[-1]
        k = pl.program_id(2)

        @pl.when(k == 0)
        def _():
            acc[...] = part

        @pl.when(k > 0)
        def _():
            acc[...] += part

        @pl.when(k == nk - 1)
        def _():
            finish(acc[...])

    if mode == "tn":
        a_spec = pl.BlockSpec((tk, tm), lambda i, j, k: (k, i))
    else:
        a_spec = pl.BlockSpec((tm, tk), lambda i, j, k: (i, k))
    if mode == "nt":
        b_spec = pl.BlockSpec((tn, tk), lambda i, j, k: (j, k))
    else:
        b_spec = pl.BlockSpec((tk, tn), lambda i, j, k: (k, j))
    o_spec = pl.BlockSpec((tm, tn), lambda i, j, k: (i, j))
    in_specs = [a_spec, b_spec] + ([o_spec] if res is not None else [])
    args = (a, b) + ((res,) if res is not None else ())
    return pl.pallas_call(
        body, name=name,
        out_shape=jax.ShapeDtypeStruct((M, N), out_dtype),
        grid=(M // tm, N // tn, nk),
        in_specs=in_specs, out_specs=o_spec,
        scratch_shapes=[pltpu.VMEM((tm, tn), F32)] if nk > 1 else [],
        compiler_params=_params("parallel", "parallel", "arbitrary"),
    )(*args)


def _rms_fwd(x, g, name):
    S, D = x.shape
    tm = _pick(S, 512)

    def body(x_ref, g_ref, h_ref):
        xf = x_ref[...]
        r = lax.rsqrt(jnp.mean(xf * xf, axis=-1, keepdims=True) + RMS_EPS)
        h_ref[...] = (xf * r * g_ref[...]).astype(BF)

    return pl.pallas_call(
        body, name=name,
        out_shape=jax.ShapeDtypeStruct((S, D), BF),
        grid=(S // tm,),
        in_specs=[pl.BlockSpec((tm, D), lambda i: (i, 0)), pl.BlockSpec((1, D), lambda i: (0, 0))],
        out_specs=pl.BlockSpec((tm, D), lambda i: (i, 0)),
        compiler_params=_params("parallel"),
    )(x, g.reshape(1, D))


def _rms_bwd(x, g, dh, dres, name):
    S, D = x.shape
    tm = _pick(S, 512)

    def body(x_ref, g_ref, dh_ref, dres_ref, dx_ref, dg_ref):
        @pl.when(pl.program_id(0) == 0)
        def _():
            dg_ref[...] = jnp.zeros_like(dg_ref)

        xf = x_ref[...]
        r = lax.rsqrt(jnp.mean(xf * xf, axis=-1, keepdims=True) + RMS_EPS)
        xh = xf * r
        dy = dh_ref[...].astype(F32)
        dg_ref[...] += jnp.sum(dy * xh, axis=0, keepdims=True)
        dxh = dy * g_ref[...]
        dx = r * (dxh - xh * jnp.mean(dxh * xh, axis=-1, keepdims=True))
        dx_ref[...] = dres_ref[...] + dx

    row = pl.BlockSpec((tm, D), lambda i: (i, 0))
    vec = pl.BlockSpec((1, D), lambda i: (0, 0))
    return pl.pallas_call(
        body, name=name,
        out_shape=(jax.ShapeDtypeStruct((S, D), F32), jax.ShapeDtypeStruct((1, D), F32)),
        grid=(S // tm,),
        in_specs=[row, vec, row, row], out_specs=(row, vec),
        compiler_params=_params("arbitrary"),
    )(x, g.reshape(1, D), dh, dres)


def _loss_head(x, g, target):
    S, D = x.shape
    tm = _pick(S, 512)

    def body(x_ref, g_ref, t_ref, loss_ref, dx_ref, dg_ref):
        @pl.when(pl.program_id(0) == 0)
        def _():
            dg_ref[...] = jnp.zeros_like(dg_ref)
            loss_ref[...] = jnp.zeros_like(loss_ref)

        xf = x_ref[...]
        gg = g_ref[...]
        r = lax.rsqrt(jnp.mean(xf * xf, axis=-1, keepdims=True) + RMS_EPS)
        xh = xf * r
        err = xh * gg - t_ref[...]
        per_tok = jnp.mean(err * err, axis=-1, keepdims=True)
        loss_ref[...] += 0.5 * jnp.sum(per_tok, axis=0, keepdims=True)
        dy = err * (1.0 / D)
        dg_ref[...] += jnp.sum(dy * xh, axis=0, keepdims=True)
        dxh = dy * gg
        dx_ref[...] = r * (dxh - xh * jnp.mean(dxh * xh, axis=-1, keepdims=True))

    row = pl.BlockSpec((tm, D), lambda i: (i, 0))
    vec = pl.BlockSpec((1, D), lambda i: (0, 0))
    one = pl.BlockSpec((1, 1), lambda i: (0, 0))
    return pl.pallas_call(
        body, name="loss_head",
        out_shape=(jax.ShapeDtypeStruct((1, 1), F32), jax.ShapeDtypeStruct((S, D), F32),
                   jax.ShapeDtypeStruct((1, D), F32)),
        grid=(S // tm,),
        in_specs=[row, vec, row], out_specs=(one, row, vec),
        compiler_params=_params("arbitrary"),
    )(x, g.reshape(1, D), target)


def _slopes(g):
    return [2.0 ** (-8.0 * (HEADS_PER_GROUP * g + j + 1) / DSW_HEADS) for j in range(HEADS_PER_GROUP)]


def _band_masks(W):
    row = lax.broadcasted_iota(jnp.int32, (W, W), 0)
    col = lax.broadcasted_iota(jnp.int32, (W, W), 1)
    d_cur = row - col
    d_prev = d_cur + W
    return d_cur, d_prev, d_cur >= 0, d_cur <= 0


def _band_specs(W, nb):
    def cur(c):
        return pl.BlockSpec((None, W, OUT_A), lambda r, n: (r, n, c))

    def prev(c):
        return pl.BlockSpec((None, W, OUT_A), lambda r, n: (r, jnp.maximum(n - 1, 0), c))

    def nxt(c):
        return pl.BlockSpec((None, W, OUT_A), lambda r, n: (r, jnp.minimum(n + 1, nb - 1), c))

    return cur, prev, nxt


def _attn_a_fwd(qkv, cols, g):
    win, dil = DSW_GROUPS[g]
    W = win // dil
    d, L, _ = qkv[0].shape
    nb = L // W
    H, hd = HEADS_PER_GROUP, HEAD_DIM
    slopes = _slopes(g)

    def body(q_ref, kp_ref, kc_ref, vp_ref, vc_ref, o_ref, l_ref):
        n = pl.program_id(1)
        d_cur, d_prev, m_cur, m_prev = _band_masks(W)
        m_prev = jnp.logical_and(m_prev, n > 0)
        for j in range(H):
            ln = slice(j * hd, (j + 1) * hd)
            sl = slopes[j] * dil
            qh = q_ref[:, ln]
            s_c = _dot(qh, kc_ref[:, ln], _NT) * ATT_SCALE - sl * d_cur.astype(F32)
            s_p = _dot(qh, kp_ref[:, ln], _NT) * ATT_SCALE - sl * d_prev.astype(F32)
            s_c = jnp.where(m_cur, s_c, NEG)
            s_p = jnp.where(m_prev, s_p, NEG)
            m = jnp.maximum(jnp.max(s_c, axis=1, keepdims=True), jnp.max(s_p, axis=1, keepdims=True))
            p_c = jnp.exp(s_c - m)
            p_p = jnp.exp(s_p - m)
            den = jnp.sum(p_c, axis=1, keepdims=True) + jnp.sum(p_p, axis=1, keepdims=True)
            o = _dot(p_c.astype(BF), vc_ref[:, ln]) + _dot(p_p.astype(BF), vp_ref[:, ln])
            o_ref[:, ln] = o / den
            l_ref[:, ln] = jnp.broadcast_to(m + jnp.log(den), (W, hd))

    cur, prev, _ = _band_specs(W, nb)
    out = jax.ShapeDtypeStruct((d, L, OUT_A), F32)
    return pl.pallas_call(
        body, name=f"attn_a_fwd_g{g}",
        out_shape=(out, out),
        grid=(d, nb),
        in_specs=[cur(cols[0]), prev(cols[1]), cur(cols[1]), prev(cols[2]), cur(cols[2])],
        out_specs=(cur(0), cur(0)),
        compiler_params=_params("parallel", "parallel"),
    )(qkv[0], qkv[1], qkv[1], qkv[2], qkv[2])


def _attn_a_bwd(qkv, cols, do, lse, dsum, g):
    win, dil = DSW_GROUPS[g]
    W = win // dil
    d, L, _ = qkv[0].shape
    nb = L // W
    H, hd = HEADS_PER_GROUP, HEAD_DIM
    slopes = _slopes(g)

    def body(q_ref, qn_ref, kp_ref, kc_ref, vp_ref, vc_ref, do_ref, don_ref, l_ref, ln_ref,
             ds_ref, dsn_ref, dq_ref, dk_ref, dv_ref):
        n = pl.program_id(1)
        d_cur, d_prev, m_cur, m_prev = _band_masks(W)
        m_p = jnp.logical_and(m_prev, n > 0)
        m_n = jnp.logical_and(m_prev, n < nb - 1)
        for j in range(H):
            ln = slice(j * hd, (j + 1) * hd)
            c0 = slice(j * hd, j * hd + 1)
            sl = slopes[j] * dil
            b_cur = sl * d_cur.astype(F32)
            b_prev = sl * d_prev.astype(F32)
            qh, qn = q_ref[:, ln], qn_ref[:, ln]
            kc, kp, vc, vp = kc_ref[:, ln], kp_ref[:, ln], vc_ref[:, ln], vp_ref[:, ln]
            dob, donb = do_ref[:, ln].astype(BF), don_ref[:, ln].astype(BF)
            lse_c, lse_n = l_ref[:, c0], ln_ref[:, c0]
            dsum_c, dsum_n = ds_ref[:, c0], dsn_ref[:, c0]
            p_cc = jnp.exp(jnp.where(m_cur, _dot(qh, kc, _NT) * ATT_SCALE - b_cur, NEG) - lse_c)
            p_cp = jnp.exp(jnp.where(m_p, _dot(qh, kp, _NT) * ATT_SCALE - b_prev, NEG) - lse_c)
            p_nc = jnp.exp(jnp.where(m_n, _dot(qn, kc, _NT) * ATT_SCALE - b_prev, NEG) - lse_n)
            ds_cc = (p_cc * (_dot(dob, vc, _NT) - dsum_c) * ATT_SCALE).astype(BF)
            ds_cp = (p_cp * (_dot(dob, vp, _NT) - dsum_c) * ATT_SCALE).astype(BF)
            ds_nc = (p_nc * (_dot(donb, vc, _NT) - dsum_n) * ATT_SCALE).astype(BF)
            dq_ref[:, ln] = (_dot(ds_cc, kc) + _dot(ds_cp, kp)).astype(BF)
            dk_ref[:, ln] = (_dot(ds_cc, qh, _TN) + _dot(ds_nc, qn, _TN)).astype(BF)
            dv_ref[:, ln] = (_dot(p_cc.astype(BF), dob, _TN) + _dot(p_nc.astype(BF), donb, _TN)).astype(BF)

    cur, prev, nxt = _band_specs(W, nb)
    out = jax.ShapeDtypeStruct((d, L, OUT_A), BF)
    cq, ck, cv = cols
    return pl.pallas_call(
        body, name=f"attn_a_bwd_g{g}",
        out_shape=(out, out, out),
        grid=(d, nb),
        in_specs=[cur(cq), nxt(cq), prev(ck), cur(ck), prev(cv), cur(cv),
                  cur(0), nxt(0), cur(0), nxt(0), cur(0), nxt(0)],
        out_specs=(cur(0), cur(0), cur(0)),
        compiler_params=_params("parallel", "parallel"),
    )(qkv[0], qkv[0], qkv[1], qkv[1], qkv[2], qkv[2], do, do, lse, lse, dsum, dsum)


SB_PAIR = 2
SB_QROWS = 2 * SB_BLOCK


def _softplus_parts(z):
    e = jnp.exp(-jnp.abs(z))
    log1p_e = jnp.where(e < 1e-3, e * (1.0 - e * (0.5 - e * (1.0 / 3.0))), jnp.log(1.0 + e))
    return e, jnp.maximum(z, 0.0) + log1p_e


def _split_dot(x, t):
    hi = x.astype(BF)
    lo = (x - hi.astype(F32)).astype(BF)
    return _dot(hi, t) + _dot(lo, t)


def _sb_block(qh, kk, causal, r_run, tri_incl):
    z = _dot(qh, kk, _NT) * ATT_SCALE
    e, sp = _softplus_parts(z)
    ls = jnp.where(causal, -sp, 0.0)
    cin = _split_dot(ls, tri_incl)
    a = jnp.where(causal, jnp.exp(z + cin + r_run), 0.0)
    return z, e, cin, a


def _sb_specs(S):
    Q, hd = SB_QROWS, HEAD_DIM
    lanes = SB_PAIR * hd
    qc = (3 * W_A) // lanes
    kc = (3 * W_A + W_B) // lanes
    vc = (3 * W_A + 2 * W_B) // lanes
    q_spec = pl.BlockSpec((Q, lanes), lambda p, i: (i, qc + p))
    k_spec = pl.BlockSpec((S, lanes), lambda p, i: (0, kc + p))
    v_spec = pl.BlockSpec((S, lanes), lambda p, i: (0, vc + p))
    o_spec = pl.BlockSpec((Q, lanes), lambda p, i: (i, p))
    full = pl.BlockSpec((S, lanes), lambda p, i: (0, p))
    return q_spec, k_spec, v_spec, o_spec, full


def _sb_iotas(i):
    B, Q = SB_BLOCK, SB_QROWS
    row = lax.broadcasted_iota(jnp.int32, (Q, B), 0) + i * Q
    col = lax.broadcasted_iota(jnp.int32, (Q, B), 1)
    tr = lax.broadcasted_iota(jnp.int32, (B, B), 0)
    tc = lax.broadcasted_iota(jnp.int32, (B, B), 1)
    return row, col, tr, tc


def _sb_fwd(proj):
    S = proj.shape[0]
    B, Q, hd = SB_BLOCK, SB_QROWS, HEAD_DIM
    nq = S // Q
    q_spec, k_spec, v_spec, o_spec, _ = _sb_specs(S)

    def body(q_ref, k_ref, v_ref, o_ref):
        i = pl.program_id(1)
        row, col, tr, tc = _sb_iotas(i)
        tri_incl = (tr >= tc).astype(BF)
        qs = [q_ref[:, h * hd:(h + 1) * hd] for h in range(SB_PAIR)]

        def cond(c):
            return jnp.logical_and(c[0] >= 0, c[-1] > SB_EXIT)

        def step(c):
            kb = c[0]
            off = pl.multiple_of(kb * B, B)
            causal = col + kb * B < row
            runs, accs = [], []
            for h in range(SB_PAIR):
                r_run, acc = c[1 + h], c[1 + SB_PAIR + h]
                kk = k_ref[pl.ds(off, B), h * hd:(h + 1) * hd]
                vv = v_ref[pl.ds(off, B), h * hd:(h + 1) * hd]
                _, _, cin, a = _sb_block(qs[h], kk, causal, r_run, tri_incl)
                accs.append(acc + _dot(a.astype(BF), vv))
                runs.append(r_run + cin[:, 0:1])
            rmax = functools.reduce(jnp.maximum, [jnp.max(r) for r in runs])
            return (kb - 1, *runs, *accs, rmax)

        init = (2 * i + 1, *[jnp.zeros((Q, 1), F32)] * SB_PAIR, *[jnp.zeros((Q, hd), F32)] * SB_PAIR,
                jnp.float32(0.0))
        fin = lax.while_loop(cond, step, init)
        for h in range(SB_PAIR):
            o_ref[:, h * hd:(h + 1) * hd] = fin[1 + SB_PAIR + h]

    return pl.pallas_call(
        body, name="sb_fwd",
        out_shape=jax.ShapeDtypeStruct((S, W_B), F32),
        grid=(SB_HEADS // SB_PAIR, nq),
        in_specs=[q_spec, k_spec, v_spec], out_specs=o_spec,
        compiler_params=_params("parallel", "parallel"),
    )(proj, proj, proj)


def _sb_bwd(proj, do, o):
    S = proj.shape[0]
    B, Q, hd = SB_BLOCK, SB_QROWS, HEAD_DIM
    nq = S // Q
    q_spec, k_spec, v_spec, o_spec, full = _sb_specs(S)
    P = SB_PAIR

    def body(q_ref, k_ref, v_ref, do_ref, o_ref, dq_ref, dk_ref, dv_ref):
        i = pl.program_id(1)

        @pl.when(i == 0)
        def _():
            dk_ref[...] = jnp.zeros_like(dk_ref)
            dv_ref[...] = jnp.zeros_like(dv_ref)

        row, col, tr, tc = _sb_iotas(i)
        tri_incl = (tr >= tc).astype(BF)
        tri_strict = (tr > tc).astype(BF)
        qs = [q_ref[:, h * hd:(h + 1) * hd] for h in range(P)]
        dobs = [do_ref[:, h * hd:(h + 1) * hd].astype(BF) for h in range(P)]
        dsums = [jnp.sum(dobs[h].astype(F32) * o_ref[:, h * hd:(h + 1) * hd], axis=1, keepdims=True)
                 for h in range(P)]

        def cond(c):
            return jnp.logical_and(c[0] >= 0, c[-1] > SB_EXIT)

        def step(c):
            kb = c[0]
            off = pl.multiple_of(kb * B, B)
            causal = col + kb * B < row
            runs, gruns, dqs = [], [], []
            for h in range(P):
                r_run, g_run, dq = c[1 + h], c[1 + P + h], c[1 + 2 * P + h]
                ln = slice(h * hd, (h + 1) * hd)
                kk = k_ref[pl.ds(off, B), ln]
                vv = v_ref[pl.ds(off, B), ln]
                z, e, cin, a = _sb_block(qs[h], kk, causal, r_run, tri_incl)
                a16 = a.astype(BF)
                gmat = a16.astype(F32) * _dot(dobs[h], vv, _NT)
                later = _split_dot(gmat, tri_strict)
                pfx = dsums[h] - g_run - later
                sig = jnp.where(z >= 0, 1.0, e) / (1.0 + e)
                dz = (jnp.where(causal, gmat - sig * pfx, 0.0) * ATT_SCALE).astype(BF)
                dqs.append(dq + _dot(dz, kk))
                dk_ref[pl.ds(off, B), ln] += _dot(dz, qs[h], _TN)
                dv_ref[pl.ds(off, B), ln] += _dot(a16, dobs[h], _TN)
                gruns.append(g_run + jnp.sum(gmat, axis=1, keepdims=True))
                runs.append(r_run + cin[:, 0:1])
            rmax = functools.reduce(jnp.maximum, [jnp.max(r) for r in runs])
            return (kb - 1, *runs, *gruns, *dqs, rmax)

        init = (2 * i + 1, *[jnp.zeros((Q, 1), F32)] * (2 * P), *[jnp.zeros((Q, hd), F32)] * P, jnp.float32(0.0))
        fin = lax.while_loop(cond, step, init)
        for h in range(P):
            dq_ref[:, h * hd:(h + 1) * hd] = fin[1 + 2 * P + h]

    out = jax.ShapeDtypeStruct((S, W_B), F32)
    return pl.pallas_call(
        body, name="sb_bwd",
        out_shape=(out, out, out),
        grid=(SB_HEADS // P, nq),
        in_specs=[q_spec, k_spec, v_spec, o_spec, o_spec], out_specs=(o_spec, full, full),
        compiler_params=_params("arbitrary", "arbitrary"),
    )(proj, proj, proj, do, o)


def _merge_fwd(o_g, l_g, o_b, proj, b_gate, w_br):
    S = o_b.shape[0]
    D = D_MODEL
    tm = _pick(S, 256)
    gcol = GATE_OFF // D

    def body(o0, o1, o2, l0, l1, l2, ob_ref, ga_ref, gb_ref, bg_ref, w_ref, mg_ref, oa_ref, lse_ref):
        la, lb, lc = l0[...], l1[...], l2[...]
        mx = jnp.maximum(jnp.maximum(la, lb), lc)
        ea, eb, ec = jnp.exp(la - mx), jnp.exp(lb - mx), jnp.exp(lc - mx)
        den = ea + eb + ec
        oa = (ea * o0[...] + eb * o1[...] + ec * o2[...]) / den
        oa_ref[...] = oa
        lse_ref[...] = mx + jnp.log(den)
        ya = _dot(oa.astype(BF), w_ref[0:OUT_A, :])
        yb = _dot(ob_ref[...].astype(BF), w_ref[OUT_A:OUT_A + W_B, :])
        bg = bg_ref[...]
        g_a = jax.nn.sigmoid(ga_ref[...].astype(F32) + bg[:, 0:D])
        g_b = jax.nn.sigmoid(gb_ref[...].astype(F32) + bg[:, D:2 * D])
        mg_ref[...] = (g_a * ya + g_b * yb).astype(BF)

    nar = pl.BlockSpec((tm, OUT_A), lambda i: (i, 0))
    wide = pl.BlockSpec((tm, D), lambda i: (i, 0))
    return pl.pallas_call(
        body, name="merge_fwd",
        out_shape=(jax.ShapeDtypeStruct((S, D), BF), jax.ShapeDtypeStruct((S, OUT_A), F32),
                   jax.ShapeDtypeStruct((S, OUT_A), F32)),
        grid=(S // tm,),
        in_specs=[nar] * 7 + [pl.BlockSpec((tm, D), lambda i: (i, gcol)),
                              pl.BlockSpec((tm, D), lambda i: (i, gcol + 1)),
                              pl.BlockSpec((1, 2 * D), lambda i: (0, 0)),
                              pl.BlockSpec((OUT_A + W_B, D), lambda i: (0, 0))],
        out_specs=(wide, nar, nar),
        compiler_params=_params("parallel"),
    )(*o_g, *l_g, o_b, proj, proj, b_gate.reshape(1, 2 * D), w_br)


def _merge_bwd(dm, oa, ob, proj, b_gate, w_br):
    S = ob.shape[0]
    D = D_MODEL
    tm = _pick(S, 256)
    gcol = GATE_OFF // D

    def body(dm_ref, oa_ref, ob_ref, ga_ref, gb_ref, bg_ref, w_ref,
             dya_ref, dyb_ref, doa_ref, dob_ref, ds_ref, dg_ref, dbg_ref):
        @pl.when(pl.program_id(0) == 0)
        def _():
            dbg_ref[...] = jnp.zeros_like(dbg_ref)

        wa = w_ref[0:OUT_A, :]
        wb = w_ref[OUT_A:OUT_A + W_B, :]
        oa = oa_ref[...]
        ya = _dot(oa.astype(BF), wa)
        yb = _dot(ob_ref[...].astype(BF), wb)
        bg = bg_ref[...]
        g_a = jax.nn.sigmoid(ga_ref[...].astype(F32) + bg[:, 0:D])
        g_b = jax.nn.sigmoid(gb_ref[...].astype(F32) + bg[:, D:2 * D])
        dm = dm_ref[...].astype(F32)
        dga = dm * ya * g_a * (1.0 - g_a)
        dgb = dm * yb * g_b * (1.0 - g_b)
        dg_ref[:, 0:D] = dga.astype(BF)
        dg_ref[:, D:2 * D] = dgb.astype(BF)
        dbg_ref[:, 0:D] += jnp.sum(dga, axis=0, keepdims=True)
        dbg_ref[:, D:2 * D] += jnp.sum(dgb, axis=0, keepdims=True)
        dya = (dm * g_a).astype(BF)
        dyb = (dm * g_b).astype(BF)
        dya_ref[...] = dya
        dyb_ref[...] = dyb
        doa = _dot(dya, wa, _NT)
        doa_ref[...] = doa
        dob_ref[...] = _dot(dyb, wb, _NT)
        r = lax.broadcasted_iota(jnp.int32, (OUT_A, OUT_A), 0) // HEAD_DIM
        c = lax.broadcasted_iota(jnp.int32, (OUT_A, OUT_A), 1) // HEAD_DIM
        ds_ref[...] = _split_dot(doa * oa, (r == c).astype(BF))

    nar = pl.BlockSpec((tm, OUT_A), lambda i: (i, 0))
    wide = pl.BlockSpec((tm, D), lambda i: (i, 0))
    wide2 = pl.BlockSpec((tm, 2 * D), lambda i: (i, 0))
    vec2 = pl.BlockSpec((1, 2 * D), lambda i: (0, 0))
    return pl.pallas_call(
        body, name="merge_bwd",
        out_shape=(jax.ShapeDtypeStruct((S, D), BF), jax.ShapeDtypeStruct((S, D), BF),
                   jax.ShapeDtypeStruct((S, OUT_A), F32), jax.ShapeDtypeStruct((S, W_B), F32),
                   jax.ShapeDtypeStruct((S, OUT_A), F32), jax.ShapeDtypeStruct((S, 2 * D), BF),
                   jax.ShapeDtypeStruct((1, 2 * D), F32)),
        grid=(S // tm,),
        in_specs=[wide, nar, nar, pl.BlockSpec((tm, D), lambda i: (i, gcol)),
                  pl.BlockSpec((tm, D), lambda i: (i, gcol + 1)), vec2,
                  pl.BlockSpec((OUT_A + W_B, D), lambda i: (0, 0))],
        out_specs=(wide, wide, nar, nar, nar, wide2, vec2),
        compiler_params=_params("arbitrary"),
    )(dm, oa, ob, proj, proj, b_gate.reshape(1, 2 * D), w_br)


_SQRT_HALF = 0.7071067811865476
_INV_SQRT_2PI = 0.3989422804014327


def _gelu_parts(a):
    cdf = 0.5 * (1.0 + lax.erf(a * _SQRT_HALF))
    pdf = _INV_SQRT_2PI * jnp.exp(-0.5 * a * a)
    return cdf, pdf


def _shift_down(a, halo, k):
    rows = lax.broadcasted_iota(jnp.int32, a.shape, 0)
    out = pltpu.roll(a, k, 0)
    for r in range(k):
        out = jnp.where(rows == r, halo[8 - k + r:8 - k + r + 1, :], out)
    return out


def _shift_up(a, halo, k):
    n = a.shape[0]
    rows = lax.broadcasted_iota(jnp.int32, a.shape, 0)
    out = pltpu.roll(a, n - k, 0)
    for r in range(k):
        out = jnp.where(rows == n - k + r, halo[r:r + 1, :], out)
    return out


def _conv_in(a_ref, h_ref, first):
    a = a_ref[...].astype(F32)
    halo = jnp.where(first, 0.0, h_ref[...].astype(F32))
    return a, _shift_down(a, halo, 1), _shift_down(a, halo, 2)


def _ffn_specs(S, tm):
    F = D_FF
    t8 = tm // 8
    a_spec = pl.BlockSpec((tm, F), lambda i: (i, 0))
    v_spec = pl.BlockSpec((tm, F), lambda i: (i, 1))
    halo_prev = pl.BlockSpec((8, F), lambda i: (jnp.maximum(i * t8 - 1, 0), 0))
    return a_spec, v_spec, halo_prev


def _ffn_act_fwd(up, conv_w, conv_b):
    S = up.shape[0]
    F = D_FF
    tm = _pick(S, 256)
    a_spec, v_spec, halo_prev = _ffn_specs(S, tm)

    def body(a_ref, h_ref, v_ref, w_ref, b_ref, act_ref):
        a0, a1, a2 = _conv_in(a_ref, h_ref, pl.program_id(0) == 0)
        w = w_ref[...]
        ac = b_ref[...] + w[0:1, :] * a2 + w[1:2, :] * a1 + w[2:3, :] * a0
        cdf, _ = _gelu_parts(ac)
        act_ref[...] = (ac * cdf * v_ref[...].astype(F32)).astype(BF)

    return pl.pallas_call(
        body, name="ffn_act_fwd",
        out_shape=jax.ShapeDtypeStruct((S, F), BF),
        grid=(S // tm,),
        in_specs=[a_spec, halo_prev, v_spec, pl.BlockSpec((3, F), lambda i: (0, 0)),
                  pl.BlockSpec((1, F), lambda i: (0, 0))],
        out_specs=a_spec,
        compiler_params=_params("parallel"),
    )(up, up, up, conv_w, conv_b.reshape(1, F))


def _ffn_act_bwd1(dact, up, conv_w, conv_b):
    S = up.shape[0]
    F = D_FF
    tm = _pick(S, 256)
    a_spec, v_spec, halo_prev = _ffn_specs(S, tm)

    def body(da_ref, a_ref, h_ref, v_ref, w_ref, b_ref, dac_ref, dv_ref, dw_ref, db_ref):
        @pl.when(pl.program_id(0) == 0)
        def _():
            dw_ref[...] = jnp.zeros_like(dw_ref)
            db_ref[...] = jnp.zeros_like(db_ref)

        a0, a1, a2 = _conv_in(a_ref, h_ref, pl.program_id(0) == 0)
        w = w_ref[...]
        ac = b_ref[...] + w[0:1, :] * a2 + w[1:2, :] * a1 + w[2:3, :] * a0
        cdf, pdf = _gelu_parts(ac)
        dact = da_ref[...].astype(F32)
        dv_ref[...] = (dact * ac * cdf).astype(BF)
        dac = dact * v_ref[...].astype(F32) * (cdf + ac * pdf)
        dac_ref[...] = dac.astype(BF)
        db_ref[...] += jnp.sum(dac, axis=0, keepdims=True)
        dw_ref[0:1, :] += jnp.sum(dac * a2, axis=0, keepdims=True)
        dw_ref[1:2, :] += jnp.sum(dac * a1, axis=0, keepdims=True)
        dw_ref[2:3, :] += jnp.sum(dac * a0, axis=0, keepdims=True)

    w_spec = pl.BlockSpec((3, F), lambda i: (0, 0))
    b_spec = pl.BlockSpec((1, F), lambda i: (0, 0))
    return pl.pallas_call(
        body, name="ffn_act_bwd1",
        out_shape=(jax.ShapeDtypeStruct((S, F), BF), jax.ShapeDtypeStruct((S, F), BF),
                   jax.ShapeDtypeStruct((3, F), F32), jax.ShapeDtypeStruct((1, F), F32)),
        grid=(S // tm,),
        in_specs=[a_spec, a_spec, halo_prev, v_spec, w_spec, b_spec],
        out_specs=(a_spec, a_spec, w_spec, b_spec),
        compiler_params=_params("arbitrary"),
    )(dact, up, up, up, conv_w, conv_b.reshape(1, F))


def _ffn_act_bwd2(dac, dv, conv_w):
    S = dac.shape[0]
    F = D_FF
    tm = _pick(S, 256)
    t8 = tm // 8
    nt = S // tm
    cur = pl.BlockSpec((tm, F), lambda i: (i, 0))
    halo_next = pl.BlockSpec((8, F), lambda i: (jnp.minimum((i + 1) * t8, S // 8 - 1), 0))

    def body(d_ref, h_ref, dv_ref, w_ref, o_ref):
        d0 = d_ref[...].astype(F32)
        halo = jnp.where(pl.program_id(0) == nt - 1, 0.0, h_ref[...].astype(F32))
        d1 = _shift_up(d0, halo, 1)
        d2 = _shift_up(d0, halo, 2)
        w = w_ref[...]
        o_ref[:, 0:F] = (w[2:3, :] * d0 + w[1:2, :] * d1 + w[0:1, :] * d2).astype(BF)
        o_ref[:, F:2 * F] = dv_ref[...]

    return pl.pallas_call(
        body, name="ffn_act_bwd2",
        out_shape=jax.ShapeDtypeStruct((S, 2 * F), BF),
        grid=(nt,),
        in_specs=[cur, halo_next, cur, pl.BlockSpec((3, F), lambda i: (0, 0))],
        out_specs=pl.BlockSpec((tm, 2 * F), lambda i: (i, 0)),
        compiler_params=_params("parallel"),
    )(dac, dac, dv, conv_w)


def _adamw(parts, w, m, v, name):
    R, C = w.shape
    tr = R
    for cand in (512, 256, 128, 64, 32, 16):
        if R % cand == 0 and cand * C * 4 <= (1 << 21):
            tr = cand
            break
    c1 = 1.0 / (1.0 - ADAM_B1 ** ADAM_STEP)
    c2 = 1.0 / (1.0 - ADAM_B2 ** ADAM_STEP)

    def body(p_ref, w_ref, m_ref, v_ref, g_ref, d_ref, nm_ref, nv_ref):
        g = p_ref[0].astype(F32)
        for j in range(1, N_DEV):
            g = g + p_ref[j].astype(F32)
        nm = ADAM_B1 * m_ref[...] + (1.0 - ADAM_B1) * g
        nv = ADAM_B2 * v_ref[...] + (1.0 - ADAM_B2) * (g * g)
        g_ref[...] = g
        nm_ref[...] = nm
        nv_ref[...] = nv
        d_ref[...] = -ADAM_LR * ((nm * c1) / (jnp.sqrt(nv * c2) + ADAM_EPS) + ADAM_WD * w_ref[...])

    blk = pl.BlockSpec((tr, C), lambda i: (i, 0))
    out = jax.ShapeDtypeStruct((R, C), F32)
    return pl.pallas_call(
        body, name=name,
        out_shape=(out, out, out, out),
        grid=(R // tr,),
        in_specs=[pl.BlockSpec((N_DEV, tr, C), lambda i: (0, i, 0)), blk, blk, blk],
        out_specs=(blk, blk, blk, blk),
        compiler_params=_params("parallel"),
    )(parts, w, m, v)


def _dil(t, dil):
    S, C = t.shape
    if dil == 1:
        return t.reshape(1, S, C)
    return t.reshape(S // dil, dil, C).transpose(1, 0, 2)


def _undil(t):
    d, L, C = t.shape
    if d == 1:
        return t.reshape(L, C)
    return t.transpose(1, 0, 2).reshape(L * d, C)


def _group_qkv(proj, g):
    dil = DSW_GROUPS[g][1]
    if dil == 1:
        p3 = _dil(proj, 1)
        return (p3, p3, p3), (g, W_A // OUT_A + g, 2 * W_A // OUT_A + g)
    c0 = g * OUT_A
    return tuple(_dil(proj[:, o + c0:o + c0 + OUT_A], dil) for o in (0, W_A, 2 * W_A)), (0, 0, 0)


def _layer_fwd(x, p, l):
    h = _rms_fwd(x, p["norm1"][l], f"rms1_fwd_{l}")
    proj = _matmul(h, p["w_in"][l], mode="nn", out_dtype=BF, name=f"proj_{l}", tm=512, tn=1024, tk=1024)
    o_g, l_g, qkv_g = [], [], []
    for g in range(N_GROUPS):
        qkv, cols = _group_qkv(proj, g)
        og, lg = _attn_a_fwd(qkv, cols, g)
        o_g.append(_undil(og))
        l_g.append(_undil(lg))
        qkv_g.append((qkv, cols))
    ob = _sb_fwd(proj)
    merged, oa, lse = _merge_fwd(o_g, l_g, ob, proj, p["b_gate"][l], p["w_br"][l])
    x1 = _matmul(merged, p["w_o"][l], mode="nn", out_dtype=F32, name=f"wo_{l}", tm=512, tn=1024, tk=1024, res=x)
    h2 = _rms_fwd(x1, p["norm2"][l], f"rms2_fwd_{l}")
    up = _matmul(h2, p["w_up"][l], mode="nn", out_dtype=BF, name=f"up_{l}", tm=512, tn=1408, tk=1024)
    act = _ffn_act_fwd(up, p["conv_w"][l], p["conv_b"][l])
    x2 = _matmul(act, p["w_down"][l], mode="nn", out_dtype=F32, name=f"down_{l}", tm=512, tn=1024, tk=D_FF, res=x1)
    saved = dict(x=x, h=h, proj=proj, qkv_g=qkv_g, oa=oa, ob=ob, lse=lse, merged=merged, x1=x1, h2=h2, up=up, act=act)
    return x2, saved


def _layer_bwd(dx2, sv, p, l):
    gr = {}
    dact = _matmul(dx2, p["w_down"][l], mode="nt", out_dtype=BF, name=f"d_act_{l}", tm=512, tn=1408, tk=1024)
    gr["w_down"] = _matmul(sv["act"], dx2, mode="tn", out_dtype=BF, name=f"dw_down_{l}", tm=1408, tn=1024, tk=512)
    dac, dv, gr["conv_w"], dcb = _ffn_act_bwd1(dact, sv["up"], p["conv_w"][l], p["conv_b"][l])
    gr["conv_b"] = dcb[0]
    dup = _ffn_act_bwd2(dac, dv, p["conv_w"][l])
    dh2 = _matmul(dup, p["w_up"][l], mode="nt", out_dtype=BF, name=f"d_h2_{l}", tm=512, tn=1024, tk=1408)
    gr["w_up"] = _matmul(sv["h2"], dup, mode="tn", out_dtype=BF, name=f"dw_up_{l}", tm=1024, tn=1408, tk=512)
    dx1, dn2 = _rms_bwd(sv["x1"], p["norm2"][l], dh2, dx2, f"rms2_bwd_{l}")
    gr["norm2"] = dn2[0]
    dm = _matmul(dx1, p["w_o"][l], mode="nt", out_dtype=BF, name=f"d_merged_{l}", tm=512, tn=1024, tk=1024)
    gr["w_o"] = _matmul(sv["merged"], dx1, mode="tn", out_dtype=BF, name=f"dw_o_{l}", tm=1024, tn=1024, tk=512)
    dya, dyb, doa, dob, dsum, dgate, dbg = _merge_bwd(dm, sv["oa"], sv["ob"], sv["proj"], p["b_gate"][l], p["w_br"][l])
    gr["b_gate"] = dbg[0]
    dwa = _matmul(sv["oa"], dya, mode="tn", out_dtype=BF, name=f"dw_bra_{l}", tm=256, tn=1024, tk=512)
    dwb = _matmul(sv["ob"], dyb, mode="tn", out_dtype=BF, name=f"dw_brb_{l}", tm=256, tn=1024, tk=512)
    gr["w_br"] = jnp.concatenate([dwa, dwb], axis=0)
    proj = sv["proj"]
    dq_a, dk_a, dv_a = [], [], []
    for g, (_, dil) in enumerate(DSW_GROUPS):
        qkv, cols = sv["qkv_g"][g]
        dqg, dkg, dvg = _attn_a_bwd(qkv, cols, _dil(doa, dil), _dil(sv["lse"], dil), _dil(dsum, dil), g)
        dq_a.append(_undil(dqg))
        dk_a.append(_undil(dkg))
        dv_a.append(_undil(dvg))
    dqb, dkb, dvb = _sb_bwd(proj, dob, sv["ob"])
    dproj = jnp.concatenate(dq_a + dk_a + dv_a + [dqb.astype(BF), dkb.astype(BF), dvb.astype(BF), dgate], axis=1)
    dh = _matmul(dproj, p["w_in"][l], mode="nt", out_dtype=BF, name=f"d_h_{l}", tm=512, tn=1024, tk=1280)
    gr["w_in"] = _matmul(sv["h"], dproj, mode="tn", out_dtype=BF, name=f"dw_in_{l}", tm=1024, tn=1280, tk=512)
    dx, dn1 = _rms_bwd(sv["x"], p["norm1"][l], dh, dx1, f"rms1_bwd_{l}")
    gr["norm1"] = dn1[0]
    return dx, gr


_COL_SHARDED = ("w_in", "w_br", "w_up")
_ROW_SHARDED = ("w_o", "w_down")


def _gather_cols(shard, name):
    g = _all_gather(shard, name)
    _, L, K, n = g.shape
    return g.transpose(1, 2, 0, 3).reshape(L, K, N_DEV * n)


def _gather_rows(shard, name):
    g = _all_gather(shard, name)
    _, L, k, N = g.shape
    return g.transpose(1, 0, 2, 3).reshape(L, N_DEV * k, N)


def _scatter_cols(full, name):
    L, K, N = full.shape
    n = N // N_DEV
    blocks = full.reshape(L, K, N_DEV, n).transpose(2, 0, 1, 3)
    return _all_to_all(blocks, name).reshape(N_DEV, L * K, n)


def _scatter_rows(full, name):
    L, K, N = full.shape
    k = K // N_DEV
    blocks = full.reshape(L, N_DEV, k, N).transpose(1, 0, 2, 3)
    return _all_to_all(blocks, name).reshape(N_DEV, L * k, N)


def kernel(x, norm1, w_in, b_gate, w_br, w_o, norm2, w_up, conv_w, conv_b, w_down, norm_f, loss_target, m_norm1, m_w_in, m_b_gate, m_w_br, m_w_o, m_norm2, m_w_up, m_conv_w, m_conv_b, m_w_down, m_norm_f, v_norm1, v_w_in, v_b_gate, v_w_br, v_w_o, v_norm2, v_w_up, v_conv_w, v_conv_b, v_w_down, v_norm_f):
    depth = norm1.shape[0]
    me = 4 * lax.axis_index("x") + 2 * lax.axis_index("y") + lax.axis_index("c")
    shards = dict(w_in=w_in, w_br=w_br, w_o=w_o, w_up=w_up, w_down=w_down)
    moments_m = dict(norm1=m_norm1, w_in=m_w_in, b_gate=m_b_gate, w_br=m_w_br, w_o=m_w_o, norm2=m_norm2,
                     w_up=m_w_up, conv_w=m_conv_w, conv_b=m_conv_b, w_down=m_w_down, norm_f=m_norm_f)
    moments_v = dict(norm1=v_norm1, w_in=v_w_in, b_gate=v_b_gate, w_br=v_w_br, w_o=v_w_o, norm2=v_norm2,
                     w_up=v_w_up, conv_w=v_conv_w, conv_b=v_conv_b, w_down=v_w_down, norm_f=v_norm_f)

    p = dict(norm1=norm1, b_gate=b_gate, norm2=norm2, conv_b=conv_b)
    for n in _COL_SHARDED:
        p[n] = _gather_cols(shards[n].astype(BF), f"gather_{n}")
    for n in _ROW_SHARDED:
        p[n] = _gather_rows(shards[n].astype(BF), f"gather_{n}")
    cw = _all_gather(conv_w, "gather_conv_w")
    p["conv_w"] = cw.transpose(1, 2, 0, 3).reshape(depth, 3, D_FF)

    xs = x[0]
    saved = []
    for l in range(depth):
        xs, sv = _layer_fwd(xs, p, l)
        saved.append(sv)
    loss_part, dx, dnf = _loss_head(xs, norm_f, loss_target[0])
    loss = lax.psum(loss_part[0, 0], ("x", "y", "c"))

    grads = [None] * depth
    for l in reversed(range(depth)):
        dx, grads[l] = _layer_bwd(dx, saved[l], p, l)
    grad_x = dx[None]

    out_g, out_d, out_m, out_v = {}, {}, {}, {}
    for n in _COL_SHARDED + _ROW_SHARDED:
        full = jnp.stack([grads[l][n] for l in range(depth)])
        parts = _scatter_cols(full, f"scatter_{n}") if n in _COL_SHARDED else _scatter_rows(full, f"scatter_{n}")
        shp = shards[n].shape
        flat = (shp[0] * shp[1], shp[2])
        res = _adamw(parts, shards[n].reshape(flat), moments_m[n].reshape(flat), moments_v[n].reshape(flat),
                     f"adamw_{n}")
        out_g[n], out_d[n], out_m[n], out_v[n] = [r.reshape(shp) for r in res]

    small = ("norm1", "b_gate", "norm2", "conv_b")
    vecs = [jnp.stack([grads[l][n] for l in range(depth)]).reshape(-1) for n in small]
    vecs.append(dnf.reshape(-1))
    vecs.append(jnp.stack([grads[l]["conv_w"] for l in range(depth)]).reshape(-1))
    sizes = [v.shape[0] for v in vecs]
    flat = jnp.concatenate(vecs)
    n_small = sum(sizes[:-1])
    pad = (-flat.shape[0]) % 1024
    flat = jnp.pad(flat, (0, pad)).reshape(-1, 128)
    allp = _all_gather(flat, "gather_small_grads").reshape(N_DEV, -1)
    rep_w = jnp.concatenate([norm1.reshape(-1), b_gate.reshape(-1), norm2.reshape(-1), conv_b.reshape(-1), norm_f])
    rep_m = jnp.concatenate([moments_m[n].reshape(-1) for n in small] + [m_norm_f])
    rep_v = jnp.concatenate([moments_v[n].reshape(-1) for n in small] + [v_norm_f])
    rows = n_small // 128
    res = _adamw(allp[:, :n_small].reshape(N_DEV, rows, 128), rep_w.reshape(rows, 128), rep_m.reshape(rows, 128),
                 rep_v.reshape(rows, 128), "adamw_small")
    off = 0
    for n, sz in zip(small + ("norm_f",), sizes[:-1]):
        shp = norm_f.shape if n == "norm_f" else p[n].shape
        out_g[n], out_d[n], out_m[n], out_v[n] = [r.reshape(-1)[off:off + sz].reshape(shp) for r in res]
        off += sz
    f = conv_w.shape[2]
    cwp = allp[:, n_small:n_small + sizes[-1]].reshape(N_DEV, depth * 3, D_FF)
    cwp = lax.dynamic_slice_in_dim(cwp, me * f, f, axis=2)
    res = _adamw(cwp, conv_w.reshape(depth * 3, f), m_conv_w.reshape(depth * 3, f), v_conv_w.reshape(depth * 3, f),
                 "adamw_conv_w")
    out_g["conv_w"], out_d["conv_w"], out_m["conv_w"], out_v["conv_w"] = [r.reshape(conv_w.shape) for r in res]

    order = ("norm1", "w_in", "b_gate", "w_br", "w_o", "norm2", "w_up", "conv_w", "conv_b", "w_down", "norm_f")
    return (loss, grad_x, *[out_g[n] for n in order], *[out_d[n] for n in order],
            *[out_m[n] for n in order], *[out_v[n] for n in order])
```

```python
import functools

import jax
import jax.numpy as jnp
from jax import lax
from jax.experimental import pallas as pl
from jax.experimental.pallas import tpu as pltpu

BF = jnp.bfloat16
F32 = jnp.float32

N_DEV = 8
D_MODEL = 1024
HEAD_DIM = 64
DSW_GROUPS = ((128, 1), (512, 4), (2048, 16))
HEADS_PER_GROUP = 4
N_GROUPS = len(DSW_GROUPS)
DSW_HEADS = HEADS_PER_GROUP * N_GROUPS
SB_HEADS = 4
W_A = DSW_HEADS * HEAD_DIM
W_B = SB_HEADS * HEAD_DIM
OUT_A = HEADS_PER_GROUP * HEAD_DIM
N_IN = 3 * W_A + 3 * W_B + 2 * D_MODEL
GATE_OFF = 3 * W_A + 3 * W_B
D_FF = 2816
SB_BLOCK = 256
RMS_EPS = 1e-6
ATT_SCALE = HEAD_DIM ** -0.5
NEG = -1e30
SB_EXIT = -110.0

ADAM_LR = 0.001
ADAM_B1 = 0.9
ADAM_B2 = 0.999
ADAM_EPS = 1e-08
ADAM_WD = 0.01
ADAM_STEP = 10

HBM_SPEC = pl.BlockSpec(memory_space=pltpu.HBM)
MESH = pl.DeviceIdType.MESH

_NN = (((1,), (0,)), ((), ()))
_NT = (((1,), (1,)), ((), ()))
_TN = (((0,), (0,)), ((), ()))


def _dot(a, b, dn=_NN):
    return lax.dot_general(a, b, dn, preferred_element_type=F32)


def _pick(dim, pref):
    if dim <= pref:
        return dim
    t = (pref // 128) * 128
    while t >= 128:
        if dim % t == 0:
            return t
        t -= 128
    return dim


def _params(*sem):
    return pltpu.CompilerParams(dimension_semantics=sem)


def _peer(k):
    x, y, c = lax.axis_index("x"), lax.axis_index("y"), lax.axis_index("c")
    px = 1 - x if (k >> 2) & 1 else x
    py = 1 - y if (k >> 1) & 1 else y
    pc = 1 - c if k & 1 else c
    return (px, py, pc), 4 * px + 2 * py + pc


def _exchange(kind, x_ref, out_ref, send_sems, recv_sems, local_sem):
    gather = kind == "gather"
    _, me = _peer(0)

    def src(idx):
        return x_ref if gather else x_ref.at[idx]

    def copy(k, dst_idx):
        peer, pidx = _peer(k)
        return pltpu.make_async_remote_copy(
            src_ref=src(pidx), dst_ref=out_ref.at[dst_idx], send_sem=send_sems.at[k - 1],
            recv_sem=recv_sems.at[k - 1], device_id=peer, device_id_type=MESH)

    mine = pltpu.make_async_copy(src(me), out_ref.at[me], local_sem)

    def start():
        mine.start()
        for k in range(1, N_DEV):
            copy(k, me).start()

    def wait():
        for k in range(1, N_DEV):
            copy(k, _peer(k)[1]).wait_recv()
        for k in range(1, N_DEV):
            copy(k, me).wait_send()
        mine.wait()

    return start, wait


_EXCHANGE_SEMS = [pltpu.SemaphoreType.DMA((N_DEV - 1,)), pltpu.SemaphoreType.DMA((N_DEV - 1,)),
                  pltpu.SemaphoreType.DMA]


def _exchange_shape(kind, x):
    return jax.ShapeDtypeStruct(((N_DEV,) + x.shape) if kind == "gather" else x.shape, x.dtype)


def _exchange_alone(kind, x, name):
    def body(x_ref, out_ref, send_sems, recv_sems, local_sem):
        start, wait = _exchange(kind, x_ref, out_ref, send_sems, recv_sems, local_sem)
        start()
        wait()

    return pl.pallas_call(
        body, name=name, out_shape=_exchange_shape(kind, x),
        in_specs=[HBM_SPEC], out_specs=HBM_SPEC, scratch_shapes=list(_EXCHANGE_SEMS),
    )(x)


def _all_gather(x, name):
    return _exchange_alone("gather", x, name)


def _all_to_all(x, name):
    return _exchange_alone("scatter", x, name)


def _call(body, *, name, grid, in_specs, out_specs, out_shape, args, scratch_shapes=(), sem=None, comm=()):
    single = not isinstance(out_shape, (tuple, list))
    outs = (out_shape,) if single else tuple(out_shape)
    ospecs = (out_specs,) if single else tuple(out_specs)
    if not comm:
        res = pl.pallas_call(
            body, name=name, out_shape=outs, grid=grid, in_specs=list(in_specs), out_specs=ospecs,
            scratch_shapes=list(scratch_shapes), compiler_params=_params(*(sem or ("arbitrary",) * len(grid))),
        )(*args)
        return res
    n_in, n_out, n_scr, nc = len(in_specs), len(outs), len(scratch_shapes), len(comm)

    def wrapped(*refs):
        ins = refs[:n_in]
        cins = refs[n_in:n_in + nc]
        o0 = n_in + nc
        kouts = refs[o0:o0 + n_out]
        couts = refs[o0 + n_out:o0 + n_out + nc]
        s0 = o0 + n_out + nc
        scr = refs[s0:s0 + n_scr]
        sems = refs[s0 + n_scr:]
        ids = [pl.program_id(ax) for ax in range(len(grid))]
        first = functools.reduce(jnp.logical_and, [i == 0 for i in ids])
        last = functools.reduce(jnp.logical_and, [i == g - 1 for i, g in zip(ids, grid)])
        ex = [_exchange(comm[c][0], cins[c], couts[c], *sems[3 * c:3 * c + 3]) for c in range(nc)]

        @pl.when(first)
        def _():
            for start, _ in ex:
                start()

        body(*ins, *kouts, *scr)

        @pl.when(last)
        def _():
            for _, wait in ex:
                wait()

    return pl.pallas_call(
        wrapped, name=name,
        out_shape=outs + tuple(_exchange_shape(k, x) for k, x in comm),
        grid=grid, in_specs=list(in_specs) + [HBM_SPEC] * nc, out_specs=ospecs + (HBM_SPEC,) * nc,
        scratch_shapes=list(scratch_shapes) + list(_EXCHANGE_SEMS) * nc,
        compiler_params=_params(*(("arbitrary",) * len(grid))),
    )(*args, *[x for _, x in comm])


def _matmul(a, b, *, mode, out_dtype, name, tm=512, tn=1024, tk=1024, res=None, comm=()):
    if mode == "nn":
        (M, K), (_, N) = a.shape, b.shape
    elif mode == "nt":
        (M, K), (N, _) = a.shape, b.shape
    else:
        (K, M), (_, N) = a.shape, b.shape
    tm, tn, tk = _pick(M, tm), _pick(N, tn), _pick(K, tk)
    nk = K // tk
    dn = {"nn": _NN, "nt": _NT, "tn": _TN}[mode]

    def body(*refs):
        a_ref, b_ref = refs[0], refs[1]
        r_ref = refs[2] if res is not None else None
        o_ref = refs[3] if res is not None else refs[2]

        def finish(r):
            if res is not None:
                r = r + r_ref[...].astype(F32)
            o_ref[...] = r.astype(out_dtype)

        part = _dot(a_ref[...].astype(BF), b_ref[...].astype(BF), dn)
        if nk == 1:
            finish(part)
            return
        acc = refs[-1]
        k = pl.program_id(2)

        @pl.when(k == 0)
        def _():
            acc[...] = jnp.zeros_like(acc)

        acc[...] += part

        @pl.when(k == nk - 1)
        def _():
            finish(acc[...])

    if mode == "tn":
        a_spec = pl.BlockSpec((tk, tm), lambda i, j, k: (k, i))
    else:
        a_spec = pl.BlockSpec((tm, tk), lambda i, j, k: (i, k))
    if mode == "nt":
        b_spec = pl.BlockSpec((tn, tk), lambda i, j, k: (j, k))
    else:
        b_spec = pl.BlockSpec((tk, tn), lambda i, j, k: (k, j))
    o_spec = pl.BlockSpec((tm, tn), lambda i, j, k: (i, j))
    in_specs = [a_spec, b_spec] + ([o_spec] if res is not None else [])
    args = (a, b) + ((res,) if res is not None else ())
    out = _call(body, name=name, grid=(M // tm, N // tn, nk), in_specs=in_specs, out_specs=o_spec,
                out_shape=jax.ShapeDtypeStruct((M, N), out_dtype), args=args,
                scratch_shapes=[pltpu.VMEM((tm, tn), F32)] if nk > 1 else [],
                sem=("parallel", "parallel", "arbitrary"), comm=comm)
    return out[0], list(out[1:])


def _rms_fwd(x, g, name):
    S, D = x.shape
    tm = _pick(S, 512)

    def body(x_ref, g_ref, h_ref):
        xf = x_ref[...]
        r = lax.rsqrt(jnp.mean(xf * xf, axis=-1, keepdims=True) + RMS_EPS)
        h_ref[...] = (xf * r * g_ref[...]).astype(BF)

    return pl.pallas_call(
        body, name=name,
        out_shape=jax.ShapeDtypeStruct((S, D), BF),
        grid=(S // tm,),
        in_specs=[pl.BlockSpec((tm, D), lambda i: (i, 0)), pl.BlockSpec((1, D), lambda i: (0, 0))],
        out_specs=pl.BlockSpec((tm, D), lambda i: (i, 0)),
        compiler_params=_params("parallel"),
    )(x, g.reshape(1, D))


def _rms_bwd(x, g, dh, dres, name):
    S, D = x.shape
    tm = _pick(S, 512)

    def body(x_ref, g_ref, dh_ref, dres_ref, dx_ref, dg_ref):
        @pl.when(pl.program_id(0) == 0)
        def _():
            dg_ref[...] = jnp.zeros_like(dg_ref)

        xf = x_ref[...]
        r = lax.rsqrt(jnp.mean(xf * xf, axis=-1, keepdims=True) + RMS_EPS)
        xh = xf * r
        dy = dh_ref[...].astype(F32)
        dg_ref[...] += jnp.sum(dy * xh, axis=0, keepdims=True)
        dxh = dy * g_ref[...]
        dx = r * (dxh - xh * jnp.mean(dxh * xh, axis=-1, keepdims=True))
        dx_ref[...] = dres_ref[...] + dx

    row = pl.BlockSpec((tm, D), lambda i: (i, 0))
    vec = pl.BlockSpec((1, D), lambda i: (0, 0))
    return pl.pallas_call(
        body, name=name,
        out_shape=(jax.ShapeDtypeStruct((S, D), F32), jax.ShapeDtypeStruct((1, D), F32)),
        grid=(S // tm,),
        in_specs=[row, vec, row, row], out_specs=(row, vec),
        compiler_params=_params("arbitrary"),
    )(x, g.reshape(1, D), dh, dres)


def _loss_head(x, g, target):
    S, D = x.shape
    tm = _pick(S, 512)

    def body(x_ref, g_ref, t_ref, loss_ref, dx_ref, dg_ref):
        @pl.when(pl.program_id(0) == 0)
        def _():
            dg_ref[...] = jnp.zeros_like(dg_ref)
            loss_ref[...] = jnp.zeros_like(loss_ref)

        xf = x_ref[...]
        gg = g_ref[...]
        r = lax.rsqrt(jnp.mean(xf * xf, axis=-1, keepdims=True) + RMS_EPS)
        xh = xf * r
        err = xh * gg - t_ref[...]
        per_tok = jnp.mean(err * err, axis=-1, keepdims=True)
        loss_ref[...] += 0.5 * jnp.sum(per_tok, axis=0, keepdims=True)
        dy = err * (1.0 / D)
        dg_ref[...] += jnp.sum(dy * xh, axis=0, keepdims=True)
        dxh = dy * gg
        dx_ref[...] = r * (dxh - xh * jnp.mean(dxh * xh, axis=-1, keepdims=True))

    row = pl.BlockSpec((tm, D), lambda i: (i, 0))
    vec = pl.BlockSpec((1, D), lambda i: (0, 0))
    one = pl.BlockSpec((1, 1), lambda i: (0, 0))
    return pl.pallas_call(
        body, name="loss_head",
        out_shape=(jax.ShapeDtypeStruct((1, 1), F32), jax.ShapeDtypeStruct((S, D), F32),
                   jax.ShapeDtypeStruct((1, D), F32)),
        grid=(S // tm,),
        in_specs=[row, vec, row], out_specs=(one, row, vec),
        compiler_params=_params("arbitrary"),
    )(x, g.reshape(1, D), target)


def _slopes(g):
    return [2.0 ** (-8.0 * (HEADS_PER_GROUP * g + j + 1) / DSW_HEADS) for j in range(HEADS_PER_GROUP)]


def _band_masks(W):
    row = lax.broadcasted_iota(jnp.int32, (W, W), 0)
    col = lax.broadcasted_iota(jnp.int32, (W, W), 1)
    d_cur = row - col
    d_prev = d_cur + W
    return d_cur, d_prev, d_cur >= 0, d_cur <= 0


def _band_specs(W, nb):
    def cur(c):
        return pl.BlockSpec((None, W, OUT_A), lambda r, n: (r, n, c))

    def prev(c):
        return pl.BlockSpec((None, W, OUT_A), lambda r, n: (r, jnp.maximum(n - 1, 0), c))

    def nxt(c):
        return pl.BlockSpec((None, W, OUT_A), lambda r, n: (r, jnp.minimum(n + 1, nb - 1), c))

    return cur, prev, nxt


def _attn_a_fwd(qkv, cols, g, comm=()):
    win, dil = DSW_GROUPS[g]
    W = win // dil
    d, L, _ = qkv[0].shape
    nb = L // W
    H, hd = HEADS_PER_GROUP, HEAD_DIM
    slopes = _slopes(g)

    def body(q_ref, kp_ref, kc_ref, vp_ref, vc_ref, o_ref, l_ref):
        n = pl.program_id(1)
        d_cur, d_prev, m_cur, m_prev = _band_masks(W)
        m_prev = jnp.logical_and(m_prev, n > 0)
        for j in range(H):
            ln = slice(j * hd, (j + 1) * hd)
            sl = slopes[j] * dil
            qh = q_ref[:, ln]
            s_c = _dot(qh, kc_ref[:, ln], _NT) * ATT_SCALE - sl * d_cur.astype(F32)
            s_p = _dot(qh, kp_ref[:, ln], _NT) * ATT_SCALE - sl * d_prev.astype(F32)
            s_c = jnp.where(m_cur, s_c, NEG)
            s_p = jnp.where(m_prev, s_p, NEG)
            m = jnp.maximum(jnp.max(s_c, axis=1, keepdims=True), jnp.max(s_p, axis=1, keepdims=True))
            p_c = jnp.exp(s_c - m)
            p_p = jnp.exp(s_p - m)
            den = jnp.sum(p_c, axis=1, keepdims=True) + jnp.sum(p_p, axis=1, keepdims=True)
            o = _dot(p_c.astype(BF), vc_ref[:, ln]) + _dot(p_p.astype(BF), vp_ref[:, ln])
            o_ref[:, ln] = o / den
            l_ref[:, ln] = jnp.broadcast_to(m + jnp.log(den), (W, hd))

    cur, prev, _ = _band_specs(W, nb)
    out = jax.ShapeDtypeStruct((d, L, OUT_A), F32)
    res = _call(body, name=f"attn_a_fwd_g{g}", grid=(d, nb),
                in_specs=[cur(cols[0]), prev(cols[1]), cur(cols[1]), prev(cols[2]), cur(cols[2])],
                out_specs=(cur(0), cur(0)), out_shape=(out, out),
                args=(qkv[0], qkv[1], qkv[1], qkv[2], qkv[2]), sem=("parallel", "parallel"), comm=comm)
    return res[0], res[1], list(res[2:])


def _attn_a_bwd(qkv, cols, do, lse, dsum, g):
    win, dil = DSW_GROUPS[g]
    W = win // dil
    d, L, _ = qkv[0].shape
    nb = L // W
    H, hd = HEADS_PER_GROUP, HEAD_DIM
    slopes = _slopes(g)

    def body(q_ref, qn_ref, kp_ref, kc_ref, vp_ref, vc_ref, do_ref, don_ref, l_ref, ln_ref,
             ds_ref, dsn_ref, dq_ref, dk_ref, dv_ref):
        n = pl.program_id(1)
        d_cur, d_prev, m_cur, m_prev = _band_masks(W)
        m_p = jnp.logical_and(m_prev, n > 0)
        m_n = jnp.logical_and(m_prev, n < nb - 1)
        for j in range(H):
            ln = slice(j * hd, (j + 1) * hd)
            c0 = slice(j * hd, j * hd + 1)
            sl = slopes[j] * dil
            b_cur = sl * d_cur.astype(F32)
            b_prev = sl * d_prev.astype(F32)
            qh, qn = q_ref[:, ln], qn_ref[:, ln]
            kc, kp, vc, vp = kc_ref[:, ln], kp_ref[:, ln], vc_ref[:, ln], vp_ref[:, ln]
            dob, donb = do_ref[:, ln].astype(BF), don_ref[:, ln].astype(BF)
            lse_c, lse_n = l_ref[:, c0], ln_ref[:, c0]
            dsum_c, dsum_n = ds_ref[:, c0], dsn_ref[:, c0]
            p_cc = jnp.exp(jnp.where(m_cur, _dot(qh, kc, _NT) * ATT_SCALE - b_cur, NEG) - lse_c)
            p_cp = jnp.exp(jnp.where(m_p, _dot(qh, kp, _NT) * ATT_SCALE - b_prev, NEG) - lse_c)
            p_nc = jnp.exp(jnp.where(m_n, _dot(qn, kc, _NT) * ATT_SCALE - b_prev, NEG) - lse_n)
            ds_cc = (p_cc * (_dot(dob, vc, _NT) - dsum_c) * ATT_SCALE).astype(BF)
            ds_cp = (p_cp * (_dot(dob, vp, _NT) - dsum_c) * ATT_SCALE).astype(BF)
            ds_nc = (p_nc * (_dot(donb, vc, _NT) - dsum_n) * ATT_SCALE).astype(BF)
            dq_ref[:, ln] = (_dot(ds_cc, kc) + _dot(ds_cp, kp)).astype(BF)
            dk_ref[:, ln] = (_dot(ds_cc, qh, _TN) + _dot(ds_nc, qn, _TN)).astype(BF)
            dv_ref[:, ln] = (_dot(p_cc.astype(BF), dob, _TN) + _dot(p_nc.astype(BF), donb, _TN)).astype(BF)

    cur, prev, nxt = _band_specs(W, nb)
    out = jax.ShapeDtypeStruct((d, L, OUT_A), BF)
    cq, ck, cv = cols
    return pl.pallas_call(
        body, name=f"attn_a_bwd_g{g}",
        out_shape=(out, out, out),
        grid=(d, nb),
        in_specs=[cur(cq), nxt(cq), prev(ck), cur(ck), prev(cv), cur(cv),
                  cur(0), nxt(0), cur(0), nxt(0), cur(0), nxt(0)],
        out_specs=(cur(0), cur(0), cur(0)),
        compiler_params=_params("parallel", "parallel"),
    )(qkv[0], qkv[0], qkv[1], qkv[1], qkv[2], qkv[2], do, do, lse, lse, dsum, dsum)


SB_PAIR = 2
SB_QROWS = SB_BLOCK


def _softplus_parts(z):
    e = jnp.exp(-jnp.abs(z))
    log1p_e = jnp.where(e < 1e-3, e * (1.0 - e * (0.5 - e * (1.0 / 3.0))), jnp.log(1.0 + e))
    return e, jnp.maximum(z, 0.0) + log1p_e


def _split_dot(x, t):
    hi = x.astype(BF)
    lo = (x - hi.astype(F32)).astype(BF)
    return _dot(hi, t) + _dot(lo, t)


def _sb_block(qh, kk, causal, r_run, tri_incl):
    z = _dot(qh, kk, _NT) * ATT_SCALE
    e, sp = _softplus_parts(z)
    ls = jnp.where(causal, -sp, 0.0)
    cin = _split_dot(ls, tri_incl)
    a = jnp.where(causal, jnp.exp(z + cin + r_run), 0.0)
    return z, e, cin, a


def _sb_specs(S):
    Q, hd = SB_QROWS, HEAD_DIM
    lanes = SB_PAIR * hd
    qc = (3 * W_A) // lanes
    kc = (3 * W_A + W_B) // lanes
    vc = (3 * W_A + 2 * W_B) // lanes
    q_spec = pl.BlockSpec((Q, lanes), lambda p, i: (i, qc + p))
    k_spec = pl.BlockSpec((S, lanes), lambda p, i: (0, kc + p))
    v_spec = pl.BlockSpec((S, lanes), lambda p, i: (0, vc + p))
    o_spec = pl.BlockSpec((Q, lanes), lambda p, i: (i, p))
    full = pl.BlockSpec((S, lanes), lambda p, i: (0, p))
    return q_spec, k_spec, v_spec, o_spec, full


def _sb_iotas(i):
    B, Q = SB_BLOCK, SB_QROWS
    row = lax.broadcasted_iota(jnp.int32, (Q, B), 0) + i * Q
    col = lax.broadcasted_iota(jnp.int32, (Q, B), 1)
    tr = lax.broadcasted_iota(jnp.int32, (B, B), 0)
    tc = lax.broadcasted_iota(jnp.int32, (B, B), 1)
    return row, col, tr, tc


def _sb_fwd(proj, comm=()):
    S = proj.shape[0]
    B, Q, hd = SB_BLOCK, SB_QROWS, HEAD_DIM
    nq = S // Q
    q_spec, k_spec, v_spec, o_spec, _ = _sb_specs(S)

    def body(q_ref, k_ref, v_ref, o_ref):
        i = pl.program_id(1)
        row, col, tr, tc = _sb_iotas(i)
        tri_incl = (tr >= tc).astype(BF)
        qs = [q_ref[:, h * hd:(h + 1) * hd] for h in range(SB_PAIR)]

        def cond(c):
            return jnp.logical_and(c[0] >= 0, c[-1] > SB_EXIT)

        def step(c):
            kb = c[0]
            off = pl.multiple_of(kb * B, B)
            causal = col + kb * B < row
            runs, accs = [], []
            for h in range(SB_PAIR):
                r_run, acc = c[1 + h], c[1 + SB_PAIR + h]
                kk = k_ref[pl.ds(off, B), h * hd:(h + 1) * hd]
                vv = v_ref[pl.ds(off, B), h * hd:(h + 1) * hd]
                _, _, cin, a = _sb_block(qs[h], kk, causal, r_run, tri_incl)
                accs.append(acc + _dot(a.astype(BF), vv))
                runs.append(r_run + cin[:, 0:1])
            rmax = functools.reduce(jnp.maximum, [jnp.max(r) for r in runs])
            return (kb - 1, *runs, *accs, rmax)

        init = (i,*[jnp.zeros((Q, 1), F32)] * SB_PAIR, *[jnp.zeros((Q, hd), F32)] * SB_PAIR,
                jnp.float32(0.0))
        fin = lax.while_loop(cond, step, init)
        for h in range(SB_PAIR):
            o_ref[:, h * hd:(h + 1) * hd] = fin[1 + SB_PAIR + h]

    res = _call(body, name="sb_fwd", grid=(SB_HEADS // SB_PAIR, nq), in_specs=[q_spec, k_spec, v_spec],
                out_specs=o_spec, out_shape=jax.ShapeDtypeStruct((S, W_B), F32), args=(proj, proj, proj),
                sem=("parallel", "parallel"), comm=comm)
    return res[0], list(res[1:])


def _sb_bwd(proj, do, o, comm=()):
    S = proj.shape[0]
    B, Q, hd = SB_BLOCK, SB_QROWS, HEAD_DIM
    nq = S // Q
    q_spec, k_spec, v_spec, o_spec, full = _sb_specs(S)
    P = SB_PAIR

    def body(q_ref, k_ref, v_ref, do_ref, o_ref, dq_ref, dk_ref, dv_ref):
        i = pl.program_id(1)

        @pl.when(i == 0)
        def _():
            dk_ref[...] = jnp.zeros_like(dk_ref)
            dv_ref[...] = jnp.zeros_like(dv_ref)

        row, col, tr, tc = _sb_iotas(i)
        tri_incl = (tr >= tc).astype(BF)
        tri_strict = (tr > tc).astype(BF)
        qs = [q_ref[:, h * hd:(h + 1) * hd] for h in range(P)]
        dobs = [do_ref[:, h * hd:(h + 1) * hd].astype(BF) for h in range(P)]
        dsums = [jnp.sum(dobs[h].astype(F32) * o_ref[:, h * hd:(h + 1) * hd], axis=1, keepdims=True)
                 for h in range(P)]

        def cond(c):
            return jnp.logical_and(c[0] >= 0, c[-1] > SB_EXIT)

        def step(c):
            kb = c[0]
            off = pl.multiple_of(kb * B, B)
            causal = col + kb * B < row
            runs, gruns, dqs = [], [], []
            for h in range(P):
                r_run, g_run, dq = c[1 + h], c[1 + P + h], c[1 + 2 * P + h]
                ln = slice(h * hd, (h + 1) * hd)
                kk = k_ref[pl.ds(off, B), ln]
                vv = v_ref[pl.ds(off, B), ln]
                z, e, cin, a = _sb_block(qs[h], kk, causal, r_run, tri_incl)
                a16 = a.astype(BF)
                gmat = a16.astype(F32) * _dot(dobs[h], vv, _NT)
                later = _split_dot(gmat, tri_strict)
                pfx = dsums[h] - g_run - later
                sig = jnp.where(z >= 0, 1.0, e) / (1.0 + e)
                dz = (jnp.where(causal, gmat - sig * pfx, 0.0) * ATT_SCALE).astype(BF)
                dqs.append(dq + _dot(dz, kk))
                dk_ref[pl.ds(off, B), ln] += _dot(dz, qs[h], _TN)
                dv_ref[pl.ds(off, B), ln] += _dot(a16, dobs[h], _TN)
                gruns.append(g_run + jnp.sum(gmat, axis=1, keepdims=True))
                runs.append(r_run + cin[:, 0:1])
            rmax = functools.reduce(jnp.maximum, [jnp.max(r) for r in runs])
            return (kb - 1, *runs, *gruns, *dqs, rmax)

        init = (i,*[jnp.zeros((Q, 1), F32)] * (2 * P), *[jnp.zeros((Q, hd), F32)] * P, jnp.float32(0.0))
        fin = lax.while_loop(cond, step, init)
        for h in range(P):
            dq_ref[:, h * hd:(h + 1) * hd] = fin[1 + 2 * P + h]

    out = jax.ShapeDtypeStruct((S, W_B), F32)
    res = _call(body, name="sb_bwd", grid=(SB_HEADS // P, nq), in_specs=[q_spec, k_spec, v_spec, o_spec, o_spec],
                out_specs=(o_spec, full, full), out_shape=(out, out, out), args=(proj, proj, proj, do, o), comm=comm)
    return res[0], res[1], res[2], list(res[3:])


def _merge_fwd(o_g, l_g, o_b, proj, b_gate, w_br):
    S = o_b.shape[0]
    D = D_MODEL
    tm = _pick(S, 256)
    gcol = GATE_OFF // D

    def body(o0, o1, o2, l0, l1, l2, ob_ref, ga_ref, gb_ref, bg_ref, w_ref, mg_ref, oa_ref, lse_ref):
        la, lb, lc = l0[...], l1[...], l2[...]
        mx = jnp.maximum(jnp.maximum(la, lb), lc)
        ea, eb, ec = jnp.exp(la - mx), jnp.exp(lb - mx), jnp.exp(lc - mx)
        den = ea + eb + ec
        oa = (ea * o0[...] + eb * o1[...] + ec * o2[...]) / den
        oa_ref[...] = oa
        lse_ref[...] = mx + jnp.log(den)
        ya = _dot(oa.astype(BF), w_ref[0:OUT_A, :])
        yb = _dot(ob_ref[...].astype(BF), w_ref[OUT_A:OUT_A + W_B, :])
        bg = bg_ref[...]
        g_a = jax.nn.sigmoid(ga_ref[...].astype(F32) + bg[:, 0:D])
        g_b = jax.nn.sigmoid(gb_ref[...].astype(F32) + bg[:, D:2 * D])
        mg_ref[...] = (g_a * ya + g_b * yb).astype(BF)

    nar = pl.BlockSpec((tm, OUT_A), lambda i: (i, 0))
    wide = pl.BlockSpec((tm, D), lambda i: (i, 0))
    return pl.pallas_call(
        body, name="merge_fwd",
        out_shape=(jax.ShapeDtypeStruct((S, D), BF), jax.ShapeDtypeStruct((S, OUT_A), F32),
                   jax.ShapeDtypeStruct((S, OUT_A), F32)),
        grid=(S // tm,),
        in_specs=[nar] * 7 + [pl.BlockSpec((tm, D), lambda i: (i, gcol)),
                              pl.BlockSpec((tm, D), lambda i: (i, gcol + 1)),
                              pl.BlockSpec((1, 2 * D), lambda i: (0, 0)),
                              pl.BlockSpec((OUT_A + W_B, D), lambda i: (0, 0))],
        out_specs=(wide, nar, nar),
        compiler_params=_params("parallel"),
    )(*o_g, *l_g, o_b, proj, proj, b_gate.reshape(1, 2 * D), w_br)


def _merge_bwd(dm, oa, ob, proj, b_gate, w_br):
    S = ob.shape[0]
    D = D_MODEL
    tm = _pick(S, 256)
    gcol = GATE_OFF // D

    def body(dm_ref, oa_ref, ob_ref, ga_ref, gb_ref, bg_ref, w_ref,
             dya_ref, dyb_ref, doa_ref, dob_ref, ds_ref, dg_ref, dbg_ref):
        @pl.when(pl.program_id(0) == 0)
        def _():
            dbg_ref[...] = jnp.zeros_like(dbg_ref)

        wa = w_ref[0:OUT_A, :]
        wb = w_ref[OUT_A:OUT_A + W_B, :]
        oa = oa_ref[...]
        ya = _dot(oa.astype(BF), wa)
        yb = _dot(ob_ref[...].astype(BF), wb)
        bg = bg_ref[...]
        g_a = jax.nn.sigmoid(ga_ref[...].astype(F32) + bg[:, 0:D])
        g_b = jax.nn.sigmoid(gb_ref[...].astype(F32) + bg[:, D:2 * D])
        dm = dm_ref[...].astype(F32)
        dga = dm * ya * g_a * (1.0 - g_a)
        dgb = dm * yb * g_b * (1.0 - g_b)
        dg_ref[:, 0:D] = dga.astype(BF)
        dg_ref[:, D:2 * D] = dgb.astype(BF)
        dbg_ref[:, 0:D] += jnp.sum(dga, axis=0, keepdims=True)
        dbg_ref[:, D:2 * D] += jnp.sum(dgb, axis=0, keepdims=True)
        dya = (dm * g_a).astype(BF)
        dyb = (dm * g_b).astype(BF)
        dya_ref[...] = dya
        dyb_ref[...] = dyb
        doa = _dot(dya, wa, _NT)
        doa_ref[...] = doa
        dob_ref[...] = _dot(dyb, wb, _NT)
        r = lax.broadcasted_iota(jnp.int32, (OUT_A, OUT_A), 0) // HEAD_DIM
        c = lax.broadcasted_iota(jnp.int32, (OUT_A, OUT_A), 1) // HEAD_DIM
        ds_ref[...] = _split_dot(doa * oa, (r == c).astype(BF))

    nar = pl.BlockSpec((tm, OUT_A), lambda i: (i, 0))
    wide = pl.BlockSpec((tm, D), lambda i: (i, 0))
    wide2 = pl.BlockSpec((tm, 2 * D), lambda i: (i, 0))
    vec2 = pl.BlockSpec((1, 2 * D), lambda i: (0, 0))
    return pl.pallas_call(
        body, name="merge_bwd",
        out_shape=(jax.ShapeDtypeStruct((S, D), BF), jax.ShapeDtypeStruct((S, D), BF),
                   jax.ShapeDtypeStruct((S, OUT_A), F32), jax.ShapeDtypeStruct((S, W_B), F32),
                   jax.ShapeDtypeStruct((S, OUT_A), F32), jax.ShapeDtypeStruct((S, 2 * D), BF),
                   jax.ShapeDtypeStruct((1, 2 * D), F32)),
        grid=(S // tm,),
        in_specs=[wide, nar, nar, pl.BlockSpec((tm, D), lambda i: (i, gcol)),
                  pl.BlockSpec((tm, D), lambda i: (i, gcol + 1)), vec2,
                  pl.BlockSpec((OUT_A + W_B, D), lambda i: (0, 0))],
        out_specs=(wide, wide, nar, nar, nar, wide2, vec2),
        compiler_params=_params("arbitrary"),
    )(dm, oa, ob, proj, proj, b_gate.reshape(1, 2 * D), w_br)


_SQRT_HALF = 0.7071067811865476
_INV_SQRT_2PI = 0.3989422804014327


def _gelu_parts(a):
    cdf = 0.5 * (1.0 + lax.erf(a * _SQRT_HALF))
    pdf = _INV_SQRT_2PI * jnp.exp(-0.5 * a * a)
    return cdf, pdf


def _shift_down(a, halo, k):
    rows = lax.broadcasted_iota(jnp.int32, a.shape, 0)
    out = pltpu.roll(a, k, 0)
    for r in range(k):
        out = jnp.where(rows == r, halo[8 - k + r:8 - k + r + 1, :], out)
    return out


def _shift_up(a, halo, k):
    n = a.shape[0]
    rows = lax.broadcasted_iota(jnp.int32, a.shape, 0)
    out = pltpu.roll(a, n - k, 0)
    for r in range(k):
        out = jnp.where(rows == n - k + r, halo[r:r + 1, :], out)
    return out


def _conv_in(a_ref, h_ref, first):
    a = a_ref[...].astype(F32)
    halo = jnp.where(first, 0.0, h_ref[...].astype(F32))
    return a, _shift_down(a, halo, 1), _shift_down(a, halo, 2)


def _ffn_specs(S, tm):
    F = D_FF
    t8 = tm // 8
    a_spec = pl.BlockSpec((tm, F), lambda i: (i, 0))
    v_spec = pl.BlockSpec((tm, F), lambda i: (i, 1))
    halo_prev = pl.BlockSpec((8, F), lambda i: (jnp.maximum(i * t8 - 1, 0), 0))
    return a_spec, v_spec, halo_prev


def _ffn_act_fwd(up, conv_w, conv_b):
    S = up.shape[0]
    F = D_FF
    tm = _pick(S, 256)
    a_spec, v_spec, halo_prev = _ffn_specs(S, tm)

    def body(a_ref, h_ref, v_ref, w_ref, b_ref, act_ref):
        a0, a1, a2 = _conv_in(a_ref, h_ref, pl.program_id(0) == 0)
        w = w_ref[...]
        ac = b_ref[...] + w[0:1, :] * a2 + w[1:2, :] * a1 + w[2:3, :] * a0
        cdf, _ = _gelu_parts(ac)
        act_ref[...] = (ac * cdf * v_ref[...].astype(F32)).astype(BF)

    return pl.pallas_call(
        body, name="ffn_act_fwd",
        out_shape=jax.ShapeDtypeStruct((S, F), BF),
        grid=(S // tm,),
        in_specs=[a_spec, halo_prev, v_spec, pl.BlockSpec((3, F), lambda i: (0, 0)),
                  pl.BlockSpec((1, F), lambda i: (0, 0))],
        out_specs=a_spec,
        compiler_params=_params("parallel"),
    )(up, up, up, conv_w, conv_b.reshape(1, F))


def _ffn_act_bwd1(dact, up, conv_w, conv_b, comm=()):
    S = up.shape[0]
    F = D_FF
    tm = _pick(S, 256)
    a_spec, v_spec, halo_prev = _ffn_specs(S, tm)

    def body(da_ref, a_ref, h_ref, v_ref, w_ref, b_ref, dac_ref, dv_ref, dw_ref, db_ref):
        @pl.when(pl.program_id(0) == 0)
        def _():
            dw_ref[...] = jnp.zeros_like(dw_ref)
            db_ref[...] = jnp.zeros_like(db_ref)

        a0, a1, a2 = _conv_in(a_ref, h_ref, pl.program_id(0) == 0)
        w = w_ref[...]
        ac = b_ref[...] + w[0:1, :] * a2 + w[1:2, :] * a1 + w[2:3, :] * a0
        cdf, pdf = _gelu_parts(ac)
        dact = da_ref[...].astype(F32)
        dv_ref[...] = (dact * ac * cdf).astype(BF)
        dac = dact * v_ref[...].astype(F32) * (cdf + ac * pdf)
        dac_ref[...] = dac.astype(BF)
        db_ref[...] += jnp.sum(dac, axis=0, keepdims=True)
        dw_ref[0:1, :] += jnp.sum(dac * a2, axis=0, keepdims=True)
        dw_ref[1:2, :] += jnp.sum(dac * a1, axis=0, keepdims=True)
        dw_ref[2:3, :] += jnp.sum(dac * a0, axis=0, keepdims=True)

    w_spec = pl.BlockSpec((3, F), lambda i: (0, 0))
    b_spec = pl.BlockSpec((1, F), lambda i: (0, 0))
    res = _call(body, name="ffn_act_bwd1", grid=(S // tm,),
                in_specs=[a_spec, a_spec, halo_prev, v_spec, w_spec, b_spec],
                out_specs=(a_spec, a_spec, w_spec, b_spec),
                out_shape=(jax.ShapeDtypeStruct((S, F), BF), jax.ShapeDtypeStruct((S, F), BF),
                           jax.ShapeDtypeStruct((3, F), F32), jax.ShapeDtypeStruct((1, F), F32)),
                args=(dact, up, up, up, conv_w, conv_b.reshape(1, F)), comm=comm)
    return res[0], res[1], res[2], res[3], list(res[4:])


def _ffn_act_bwd2(dac, dv, conv_w):
    S = dac.shape[0]
    F = D_FF
    tm = _pick(S, 256)
    t8 = tm // 8
    nt = S // tm
    cur = pl.BlockSpec((tm, F), lambda i: (i, 0))
    halo_next = pl.BlockSpec((8, F), lambda i: (jnp.minimum((i + 1) * t8, S // 8 - 1), 0))

    def body(d_ref, h_ref, dv_ref, w_ref, o_ref):
        d0 = d_ref[...].astype(F32)
        halo = jnp.where(pl.program_id(0) == nt - 1, 0.0, h_ref[...].astype(F32))
        d1 = _shift_up(d0, halo, 1)
        d2 = _shift_up(d0, halo, 2)
        w = w_ref[...]
        o_ref[:, 0:F] = (w[2:3, :] * d0 + w[1:2, :] * d1 + w[0:1, :] * d2).astype(BF)
        o_ref[:, F:2 * F] = dv_ref[...]

    return pl.pallas_call(
        body, name="ffn_act_bwd2",
        out_shape=jax.ShapeDtypeStruct((S, 2 * F), BF),
        grid=(nt,),
        in_specs=[cur, halo_next, cur, pl.BlockSpec((3, F), lambda i: (0, 0))],
        out_specs=pl.BlockSpec((tm, 2 * F), lambda i: (i, 0)),
        compiler_params=_params("parallel"),
    )(dac, dac, dv, conv_w)


def _adamw(parts, w, m, v, name):
    R, C = w.shape
    tr = R
    for cand in (512, 256, 128, 64, 32, 16):
        if R % cand == 0 and cand * C * 4 <= (1 << 21):
            tr = cand
            break
    c1 = 1.0 / (1.0 - ADAM_B1 ** ADAM_STEP)
    c2 = 1.0 / (1.0 - ADAM_B2 ** ADAM_STEP)

    def body(p_ref, w_ref, m_ref, v_ref, g_ref, d_ref, nm_ref, nv_ref):
        g = p_ref[0].astype(F32)
        for j in range(1, N_DEV):
            g = g + p_ref[j].astype(F32)
        nm = ADAM_B1 * m_ref[...] + (1.0 - ADAM_B1) * g
        nv = ADAM_B2 * v_ref[...] + (1.0 - ADAM_B2) * (g * g)
        g_ref[...] = g
        nm_ref[...] = nm
        nv_ref[...] = nv
        d_ref[...] = -ADAM_LR * ((nm * c1) / (jnp.sqrt(nv * c2) + ADAM_EPS) + ADAM_WD * w_ref[...])

    blk = pl.BlockSpec((tr, C), lambda i: (i, 0))
    out = jax.ShapeDtypeStruct((R, C), F32)
    return pl.pallas_call(
        body, name=name,
        out_shape=(out, out, out, out),
        grid=(R // tr,),
        in_specs=[pl.BlockSpec((N_DEV, tr, C), lambda i: (0, i, 0)), blk, blk, blk],
        out_specs=(blk, blk, blk, blk),
        compiler_params=_params("parallel"),
    )(parts, w, m, v)


def _dil(t, dil):
    S, C = t.shape
    if dil == 1:
        return t.reshape(1, S, C)
    return t.reshape(S // dil, dil, C).transpose(1, 0, 2)


def _undil(t):
    d, L, C = t.shape
    if d == 1:
        return t.reshape(L, C)
    return t.transpose(1, 0, 2).reshape(L * d, C)


def _group_qkv(proj, g):
    dil = DSW_GROUPS[g][1]
    if dil == 1:
        p3 = _dil(proj, 1)
        return (p3, p3, p3), (g, W_A // OUT_A + g, 2 * W_A // OUT_A + g)
    c0 = g * OUT_A
    return tuple(_dil(proj[:, o + c0:o + c0 + OUT_A], dil) for o in (0, W_A, 2 * W_A)), (0, 0, 0)


_COL_SHARDED = ("w_in", "w_br", "w_up")
_ROW_SHARDED = ("w_o", "w_down")


class _Plan:
    def __init__(self):
        self.riders = {}
        self.landed = {}

    def ride(self, slot, key, kind, x):
        self.riders.setdefault(slot, []).append((key, kind, x))

    def run(self, slot, fn, *args, **kw):
        items = self.riders.pop(slot, [])
        res = fn(*args, comm=[(kind, x) for _, kind, x in items], **kw)
        for (key, _, _), r in zip(items, res[-1]):
            self.landed[key] = r
        return res[0] if len(res) == 2 else res[:-1]

    def weight(self, n, l):
        g = self.landed[(n, l)]
        if n in _COL_SHARDED:
            return g.transpose(1, 0, 2).reshape(g.shape[1], -1)
        return g.reshape(-1, g.shape[2])

    def scatter(self, slot, n, l, full):
        K, N = full.shape
        if n in _COL_SHARDED:
            blocks = full.reshape(K, N_DEV, N // N_DEV).transpose(1, 0, 2)
        else:
            blocks = full.reshape(N_DEV, K // N_DEV, N)
        self.ride(slot, ("d" + n, l), "scatter", blocks)


def _layer_fwd(x, p, plan, l):
    h = _rms_fwd(x, p["norm1"][l], f"rms1_fwd_{l}")
    proj = plan.run(f"proj_{l}", _matmul, h, plan.weight("w_in", l), mode="nn", out_dtype=BF, name=f"proj_{l}",
                    tm=512, tn=1024, tk=1024)
    o_g, l_g, qkv_g = [], [], []
    for g in range(N_GROUPS):
        qkv, cols = _group_qkv(proj, g)
        og, lg = plan.run(f"attn_a_fwd_g{g}_{l}", _attn_a_fwd, qkv, cols, g)
        o_g.append(_undil(og))
        l_g.append(_undil(lg))
        qkv_g.append((qkv, cols))
    ob = plan.run(f"sb_fwd_{l}", _sb_fwd, proj)
    merged, oa, lse = _merge_fwd(o_g, l_g, ob, proj, p["b_gate"][l], plan.weight("w_br", l))
    x1 = plan.run(f"wo_{l}", _matmul, merged, plan.weight("w_o", l), mode="nn", out_dtype=F32, name=f"wo_{l}",
                  tm=512, tn=1024, tk=1024, res=x)
    h2 = _rms_fwd(x1, p["norm2"][l], f"rms2_fwd_{l}")
    up = plan.run(f"up_{l}", _matmul, h2, plan.weight("w_up", l), mode="nn", out_dtype=BF, name=f"up_{l}",
                  tm=512, tn=1408, tk=1024)
    act = _ffn_act_fwd(up, p["conv_w"][l], p["conv_b"][l])
    x2 = plan.run(f"down_{l}", _matmul, act, plan.weight("w_down", l), mode="nn", out_dtype=F32, name=f"down_{l}",
                  tm=512, tn=1024, tk=D_FF, res=x1)
    saved = dict(x=x, h=h, proj=proj, qkv_g=qkv_g, oa=oa, ob=ob, lse=lse, merged=merged, x1=x1, h2=h2, up=up, act=act)
    return x2, saved


def _layer_bwd(dx2, sv, p, plan, l):
    gr = {}
    dact = plan.run(f"d_act_{l}", _matmul, dx2, plan.weight("w_down", l), mode="nt", out_dtype=BF, name=f"d_act_{l}",
                    tm=512, tn=1408, tk=1024)
    dwd = plan.run(f"dw_down_{l}", _matmul, sv["act"], dx2, mode="tn", out_dtype=BF, name=f"dw_down_{l}",
                   tm=1408, tn=1024, tk=512)
    plan.scatter(f"d_h2_{l}", "w_down", l, dwd)
    dac, dv, gr["conv_w"], dcb = plan.run(f"ffn_act_bwd1_{l}", _ffn_act_bwd1, dact, sv["up"], p["conv_w"][l],
                                          p["conv_b"][l])
    gr["conv_b"] = dcb[0]
    dup = _ffn_act_bwd2(dac, dv, p["conv_w"][l])
    dh2 = plan.run(f"d_h2_{l}", _matmul, dup, plan.weight("w_up", l), mode="nt", out_dtype=BF, name=f"d_h2_{l}",
                   tm=512, tn=1024, tk=1408)
    dwu = plan.run(f"dw_up_{l}", _matmul, sv["h2"], dup, mode="tn", out_dtype=BF, name=f"dw_up_{l}",
                   tm=1024, tn=1408, tk=512)
    plan.scatter(f"sb_bwd_{l}", "w_up", l, dwu)
    dx1, dn2 = _rms_bwd(sv["x1"], p["norm2"][l], dh2, dx2, f"rms2_bwd_{l}")
    gr["norm2"] = dn2[0]
    dm = plan.run(f"d_merged_{l}", _matmul, dx1, plan.weight("w_o", l), mode="nt", out_dtype=BF,
                  name=f"d_merged_{l}", tm=512, tn=1024, tk=1024)
    dwo = plan.run(f"dw_o_{l}", _matmul, sv["merged"], dx1, mode="tn", out_dtype=BF, name=f"dw_o_{l}",
                   tm=1024, tn=1024, tk=512)
    plan.scatter(f"d_h_{l}", "w_o", l, dwo)
    dya, dyb, doa, dob, dsum, dgate, dbg = _merge_bwd(dm, sv["oa"], sv["ob"], sv["proj"], p["b_gate"][l],
                                                      plan.weight("w_br", l))
    gr["b_gate"] = dbg[0]
    dwa = plan.run(f"dw_bra_{l}", _matmul, sv["oa"], dya, mode="tn", out_dtype=BF, name=f"dw_bra_{l}",
                   tm=256, tn=1024, tk=512)
    dwb = plan.run(f"dw_brb_{l}", _matmul, sv["ob"], dyb, mode="tn", out_dtype=BF, name=f"dw_brb_{l}",
                   tm=256, tn=1024, tk=512)
    plan.scatter(f"d_h_{l}", "w_br", l, jnp.concatenate([dwa, dwb], axis=0))
    proj = sv["proj"]
    dq_a, dk_a, dv_a = [], [], []
    for g, (_, dil) in enumerate(DSW_GROUPS):
        qkv, cols = sv["qkv_g"][g]
        dqg, dkg, dvg = _attn_a_bwd(qkv, cols, _dil(doa, dil), _dil(sv["lse"], dil), _dil(dsum, dil), g)
        dq_a.append(_undil(dqg))
        dk_a.append(_undil(dkg))
        dv_a.append(_undil(dvg))
    dqb, dkb, dvb = plan.run(f"sb_bwd_{l}", _sb_bwd, proj, dob, sv["ob"])
    dproj = jnp.concatenate(dq_a + dk_a + dv_a + [dqb.astype(BF), dkb.astype(BF), dvb.astype(BF), dgate], axis=1)
    dh = plan.run(f"d_h_{l}", _matmul, dproj, plan.weight("w_in", l), mode="nt", out_dtype=BF, name=f"d_h_{l}",
                  tm=512, tn=1024, tk=1280)
    dwi = plan.run(f"dw_in_{l}", _matmul, sv["h"], dproj, mode="tn", out_dtype=BF, name=f"dw_in_{l}",
                   tm=1024, tn=1280, tk=512)
    plan.scatter(f"ffn_act_bwd1_{l - 1}", "w_in", l, dwi)
    dx, dn1 = _rms_bwd(sv["x"], p["norm1"][l], dh, dx1, f"rms1_bwd_{l}")
    gr["norm1"] = dn1[0]
    return dx, gr


def kernel(x, norm1, w_in, b_gate, w_br, w_o, norm2, w_up, conv_w, conv_b, w_down, norm_f, loss_target, m_norm1, m_w_in, m_b_gate, m_w_br, m_w_o, m_norm2, m_w_up, m_conv_w, m_conv_b, m_w_down, m_norm_f, v_norm1, v_w_in, v_b_gate, v_w_br, v_w_o, v_norm2, v_w_up, v_conv_w, v_conv_b, v_w_down, v_norm_f):
    depth = norm1.shape[0]
    me = 4 * lax.axis_index("x") + 2 * lax.axis_index("y") + lax.axis_index("c")
    shards = dict(w_in=w_in, w_br=w_br, w_o=w_o, w_up=w_up, w_down=w_down)
    moments_m = dict(norm1=m_norm1, w_in=m_w_in, b_gate=m_b_gate, w_br=m_w_br, w_o=m_w_o, norm2=m_norm2,
                     w_up=m_w_up, conv_w=m_conv_w, conv_b=m_conv_b, w_down=m_w_down, norm_f=m_norm_f)
    moments_v = dict(norm1=v_norm1, w_in=v_w_in, b_gate=v_b_gate, w_br=v_w_br, w_o=v_w_o, norm2=v_norm2,
                     w_up=v_w_up, conv_w=v_conv_w, conv_b=v_conv_b, w_down=v_w_down, norm_f=v_norm_f)

    plan = _Plan()
    wb = {n: s.astype(BF) for n, s in shards.items()}
    p = dict(norm1=norm1, b_gate=b_gate, norm2=norm2, conv_b=conv_b)
    cw = _all_gather(conv_w, "gather_conv_w")
    p["conv_w"] = cw.transpose(1, 2, 0, 3).reshape(depth, 3, D_FF)
    plan.landed[("w_in", 0)] = _all_gather(wb["w_in"][0], "gather_w_in_0")
    for l in range(depth):
        plan.ride(f"proj_{l}", ("w_up", l), "gather", wb["w_up"][l])
        plan.ride(f"attn_a_fwd_g0_{l}" if l == 0 else f"down_{l - 1}", ("w_br", l), "gather", wb["w_br"][l])
        plan.ride(f"attn_a_fwd_g0_{l}" if l == 0 else f"down_{l - 1}", ("w_o", l), "gather", wb["w_o"][l])
        plan.ride(f"sb_fwd_{l}", ("w_down", l), "gather", wb["w_down"][l])
        if l + 1 < depth:
            plan.ride(f"up_{l}", ("w_in", l + 1), "gather", wb["w_in"][l + 1])

    xs = x[0]
    saved = []
    for l in range(depth):
        xs, sv = _layer_fwd(xs, p, plan, l)
        saved.append(sv)
    loss_part, dx, dnf = _loss_head(xs, norm_f, loss_target[0])
    loss = lax.psum(loss_part[0, 0], ("x", "y", "c"))

    grads = [None] * depth
    for l in reversed(range(depth)):
        dx, grads[l] = _layer_bwd(dx, saved[l], p, plan, l)
    grad_x = dx[None]
    (_, _, last), = plan.riders.pop("ffn_act_bwd1_-1")
    plan.landed[("dw_in", 0)] = _all_to_all(last, "scatter_w_in_0")
    assert not plan.riders, sorted(plan.riders)

    out_g, out_d, out_m, out_v = {}, {}, {}, {}
    for n in _COL_SHARDED + _ROW_SHARDED:
        parts = jnp.concatenate([plan.landed[("d" + n, l)] for l in range(depth)], axis=1)
        shp = shards[n].shape
        flat = (shp[0] * shp[1], shp[2])
        res = _adamw(parts, shards[n].reshape(flat), moments_m[n].reshape(flat), moments_v[n].reshape(flat),
                     f"adamw_{n}")
        out_g[n], out_d[n], out_m[n], out_v[n] = [r.reshape(shp) for r in res]

    small = ("norm1", "b_gate", "norm2", "conv_b")
    vecs = [jnp.stack([grads[l][n] for l in range(depth)]).reshape(-1) for n in small]
    vecs.append(dnf.reshape(-1))
    vecs.append(jnp.stack([grads[l]["conv_w"] for l in range(depth)]).reshape(-1))
    sizes = [v.shape[0] for v in vecs]
    flat = jnp.concatenate(vecs)
    n_small = sum(sizes[:-1])
    pad = (-flat.shape[0]) % 1024
    flat = jnp.pad(flat, (0, pad)).reshape(-1, 128)
    allp = _all_gather(flat, "gather_small_grads").reshape(N_DEV, -1)
    rep_w = jnp.concatenate([norm1.reshape(-1), b_gate.reshape(-1), norm2.reshape(-1), conv_b.reshape(-1), norm_f])
    rep_m = jnp.concatenate([moments_m[n].reshape(-1) for n in small] + [m_norm_f])
    rep_v = jnp.concatenate([moments_v[n].reshape(-1) for n in small] + [v_norm_f])
    rows = n_small // 128
    res = _adamw(allp[:, :n_small].reshape(N_DEV, rows, 128), rep_w.reshape(rows, 128), rep_m.reshape(rows, 128),
                 rep_v.reshape(rows, 128), "adamw_small")
    off = 0
    for n, sz in zip(small + ("norm_f",), sizes[:-1]):
        shp = norm_f.shape if n == "norm_f" else p[n].shape
        out_g[n], out_d[n], out_m[n], out_v[n] = [r.reshape(-1)[off:off + sz].reshape(shp) for r in res]
        off += sz
    f = conv_w.shape[2]
    cwp = allp[:, n_small:n_small + sizes[-1]].reshape(N_DEV, depth * 3, D_FF)
    cwp = lax.dynamic_slice_in_dim(cwp, me * f, f, axis=2)
    res = _adamw(cwp, conv_w.reshape(depth * 3, f), m_conv_w.reshape(depth * 3, f), v_conv_w.reshape(depth * 3, f),
                 "adamw_conv_w")
    out_g["conv_w"], out_d["conv_w"], out_m["conv_w"], out_v["conv_w"] = [r.reshape(conv_w.shape) for r in res]

    order = ("norm1", "w_in", "b_gate", "w_br", "w_o", "norm2", "w_up", "conv_w", "conv_b", "w_down", "norm_f")
    return (loss, grad_x, *[out_g[n] for n in order], *[out_d[n] for n in order],
            *[out_m[n] for n in order], *[out_v[n] for n in order])
```

```python
import functools

import jax
import jax.numpy as jnp
from jax import lax
from jax.experimental import pallas as pl
from jax.experimental.pallas import tpu as pltpu

BF = jnp.bfloat16
F32 = jnp.float32

N_DEV = 8
D_MODEL = 1024
HEAD_DIM = 64
DSW_GROUPS = ((128, 1), (512, 4), (2048, 16))
HEADS_PER_GROUP = 4
N_GROUPS = len(DSW_GROUPS)
DSW_HEADS = HEADS_PER_GROUP * N_GROUPS
SB_HEADS = 4
W_A = DSW_HEADS * HEAD_DIM
W_B = SB_HEADS * HEAD_DIM
OUT_A = HEADS_PER_GROUP * HEAD_DIM
N_IN = 3 * W_A + 3 * W_B + 2 * D_MODEL
GATE_OFF = 3 * W_A + 3 * W_B
D_FF = 2816
SB_BLOCK = 256
RMS_EPS = 1e-6
ATT_SCALE = HEAD_DIM ** -0.5
NEG = -1e30
SB_EXIT = -110.0

ADAM_LR = 0.001
ADAM_B1 = 0.9
ADAM_B2 = 0.999
ADAM_EPS = 1e-08
ADAM_WD = 0.01
ADAM_STEP = 10

HBM_SPEC = pl.BlockSpec(memory_space=pltpu.HBM)
MESH = pl.DeviceIdType.MESH

_NN = (((1,), (0,)), ((), ()))
_NT = (((1,), (1,)), ((), ()))
_TN = (((0,), (0,)), ((), ()))


def _dot(a, b, dn=_NN):
    return lax.dot_general(a, b, dn, preferred_element_type=F32)


def _pick(dim, pref):
    if dim <= pref:
        return dim
    t = (pref // 128) * 128
    while t >= 128:
        if dim % t == 0:
            return t
        t -= 128
    return dim


def _params(*sem):
    return pltpu.CompilerParams(dimension_semantics=sem)


def _peer(k):
    x, y, c = lax.axis_index("x"), lax.axis_index("y"), lax.axis_index("c")
    px = 1 - x if (k >> 2) & 1 else x
    py = 1 - y if (k >> 1) & 1 else y
    pc = 1 - c if k & 1 else c
    return (px, py, pc), 4 * px + 2 * py + pc


def _exchange(kind, x_ref, out_ref, send_sems, recv_sems, local_sem):
    gather = kind == "gather"
    _, me = _peer(0)

    def src(idx):
        return x_ref if gather else x_ref.at[idx]

    def copy(k, dst_idx):
        peer, pidx = _peer(k)
        return pltpu.make_async_remote_copy(
            src_ref=src(pidx), dst_ref=out_ref.at[dst_idx], send_sem=send_sems.at[k - 1],
            recv_sem=recv_sems.at[k - 1], device_id=peer, device_id_type=MESH)

    mine = pltpu.make_async_copy(src(me), out_ref.at[me], local_sem)

    def start():
        mine.start()
        for k in range(1, N_DEV):
            copy(k, me).start()

    def wait():
        for k in range(1, N_DEV):
            copy(k, _peer(k)[1]).wait_recv()
        for k in range(1, N_DEV):
            copy(k, me).wait_send()
        mine.wait()

    return start, wait


_EXCHANGE_SEMS = [pltpu.SemaphoreType.DMA((N_DEV - 1,)), pltpu.SemaphoreType.DMA((N_DEV - 1,)),
                  pltpu.SemaphoreType.DMA]


def _exchange_shape(kind, x):
    return jax.ShapeDtypeStruct(((N_DEV,) + x.shape) if kind == "gather" else x.shape, x.dtype)


def _exchange_alone(kind, x, name):
    def body(x_ref, out_ref, send_sems, recv_sems, local_sem):
        start, wait = _exchange(kind, x_ref, out_ref, send_sems, recv_sems, local_sem)
        start()
        wait()

    return pl.pallas_call(
        body, name=name, out_shape=_exchange_shape(kind, x),
        in_specs=[HBM_SPEC], out_specs=HBM_SPEC, scratch_shapes=list(_EXCHANGE_SEMS),
    )(x)


def _all_gather(x, name):
    return _exchange_alone("gather", x, name)


def _all_to_all(x, name):
    return _exchange_alone("scatter", x, name)


def _call(body, *, name, grid, in_specs, out_specs, out_shape, args, scratch_shapes=(), sem=None, comm=()):
    single = not isinstance(out_shape, (tuple, list))
    outs = (out_shape,) if single else tuple(out_shape)
    ospecs = (out_specs,) if single else tuple(out_specs)
    if not comm:
        res = pl.pallas_call(
            body, name=name, out_shape=outs, grid=grid, in_specs=list(in_specs), out_specs=ospecs,
            scratch_shapes=list(scratch_shapes), compiler_params=_params(*(sem or ("arbitrary",) * len(grid))),
        )(*args)
        return res
    n_in, n_out, n_scr, nc = len(in_specs), len(outs), len(scratch_shapes), len(comm)

    def wrapped(*refs):
        ins = refs[:n_in]
        cins = refs[n_in:n_in + nc]
        o0 = n_in + nc
        kouts = refs[o0:o0 + n_out]
        couts = refs[o0 + n_out:o0 + n_out + nc]
        s0 = o0 + n_out + nc
        scr = refs[s0:s0 + n_scr]
        sems = refs[s0 + n_scr:]
        ids = [pl.program_id(ax) for ax in range(len(grid))]
        first = functools.reduce(jnp.logical_and, [i == 0 for i in ids])
        last = functools.reduce(jnp.logical_and, [i == g - 1 for i, g in zip(ids, grid)])
        ex = [_exchange(comm[c][0], cins[c], couts[c], *sems[3 * c:3 * c + 3]) for c in range(nc)]

        @pl.when(first)
        def _():
            for start, _ in ex:
                start()

        body(*ins, *kouts, *scr)

        @pl.when(last)
        def _():
            for _, wait in ex:
                wait()

    return pl.pallas_call(
        wrapped, name=name,
        out_shape=outs + tuple(_exchange_shape(k, x) for k, x in comm),
        grid=grid, in_specs=list(in_specs) + [HBM_SPEC] * nc, out_specs=ospecs + (HBM_SPEC,) * nc,
        scratch_shapes=list(scratch_shapes) + list(_EXCHANGE_SEMS) * nc,
        compiler_params=_params(*(("arbitrary",) * len(grid))),
    )(*args, *[x for _, x in comm])


def _matmul(a, b, *, mode, out_dtype, name, tm=512, tn=1024, tk=1024, res=None, comm=()):
    if mode == "nn":
        (M, K), (_, N) = a.shape, b.shape
    elif mode == "nt":
        (M, K), (N, _) = a.shape, b.shape
    else:
        (K, M), (_, N) = a.shape, b.shape
    tm, tn, tk = _pick(M, tm), _pick(N, tn), _pick(K, tk)
    nk = K // tk
    dn = {"nn": _NN, "nt": _NT, "tn": _TN}[mode]

    def body(*refs):
        a_ref, b_ref = refs[0], refs[1]
        r_ref = refs[2] if res is not None else None
        o_ref = refs[3] if res is not None else refs[2]

        def finish(r):
            if res is not None:
                r = r + r_ref[...].astype(F32)
            o_ref[...] = r.astype(out_dtype)

        part = _dot(a_ref[...].astype(BF), b_ref[...].astype(BF), dn)
        if nk == 1:
            finish(part)
            return
        acc = refs[-1]
        k = pl.program_id(2)

        @pl.when(k == 0)
        def _():
            acc[...] = jnp.zeros_like(acc)

        acc[...] += part

        @pl.when(k == nk - 1)
        def _():
            finish(acc[...])

    if mode == "tn":
        a_spec = pl.BlockSpec((tk, tm), lambda j, i, k: (k, i))
    else:
        a_spec = pl.BlockSpec((tm, tk), lambda j, i, k: (i, k))
    if mode == "nt":
        b_spec = pl.BlockSpec((tn, tk), lambda j, i, k: (j, k))
    else:
        b_spec = pl.BlockSpec((tk, tn), lambda j, i, k: (k, j))
    o_spec = pl.BlockSpec((tm, tn), lambda j, i, k: (i, j))
    in_specs = [a_spec, b_spec] + ([o_spec] if res is not None else [])
    args = (a, b) + ((res,) if res is not None else ())
    out = _call(body, name=name, grid=(N // tn, M // tm, nk), in_specs=in_specs, out_specs=o_spec,
                out_shape=jax.ShapeDtypeStruct((M, N), out_dtype), args=args,
                scratch_shapes=[pltpu.VMEM((tm, tn), F32)] if nk > 1 else [],
                sem=("parallel", "parallel", "arbitrary"), comm=comm)
    return out[0], list(out[1:])


def _rms_fwd(x, g, name):
    S, D = x.shape
    tm = _pick(S, 512)

    def body(x_ref, g_ref, h_ref):
        xf = x_ref[...]
        r = lax.rsqrt(jnp.mean(xf * xf, axis=-1, keepdims=True) + RMS_EPS)
        h_ref[...] = (xf * r * g_ref[...]).astype(BF)

    return pl.pallas_call(
        body, name=name,
        out_shape=jax.ShapeDtypeStruct((S, D), BF),
        grid=(S // tm,),
        in_specs=[pl.BlockSpec((tm, D), lambda i: (i, 0)), pl.BlockSpec((1, D), lambda i: (0, 0))],
        out_specs=pl.BlockSpec((tm, D), lambda i: (i, 0)),
        compiler_params=_params("parallel"),
    )(x, g.reshape(1, D))


def _rms_bwd(x, g, dh, dres, name):
    S, D = x.shape
    tm = _pick(S, 512)

    def body(x_ref, g_ref, dh_ref, dres_ref, dx_ref, dg_ref):
        @pl.when(pl.program_id(0) == 0)
        def _():
            dg_ref[...] = jnp.zeros_like(dg_ref)

        xf = x_ref[...]
        r = lax.rsqrt(jnp.mean(xf * xf, axis=-1, keepdims=True) + RMS_EPS)
        xh = xf * r
        dy = dh_ref[...].astype(F32)
        dg_ref[...] += jnp.sum(dy * xh, axis=0, keepdims=True)
        dxh = dy * g_ref[...]
        dx = r * (dxh - xh * jnp.mean(dxh * xh, axis=-1, keepdims=True))
        dx_ref[...] = dres_ref[...] + dx

    row = pl.BlockSpec((tm, D), lambda i: (i, 0))
    vec = pl.BlockSpec((1, D), lambda i: (0, 0))
    return pl.pallas_call(
        body, name=name,
        out_shape=(jax.ShapeDtypeStruct((S, D), F32), jax.ShapeDtypeStruct((1, D), F32)),
        grid=(S // tm,),
        in_specs=[row, vec, row, row], out_specs=(row, vec),
        compiler_params=_params("arbitrary"),
    )(x, g.reshape(1, D), dh, dres)


def _loss_head(x, g, target):
    S, D = x.shape
    tm = _pick(S, 512)

    def body(x_ref, g_ref, t_ref, loss_ref, dx_ref, dg_ref):
        @pl.when(pl.program_id(0) == 0)
        def _():
            dg_ref[...] = jnp.zeros_like(dg_ref)
            loss_ref[...] = jnp.zeros_like(loss_ref)

        xf = x_ref[...]
        gg = g_ref[...]
        r = lax.rsqrt(jnp.mean(xf * xf, axis=-1, keepdims=True) + RMS_EPS)
        xh = xf * r
        err = xh * gg - t_ref[...]
        per_tok = jnp.mean(err * err, axis=-1, keepdims=True)
        loss_ref[...] += 0.5 * jnp.sum(per_tok, axis=0, keepdims=True)
        dy = err * (1.0 / D)
        dg_ref[...] += jnp.sum(dy * xh, axis=0, keepdims=True)
        dxh = dy * gg
        dx_ref[...] = r * (dxh - xh * jnp.mean(dxh * xh, axis=-1, keepdims=True))

    row = pl.BlockSpec((tm, D), lambda i: (i, 0))
    vec = pl.BlockSpec((1, D), lambda i: (0, 0))
    one = pl.BlockSpec((1, 1), lambda i: (0, 0))
    return pl.pallas_call(
        body, name="loss_head",
        out_shape=(jax.ShapeDtypeStruct((1, 1), F32), jax.ShapeDtypeStruct((S, D), F32),
                   jax.ShapeDtypeStruct((1, D), F32)),
        grid=(S // tm,),
        in_specs=[row, vec, row], out_specs=(one, row, vec),
        compiler_params=_params("arbitrary"),
    )(x, g.reshape(1, D), target)


def _slopes(g):
    return [2.0 ** (-8.0 * (HEADS_PER_GROUP * g + j + 1) / DSW_HEADS) for j in range(HEADS_PER_GROUP)]


def _band_masks(W):
    row = lax.broadcasted_iota(jnp.int32, (W, W), 0)
    col = lax.broadcasted_iota(jnp.int32, (W, W), 1)
    d_cur = row - col
    d_prev = d_cur + W
    return d_cur, d_prev, d_cur >= 0, d_cur <= 0


def _band_specs(W, nb, per):
    def cur(c):
        return pl.BlockSpec((None, per * W, OUT_A), lambda r, n: (r, n, c))

    def prev(c):
        return pl.BlockSpec((None, W, OUT_A), lambda r, n: (r, jnp.maximum(per * n - 1, 0), c))

    def nxt(c):
        return pl.BlockSpec((None, W, OUT_A), lambda r, n: (r, jnp.minimum(per * (n + 1), nb - 1), c))

    return cur, prev, nxt


def _blocks_per_step(nb):
    return 2 if nb % 2 == 0 else 1


def _attn_a_fwd(qkv, cols, g, comm=()):
    win, dil = DSW_GROUPS[g]
    W = win // dil
    d, L, _ = qkv[0].shape
    nb = L // W
    per = _blocks_per_step(nb)
    H, hd = HEADS_PER_GROUP, HEAD_DIM
    slopes = _slopes(g)

    def body(q_ref, kp_ref, kc_ref, vp_ref, vc_ref, o_ref, l_ref):
        n = pl.program_id(1)
        d_cur, d_prev, m_cur, m_prev = _band_masks(W)
        m_first = jnp.logical_and(m_prev, n > 0)
        for b in range(per):
            rows = slice(b * W, (b + 1) * W)
            before = slice((b - 1) * W, b * W)
            for j in range(H):
                ln = slice(j * hd, (j + 1) * hd)
                sl = slopes[j] * dil
                qh = q_ref[rows, ln]
                kp, vp = (kp_ref[:, ln], vp_ref[:, ln]) if b == 0 else (kc_ref[before, ln], vc_ref[before, ln])
                s_c = _dot(qh, kc_ref[rows, ln], _NT) * ATT_SCALE - sl * d_cur.astype(F32)
                s_p = _dot(qh, kp, _NT) * ATT_SCALE - sl * d_prev.astype(F32)
                s_c = jnp.where(m_cur, s_c, NEG)
                s_p = jnp.where(m_first if b == 0 else m_prev, s_p, NEG)
                m = jnp.maximum(jnp.max(s_c, axis=1, keepdims=True), jnp.max(s_p, axis=1, keepdims=True))
                p_c = jnp.exp(s_c - m)
                p_p = jnp.exp(s_p - m)
                den = jnp.sum(p_c, axis=1, keepdims=True) + jnp.sum(p_p, axis=1, keepdims=True)
                o = _dot(p_c.astype(BF), vc_ref[rows, ln]) + _dot(p_p.astype(BF), vp)
                o_ref[rows, ln] = o / den
                l_ref[rows, ln] = jnp.broadcast_to(m + jnp.log(den), (W, hd))

    cur, prev, _ = _band_specs(W, nb, per)
    out = jax.ShapeDtypeStruct((d, L, OUT_A), F32)
    res = _call(body, name=f"attn_a_fwd_g{g}", grid=(d, nb // per),
                in_specs=[cur(cols[0]), prev(cols[1]), cur(cols[1]), prev(cols[2]), cur(cols[2])],
                out_specs=(cur(0), cur(0)), out_shape=(out, out),
                args=(qkv[0], qkv[1], qkv[1], qkv[2], qkv[2]), sem=("parallel", "parallel"), comm=comm)
    return res[0], res[1], list(res[2:])


def _attn_a_bwd(qkv, cols, do, lse, dsum, g):
    win, dil = DSW_GROUPS[g]
    W = win // dil
    d, L, _ = qkv[0].shape
    nb = L // W
    per = _blocks_per_step(nb)
    nsteps = nb // per
    H, hd = HEADS_PER_GROUP, HEAD_DIM
    slopes = _slopes(g)

    def body(q_ref, qn_ref, kp_ref, kc_ref, vp_ref, vc_ref, do_ref, don_ref, l_ref, ln_ref,
             ds_ref, dsn_ref, dq_ref, dk_ref, dv_ref):
        n = pl.program_id(1)
        d_cur, d_prev, m_cur, m_prev = _band_masks(W)
        m_first = jnp.logical_and(m_prev, n > 0)
        m_last = jnp.logical_and(m_prev, n < nsteps - 1)
        for b in range(per):
            rows = slice(b * W, (b + 1) * W)
            before = slice((b - 1) * W, b * W)
            after = slice((b + 1) * W, (b + 2) * W)
            first, last = b == 0, b == per - 1
            for j in range(H):
                ln = slice(j * hd, (j + 1) * hd)
                c0 = slice(j * hd, j * hd + 1)
                sl = slopes[j] * dil
                b_cur = sl * d_cur.astype(F32)
                b_prev = sl * d_prev.astype(F32)
                qh = q_ref[rows, ln]
                qn = qn_ref[:, ln] if last else q_ref[after, ln]
                kc, vc = kc_ref[rows, ln], vc_ref[rows, ln]
                kp, vp = (kp_ref[:, ln], vp_ref[:, ln]) if first else (kc_ref[before, ln], vc_ref[before, ln])
                dob = do_ref[rows, ln].astype(BF)
                donb = (don_ref[:, ln] if last else do_ref[after, ln]).astype(BF)
                lse_c, dsum_c = l_ref[rows, c0], ds_ref[rows, c0]
                lse_n = ln_ref[:, c0] if last else l_ref[after, c0]
                dsum_n = dsn_ref[:, c0] if last else ds_ref[after, c0]
                m_p = m_first if first else m_prev
                m_n = m_last if last else m_prev
                p_cc = jnp.exp(jnp.where(m_cur, _dot(qh, kc, _NT) * ATT_SCALE - b_cur, NEG) - lse_c)
                p_cp = jnp.exp(jnp.where(m_p, _dot(qh, kp, _NT) * ATT_SCALE - b_prev, NEG) - lse_c)
                p_nc = jnp.exp(jnp.where(m_n, _dot(qn, kc, _NT) * ATT_SCALE - b_prev, NEG) - lse_n)
                ds_cc = (p_cc * (_dot(dob, vc, _NT) - dsum_c) * ATT_SCALE).astype(BF)
                ds_cp = (p_cp * (_dot(dob, vp, _NT) - dsum_c) * ATT_SCALE).astype(BF)
                ds_nc = (p_nc * (_dot(donb, vc, _NT) - dsum_n) * ATT_SCALE).astype(BF)
                dq_ref[rows, ln] = (_dot(ds_cc, kc) + _dot(ds_cp, kp)).astype(BF)
                dk_ref[rows, ln] = (_dot(ds_cc, qh, _TN) + _dot(ds_nc, qn, _TN)).astype(BF)
                dv_ref[rows, ln] = (_dot(p_cc.astype(BF), dob, _TN) + _dot(p_nc.astype(BF), donb, _TN)).astype(BF)

    cur, prev, nxt = _band_specs(W, nb, per)
    out = jax.ShapeDtypeStruct((d, L, OUT_A), BF)
    cq, ck, cv = cols
    return pl.pallas_call(
        body, name=f"attn_a_bwd_g{g}",
        out_shape=(out, out, out),
        grid=(d, nsteps),
        in_specs=[cur(cq), nxt(cq), prev(ck), cur(ck), prev(cv), cur(cv),
                  cur(0), nxt(0), cur(0), nxt(0), cur(0), nxt(0)],
        out_specs=(cur(0), cur(0), cur(0)),
        compiler_params=_params("parallel", "parallel"),
    )(qkv[0], qkv[0], qkv[1], qkv[1], qkv[2], qkv[2], do, do, lse, lse, dsum, dsum)


SB_PAIR = 2
SB_QROWS = SB_BLOCK


def _softplus_parts(z):
    e = jnp.exp(-jnp.abs(z))
    log1p_e = jnp.where(e < 1e-3, e * (1.0 - e * (0.5 - e * (1.0 / 3.0))), jnp.log(1.0 + e))
    return e, jnp.maximum(z, 0.0) + log1p_e


def _split_dot(x, t):
    hi = x.astype(BF)
    lo = (x - hi.astype(F32)).astype(BF)
    return _dot(hi, t) + _dot(lo, t)


def _sb_block(qh, kk, causal, r_run, tri_incl):
    z = _dot(qh, kk, _NT) * ATT_SCALE
    e, sp = _softplus_parts(z)
    ls = jnp.where(causal, -sp, 0.0)
    cin = _split_dot(ls, tri_incl)
    a = jnp.where(causal, jnp.exp(z + cin + r_run), 0.0)
    return z, e, cin, a


def _sb_specs(S):
    Q, hd = SB_QROWS, HEAD_DIM
    lanes = SB_PAIR * hd
    qc = (3 * W_A) // lanes
    kc = (3 * W_A + W_B) // lanes
    vc = (3 * W_A + 2 * W_B) // lanes
    q_spec = pl.BlockSpec((Q, lanes), lambda p, i: (i, qc + p))
    k_spec = pl.BlockSpec((S, lanes), lambda p, i: (0, kc + p))
    v_spec = pl.BlockSpec((S, lanes), lambda p, i: (0, vc + p))
    o_spec = pl.BlockSpec((Q, lanes), lambda p, i: (i, p))
    full = pl.BlockSpec((S, lanes), lambda p, i: (0, p))
    return q_spec, k_spec, v_spec, o_spec, full


def _sb_iotas(i):
    B, Q = SB_BLOCK, SB_QROWS
    row = lax.broadcasted_iota(jnp.int32, (Q, B), 0) + i * Q
    col = lax.broadcasted_iota(jnp.int32, (Q, B), 1)
    tr = lax.broadcasted_iota(jnp.int32, (B, B), 0)
    tc = lax.broadcasted_iota(jnp.int32, (B, B), 1)
    return row, col, tr, tc


def _sb_fwd(proj, comm=()):
    S = proj.shape[0]
    B, Q, hd = SB_BLOCK, SB_QROWS, HEAD_DIM
    nq = S // Q
    q_spec, k_spec, v_spec, o_spec, _ = _sb_specs(S)

    def body(q_ref, k_ref, v_ref, o_ref):
        i = pl.program_id(1)
        row, col, tr, tc = _sb_iotas(i)
        tri_incl = (tr >= tc).astype(BF)
        qs = [q_ref[:, h * hd:(h + 1) * hd] for h in range(SB_PAIR)]

        def cond(c):
            return jnp.logical_and(c[0] >= 0, c[-1] > SB_EXIT)

        def step(c):
            kb = c[0]
            off = pl.multiple_of(kb * B, B)
            causal = col + kb * B < row
            runs, accs = [], []
            for h in range(SB_PAIR):
                r_run, acc = c[1 + h], c[1 + SB_PAIR + h]
                kk = k_ref[pl.ds(off, B), h * hd:(h + 1) * hd]
                vv = v_ref[pl.ds(off, B), h * hd:(h + 1) * hd]
                _, _, cin, a = _sb_block(qs[h], kk, causal, r_run, tri_incl)
                accs.append(acc + _dot(a.astype(BF), vv))
                runs.append(r_run + cin[:, 0:1])
            rmax = functools.reduce(jnp.maximum, [jnp.max(r) for r in runs])
            return (kb - 1, *runs, *accs, rmax)

        init = (i,*[jnp.zeros((Q, 1), F32)] * SB_PAIR, *[jnp.zeros((Q, hd), F32)] * SB_PAIR,
                jnp.float32(0.0))
        fin = lax.while_loop(cond, step, init)
        for h in range(SB_PAIR):
            o_ref[:, h * hd:(h + 1) * hd] = fin[1 + SB_PAIR + h]

    res = _call(body, name="sb_fwd", grid=(SB_HEADS // SB_PAIR, nq), in_specs=[q_spec, k_spec, v_spec],
                out_specs=o_spec, out_shape=jax.ShapeDtypeStruct((S, W_B), F32), args=(proj, proj, proj),
                sem=("parallel", "parallel"), comm=comm)
    return res[0], list(res[1:])


def _sb_bwd(proj, do, o, comm=()):
    S = proj.shape[0]
    B, Q, hd = SB_BLOCK, SB_QROWS, HEAD_DIM
    nq = S // Q
    q_spec, k_spec, v_spec, o_spec, full = _sb_specs(S)
    P = SB_PAIR

    def body(q_ref, k_ref, v_ref, do_ref, o_ref, dq_ref, dk_ref, dv_ref):
        i = pl.program_id(1)

        @pl.when(i == 0)
        def _():
            dk_ref[...] = jnp.zeros_like(dk_ref)
            dv_ref[...] = jnp.zeros_like(dv_ref)

        row, col, tr, tc = _sb_iotas(i)
        tri_incl = (tr >= tc).astype(BF)
        tri_strict = (tr > tc).astype(BF)
        qs = [q_ref[:, h * hd:(h + 1) * hd] for h in range(P)]
        dobs = [do_ref[:, h * hd:(h + 1) * hd].astype(BF) for h in range(P)]
        dsums = [jnp.sum(dobs[h].astype(F32) * o_ref[:, h * hd:(h + 1) * hd], axis=1, keepdims=True)
                 for h in range(P)]

        def cond(c):
            return jnp.logical_and(c[0] >= 0, c[-1] > SB_EXIT)

        def step(c):
            kb = c[0]
            off = pl.multiple_of(kb * B, B)
            causal = col + kb * B < row
            runs, gruns, dqs = [], [], []
            for h in range(P):
                r_run, g_run, dq = c[1 + h], c[1 + P + h], c[1 + 2 * P + h]
                ln = slice(h * hd, (h + 1) * hd)
                kk = k_ref[pl.ds(off, B), ln]
                vv = v_ref[pl.ds(off, B), ln]
                z, e, cin, a = _sb_block(qs[h], kk, causal, r_run, tri_incl)
                a16 = a.astype(BF)
                gmat = a16.astype(F32) * _dot(dobs[h], vv, _NT)
                later = _split_dot(gmat, tri_strict)
                pfx = dsums[h] - g_run - later
                sig = jnp.where(z >= 0, 1.0, e) / (1.0 + e)
                dz = (jnp.where(causal, gmat - sig * pfx, 0.0) * ATT_SCALE).astype(BF)
                dqs.append(dq + _dot(dz, kk))
                dk_ref[pl.ds(off, B), ln] += _dot(dz, qs[h], _TN)
                dv_ref[pl.ds(off, B), ln] += _dot(a16, dobs[h], _TN)
                gruns.append(g_run + jnp.sum(gmat, axis=1, keepdims=True))
                runs.append(r_run + cin[:, 0:1])
            rmax = functools.reduce(jnp.maximum, [jnp.max(r) for r in runs])
            return (kb - 1, *runs, *gruns, *dqs, rmax)

        init = (i,*[jnp.zeros((Q, 1), F32)] * (2 * P), *[jnp.zeros((Q, hd), F32)] * P, jnp.float32(0.0))
        fin = lax.while_loop(cond, step, init)
        for h in range(P):
            dq_ref[:, h * hd:(h + 1) * hd] = fin[1 + 2 * P + h]

    out = jax.ShapeDtypeStruct((S, W_B), F32)
    res = _call(body, name="sb_bwd", grid=(SB_HEADS // P, nq), in_specs=[q_spec, k_spec, v_spec, o_spec, o_spec],
                out_specs=(o_spec, full, full), out_shape=(out, out, out), args=(proj, proj, proj, do, o), comm=comm)
    return res[0], res[1], res[2], list(res[3:])


def _merge_fwd(o_g, l_g, o_b, proj, b_gate, w_br):
    S = o_b.shape[0]
    D = D_MODEL
    tm = _pick(S, 256)
    gcol = GATE_OFF // D

    def body(o0, o1, o2, l0, l1, l2, ob_ref, ga_ref, gb_ref, bg_ref, w_ref, mg_ref, oa_ref, lse_ref):
        la, lb, lc = l0[...], l1[...], l2[...]
        mx = jnp.maximum(jnp.maximum(la, lb), lc)
        ea, eb, ec = jnp.exp(la - mx), jnp.exp(lb - mx), jnp.exp(lc - mx)
        den = ea + eb + ec
        oa = (ea * o0[...] + eb * o1[...] + ec * o2[...]) / den
        oa_ref[...] = oa
        lse_ref[...] = mx + jnp.log(den)
        ya = _dot(oa.astype(BF), w_ref[0:OUT_A, :])
        yb = _dot(ob_ref[...].astype(BF), w_ref[OUT_A:OUT_A + W_B, :])
        bg = bg_ref[...]
        g_a = jax.nn.sigmoid(ga_ref[...].astype(F32) + bg[:, 0:D])
        g_b = jax.nn.sigmoid(gb_ref[...].astype(F32) + bg[:, D:2 * D])
        mg_ref[...] = (g_a * ya + g_b * yb).astype(BF)

    nar = pl.BlockSpec((tm, OUT_A), lambda i: (i, 0))
    wide = pl.BlockSpec((tm, D), lambda i: (i, 0))
    return pl.pallas_call(
        body, name="merge_fwd",
        out_shape=(jax.ShapeDtypeStruct((S, D), BF), jax.ShapeDtypeStruct((S, OUT_A), F32),
                   jax.ShapeDtypeStruct((S, OUT_A), F32)),
        grid=(S // tm,),
        in_specs=[nar] * 7 + [pl.BlockSpec((tm, D), lambda i: (i, gcol)),
                              pl.BlockSpec((tm, D), lambda i: (i, gcol + 1)),
                              pl.BlockSpec((1, 2 * D), lambda i: (0, 0)),
                              pl.BlockSpec((OUT_A + W_B, D), lambda i: (0, 0))],
        out_specs=(wide, nar, nar),
        compiler_params=_params("parallel"),
    )(*o_g, *l_g, o_b, proj, proj, b_gate.reshape(1, 2 * D), w_br)


def _merge_bwd(dm, oa, ob, proj, b_gate, w_br):
    S = ob.shape[0]
    D = D_MODEL
    tm = _pick(S, 256)
    gcol = GATE_OFF // D

    def body(dm_ref, oa_ref, ob_ref, ga_ref, gb_ref, bg_ref, w_ref,
             dya_ref, dyb_ref, doa_ref, dob_ref, ds_ref, dg_ref, dbg_ref):
        @pl.when(pl.program_id(0) == 0)
        def _():
            dbg_ref[...] = jnp.zeros_like(dbg_ref)

        wa = w_ref[0:OUT_A, :]
        wb = w_ref[OUT_A:OUT_A + W_B, :]
        oa = oa_ref[...]
        ya = _dot(oa.astype(BF), wa)
        yb = _dot(ob_ref[...].astype(BF), wb)
        bg = bg_ref[...]
        g_a = jax.nn.sigmoid(ga_ref[...].astype(F32) + bg[:, 0:D])
        g_b = jax.nn.sigmoid(gb_ref[...].astype(F32) + bg[:, D:2 * D])
        dm = dm_ref[...].astype(F32)
        dga = dm * ya * g_a * (1.0 - g_a)
        dgb = dm * yb * g_b * (1.0 - g_b)
        dg_ref[:, 0:D] = dga.astype(BF)
        dg_ref[:, D:2 * D] = dgb.astype(BF)
        dbg_ref[:, 0:D] += jnp.sum(dga, axis=0, keepdims=True)
        dbg_ref[:, D:2 * D] += jnp.sum(dgb, axis=0, keepdims=True)
        dya = (dm * g_a).astype(BF)
        dyb = (dm * g_b).astype(BF)
        dya_ref[...] = dya
        dyb_ref[...] = dyb
        doa = _dot(dya, wa, _NT)
        doa_ref[...] = doa
        dob_ref[...] = _dot(dyb, wb, _NT)
        r = lax.broadcasted_iota(jnp.int32, (OUT_A, OUT_A), 0) // HEAD_DIM
        c = lax.broadcasted_iota(jnp.int32, (OUT_A, OUT_A), 1) // HEAD_DIM
        ds_ref[...] = _split_dot(doa * oa, (r == c).astype(BF))

    nar = pl.BlockSpec((tm, OUT_A), lambda i: (i, 0))
    wide = pl.BlockSpec((tm, D), lambda i: (i, 0))
    wide2 = pl.BlockSpec((tm, 2 * D), lambda i: (i, 0))
    vec2 = pl.BlockSpec((1, 2 * D), lambda i: (0, 0))
    return pl.pallas_call(
        body, name="merge_bwd",
        out_shape=(jax.ShapeDtypeStruct((S, D), BF), jax.ShapeDtypeStruct((S, D), BF),
                   jax.ShapeDtypeStruct((S, OUT_A), F32), jax.ShapeDtypeStruct((S, W_B), F32),
                   jax.ShapeDtypeStruct((S, OUT_A), F32), jax.ShapeDtypeStruct((S, 2 * D), BF),
                   jax.ShapeDtypeStruct((1, 2 * D), F32)),
        grid=(S // tm,),
        in_specs=[wide, nar, nar, pl.BlockSpec((tm, D), lambda i: (i, gcol)),
                  pl.BlockSpec((tm, D), lambda i: (i, gcol + 1)), vec2,
                  pl.BlockSpec((OUT_A + W_B, D), lambda i: (0, 0))],
        out_specs=(wide, wide, nar, nar, nar, wide2, vec2),
        compiler_params=_params("arbitrary"),
    )(dm, oa, ob, proj, proj, b_gate.reshape(1, 2 * D), w_br)


_SQRT_HALF = 0.7071067811865476
_INV_SQRT_2PI = 0.3989422804014327


def _gelu_parts(a):
    cdf = 0.5 * (1.0 + lax.erf(a * _SQRT_HALF))
    pdf = _INV_SQRT_2PI * jnp.exp(-0.5 * a * a)
    return cdf, pdf


def _shift_down(a, halo, k):
    rows = lax.broadcasted_iota(jnp.int32, a.shape, 0)
    out = pltpu.roll(a, k, 0)
    for r in range(k):
        out = jnp.where(rows == r, halo[8 - k + r:8 - k + r + 1, :], out)
    return out


def _shift_up(a, halo, k):
    n = a.shape[0]
    rows = lax.broadcasted_iota(jnp.int32, a.shape, 0)
    out = pltpu.roll(a, n - k, 0)
    for r in range(k):
        out = jnp.where(rows == n - k + r, halo[r:r + 1, :], out)
    return out


def _conv_in(a_ref, h_ref, first):
    a = a_ref[...].astype(F32)
    halo = jnp.where(first, 0.0, h_ref[...].astype(F32))
    return a, _shift_down(a, halo, 1), _shift_down(a, halo, 2)


def _ffn_specs(S, tm):
    F = D_FF
    t8 = tm // 8
    a_spec = pl.BlockSpec((tm, F), lambda i: (i, 0))
    v_spec = pl.BlockSpec((tm, F), lambda i: (i, 1))
    halo_prev = pl.BlockSpec((8, F), lambda i: (jnp.maximum(i * t8 - 1, 0), 0))
    return a_spec, v_spec, halo_prev


def _ffn_act_fwd(up, conv_w, conv_b):
    S = up.shape[0]
    F = D_FF
    tm = _pick(S, 256)
    a_spec, v_spec, halo_prev = _ffn_specs(S, tm)

    def body(a_ref, h_ref, v_ref, w_ref, b_ref, act_ref):
        a0, a1, a2 = _conv_in(a_ref, h_ref, pl.program_id(0) == 0)
        w = w_ref[...]
        ac = b_ref[...] + w[0:1, :] * a2 + w[1:2, :] * a1 + w[2:3, :] * a0
        cdf, _ = _gelu_parts(ac)
        act_ref[...] = (ac * cdf * v_ref[...].astype(F32)).astype(BF)

    return pl.pallas_call(
        body, name="ffn_act_fwd",
        out_shape=jax.ShapeDtypeStruct((S, F), BF),
        grid=(S // tm,),
        in_specs=[a_spec, halo_prev, v_spec, pl.BlockSpec((3, F), lambda i: (0, 0)),
                  pl.BlockSpec((1, F), lambda i: (0, 0))],
        out_specs=a_spec,
        compiler_params=_params("parallel"),
    )(up, up, up, conv_w, conv_b.reshape(1, F))


def _ffn_act_bwd1(dact, up, conv_w, conv_b, comm=()):
    S = up.shape[0]
    F = D_FF
    tm = _pick(S, 256)
    a_spec, v_spec, halo_prev = _ffn_specs(S, tm)

    def body(da_ref, a_ref, h_ref, v_ref, w_ref, b_ref, dac_ref, dv_ref, dw_ref, db_ref):
        @pl.when(pl.program_id(0) == 0)
        def _():
            dw_ref[...] = jnp.zeros_like(dw_ref)
            db_ref[...] = jnp.zeros_like(db_ref)

        a0, a1, a2 = _conv_in(a_ref, h_ref, pl.program_id(0) == 0)
        w = w_ref[...]
        ac = b_ref[...] + w[0:1, :] * a2 + w[1:2, :] * a1 + w[2:3, :] * a0
        cdf, pdf = _gelu_parts(ac)
        dact = da_ref[...].astype(F32)
        dv_ref[...] = (dact * ac * cdf).astype(BF)
        dac = dact * v_ref[...].astype(F32) * (cdf + ac * pdf)
        dac_ref[...] = dac.astype(BF)
        db_ref[...] += jnp.sum(dac, axis=0, keepdims=True)
        dw_ref[0:1, :] += jnp.sum(dac * a2, axis=0, keepdims=True)
        dw_ref[1:2, :] += jnp.sum(dac * a1, axis=0, keepdims=True)
        dw_ref[2:3, :] += jnp.sum(dac * a0, axis=0, keepdims=True)

    w_spec = pl.BlockSpec((3, F), lambda i: (0, 0))
    b_spec = pl.BlockSpec((1, F), lambda i: (0, 0))
    res = _call(body, name="ffn_act_bwd1", grid=(S // tm,),
                in_specs=[a_spec, a_spec, halo_prev, v_spec, w_spec, b_spec],
                out_specs=(a_spec, a_spec, w_spec, b_spec),
                out_shape=(jax.ShapeDtypeStruct((S, F), BF), jax.ShapeDtypeStruct((S, F), BF),
                           jax.ShapeDtypeStruct((3, F), F32), jax.ShapeDtypeStruct((1, F), F32)),
                args=(dact, up, up, up, conv_w, conv_b.reshape(1, F)), comm=comm)
    return res[0], res[1], res[2], res[3], list(res[4:])


def _ffn_act_bwd2(dac, dv, conv_w):
    S = dac.shape[0]
    F = D_FF
    tm = _pick(S, 256)
    t8 = tm // 8
    nt = S // tm
    cur = pl.BlockSpec((tm, F), lambda i: (i, 0))
    halo_next = pl.BlockSpec((8, F), lambda i: (jnp.minimum((i + 1) * t8, S // 8 - 1), 0))

    def body(d_ref, h_ref, dv_ref, w_ref, o_ref):
        d0 = d_ref[...].astype(F32)
        halo = jnp.where(pl.program_id(0) == nt - 1, 0.0, h_ref[...].astype(F32))
        d1 = _shift_up(d0, halo, 1)
        d2 = _shift_up(d0, halo, 2)
        w = w_ref[...]
        o_ref[:, 0:F] = (w[2:3, :] * d0 + w[1:2, :] * d1 + w[0:1, :] * d2).astype(BF)
        o_ref[:, F:2 * F] = dv_ref[...]

    return pl.pallas_call(
        body, name="ffn_act_bwd2",
        out_shape=jax.ShapeDtypeStruct((S, 2 * F), BF),
        grid=(nt,),
        in_specs=[cur, halo_next, cur, pl.BlockSpec((3, F), lambda i: (0, 0))],
        out_specs=pl.BlockSpec((tm, 2 * F), lambda i: (i, 0)),
        compiler_params=_params("parallel"),
    )(dac, dac, dv, conv_w)


def _adamw(parts, w, m, v, name):
    R, C = w.shape
    tr = R
    for cand in (512, 256, 128, 64, 32, 16):
        if R % cand == 0 and cand * C * 4 <= (1 << 21):
            tr = cand
            break
    c1 = 1.0 / (1.0 - ADAM_B1 ** ADAM_STEP)
    c2 = 1.0 / (1.0 - ADAM_B2 ** ADAM_STEP)

    def body(p_ref, w_ref, m_ref, v_ref, g_ref, d_ref, nm_ref, nv_ref):
        g = p_ref[0].astype(F32)
        for j in range(1, N_DEV):
            g = g + p_ref[j].astype(F32)
        nm = ADAM_B1 * m_ref[...] + (1.0 - ADAM_B1) * g
        nv = ADAM_B2 * v_ref[...] + (1.0 - ADAM_B2) * (g * g)
        g_ref[...] = g
        nm_ref[...] = nm
        nv_ref[...] = nv
        d_ref[...] = -ADAM_LR * ((nm * c1) / (jnp.sqrt(nv * c2) + ADAM_EPS) + ADAM_WD * w_ref[...])

    blk = pl.BlockSpec((tr, C), lambda i: (i, 0))
    out = jax.ShapeDtypeStruct((R, C), F32)
    return pl.pallas_call(
        body, name=name,
        out_shape=(out, out, out, out),
        grid=(R // tr,),
        in_specs=[pl.BlockSpec((N_DEV, tr, C), lambda i: (0, i, 0)), blk, blk, blk],
        out_specs=(blk, blk, blk, blk),
        compiler_params=_params("parallel"),
    )(parts, w, m, v)


def _dil(t, dil):
    S, C = t.shape
    if dil == 1:
        return t.reshape(1, S, C)
    return t.reshape(S // dil, dil, C).transpose(1, 0, 2)


def _undil(t):
    d, L, C = t.shape
    if d == 1:
        return t.reshape(L, C)
    return t.transpose(1, 0, 2).reshape(L * d, C)


def _group_qkv(proj, g):
    dil = DSW_GROUPS[g][1]
    if dil == 1:
        p3 = _dil(proj, 1)
        return (p3, p3, p3), (g, W_A // OUT_A + g, 2 * W_A // OUT_A + g)
    c0 = g * OUT_A
    return tuple(_dil(proj[:, o + c0:o + c0 + OUT_A], dil) for o in (0, W_A, 2 * W_A)), (0, 0, 0)


_COL_SHARDED = ("w_in", "w_br", "w_up")
_ROW_SHARDED = ("w_o", "w_down")


class _Plan:
    def __init__(self):
        self.riders = {}
        self.landed = {}

    def ride(self, slot, key, kind, x):
        self.riders.setdefault(slot, []).append((key, kind, x))

    def run(self, slot, fn, *args, **kw):
        items = self.riders.pop(slot, [])
        res = fn(*args, comm=[(kind, x) for _, kind, x in items], **kw)
        for (key, _, _), r in zip(items, res[-1]):
            self.landed[key] = r
        return res[0] if len(res) == 2 else res[:-1]

    def weight(self, n, l):
        g = self.landed[(n, l)]
        if n in _COL_SHARDED:
            return g.transpose(1, 0, 2).reshape(g.shape[1], -1)
        return g.reshape(-1, g.shape[2])

    def scatter(self, slot, n, l, full):
        K, N = full.shape
        if n in _COL_SHARDED:
            blocks = full.reshape(K, N_DEV, N // N_DEV).transpose(1, 0, 2)
        else:
            blocks = full.reshape(N_DEV, K // N_DEV, N)
        self.ride(slot, ("d" + n, l), "scatter", blocks)


def _layer_fwd(x, p, plan, l):
    h = _rms_fwd(x, p["norm1"][l], f"rms1_fwd_{l}")
    proj = plan.run(f"proj_{l}", _matmul, h, plan.weight("w_in", l), mode="nn", out_dtype=BF, name=f"proj_{l}",
                    tm=1024, tn=1024, tk=1024)
    o_g, l_g, qkv_g = [], [], []
    for g in range(N_GROUPS):
        qkv, cols = _group_qkv(proj, g)
        og, lg = plan.run(f"attn_a_fwd_g{g}_{l}", _attn_a_fwd, qkv, cols, g)
        o_g.append(_undil(og))
        l_g.append(_undil(lg))
        qkv_g.append((qkv, cols))
    ob = plan.run(f"sb_fwd_{l}", _sb_fwd, proj)
    merged, oa, lse = _merge_fwd(o_g, l_g, ob, proj, p["b_gate"][l], plan.weight("w_br", l))
    x1 = plan.run(f"wo_{l}", _matmul, merged, plan.weight("w_o", l), mode="nn", out_dtype=F32, name=f"wo_{l}",
                  tm=1024, tn=1024, tk=1024, res=x)
    h2 = _rms_fwd(x1, p["norm2"][l], f"rms2_fwd_{l}")
    up = plan.run(f"up_{l}", _matmul, h2, plan.weight("w_up", l), mode="nn", out_dtype=BF, name=f"up_{l}",
                  tm=1024, tn=1408, tk=1024)
    act = _ffn_act_fwd(up, p["conv_w"][l], p["conv_b"][l])
    x2 = plan.run(f"down_{l}", _matmul, act, plan.weight("w_down", l), mode="nn", out_dtype=F32, name=f"down_{l}",
                  tm=1024, tn=1024, tk=D_FF, res=x1)
    saved = dict(x=x, h=h, proj=proj, qkv_g=qkv_g, oa=oa, ob=ob, lse=lse, merged=merged, x1=x1, h2=h2, up=up, act=act)
    return x2, saved


def _layer_bwd(dx2, sv, p, plan, l):
    gr = {}
    dact = plan.run(f"d_act_{l}", _matmul, dx2, plan.weight("w_down", l), mode="nt", out_dtype=BF, name=f"d_act_{l}",
                    tm=1024, tn=1408, tk=1024)
    dwd = plan.run(f"dw_down_{l}", _matmul, sv["act"], dx2, mode="tn", out_dtype=BF, name=f"dw_down_{l}",
                   tm=1408, tn=1024, tk=2048)
    plan.scatter(f"d_h2_{l}", "w_down", l, dwd)
    dac, dv, gr["conv_w"], dcb = plan.run(f"ffn_act_bwd1_{l}", _ffn_act_bwd1, dact, sv["up"], p["conv_w"][l],
                                          p["conv_b"][l])
    gr["conv_b"] = dcb[0]
    dup = _ffn_act_bwd2(dac, dv, p["conv_w"][l])
    dh2 = plan.run(f"d_h2_{l}", _matmul, dup, plan.weight("w_up", l), mode="nt", out_dtype=BF, name=f"d_h2_{l}",
                   tm=512, tn=1024, tk=5632)
    dwu = plan.run(f"dw_up_{l}", _matmul, sv["h2"], dup, mode="tn", out_dtype=BF, name=f"dw_up_{l}",
                   tm=1024, tn=1408, tk=2048)
    plan.scatter(f"sb_bwd_{l}", "w_up", l, dwu)
    dx1, dn2 = _rms_bwd(sv["x1"], p["norm2"][l], dh2, dx2, f"rms2_bwd_{l}")
    gr["norm2"] = dn2[0]
    dm = plan.run(f"d_merged_{l}", _matmul, dx1, plan.weight("w_o", l), mode="nt", out_dtype=BF,
                  name=f"d_merged_{l}", tm=1024, tn=1024, tk=1024)
    dwo = plan.run(f"dw_o_{l}", _matmul, sv["merged"], dx1, mode="tn", out_dtype=BF, name=f"dw_o_{l}",
                   tm=1024, tn=1024, tk=2048)
    plan.scatter(f"d_h_{l}", "w_o", l, dwo)
    dya, dyb, doa, dob, dsum, dgate, dbg = _merge_bwd(dm, sv["oa"], sv["ob"], sv["proj"], p["b_gate"][l],
                                                      plan.weight("w_br", l))
    gr["b_gate"] = dbg[0]
    dwa = plan.run(f"dw_bra_{l}", _matmul, sv["oa"], dya, mode="tn", out_dtype=BF, name=f"dw_bra_{l}",
                   tm=256, tn=1024, tk=2048)
    dwb = plan.run(f"dw_brb_{l}", _matmul, sv["ob"], dyb, mode="tn", out_dtype=BF, name=f"dw_brb_{l}",
                   tm=256, tn=1024, tk=2048)
    plan.scatter(f"d_h_{l}", "w_br", l, jnp.concatenate([dwa, dwb], axis=0))
    proj = sv["proj"]
    dq_a, dk_a, dv_a = [], [], []
    for g, (_, dil) in enumerate(DSW_GROUPS):
        qkv, cols = sv["qkv_g"][g]
        dqg, dkg, dvg = _attn_a_bwd(qkv, cols, _dil(doa, dil), _dil(sv["lse"], dil), _dil(dsum, dil), g)
        dq_a.append(_undil(dqg))
        dk_a.append(_undil(dkg))
        dv_a.append(_undil(dvg))
    dqb, dkb, dvb = plan.run(f"sb_bwd_{l}", _sb_bwd, proj, dob, sv["ob"])
    dproj = jnp.concatenate(dq_a + dk_a + dv_a + [dqb.astype(BF), dkb.astype(BF), dvb.astype(BF), dgate], axis=1)
    dh = plan.run(f"d_h_{l}", _matmul, dproj, plan.weight("w_in", l), mode="nt", out_dtype=BF, name=f"d_h_{l}",
                  tm=512, tn=1024, tk=5120)
    dwi = plan.run(f"dw_in_{l}", _matmul, sv["h"], dproj, mode="tn", out_dtype=BF, name=f"dw_in_{l}",
                   tm=1024, tn=1280, tk=2048)
    plan.scatter(f"ffn_act_bwd1_{l - 1}", "w_in", l, dwi)
    dx, dn1 = _rms_bwd(sv["x"], p["norm1"][l], dh, dx1, f"rms1_bwd_{l}")
    gr["norm1"] = dn1[0]
    return dx, gr


def kernel(x, norm1, w_in, b_gate, w_br, w_o, norm2, w_up, conv_w, conv_b, w_down, norm_f, loss_target, m_norm1, m_w_in, m_b_gate, m_w_br, m_w_o, m_norm2, m_w_up, m_conv_w, m_conv_b, m_w_down, m_norm_f, v_norm1, v_w_in, v_b_gate, v_w_br, v_w_o, v_norm2, v_w_up, v_conv_w, v_conv_b, v_w_down, v_norm_f):
    depth = norm1.shape[0]
    me = 4 * lax.axis_index("x") + 2 * lax.axis_index("y") + lax.axis_index("c")
    shards = dict(w_in=w_in, w_br=w_br, w_o=w_o, w_up=w_up, w_down=w_down)
    moments_m = dict(norm1=m_norm1, w_in=m_w_in, b_gate=m_b_gate, w_br=m_w_br, w_o=m_w_o, norm2=m_norm2,
                     w_up=m_w_up, conv_w=m_conv_w, conv_b=m_conv_b, w_down=m_w_down, norm_f=m_norm_f)
    moments_v = dict(norm1=v_norm1, w_in=v_w_in, b_gate=v_b_gate, w_br=v_w_br, w_o=v_w_o, norm2=v_norm2,
                     w_up=v_w_up, conv_w=v_conv_w, conv_b=v_conv_b, w_down=v_w_down, norm_f=v_norm_f)

    plan = _Plan()
    wb = {n: s.astype(BF) for n, s in shards.items()}
    p = dict(norm1=norm1, b_gate=b_gate, norm2=norm2, conv_b=conv_b)
    cw = _all_gather(conv_w, "gather_conv_w")
    p["conv_w"] = cw.transpose(1, 2, 0, 3).reshape(depth, 3, D_FF)
    plan.landed[("w_in", 0)] = _all_gather(wb["w_in"][0], "gather_w_in_0")
    for l in range(depth):
        plan.ride(f"proj_{l}", ("w_up", l), "gather", wb["w_up"][l])
        plan.ride(f"attn_a_fwd_g0_{l}" if l == 0 else f"down_{l - 1}", ("w_br", l), "gather", wb["w_br"][l])
        plan.ride(f"attn_a_fwd_g0_{l}" if l == 0 else f"down_{l - 1}", ("w_o", l), "gather", wb["w_o"][l])
        plan.ride(f"sb_fwd_{l}", ("w_down", l), "gather", wb["w_down"][l])
        if l + 1 < depth:
            plan.ride(f"up_{l}", ("w_in", l + 1), "gather", wb["w_in"][l + 1])

    xs = x[0]
    saved = []
    for l in range(depth):
        xs, sv = _layer_fwd(xs, p, plan, l)
        saved.append(sv)
    loss_part, dx, dnf = _loss_head(xs, norm_f, loss_target[0])
    loss = lax.psum(loss_part[0, 0], ("x", "y", "c"))

    grads = [None] * depth
    for l in reversed(range(depth)):
        dx, grads[l] = _layer_bwd(dx, saved[l], p, plan, l)
    grad_x = dx[None]
    (_, _, last), = plan.riders.pop("ffn_act_bwd1_-1")
    plan.landed[("dw_in", 0)] = _all_to_all(last, "scatter_w_in_0")
    assert not plan.riders, sorted(plan.riders)

    out_g, out_d, out_m, out_v = {}, {}, {}, {}
    for n in _COL_SHARDED + _ROW_SHARDED:
        parts = jnp.concatenate([plan.landed[("d" + n, l)] for l in range(depth)], axis=1)
        shp = shards[n].shape
        flat = (shp[0] * shp[1], shp[2])
        res = _adamw(parts, shards[n].reshape(flat), moments_m[n].reshape(flat), moments_v[n].reshape(flat),
                     f"adamw_{n}")
        out_g[n], out_d[n], out_m[n], out_v[n] = [r.reshape(shp) for r in res]

    small = ("norm1", "b_gate", "norm2", "conv_b")
    vecs = [jnp.stack([grads[l][n] for l in range(depth)]).reshape(-1) for n in small]
    vecs.append(dnf.reshape(-1))
    vecs.append(jnp.stack([grads[l]["conv_w"] for l in range(depth)]).reshape(-1))
    sizes = [v.shape[0] for v in vecs]
    flat = jnp.concatenate(vecs)
    n_small = sum(sizes[:-1])
    pad = (-flat.shape[0]) % 1024
    flat = jnp.pad(flat, (0, pad)).reshape(-1, 128)
    allp = _all_gather(flat, "gather_small_grads").reshape(N_DEV, -1)
    rep_w = jnp.concatenate([norm1.reshape(-1), b_gate.reshape(-1), norm2.reshape(-1), conv_b.reshape(-1), norm_f])
    rep_m = jnp.concatenate([moments_m[n].reshape(-1) for n in small] + [m_norm_f])
    rep_v = jnp.concatenate([moments_v[n].reshape(-1) for n in small] + [v_norm_f])
    rows = n_small // 128
    res = _adamw(allp[:, :n_small].reshape(N_DEV, rows, 128), rep_w.reshape(rows, 128), rep_m.reshape(rows, 128),
                 rep_v.reshape(rows, 128), "adamw_small")
    off = 0
    for n, sz in zip(small + ("norm_f",), sizes[:-1]):
        shp = norm_f.shape if n == "norm_f" else p[n].shape
        out_g[n], out_d[n], out_m[n], out_v[n] = [r.reshape(-1)[off:off + sz].reshape(shp) for r in res]
        off += sz
    f = conv_w.shape[2]
    cwp = allp[:, n_small:n_small + sizes[-1]].reshape(N_DEV, depth * 3, D_FF)
    cwp = lax.dynamic_slice_in_dim(cwp, me * f, f, axis=2)
    res = _adamw(cwp, conv_w.reshape(depth * 3, f), m_conv_w.reshape(depth * 3, f), v_conv_w.reshape(depth * 3, f),
                 "adamw_conv_w")
    out_g["conv_w"], out_d["conv_w"], out_m["conv_w"], out_v["conv_w"] = [r.reshape(conv_w.shape) for r in res]

    order = ("norm1", "w_in", "b_gate", "w_br", "w_o", "norm2", "w_up", "conv_w", "conv_b", "w_down", "norm_f")
    return (loss, grad_x, *[out_g[n] for n in order], *[out_d[n] for n in order],
            *[out_m[n] for n in order], *[out_v[n] for n in order])
```

```python
import functools

import jax
import jax.numpy as jnp
from jax import lax
from jax.experimental import pallas as pl
from jax.experimental.pallas import tpu as pltpu

BF = jnp.bfloat16
F32 = jnp.float32

N_DEV = 8
D_MODEL = 1024
HEAD_DIM = 64
DSW_GROUPS = ((128, 1), (512, 4), (2048, 16))
HEADS_PER_GROUP = 4
N_GROUPS = len(DSW_GROUPS)
DSW_HEADS = HEADS_PER_GROUP * N_GROUPS
SB_HEADS = 4
W_A = DSW_HEADS * HEAD_DIM
W_B = SB_HEADS * HEAD_DIM
OUT_A = HEADS_PER_GROUP * HEAD_DIM
N_IN = 3 * W_A + 3 * W_B + 2 * D_MODEL
GATE_OFF = 3 * W_A + 3 * W_B
D_FF = 2816
SB_BLOCK = 256
RMS_EPS = 1e-6
ATT_SCALE = HEAD_DIM ** -0.5
NEG = -1e30
SB_EXIT = -110.0

ADAM_LR = 0.001
ADAM_B1 = 0.9
ADAM_B2 = 0.999
ADAM_EPS = 1e-08
ADAM_WD = 0.01
ADAM_STEP = 10

HBM_SPEC = pl.BlockSpec(memory_space=pltpu.HBM)
MESH = pl.DeviceIdType.MESH

_NN = (((1,), (0,)), ((), ()))
_NT = (((1,), (1,)), ((), ()))
_TN = (((0,), (0,)), ((), ()))


def _dot(a, b, dn=_NN):
    return lax.dot_general(a, b, dn, preferred_element_type=F32)


def _pick(dim, pref):
    if dim <= pref:
        return dim
    t = (pref // 128) * 128
    while t >= 128:
        if dim % t == 0:
            return t
        t -= 128
    return dim


def _params(*sem):
    return pltpu.CompilerParams(dimension_semantics=sem)


def _peer(k):
    x, y, c = lax.axis_index("x"), lax.axis_index("y"), lax.axis_index("c")
    px = 1 - x if (k >> 2) & 1 else x
    py = 1 - y if (k >> 1) & 1 else y
    pc = 1 - c if k & 1 else c
    return (px, py, pc), 4 * px + 2 * py + pc


def _exchange(kind, x_ref, out_ref, send_sems, recv_sems, local_sem):
    gather = kind == "gather"
    _, me = _peer(0)

    def src(idx):
        return x_ref if gather else x_ref.at[idx]

    def copy(k, dst_idx):
        peer, pidx = _peer(k)
        return pltpu.make_async_remote_copy(
            src_ref=src(pidx), dst_ref=out_ref.at[dst_idx], send_sem=send_sems.at[k - 1],
            recv_sem=recv_sems.at[k - 1], device_id=peer, device_id_type=MESH)

    mine = pltpu.make_async_copy(src(me), out_ref.at[me], local_sem)

    def start():
        mine.start()
        for k in range(1, N_DEV):
            copy(k, me).start()

    def wait():
        for k in range(1, N_DEV):
            copy(k, _peer(k)[1]).wait_recv()
        for k in range(1, N_DEV):
            copy(k, me).wait_send()
        mine.wait()

    return start, wait


_EXCHANGE_SEMS = [pltpu.SemaphoreType.DMA((N_DEV - 1,)), pltpu.SemaphoreType.DMA((N_DEV - 1,)),
                  pltpu.SemaphoreType.DMA]


def _exchange_shape(kind, x):
    return jax.ShapeDtypeStruct(((N_DEV,) + x.shape) if kind == "gather" else x.shape, x.dtype)


def _exchange_alone(kind, x, name):
    def body(x_ref, out_ref, send_sems, recv_sems, local_sem):
        start, wait = _exchange(kind, x_ref, out_ref, send_sems, recv_sems, local_sem)
        start()
        wait()

    return pl.pallas_call(
        body, name=name, out_shape=_exchange_shape(kind, x),
        in_specs=[HBM_SPEC], out_specs=HBM_SPEC, scratch_shapes=list(_EXCHANGE_SEMS),
    )(x)


def _all_gather(x, name):
    return _exchange_alone("gather", x, name)


def _all_to_all(x, name):
    return _exchange_alone("scatter", x, name)


def _call(body, *, name, grid, in_specs, out_specs, out_shape, args, scratch_shapes=(), sem=None, comm=()):
    single = not isinstance(out_shape, (tuple, list))
    outs = (out_shape,) if single else tuple(out_shape)
    ospecs = (out_specs,) if single else tuple(out_specs)
    if not comm:
        res = pl.pallas_call(
            body, name=name, out_shape=outs, grid=grid, in_specs=list(in_specs), out_specs=ospecs,
            scratch_shapes=list(scratch_shapes), compiler_params=_params(*(sem or ("arbitrary",) * len(grid))),
        )(*args)
        return res
    n_in, n_out, n_scr, nc = len(in_specs), len(outs), len(scratch_shapes), len(comm)

    def wrapped(*refs):
        ins = refs[:n_in]
        cins = refs[n_in:n_in + nc]
        o0 = n_in + nc
        kouts = refs[o0:o0 + n_out]
        couts = refs[o0 + n_out:o0 + n_out + nc]
        s0 = o0 + n_out + nc
        scr = refs[s0:s0 + n_scr]
        sems = refs[s0 + n_scr:]
        ids = [pl.program_id(ax) for ax in range(len(grid))]
        first = functools.reduce(jnp.logical_and, [i == 0 for i in ids])
        last = functools.reduce(jnp.logical_and, [i == g - 1 for i, g in zip(ids, grid)])
        ex = [_exchange(comm[c][0], cins[c], couts[c], *sems[3 * c:3 * c + 3]) for c in range(nc)]

        @pl.when(first)
        def _():
            for start, _ in ex:
                start()

        body(*ins, *kouts, *scr)

        @pl.when(last)
        def _():
            for _, wait in ex:
                wait()

    return pl.pallas_call(
        wrapped, name=name,
        out_shape=outs + tuple(_exchange_shape(k, x) for k, x in comm),
        grid=grid, in_specs=list(in_specs) + [HBM_SPEC] * nc, out_specs=ospecs + (HBM_SPEC,) * nc,
        scratch_shapes=list(scratch_shapes) + list(_EXCHANGE_SEMS) * nc,
        compiler_params=_params(*(("arbitrary",) * len(grid))),
    )(*args, *[x for _, x in comm])


def _matmul(a, b, *, mode, out_dtype, name, tm=512, tn=1024, tk=1024, res=None, comm=()):
    if mode == "nn":
        (M, K), (_, N) = a.shape, b.shape
    elif mode == "nt":
        (M, K), (N, _) = a.shape, b.shape
    else:
        (K, M), (_, N) = a.shape, b.shape
    tm, tn, tk = _pick(M, tm), _pick(N, tn), _pick(K, tk)
    nk = K // tk
    dn = {"nn": _NN, "nt": _NT, "tn": _TN}[mode]

    def body(*refs):
        a_ref, b_ref = refs[0], refs[1]
        r_ref = refs[2] if res is not None else None
        o_ref = refs[3] if res is not None else refs[2]

        def finish(r):
            if res is not None:
                r = r + r_ref[...].astype(F32)
            o_ref[...] = r.astype(out_dtype)

        part = _dot(a_ref[...].astype(BF), b_ref[...].astype(BF), dn)
        if nk == 1:
            finish(part)
            return
        acc = refs[-1]
        k = pl.program_id(2)

        @pl.when(k == 0)
        def _():
            acc[...] = jnp.zeros_like(acc)

        acc[...] += part

        @pl.when(k == nk - 1)
        def _():
            finish(acc[...])

    if mode == "tn":
        a_spec = pl.BlockSpec((tk, tm), lambda j, i, k: (k, i))
    else:
        a_spec = pl.BlockSpec((tm, tk), lambda j, i, k: (i, k))
    if mode == "nt":
        b_spec = pl.BlockSpec((tn, tk), lambda j, i, k: (j, k))
    else:
        b_spec = pl.BlockSpec((tk, tn), lambda j, i, k: (k, j))
    o_spec = pl.BlockSpec((tm, tn), lambda j, i, k: (i, j))
    in_specs = [a_spec, b_spec] + ([o_spec] if res is not None else [])
    args = (a, b) + ((res,) if res is not None else ())
    out = _call(body, name=name, grid=(N // tn, M // tm, nk), in_specs=in_specs, out_specs=o_spec,
                out_shape=jax.ShapeDtypeStruct((M, N), out_dtype), args=args,
                scratch_shapes=[pltpu.VMEM((tm, tn), F32)] if nk > 1 else [],
                sem=("parallel", "parallel", "arbitrary"), comm=comm)
    return out[0], list(out[1:])


def _rms_fwd(x, g, name):
    S, D = x.shape
    tm = _pick(S, 512)

    def body(x_ref, g_ref, h_ref):
        xf = x_ref[...]
        r = lax.rsqrt(jnp.mean(xf * xf, axis=-1, keepdims=True) + RMS_EPS)
        h_ref[...] = (xf * r * g_ref[...]).astype(BF)

    return pl.pallas_call(
        body, name=name,
        out_shape=jax.ShapeDtypeStruct((S, D), BF),
        grid=(S // tm,),
        in_specs=[pl.BlockSpec((tm, D), lambda i: (i, 0)), pl.BlockSpec((1, D), lambda i: (0, 0))],
        out_specs=pl.BlockSpec((tm, D), lambda i: (i, 0)),
        compiler_params=_params("parallel"),
    )(x, g.reshape(1, D))


def _rms_bwd(x, g, dh, dres, name):
    S, D = x.shape
    tm = _pick(S, 512)

    def body(x_ref, g_ref, dh_ref, dres_ref, dx_ref, dg_ref):
        @pl.when(pl.program_id(0) == 0)
        def _():
            dg_ref[...] = jnp.zeros_like(dg_ref)

        xf = x_ref[...]
        r = lax.rsqrt(jnp.mean(xf * xf, axis=-1, keepdims=True) + RMS_EPS)
        xh = xf * r
        dy = dh_ref[...].astype(F32)
        dg_ref[...] += jnp.sum(dy * xh, axis=0, keepdims=True)
        dxh = dy * g_ref[...]
        dx = r * (dxh - xh * jnp.mean(dxh * xh, axis=-1, keepdims=True))
        dx_ref[...] = dres_ref[...] + dx

    row = pl.BlockSpec((tm, D), lambda i: (i, 0))
    vec = pl.BlockSpec((1, D), lambda i: (0, 0))
    return pl.pallas_call(
        body, name=name,
        out_shape=(jax.ShapeDtypeStruct((S, D), F32), jax.ShapeDtypeStruct((1, D), F32)),
        grid=(S // tm,),
        in_specs=[row, vec, row, row], out_specs=(row, vec),
        compiler_params=_params("arbitrary"),
    )(x, g.reshape(1, D), dh, dres)


def _loss_head(x, g, target):
    S, D = x.shape
    tm = _pick(S, 512)

    def body(x_ref, g_ref, t_ref, loss_ref, dx_ref, dg_ref):
        @pl.when(pl.program_id(0) == 0)
        def _():
            dg_ref[...] = jnp.zeros_like(dg_ref)
            loss_ref[...] = jnp.zeros_like(loss_ref)

        xf = x_ref[...]
        gg = g_ref[...]
        r = lax.rsqrt(jnp.mean(xf * xf, axis=-1, keepdims=True) + RMS_EPS)
        xh = xf * r
        err = xh * gg - t_ref[...]
        per_tok = jnp.mean(err * err, axis=-1, keepdims=True)
        loss_ref[...] += 0.5 * jnp.sum(per_tok, axis=0, keepdims=True)
        dy = err * (1.0 / D)
        dg_ref[...] += jnp.sum(dy * xh, axis=0, keepdims=True)
        dxh = dy * gg
        dx_ref[...] = r * (dxh - xh * jnp.mean(dxh * xh, axis=-1, keepdims=True))

    row = pl.BlockSpec((tm, D), lambda i: (i, 0))
    vec = pl.BlockSpec((1, D), lambda i: (0, 0))
    one = pl.BlockSpec((1, 1), lambda i: (0, 0))
    return pl.pallas_call(
        body, name="loss_head",
        out_shape=(jax.ShapeDtypeStruct((1, 1), F32), jax.ShapeDtypeStruct((S, D), F32),
                   jax.ShapeDtypeStruct((1, D), F32)),
        grid=(S // tm,),
        in_specs=[row, vec, row], out_specs=(one, row, vec),
        compiler_params=_params("arbitrary"),
    )(x, g.reshape(1, D), target)


def _slopes(g):
    return [2.0 ** (-8.0 * (HEADS_PER_GROUP * g + j + 1) / DSW_HEADS) for j in range(HEADS_PER_GROUP)]


def _band_masks(W):
    row = lax.broadcasted_iota(jnp.int32, (W, W), 0)
    col = lax.broadcasted_iota(jnp.int32, (W, W), 1)
    d_cur = row - col
    d_prev = d_cur + W
    return d_cur, d_prev, d_cur >= 0, d_cur <= 0


def _band_specs(W, nb, per):
    def cur(c):
        return pl.BlockSpec((None, per * W, OUT_A), lambda r, n: (r, n, c))

    def prev(c):
        return pl.BlockSpec((None, W, OUT_A), lambda r, n: (r, jnp.maximum(per * n - 1, 0), c))

    def nxt(c):
        return pl.BlockSpec((None, W, OUT_A), lambda r, n: (r, jnp.minimum(per * (n + 1), nb - 1), c))

    return cur, prev, nxt


def _blocks_per_step(nb):
    return 2 if nb % 2 == 0 else 1


def _head_stack(W):
    H, hd = HEADS_PER_GROUP, HEAD_DIM
    lane_head = lax.broadcasted_iota(jnp.int32, (W, OUT_A), 1) // hd

    def stack(x):
        return jnp.concatenate([jnp.where(lane_head == h, x, jnp.zeros_like(x)) for h in range(H)], axis=0)

    def unstack(y):
        out = jnp.where(lane_head == 0, y[0:W], 0.0)
        for h in range(1, H):
            out = jnp.where(lane_head == h, y[h * W:(h + 1) * W], out)
        return out

    def column(ref, rows):
        return jnp.concatenate([ref[rows, h * hd:h * hd + 1] for h in range(H)], axis=0)

    def tile(x):
        return jnp.concatenate([x] * H, axis=0)

    return stack, unstack, column, tile


def _stacked_bias(W, slopes, dil):
    d_cur, d_prev, m_cur, m_prev = _band_masks(W)
    b_cur = jnp.concatenate([(s * dil) * d_cur.astype(F32) for s in slopes], axis=0)
    b_prev = jnp.concatenate([(s * dil) * d_prev.astype(F32) for s in slopes], axis=0)
    H = len(slopes)
    return b_cur, b_prev, jnp.concatenate([m_cur] * H, axis=0), jnp.concatenate([m_prev] * H, axis=0)


def _attn_a_fwd(qkv, cols, g, comm=()):
    win, dil = DSW_GROUPS[g]
    W = win // dil
    d, L, _ = qkv[0].shape
    nb = L // W
    per = _blocks_per_step(nb)
    slopes = _slopes(g)

    def body(q_ref, kp_ref, kc_ref, vp_ref, vc_ref, o_ref, l_ref):
        n = pl.program_id(1)
        stack, unstack, _, _ = _head_stack(W)
        b_cur, b_prev, m_cur, m_prev = _stacked_bias(W, slopes, dil)
        m_first = jnp.logical_and(m_prev, n > 0)
        for b in range(per):
            rows = slice(b * W, (b + 1) * W)
            before = slice((b - 1) * W, b * W)
            qs = stack(q_ref[rows, :])
            kc, vc = kc_ref[rows, :], vc_ref[rows, :]
            kp, vp = (kp_ref[...], vp_ref[...]) if b == 0 else (kc_ref[before, :], vc_ref[before, :])
            s_c = jnp.where(m_cur, _dot(qs, kc, _NT) * ATT_SCALE - b_cur, NEG)
            s_p = jnp.where(m_first if b == 0 else m_prev, _dot(qs, kp, _NT) * ATT_SCALE - b_prev, NEG)
            m = jnp.maximum(jnp.max(s_c, axis=1, keepdims=True), jnp.max(s_p, axis=1, keepdims=True))
            p_c = jnp.exp(s_c - m)
            p_p = jnp.exp(s_p - m)
            den = jnp.sum(p_c, axis=1, keepdims=True) + jnp.sum(p_p, axis=1, keepdims=True)
            pv = _dot(p_c.astype(BF), vc) + _dot(p_p.astype(BF), vp)
            o_ref[rows, :] = unstack(pv / den)
            l_ref[rows, :] = unstack(jnp.broadcast_to(m + jnp.log(den), pv.shape))

    cur, prev, _ = _band_specs(W, nb, per)
    out = jax.ShapeDtypeStruct((d, L, OUT_A), F32)
    res = _call(body, name=f"attn_a_fwd_g{g}", grid=(d, nb // per),
                in_specs=[cur(cols[0]), prev(cols[1]), cur(cols[1]), prev(cols[2]), cur(cols[2])],
                out_specs=(cur(0), cur(0)), out_shape=(out, out),
                args=(qkv[0], qkv[1], qkv[1], qkv[2], qkv[2]), sem=("parallel", "parallel"), comm=comm)
    return res[0], res[1], list(res[2:])


def _attn_a_bwd(qkv, cols, do, lse, dsum, g):
    win, dil = DSW_GROUPS[g]
    W = win // dil
    d, L, _ = qkv[0].shape
    nb = L // W
    per = _blocks_per_step(nb)
    nsteps = nb // per
    slopes = _slopes(g)

    def body(q_ref, qn_ref, kp_ref, kc_ref, vp_ref, vc_ref, do_ref, don_ref, l_ref, ln_ref,
             ds_ref, dsn_ref, dq_ref, dk_ref, dv_ref):
        n = pl.program_id(1)
        stack, unstack, column, _ = _head_stack(W)
        b_cur, b_prev, m_cur, m_prev = _stacked_bias(W, slopes, dil)
        m_first = jnp.logical_and(m_prev, n > 0)
        m_last = jnp.logical_and(m_prev, n < nsteps - 1)
        everything = slice(None)
        for b in range(per):
            rows = slice(b * W, (b + 1) * W)
            before = slice((b - 1) * W, b * W)
            after = slice((b + 1) * W, (b + 2) * W)
            first, last = b == 0, b == per - 1
            qs = stack(q_ref[rows, :])
            qn = stack(qn_ref[...] if last else q_ref[after, :])
            dos = stack(do_ref[rows, :].astype(BF))
            don = stack((don_ref[...] if last else do_ref[after, :]).astype(BF))
            kc, vc = kc_ref[rows, :], vc_ref[rows, :]
            kp, vp = (kp_ref[...], vp_ref[...]) if first else (kc_ref[before, :], vc_ref[before, :])
            lse_c, dsum_c = column(l_ref, rows), column(ds_ref, rows)
            lse_n = column(ln_ref, everything) if last else column(l_ref, after)
            dsum_n = column(dsn_ref, everything) if last else column(ds_ref, after)
            m_p = m_first if first else m_prev
            m_n = m_last if last else m_prev
            p_cc = jnp.exp(jnp.where(m_cur, _dot(qs, kc, _NT) * ATT_SCALE - b_cur, NEG) - lse_c)
            p_cp = jnp.exp(jnp.where(m_p, _dot(qs, kp, _NT) * ATT_SCALE - b_prev, NEG) - lse_c)
            p_nc = jnp.exp(jnp.where(m_n, _dot(qn, kc, _NT) * ATT_SCALE - b_prev, NEG) - lse_n)
            ds_cc = (p_cc * (_dot(dos, vc, _NT) - dsum_c) * ATT_SCALE).astype(BF)
            ds_cp = (p_cp * (_dot(dos, vp, _NT) - dsum_c) * ATT_SCALE).astype(BF)
            ds_nc = (p_nc * (_dot(don, vc, _NT) - dsum_n) * ATT_SCALE).astype(BF)
            dq_ref[rows, :] = unstack(_dot(ds_cc, kc) + _dot(ds_cp, kp)).astype(BF)
            dk_ref[rows, :] = (_dot(ds_cc, qs, _TN) + _dot(ds_nc, qn, _TN)).astype(BF)
            dv_ref[rows, :] = (_dot(p_cc.astype(BF), dos, _TN) + _dot(p_nc.astype(BF), don, _TN)).astype(BF)

    cur, prev, nxt = _band_specs(W, nb, per)
    out = jax.ShapeDtypeStruct((d, L, OUT_A), BF)
    cq, ck, cv = cols
    return pl.pallas_call(
        body, name=f"attn_a_bwd_g{g}",
        out_shape=(out, out, out),
        grid=(d, nsteps),
        in_specs=[cur(cq), nxt(cq), prev(ck), cur(ck), prev(cv), cur(cv),
                  cur(0), nxt(0), cur(0), nxt(0), cur(0), nxt(0)],
        out_specs=(cur(0), cur(0), cur(0)),
        compiler_params=_params("parallel", "parallel"),
    )(qkv[0], qkv[0], qkv[1], qkv[1], qkv[2], qkv[2], do, do, lse, lse, dsum, dsum)


SB_PAIR = 2
SB_QROWS = SB_BLOCK


def _softplus_parts(z):
    e = jnp.exp(-jnp.abs(z))
    log1p_e = jnp.where(e < 1e-3, e * (1.0 - e * (0.5 - e * (1.0 / 3.0))), jnp.log(1.0 + e))
    return e, jnp.maximum(z, 0.0) + log1p_e


def _split_dot(x, t):
    hi = x.astype(BF)
    lo = (x - hi.astype(F32)).astype(BF)
    return _dot(hi, t) + _dot(lo, t)


def _sb_block(qh, kk, causal, r_run, tri_incl):
    z = _dot(qh, kk, _NT) * ATT_SCALE
    e, sp = _softplus_parts(z)
    ls = jnp.where(causal, -sp, 0.0)
    cin = _split_dot(ls, tri_incl)
    a = jnp.where(causal, jnp.exp(z + cin + r_run), 0.0)
    return z, e, cin, a


def _sb_specs(S):
    Q, hd = SB_QROWS, HEAD_DIM
    lanes = SB_PAIR * hd
    qc = (3 * W_A) // lanes
    kc = (3 * W_A + W_B) // lanes
    vc = (3 * W_A + 2 * W_B) // lanes
    q_spec = pl.BlockSpec((Q, lanes), lambda p, i: (i, qc + p))
    k_spec = pl.BlockSpec((S, lanes), lambda p, i: (0, kc + p))
    v_spec = pl.BlockSpec((S, lanes), lambda p, i: (0, vc + p))
    o_spec = pl.BlockSpec((Q, lanes), lambda p, i: (i, p))
    full = pl.BlockSpec((S, lanes), lambda p, i: (0, p))
    return q_spec, k_spec, v_spec, o_spec, full


def _sb_stack():
    Q, hd = SB_QROWS, HEAD_DIM
    lane_head = lax.broadcasted_iota(jnp.int32, (Q, SB_PAIR * hd), 1) // hd

    def stack(x):
        return jnp.concatenate([jnp.where(lane_head == h, x, jnp.zeros_like(x)) for h in range(SB_PAIR)], axis=0)

    def unstack(y):
        out = jnp.where(lane_head == 0, y[0:Q], 0.0)
        for h in range(1, SB_PAIR):
            out = jnp.where(lane_head == h, y[h * Q:(h + 1) * Q], out)
        return out

    return stack, unstack


def _sb_iotas(i):
    B, Q = SB_BLOCK, SB_QROWS
    row = lax.broadcasted_iota(jnp.int32, (Q, B), 0) + i * Q
    row = jnp.concatenate([row] * SB_PAIR, axis=0)
    col = lax.broadcasted_iota(jnp.int32, (SB_PAIR * Q, B), 1)
    tr = lax.broadcasted_iota(jnp.int32, (B, B), 0)
    tc = lax.broadcasted_iota(jnp.int32, (B, B), 1)
    return row, col, tr, tc


def _sb_fwd(proj, comm=()):
    S = proj.shape[0]
    B, Q, hd = SB_BLOCK, SB_QROWS, HEAD_DIM
    nq = S // Q
    R = SB_PAIR * Q
    q_spec, k_spec, v_spec, o_spec, _ = _sb_specs(S)

    def body(q_ref, k_ref, v_ref, o_ref):
        i = pl.program_id(1)
        row, col, tr, tc = _sb_iotas(i)
        tri_incl = (tr >= tc).astype(BF)
        stack, unstack = _sb_stack()
        qs = stack(q_ref[...])

        def cond(c):
            return jnp.logical_and(c[0] >= 0, c[-1] > SB_EXIT)

        def step(c):
            kb, r_run, acc, _ = c
            off = pl.multiple_of(kb * B, B)
            causal = col + kb * B < row
            _, _, cin, a = _sb_block(qs, k_ref[pl.ds(off, B), :], causal, r_run, tri_incl)
            acc = acc + _dot(a.astype(BF), v_ref[pl.ds(off, B), :])
            r_run = r_run + cin[:, 0:1]
            return kb - 1, r_run, acc, jnp.max(r_run)

        init = (i, jnp.zeros((R, 1), F32), jnp.zeros((R, SB_PAIR * hd), F32), jnp.float32(0.0))
        fin = lax.while_loop(cond, step, init)
        o_ref[...] = unstack(fin[2])

    res = _call(body, name="sb_fwd", grid=(SB_HEADS // SB_PAIR, nq), in_specs=[q_spec, k_spec, v_spec],
                out_specs=o_spec, out_shape=jax.ShapeDtypeStruct((S, W_B), F32), args=(proj, proj, proj),
                sem=("parallel", "parallel"), comm=comm)
    return res[0], list(res[1:])


def _sb_bwd(proj, do, o, comm=()):
    S = proj.shape[0]
    B, Q, hd = SB_BLOCK, SB_QROWS, HEAD_DIM
    nq = S // Q
    R = SB_PAIR * Q
    q_spec, k_spec, v_spec, o_spec, full = _sb_specs(S)

    def body(q_ref, k_ref, v_ref, do_ref, o_ref, dq_ref, dk_ref, dv_ref):
        i = pl.program_id(1)

        @pl.when(i == 0)
        def _():
            dk_ref[...] = jnp.zeros_like(dk_ref)
            dv_ref[...] = jnp.zeros_like(dv_ref)

        row, col, tr, tc = _sb_iotas(i)
        tri_incl = (tr >= tc).astype(BF)
        tri_strict = (tr > tc).astype(BF)
        stack, unstack = _sb_stack()
        qs = stack(q_ref[...])
        dobs = stack(do_ref[...].astype(BF))
        o_all = o_ref[...]
        dsum = jnp.sum(dobs.astype(F32) * jnp.concatenate([o_all] * SB_PAIR, axis=0), axis=1, keepdims=True)

        def cond(c):
            return jnp.logical_and(c[0] >= 0, c[-1] > SB_EXIT)

        def step(c):
            kb, r_run, g_run, dq, _ = c
            off = pl.multiple_of(kb * B, B)
            causal = col + kb * B < row
            kk = k_ref[pl.ds(off, B), :]
            vv = v_ref[pl.ds(off, B), :]
            z, e, cin, a = _sb_block(qs, kk, causal, r_run, tri_incl)
            a16 = a.astype(BF)
            gmat = a16.astype(F32) * _dot(dobs, vv, _NT)
            later = _split_dot(gmat, tri_strict)
            pfx = dsum - g_run - later
            sig = jnp.where(z >= 0, 1.0, e) / (1.0 + e)
            dz = (jnp.where(causal, gmat - sig * pfx, 0.0) * ATT_SCALE).astype(BF)
            dq = dq + _dot(dz, kk)
            dk_ref[pl.ds(off, B), :] += _dot(dz, qs, _TN)
            dv_ref[pl.ds(off, B), :] += _dot(a16, dobs, _TN)
            g_run = g_run + jnp.sum(gmat, axis=1, keepdims=True)
            r_run = r_run + cin[:, 0:1]
            return kb - 1, r_run, g_run, dq, jnp.max(r_run)

        init = (i, jnp.zeros((R, 1), F32), jnp.zeros((R, 1), F32), jnp.zeros((R, SB_PAIR * hd), F32),
                jnp.float32(0.0))
        fin = lax.while_loop(cond, step, init)
        dq_ref[...] = unstack(fin[3])

    out = jax.ShapeDtypeStruct((S, W_B), F32)
    res = _call(body, name="sb_bwd", grid=(SB_HEADS // SB_PAIR, nq), in_specs=[q_spec, k_spec, v_spec, o_spec, o_spec],
                out_specs=(o_spec, full, full), out_shape=(out, out, out), args=(proj, proj, proj, do, o), comm=comm)
    return res[0], res[1], res[2], list(res[3:])


def _merge_fwd(o_g, l_g, o_b, proj, b_gate, w_br):
    S = o_b.shape[0]
    D = D_MODEL
    tm = _pick(S, 256)
    gcol = GATE_OFF // D

    def body(o0, o1, o2, l0, l1, l2, ob_ref, ga_ref, gb_ref, bg_ref, w_ref, mg_ref, oa_ref, lse_ref):
        la, lb, lc = l0[...], l1[...], l2[...]
        mx = jnp.maximum(jnp.maximum(la, lb), lc)
        ea, eb, ec = jnp.exp(la - mx), jnp.exp(lb - mx), jnp.exp(lc - mx)
        den = ea + eb + ec
        oa = (ea * o0[...] + eb * o1[...] + ec * o2[...]) / den
        oa_ref[...] = oa
        lse_ref[...] = mx + jnp.log(den)
        ya = _dot(oa.astype(BF), w_ref[0:OUT_A, :])
        yb = _dot(ob_ref[...].astype(BF), w_ref[OUT_A:OUT_A + W_B, :])
        bg = bg_ref[...]
        g_a = jax.nn.sigmoid(ga_ref[...].astype(F32) + bg[:, 0:D])
        g_b = jax.nn.sigmoid(gb_ref[...].astype(F32) + bg[:, D:2 * D])
        mg_ref[...] = (g_a * ya + g_b * yb).astype(BF)

    nar = pl.BlockSpec((tm, OUT_A), lambda i: (i, 0))
    wide = pl.BlockSpec((tm, D), lambda i: (i, 0))
    return pl.pallas_call(
        body, name="merge_fwd",
        out_shape=(jax.ShapeDtypeStruct((S, D), BF), jax.ShapeDtypeStruct((S, OUT_A), F32),
                   jax.ShapeDtypeStruct((S, OUT_A), F32)),
        grid=(S // tm,),
        in_specs=[nar] * 7 + [pl.BlockSpec((tm, D), lambda i: (i, gcol)),
                              pl.BlockSpec((tm, D), lambda i: (i, gcol + 1)),
                              pl.BlockSpec((1, 2 * D), lambda i: (0, 0)),
                              pl.BlockSpec((OUT_A + W_B, D), lambda i: (0, 0))],
        out_specs=(wide, nar, nar),
        compiler_params=_params("parallel"),
    )(*o_g, *l_g, o_b, proj, proj, b_gate.reshape(1, 2 * D), w_br)


def _merge_bwd(dm, oa, ob, proj, b_gate, w_br):
    S = ob.shape[0]
    D = D_MODEL
    tm = _pick(S, 256)
    gcol = GATE_OFF // D

    def body(dm_ref, oa_ref, ob_ref, ga_ref, gb_ref, bg_ref, w_ref,
             dya_ref, dyb_ref, doa_ref, dob_ref, ds_ref, dg_ref, dbg_ref):
        @pl.when(pl.program_id(0) == 0)
        def _():
            dbg_ref[...] = jnp.zeros_like(dbg_ref)

        wa = w_ref[0:OUT_A, :]
        wb = w_ref[OUT_A:OUT_A + W_B, :]
        oa = oa_ref[...]
        ya = _dot(oa.astype(BF), wa)
        yb = _dot(ob_ref[...].astype(BF), wb)
        bg = bg_ref[...]
        g_a = jax.nn.sigmoid(ga_ref[...].astype(F32) + bg[:, 0:D])
        g_b = jax.nn.sigmoid(gb_ref[...].astype(F32) + bg[:, D:2 * D])
        dm = dm_ref[...].astype(F32)
        dga = dm * ya * g_a * (1.0 - g_a)
        dgb = dm * yb * g_b * (1.0 - g_b)
        dg_ref[:, 0:D] = dga.astype(BF)
        dg_ref[:, D:2 * D] = dgb.astype(BF)
        dbg_ref[:, 0:D] += jnp.sum(dga, axis=0, keepdims=True)
        dbg_ref[:, D:2 * D] += jnp.sum(dgb, axis=0, keepdims=True)
        dya = (dm * g_a).astype(BF)
        dyb = (dm * g_b).astype(BF)
        dya_ref[...] = dya
        dyb_ref[...] = dyb
        doa = _dot(dya, wa, _NT)
        doa_ref[...] = doa
        dob_ref[...] = _dot(dyb, wb, _NT)
        r = lax.broadcasted_iota(jnp.int32, (OUT_A, OUT_A), 0) // HEAD_DIM
        c = lax.broadcasted_iota(jnp.int32, (OUT_A, OUT_A), 1) // HEAD_DIM
        ds_ref[...] = _split_dot(doa * oa, (r == c).astype(BF))

    nar = pl.BlockSpec((tm, OUT_A), lambda i: (i, 0))
    wide = pl.BlockSpec((tm, D), lambda i: (i, 0))
    wide2 = pl.BlockSpec((tm, 2 * D), lambda i: (i, 0))
    vec2 = pl.BlockSpec((1, 2 * D), lambda i: (0, 0))
    return pl.pallas_call(
        body, name="merge_bwd",
        out_shape=(jax.ShapeDtypeStruct((S, D), BF), jax.ShapeDtypeStruct((S, D), BF),
                   jax.ShapeDtypeStruct((S, OUT_A), F32), jax.ShapeDtypeStruct((S, W_B), F32),
                   jax.ShapeDtypeStruct((S, OUT_A), F32), jax.ShapeDtypeStruct((S, 2 * D), BF),
                   jax.ShapeDtypeStruct((1, 2 * D), F32)),
        grid=(S // tm,),
        in_specs=[wide, nar, nar, pl.BlockSpec((tm, D), lambda i: (i, gcol)),
                  pl.BlockSpec((tm, D), lambda i: (i, gcol + 1)), vec2,
                  pl.BlockSpec((OUT_A + W_B, D), lambda i: (0, 0))],
        out_specs=(wide, wide, nar, nar, nar, wide2, vec2),
        compiler_params=_params("arbitrary"),
    )(dm, oa, ob, proj, proj, b_gate.reshape(1, 2 * D), w_br)


_SQRT_HALF = 0.7071067811865476
_INV_SQRT_2PI = 0.3989422804014327


def _gelu_parts(a):
    cdf = 0.5 * (1.0 + lax.erf(a * _SQRT_HALF))
    pdf = _INV_SQRT_2PI * jnp.exp(-0.5 * a * a)
    return cdf, pdf


def _shift_down(a, halo, k):
    rows = lax.broadcasted_iota(jnp.int32, a.shape, 0)
    out = pltpu.roll(a, k, 0)
    for r in range(k):
        out = jnp.where(rows == r, halo[8 - k + r:8 - k + r + 1, :], out)
    return out


def _shift_up(a, halo, k):
    n = a.shape[0]
    rows = lax.broadcasted_iota(jnp.int32, a.shape, 0)
    out = pltpu.roll(a, n - k, 0)
    for r in range(k):
        out = jnp.where(rows == n - k + r, halo[r:r + 1, :], out)
    return out


def _conv_in(a_ref, h_ref, first):
    a = a_ref[...].astype(F32)
    halo = jnp.where(first, 0.0, h_ref[...].astype(F32))
    return a, _shift_down(a, halo, 1), _shift_down(a, halo, 2)


def _ffn_specs(S, tm):
    F = D_FF
    t8 = tm // 8
    a_spec = pl.BlockSpec((tm, F), lambda i: (i, 0))
    v_spec = pl.BlockSpec((tm, F), lambda i: (i, 1))
    halo_prev = pl.BlockSpec((8, F), lambda i: (jnp.maximum(i * t8 - 1, 0), 0))
    return a_spec, v_spec, halo_prev


def _ffn_act_fwd(up, conv_w, conv_b):
    S = up.shape[0]
    F = D_FF
    tm = _pick(S, 256)
    a_spec, v_spec, halo_prev = _ffn_specs(S, tm)

    def body(a_ref, h_ref, v_ref, w_ref, b_ref, act_ref):
        a0, a1, a2 = _conv_in(a_ref, h_ref, pl.program_id(0) == 0)
        w = w_ref[...]
        ac = b_ref[...] + w[0:1, :] * a2 + w[1:2, :] * a1 + w[2:3, :] * a0
        cdf, _ = _gelu_parts(ac)
        act_ref[...] = (ac * cdf * v_ref[...].astype(F32)).astype(BF)

    return pl.pallas_call(
        body, name="ffn_act_fwd",
        out_shape=jax.ShapeDtypeStruct((S, F), BF),
        grid=(S // tm,),
        in_specs=[a_spec, halo_prev, v_spec, pl.BlockSpec((3, F), lambda i: (0, 0)),
                  pl.BlockSpec((1, F), lambda i: (0, 0))],
        out_specs=a_spec,
        compiler_params=_params("parallel"),
    )(up, up, up, conv_w, conv_b.reshape(1, F))


def _ffn_act_bwd1(dact, up, conv_w, conv_b, comm=()):
    S = up.shape[0]
    F = D_FF
    tm = _pick(S, 256)
    a_spec, v_spec, halo_prev = _ffn_specs(S, tm)

    def body(da_ref, a_ref, h_ref, v_ref, w_ref, b_ref, dac_ref, dv_ref, dw_ref, db_ref):
        @pl.when(pl.program_id(0) == 0)
        def _():
            dw_ref[...] = jnp.zeros_like(dw_ref)
            db_ref[...] = jnp.zeros_like(db_ref)

        a0, a1, a2 = _conv_in(a_ref, h_ref, pl.program_id(0) == 0)
        w = w_ref[...]
        ac = b_ref[...] + w[0:1, :] * a2 + w[1:2, :] * a1 + w[2:3, :] * a0
        cdf, pdf = _gelu_parts(ac)
        dact = da_ref[...].astype(F32)
        dv_ref[...] = (dact * ac * cdf).astype(BF)
        dac = dact * v_ref[...].astype(F32) * (cdf + ac * pdf)
        dac_ref[...] = dac.astype(BF)
        db_ref[...] += jnp.sum(dac, axis=0, keepdims=True)
        dw_ref[0:1, :] += jnp.sum(dac * a2, axis=0, keepdims=True)
        dw_ref[1:2, :] += jnp.sum(dac * a1, axis=0, keepdims=True)
        dw_ref[2:3, :] += jnp.sum(dac * a0, axis=0, keepdims=True)

    w_spec = pl.BlockSpec((3, F), lambda i: (0, 0))
    b_spec = pl.BlockSpec((1, F), lambda i: (0, 0))
    res = _call(body, name="ffn_act_bwd1", grid=(S // tm,),
                in_specs=[a_spec, a_spec, halo_prev, v_spec, w_spec, b_spec],
                out_specs=(a_spec, a_spec, w_spec, b_spec),
                out_shape=(jax.ShapeDtypeStruct((S, F), BF), jax.ShapeDtypeStruct((S, F), BF),
                           jax.ShapeDtypeStruct((3, F), F32), jax.ShapeDtypeStruct((1, F), F32)),
                args=(dact, up, up, up, conv_w, conv_b.reshape(1, F)), comm=comm)
    return res[0], res[1], res[2], res[3], list(res[4:])


def _ffn_act_bwd2(dac, dv, conv_w):
    S = dac.shape[0]
    F = D_FF
    tm = _pick(S, 256)
    t8 = tm // 8
    nt = S // tm
    cur = pl.BlockSpec((tm, F), lambda i: (i, 0))
    halo_next = pl.BlockSpec((8, F), lambda i: (jnp.minimum((i + 1) * t8, S // 8 - 1), 0))

    def body(d_ref, h_ref, dv_ref, w_ref, o_ref):
        d0 = d_ref[...].astype(F32)
        halo = jnp.where(pl.program_id(0) == nt - 1, 0.0, h_ref[...].astype(F32))
        d1 = _shift_up(d0, halo, 1)
        d2 = _shift_up(d0, halo, 2)
        w = w_ref[...]
        o_ref[:, 0:F] = (w[2:3, :] * d0 + w[1:2, :] * d1 + w[0:1, :] * d2).astype(BF)
        o_ref[:, F:2 * F] = dv_ref[...]

    return pl.pallas_call(
        body, name="ffn_act_bwd2",
        out_shape=jax.ShapeDtypeStruct((S, 2 * F), BF),
        grid=(nt,),
        in_specs=[cur, halo_next, cur, pl.BlockSpec((3, F), lambda i: (0, 0))],
        out_specs=pl.BlockSpec((tm, 2 * F), lambda i: (i, 0)),
        compiler_params=_params("parallel"),
    )(dac, dac, dv, conv_w)


def _adamw(parts, w, m, v, name):
    R, C = w.shape
    tr = R
    for cand in (512, 256, 128, 64, 32, 16):
        if R % cand == 0 and cand * C * 4 <= (1 << 21):
            tr = cand
            break
    c1 = 1.0 / (1.0 - ADAM_B1 ** ADAM_STEP)
    c2 = 1.0 / (1.0 - ADAM_B2 ** ADAM_STEP)

    def body(p_ref, w_ref, m_ref, v_ref, g_ref, d_ref, nm_ref, nv_ref):
        g = p_ref[0].astype(F32)
        for j in range(1, N_DEV):
            g = g + p_ref[j].astype(F32)
        nm = ADAM_B1 * m_ref[...] + (1.0 - ADAM_B1) * g
        nv = ADAM_B2 * v_ref[...] + (1.0 - ADAM_B2) * (g * g)
        g_ref[...] = g
        nm_ref[...] = nm
        nv_ref[...] = nv
        d_ref[...] = -ADAM_LR * ((nm * c1) / (jnp.sqrt(nv * c2) + ADAM_EPS) + ADAM_WD * w_ref[...])

    blk = pl.BlockSpec((tr, C), lambda i: (i, 0))
    out = jax.ShapeDtypeStruct((R, C), F32)
    return pl.pallas_call(
        body, name=name,
        out_shape=(out, out, out, out),
        grid=(R // tr,),
        in_specs=[pl.BlockSpec((N_DEV, tr, C), lambda i: (0, i, 0)), blk, blk, blk],
        out_specs=(blk, blk, blk, blk),
        compiler_params=_params("parallel"),
    )(parts, w, m, v)


def _dil(t, dil):
    S, C = t.shape
    if dil == 1:
        return t.reshape(1, S, C)
    return t.reshape(S // dil, dil, C).transpose(1, 0, 2)


def _undil(t):
    d, L, C = t.shape
    if d == 1:
        return t.reshape(L, C)
    return t.transpose(1, 0, 2).reshape(L * d, C)


def _group_qkv(proj, g):
    dil = DSW_GROUPS[g][1]
    if dil == 1:
        p3 = _dil(proj, 1)
        return (p3, p3, p3), (g, W_A // OUT_A + g, 2 * W_A // OUT_A + g)
    c0 = g * OUT_A
    return tuple(_dil(proj[:, o + c0:o + c0 + OUT_A], dil) for o in (0, W_A, 2 * W_A)), (0, 0, 0)


_COL_SHARDED = ("w_in", "w_br", "w_up")
_ROW_SHARDED = ("w_o", "w_down")


class _Plan:
    def __init__(self):
        self.riders = {}
        self.landed = {}

    def ride(self, slot, key, kind, x):
        self.riders.setdefault(slot, []).append((key, kind, x))

    def run(self, slot, fn, *args, **kw):
        items = self.riders.pop(slot, [])
        res = fn(*args, comm=[(kind, x) for _, kind, x in items], **kw)
        for (key, _, _), r in zip(items, res[-1]):
            self.landed[key] = r
        return res[0] if len(res) == 2 else res[:-1]

    def weight(self, n, l):
        g = self.landed[(n, l)]
        if n in _COL_SHARDED:
            return g.transpose(1, 0, 2).reshape(g.shape[1], -1)
        return g.reshape(-1, g.shape[2])

    def scatter(self, slot, n, l, full):
        K, N = full.shape
        if n in _COL_SHARDED:
            blocks = full.reshape(K, N_DEV, N // N_DEV).transpose(1, 0, 2)
        else:
            blocks = full.reshape(N_DEV, K // N_DEV, N)
        self.ride(slot, ("d" + n, l), "scatter", blocks)


def _layer_fwd(x, p, plan, l):
    h = _rms_fwd(x, p["norm1"][l], f"rms1_fwd_{l}")
    proj = plan.run(f"proj_{l}", _matmul, h, plan.weight("w_in", l), mode="nn", out_dtype=BF, name=f"proj_{l}",
                    tm=1024, tn=1024, tk=1024)
    o_g, l_g, qkv_g = [], [], []
    for g in range(N_GROUPS):
        qkv, cols = _group_qkv(proj, g)
        og, lg = plan.run(f"attn_a_fwd_g{g}_{l}", _attn_a_fwd, qkv, cols, g)
        o_g.append(_undil(og))
        l_g.append(_undil(lg))
        qkv_g.append((qkv, cols))
    ob = plan.run(f"sb_fwd_{l}", _sb_fwd, proj)
    merged, oa, lse = _merge_fwd(o_g, l_g, ob, proj, p["b_gate"][l], plan.weight("w_br", l))
    x1 = plan.run(f"wo_{l}", _matmul, merged, plan.weight("w_o", l), mode="nn", out_dtype=F32, name=f"wo_{l}",
                  tm=1024, tn=1024, tk=1024, res=x)
    h2 = _rms_fwd(x1, p["norm2"][l], f"rms2_fwd_{l}")
    up = plan.run(f"up_{l}", _matmul, h2, plan.weight("w_up", l), mode="nn", out_dtype=BF, name=f"up_{l}",
                  tm=1024, tn=1408, tk=1024)
    act = _ffn_act_fwd(up, p["conv_w"][l], p["conv_b"][l])
    x2 = plan.run(f"down_{l}", _matmul, act, plan.weight("w_down", l), mode="nn", out_dtype=F32, name=f"down_{l}",
                  tm=1024, tn=1024, tk=D_FF, res=x1)
    saved = dict(x=x, h=h, proj=proj, qkv_g=qkv_g, oa=oa, ob=ob, lse=lse, merged=merged, x1=x1, h2=h2, up=up, act=act)
    return x2, saved


def _layer_bwd(dx2, sv, p, plan, l):
    gr = {}
    dact = plan.run(f"d_act_{l}", _matmul, dx2, plan.weight("w_down", l), mode="nt", out_dtype=BF, name=f"d_act_{l}",
                    tm=1024, tn=1408, tk=1024)
    dwd = plan.run(f"dw_down_{l}", _matmul, sv["act"], dx2, mode="tn", out_dtype=BF, name=f"dw_down_{l}",
                   tm=1408, tn=1024, tk=2048)
    plan.scatter(f"d_h2_{l}", "w_down", l, dwd)
    dac, dv, gr["conv_w"], dcb = plan.run(f"ffn_act_bwd1_{l}", _ffn_act_bwd1, dact, sv["up"], p["conv_w"][l],
                                          p["conv_b"][l])
    gr["conv_b"] = dcb[0]
    dup = _ffn_act_bwd2(dac, dv, p["conv_w"][l])
    dh2 = plan.run(f"d_h2_{l}", _matmul, dup, plan.weight("w_up", l), mode="nt", out_dtype=BF, name=f"d_h2_{l}",
                   tm=512, tn=1024, tk=5632)
    dwu = plan.run(f"dw_up_{l}", _matmul, sv["h2"], dup, mode="tn", out_dtype=BF, name=f"dw_up_{l}",
                   tm=1024, tn=1408, tk=2048)
    plan.scatter(f"sb_bwd_{l}", "w_up", l, dwu)
    dx1, dn2 = _rms_bwd(sv["x1"], p["norm2"][l], dh2, dx2, f"rms2_bwd_{l}")
    gr["norm2"] = dn2[0]
    dm = plan.run(f"d_merged_{l}", _matmul, dx1, plan.weight("w_o", l), mode="nt", out_dtype=BF,
                  name=f"d_merged_{l}", tm=1024, tn=1024, tk=1024)
    dwo = plan.run(f"dw_o_{l}", _matmul, sv["merged"], dx1, mode="tn", out_dtype=BF, name=f"dw_o_{l}",
                   tm=1024, tn=1024, tk=2048)
    plan.scatter(f"d_h_{l}", "w_o", l, dwo)
    dya, dyb, doa, dob, dsum, dgate, dbg = _merge_bwd(dm, sv["oa"], sv["ob"], sv["proj"], p["b_gate"][l],
                                                      plan.weight("w_br", l))
    gr["b_gate"] = dbg[0]
    dwa = plan.run(f"dw_bra_{l}", _matmul, sv["oa"], dya, mode="tn", out_dtype=BF, name=f"dw_bra_{l}",
                   tm=256, tn=1024, tk=2048)
    dwb = plan.run(f"dw_brb_{l}", _matmul, sv["ob"], dyb, mode="tn", out_dtype=BF, name=f"dw_brb_{l}",
                   tm=256, tn=1024, tk=2048)
    plan.scatter(f"d_h_{l}", "w_br", l, jnp.concatenate([dwa, dwb], axis=0))
    proj = sv["proj"]
    dq_a, dk_a, dv_a = [], [], []
    for g, (_, dil) in enumerate(DSW_GROUPS):
        qkv, cols = sv["qkv_g"][g]
        dqg, dkg, dvg = _attn_a_bwd(qkv, cols, _dil(doa, dil), _dil(sv["lse"], dil), _dil(dsum, dil), g)
        dq_a.append(_undil(dqg))
        dk_a.append(_undil(dkg))
        dv_a.append(_undil(dvg))
    dqb, dkb, dvb = plan.run(f"sb_bwd_{l}", _sb_bwd, proj, dob, sv["ob"])
    dproj = jnp.concatenate(dq_a + dk_a + dv_a + [dqb.astype(BF), dkb.astype(BF), dvb.astype(BF), dgate], axis=1)
    dh = plan.run(f"d_h_{l}", _matmul, dproj, plan.weight("w_in", l), mode="nt", out_dtype=BF, name=f"d_h_{l}",
                  tm=512, tn=1024, tk=5120)
    dwi = plan.run(f"dw_in_{l}", _matmul, sv["h"], dproj, mode="tn", out_dtype=BF, name=f"dw_in_{l}",
                   tm=1024, tn=1280, tk=2048)
    plan.scatter(f"ffn_act_bwd1_{l - 1}", "w_in", l, dwi)
    dx, dn1 = _rms_bwd(sv["x"], p["norm1"][l], dh, dx1, f"rms1_bwd_{l}")
    gr["norm1"] = dn1[0]
    return dx, gr


def kernel(x, norm1, w_in, b_gate, w_br, w_o, norm2, w_up, conv_w, conv_b, w_down, norm_f, loss_target, m_norm1, m_w_in, m_b_gate, m_w_br, m_w_o, m_norm2, m_w_up, m_conv_w, m_conv_b, m_w_down, m_norm_f, v_norm1, v_w_in, v_b_gate, v_w_br, v_w_o, v_norm2, v_w_up, v_conv_w, v_conv_b, v_w_down, v_norm_f):
    depth = norm1.shape[0]
    me = 4 * lax.axis_index("x") + 2 * lax.axis_index("y") + lax.axis_index("c")
    shards = dict(w_in=w_in, w_br=w_br, w_o=w_o, w_up=w_up, w_down=w_down)
    moments_m = dict(norm1=m_norm1, w_in=m_w_in, b_gate=m_b_gate, w_br=m_w_br, w_o=m_w_o, norm2=m_norm2,
                     w_up=m_w_up, conv_w=m_conv_w, conv_b=m_conv_b, w_down=m_w_down, norm_f=m_norm_f)
    moments_v = dict(norm1=v_norm1, w_in=v_w_in, b_gate=v_b_gate, w_br=v_w_br, w_o=v_w_o, norm2=v_norm2,
                     w_up=v_w_up, conv_w=v_conv_w, conv_b=v_conv_b, w_down=v_w_down, norm_f=v_norm_f)

    plan = _Plan()
    wb = {n: s.astype(BF) for n, s in shards.items()}
    p = dict(norm1=norm1, b_gate=b_gate, norm2=norm2, conv_b=conv_b)
    cw = _all_gather(conv_w, "gather_conv_w")
    p["conv_w"] = cw.transpose(1, 2, 0, 3).reshape(depth, 3, D_FF)
    plan.landed[("w_in", 0)] = _all_gather(wb["w_in"][0], "gather_w_in_0")
    for l in range(depth):
        plan.ride(f"proj_{l}", ("w_up", l), "gather", wb["w_up"][l])
        plan.ride(f"attn_a_fwd_g0_{l}" if l == 0 else f"down_{l - 1}", ("w_br", l), "gather", wb["w_br"][l])
        plan.ride(f"attn_a_fwd_g0_{l}" if l == 0 else f"down_{l - 1}", ("w_o", l), "gather", wb["w_o"][l])
        plan.ride(f"sb_fwd_{l}", ("w_down", l), "gather", wb["w_down"][l])
        if l + 1 < depth:
            plan.ride(f"up_{l}", ("w_in", l + 1), "gather", wb["w_in"][l + 1])

    xs = x[0]
    saved = []
    for l in range(depth):
        xs, sv = _layer_fwd(xs, p, plan, l)
        saved.append(sv)
    loss_part, dx, dnf = _loss_head(xs, norm_f, loss_target[0])
    loss = lax.psum(loss_part[0, 0], ("x", "y", "c"))

    grads = [None] * depth
    for l in reversed(range(depth)):
        dx, grads[l] = _layer_bwd(dx, saved[l], p, plan, l)
    grad_x = dx[None]
    (_, _, last), = plan.riders.pop("ffn_act_bwd1_-1")
    plan.landed[("dw_in", 0)] = _all_to_all(last, "scatter_w_in_0")
    assert not plan.riders, sorted(plan.riders)

    out_g, out_d, out_m, out_v = {}, {}, {}, {}
    for n in _COL_SHARDED + _ROW_SHARDED:
        parts = jnp.concatenate([plan.landed[("d" + n, l)] for l in range(depth)], axis=1)
        shp = shards[n].shape
        flat = (shp[0] * shp[1], shp[2])
        res = _adamw(parts, shards[n].reshape(flat), moments_m[n].reshape(flat), moments_v[n].reshape(flat),
                     f"adamw_{n}")
        out_g[n], out_d[n], out_m[n], out_v[n] = [r.reshape(shp) for r in res]

    small = ("norm1", "b_gate", "norm2", "conv_b")
    vecs = [jnp.stack([grads[l][n] for l in range(depth)]).reshape(-1) for n in small]
    vecs.append(dnf.reshape(-1))
    vecs.append(jnp.stack([grads[l]["conv_w"] for l in range(depth)]).reshape(-1))
    sizes = [v.shape[0] for v in vecs]
    flat = jnp.concatenate(vecs)
    n_small = sum(sizes[:-1])
    pad = (-flat.shape[0]) % 1024
    flat = jnp.pad(flat, (0, pad)).reshape(-1, 128)
    allp = _all_gather(flat, "gather_small_grads").reshape(N_DEV, -1)
    rep_w = jnp.concatenate([norm1.reshape(-1), b_gate.reshape(-1), norm2.reshape(-1), conv_b.reshape(-1), norm_f])
    rep_m = jnp.concatenate([moments_m[n].reshape(-1) for n in small] + [m_norm_f])
    rep_v = jnp.concatenate([moments_v[n].reshape(-1) for n in small] + [v_norm_f])
    rows = n_small // 128
    res = _adamw(allp[:, :n_small].reshape(N_DEV, rows, 128), rep_w.reshape(rows, 128), rep_m.reshape(rows, 128),
                 rep_v.reshape(rows, 128), "adamw_small")
    off = 0
    for n, sz in zip(small + ("norm_f",), sizes[:-1]):
        shp = norm_f.shape if n == "norm_f" else p[n].shape
        out_g[n], out_d[n], out_m[n], out_v[n] = [r.reshape(-1)[off:off + sz].reshape(shp) for r in res]
        off += sz
    f = conv_w.shape[2]
    cwp = allp[:, n_small:n_small + sizes[-1]].reshape(N_DEV, depth * 3, D_FF)
    cwp = lax.dynamic_slice_in_dim(cwp, me * f, f, axis=2)
    res = _adamw(cwp, conv_w.reshape(depth * 3, f), m_conv_w.reshape(depth * 3, f), v_conv_w.reshape(depth * 3, f),
                 "adamw_conv_w")
    out_g["conv_w"], out_d["conv_w"], out_m["conv_w"], out_v["conv_w"] = [r.reshape(conv_w.shape) for r in res]

    order = ("norm1", "w_in", "b_gate", "w_br", "w_o", "norm2", "w_up", "conv_w", "conv_b", "w_down", "norm_f")
    return (loss, grad_x, *[out_g[n] for n in order], *[out_d[n] for n in order],
            *[out_m[n] for n in order], *[out_v[n] for n in order])
```

```python
import functools

import jax
import jax.numpy as jnp
from jax import lax
from jax.experimental import pallas as pl
from jax.experimental.pallas import tpu as pltpu

BF = jnp.bfloat16
F32 = jnp.float32

N_DEV = 8
D_MODEL = 1024
HEAD_DIM = 64
DSW_GROUPS = ((128, 1), (512, 4), (2048, 16))
HEADS_PER_GROUP = 4
N_GROUPS = len(DSW_GROUPS)
DSW_HEADS = HEADS_PER_GROUP * N_GROUPS
SB_HEADS = 4
W_A = DSW_HEADS * HEAD_DIM
W_B = SB_HEADS * HEAD_DIM
OUT_A = HEADS_PER_GROUP * HEAD_DIM
N_IN = 3 * W_A + 3 * W_B + 2 * D_MODEL
GATE_OFF = 3 * W_A + 3 * W_B
D_FF = 2816
SB_BLOCK = 256
RMS_EPS = 1e-6
ATT_SCALE = HEAD_DIM ** -0.5
NEG = -1e30
SB_EXIT = -110.0

ADAM_LR = 0.001
ADAM_B1 = 0.9
ADAM_B2 = 0.999
ADAM_EPS = 1e-08
ADAM_WD = 0.01
ADAM_STEP = 10

HBM_SPEC = pl.BlockSpec(memory_space=pltpu.HBM)
MESH = pl.DeviceIdType.MESH

_NN = (((1,), (0,)), ((), ()))
_NT = (((1,), (1,)), ((), ()))
_TN = (((0,), (0,)), ((), ()))


def _dot(a, b, dn=_NN):
    return lax.dot_general(a, b, dn, preferred_element_type=F32)


def _pick(dim, pref):
    if dim <= pref:
        return dim
    t = (pref // 128) * 128
    while t >= 128:
        if dim % t == 0:
            return t
        t -= 128
    return dim


def _params(*sem):
    return pltpu.CompilerParams(dimension_semantics=sem)


def _peer(k):
    x, y, c = lax.axis_index("x"), lax.axis_index("y"), lax.axis_index("c")
    px = 1 - x if (k >> 2) & 1 else x
    py = 1 - y if (k >> 1) & 1 else y
    pc = 1 - c if k & 1 else c
    return (px, py, pc), 4 * px + 2 * py + pc


def _exchange(kind, x_ref, out_ref, send_sems, recv_sems, local_sem):
    gather = kind == "gather"
    _, me = _peer(0)

    def src(idx):
        return x_ref if gather else x_ref.at[idx]

    def copy(k, dst_idx):
        peer, pidx = _peer(k)
        return pltpu.make_async_remote_copy(
            src_ref=src(pidx), dst_ref=out_ref.at[dst_idx], send_sem=send_sems.at[k - 1],
            recv_sem=recv_sems.at[k - 1], device_id=peer, device_id_type=MESH)

    mine = pltpu.make_async_copy(src(me), out_ref.at[me], local_sem)

    def start():
        mine.start()
        for k in range(1, N_DEV):
            copy(k, me).start()

    def wait():
        for k in range(1, N_DEV):
            copy(k, _peer(k)[1]).wait_recv()
        for k in range(1, N_DEV):
            copy(k, me).wait_send()
        mine.wait()

    return start, wait


_EXCHANGE_SEMS = [pltpu.SemaphoreType.DMA((N_DEV - 1,)), pltpu.SemaphoreType.DMA((N_DEV - 1,)),
                  pltpu.SemaphoreType.DMA]


def _exchange_shape(kind, x):
    return jax.ShapeDtypeStruct(((N_DEV,) + x.shape) if kind == "gather" else x.shape, x.dtype)


def _exchange_alone(kind, x, name):
    def body(x_ref, out_ref, send_sems, recv_sems, local_sem):
        start, wait = _exchange(kind, x_ref, out_ref, send_sems, recv_sems, local_sem)
        start()
        wait()

    return pl.pallas_call(
        body, name=name, out_shape=_exchange_shape(kind, x),
        in_specs=[HBM_SPEC], out_specs=HBM_SPEC, scratch_shapes=list(_EXCHANGE_SEMS),
    )(x)


def _all_gather(x, name):
    return _exchange_alone("gather", x, name)


def _all_to_all(x, name):
    return _exchange_alone("scatter", x, name)


def _call(body, *, name, grid, in_specs, out_specs, out_shape, args, scratch_shapes=(), sem=None, comm=()):
    single = not isinstance(out_shape, (tuple, list))
    outs = (out_shape,) if single else tuple(out_shape)
    ospecs = (out_specs,) if single else tuple(out_specs)
    if not comm:
        res = pl.pallas_call(
            body, name=name, out_shape=outs, grid=grid, in_specs=list(in_specs), out_specs=ospecs,
            scratch_shapes=list(scratch_shapes), compiler_params=_params(*(sem or ("arbitrary",) * len(grid))),
        )(*args)
        return res
    n_in, n_out, n_scr, nc = len(in_specs), len(outs), len(scratch_shapes), len(comm)

    def wrapped(*refs):
        ins = refs[:n_in]
        cins = refs[n_in:n_in + nc]
        o0 = n_in + nc
        kouts = refs[o0:o0 + n_out]
        couts = refs[o0 + n_out:o0 + n_out + nc]
        s0 = o0 + n_out + nc
        scr = refs[s0:s0 + n_scr]
        sems = refs[s0 + n_scr:]
        ids = [pl.program_id(ax) for ax in range(len(grid))]
        first = functools.reduce(jnp.logical_and, [i == 0 for i in ids])
        last = functools.reduce(jnp.logical_and, [i == g - 1 for i, g in zip(ids, grid)])
        ex = [_exchange(comm[c][0], cins[c], couts[c], *sems[3 * c:3 * c + 3]) for c in range(nc)]

        @pl.when(first)
        def _():
            for start, _ in ex:
                start()

        body(*ins, *kouts, *scr)

        @pl.when(last)
        def _():
            for _, wait in ex:
                wait()

    return pl.pallas_call(
        wrapped, name=name,
        out_shape=outs + tuple(_exchange_shape(k, x) for k, x in comm),
        grid=grid, in_specs=list(in_specs) + [HBM_SPEC] * nc, out_specs=ospecs + (HBM_SPEC,) * nc,
        scratch_shapes=list(scratch_shapes) + list(_EXCHANGE_SEMS) * nc,
        compiler_params=_params(*(("arbitrary",) * len(grid))),
    )(*args, *[x for _, x in comm])


def _matmul(a, b, *, mode, out_dtype, name, tm=512, tn=1024, tk=1024, res=None, comm=()):
    if mode == "nn":
        (M, K), (_, N) = a.shape, b.shape
    elif mode == "nt":
        (M, K), (N, _) = a.shape, b.shape
    else:
        (K, M), (_, N) = a.shape, b.shape
    tm, tn, tk = _pick(M, tm), _pick(N, tn), _pick(K, tk)
    nk = K // tk
    dn = {"nn": _NN, "nt": _NT, "tn": _TN}[mode]

    def body(*refs):
        a_ref, b_ref = refs[0], refs[1]
        r_ref = refs[2] if res is not None else None
        o_ref = refs[3] if res is not None else refs[2]

        def finish(r):
            if res is not None:
                r = r + r_ref[...].astype(F32)
            o_ref[...] = r.astype(out_dtype)

        part = _dot(a_ref[...].astype(BF), b_ref[...].astype(BF), dn)
        if nk == 1:
            finish(part)
            return
        acc = refs[-1]
        k = pl.program_id(2)

        @pl.when(k == 0)
        def _():
            acc[...] = jnp.zeros_like(acc)

        acc[...] += part

        @pl.when(k == nk - 1)
        def _():
            finish(acc[...])

    if mode == "tn":
        a_spec = pl.BlockSpec((tk, tm), lambda j, i, k: (k, i))
    else:
        a_spec = pl.BlockSpec((tm, tk), lambda j, i, k: (i, k))
    if mode == "nt":
        b_spec = pl.BlockSpec((tn, tk), lambda j, i, k: (j, k))
    else:
        b_spec = pl.BlockSpec((tk, tn), lambda j, i, k: (k, j))
    o_spec = pl.BlockSpec((tm, tn), lambda j, i, k: (i, j))
    in_specs = [a_spec, b_spec] + ([o_spec] if res is not None else [])
    args = (a, b) + ((res,) if res is not None else ())
    out = _call(body, name=name, grid=(N // tn, M // tm, nk), in_specs=in_specs, out_specs=o_spec,
                out_shape=jax.ShapeDtypeStruct((M, N), out_dtype), args=args,
                scratch_shapes=[pltpu.VMEM((tm, tn), F32)] if nk > 1 else [],
                sem=("parallel", "parallel", "arbitrary"), comm=comm)
    return out[0], list(out[1:])


def _rms_fwd(x, g, name):
    S, D = x.shape
    tm = _pick(S, 512)

    def body(x_ref, g_ref, h_ref):
        xf = x_ref[...]
        r = lax.rsqrt(jnp.mean(xf * xf, axis=-1, keepdims=True) + RMS_EPS)
        h_ref[...] = (xf * r * g_ref[...]).astype(BF)

    return pl.pallas_call(
        body, name=name,
        out_shape=jax.ShapeDtypeStruct((S, D), BF),
        grid=(S // tm,),
        in_specs=[pl.BlockSpec((tm, D), lambda i: (i, 0)), pl.BlockSpec((1, D), lambda i: (0, 0))],
        out_specs=pl.BlockSpec((tm, D), lambda i: (i, 0)),
        compiler_params=_params("parallel"),
    )(x, g.reshape(1, D))


def _rms_bwd(x, g, dh, dres, name):
    S, D = x.shape
    tm = _pick(S, 512)

    def body(x_ref, g_ref, dh_ref, dres_ref, dx_ref, dg_ref):
        @pl.when(pl.program_id(0) == 0)
        def _():
            dg_ref[...] = jnp.zeros_like(dg_ref)

        xf = x_ref[...]
        r = lax.rsqrt(jnp.mean(xf * xf, axis=-1, keepdims=True) + RMS_EPS)
        xh = xf * r
        dy = dh_ref[...].astype(F32)
        dg_ref[...] += jnp.sum(dy * xh, axis=0, keepdims=True)
        dxh = dy * g_ref[...]
        dx = r * (dxh - xh * jnp.mean(dxh * xh, axis=-1, keepdims=True))
        dx_ref[...] = dres_ref[...] + dx

    row = pl.BlockSpec((tm, D), lambda i: (i, 0))
    vec = pl.BlockSpec((1, D), lambda i: (0, 0))
    return pl.pallas_call(
        body, name=name,
        out_shape=(jax.ShapeDtypeStruct((S, D), F32), jax.ShapeDtypeStruct((1, D), F32)),
        grid=(S // tm,),
        in_specs=[row, vec, row, row], out_specs=(row, vec),
        compiler_params=_params("arbitrary"),
    )(x, g.reshape(1, D), dh, dres)


def _loss_head(x, g, target):
    S, D = x.shape
    tm = _pick(S, 512)

    def body(x_ref, g_ref, t_ref, loss_ref, dx_ref, dg_ref):
        @pl.when(pl.program_id(0) == 0)
        def _():
            dg_ref[...] = jnp.zeros_like(dg_ref)
            loss_ref[...] = jnp.zeros_like(loss_ref)

        xf = x_ref[...]
        gg = g_ref[...]
        r = lax.rsqrt(jnp.mean(xf * xf, axis=-1, keepdims=True) + RMS_EPS)
        xh = xf * r
        err = xh * gg - t_ref[...]
        per_tok = jnp.mean(err * err, axis=-1, keepdims=True)
        loss_ref[...] += 0.5 * jnp.sum(per_tok, axis=0, keepdims=True)
        dy = err * (1.0 / D)
        dg_ref[...] += jnp.sum(dy * xh, axis=0, keepdims=True)
        dxh = dy * gg
        dx_ref[...] = r * (dxh - xh * jnp.mean(dxh * xh, axis=-1, keepdims=True))

    row = pl.BlockSpec((tm, D), lambda i: (i, 0))
    vec = pl.BlockSpec((1, D), lambda i: (0, 0))
    one = pl.BlockSpec((1, 1), lambda i: (0, 0))
    return pl.pallas_call(
        body, name="loss_head",
        out_shape=(jax.ShapeDtypeStruct((1, 1), F32), jax.ShapeDtypeStruct((S, D), F32),
                   jax.ShapeDtypeStruct((1, D), F32)),
        grid=(S // tm,),
        in_specs=[row, vec, row], out_specs=(one, row, vec),
        compiler_params=_params("arbitrary"),
    )(x, g.reshape(1, D), target)


def _slopes(g):
    return [2.0 ** (-8.0 * (HEADS_PER_GROUP * g + j + 1) / DSW_HEADS) for j in range(HEADS_PER_GROUP)]


def _band_masks(W):
    row = lax.broadcasted_iota(jnp.int32, (W, W), 0)
    col = lax.broadcasted_iota(jnp.int32, (W, W), 1)
    d_cur = row - col
    d_prev = d_cur + W
    return d_cur, d_prev, d_cur >= 0, d_cur <= 0


def _band_specs(W, nb, per):
    def cur(c):
        return pl.BlockSpec((None, per * W, OUT_A), lambda r, n: (r, n, c))

    def prev(c):
        return pl.BlockSpec((None, W, OUT_A), lambda r, n: (r, jnp.maximum(per * n - 1, 0), c))

    def nxt(c):
        return pl.BlockSpec((None, W, OUT_A), lambda r, n: (r, jnp.minimum(per * (n + 1), nb - 1), c))

    return cur, prev, nxt


def _blocks_per_step(nb):
    return 2 if nb % 2 == 0 else 1


def _head_stack(W):
    H, hd = HEADS_PER_GROUP, HEAD_DIM
    lane_head = lax.broadcasted_iota(jnp.int32, (W, OUT_A), 1) // hd

    def stack(x):
        return jnp.concatenate([jnp.where(lane_head == h, x, jnp.zeros_like(x)) for h in range(H)], axis=0)

    def unstack(y):
        out = jnp.where(lane_head == 0, y[0:W], 0.0)
        for h in range(1, H):
            out = jnp.where(lane_head == h, y[h * W:(h + 1) * W], out)
        return out

    def column(ref, rows, off=0):
        return jnp.concatenate([ref[rows, h * hd + off:h * hd + off + 1] for h in range(H)], axis=0)

    def tile(x):
        return jnp.concatenate([x] * H, axis=0)

    return stack, unstack, column, tile


def _stacked_bias(W, slopes, dil):
    d_cur, d_prev, m_cur, m_prev = _band_masks(W)
    b_cur = jnp.concatenate([(s * dil) * d_cur.astype(F32) for s in slopes], axis=0)
    b_prev = jnp.concatenate([(s * dil) * d_prev.astype(F32) for s in slopes], axis=0)
    H = len(slopes)
    return b_cur, b_prev, jnp.concatenate([m_cur] * H, axis=0), jnp.concatenate([m_prev] * H, axis=0)


def _attn_a_fwd(qkv, cols, g, comm=()):
    win, dil = DSW_GROUPS[g]
    W = win // dil
    d, L, _ = qkv[0].shape
    nb = L // W
    per = _blocks_per_step(nb)
    slopes = _slopes(g)

    def body(q_ref, kp_ref, kc_ref, vp_ref, vc_ref, o_ref, l_ref):
        n = pl.program_id(1)
        stack, unstack, _, _ = _head_stack(W)
        b_cur, b_prev, m_cur, m_prev = _stacked_bias(W, slopes, dil)
        m_first = jnp.logical_and(m_prev, n > 0)
        for b in range(per):
            rows = slice(b * W, (b + 1) * W)
            before = slice((b - 1) * W, b * W)
            qs = stack(q_ref[rows, :])
            kc, vc = kc_ref[rows, :], vc_ref[rows, :]
            kp, vp = (kp_ref[...], vp_ref[...]) if b == 0 else (kc_ref[before, :], vc_ref[before, :])
            s_c = jnp.where(m_cur, _dot(qs, kc, _NT) * ATT_SCALE - b_cur, NEG)
            s_p = jnp.where(m_first if b == 0 else m_prev, _dot(qs, kp, _NT) * ATT_SCALE - b_prev, NEG)
            m = jnp.maximum(jnp.max(s_c, axis=1, keepdims=True), jnp.max(s_p, axis=1, keepdims=True))
            p_c = jnp.exp(s_c - m)
            p_p = jnp.exp(s_p - m)
            den = jnp.sum(p_c, axis=1, keepdims=True) + jnp.sum(p_p, axis=1, keepdims=True)
            pv = _dot(p_c.astype(BF), vc) + _dot(p_p.astype(BF), vp)
            o_ref[rows, :] = unstack(pv / den)
            l_ref[rows, :] = unstack(jnp.broadcast_to(m + jnp.log(den), pv.shape))

    cur, prev, _ = _band_specs(W, nb, per)
    out = jax.ShapeDtypeStruct((d, L, OUT_A), F32)
    res = _call(body, name=f"attn_a_fwd_g{g}", grid=(d, nb // per),
                in_specs=[cur(cols[0]), prev(cols[1]), cur(cols[1]), prev(cols[2]), cur(cols[2])],
                out_specs=(cur(0), cur(0)), out_shape=(out, out),
                args=(qkv[0], qkv[1], qkv[1], qkv[2], qkv[2]), sem=("parallel", "parallel"), comm=comm)
    return res[0], res[1], list(res[2:])


def _attn_a_bwd(qkv, cols, do, stats, g):
    win, dil = DSW_GROUPS[g]
    W = win // dil
    d, L, _ = qkv[0].shape
    nb = L // W
    per = _blocks_per_step(nb)
    nsteps = nb // per
    slopes = _slopes(g)

    def body(q_ref, qn_ref, kp_ref, kc_ref, vp_ref, vc_ref, do_ref, don_ref, st_ref, stn_ref,
             dq_ref, dk_ref, dv_ref):
        n = pl.program_id(1)
        stack, unstack, column, _ = _head_stack(W)
        b_cur, b_prev, m_cur, m_prev = _stacked_bias(W, slopes, dil)
        m_first = jnp.logical_and(m_prev, n > 0)
        m_last = jnp.logical_and(m_prev, n < nsteps - 1)
        everything = slice(None)
        for b in range(per):
            rows = slice(b * W, (b + 1) * W)
            before = slice((b - 1) * W, b * W)
            after = slice((b + 1) * W, (b + 2) * W)
            first, last = b == 0, b == per - 1
            qs = stack(q_ref[rows, :])
            qn = stack(qn_ref[...] if last else q_ref[after, :])
            dos = stack(do_ref[rows, :])
            don = stack(don_ref[...] if last else do_ref[after, :])
            kc, vc = kc_ref[rows, :], vc_ref[rows, :]
            kp, vp = (kp_ref[...], vp_ref[...]) if first else (kc_ref[before, :], vc_ref[before, :])
            lse_c, dsum_c = column(st_ref, rows), column(st_ref, rows, STAT_OFF)
            lse_n = column(stn_ref, everything) if last else column(st_ref, after)
            dsum_n = column(stn_ref, everything, STAT_OFF) if last else column(st_ref, after, STAT_OFF)
            m_p = m_first if first else m_prev
            m_n = m_last if last else m_prev
            p_cc = jnp.exp(jnp.where(m_cur, _dot(qs, kc, _NT) * ATT_SCALE - b_cur, NEG) - lse_c)
            p_cp = jnp.exp(jnp.where(m_p, _dot(qs, kp, _NT) * ATT_SCALE - b_prev, NEG) - lse_c)
            p_nc = jnp.exp(jnp.where(m_n, _dot(qn, kc, _NT) * ATT_SCALE - b_prev, NEG) - lse_n)
            ds_cc = (p_cc * (_dot(dos, vc, _NT) - dsum_c) * ATT_SCALE).astype(BF)
            ds_cp = (p_cp * (_dot(dos, vp, _NT) - dsum_c) * ATT_SCALE).astype(BF)
            ds_nc = (p_nc * (_dot(don, vc, _NT) - dsum_n) * ATT_SCALE).astype(BF)
            dq_ref[rows, :] = unstack(_dot(ds_cc, kc) + _dot(ds_cp, kp)).astype(BF)
            dk_ref[rows, :] = (_dot(ds_cc, qs, _TN) + _dot(ds_nc, qn, _TN)).astype(BF)
            dv_ref[rows, :] = (_dot(p_cc.astype(BF), dos, _TN) + _dot(p_nc.astype(BF), don, _TN)).astype(BF)

    cur, prev, nxt = _band_specs(W, nb, per)
    out = jax.ShapeDtypeStruct((d, L, OUT_A), BF)
    cq, ck, cv = cols
    return pl.pallas_call(
        body, name=f"attn_a_bwd_g{g}",
        out_shape=(out, out, out),
        grid=(d, nsteps),
        in_specs=[cur(cq), nxt(cq), prev(ck), cur(ck), prev(cv), cur(cv), cur(0), nxt(0), cur(0), nxt(0)],
        out_specs=(cur(0), cur(0), cur(0)),
        compiler_params=_params("parallel", "parallel"),
    )(qkv[0], qkv[0], qkv[1], qkv[1], qkv[2], qkv[2], do, do, stats, stats)


SB_PAIR = 2
SB_QROWS = SB_BLOCK


def _softplus_parts(z):
    e = jnp.exp(-jnp.abs(z))
    log1p_e = jnp.where(e < 1e-4, e, jnp.log(1.0 + e))
    return e, jnp.maximum(z, 0.0) + log1p_e


def _split_dot(x, t):
    hi = x.astype(BF)
    lo = (x - hi.astype(F32)).astype(BF)
    return _dot(hi, t) + _dot(lo, t)


def _sb_block(qh, kk, causal, r_run, tri_incl):
    z = _dot(qh, kk, _NT)
    e, sp = _softplus_parts(z)
    ls = jnp.where(causal, -sp, 0.0)
    cin = _split_dot(ls, tri_incl)
    a = jnp.where(causal, jnp.exp(z + cin + r_run), 0.0)
    return z, e, cin, a


def _sb_specs(S):
    Q, hd = SB_QROWS, HEAD_DIM
    lanes = SB_PAIR * hd
    qc = (3 * W_A) // lanes
    kc = (3 * W_A + W_B) // lanes
    vc = (3 * W_A + 2 * W_B) // lanes
    q_spec = pl.BlockSpec((Q, lanes), lambda p, i: (i, qc + p))
    k_spec = pl.BlockSpec((S, lanes), lambda p, i: (0, kc + p))
    v_spec = pl.BlockSpec((S, lanes), lambda p, i: (0, vc + p))
    o_spec = pl.BlockSpec((Q, lanes), lambda p, i: (i, p))
    full = pl.BlockSpec((S, lanes), lambda p, i: (0, p))
    return q_spec, k_spec, v_spec, o_spec, full


def _sb_stack():
    Q, hd = SB_QROWS, HEAD_DIM
    lane_head = lax.broadcasted_iota(jnp.int32, (Q, SB_PAIR * hd), 1) // hd

    def stack(x):
        return jnp.concatenate([jnp.where(lane_head == h, x, jnp.zeros_like(x)) for h in range(SB_PAIR)], axis=0)

    def unstack(y):
        out = jnp.where(lane_head == 0, y[0:Q], 0.0)
        for h in range(1, SB_PAIR):
            out = jnp.where(lane_head == h, y[h * Q:(h + 1) * Q], out)
        return out

    return stack, unstack


def _sb_iotas(i):
    B, Q = SB_BLOCK, SB_QROWS
    row = lax.broadcasted_iota(jnp.int32, (Q, B), 0) + i * Q
    col = lax.broadcasted_iota(jnp.int32, (Q, B), 1)
    ahead = jnp.concatenate([col - row] * SB_PAIR, axis=0)
    tr = lax.broadcasted_iota(jnp.int32, (B, B), 0)
    tc = lax.broadcasted_iota(jnp.int32, (B, B), 1)
    return ahead, tr, tc


def _sb_fwd(proj, comm=()):
    S = proj.shape[0]
    B, Q, hd = SB_BLOCK, SB_QROWS, HEAD_DIM
    nq = S // Q
    R = SB_PAIR * Q
    q_spec, k_spec, v_spec, o_spec, _ = _sb_specs(S)

    def body(q_ref, k_ref, v_ref, o_ref):
        i = pl.program_id(1)
        ahead, tr, tc = _sb_iotas(i)
        tri_incl = (tr >= tc).astype(BF)
        stack, unstack = _sb_stack()
        qs = stack(q_ref[...] * ATT_SCALE)

        def cond(c):
            return jnp.logical_and(c[0] >= 0, c[-1] > SB_EXIT)

        def step(c):
            kb, r_run, acc, _ = c
            off = pl.multiple_of(kb * B, B)
            causal = ahead < -kb * B
            _, _, cin, a = _sb_block(qs, k_ref[pl.ds(off, B), :], causal, r_run, tri_incl)
            acc = acc + _dot(a.astype(BF), v_ref[pl.ds(off, B), :])
            r_run = r_run + cin[:, 0:1]
            return kb - 1, r_run, acc, jnp.max(r_run)

        init = (i, jnp.zeros((R, 1), F32), jnp.zeros((R, SB_PAIR * hd), F32), jnp.float32(0.0))
        fin = lax.while_loop(cond, step, init)
        o_ref[...] = unstack(fin[2])

    res = _call(body, name="sb_fwd", grid=(SB_HEADS // SB_PAIR, nq), in_specs=[q_spec, k_spec, v_spec],
                out_specs=o_spec, out_shape=jax.ShapeDtypeStruct((S, W_B), F32), args=(proj, proj, proj),
                sem=("parallel", "parallel"), comm=comm)
    return res[0], list(res[1:])


def _sb_bwd(proj, do, o, comm=()):
    S = proj.shape[0]
    B, Q, hd = SB_BLOCK, SB_QROWS, HEAD_DIM
    nq = S // Q
    R = SB_PAIR * Q
    q_spec, k_spec, v_spec, o_spec, full = _sb_specs(S)

    def body(q_ref, k_ref, v_ref, do_ref, o_ref, dq_ref, dk_ref, dv_ref):
        i = pl.program_id(1)

        @pl.when(i == 0)
        def _():
            dk_ref[...] = jnp.zeros_like(dk_ref)
            dv_ref[...] = jnp.zeros_like(dv_ref)

        ahead, tr, tc = _sb_iotas(i)
        tri_incl = (tr >= tc).astype(BF)
        tri_strict = (tr > tc).astype(BF)
        stack, unstack = _sb_stack()
        qs = stack(q_ref[...] * ATT_SCALE)
        dobs = stack(do_ref[...])
        o_all = o_ref[...]
        dsum = jnp.sum(dobs.astype(F32) * jnp.concatenate([o_all] * SB_PAIR, axis=0), axis=1, keepdims=True)

        def cond(c):
            return jnp.logical_and(c[0] >= 0, c[-1] > SB_EXIT)

        def step(c):
            kb, r_run, g_run, dq, _ = c
            off = pl.multiple_of(kb * B, B)
            causal = ahead < -kb * B
            kk = k_ref[pl.ds(off, B), :]
            vv = v_ref[pl.ds(off, B), :]
            z, e, cin, a = _sb_block(qs, kk, causal, r_run, tri_incl)
            a16 = a.astype(BF)
            gmat = a16.astype(F32) * _dot(dobs, vv, _NT)
            later = _split_dot(gmat, tri_strict)
            pfx = dsum - g_run - later
            sig = jnp.where(z >= 0, 1.0, e) / (1.0 + e)
            dz = jnp.where(causal, gmat - sig * pfx, 0.0).astype(BF)
            dq = dq + _dot(dz, kk)
            dk_ref[pl.ds(off, B), :] += _dot(dz, qs, _TN)
            dv_ref[pl.ds(off, B), :] += _dot(a16, dobs, _TN)
            g_run = g_run + jnp.sum(gmat, axis=1, keepdims=True)
            r_run = r_run + cin[:, 0:1]
            return kb - 1, r_run, g_run, dq, jnp.max(r_run)

        init = (i, jnp.zeros((R, 1), F32), jnp.zeros((R, 1), F32), jnp.zeros((R, SB_PAIR * hd), F32),
                jnp.float32(0.0))
        fin = lax.while_loop(cond, step, init)
        dq_ref[...] = unstack(fin[3]) * ATT_SCALE

    out = jax.ShapeDtypeStruct((S, W_B), F32)
    res = _call(body, name="sb_bwd", grid=(SB_HEADS // SB_PAIR, nq), in_specs=[q_spec, k_spec, v_spec, o_spec, o_spec],
                out_specs=(o_spec, full, full), out_shape=(out, out, out), args=(proj, proj, proj, do, o), comm=comm)
    return res[0], res[1], res[2], list(res[3:])


def _merge_fwd(o_g, l_g, o_b, proj, b_gate, w_br):
    S = o_b.shape[0]
    D = D_MODEL
    tm = _pick(S, 256)
    gcol = GATE_OFF // D

    def body(o0, o1, o2, l0, l1, l2, ob_ref, ga_ref, gb_ref, bg_ref, w_ref, mg_ref, oa_ref, lse_ref):
        la, lb, lc = l0[...], l1[...], l2[...]
        mx = jnp.maximum(jnp.maximum(la, lb), lc)
        ea, eb, ec = jnp.exp(la - mx), jnp.exp(lb - mx), jnp.exp(lc - mx)
        den = ea + eb + ec
        oa = (ea * o0[...] + eb * o1[...] + ec * o2[...]) / den
        oa_ref[...] = oa
        lse_ref[...] = mx + jnp.log(den)
        ya = _dot(oa.astype(BF), w_ref[0:OUT_A, :])
        yb = _dot(ob_ref[...].astype(BF), w_ref[OUT_A:OUT_A + W_B, :])
        bg = bg_ref[...]
        g_a = jax.nn.sigmoid(ga_ref[...].astype(F32) + bg[:, 0:D])
        g_b = jax.nn.sigmoid(gb_ref[...].astype(F32) + bg[:, D:2 * D])
        mg_ref[...] = (g_a * ya + g_b * yb).astype(BF)

    nar = pl.BlockSpec((tm, OUT_A), lambda i: (i, 0))
    wide = pl.BlockSpec((tm, D), lambda i: (i, 0))
    return pl.pallas_call(
        body, name="merge_fwd",
        out_shape=(jax.ShapeDtypeStruct((S, D), BF), jax.ShapeDtypeStruct((S, OUT_A), F32),
                   jax.ShapeDtypeStruct((S, OUT_A), F32)),
        grid=(S // tm,),
        in_specs=[nar] * 7 + [pl.BlockSpec((tm, D), lambda i: (i, gcol)),
                              pl.BlockSpec((tm, D), lambda i: (i, gcol + 1)),
                              pl.BlockSpec((1, 2 * D), lambda i: (0, 0)),
                              pl.BlockSpec((OUT_A + W_B, D), lambda i: (0, 0))],
        out_specs=(wide, nar, nar),
        compiler_params=_params("parallel"),
    )(*o_g, *l_g, o_b, proj, proj, b_gate.reshape(1, 2 * D), w_br)


STAT_OFF = HEAD_DIM // 2


def _merge_bwd(dm, oa, ob, lse, proj, b_gate, w_br):
    S = ob.shape[0]
    D = D_MODEL
    tm = _pick(S, 256)
    gcol = GATE_OFF // D

    def body(dm_ref, oa_ref, ob_ref, l_ref, ga_ref, gb_ref, bg_ref, w_ref,
             dya_ref, dyb_ref, doa_ref, dob_ref, st_ref, dg_ref, dbg_ref):
        @pl.when(pl.program_id(0) == 0)
        def _():
            dbg_ref[...] = jnp.zeros_like(dbg_ref)

        wa = w_ref[0:OUT_A, :]
        wb = w_ref[OUT_A:OUT_A + W_B, :]
        oa = oa_ref[...]
        ya = _dot(oa.astype(BF), wa)
        yb = _dot(ob_ref[...].astype(BF), wb)
        bg = bg_ref[...]
        g_a = jax.nn.sigmoid(ga_ref[...].astype(F32) + bg[:, 0:D])
        g_b = jax.nn.sigmoid(gb_ref[...].astype(F32) + bg[:, D:2 * D])
        dm = dm_ref[...].astype(F32)
        dga = dm * ya * g_a * (1.0 - g_a)
        dgb = dm * yb * g_b * (1.0 - g_b)
        dg_ref[:, 0:D] = dga.astype(BF)
        dg_ref[:, D:2 * D] = dgb.astype(BF)
        dbg_ref[:, 0:D] += jnp.sum(dga, axis=0, keepdims=True)
        dbg_ref[:, D:2 * D] += jnp.sum(dgb, axis=0, keepdims=True)
        dya = (dm * g_a).astype(BF)
        dyb = (dm * g_b).astype(BF)
        dya_ref[...] = dya
        dyb_ref[...] = dyb
        doa = _dot(dya, wa, _NT).astype(BF)
        doa_ref[...] = doa
        dob_ref[...] = _dot(dyb, wb, _NT).astype(BF)
        r = lax.broadcasted_iota(jnp.int32, (OUT_A, OUT_A), 0) // HEAD_DIM
        c = lax.broadcasted_iota(jnp.int32, (OUT_A, OUT_A), 1) // HEAD_DIM
        dsum = _split_dot(doa.astype(F32) * oa, (r == c).astype(BF))
        lane = lax.broadcasted_iota(jnp.int32, dsum.shape, 1) % HEAD_DIM
        st_ref[...] = jnp.where(lane < STAT_OFF, l_ref[...], dsum)

    nar = pl.BlockSpec((tm, OUT_A), lambda i: (i, 0))
    wide = pl.BlockSpec((tm, D), lambda i: (i, 0))
    wide2 = pl.BlockSpec((tm, 2 * D), lambda i: (i, 0))
    vec2 = pl.BlockSpec((1, 2 * D), lambda i: (0, 0))
    return pl.pallas_call(
        body, name="merge_bwd",
        out_shape=(jax.ShapeDtypeStruct((S, D), BF), jax.ShapeDtypeStruct((S, D), BF),
                   jax.ShapeDtypeStruct((S, OUT_A), BF), jax.ShapeDtypeStruct((S, W_B), BF),
                   jax.ShapeDtypeStruct((S, OUT_A), F32), jax.ShapeDtypeStruct((S, 2 * D), BF),
                   jax.ShapeDtypeStruct((1, 2 * D), F32)),
        grid=(S // tm,),
        in_specs=[wide, nar, nar, nar, pl.BlockSpec((tm, D), lambda i: (i, gcol)),
                  pl.BlockSpec((tm, D), lambda i: (i, gcol + 1)), vec2,
                  pl.BlockSpec((OUT_A + W_B, D), lambda i: (0, 0))],
        out_specs=(wide, wide, nar, nar, nar, wide2, vec2),
        compiler_params=_params("arbitrary"),
    )(dm, oa, ob, lse, proj, proj, b_gate.reshape(1, 2 * D), w_br)


_SQRT_HALF = 0.7071067811865476
_INV_SQRT_2PI = 0.3989422804014327


def _gelu_parts(a):
    cdf = 0.5 * (1.0 + lax.erf(a * _SQRT_HALF))
    pdf = _INV_SQRT_2PI * jnp.exp(-0.5 * a * a)
    return cdf, pdf


def _shift_down(a, halo, k):
    rows = lax.broadcasted_iota(jnp.int32, a.shape, 0)
    out = pltpu.roll(a, k, 0)
    for r in range(k):
        out = jnp.where(rows == r, halo[8 - k + r:8 - k + r + 1, :], out)
    return out


def _shift_up(a, halo, k):
    n = a.shape[0]
    rows = lax.broadcasted_iota(jnp.int32, a.shape, 0)
    out = pltpu.roll(a, n - k, 0)
    for r in range(k):
        out = jnp.where(rows == n - k + r, halo[r:r + 1, :], out)
    return out


def _conv_in(a_ref, h_ref, first):
    a = a_ref[...].astype(F32)
    halo = jnp.where(first, 0.0, h_ref[...].astype(F32))
    return a, _shift_down(a, halo, 1), _shift_down(a, halo, 2)


def _ffn_specs(S, tm):
    F = D_FF
    t8 = tm // 8
    a_spec = pl.BlockSpec((tm, F), lambda i: (i, 0))
    v_spec = pl.BlockSpec((tm, F), lambda i: (i, 1))
    halo_prev = pl.BlockSpec((8, F), lambda i: (jnp.maximum(i * t8 - 1, 0), 0))
    return a_spec, v_spec, halo_prev


def _ffn_act_fwd(up, conv_w, conv_b):
    S = up.shape[0]
    F = D_FF
    tm = _pick(S, 256)
    a_spec, v_spec, halo_prev = _ffn_specs(S, tm)

    def body(a_ref, h_ref, v_ref, w_ref, b_ref, act_ref):
        a0, a1, a2 = _conv_in(a_ref, h_ref, pl.program_id(0) == 0)
        w = w_ref[...]
        ac = b_ref[...] + w[0:1, :] * a2 + w[1:2, :] * a1 + w[2:3, :] * a0
        cdf, _ = _gelu_parts(ac)
        act_ref[...] = (ac * cdf * v_ref[...].astype(F32)).astype(BF)

    return pl.pallas_call(
        body, name="ffn_act_fwd",
        out_shape=jax.ShapeDtypeStruct((S, F), BF),
        grid=(S // tm,),
        in_specs=[a_spec, halo_prev, v_spec, pl.BlockSpec((3, F), lambda i: (0, 0)),
                  pl.BlockSpec((1, F), lambda i: (0, 0))],
        out_specs=a_spec,
        compiler_params=_params("parallel"),
    )(up, up, up, conv_w, conv_b.reshape(1, F))


def _ffn_act_bwd(dact, up, conv_w, conv_b, comm=()):
    S = up.shape[0]
    F = D_FF
    tm = _pick(S, 256)
    t8 = tm // 8
    nt = S // tm
    a_spec, v_spec, halo_prev = _ffn_specs(S, tm)

    def nxt(col):
        return pl.BlockSpec((8, F), lambda i: (jnp.minimum((i + 1) * t8, S // 8 - 1), col))

    def body(da_ref, dan_ref, a_ref, h_ref, an_ref, v_ref, vn_ref, w_ref, b_ref, dup_ref, dw_ref, db_ref):
        i = pl.program_id(0)

        @pl.when(i == 0)
        def _():
            dw_ref[...] = jnp.zeros_like(dw_ref)
            db_ref[...] = jnp.zeros_like(db_ref)

        w = w_ref[...]
        bias = b_ref[...]

        def conv_grad(a0, a1, a2, dact, v):
            ac = bias + w[0:1, :] * a2 + w[1:2, :] * a1 + w[2:3, :] * a0
            cdf, pdf = _gelu_parts(ac)
            return ac, cdf, dact * v * (cdf + ac * pdf)

        a0, a1, a2 = _conv_in(a_ref, h_ref, i == 0)
        dact = da_ref[...].astype(F32)
        ac, cdf, dac = conv_grad(a0, a1, a2, dact, v_ref[...].astype(F32))
        dup_ref[:, F:2 * F] = (dact * ac * cdf).astype(BF)
        db_ref[...] += jnp.sum(dac, axis=0, keepdims=True)
        dw_ref[0:1, :] += jnp.sum(dac * a2, axis=0, keepdims=True)
        dw_ref[1:2, :] += jnp.sum(dac * a1, axis=0, keepdims=True)
        dw_ref[2:3, :] += jnp.sum(dac * a0, axis=0, keepdims=True)
        n0 = an_ref[...].astype(F32)
        tail = a_ref[tm - 8:tm, :].astype(F32)
        _, _, dac_n = conv_grad(n0, _shift_down(n0, tail, 1), _shift_down(n0, tail, 2),
                                dan_ref[...].astype(F32), vn_ref[...].astype(F32))
        dac_n = jnp.where(i == nt - 1, 0.0, dac_n)
        d1 = _shift_up(dac, dac_n, 1)
        d2 = _shift_up(dac, dac_n, 2)
        dup_ref[:, 0:F] = (w[2:3, :] * dac + w[1:2, :] * d1 + w[0:1, :] * d2).astype(BF)

    w_spec = pl.BlockSpec((3, F), lambda i: (0, 0))
    b_spec = pl.BlockSpec((1, F), lambda i: (0, 0))
    res = _call(body, name="ffn_act_bwd", grid=(nt,),
                in_specs=[a_spec, nxt(0), a_spec, halo_prev, nxt(0), v_spec, nxt(1), w_spec, b_spec],
                out_specs=(pl.BlockSpec((tm, 2 * F), lambda i: (i, 0)), w_spec, b_spec),
                out_shape=(jax.ShapeDtypeStruct((S, 2 * F), BF), jax.ShapeDtypeStruct((3, F), F32),
                           jax.ShapeDtypeStruct((1, F), F32)),
                args=(dact, dact, up, up, up, up, up, conv_w, conv_b.reshape(1, F)), comm=comm)
    return res[0], res[1], res[2], list(res[3:])


def _adamw(parts, w, m, v, name):
    R, C = w.shape
    tr = R
    for cand in (512, 256, 128, 64, 32, 16):
        if R % cand == 0 and cand * C * 4 <= (1 << 21):
            tr = cand
            break
    c1 = 1.0 / (1.0 - ADAM_B1 ** ADAM_STEP)
    c2 = 1.0 / (1.0 - ADAM_B2 ** ADAM_STEP)

    def body(p_ref, w_ref, m_ref, v_ref, g_ref, d_ref, nm_ref, nv_ref):
        g = p_ref[0].astype(F32)
        for j in range(1, N_DEV):
            g = g + p_ref[j].astype(F32)
        nm = ADAM_B1 * m_ref[...] + (1.0 - ADAM_B1) * g
        nv = ADAM_B2 * v_ref[...] + (1.0 - ADAM_B2) * (g * g)
        g_ref[...] = g
        nm_ref[...] = nm
        nv_ref[...] = nv
        d_ref[...] = -ADAM_LR * ((nm * c1) / (jnp.sqrt(nv * c2) + ADAM_EPS) + ADAM_WD * w_ref[...])

    blk = pl.BlockSpec((tr, C), lambda i: (i, 0))
    out = jax.ShapeDtypeStruct((R, C), F32)
    return pl.pallas_call(
        body, name=name,
        out_shape=(out, out, out, out),
        grid=(R // tr,),
        in_specs=[pl.BlockSpec((N_DEV, tr, C), lambda i: (0, i, 0)), blk, blk, blk],
        out_specs=(blk, blk, blk, blk),
        compiler_params=_params("parallel"),
    )(parts, w, m, v)


def _dil(t, dil):
    S, C = t.shape
    if dil == 1:
        return t.reshape(1, S, C)
    return t.reshape(S // dil, dil, C).transpose(1, 0, 2)


def _undil(t):
    d, L, C = t.shape
    if d == 1:
        return t.reshape(L, C)
    return t.transpose(1, 0, 2).reshape(L * d, C)


def _group_qkv(proj, g):
    dil = DSW_GROUPS[g][1]
    if dil == 1:
        p3 = _dil(proj, 1)
        return (p3, p3, p3), (g, W_A // OUT_A + g, 2 * W_A // OUT_A + g)
    c0 = g * OUT_A
    return tuple(_dil(proj[:, o + c0:o + c0 + OUT_A], dil) for o in (0, W_A, 2 * W_A)), (0, 0, 0)


_COL_SHARDED = ("w_in", "w_br", "w_up")
_ROW_SHARDED = ("w_o", "w_down")


class _Plan:
    def __init__(self):
        self.riders = {}
        self.landed = {}

    def ride(self, slot, key, kind, x):
        self.riders.setdefault(slot, []).append((key, kind, x))

    def run(self, slot, fn, *args, **kw):
        items = self.riders.pop(slot, [])
        res = fn(*args, comm=[(kind, x) for _, kind, x in items], **kw)
        for (key, _, _), r in zip(items, res[-1]):
            self.landed[key] = r
        return res[0] if len(res) == 2 else res[:-1]

    def weight(self, n, l):
        g = self.landed[(n, l)]
        if n in _COL_SHARDED:
            return g.transpose(1, 0, 2).reshape(g.shape[1], -1)
        return g.reshape(-1, g.shape[2])

    def scatter(self, slot, n, l, full):
        K, N = full.shape
        if n in _COL_SHARDED:
            blocks = full.reshape(K, N_DEV, N // N_DEV).transpose(1, 0, 2)
        else:
            blocks = full.reshape(N_DEV, K // N_DEV, N)
        self.ride(slot, ("d" + n, l), "scatter", blocks)


def _layer_fwd(x, p, plan, l):
    h = _rms_fwd(x, p["norm1"][l], f"rms1_fwd_{l}")
    proj = plan.run(f"proj_{l}", _matmul, h, plan.weight("w_in", l), mode="nn", out_dtype=BF, name=f"proj_{l}",
                    tm=1024, tn=1024, tk=1024)
    o_g, l_g, qkv_g = [], [], []
    for g in range(N_GROUPS):
        qkv, cols = _group_qkv(proj, g)
        og, lg = plan.run(f"attn_a_fwd_g{g}_{l}", _attn_a_fwd, qkv, cols, g)
        o_g.append(_undil(og))
        l_g.append(_undil(lg))
        qkv_g.append((qkv, cols))
    ob = plan.run(f"sb_fwd_{l}", _sb_fwd, proj)
    merged, oa, lse = _merge_fwd(o_g, l_g, ob, proj, p["b_gate"][l], plan.weight("w_br", l))
    x1 = plan.run(f"wo_{l}", _matmul, merged, plan.weight("w_o", l), mode="nn", out_dtype=F32, name=f"wo_{l}",
                  tm=1024, tn=1024, tk=1024, res=x)
    h2 = _rms_fwd(x1, p["norm2"][l], f"rms2_fwd_{l}")
    up = plan.run(f"up_{l}", _matmul, h2, plan.weight("w_up", l), mode="nn", out_dtype=BF, name=f"up_{l}",
                  tm=1024, tn=1408, tk=1024)
    act = _ffn_act_fwd(up, p["conv_w"][l], p["conv_b"][l])
    x2 = plan.run(f"down_{l}", _matmul, act, plan.weight("w_down", l), mode="nn", out_dtype=F32, name=f"down_{l}",
                  tm=1024, tn=1024, tk=D_FF, res=x1)
    saved = dict(x=x, h=h, proj=proj, qkv_g=qkv_g, oa=oa, ob=ob, lse=lse, merged=merged, x1=x1, h2=h2, up=up, act=act)
    return x2, saved


def _layer_bwd(dx2, sv, p, plan, l):
    gr = {}
    dact = plan.run(f"d_act_{l}", _matmul, dx2, plan.weight("w_down", l), mode="nt", out_dtype=BF, name=f"d_act_{l}",
                    tm=1024, tn=1408, tk=1024)
    dwd = plan.run(f"dw_down_{l}", _matmul, sv["act"], dx2, mode="tn", out_dtype=BF, name=f"dw_down_{l}",
                   tm=1408, tn=1024, tk=2048)
    plan.scatter(f"d_h2_{l}", "w_down", l, dwd)
    dup, gr["conv_w"], dcb = plan.run(f"ffn_act_bwd_{l}", _ffn_act_bwd, dact, sv["up"], p["conv_w"][l], p["conv_b"][l])
    gr["conv_b"] = dcb[0]
    dh2 = plan.run(f"d_h2_{l}", _matmul, dup, plan.weight("w_up", l), mode="nt", out_dtype=BF, name=f"d_h2_{l}",
                   tm=512, tn=1024, tk=5632)
    dwu = plan.run(f"dw_up_{l}", _matmul, sv["h2"], dup, mode="tn", out_dtype=BF, name=f"dw_up_{l}",
                   tm=1024, tn=1408, tk=2048)
    plan.scatter(f"sb_bwd_{l}", "w_up", l, dwu)
    dx1, dn2 = _rms_bwd(sv["x1"], p["norm2"][l], dh2, dx2, f"rms2_bwd_{l}")
    gr["norm2"] = dn2[0]
    dm = plan.run(f"d_merged_{l}", _matmul, dx1, plan.weight("w_o", l), mode="nt", out_dtype=BF,
                  name=f"d_merged_{l}", tm=1024, tn=1024, tk=1024)
    dwo = plan.run(f"dw_o_{l}", _matmul, sv["merged"], dx1, mode="tn", out_dtype=BF, name=f"dw_o_{l}",
                   tm=1024, tn=1024, tk=2048)
    plan.scatter(f"d_h_{l}", "w_o", l, dwo)
    dya, dyb, doa, dob, stats, dgate, dbg = _merge_bwd(dm, sv["oa"], sv["ob"], sv["lse"], sv["proj"], p["b_gate"][l],
                                                       plan.weight("w_br", l))
    gr["b_gate"] = dbg[0]
    dwa = plan.run(f"dw_bra_{l}", _matmul, sv["oa"], dya, mode="tn", out_dtype=BF, name=f"dw_bra_{l}",
                   tm=256, tn=1024, tk=2048)
    dwb = plan.run(f"dw_brb_{l}", _matmul, sv["ob"], dyb, mode="tn", out_dtype=BF, name=f"dw_brb_{l}",
                   tm=256, tn=1024, tk=2048)
    plan.scatter(f"d_h_{l}", "w_br", l, jnp.concatenate([dwa, dwb], axis=0))
    proj = sv["proj"]
    dq_a, dk_a, dv_a = [], [], []
    for g, (_, dil) in enumerate(DSW_GROUPS):
        qkv, cols = sv["qkv_g"][g]
        dqg, dkg, dvg = _attn_a_bwd(qkv, cols, _dil(doa, dil), _dil(stats, dil), g)
        dq_a.append(_undil(dqg))
        dk_a.append(_undil(dkg))
        dv_a.append(_undil(dvg))
    dqb, dkb, dvb = plan.run(f"sb_bwd_{l}", _sb_bwd, proj, dob, sv["ob"])
    dproj = jnp.concatenate(dq_a + dk_a + dv_a + [dqb.astype(BF), dkb.astype(BF), dvb.astype(BF), dgate], axis=1)
    dh = plan.run(f"d_h_{l}", _matmul, dproj, plan.weight("w_in", l), mode="nt", out_dtype=BF, name=f"d_h_{l}",
                  tm=512, tn=1024, tk=5120)
    dwi = plan.run(f"dw_in_{l}", _matmul, sv["h"], dproj, mode="tn", out_dtype=BF, name=f"dw_in_{l}",
                   tm=1024, tn=1280, tk=2048)
    plan.scatter(f"ffn_act_bwd_{l - 1}", "w_in", l, dwi)
    dx, dn1 = _rms_bwd(sv["x"], p["norm1"][l], dh, dx1, f"rms1_bwd_{l}")
    gr["norm1"] = dn1[0]
    return dx, gr


def kernel(x, norm1, w_in, b_gate, w_br, w_o, norm2, w_up, conv_w, conv_b, w_down, norm_f, loss_target, m_norm1, m_w_in, m_b_gate, m_w_br, m_w_o, m_norm2, m_w_up, m_conv_w, m_conv_b, m_w_down, m_norm_f, v_norm1, v_w_in, v_b_gate, v_w_br, v_w_o, v_norm2, v_w_up, v_conv_w, v_conv_b, v_w_down, v_norm_f):
    depth = norm1.shape[0]
    me = 4 * lax.axis_index("x") + 2 * lax.axis_index("y") + lax.axis_index("c")
    shards = dict(w_in=w_in, w_br=w_br, w_o=w_o, w_up=w_up, w_down=w_down)
    moments_m = dict(norm1=m_norm1, w_in=m_w_in, b_gate=m_b_gate, w_br=m_w_br, w_o=m_w_o, norm2=m_norm2,
                     w_up=m_w_up, conv_w=m_conv_w, conv_b=m_conv_b, w_down=m_w_down, norm_f=m_norm_f)
    moments_v = dict(norm1=v_norm1, w_in=v_w_in, b_gate=v_b_gate, w_br=v_w_br, w_o=v_w_o, norm2=v_norm2,
                     w_up=v_w_up, conv_w=v_conv_w, conv_b=v_conv_b, w_down=v_w_down, norm_f=v_norm_f)

    plan = _Plan()
    wb = {n: s.astype(BF) for n, s in shards.items()}
    p = dict(norm1=norm1, b_gate=b_gate, norm2=norm2, conv_b=conv_b)
    cw = _all_gather(conv_w, "gather_conv_w")
    p["conv_w"] = cw.transpose(1, 2, 0, 3).reshape(depth, 3, D_FF)
    plan.landed[("w_in", 0)] = _all_gather(wb["w_in"][0], "gather_w_in_0")
    for l in range(depth):
        plan.ride(f"proj_{l}", ("w_down", l), "gather", wb["w_down"][l])
        plan.ride(f"attn_a_fwd_g0_{l}" if l == 0 else f"down_{l - 1}", ("w_br", l), "gather", wb["w_br"][l])
        plan.ride(f"attn_a_fwd_g0_{l}" if l == 0 else f"down_{l - 1}", ("w_o", l), "gather", wb["w_o"][l])
        plan.ride(f"sb_fwd_{l}", ("w_up", l), "gather", wb["w_up"][l])
        if l + 1 < depth:
            plan.ride(f"up_{l}", ("w_in", l + 1), "gather", wb["w_in"][l + 1])

    xs = x[0]
    saved = []
    for l in range(depth):
        xs, sv = _layer_fwd(xs, p, plan, l)
        saved.append(sv)
    loss_part, dx, dnf = _loss_head(xs, norm_f, loss_target[0])
    loss = lax.psum(loss_part[0, 0], ("x", "y", "c"))

    grads = [None] * depth
    for l in reversed(range(depth)):
        dx, grads[l] = _layer_bwd(dx, saved[l], p, plan, l)
    grad_x = dx[None]
    (_, _, last), = plan.riders.pop("ffn_act_bwd_-1")
    plan.landed[("dw_in", 0)] = _all_to_all(last, "scatter_w_in_0")
    assert not plan.riders, sorted(plan.riders)

    out_g, out_d, out_m, out_v = {}, {}, {}, {}
    for n in _COL_SHARDED + _ROW_SHARDED:
        parts = jnp.concatenate([plan.landed[("d" + n, l)] for l in range(depth)], axis=1)
        shp = shards[n].shape
        flat = (shp[0] * shp[1], shp[2])
        res = _adamw(parts, shards[n].reshape(flat), moments_m[n].reshape(flat), moments_v[n].reshape(flat),
                     f"adamw_{n}")
        out_g[n], out_d[n], out_m[n], out_v[n] = [r.reshape(shp) for r in res]

    small = ("norm1", "b_gate", "norm2", "conv_b")
    vecs = [jnp.stack([grads[l][n] for l in range(depth)]).reshape(-1) for n in small]
    vecs.append(dnf.reshape(-1))
    vecs.append(jnp.stack([grads[l]["conv_w"] for l in range(depth)]).reshape(-1))
    sizes = [v.shape[0] for v in vecs]
    flat = jnp.concatenate(vecs)
    n_small = sum(sizes[:-1])
    pad = (-flat.shape[0]) % 1024
    flat = jnp.pad(flat, (0, pad)).reshape(-1, 128)
    allp = _all_gather(flat, "gather_small_grads").reshape(N_DEV, -1)
    rep_w = jnp.concatenate([norm1.reshape(-1), b_gate.reshape(-1), norm2.reshape(-1), conv_b.reshape(-1), norm_f])
    rep_m = jnp.concatenate([moments_m[n].reshape(-1) for n in small] + [m_norm_f])
    rep_v = jnp.concatenate([moments_v[n].reshape(-1) for n in small] + [v_norm_f])
    rows = n_small // 128
    res = _adamw(allp[:, :n_small].reshape(N_DEV, rows, 128), rep_w.reshape(rows, 128), rep_m.reshape(rows, 128),
                 rep_v.reshape(rows, 128), "adamw_small")
    off = 0
    for n, sz in zip(small + ("norm_f",), sizes[:-1]):
        shp = norm_f.shape if n == "norm_f" else p[n].shape
        out_g[n], out_d[n], out_m[n], out_v[n] = [r.reshape(-1)[off:off + sz].reshape(shp) for r in res]
        off += sz
    f = conv_w.shape[2]
    cwp = allp[:, n_small:n_small + sizes[-1]].reshape(N_DEV, depth * 3, D_FF)
    cwp = lax.dynamic_slice_in_dim(cwp, me * f, f, axis=2)
    res = _adamw(cwp, conv_w.reshape(depth * 3, f), m_conv_w.reshape(depth * 3, f), v_conv_w.reshape(depth * 3, f),
                 "adamw_conv_w")
    out_g["conv_w"], out_d["conv_w"], out_m["conv_w"], out_v["conv_w"] = [r.reshape(conv_w.shape) for r in res]

    order = ("norm1", "w_in", "b_gate", "w_br", "w_o", "norm2", "w_up", "conv_w", "conv_b", "w_down", "norm_f")
    return (loss, grad_x, *[out_g[n] for n in order], *[out_d[n] for n in order],
            *[out_m[n] for n in order], *[out_v[n] for n in order])
```

```python
import functools

import jax
import jax.numpy as jnp
from jax import lax
from jax.experimental import pallas as pl
from jax.experimental.pallas import tpu as pltpu

BF = jnp.bfloat16
F32 = jnp.float32

N_DEV = 8
D_MODEL = 1024
HEAD_DIM = 64
DSW_GROUPS = ((128, 1), (512, 4), (2048, 16))
HEADS_PER_GROUP = 4
N_GROUPS = len(DSW_GROUPS)
DSW_HEADS = HEADS_PER_GROUP * N_GROUPS
SB_HEADS = 4
W_A = DSW_HEADS * HEAD_DIM
W_B = SB_HEADS * HEAD_DIM
OUT_A = HEADS_PER_GROUP * HEAD_DIM
N_IN = 3 * W_A + 3 * W_B + 2 * D_MODEL
GATE_OFF = 3 * W_A + 3 * W_B
D_FF = 2816
SB_BLOCK = 256
RMS_EPS = 1e-6
ATT_SCALE = HEAD_DIM ** -0.5
NEG = -1e30
SB_EXIT = -110.0

ADAM_LR = 0.001
ADAM_B1 = 0.9
ADAM_B2 = 0.999
ADAM_EPS = 1e-08
ADAM_WD = 0.01
ADAM_STEP = 10

HBM_SPEC = pl.BlockSpec(memory_space=pltpu.HBM)
MESH = pl.DeviceIdType.MESH

_NN = (((1,), (0,)), ((), ()))
_NT = (((1,), (1,)), ((), ()))
_TN = (((0,), (0,)), ((), ()))


def _dot(a, b, dn=_NN):
    return lax.dot_general(a, b, dn, preferred_element_type=F32)


def _pick(dim, pref):
    if dim <= pref:
        return dim
    t = (pref // 128) * 128
    while t >= 128:
        if dim % t == 0:
            return t
        t -= 128
    return dim


def _params(*sem):
    return pltpu.CompilerParams(dimension_semantics=sem)


def _peer(k):
    x, y, c = lax.axis_index("x"), lax.axis_index("y"), lax.axis_index("c")
    px = 1 - x if (k >> 2) & 1 else x
    py = 1 - y if (k >> 1) & 1 else y
    pc = 1 - c if k & 1 else c
    return (px, py, pc), 4 * px + 2 * py + pc


def _exchange(kind, x_ref, out_ref, send_sems, recv_sems, local_sem):
    gather = kind == "gather"
    _, me = _peer(0)

    def src(idx):
        return x_ref if gather else x_ref.at[idx]

    def copy(k, dst_idx):
        peer, pidx = _peer(k)
        return pltpu.make_async_remote_copy(
            src_ref=src(pidx), dst_ref=out_ref.at[dst_idx], send_sem=send_sems.at[k - 1],
            recv_sem=recv_sems.at[k - 1], device_id=peer, device_id_type=MESH)

    mine = pltpu.make_async_copy(src(me), out_ref.at[me], local_sem)

    def start():
        mine.start()
        for k in range(1, N_DEV):
            copy(k, me).start()

    def wait():
        for k in range(1, N_DEV):
            copy(k, _peer(k)[1]).wait_recv()
        for k in range(1, N_DEV):
            copy(k, me).wait_send()
        mine.wait()

    return start, wait


_EXCHANGE_SEMS = [pltpu.SemaphoreType.DMA((N_DEV - 1,)), pltpu.SemaphoreType.DMA((N_DEV - 1,)),
                  pltpu.SemaphoreType.DMA]


def _exchange_shape(kind, x):
    return jax.ShapeDtypeStruct(((N_DEV,) + x.shape) if kind == "gather" else x.shape, x.dtype)


def _exchange_alone(kind, x, name):
    def body(x_ref, out_ref, send_sems, recv_sems, local_sem):
        start, wait = _exchange(kind, x_ref, out_ref, send_sems, recv_sems, local_sem)
        start()
        wait()

    return pl.pallas_call(
        body, name=name, out_shape=_exchange_shape(kind, x),
        in_specs=[HBM_SPEC], out_specs=HBM_SPEC, scratch_shapes=list(_EXCHANGE_SEMS),
    )(x)


def _all_gather(x, name):
    return _exchange_alone("gather", x, name)


def _all_to_all(x, name):
    return _exchange_alone("scatter", x, name)


def _call(body, *, name, grid, in_specs, out_specs, out_shape, args, scratch_shapes=(), sem=None, comm=()):
    single = not isinstance(out_shape, (tuple, list))
    outs = (out_shape,) if single else tuple(out_shape)
    ospecs = (out_specs,) if single else tuple(out_specs)
    if not comm:
        res = pl.pallas_call(
            body, name=name, out_shape=outs, grid=grid, in_specs=list(in_specs), out_specs=ospecs,
            scratch_shapes=list(scratch_shapes), compiler_params=_params(*(sem or ("arbitrary",) * len(grid))),
        )(*args)
        return res
    n_in, n_out, n_scr, nc = len(in_specs), len(outs), len(scratch_shapes), len(comm)

    def wrapped(*refs):
        ins = refs[:n_in]
        cins = refs[n_in:n_in + nc]
        o0 = n_in + nc
        kouts = refs[o0:o0 + n_out]
        couts = refs[o0 + n_out:o0 + n_out + nc]
        s0 = o0 + n_out + nc
        scr = refs[s0:s0 + n_scr]
        sems = refs[s0 + n_scr:]
        ids = [pl.program_id(ax) for ax in range(len(grid))]
        first = functools.reduce(jnp.logical_and, [i == 0 for i in ids])
        last = functools.reduce(jnp.logical_and, [i == g - 1 for i, g in zip(ids, grid)])
        ex = [_exchange(comm[c][0], cins[c], couts[c], *sems[3 * c:3 * c + 3]) for c in range(nc)]

        @pl.when(first)
        def _():
            for start, _ in ex:
                start()

        body(*ins, *kouts, *scr)

        @pl.when(last)
        def _():
            for _, wait in ex:
                wait()

    return pl.pallas_call(
        wrapped, name=name,
        out_shape=outs + tuple(_exchange_shape(k, x) for k, x in comm),
        grid=grid, in_specs=list(in_specs) + [HBM_SPEC] * nc, out_specs=ospecs + (HBM_SPEC,) * nc,
        scratch_shapes=list(scratch_shapes) + list(_EXCHANGE_SEMS) * nc,
        compiler_params=_params(*(("arbitrary",) * len(grid))),
    )(*args, *[x for _, x in comm])


def _matmul(a, b, *, mode, out_dtype, name, tm=512, tn=1024, tk=1024, res=None, comm=()):
    if mode == "nn":
        (M, K), (_, N) = a.shape, b.shape
    elif mode == "nt":
        (M, K), (N, _) = a.shape, b.shape
    else:
        (K, M), (_, N) = a.shape, b.shape
    tm, tn, tk = _pick(M, tm), _pick(N, tn), _pick(K, tk)
    nk = K // tk
    dn = {"nn": _NN, "nt": _NT, "tn": _TN}[mode]

    def body(*refs):
        a_ref, b_ref = refs[0], refs[1]
        r_ref = refs[2] if res is not None else None
        o_ref = refs[3] if res is not None else refs[2]

        def finish(r):
            if res is not None:
                r = r + r_ref[...].astype(F32)
            o_ref[...] = r.astype(out_dtype)

        part = _dot(a_ref[...].astype(BF), b_ref[...].astype(BF), dn)
        if nk == 1:
            finish(part)
            return
        acc = refs[-1]
        k = pl.program_id(2)

        @pl.when(k == 0)
        def _():
            acc[...] = jnp.zeros_like(acc)

        acc[...] += part

        @pl.when(k == nk - 1)
        def _():
            finish(acc[...])

    if mode == "tn":
        a_spec = pl.BlockSpec((tk, tm), lambda j, i, k: (k, i))
    else:
        a_spec = pl.BlockSpec((tm, tk), lambda j, i, k: (i, k))
    if mode == "nt":
        b_spec = pl.BlockSpec((tn, tk), lambda j, i, k: (j, k))
    else:
        b_spec = pl.BlockSpec((tk, tn), lambda j, i, k: (k, j))
    o_spec = pl.BlockSpec((tm, tn), lambda j, i, k: (i, j))
    in_specs = [a_spec, b_spec] + ([o_spec] if res is not None else [])
    args = (a, b) + ((res,) if res is not None else ())
    out = _call(body, name=name, grid=(N // tn, M // tm, nk), in_specs=in_specs, out_specs=o_spec,
                out_shape=jax.ShapeDtypeStruct((M, N), out_dtype), args=args,
                scratch_shapes=[pltpu.VMEM((tm, tn), F32)] if nk > 1 else [],
                sem=("parallel", "parallel", "arbitrary"), comm=comm)
    return out[0], list(out[1:])


def _rms_fwd(x, g, name):
    S, D = x.shape
    tm = _pick(S, 512)

    def body(x_ref, g_ref, h_ref):
        xf = x_ref[...]
        r = lax.rsqrt(jnp.mean(xf * xf, axis=-1, keepdims=True) + RMS_EPS)
        h_ref[...] = (xf * r * g_ref[...]).astype(BF)

    return pl.pallas_call(
        body, name=name,
        out_shape=jax.ShapeDtypeStruct((S, D), BF),
        grid=(S // tm,),
        in_specs=[pl.BlockSpec((tm, D), lambda i: (i, 0)), pl.BlockSpec((1, D), lambda i: (0, 0))],
        out_specs=pl.BlockSpec((tm, D), lambda i: (i, 0)),
        compiler_params=_params("parallel"),
    )(x, g.reshape(1, D))


def _rms_bwd(x, g, dh, dres, name):
    S, D = x.shape
    tm = _pick(S, 512)

    def body(x_ref, g_ref, dh_ref, dres_ref, dx_ref, dg_ref):
        @pl.when(pl.program_id(0) == 0)
        def _():
            dg_ref[...] = jnp.zeros_like(dg_ref)

        xf = x_ref[...]
        r = lax.rsqrt(jnp.mean(xf * xf, axis=-1, keepdims=True) + RMS_EPS)
        xh = xf * r
        dy = dh_ref[...].astype(F32)
        dg_ref[...] += jnp.sum(dy * xh, axis=0, keepdims=True)
        dxh = dy * g_ref[...]
        dx = r * (dxh - xh * jnp.mean(dxh * xh, axis=-1, keepdims=True))
        dx_ref[...] = dres_ref[...] + dx

    row = pl.BlockSpec((tm, D), lambda i: (i, 0))
    vec = pl.BlockSpec((1, D), lambda i: (0, 0))
    return pl.pallas_call(
        body, name=name,
        out_shape=(jax.ShapeDtypeStruct((S, D), F32), jax.ShapeDtypeStruct((1, D), F32)),
        grid=(S // tm,),
        in_specs=[row, vec, row, row], out_specs=(row, vec),
        compiler_params=_params("arbitrary"),
    )(x, g.reshape(1, D), dh, dres)


def _matmul_rms_bwd(dy, w, x, g, dres, *, name, tm=512, comm=()):
    S, K = dy.shape
    D = w.shape[0]
    tm = _pick(S, tm)

    def body(dy_ref, w_ref, x_ref, g_ref, dres_ref, dx_ref, dg_ref):
        @pl.when(pl.program_id(0) == 0)
        def _():
            dg_ref[...] = jnp.zeros_like(dg_ref)

        dh = _dot(dy_ref[...].astype(BF), w_ref[...], _NT)
        xf = x_ref[...]
        r = lax.rsqrt(jnp.mean(xf * xf, axis=-1, keepdims=True) + RMS_EPS)
        xh = xf * r
        dg_ref[...] += jnp.sum(dh * xh, axis=0, keepdims=True)
        dxh = dh * g_ref[...]
        dx_ref[...] = dres_ref[...] + r * (dxh - xh * jnp.mean(dxh * xh, axis=-1, keepdims=True))

    row = pl.BlockSpec((tm, D), lambda i: (i, 0))
    vec = pl.BlockSpec((1, D), lambda i: (0, 0))
    res = _call(body, name=name, grid=(S // tm,),
                in_specs=[pl.BlockSpec((tm, K), lambda i: (i, 0)), pl.BlockSpec((D, K), lambda i: (0, 0)), row, vec, row],
                out_specs=(row, vec),
                out_shape=(jax.ShapeDtypeStruct((S, D), F32), jax.ShapeDtypeStruct((1, D), F32)),
                args=(dy, w, x, g.reshape(1, D), dres), comm=comm)
    return res[0], res[1], list(res[2:])


def _loss_head(x, g, target):
    S, D = x.shape
    tm = _pick(S, 512)

    def body(x_ref, g_ref, t_ref, loss_ref, dx_ref, dg_ref):
        @pl.when(pl.program_id(0) == 0)
        def _():
            dg_ref[...] = jnp.zeros_like(dg_ref)
            loss_ref[...] = jnp.zeros_like(loss_ref)

        xf = x_ref[...]
        gg = g_ref[...]
        r = lax.rsqrt(jnp.mean(xf * xf, axis=-1, keepdims=True) + RMS_EPS)
        xh = xf * r
        err = xh * gg - t_ref[...]
        per_tok = jnp.mean(err * err, axis=-1, keepdims=True)
        loss_ref[...] += 0.5 * jnp.sum(per_tok, axis=0, keepdims=True)
        dy = err * (1.0 / D)
        dg_ref[...] += jnp.sum(dy * xh, axis=0, keepdims=True)
        dxh = dy * gg
        dx_ref[...] = r * (dxh - xh * jnp.mean(dxh * xh, axis=-1, keepdims=True))

    row = pl.BlockSpec((tm, D), lambda i: (i, 0))
    vec = pl.BlockSpec((1, D), lambda i: (0, 0))
    one = pl.BlockSpec((1, 1), lambda i: (0, 0))
    return pl.pallas_call(
        body, name="loss_head",
        out_shape=(jax.ShapeDtypeStruct((1, 1), F32), jax.ShapeDtypeStruct((S, D), F32),
                   jax.ShapeDtypeStruct((1, D), F32)),
        grid=(S // tm,),
        in_specs=[row, vec, row], out_specs=(one, row, vec),
        compiler_params=_params("arbitrary"),
    )(x, g.reshape(1, D), target)


def _slopes(g):
    return [2.0 ** (-8.0 * (HEADS_PER_GROUP * g + j + 1) / DSW_HEADS) for j in range(HEADS_PER_GROUP)]


def _band_masks(W):
    row = lax.broadcasted_iota(jnp.int32, (W, W), 0)
    col = lax.broadcasted_iota(jnp.int32, (W, W), 1)
    d_cur = row - col
    d_prev = d_cur + W
    return d_cur, d_prev, d_cur >= 0, d_cur <= 0


def _band_specs(W, nb, per):
    def cur(c):
        return pl.BlockSpec((None, per * W, OUT_A), lambda r, n: (r, n, c))

    def prev(c):
        return pl.BlockSpec((None, W, OUT_A), lambda r, n: (r, jnp.maximum(per * n - 1, 0), c))

    def nxt(c):
        return pl.BlockSpec((None, W, OUT_A), lambda r, n: (r, jnp.minimum(per * (n + 1), nb - 1), c))

    return cur, prev, nxt


def _blocks_per_step(nb):
    return 2 if nb % 2 == 0 else 1


def _head_stack(W):
    H, hd = HEADS_PER_GROUP, HEAD_DIM
    lane_head = lax.broadcasted_iota(jnp.int32, (W, OUT_A), 1) // hd

    def stack(x):
        return jnp.concatenate([jnp.where(lane_head == h, x, jnp.zeros_like(x)) for h in range(H)], axis=0)

    def unstack(y):
        out = jnp.where(lane_head == 0, y[0:W], 0.0)
        for h in range(1, H):
            out = jnp.where(lane_head == h, y[h * W:(h + 1) * W], out)
        return out

    def column(ref, rows, off=0):
        return jnp.concatenate([ref[rows, h * hd + off:h * hd + off + 1] for h in range(H)], axis=0)

    def tile(x):
        return jnp.concatenate([x] * H, axis=0)

    return stack, unstack, column, tile


def _stacked_bias(W, slopes, dil):
    d_cur, d_prev, m_cur, m_prev = _band_masks(W)
    b_cur = jnp.concatenate([(s * dil) * d_cur.astype(F32) for s in slopes], axis=0)
    b_prev = jnp.concatenate([(s * dil) * d_prev.astype(F32) for s in slopes], axis=0)
    H = len(slopes)
    return b_cur, b_prev, jnp.concatenate([m_cur] * H, axis=0), jnp.concatenate([m_prev] * H, axis=0)


def _attn_a_fwd(qkv, cols, g, comm=()):
    win, dil = DSW_GROUPS[g]
    W = win // dil
    d, L, _ = qkv[0].shape
    nb = L // W
    per = _blocks_per_step(nb)
    slopes = _slopes(g)

    def body(q_ref, kp_ref, kc_ref, vp_ref, vc_ref, o_ref, l_ref):
        n = pl.program_id(1)
        stack, unstack, _, _ = _head_stack(W)
        b_cur, b_prev, m_cur, m_prev = _stacked_bias(W, slopes, dil)
        m_first = jnp.logical_and(m_prev, n > 0)
        for b in range(per):
            rows = slice(b * W, (b + 1) * W)
            before = slice((b - 1) * W, b * W)
            qs = stack(q_ref[rows, :])
            kc, vc = kc_ref[rows, :], vc_ref[rows, :]
            kp, vp = (kp_ref[...], vp_ref[...]) if b == 0 else (kc_ref[before, :], vc_ref[before, :])
            s_c = jnp.where(m_cur, _dot(qs, kc, _NT) * ATT_SCALE - b_cur, NEG)
            s_p = jnp.where(m_first if b == 0 else m_prev, _dot(qs, kp, _NT) * ATT_SCALE - b_prev, NEG)
            m = jnp.maximum(jnp.max(s_c, axis=1, keepdims=True), jnp.max(s_p, axis=1, keepdims=True))
            p_c = jnp.exp(s_c - m)
            p_p = jnp.exp(s_p - m)
            den = jnp.sum(p_c, axis=1, keepdims=True) + jnp.sum(p_p, axis=1, keepdims=True)
            pv = _dot(p_c.astype(BF), vc) + _dot(p_p.astype(BF), vp)
            o_ref[rows, :] = unstack(pv / den)
            l_ref[rows, :] = unstack(jnp.broadcast_to(m + jnp.log(den), pv.shape))

    cur, prev, _ = _band_specs(W, nb, per)
    out = jax.ShapeDtypeStruct((d, L, OUT_A), F32)
    res = _call(body, name=f"attn_a_fwd_g{g}", grid=(d, nb // per),
                in_specs=[cur(cols[0]), prev(cols[1]), cur(cols[1]), prev(cols[2]), cur(cols[2])],
                out_specs=(cur(0), cur(0)), out_shape=(out, out),
                args=(qkv[0], qkv[1], qkv[1], qkv[2], qkv[2]), sem=("parallel", "parallel"), comm=comm)
    return res[0], res[1], list(res[2:])


def _attn_a_bwd(qkv, cols, do, stats, g):
    win, dil = DSW_GROUPS[g]
    W = win // dil
    d, L, _ = qkv[0].shape
    nb = L // W
    per = _blocks_per_step(nb)
    nsteps = nb // per
    slopes = _slopes(g)

    def body(q_ref, qn_ref, kp_ref, kc_ref, vp_ref, vc_ref, do_ref, don_ref, st_ref, stn_ref,
             dq_ref, dk_ref, dv_ref):
        n = pl.program_id(1)
        stack, unstack, column, _ = _head_stack(W)
        b_cur, b_prev, m_cur, m_prev = _stacked_bias(W, slopes, dil)
        m_first = jnp.logical_and(m_prev, n > 0)
        m_last = jnp.logical_and(m_prev, n < nsteps - 1)
        everything = slice(None)
        for b in range(per):
            rows = slice(b * W, (b + 1) * W)
            before = slice((b - 1) * W, b * W)
            after = slice((b + 1) * W, (b + 2) * W)
            first, last = b == 0, b == per - 1
            qs = stack(q_ref[rows, :])
            qn = stack(qn_ref[...] if last else q_ref[after, :])
            dos = stack(do_ref[rows, :])
            don = stack(don_ref[...] if last else do_ref[after, :])
            kc, vc = kc_ref[rows, :], vc_ref[rows, :]
            kp, vp = (kp_ref[...], vp_ref[...]) if first else (kc_ref[before, :], vc_ref[before, :])
            lse_c, dsum_c = column(st_ref, rows), column(st_ref, rows, STAT_OFF)
            lse_n = column(stn_ref, everything) if last else column(st_ref, after)
            dsum_n = column(stn_ref, everything, STAT_OFF) if last else column(st_ref, after, STAT_OFF)
            m_p = m_first if first else m_prev
            m_n = m_last if last else m_prev
            p_cc = jnp.exp(jnp.where(m_cur, _dot(qs, kc, _NT) * ATT_SCALE - b_cur, NEG) - lse_c)
            p_cp = jnp.exp(jnp.where(m_p, _dot(qs, kp, _NT) * ATT_SCALE - b_prev, NEG) - lse_c)
            p_nc = jnp.exp(jnp.where(m_n, _dot(qn, kc, _NT) * ATT_SCALE - b_prev, NEG) - lse_n)
            ds_cc = (p_cc * (_dot(dos, vc, _NT) - dsum_c) * ATT_SCALE).astype(BF)
            ds_cp = (p_cp * (_dot(dos, vp, _NT) - dsum_c) * ATT_SCALE).astype(BF)
            ds_nc = (p_nc * (_dot(don, vc, _NT) - dsum_n) * ATT_SCALE).astype(BF)
            dq_ref[rows, :] = unstack(_dot(ds_cc, kc) + _dot(ds_cp, kp)).astype(BF)
            dk_ref[rows, :] = (_dot(ds_cc, qs, _TN) + _dot(ds_nc, qn, _TN)).astype(BF)
            dv_ref[rows, :] = (_dot(p_cc.astype(BF), dos, _TN) + _dot(p_nc.astype(BF), don, _TN)).astype(BF)

    cur, prev, nxt = _band_specs(W, nb, per)
    out = jax.ShapeDtypeStruct((d, L, OUT_A), BF)
    cq, ck, cv = cols
    return pl.pallas_call(
        body, name=f"attn_a_bwd_g{g}",
        out_shape=(out, out, out),
        grid=(d, nsteps),
        in_specs=[cur(cq), nxt(cq), prev(ck), cur(ck), prev(cv), cur(cv), cur(0), nxt(0), cur(0), nxt(0)],
        out_specs=(cur(0), cur(0), cur(0)),
        compiler_params=_params("parallel", "parallel"),
    )(qkv[0], qkv[0], qkv[1], qkv[1], qkv[2], qkv[2], do, do, stats, stats)


SB_PAIR = 2
SB_QROWS = SB_BLOCK


def _softplus_parts(z):
    e = jnp.exp(-jnp.abs(z))
    log1p_e = jnp.where(e < 1e-4, e, jnp.log(1.0 + e))
    return e, jnp.maximum(z, 0.0) + log1p_e


def _split_dot(x, t):
    hi = x.astype(BF)
    lo = (x - hi.astype(F32)).astype(BF)
    return _dot(hi, t) + _dot(lo, t)


def _sb_block(qh, kk, causal, r_run, tri_incl):
    z = _dot(qh, kk, _NT)
    e, sp = _softplus_parts(z)
    ls = jnp.where(causal, -sp, 0.0)
    cin = _split_dot(ls, tri_incl)
    a = jnp.where(causal, jnp.exp(z + cin + r_run), 0.0)
    return z, e, cin, a


def _sb_specs(S):
    Q, hd = SB_QROWS, HEAD_DIM
    lanes = SB_PAIR * hd
    qc = (3 * W_A) // lanes
    kc = (3 * W_A + W_B) // lanes
    vc = (3 * W_A + 2 * W_B) // lanes
    q_spec = pl.BlockSpec((Q, lanes), lambda p, i: (i, qc + p))
    k_spec = pl.BlockSpec((S, lanes), lambda p, i: (0, kc + p))
    v_spec = pl.BlockSpec((S, lanes), lambda p, i: (0, vc + p))
    o_spec = pl.BlockSpec((Q, lanes), lambda p, i: (i, p))
    full = pl.BlockSpec((S, lanes), lambda p, i: (0, p))
    return q_spec, k_spec, v_spec, o_spec, full


def _sb_stack():
    Q, hd = SB_QROWS, HEAD_DIM
    lane_head = lax.broadcasted_iota(jnp.int32, (Q, SB_PAIR * hd), 1) // hd

    def stack(x):
        return jnp.concatenate([jnp.where(lane_head == h, x, jnp.zeros_like(x)) for h in range(SB_PAIR)], axis=0)

    def unstack(y):
        out = jnp.where(lane_head == 0, y[0:Q], 0.0)
        for h in range(1, SB_PAIR):
            out = jnp.where(lane_head == h, y[h * Q:(h + 1) * Q], out)
        return out

    return stack, unstack


def _sb_iotas(i):
    B, Q = SB_BLOCK, SB_QROWS
    row = lax.broadcasted_iota(jnp.int32, (Q, B), 0) + i * Q
    col = lax.broadcasted_iota(jnp.int32, (Q, B), 1)
    ahead = jnp.concatenate([col - row] * SB_PAIR, axis=0)
    tr = lax.broadcasted_iota(jnp.int32, (B, B), 0)
    tc = lax.broadcasted_iota(jnp.int32, (B, B), 1)
    return ahead, tr, tc


def _sb_fwd(proj, comm=()):
    S = proj.shape[0]
    B, Q, hd = SB_BLOCK, SB_QROWS, HEAD_DIM
    nq = S // Q
    R = SB_PAIR * Q
    q_spec, k_spec, v_spec, o_spec, _ = _sb_specs(S)

    def body(q_ref, k_ref, v_ref, o_ref):
        i = pl.program_id(1)
        ahead, tr, tc = _sb_iotas(i)
        tri_incl = (tr >= tc).astype(BF)
        stack, unstack = _sb_stack()
        qs = stack(q_ref[...] * ATT_SCALE)

        def cond(c):
            return jnp.logical_and(c[0] >= 0, c[-1] > SB_EXIT)

        def step(c):
            kb, r_run, acc, _ = c
            off = pl.multiple_of(kb * B, B)
            causal = ahead < -kb * B
            _, _, cin, a = _sb_block(qs, k_ref[pl.ds(off, B), :], causal, r_run, tri_incl)
            acc = acc + _dot(a.astype(BF), v_ref[pl.ds(off, B), :])
            r_run = r_run + cin[:, 0:1]
            return kb - 1, r_run, acc, jnp.max(r_run)

        init = (i, jnp.zeros((R, 1), F32), jnp.zeros((R, SB_PAIR * hd), F32), jnp.float32(0.0))
        fin = lax.while_loop(cond, step, init)
        o_ref[...] = unstack(fin[2])

    res = _call(body, name="sb_fwd", grid=(SB_HEADS // SB_PAIR, nq), in_specs=[q_spec, k_spec, v_spec],
                out_specs=o_spec, out_shape=jax.ShapeDtypeStruct((S, W_B), F32), args=(proj, proj, proj),
                sem=("parallel", "parallel"), comm=comm)
    return res[0], list(res[1:])


def _sb_bwd(proj, do, o, comm=()):
    S = proj.shape[0]
    B, Q, hd = SB_BLOCK, SB_QROWS, HEAD_DIM
    nq = S // Q
    R = SB_PAIR * Q
    q_spec, k_spec, v_spec, o_spec, full = _sb_specs(S)

    def body(q_ref, k_ref, v_ref, do_ref, o_ref, dq_ref, dk_ref, dv_ref):
        i = pl.program_id(1)

        @pl.when(i == 0)
        def _():
            dk_ref[...] = jnp.zeros_like(dk_ref)
            dv_ref[...] = jnp.zeros_like(dv_ref)

        ahead, tr, tc = _sb_iotas(i)
        tri_incl = (tr >= tc).astype(BF)
        tri_strict = (tr > tc).astype(BF)
        stack, unstack = _sb_stack()
        qs = stack(q_ref[...] * ATT_SCALE)
        dobs = stack(do_ref[...])
        o_all = o_ref[...]
        dsum = jnp.sum(dobs.astype(F32) * jnp.concatenate([o_all] * SB_PAIR, axis=0), axis=1, keepdims=True)

        def cond(c):
            return jnp.logical_and(c[0] >= 0, c[-1] > SB_EXIT)

        def step(c):
            kb, r_run, g_run, dq, _ = c
            off = pl.multiple_of(kb * B, B)
            causal = ahead < -kb * B
            kk = k_ref[pl.ds(off, B), :]
            vv = v_ref[pl.ds(off, B), :]
            z, e, cin, a = _sb_block(qs, kk, causal, r_run, tri_incl)
            a16 = a.astype(BF)
            gmat = a16.astype(F32) * _dot(dobs, vv, _NT)
            later = _split_dot(gmat, tri_strict)
            pfx = dsum - g_run - later
            sig = jnp.where(z >= 0, 1.0, e) / (1.0 + e)
            dz = jnp.where(causal, gmat - sig * pfx, 0.0).astype(BF)
            dq = dq + _dot(dz, kk)
            dk_ref[pl.ds(off, B), :] += _dot(dz, qs, _TN)
            dv_ref[pl.ds(off, B), :] += _dot(a16, dobs, _TN)
            g_run = g_run + jnp.sum(gmat, axis=1, keepdims=True)
            r_run = r_run + cin[:, 0:1]
            return kb - 1, r_run, g_run, dq, jnp.max(r_run)

        init = (i, jnp.zeros((R, 1), F32), jnp.zeros((R, 1), F32), jnp.zeros((R, SB_PAIR * hd), F32),
                jnp.float32(0.0))
        fin = lax.while_loop(cond, step, init)
        dq_ref[...] = unstack(fin[3]) * ATT_SCALE

    out = jax.ShapeDtypeStruct((S, W_B), F32)
    res = _call(body, name="sb_bwd", grid=(SB_HEADS // SB_PAIR, nq), in_specs=[q_spec, k_spec, v_spec, o_spec, o_spec],
                out_specs=(o_spec, full, full), out_shape=(out, out, out), args=(proj, proj, proj, do, o), comm=comm)
    return res[0], res[1], res[2], list(res[3:])


def _merge_fwd(o_g, l_g, o_b, proj, b_gate, w_br):
    S = o_b.shape[0]
    D = D_MODEL
    tm = _pick(S, 256)
    gcol = GATE_OFF // D

    def body(o0, o1, o2, l0, l1, l2, ob_ref, ga_ref, gb_ref, bg_ref, w_ref, mg_ref, oa_ref, lse_ref):
        la, lb, lc = l0[...], l1[...], l2[...]
        mx = jnp.maximum(jnp.maximum(la, lb), lc)
        ea, eb, ec = jnp.exp(la - mx), jnp.exp(lb - mx), jnp.exp(lc - mx)
        den = ea + eb + ec
        oa = (ea * o0[...] + eb * o1[...] + ec * o2[...]) / den
        oa_ref[...] = oa
        lse_ref[...] = mx + jnp.log(den)
        ya = _dot(oa.astype(BF), w_ref[0:OUT_A, :])
        yb = _dot(ob_ref[...].astype(BF), w_ref[OUT_A:OUT_A + W_B, :])
        bg = bg_ref[...]
        g_a = jax.nn.sigmoid(ga_ref[...].astype(F32) + bg[:, 0:D])
        g_b = jax.nn.sigmoid(gb_ref[...].astype(F32) + bg[:, D:2 * D])
        mg_ref[...] = (g_a * ya + g_b * yb).astype(BF)

    nar = pl.BlockSpec((tm, OUT_A), lambda i: (i, 0))
    wide = pl.BlockSpec((tm, D), lambda i: (i, 0))
    return pl.pallas_call(
        body, name="merge_fwd",
        out_shape=(jax.ShapeDtypeStruct((S, D), BF), jax.ShapeDtypeStruct((S, OUT_A), F32),
                   jax.ShapeDtypeStruct((S, OUT_A), F32)),
        grid=(S // tm,),
        in_specs=[nar] * 7 + [pl.BlockSpec((tm, D), lambda i: (i, gcol)),
                              pl.BlockSpec((tm, D), lambda i: (i, gcol + 1)),
                              pl.BlockSpec((1, 2 * D), lambda i: (0, 0)),
                              pl.BlockSpec((OUT_A + W_B, D), lambda i: (0, 0))],
        out_specs=(wide, nar, nar),
        compiler_params=_params("parallel"),
    )(*o_g, *l_g, o_b, proj, proj, b_gate.reshape(1, 2 * D), w_br)


STAT_OFF = HEAD_DIM // 2


def _merge_bwd(dm, oa, ob, lse, proj, b_gate, w_br):
    S = ob.shape[0]
    D = D_MODEL
    tm = _pick(S, 256)
    gcol = GATE_OFF // D

    def body(dm_ref, oa_ref, ob_ref, l_ref, ga_ref, gb_ref, bg_ref, w_ref,
             dya_ref, dyb_ref, doa_ref, dob_ref, st_ref, dg_ref, dbg_ref):
        @pl.when(pl.program_id(0) == 0)
        def _():
            dbg_ref[...] = jnp.zeros_like(dbg_ref)

        wa = w_ref[0:OUT_A, :]
        wb = w_ref[OUT_A:OUT_A + W_B, :]
        oa = oa_ref[...]
        ya = _dot(oa.astype(BF), wa)
        yb = _dot(ob_ref[...].astype(BF), wb)
        bg = bg_ref[...]
        g_a = jax.nn.sigmoid(ga_ref[...].astype(F32) + bg[:, 0:D])
        g_b = jax.nn.sigmoid(gb_ref[...].astype(F32) + bg[:, D:2 * D])
        dm = dm_ref[...].astype(F32)
        dga = dm * ya * g_a * (1.0 - g_a)
        dgb = dm * yb * g_b * (1.0 - g_b)
        dg_ref[:, 0:D] = dga.astype(BF)
        dg_ref[:, D:2 * D] = dgb.astype(BF)
        dbg_ref[:, 0:D] += jnp.sum(dga, axis=0, keepdims=True)
        dbg_ref[:, D:2 * D] += jnp.sum(dgb, axis=0, keepdims=True)
        dya = (dm * g_a).astype(BF)
        dyb = (dm * g_b).astype(BF)
        dya_ref[...] = dya
        dyb_ref[...] = dyb
        doa = _dot(dya, wa, _NT).astype(BF)
        doa_ref[...] = doa
        dob_ref[...] = _dot(dyb, wb, _NT).astype(BF)
        r = lax.broadcasted_iota(jnp.int32, (OUT_A, OUT_A), 0) // HEAD_DIM
        c = lax.broadcasted_iota(jnp.int32, (OUT_A, OUT_A), 1) // HEAD_DIM
        dsum = _split_dot(doa.astype(F32) * oa, (r == c).astype(BF))
        lane = lax.broadcasted_iota(jnp.int32, dsum.shape, 1) % HEAD_DIM
        st_ref[...] = jnp.where(lane < STAT_OFF, l_ref[...], dsum)

    nar = pl.BlockSpec((tm, OUT_A), lambda i: (i, 0))
    wide = pl.BlockSpec((tm, D), lambda i: (i, 0))
    wide2 = pl.BlockSpec((tm, 2 * D), lambda i: (i, 0))
    vec2 = pl.BlockSpec((1, 2 * D), lambda i: (0, 0))
    return pl.pallas_call(
        body, name="merge_bwd",
        out_shape=(jax.ShapeDtypeStruct((S, D), BF), jax.ShapeDtypeStruct((S, D), BF),
                   jax.ShapeDtypeStruct((S, OUT_A), BF), jax.ShapeDtypeStruct((S, W_B), BF),
                   jax.ShapeDtypeStruct((S, OUT_A), F32), jax.ShapeDtypeStruct((S, 2 * D), BF),
                   jax.ShapeDtypeStruct((1, 2 * D), F32)),
        grid=(S // tm,),
        in_specs=[wide, nar, nar, nar, pl.BlockSpec((tm, D), lambda i: (i, gcol)),
                  pl.BlockSpec((tm, D), lambda i: (i, gcol + 1)), vec2,
                  pl.BlockSpec((OUT_A + W_B, D), lambda i: (0, 0))],
        out_specs=(wide, wide, nar, nar, nar, wide2, vec2),
        compiler_params=_params("arbitrary"),
    )(dm, oa, ob, lse, proj, proj, b_gate.reshape(1, 2 * D), w_br)


_SQRT_HALF = 0.7071067811865476
_INV_SQRT_2PI = 0.3989422804014327


def _gelu_parts(a):
    cdf = 0.5 * (1.0 + lax.erf(a * _SQRT_HALF))
    pdf = _INV_SQRT_2PI * jnp.exp(-0.5 * a * a)
    return cdf, pdf


def _shift_down(a, halo, k):
    rows = lax.broadcasted_iota(jnp.int32, a.shape, 0)
    out = pltpu.roll(a, k, 0)
    for r in range(k):
        out = jnp.where(rows == r, halo[8 - k + r:8 - k + r + 1, :], out)
    return out


def _shift_up(a, halo, k):
    n = a.shape[0]
    rows = lax.broadcasted_iota(jnp.int32, a.shape, 0)
    out = pltpu.roll(a, n - k, 0)
    for r in range(k):
        out = jnp.where(rows == n - k + r, halo[r:r + 1, :], out)
    return out


def _conv_in(a_ref, h_ref, first):
    a = a_ref[...].astype(F32)
    halo = jnp.where(first, 0.0, h_ref[...].astype(F32))
    return a, _shift_down(a, halo, 1), _shift_down(a, halo, 2)


def _ffn_specs(S, tm):
    F = D_FF
    t8 = tm // 8
    a_spec = pl.BlockSpec((tm, F), lambda i: (i, 0))
    v_spec = pl.BlockSpec((tm, F), lambda i: (i, 1))
    halo_prev = pl.BlockSpec((8, F), lambda i: (jnp.maximum(i * t8 - 1, 0), 0))
    return a_spec, v_spec, halo_prev


def _ffn_act_fwd(up, conv_w, conv_b):
    S = up.shape[0]
    F = D_FF
    tm = _pick(S, 256)
    a_spec, v_spec, halo_prev = _ffn_specs(S, tm)

    def body(a_ref, h_ref, v_ref, w_ref, b_ref, act_ref):
        a0, a1, a2 = _conv_in(a_ref, h_ref, pl.program_id(0) == 0)
        w = w_ref[...]
        ac = b_ref[...] + w[0:1, :] * a2 + w[1:2, :] * a1 + w[2:3, :] * a0
        cdf, _ = _gelu_parts(ac)
        act_ref[...] = (ac * cdf * v_ref[...].astype(F32)).astype(BF)

    return pl.pallas_call(
        body, name="ffn_act_fwd",
        out_shape=jax.ShapeDtypeStruct((S, F), BF),
        grid=(S // tm,),
        in_specs=[a_spec, halo_prev, v_spec, pl.BlockSpec((3, F), lambda i: (0, 0)),
                  pl.BlockSpec((1, F), lambda i: (0, 0))],
        out_specs=a_spec,
        compiler_params=_params("parallel"),
    )(up, up, up, conv_w, conv_b.reshape(1, F))


FFN_CHUNK = 256


def _ffn_down(up, conv_w, conv_b, w_down, res, *, name, comm=()):
    S = up.shape[0]
    F = D_FF
    D = w_down.shape[1]
    tm = _pick(S, 256)
    a_spec, v_spec, halo_prev = _ffn_specs(S, tm)

    def body(a_ref, h_ref, v_ref, w_ref, b_ref, wd_ref, r_ref, o_ref, act_ref):
        first = pl.program_id(0) == 0
        acc = r_ref[...]
        for c0 in range(0, F, FFN_CHUNK):
            cs = slice(c0, c0 + FFN_CHUNK)
            a = a_ref[:, cs].astype(F32)
            halo = jnp.where(first, 0.0, h_ref[:, cs].astype(F32))
            w = w_ref[:, cs]
            ac = b_ref[:, cs] + w[0:1, :] * _shift_down(a, halo, 2) + w[1:2, :] * _shift_down(a, halo, 1) + w[2:3, :] * a
            cdf, _ = _gelu_parts(ac)
            act = (ac * cdf * v_ref[:, cs].astype(F32)).astype(BF)
            act_ref[:, cs] = act
            acc = acc + _dot(act, wd_ref[cs, :])
        o_ref[...] = acc

    row = pl.BlockSpec((tm, D), lambda i: (i, 0))
    res_ = _call(body, name=name, grid=(S // tm,),
                 in_specs=[a_spec, halo_prev, v_spec, pl.BlockSpec((3, F), lambda i: (0, 0)),
                           pl.BlockSpec((1, F), lambda i: (0, 0)), pl.BlockSpec((F, D), lambda i: (0, 0)), row],
                 out_specs=(row, a_spec),
                 out_shape=(jax.ShapeDtypeStruct((S, D), F32), jax.ShapeDtypeStruct((S, F), BF)),
                 args=(up, up, up, conv_w, conv_b.reshape(1, F), w_down, res), comm=comm)
    return res_[0], res_[1], list(res_[2:])


def _ffn_act_bwd(dact, up, conv_w, conv_b, comm=()):
    S = up.shape[0]
    F = D_FF
    tm = _pick(S, 256)
    t8 = tm // 8
    nt = S // tm
    a_spec, v_spec, halo_prev = _ffn_specs(S, tm)

    def nxt(col):
        return pl.BlockSpec((8, F), lambda i: (jnp.minimum((i + 1) * t8, S // 8 - 1), col))

    def body(da_ref, dan_ref, a_ref, h_ref, an_ref, v_ref, vn_ref, w_ref, b_ref, dup_ref, dw_ref, db_ref):
        i = pl.program_id(0)

        @pl.when(i == 0)
        def _():
            dw_ref[...] = jnp.zeros_like(dw_ref)
            db_ref[...] = jnp.zeros_like(db_ref)

        w = w_ref[...]
        bias = b_ref[...]

        def conv_grad(a0, a1, a2, dact, v):
            ac = bias + w[0:1, :] * a2 + w[1:2, :] * a1 + w[2:3, :] * a0
            cdf, pdf = _gelu_parts(ac)
            return ac, cdf, dact * v * (cdf + ac * pdf)

        a0, a1, a2 = _conv_in(a_ref, h_ref, i == 0)
        dact = da_ref[...].astype(F32)
        ac, cdf, dac = conv_grad(a0, a1, a2, dact, v_ref[...].astype(F32))
        dup_ref[:, F:2 * F] = (dact * ac * cdf).astype(BF)
        db_ref[...] += jnp.sum(dac, axis=0, keepdims=True)
        dw_ref[0:1, :] += jnp.sum(dac * a2, axis=0, keepdims=True)
        dw_ref[1:2, :] += jnp.sum(dac * a1, axis=0, keepdims=True)
        dw_ref[2:3, :] += jnp.sum(dac * a0, axis=0, keepdims=True)
        n0 = an_ref[...].astype(F32)
        tail = a_ref[tm - 8:tm, :].astype(F32)
        _, _, dac_n = conv_grad(n0, _shift_down(n0, tail, 1), _shift_down(n0, tail, 2),
                                dan_ref[...].astype(F32), vn_ref[...].astype(F32))
        dac_n = jnp.where(i == nt - 1, 0.0, dac_n)
        d1 = _shift_up(dac, dac_n, 1)
        d2 = _shift_up(dac, dac_n, 2)
        dup_ref[:, 0:F] = (w[2:3, :] * dac + w[1:2, :] * d1 + w[0:1, :] * d2).astype(BF)

    w_spec = pl.BlockSpec((3, F), lambda i: (0, 0))
    b_spec = pl.BlockSpec((1, F), lambda i: (0, 0))
    res = _call(body, name="ffn_act_bwd", grid=(nt,),
                in_specs=[a_spec, nxt(0), a_spec, halo_prev, nxt(0), v_spec, nxt(1), w_spec, b_spec],
                out_specs=(pl.BlockSpec((tm, 2 * F), lambda i: (i, 0)), w_spec, b_spec),
                out_shape=(jax.ShapeDtypeStruct((S, 2 * F), BF), jax.ShapeDtypeStruct((3, F), F32),
                           jax.ShapeDtypeStruct((1, F), F32)),
                args=(dact, dact, up, up, up, up, up, conv_w, conv_b.reshape(1, F)), comm=comm)
    return res[0], res[1], res[2], list(res[3:])


def _adamw(parts, w, m, v, name):
    R, C = w.shape
    tr = R
    for cand in (512, 256, 128, 64, 32, 16):
        if R % cand == 0 and cand * C * 4 <= (1 << 21):
            tr = cand
            break
    c1 = 1.0 / (1.0 - ADAM_B1 ** ADAM_STEP)
    c2 = 1.0 / (1.0 - ADAM_B2 ** ADAM_STEP)

    def body(p_ref, w_ref, m_ref, v_ref, g_ref, d_ref, nm_ref, nv_ref):
        g = p_ref[0].astype(F32)
        for j in range(1, N_DEV):
            g = g + p_ref[j].astype(F32)
        nm = ADAM_B1 * m_ref[...] + (1.0 - ADAM_B1) * g
        nv = ADAM_B2 * v_ref[...] + (1.0 - ADAM_B2) * (g * g)
        g_ref[...] = g
        nm_ref[...] = nm
        nv_ref[...] = nv
        d_ref[...] = -ADAM_LR * ((nm * c1) / (jnp.sqrt(nv * c2) + ADAM_EPS) + ADAM_WD * w_ref[...])

    blk = pl.BlockSpec((tr, C), lambda i: (i, 0))
    out = jax.ShapeDtypeStruct((R, C), F32)
    return pl.pallas_call(
        body, name=name,
        out_shape=(out, out, out, out),
        grid=(R // tr,),
        in_specs=[pl.BlockSpec((N_DEV, tr, C), lambda i: (0, i, 0)), blk, blk, blk],
        out_specs=(blk, blk, blk, blk),
        compiler_params=_params("parallel"),
    )(parts, w, m, v)


def _dil(t, dil):
    S, C = t.shape
    if dil == 1:
        return t.reshape(1, S, C)
    return t.reshape(S // dil, dil, C).transpose(1, 0, 2)


def _undil(t):
    d, L, C = t.shape
    if d == 1:
        return t.reshape(L, C)
    return t.transpose(1, 0, 2).reshape(L * d, C)


def _group_qkv(proj, g):
    dil = DSW_GROUPS[g][1]
    if dil == 1:
        p3 = _dil(proj, 1)
        return (p3, p3, p3), (g, W_A // OUT_A + g, 2 * W_A // OUT_A + g)
    c0 = g * OUT_A
    return tuple(_dil(proj[:, o + c0:o + c0 + OUT_A], dil) for o in (0, W_A, 2 * W_A)), (0, 0, 0)


_COL_SHARDED = ("w_in", "w_br", "w_up")
_ROW_SHARDED = ("w_o", "w_down")


class _Plan:
    def __init__(self):
        self.riders = {}
        self.landed = {}

    def ride(self, slot, key, kind, x):
        self.riders.setdefault(slot, []).append((key, kind, x))

    def run(self, slot, fn, *args, **kw):
        items = self.riders.pop(slot, [])
        res = fn(*args, comm=[(kind, x) for _, kind, x in items], **kw)
        for (key, _, _), r in zip(items, res[-1]):
            self.landed[key] = r
        return res[0] if len(res) == 2 else res[:-1]

    def weight(self, n, l):
        g = self.landed[(n, l)]
        if n in _COL_SHARDED:
            return g.transpose(1, 0, 2).reshape(g.shape[1], -1)
        return g.reshape(-1, g.shape[2])

    def scatter(self, slot, n, l, full):
        K, N = full.shape
        if n in _COL_SHARDED:
            blocks = full.reshape(K, N_DEV, N // N_DEV).transpose(1, 0, 2)
        else:
            blocks = full.reshape(N_DEV, K // N_DEV, N)
        self.ride(slot, ("d" + n, l), "scatter", blocks)


def _layer_fwd(x, p, plan, l):
    h = _rms_fwd(x, p["norm1"][l], f"rms1_fwd_{l}")
    proj = plan.run(f"proj_{l}", _matmul, h, plan.weight("w_in", l), mode="nn", out_dtype=BF, name=f"proj_{l}",
                    tm=1024, tn=1024, tk=1024)
    o_g, l_g, qkv_g = [], [], []
    for g in range(N_GROUPS):
        qkv, cols = _group_qkv(proj, g)
        og, lg = plan.run(f"attn_a_fwd_g{g}_{l}", _attn_a_fwd, qkv, cols, g)
        o_g.append(_undil(og))
        l_g.append(_undil(lg))
        qkv_g.append((qkv, cols))
    ob = plan.run(f"sb_fwd_{l}", _sb_fwd, proj)
    merged, oa, lse = _merge_fwd(o_g, l_g, ob, proj, p["b_gate"][l], plan.weight("w_br", l))
    x1 = plan.run(f"wo_{l}", _matmul, merged, plan.weight("w_o", l), mode="nn", out_dtype=F32, name=f"wo_{l}",
                  tm=1024, tn=1024, tk=1024, res=x)
    h2 = _rms_fwd(x1, p["norm2"][l], f"rms2_fwd_{l}")
    up = plan.run(f"up_{l}", _matmul, h2, plan.weight("w_up", l), mode="nn", out_dtype=BF, name=f"up_{l}",
                  tm=1024, tn=1408, tk=1024)
    x2, act = plan.run(f"down_{l}", _ffn_down, up, p["conv_w"][l], p["conv_b"][l], plan.weight("w_down", l), x1,
                       name=f"down_{l}")
    saved = dict(x=x, h=h, proj=proj, qkv_g=qkv_g, oa=oa, ob=ob, lse=lse, merged=merged, x1=x1, h2=h2, up=up, act=act)
    return x2, saved


def _layer_bwd(dx2, sv, p, plan, l):
    gr = {}
    dact = plan.run(f"d_act_{l}", _matmul, dx2, plan.weight("w_down", l), mode="nt", out_dtype=BF, name=f"d_act_{l}",
                    tm=1024, tn=1408, tk=1024)
    dwd = plan.run(f"dw_down_{l}", _matmul, sv["act"], dx2, mode="tn", out_dtype=BF, name=f"dw_down_{l}",
                   tm=1408, tn=1024, tk=2048)
    plan.scatter(f"d_h2_{l}", "w_down", l, dwd)
    dup, gr["conv_w"], dcb = plan.run(f"ffn_act_bwd_{l}", _ffn_act_bwd, dact, sv["up"], p["conv_w"][l], p["conv_b"][l])
    gr["conv_b"] = dcb[0]
    dx1, dn2 = plan.run(f"d_h2_{l}", _matmul_rms_bwd, dup, plan.weight("w_up", l), sv["x1"], p["norm2"][l], dx2,
                        name=f"d_h2_{l}")
    dwu = plan.run(f"dw_up_{l}", _matmul, sv["h2"], dup, mode="tn", out_dtype=BF, name=f"dw_up_{l}",
                   tm=1024, tn=1408, tk=2048)
    plan.scatter(f"sb_bwd_{l}", "w_up", l, dwu)
    gr["norm2"] = dn2[0]
    dm = plan.run(f"d_merged_{l}", _matmul, dx1, plan.weight("w_o", l), mode="nt", out_dtype=BF,
                  name=f"d_merged_{l}", tm=1024, tn=1024, tk=1024)
    dwo = plan.run(f"dw_o_{l}", _matmul, sv["merged"], dx1, mode="tn", out_dtype=BF, name=f"dw_o_{l}",
                   tm=1024, tn=1024, tk=2048)
    plan.scatter(f"dw_in_{l}", "w_o", l, dwo)
    dya, dyb, doa, dob, stats, dgate, dbg = _merge_bwd(dm, sv["oa"], sv["ob"], sv["lse"], sv["proj"], p["b_gate"][l],
                                                       plan.weight("w_br", l))
    gr["b_gate"] = dbg[0]
    dwa = plan.run(f"dw_bra_{l}", _matmul, sv["oa"], dya, mode="tn", out_dtype=BF, name=f"dw_bra_{l}",
                   tm=256, tn=1024, tk=2048)
    dwb = plan.run(f"dw_brb_{l}", _matmul, sv["ob"], dyb, mode="tn", out_dtype=BF, name=f"dw_brb_{l}",
                   tm=256, tn=1024, tk=2048)
    plan.scatter(f"dw_in_{l}", "w_br", l, jnp.concatenate([dwa, dwb], axis=0))
    proj = sv["proj"]
    dq_a, dk_a, dv_a = [], [], []
    for g, (_, dil) in enumerate(DSW_GROUPS):
        qkv, cols = sv["qkv_g"][g]
        dqg, dkg, dvg = _attn_a_bwd(qkv, cols, _dil(doa, dil), _dil(stats, dil), g)
        dq_a.append(_undil(dqg))
        dk_a.append(_undil(dkg))
        dv_a.append(_undil(dvg))
    dqb, dkb, dvb = plan.run(f"sb_bwd_{l}", _sb_bwd, proj, dob, sv["ob"])
    dproj = jnp.concatenate(dq_a + dk_a + dv_a + [dqb.astype(BF), dkb.astype(BF), dvb.astype(BF), dgate], axis=1)
    dx, dn1 = plan.run(f"d_h_{l}", _matmul_rms_bwd, dproj, plan.weight("w_in", l), sv["x"], p["norm1"][l], dx1,
                       name=f"d_h_{l}")
    dwi = plan.run(f"dw_in_{l}", _matmul, sv["h"], dproj, mode="tn", out_dtype=BF, name=f"dw_in_{l}",
                   tm=1024, tn=1280, tk=2048)
    plan.scatter(f"ffn_act_bwd_{l - 1}", "w_in", l, dwi)
    gr["norm1"] = dn1[0]
    return dx, gr


def kernel(x, norm1, w_in, b_gate, w_br, w_o, norm2, w_up, conv_w, conv_b, w_down, norm_f, loss_target, m_norm1, m_w_in, m_b_gate, m_w_br, m_w_o, m_norm2, m_w_up, m_conv_w, m_conv_b, m_w_down, m_norm_f, v_norm1, v_w_in, v_b_gate, v_w_br, v_w_o, v_norm2, v_w_up, v_conv_w, v_conv_b, v_w_down, v_norm_f):
    depth = norm1.shape[0]
    me = 4 * lax.axis_index("x") + 2 * lax.axis_index("y") + lax.axis_index("c")
    shards = dict(w_in=w_in, w_br=w_br, w_o=w_o, w_up=w_up, w_down=w_down)
    moments_m = dict(norm1=m_norm1, w_in=m_w_in, b_gate=m_b_gate, w_br=m_w_br, w_o=m_w_o, norm2=m_norm2,
                     w_up=m_w_up, conv_w=m_conv_w, conv_b=m_conv_b, w_down=m_w_down, norm_f=m_norm_f)
    moments_v = dict(norm1=v_norm1, w_in=v_w_in, b_gate=v_b_gate, w_br=v_w_br, w_o=v_w_o, norm2=v_norm2,
                     w_up=v_w_up, conv_w=v_conv_w, conv_b=v_conv_b, w_down=v_w_down, norm_f=v_norm_f)

    plan = _Plan()
    wb = {n: s.astype(BF) for n, s in shards.items()}
    p = dict(norm1=norm1, b_gate=b_gate, norm2=norm2, conv_b=conv_b)
    cw = _all_gather(conv_w, "gather_conv_w")
    p["conv_w"] = cw.transpose(1, 2, 0, 3).reshape(depth, 3, D_FF)
    plan.landed[("w_in", 0)] = _all_gather(wb["w_in"][0], "gather_w_in_0")
    for l in range(depth):
        plan.ride(f"proj_{l}", ("w_down", l), "gather", wb["w_down"][l])
        plan.ride(f"attn_a_fwd_g0_{l}" if l == 0 else f"down_{l - 1}", ("w_br", l), "gather", wb["w_br"][l])
        plan.ride(f"attn_a_fwd_g0_{l}" if l == 0 else f"down_{l - 1}", ("w_o", l), "gather", wb["w_o"][l])
        plan.ride(f"sb_fwd_{l}", ("w_up", l), "gather", wb["w_up"][l])
        if l + 1 < depth:
            plan.ride(f"up_{l}", ("w_in", l + 1), "gather", wb["w_in"][l + 1])

    xs = x[0]
    saved = []
    for l in range(depth):
        xs, sv = _layer_fwd(xs, p, plan, l)
        saved.append(sv)
    loss_part, dx, dnf = _loss_head(xs, norm_f, loss_target[0])
    loss = lax.psum(loss_part[0, 0], ("x", "y", "c"))

    grads = [None] * depth
    for l in reversed(range(depth)):
        dx, grads[l] = _layer_bwd(dx, saved[l], p, plan, l)
    grad_x = dx[None]
    (_, _, last), = plan.riders.pop("ffn_act_bwd_-1")
    plan.landed[("dw_in", 0)] = _all_to_all(last, "scatter_w_in_0")
    assert not plan.riders, sorted(plan.riders)

    out_g, out_d, out_m, out_v = {}, {}, {}, {}
    for n in _COL_SHARDED + _ROW_SHARDED:
        parts = jnp.concatenate([plan.landed[("d" + n, l)] for l in range(depth)], axis=1)
        shp = shards[n].shape
        flat = (shp[0] * shp[1], shp[2])
        res = _adamw(parts, shards[n].reshape(flat), moments_m[n].reshape(flat), moments_v[n].reshape(flat),
                     f"adamw_{n}")
        out_g[n], out_d[n], out_m[n], out_v[n] = [r.reshape(shp) for r in res]

    small = ("norm1", "b_gate", "norm2", "conv_b")
    vecs = [jnp.stack([grads[l][n] for l in range(depth)]).reshape(-1) for n in small]
    vecs.append(dnf.reshape(-1))
    vecs.append(jnp.stack([grads[l]["conv_w"] for l in range(depth)]).reshape(-1))
    sizes = [v.shape[0] for v in vecs]
    flat = jnp.concatenate(vecs)
    n_small = sum(sizes[:-1])
    pad = (-flat.shape[0]) % 1024
    flat = jnp.pad(flat, (0, pad)).reshape(-1, 128)
    allp = _all_gather(flat, "gather_small_grads").reshape(N_DEV, -1)
    rep_w = jnp.concatenate([norm1.reshape(-1), b_gate.reshape(-1), norm2.reshape(-1), conv_b.reshape(-1), norm_f])
    rep_m = jnp.concatenate([moments_m[n].reshape(-1) for n in small] + [m_norm_f])
    rep_v = jnp.concatenate([moments_v[n].reshape(-1) for n in small] + [v_norm_f])
    rows = n_small // 128
    res = _adamw(allp[:, :n_small].reshape(N_DEV, rows, 128), rep_w.reshape(rows, 128), rep_m.reshape(rows, 128),
                 rep_v.reshape(rows, 128), "adamw_small")
    off = 0
    for n, sz in zip(small + ("norm_f",), sizes[:-1]):
        shp = norm_f.shape if n == "norm_f" else p[n].shape
        out_g[n], out_d[n], out_m[n], out_v[n] = [r.reshape(-1)[off:off + sz].reshape(shp) for r in res]
        off += sz
    f = conv_w.shape[2]
    cwp = allp[:, n_small:n_small + sizes[-1]].reshape(N_DEV, depth * 3, D_FF)
    cwp = lax.dynamic_slice_in_dim(cwp, me * f, f, axis=2)
    res = _adamw(cwp, conv_w.reshape(depth * 3, f), m_conv_w.reshape(depth * 3, f), v_conv_w.reshape(depth * 3, f),
                 "adamw_conv_w")
    out_g["conv_w"], out_d["conv_w"], out_m["conv_w"], out_v["conv_w"] = [r.reshape(conv_w.shape) for r in res]

    order = ("norm1", "w_in", "b_gate", "w_br", "w_o", "norm2", "w_up", "conv_w", "conv_b", "w_down", "norm_f")
    return (loss, grad_x, *[out_g[n] for n in order], *[out_d[n] for n in order],
            *[out_m[n] for n in order], *[out_v[n] for n in order])
```

```python
import functools

import jax
import jax.numpy as jnp
from jax import lax
from jax.experimental import pallas as pl
from jax.experimental.pallas import tpu as pltpu

BF = jnp.bfloat16
F32 = jnp.float32

N_DEV = 8
D_MODEL = 1024
HEAD_DIM = 64
DSW_GROUPS = ((128, 1), (512, 4), (2048, 16))
HEADS_PER_GROUP = 4
N_GROUPS = len(DSW_GROUPS)
DSW_HEADS = HEADS_PER_GROUP * N_GROUPS
SB_HEADS = 4
W_A = DSW_HEADS * HEAD_DIM
W_B = SB_HEADS * HEAD_DIM
OUT_A = HEADS_PER_GROUP * HEAD_DIM
N_IN = 3 * W_A + 3 * W_B + 2 * D_MODEL
GATE_OFF = 3 * W_A + 3 * W_B
D_FF = 2816
SB_BLOCK = 256
RMS_EPS = 1e-6
ATT_SCALE = HEAD_DIM ** -0.5
NEG = -1e30
SB_EXIT = -110.0

ADAM_LR = 0.001
ADAM_B1 = 0.9
ADAM_B2 = 0.999
ADAM_EPS = 1e-08
ADAM_WD = 0.01
ADAM_STEP = 10

HBM_SPEC = pl.BlockSpec(memory_space=pltpu.HBM)
MESH = pl.DeviceIdType.MESH

_NN = (((1,), (0,)), ((), ()))
_NT = (((1,), (1,)), ((), ()))
_TN = (((0,), (0,)), ((), ()))


def _dot(a, b, dn=_NN):
    return lax.dot_general(a, b, dn, preferred_element_type=F32)


def _pick(dim, pref):
    if dim <= pref:
        return dim
    t = (pref // 128) * 128
    while t >= 128:
        if dim % t == 0:
            return t
        t -= 128
    return dim


def _params(*sem):
    return pltpu.CompilerParams(dimension_semantics=sem)


def _peer(k):
    x, y, c = lax.axis_index("x"), lax.axis_index("y"), lax.axis_index("c")
    px = 1 - x if (k >> 2) & 1 else x
    py = 1 - y if (k >> 1) & 1 else y
    pc = 1 - c if k & 1 else c
    return (px, py, pc), 4 * px + 2 * py + pc


def _exchange(kind, x_ref, out_ref, send_sems, recv_sems, local_sem):
    gather = kind == "gather"
    _, me = _peer(0)

    def src(idx):
        return x_ref if gather else x_ref.at[idx]

    def copy(k, dst_idx):
        peer, pidx = _peer(k)
        return pltpu.make_async_remote_copy(
            src_ref=src(pidx), dst_ref=out_ref.at[dst_idx], send_sem=send_sems.at[k - 1],
            recv_sem=recv_sems.at[k - 1], device_id=peer, device_id_type=MESH)

    mine = pltpu.make_async_copy(src(me), out_ref.at[me], local_sem)

    def start():
        mine.start()
        for k in range(1, N_DEV):
            copy(k, me).start()

    def wait():
        for k in range(1, N_DEV):
            copy(k, _peer(k)[1]).wait_recv()
        for k in range(1, N_DEV):
            copy(k, me).wait_send()
        mine.wait()

    return start, wait


_EXCHANGE_SEMS = [pltpu.SemaphoreType.DMA((N_DEV - 1,)), pltpu.SemaphoreType.DMA((N_DEV - 1,)),
                  pltpu.SemaphoreType.DMA]


def _exchange_shape(kind, x):
    return jax.ShapeDtypeStruct(((N_DEV,) + x.shape) if kind == "gather" else x.shape, x.dtype)


def _exchange_alone(kind, x, name):
    def body(x_ref, out_ref, send_sems, recv_sems, local_sem):
        start, wait = _exchange(kind, x_ref, out_ref, send_sems, recv_sems, local_sem)
        start()
        wait()

    return pl.pallas_call(
        body, name=name, out_shape=_exchange_shape(kind, x),
        in_specs=[HBM_SPEC], out_specs=HBM_SPEC, scratch_shapes=list(_EXCHANGE_SEMS),
    )(x)


def _all_gather(x, name):
    return _exchange_alone("gather", x, name)


def _all_to_all(x, name):
    return _exchange_alone("scatter", x, name)


def _call(body, *, name, grid, in_specs, out_specs, out_shape, args, scratch_shapes=(), sem=None, comm=()):
    single = not isinstance(out_shape, (tuple, list))
    outs = (out_shape,) if single else tuple(out_shape)
    ospecs = (out_specs,) if single else tuple(out_specs)
    if not comm:
        res = pl.pallas_call(
            body, name=name, out_shape=outs, grid=grid, in_specs=list(in_specs), out_specs=ospecs,
            scratch_shapes=list(scratch_shapes), compiler_params=_params(*(sem or ("arbitrary",) * len(grid))),
        )(*args)
        return res
    n_in, n_out, n_scr, nc = len(in_specs), len(outs), len(scratch_shapes), len(comm)

    def wrapped(*refs):
        ins = refs[:n_in]
        cins = refs[n_in:n_in + nc]
        o0 = n_in + nc
        kouts = refs[o0:o0 + n_out]
        couts = refs[o0 + n_out:o0 + n_out + nc]
        s0 = o0 + n_out + nc
        scr = refs[s0:s0 + n_scr]
        sems = refs[s0 + n_scr:]
        ids = [pl.program_id(ax) for ax in range(len(grid))]
        first = functools.reduce(jnp.logical_and, [i == 0 for i in ids])
        last = functools.reduce(jnp.logical_and, [i == g - 1 for i, g in zip(ids, grid)])
        ex = [_exchange(comm[c][0], cins[c], couts[c], *sems[3 * c:3 * c + 3]) for c in range(nc)]

        @pl.when(first)
        def _():
            for start, _ in ex:
                start()

        body(*ins, *kouts, *scr)

        @pl.when(last)
        def _():
            for _, wait in ex:
                wait()

    return pl.pallas_call(
        wrapped, name=name,
        out_shape=outs + tuple(_exchange_shape(k, x) for k, x in comm),
        grid=grid, in_specs=list(in_specs) + [HBM_SPEC] * nc, out_specs=ospecs + (HBM_SPEC,) * nc,
        scratch_shapes=list(scratch_shapes) + list(_EXCHANGE_SEMS) * nc,
        compiler_params=_params(*(("arbitrary",) * len(grid))),
    )(*args, *[x for _, x in comm])


def _matmul(a, b, *, mode, out_dtype, name, tm=512, tn=1024, tk=1024, res=None, comm=()):
    if mode == "nn":
        (M, K), (_, N) = a.shape, b.shape
    elif mode == "nt":
        (M, K), (N, _) = a.shape, b.shape
    else:
        (K, M), (_, N) = a.shape, b.shape
    tm, tn, tk = _pick(M, tm), _pick(N, tn), _pick(K, tk)
    nk = K // tk
    dn = {"nn": _NN, "nt": _NT, "tn": _TN}[mode]

    def body(*refs):
        a_ref, b_ref = refs[0], refs[1]
        r_ref = refs[2] if res is not None else None
        o_ref = refs[3] if res is not None else refs[2]

        def finish(r):
            if res is not None:
                r = r + r_ref[...].astype(F32)
            o_ref[...] = r.astype(out_dtype)

        part = _dot(a_ref[...].astype(BF), b_ref[...].astype(BF), dn)
        if nk == 1:
            finish(part)
            return
        acc = refs[-1]
        k = pl.program_id(2)

        @pl.when(k == 0)
        def _():
            acc[...] = jnp.zeros_like(acc)

        acc[...] += part

        @pl.when(k == nk - 1)
        def _():
            finish(acc[...])

    if mode == "tn":
        a_spec = pl.BlockSpec((tk, tm), lambda j, i, k: (k, i))
    else:
        a_spec = pl.BlockSpec((tm, tk), lambda j, i, k: (i, k))
    if mode == "nt":
        b_spec = pl.BlockSpec((tn, tk), lambda j, i, k: (j, k))
    else:
        b_spec = pl.BlockSpec((tk, tn), lambda j, i, k: (k, j))
    o_spec = pl.BlockSpec((tm, tn), lambda j, i, k: (i, j))
    in_specs = [a_spec, b_spec] + ([o_spec] if res is not None else [])
    args = (a, b) + ((res,) if res is not None else ())
    out = _call(body, name=name, grid=(N // tn, M // tm, nk), in_specs=in_specs, out_specs=o_spec,
                out_shape=jax.ShapeDtypeStruct((M, N), out_dtype), args=args,
                scratch_shapes=[pltpu.VMEM((tm, tn), F32)] if nk > 1 else [],
                sem=("parallel", "parallel", "arbitrary"), comm=comm)
    return out[0], list(out[1:])


def _rms_fwd(x, g, name):
    S, D = x.shape
    tm = _pick(S, 512)

    def body(x_ref, g_ref, h_ref):
        xf = x_ref[...]
        r = lax.rsqrt(jnp.mean(xf * xf, axis=-1, keepdims=True) + RMS_EPS)
        h_ref[...] = (xf * r * g_ref[...]).astype(BF)

    return pl.pallas_call(
        body, name=name,
        out_shape=jax.ShapeDtypeStruct((S, D), BF),
        grid=(S // tm,),
        in_specs=[pl.BlockSpec((tm, D), lambda i: (i, 0)), pl.BlockSpec((1, D), lambda i: (0, 0))],
        out_specs=pl.BlockSpec((tm, D), lambda i: (i, 0)),
        compiler_params=_params("parallel"),
    )(x, g.reshape(1, D))


def _rms_bwd(x, g, dh, dres, name):
    S, D = x.shape
    tm = _pick(S, 512)

    def body(x_ref, g_ref, dh_ref, dres_ref, dx_ref, dg_ref):
        @pl.when(pl.program_id(0) == 0)
        def _():
            dg_ref[...] = jnp.zeros_like(dg_ref)

        xf = x_ref[...]
        r = lax.rsqrt(jnp.mean(xf * xf, axis=-1, keepdims=True) + RMS_EPS)
        xh = xf * r
        dy = dh_ref[...].astype(F32)
        dg_ref[...] += jnp.sum(dy * xh, axis=0, keepdims=True)
        dxh = dy * g_ref[...]
        dx = r * (dxh - xh * jnp.mean(dxh * xh, axis=-1, keepdims=True))
        dx_ref[...] = dres_ref[...] + dx

    row = pl.BlockSpec((tm, D), lambda i: (i, 0))
    vec = pl.BlockSpec((1, D), lambda i: (0, 0))
    return pl.pallas_call(
        body, name=name,
        out_shape=(jax.ShapeDtypeStruct((S, D), F32), jax.ShapeDtypeStruct((1, D), F32)),
        grid=(S // tm,),
        in_specs=[row, vec, row, row], out_specs=(row, vec),
        compiler_params=_params("arbitrary"),
    )(x, g.reshape(1, D), dh, dres)


def _matmul_rms_bwd(dy, w, x, g, dres, *, name, tm=512, comm=()):
    S, K = dy.shape
    D = w.shape[0]
    tm = _pick(S, tm)

    def body(dy_ref, w_ref, x_ref, g_ref, dres_ref, dx_ref, dg_ref):
        @pl.when(pl.program_id(0) == 0)
        def _():
            dg_ref[...] = jnp.zeros_like(dg_ref)

        dh = _dot(dy_ref[...].astype(BF), w_ref[...], _NT)
        xf = x_ref[...]
        r = lax.rsqrt(jnp.mean(xf * xf, axis=-1, keepdims=True) + RMS_EPS)
        xh = xf * r
        dg_ref[...] += jnp.sum(dh * xh, axis=0, keepdims=True)
        dxh = dh * g_ref[...]
        dx_ref[...] = dres_ref[...] + r * (dxh - xh * jnp.mean(dxh * xh, axis=-1, keepdims=True))

    row = pl.BlockSpec((tm, D), lambda i: (i, 0))
    vec = pl.BlockSpec((1, D), lambda i: (0, 0))
    res = _call(body, name=name, grid=(S // tm,),
                in_specs=[pl.BlockSpec((tm, K), lambda i: (i, 0)), pl.BlockSpec((D, K), lambda i: (0, 0)), row, vec, row],
                out_specs=(row, vec),
                out_shape=(jax.ShapeDtypeStruct((S, D), F32), jax.ShapeDtypeStruct((1, D), F32)),
                args=(dy, w, x, g.reshape(1, D), dres), comm=comm)
    return res[0], res[1], list(res[2:])


def _loss_head(x, g, target):
    S, D = x.shape
    tm = _pick(S, 512)

    def body(x_ref, g_ref, t_ref, loss_ref, dx_ref, dg_ref):
        @pl.when(pl.program_id(0) == 0)
        def _():
            dg_ref[...] = jnp.zeros_like(dg_ref)
            loss_ref[...] = jnp.zeros_like(loss_ref)

        xf = x_ref[...]
        gg = g_ref[...]
        r = lax.rsqrt(jnp.mean(xf * xf, axis=-1, keepdims=True) + RMS_EPS)
        xh = xf * r
        err = xh * gg - t_ref[...]
        per_tok = jnp.mean(err * err, axis=-1, keepdims=True)
        loss_ref[...] += 0.5 * jnp.sum(per_tok, axis=0, keepdims=True)
        dy = err * (1.0 / D)
        dg_ref[...] += jnp.sum(dy * xh, axis=0, keepdims=True)
        dxh = dy * gg
        dx_ref[...] = r * (dxh - xh * jnp.mean(dxh * xh, axis=-1, keepdims=True))

    row = pl.BlockSpec((tm, D), lambda i: (i, 0))
    vec = pl.BlockSpec((1, D), lambda i: (0, 0))
    one = pl.BlockSpec((1, 1), lambda i: (0, 0))
    return pl.pallas_call(
        body, name="loss_head",
        out_shape=(jax.ShapeDtypeStruct((1, 1), F32), jax.ShapeDtypeStruct((S, D), F32),
                   jax.ShapeDtypeStruct((1, D), F32)),
        grid=(S // tm,),
        in_specs=[row, vec, row], out_specs=(one, row, vec),
        compiler_params=_params("arbitrary"),
    )(x, g.reshape(1, D), target)


def _slopes(g):
    return [2.0 ** (-8.0 * (HEADS_PER_GROUP * g + j + 1) / DSW_HEADS) for j in range(HEADS_PER_GROUP)]


def _band_masks(W):
    row = lax.broadcasted_iota(jnp.int32, (W, W), 0)
    col = lax.broadcasted_iota(jnp.int32, (W, W), 1)
    d_cur = row - col
    d_prev = d_cur + W
    return d_cur, d_prev, d_cur >= 0, d_cur <= 0


def _band_specs(W, nb, per):
    def cur(c):
        return pl.BlockSpec((None, per * W, OUT_A), lambda r, n: (r, n, c))

    def prev(c):
        return pl.BlockSpec((None, W, OUT_A), lambda r, n: (r, jnp.maximum(per * n - 1, 0), c))

    def nxt(c):
        return pl.BlockSpec((None, W, OUT_A), lambda r, n: (r, jnp.minimum(per * (n + 1), nb - 1), c))

    return cur, prev, nxt


def _blocks_per_step(nb):
    return 2 if nb % 2 == 0 else 1


def _head_stack(W):
    H, hd = HEADS_PER_GROUP, HEAD_DIM
    lane_head = lax.broadcasted_iota(jnp.int32, (W, OUT_A), 1) // hd

    def stack(x):
        return jnp.concatenate([jnp.where(lane_head == h, x, jnp.zeros_like(x)) for h in range(H)], axis=0)

    def unstack(y):
        out = jnp.where(lane_head == 0, y[0:W], 0.0)
        for h in range(1, H):
            out = jnp.where(lane_head == h, y[h * W:(h + 1) * W], out)
        return out

    def column(ref, rows, off=0):
        return jnp.concatenate([ref[rows, h * hd + off:h * hd + off + 1] for h in range(H)], axis=0)

    def tile(x):
        return jnp.concatenate([x] * H, axis=0)

    return stack, unstack, column, tile


def _stacked_bias(W, slopes, dil):
    d_cur, d_prev, m_cur, m_prev = _band_masks(W)
    b_cur = jnp.concatenate([(s * dil) * d_cur.astype(F32) for s in slopes], axis=0)
    b_prev = jnp.concatenate([(s * dil) * d_prev.astype(F32) for s in slopes], axis=0)
    H = len(slopes)
    return b_cur, b_prev, jnp.concatenate([m_cur] * H, axis=0), jnp.concatenate([m_prev] * H, axis=0)


def _attn_a_fwd(qkv, cols, g, comm=()):
    win, dil = DSW_GROUPS[g]
    W = win // dil
    d, L, _ = qkv[0].shape
    nb = L // W
    per = _blocks_per_step(nb)
    slopes = _slopes(g)

    def body(q_ref, kp_ref, kc_ref, vp_ref, vc_ref, o_ref, l_ref):
        n = pl.program_id(1)
        stack, unstack, _, _ = _head_stack(W)
        b_cur, b_prev, m_cur, m_prev = _stacked_bias(W, slopes, dil)
        m_first = jnp.logical_and(m_prev, n > 0)
        for b in range(per):
            rows = slice(b * W, (b + 1) * W)
            before = slice((b - 1) * W, b * W)
            qs = stack(q_ref[rows, :])
            kc, vc = kc_ref[rows, :], vc_ref[rows, :]
            kp, vp = (kp_ref[...], vp_ref[...]) if b == 0 else (kc_ref[before, :], vc_ref[before, :])
            s_c = jnp.where(m_cur, _dot(qs, kc, _NT) * ATT_SCALE - b_cur, NEG)
            s_p = jnp.where(m_first if b == 0 else m_prev, _dot(qs, kp, _NT) * ATT_SCALE - b_prev, NEG)
            m = jnp.maximum(jnp.max(s_c, axis=1, keepdims=True), jnp.max(s_p, axis=1, keepdims=True))
            p_c = jnp.exp(s_c - m)
            p_p = jnp.exp(s_p - m)
            den = jnp.sum(p_c, axis=1, keepdims=True) + jnp.sum(p_p, axis=1, keepdims=True)
            pv = _dot(p_c.astype(BF), vc) + _dot(p_p.astype(BF), vp)
            o_ref[rows, :] = unstack(pv / den)
            l_ref[rows, :] = unstack(jnp.broadcast_to(m + jnp.log(den), pv.shape))

    cur, prev, _ = _band_specs(W, nb, per)
    out = jax.ShapeDtypeStruct((d, L, OUT_A), F32)
    res = _call(body, name=f"attn_a_fwd_g{g}", grid=(d, nb // per),
                in_specs=[cur(cols[0]), prev(cols[1]), cur(cols[1]), prev(cols[2]), cur(cols[2])],
                out_specs=(cur(0), cur(0)), out_shape=(out, out),
                args=(qkv[0], qkv[1], qkv[1], qkv[2], qkv[2]), sem=("parallel", "parallel"), comm=comm)
    return res[0], res[1], list(res[2:])


def _attn_a_bwd(qkv, cols, do, stats, g):
    win, dil = DSW_GROUPS[g]
    W = win // dil
    d, L, _ = qkv[0].shape
    nb = L // W
    per = _blocks_per_step(nb)
    nsteps = nb // per
    slopes = _slopes(g)

    def body(q_ref, qn_ref, kp_ref, kc_ref, vp_ref, vc_ref, do_ref, don_ref, st_ref, stn_ref,
             dq_ref, dk_ref, dv_ref):
        n = pl.program_id(1)
        stack, unstack, column, _ = _head_stack(W)
        b_cur, b_prev, m_cur, m_prev = _stacked_bias(W, slopes, dil)
        m_first = jnp.logical_and(m_prev, n > 0)
        m_last = jnp.logical_and(m_prev, n < nsteps - 1)
        everything = slice(None)
        for b in range(per):
            rows = slice(b * W, (b + 1) * W)
            before = slice((b - 1) * W, b * W)
            after = slice((b + 1) * W, (b + 2) * W)
            first, last = b == 0, b == per - 1
            qs = stack(q_ref[rows, :])
            qn = stack(qn_ref[...] if last else q_ref[after, :])
            dos = stack(do_ref[rows, :])
            don = stack(don_ref[...] if last else do_ref[after, :])
            kc, vc = kc_ref[rows, :], vc_ref[rows, :]
            kp, vp = (kp_ref[...], vp_ref[...]) if first else (kc_ref[before, :], vc_ref[before, :])
            lse_c, dsum_c = column(st_ref, rows), column(st_ref, rows, STAT_OFF)
            lse_n = column(stn_ref, everything) if last else column(st_ref, after)
            dsum_n = column(stn_ref, everything, STAT_OFF) if last else column(st_ref, after, STAT_OFF)
            m_p = m_first if first else m_prev
            m_n = m_last if last else m_prev
            p_cc = jnp.exp(jnp.where(m_cur, _dot(qs, kc, _NT) * ATT_SCALE - b_cur, NEG) - lse_c)
            p_cp = jnp.exp(jnp.where(m_p, _dot(qs, kp, _NT) * ATT_SCALE - b_prev, NEG) - lse_c)
            p_nc = jnp.exp(jnp.where(m_n, _dot(qn, kc, _NT) * ATT_SCALE - b_prev, NEG) - lse_n)
            ds_cc = (p_cc * (_dot(dos, vc, _NT) - dsum_c) * ATT_SCALE).astype(BF)
            ds_cp = (p_cp * (_dot(dos, vp, _NT) - dsum_c) * ATT_SCALE).astype(BF)
            ds_nc = (p_nc * (_dot(don, vc, _NT) - dsum_n) * ATT_SCALE).astype(BF)
            dq_ref[rows, :] = unstack(_dot(ds_cc, kc) + _dot(ds_cp, kp)).astype(BF)
            dk_ref[rows, :] = (_dot(ds_cc, qs, _TN) + _dot(ds_nc, qn, _TN)).astype(BF)
            dv_ref[rows, :] = (_dot(p_cc.astype(BF), dos, _TN) + _dot(p_nc.astype(BF), don, _TN)).astype(BF)

    cur, prev, nxt = _band_specs(W, nb, per)
    out = jax.ShapeDtypeStruct((d, L, OUT_A), BF)
    cq, ck, cv = cols
    return pl.pallas_call(
        body, name=f"attn_a_bwd_g{g}",
        out_shape=(out, out, out),
        grid=(d, nsteps),
        in_specs=[cur(cq), nxt(cq), prev(ck), cur(ck), prev(cv), cur(cv), cur(0), nxt(0), cur(0), nxt(0)],
        out_specs=(cur(0), cur(0), cur(0)),
        compiler_params=_params("parallel", "parallel"),
    )(qkv[0], qkv[0], qkv[1], qkv[1], qkv[2], qkv[2], do, do, stats, stats)


SB_PAIR = 2
SB_QROWS = SB_BLOCK


def _softplus_parts(z):
    e = jnp.exp(-jnp.abs(z))
    log1p_e = jnp.where(e < 1e-4, e, jnp.log(1.0 + e))
    return e, jnp.maximum(z, 0.0) + log1p_e


def _split_dot(x, t):
    hi = x.astype(BF)
    lo = (x - hi.astype(F32)).astype(BF)
    return _dot(hi, t) + _dot(lo, t)


def _sb_block(qh, kk, causal, r_run, tri_incl):
    z = _dot(qh, kk, _NT)
    e, sp = _softplus_parts(z)
    ls = jnp.where(causal, -sp, 0.0)
    cin = _split_dot(ls, tri_incl)
    a = jnp.where(causal, jnp.exp(z + cin + r_run), 0.0)
    return z, e, cin, a


def _sb_specs(S):
    Q, hd = SB_QROWS, HEAD_DIM
    lanes = SB_PAIR * hd
    qc = (3 * W_A) // lanes
    kc = (3 * W_A + W_B) // lanes
    vc = (3 * W_A + 2 * W_B) // lanes
    q_spec = pl.BlockSpec((Q, lanes), lambda p, i: (i, qc + p))
    k_spec = pl.BlockSpec((S, lanes), lambda p, i: (0, kc + p))
    v_spec = pl.BlockSpec((S, lanes), lambda p, i: (0, vc + p))
    o_spec = pl.BlockSpec((Q, lanes), lambda p, i: (i, p))
    full = pl.BlockSpec((S, lanes), lambda p, i: (0, p))
    return q_spec, k_spec, v_spec, o_spec, full


def _sb_stack():
    Q, hd = SB_QROWS, HEAD_DIM
    lane_head = lax.broadcasted_iota(jnp.int32, (Q, SB_PAIR * hd), 1) // hd

    def stack(x):
        return jnp.concatenate([jnp.where(lane_head == h, x, jnp.zeros_like(x)) for h in range(SB_PAIR)], axis=0)

    def unstack(y):
        out = jnp.where(lane_head == 0, y[0:Q], 0.0)
        for h in range(1, SB_PAIR):
            out = jnp.where(lane_head == h, y[h * Q:(h + 1) * Q], out)
        return out

    return stack, unstack


def _sb_iotas(i):
    B, Q = SB_BLOCK, SB_QROWS
    row = lax.broadcasted_iota(jnp.int32, (Q, B), 0) + i * Q
    col = lax.broadcasted_iota(jnp.int32, (Q, B), 1)
    ahead = jnp.concatenate([col - row] * SB_PAIR, axis=0)
    tr = lax.broadcasted_iota(jnp.int32, (B, B), 0)
    tc = lax.broadcasted_iota(jnp.int32, (B, B), 1)
    return ahead, tr, tc


def _sb_fwd(proj, comm=()):
    S = proj.shape[0]
    B, Q, hd = SB_BLOCK, SB_QROWS, HEAD_DIM
    nq = S // Q
    R = SB_PAIR * Q
    q_spec, k_spec, v_spec, o_spec, _ = _sb_specs(S)

    def body(q_ref, k_ref, v_ref, o_ref):
        i = pl.program_id(1)
        ahead, tr, tc = _sb_iotas(i)
        tri_incl = (tr >= tc).astype(BF)
        stack, unstack = _sb_stack()
        qs = stack(q_ref[...] * ATT_SCALE)

        def cond(c):
            return jnp.logical_and(c[0] >= 0, c[-1] > SB_EXIT)

        def step(c):
            kb, r_run, acc, _ = c
            off = pl.multiple_of(kb * B, B)
            causal = ahead < -kb * B
            _, _, cin, a = _sb_block(qs, k_ref[pl.ds(off, B), :], causal, r_run, tri_incl)
            acc = acc + _dot(a.astype(BF), v_ref[pl.ds(off, B), :])
            r_run = r_run + cin[:, 0:1]
            return kb - 1, r_run, acc, jnp.max(r_run)

        init = (i, jnp.zeros((R, 1), F32), jnp.zeros((R, SB_PAIR * hd), F32), jnp.float32(0.0))
        fin = lax.while_loop(cond, step, init)
        o_ref[...] = unstack(fin[2])

    res = _call(body, name="sb_fwd", grid=(SB_HEADS // SB_PAIR, nq), in_specs=[q_spec, k_spec, v_spec],
                out_specs=o_spec, out_shape=jax.ShapeDtypeStruct((S, W_B), F32), args=(proj, proj, proj),
                sem=("parallel", "parallel"), comm=comm)
    return res[0], list(res[1:])


def _sb_bwd(proj, do, o, comm=()):
    S = proj.shape[0]
    B, Q, hd = SB_BLOCK, SB_QROWS, HEAD_DIM
    nq = S // Q
    R = SB_PAIR * Q
    q_spec, k_spec, v_spec, o_spec, full = _sb_specs(S)

    def body(q_ref, k_ref, v_ref, do_ref, o_ref, dq_ref, dk_ref, dv_ref):
        i = pl.program_id(1)

        @pl.when(i == 0)
        def _():
            dk_ref[...] = jnp.zeros_like(dk_ref)
            dv_ref[...] = jnp.zeros_like(dv_ref)

        ahead, tr, tc = _sb_iotas(i)
        tri_incl = (tr >= tc).astype(BF)
        tri_strict = (tr > tc).astype(BF)
        stack, unstack = _sb_stack()
        qs = stack(q_ref[...] * ATT_SCALE)
        dobs = stack(do_ref[...])
        o_all = o_ref[...]
        dsum = jnp.sum(dobs.astype(F32) * jnp.concatenate([o_all] * SB_PAIR, axis=0), axis=1, keepdims=True)

        def cond(c):
            return jnp.logical_and(c[0] >= 0, c[-1] > SB_EXIT)

        def step(c):
            kb, r_run, g_run, dq, _ = c
            off = pl.multiple_of(kb * B, B)
            causal = ahead < -kb * B
            kk = k_ref[pl.ds(off, B), :]
            vv = v_ref[pl.ds(off, B), :]
            z, e, cin, a = _sb_block(qs, kk, causal, r_run, tri_incl)
            a16 = a.astype(BF)
            gmat = a16.astype(F32) * _dot(dobs, vv, _NT)
            later = _split_dot(gmat, tri_strict)
            pfx = dsum - g_run - later
            sig = jnp.where(z >= 0, 1.0, e) / (1.0 + e)
            dz = jnp.where(causal, gmat - sig * pfx, 0.0).astype(BF)
            dq = dq + _dot(dz, kk)
            dk_ref[pl.ds(off, B), :] += _dot(dz, qs, _TN)
            dv_ref[pl.ds(off, B), :] += _dot(a16, dobs, _TN)
            g_run = g_run + jnp.sum(gmat, axis=1, keepdims=True)
            r_run = r_run + cin[:, 0:1]
            return kb - 1, r_run, g_run, dq, jnp.max(r_run)

        init = (i, jnp.zeros((R, 1), F32), jnp.zeros((R, 1), F32), jnp.zeros((R, SB_PAIR * hd), F32),
                jnp.float32(0.0))
        fin = lax.while_loop(cond, step, init)
        dq_ref[...] = unstack(fin[3]) * ATT_SCALE

    out = jax.ShapeDtypeStruct((S, W_B), F32)
    res = _call(body, name="sb_bwd", grid=(SB_HEADS // SB_PAIR, nq), in_specs=[q_spec, k_spec, v_spec, o_spec, o_spec],
                out_specs=(o_spec, full, full), out_shape=(out, out, out), args=(proj, proj, proj, do, o), comm=comm)
    return res[0], res[1], res[2], list(res[3:])


def _merge_fwd(o_g, l_g, o_b, proj, b_gate, w_br):
    S = o_b.shape[0]
    D = D_MODEL
    tm = _pick(S, 256)
    gcol = GATE_OFF // D

    def body(o0, o1, o2, l0, l1, l2, ob_ref, ga_ref, gb_ref, bg_ref, w_ref, mg_ref, oa_ref, lse_ref):
        la, lb, lc = l0[...], l1[...], l2[...]
        mx = jnp.maximum(jnp.maximum(la, lb), lc)
        ea, eb, ec = jnp.exp(la - mx), jnp.exp(lb - mx), jnp.exp(lc - mx)
        den = ea + eb + ec
        oa = (ea * o0[...] + eb * o1[...] + ec * o2[...]) / den
        oa_ref[...] = oa
        lse_ref[...] = mx + jnp.log(den)
        ya = _dot(oa.astype(BF), w_ref[0:OUT_A, :])
        yb = _dot(ob_ref[...].astype(BF), w_ref[OUT_A:OUT_A + W_B, :])
        bg = bg_ref[...]
        g_a = jax.nn.sigmoid(ga_ref[...].astype(F32) + bg[:, 0:D])
        g_b = jax.nn.sigmoid(gb_ref[...].astype(F32) + bg[:, D:2 * D])
        mg_ref[...] = (g_a * ya + g_b * yb).astype(BF)

    nar = pl.BlockSpec((tm, OUT_A), lambda i: (i, 0))
    wide = pl.BlockSpec((tm, D), lambda i: (i, 0))
    return pl.pallas_call(
        body, name="merge_fwd",
        out_shape=(jax.ShapeDtypeStruct((S, D), BF), jax.ShapeDtypeStruct((S, OUT_A), F32),
                   jax.ShapeDtypeStruct((S, OUT_A), F32)),
        grid=(S // tm,),
        in_specs=[nar] * 7 + [pl.BlockSpec((tm, D), lambda i: (i, gcol)),
                              pl.BlockSpec((tm, D), lambda i: (i, gcol + 1)),
                              pl.BlockSpec((1, 2 * D), lambda i: (0, 0)),
                              pl.BlockSpec((OUT_A + W_B, D), lambda i: (0, 0))],
        out_specs=(wide, nar, nar),
        compiler_params=_params("parallel"),
    )(*o_g, *l_g, o_b, proj, proj, b_gate.reshape(1, 2 * D), w_br)


STAT_OFF = HEAD_DIM // 2


def _merge_bwd(dm, oa, ob, lse, proj, b_gate, w_br):
    S = ob.shape[0]
    D = D_MODEL
    tm = _pick(S, 256)
    gcol = GATE_OFF // D

    def body(dm_ref, oa_ref, ob_ref, l_ref, ga_ref, gb_ref, bg_ref, w_ref,
             dya_ref, dyb_ref, doa_ref, dob_ref, st_ref, dg_ref, dbg_ref):
        @pl.when(pl.program_id(0) == 0)
        def _():
            dbg_ref[...] = jnp.zeros_like(dbg_ref)

        wa = w_ref[0:OUT_A, :]
        wb = w_ref[OUT_A:OUT_A + W_B, :]
        oa = oa_ref[...]
        ya = _dot(oa.astype(BF), wa)
        yb = _dot(ob_ref[...].astype(BF), wb)
        bg = bg_ref[...]
        g_a = jax.nn.sigmoid(ga_ref[...].astype(F32) + bg[:, 0:D])
        g_b = jax.nn.sigmoid(gb_ref[...].astype(F32) + bg[:, D:2 * D])
        dm = dm_ref[...].astype(F32)
        dga = dm * ya * g_a * (1.0 - g_a)
        dgb = dm * yb * g_b * (1.0 - g_b)
        dg_ref[:, 0:D] = dga.astype(BF)
        dg_ref[:, D:2 * D] = dgb.astype(BF)
        dbg_ref[:, 0:D] += jnp.sum(dga, axis=0, keepdims=True)
        dbg_ref[:, D:2 * D] += jnp.sum(dgb, axis=0, keepdims=True)
        dya = (dm * g_a).astype(BF)
        dyb = (dm * g_b).astype(BF)
        dya_ref[...] = dya
        dyb_ref[...] = dyb
        doa = _dot(dya, wa, _NT).astype(BF)
        doa_ref[...] = doa
        dob_ref[...] = _dot(dyb, wb, _NT).astype(BF)
        r = lax.broadcasted_iota(jnp.int32, (OUT_A, OUT_A), 0) // HEAD_DIM
        c = lax.broadcasted_iota(jnp.int32, (OUT_A, OUT_A), 1) // HEAD_DIM
        dsum = _split_dot(doa.astype(F32) * oa, (r == c).astype(BF))
        lane = lax.broadcasted_iota(jnp.int32, dsum.shape, 1) % HEAD_DIM
        st_ref[...] = jnp.where(lane < STAT_OFF, l_ref[...], dsum)

    nar = pl.BlockSpec((tm, OUT_A), lambda i: (i, 0))
    wide = pl.BlockSpec((tm, D), lambda i: (i, 0))
    wide2 = pl.BlockSpec((tm, 2 * D), lambda i: (i, 0))
    vec2 = pl.BlockSpec((1, 2 * D), lambda i: (0, 0))
    return pl.pallas_call(
        body, name="merge_bwd",
        out_shape=(jax.ShapeDtypeStruct((S, D), BF), jax.ShapeDtypeStruct((S, D), BF),
                   jax.ShapeDtypeStruct((S, OUT_A), BF), jax.ShapeDtypeStruct((S, W_B), BF),
                   jax.ShapeDtypeStruct((S, OUT_A), F32), jax.ShapeDtypeStruct((S, 2 * D), BF),
                   jax.ShapeDtypeStruct((1, 2 * D), F32)),
        grid=(S // tm,),
        in_specs=[wide, nar, nar, nar, pl.BlockSpec((tm, D), lambda i: (i, gcol)),
                  pl.BlockSpec((tm, D), lambda i: (i, gcol + 1)), vec2,
                  pl.BlockSpec((OUT_A + W_B, D), lambda i: (0, 0))],
        out_specs=(wide, wide, nar, nar, nar, wide2, vec2),
        compiler_params=_params("arbitrary"),
    )(dm, oa, ob, lse, proj, proj, b_gate.reshape(1, 2 * D), w_br)


_SQRT_HALF = 0.7071067811865476
_INV_SQRT_2PI = 0.3989422804014327


def _gelu_parts(a):
    cdf = 0.5 * (1.0 + lax.erf(a * _SQRT_HALF))
    pdf = _INV_SQRT_2PI * jnp.exp(-0.5 * a * a)
    return cdf, pdf


def _shift_down(a, halo, k):
    rows = lax.broadcasted_iota(jnp.int32, a.shape, 0)
    out = pltpu.roll(a, k, 0)
    for r in range(k):
        out = jnp.where(rows == r, halo[8 - k + r:8 - k + r + 1, :], out)
    return out


def _shift_up(a, halo, k):
    n = a.shape[0]
    rows = lax.broadcasted_iota(jnp.int32, a.shape, 0)
    out = pltpu.roll(a, n - k, 0)
    for r in range(k):
        out = jnp.where(rows == n - k + r, halo[r:r + 1, :], out)
    return out


def _conv_in(a_ref, h_ref, first):
    a = a_ref[...].astype(F32)
    halo = jnp.where(first, 0.0, h_ref[...].astype(F32))
    return a, _shift_down(a, halo, 1), _shift_down(a, halo, 2)


def _ffn_specs(S, tm):
    F = D_FF
    t8 = tm // 8
    a_spec = pl.BlockSpec((tm, F), lambda i: (i, 0))
    v_spec = pl.BlockSpec((tm, F), lambda i: (i, 1))
    halo_prev = pl.BlockSpec((8, F), lambda i: (jnp.maximum(i * t8 - 1, 0), 0))
    return a_spec, v_spec, halo_prev


def _ffn_act_fwd(up, conv_w, conv_b):
    S = up.shape[0]
    F = D_FF
    tm = _pick(S, 256)
    a_spec, v_spec, halo_prev = _ffn_specs(S, tm)

    def body(a_ref, h_ref, v_ref, w_ref, b_ref, act_ref):
        a0, a1, a2 = _conv_in(a_ref, h_ref, pl.program_id(0) == 0)
        w = w_ref[...]
        ac = b_ref[...] + w[0:1, :] * a2 + w[1:2, :] * a1 + w[2:3, :] * a0
        cdf, _ = _gelu_parts(ac)
        act_ref[...] = (ac * cdf * v_ref[...].astype(F32)).astype(BF)

    return pl.pallas_call(
        body, name="ffn_act_fwd",
        out_shape=jax.ShapeDtypeStruct((S, F), BF),
        grid=(S // tm,),
        in_specs=[a_spec, halo_prev, v_spec, pl.BlockSpec((3, F), lambda i: (0, 0)),
                  pl.BlockSpec((1, F), lambda i: (0, 0))],
        out_specs=a_spec,
        compiler_params=_params("parallel"),
    )(up, up, up, conv_w, conv_b.reshape(1, F))


FFN_CHUNK = 256


def _ffn_down(up, conv_w, conv_b, w_down, res, *, name, comm=()):
    S = up.shape[0]
    F = D_FF
    D = w_down.shape[1]
    tm = _pick(S, 256)
    a_spec, v_spec, halo_prev = _ffn_specs(S, tm)

    def body(a_ref, h_ref, v_ref, w_ref, b_ref, wd_ref, r_ref, o_ref, act_ref, ac_ref):
        first = pl.program_id(0) == 0
        acc = r_ref[...]
        for c0 in range(0, F, FFN_CHUNK):
            cs = slice(c0, c0 + FFN_CHUNK)
            a = a_ref[:, cs].astype(F32)
            halo = jnp.where(first, 0.0, h_ref[:, cs].astype(F32))
            w = w_ref[:, cs]
            ac = b_ref[:, cs] + w[0:1, :] * _shift_down(a, halo, 2) + w[1:2, :] * _shift_down(a, halo, 1) + w[2:3, :] * a
            ac_ref[:, cs] = ac.astype(BF)
            cdf, _ = _gelu_parts(ac)
            act = (ac * cdf * v_ref[:, cs].astype(F32)).astype(BF)
            act_ref[:, cs] = act
            acc = acc + _dot(act, wd_ref[cs, :])
        o_ref[...] = acc

    row = pl.BlockSpec((tm, D), lambda i: (i, 0))
    res_ = _call(body, name=name, grid=(S // tm,),
                 in_specs=[a_spec, halo_prev, v_spec, pl.BlockSpec((3, F), lambda i: (0, 0)),
                           pl.BlockSpec((1, F), lambda i: (0, 0)), pl.BlockSpec((F, D), lambda i: (0, 0)), row],
                 out_specs=(row, a_spec, a_spec),
                 out_shape=(jax.ShapeDtypeStruct((S, D), F32), jax.ShapeDtypeStruct((S, F), BF),
                            jax.ShapeDtypeStruct((S, F), BF)),
                 args=(up, up, up, conv_w, conv_b.reshape(1, F), w_down, res), comm=comm)
    return res_[0], res_[1], res_[2], list(res_[3:])


def _shift_up_pair(a, nxt):
    n = a.shape[0]
    r8 = lax.broadcasted_iota(jnp.int32, (8,) + a.shape[1:], 0)
    out = []
    for k in (1, 2):
        rolled = pltpu.roll(a, n - k, 0)
        tail = jnp.where(r8 >= 8 - k, pltpu.roll(nxt, 8 - k, 0), rolled[n - 8:n])
        out.append(jnp.concatenate([rolled[0:n - 8], tail], axis=0))
    return out


def _ffn_bwd(dx, w_down, up, ac, conv_w, comm=()):
    S = up.shape[0]
    F = D_FF
    D = dx.shape[1]
    tm = _pick(S, 256)
    t8 = tm // 8
    nt = S // tm
    a_spec, v_spec, _ = _ffn_specs(S, tm)

    def nxt(width, col):
        return pl.BlockSpec((8, width), lambda i: (jnp.minimum((i + 1) * t8, S // 8 - 1), col))

    def body(dx_ref, dxn_ref, wd_ref, ac_ref, acn_ref, a_ref, v_ref, vn_ref, w_ref, dup_ref, dw_ref, db_ref):
        i = pl.program_id(0)

        @pl.when(i == 0)
        def _():
            dw_ref[...] = jnp.zeros_like(dw_ref)
            db_ref[...] = jnp.zeros_like(db_ref)

        dx16 = dx_ref[...].astype(BF)
        dxn16 = dxn_ref[...].astype(BF)
        last = i == nt - 1

        def dconv(dact, ac, v):
            cdf, pdf = _gelu_parts(ac)
            return cdf, dact * v * (cdf + ac * pdf)

        for c0 in range(0, F, FFN_CHUNK):
            cs = slice(c0, c0 + FFN_CHUNK)
            wd = wd_ref[cs, :]
            dact = _dot(dx16, wd, _NT)
            ac = ac_ref[:, cs].astype(F32)
            cdf, dac = dconv(dact, ac, v_ref[:, cs].astype(F32))
            dup_ref[:, F + c0:F + c0 + FFN_CHUNK] = (dact * ac * cdf).astype(BF)
            _, dac_n = dconv(_dot(dxn16, wd, _NT), acn_ref[:, cs].astype(F32), vn_ref[:, cs].astype(F32))
            d1, d2 = _shift_up_pair(dac, jnp.where(last, 0.0, dac_n))
            w = w_ref[:, cs]
            dup_ref[:, cs] = (w[2:3, :] * dac + w[1:2, :] * d1 + w[0:1, :] * d2).astype(BF)
            a = a_ref[:, cs].astype(F32)
            db_ref[:, cs] += jnp.sum(dac, axis=0, keepdims=True)
            dw_ref[0:1, cs] += jnp.sum(d2 * a, axis=0, keepdims=True)
            dw_ref[1:2, cs] += jnp.sum(d1 * a, axis=0, keepdims=True)
            dw_ref[2:3, cs] += jnp.sum(dac * a, axis=0, keepdims=True)

    w_spec = pl.BlockSpec((3, F), lambda i: (0, 0))
    b_spec = pl.BlockSpec((1, F), lambda i: (0, 0))
    res = _call(body, name="ffn_bwd", grid=(nt,),
                in_specs=[pl.BlockSpec((tm, D), lambda i: (i, 0)), nxt(D, 0), pl.BlockSpec((F, D), lambda i: (0, 0)),
                          a_spec, nxt(F, 0), a_spec, v_spec, nxt(F, 1), w_spec],
                out_specs=(pl.BlockSpec((tm, 2 * F), lambda i: (i, 0)), w_spec, b_spec),
                out_shape=(jax.ShapeDtypeStruct((S, 2 * F), BF), jax.ShapeDtypeStruct((3, F), F32),
                           jax.ShapeDtypeStruct((1, F), F32)),
                args=(dx, dx, w_down, ac, ac, up, up, up, conv_w), comm=comm)
    return res[0], res[1], res[2], list(res[3:])


def _adamw(parts, w, m, v, name):
    R, C = w.shape
    tr = R
    for cand in (512, 256, 128, 64, 32, 16):
        if R % cand == 0 and cand * C * 4 <= (1 << 21):
            tr = cand
            break
    c1 = 1.0 / (1.0 - ADAM_B1 ** ADAM_STEP)
    c2 = 1.0 / (1.0 - ADAM_B2 ** ADAM_STEP)

    def body(p_ref, w_ref, m_ref, v_ref, g_ref, d_ref, nm_ref, nv_ref):
        g = p_ref[0].astype(F32)
        for j in range(1, N_DEV):
            g = g + p_ref[j].astype(F32)
        nm = ADAM_B1 * m_ref[...] + (1.0 - ADAM_B1) * g
        nv = ADAM_B2 * v_ref[...] + (1.0 - ADAM_B2) * (g * g)
        g_ref[...] = g
        nm_ref[...] = nm
        nv_ref[...] = nv
        d_ref[...] = -ADAM_LR * ((nm * c1) / (jnp.sqrt(nv * c2) + ADAM_EPS) + ADAM_WD * w_ref[...])

    blk = pl.BlockSpec((tr, C), lambda i: (i, 0))
    out = jax.ShapeDtypeStruct((R, C), F32)
    return pl.pallas_call(
        body, name=name,
        out_shape=(out, out, out, out),
        grid=(R // tr,),
        in_specs=[pl.BlockSpec((N_DEV, tr, C), lambda i: (0, i, 0)), blk, blk, blk],
        out_specs=(blk, blk, blk, blk),
        compiler_params=_params("parallel"),
    )(parts, w, m, v)


def _dil(t, dil):
    S, C = t.shape
    if dil == 1:
        return t.reshape(1, S, C)
    return t.reshape(S // dil, dil, C).transpose(1, 0, 2)


def _undil(t):
    d, L, C = t.shape
    if d == 1:
        return t.reshape(L, C)
    return t.transpose(1, 0, 2).reshape(L * d, C)


def _group_qkv(proj, g):
    dil = DSW_GROUPS[g][1]
    if dil == 1:
        p3 = _dil(proj, 1)
        return (p3, p3, p3), (g, W_A // OUT_A + g, 2 * W_A // OUT_A + g)
    c0 = g * OUT_A
    return tuple(_dil(proj[:, o + c0:o + c0 + OUT_A], dil) for o in (0, W_A, 2 * W_A)), (0, 0, 0)


_COL_SHARDED = ("w_in", "w_br", "w_up")
_ROW_SHARDED = ("w_o", "w_down")


class _Plan:
    def __init__(self):
        self.riders = {}
        self.landed = {}

    def ride(self, slot, key, kind, x):
        self.riders.setdefault(slot, []).append((key, kind, x))

    def run(self, slot, fn, *args, **kw):
        items = self.riders.pop(slot, [])
        res = fn(*args, comm=[(kind, x) for _, kind, x in items], **kw)
        for (key, _, _), r in zip(items, res[-1]):
            self.landed[key] = r
        return res[0] if len(res) == 2 else res[:-1]

    def weight(self, n, l):
        g = self.landed[(n, l)]
        if n in _COL_SHARDED:
            return g.transpose(1, 0, 2).reshape(g.shape[1], -1)
        return g.reshape(-1, g.shape[2])

    def scatter(self, slot, n, l, full):
        K, N = full.shape
        if n in _COL_SHARDED:
            blocks = full.reshape(K, N_DEV, N // N_DEV).transpose(1, 0, 2)
        else:
            blocks = full.reshape(N_DEV, K // N_DEV, N)
        self.ride(slot, ("d" + n, l), "scatter", blocks)


def _layer_fwd(x, p, plan, l):
    h = _rms_fwd(x, p["norm1"][l], f"rms1_fwd_{l}")
    proj = plan.run(f"proj_{l}", _matmul, h, plan.weight("w_in", l), mode="nn", out_dtype=BF, name=f"proj_{l}",
                    tm=1024, tn=1024, tk=1024)
    o_g, l_g, qkv_g = [], [], []
    for g in range(N_GROUPS):
        qkv, cols = _group_qkv(proj, g)
        og, lg = plan.run(f"attn_a_fwd_g{g}_{l}", _attn_a_fwd, qkv, cols, g)
        o_g.append(_undil(og))
        l_g.append(_undil(lg))
        qkv_g.append((qkv, cols))
    ob = plan.run(f"sb_fwd_{l}", _sb_fwd, proj)
    merged, oa, lse = _merge_fwd(o_g, l_g, ob, proj, p["b_gate"][l], plan.weight("w_br", l))
    x1 = plan.run(f"wo_{l}", _matmul, merged, plan.weight("w_o", l), mode="nn", out_dtype=F32, name=f"wo_{l}",
                  tm=1024, tn=1024, tk=1024, res=x)
    h2 = _rms_fwd(x1, p["norm2"][l], f"rms2_fwd_{l}")
    up = plan.run(f"up_{l}", _matmul, h2, plan.weight("w_up", l), mode="nn", out_dtype=BF, name=f"up_{l}",
                  tm=1024, tn=1408, tk=1024)
    x2, act, ac = plan.run(f"down_{l}", _ffn_down, up, p["conv_w"][l], p["conv_b"][l], plan.weight("w_down", l), x1,
                           name=f"down_{l}")
    saved = dict(x=x, h=h, proj=proj, qkv_g=qkv_g, oa=oa, ob=ob, lse=lse, merged=merged, x1=x1, h2=h2, up=up, act=act, ac=ac)
    return x2, saved


def _layer_bwd(dx2, sv, p, plan, l):
    gr = {}
    dwd = plan.run(f"dw_down_{l}", _matmul, sv["act"], dx2, mode="tn", out_dtype=BF, name=f"dw_down_{l}",
                   tm=1408, tn=1024, tk=2048)
    plan.scatter(f"d_h2_{l}", "w_down", l, dwd)
    dup, gr["conv_w"], dcb = plan.run(f"ffn_bwd_{l}", _ffn_bwd, dx2, plan.weight("w_down", l), sv["up"], sv["ac"],
                                      p["conv_w"][l])
    gr["conv_b"] = dcb[0]
    dx1, dn2 = plan.run(f"d_h2_{l}", _matmul_rms_bwd, dup, plan.weight("w_up", l), sv["x1"], p["norm2"][l], dx2,
                        name=f"d_h2_{l}")
    dwu = plan.run(f"dw_up_{l}", _matmul, sv["h2"], dup, mode="tn", out_dtype=BF, name=f"dw_up_{l}",
                   tm=1024, tn=1408, tk=2048)
    plan.scatter(f"sb_bwd_{l}", "w_up", l, dwu)
    gr["norm2"] = dn2[0]
    dm = plan.run(f"d_merged_{l}", _matmul, dx1, plan.weight("w_o", l), mode="nt", out_dtype=BF,
                  name=f"d_merged_{l}", tm=1024, tn=1024, tk=1024)
    dwo = plan.run(f"dw_o_{l}", _matmul, sv["merged"], dx1, mode="tn", out_dtype=BF, name=f"dw_o_{l}",
                   tm=1024, tn=1024, tk=2048)
    plan.scatter(f"dw_in_{l}", "w_o", l, dwo)
    dya, dyb, doa, dob, stats, dgate, dbg = _merge_bwd(dm, sv["oa"], sv["ob"], sv["lse"], sv["proj"], p["b_gate"][l],
                                                       plan.weight("w_br", l))
    gr["b_gate"] = dbg[0]
    dwa = plan.run(f"dw_bra_{l}", _matmul, sv["oa"], dya, mode="tn", out_dtype=BF, name=f"dw_bra_{l}",
                   tm=256, tn=1024, tk=2048)
    dwb = plan.run(f"dw_brb_{l}", _matmul, sv["ob"], dyb, mode="tn", out_dtype=BF, name=f"dw_brb_{l}",
                   tm=256, tn=1024, tk=2048)
    plan.scatter(f"dw_in_{l}", "w_br", l, jnp.concatenate([dwa, dwb], axis=0))
    proj = sv["proj"]
    dq_a, dk_a, dv_a = [], [], []
    for g, (_, dil) in enumerate(DSW_GROUPS):
        qkv, cols = sv["qkv_g"][g]
        dqg, dkg, dvg = _attn_a_bwd(qkv, cols, _dil(doa, dil), _dil(stats, dil), g)
        dq_a.append(_undil(dqg))
        dk_a.append(_undil(dkg))
        dv_a.append(_undil(dvg))
    dqb, dkb, dvb = plan.run(f"sb_bwd_{l}", _sb_bwd, proj, dob, sv["ob"])
    dproj = jnp.concatenate(dq_a + dk_a + dv_a + [dqb.astype(BF), dkb.astype(BF), dvb.astype(BF), dgate], axis=1)
    dx, dn1 = plan.run(f"d_h_{l}", _matmul_rms_bwd, dproj, plan.weight("w_in", l), sv["x"], p["norm1"][l], dx1,
                       name=f"d_h_{l}")
    dwi = plan.run(f"dw_in_{l}", _matmul, sv["h"], dproj, mode="tn", out_dtype=BF, name=f"dw_in_{l}",
                   tm=1024, tn=1280, tk=2048)
    plan.scatter(f"ffn_bwd_{l - 1}", "w_in", l, dwi)
    gr["norm1"] = dn1[0]
    return dx, gr


def kernel(x, norm1, w_in, b_gate, w_br, w_o, norm2, w_up, conv_w, conv_b, w_down, norm_f, loss_target, m_norm1, m_w_in, m_b_gate, m_w_br, m_w_o, m_norm2, m_w_up, m_conv_w, m_conv_b, m_w_down, m_norm_f, v_norm1, v_w_in, v_b_gate, v_w_br, v_w_o, v_norm2, v_w_up, v_conv_w, v_conv_b, v_w_down, v_norm_f):
    depth = norm1.shape[0]
    me = 4 * lax.axis_index("x") + 2 * lax.axis_index("y") + lax.axis_index("c")
    shards = dict(w_in=w_in, w_br=w_br, w_o=w_o, w_up=w_up, w_down=w_down)
    moments_m = dict(norm1=m_norm1, w_in=m_w_in, b_gate=m_b_gate, w_br=m_w_br, w_o=m_w_o, norm2=m_norm2,
                     w_up=m_w_up, conv_w=m_conv_w, conv_b=m_conv_b, w_down=m_w_down, norm_f=m_norm_f)
    moments_v = dict(norm1=v_norm1, w_in=v_w_in, b_gate=v_b_gate, w_br=v_w_br, w_o=v_w_o, norm2=v_norm2,
                     w_up=v_w_up, conv_w=v_conv_w, conv_b=v_conv_b, w_down=v_w_down, norm_f=v_norm_f)

    plan = _Plan()
    wb = {n: s.astype(BF) for n, s in shards.items()}
    p = dict(norm1=norm1, b_gate=b_gate, norm2=norm2, conv_b=conv_b)
    cw = _all_gather(conv_w, "gather_conv_w")
    p["conv_w"] = cw.transpose(1, 2, 0, 3).reshape(depth, 3, D_FF)
    plan.landed[("w_in", 0)] = _all_gather(wb["w_in"][0], "gather_w_in_0")
    for l in range(depth):
        plan.ride(f"proj_{l}", ("w_down", l), "gather", wb["w_down"][l])
        plan.ride(f"attn_a_fwd_g0_{l}" if l == 0 else f"down_{l - 1}", ("w_br", l), "gather", wb["w_br"][l])
        plan.ride(f"attn_a_fwd_g0_{l}" if l == 0 else f"down_{l - 1}", ("w_o", l), "gather", wb["w_o"][l])
        plan.ride(f"sb_fwd_{l}", ("w_up", l), "gather", wb["w_up"][l])
        if l + 1 < depth:
            plan.ride(f"up_{l}", ("w_in", l + 1), "gather", wb["w_in"][l + 1])

    xs = x[0]
    saved = []
    for l in range(depth):
        xs, sv = _layer_fwd(xs, p, plan, l)
        saved.append(sv)
    loss_part, dx, dnf = _loss_head(xs, norm_f, loss_target[0])
    loss = lax.psum(loss_part[0, 0], ("x", "y", "c"))

    grads = [None] * depth
    for l in reversed(range(depth)):
        dx, grads[l] = _layer_bwd(dx, saved[l], p, plan, l)
    grad_x = dx[None]
    (_, _, last), = plan.riders.pop("ffn_bwd_-1")
    plan.landed[("dw_in", 0)] = _all_to_all(last, "scatter_w_in_0")
    assert not plan.riders, sorted(plan.riders)

    out_g, out_d, out_m, out_v = {}, {}, {}, {}
    for n in _COL_SHARDED + _ROW_SHARDED:
        parts = jnp.concatenate([plan.landed[("d" + n, l)] for l in range(depth)], axis=1)
        shp = shards[n].shape
        flat = (shp[0] * shp[1], shp[2])
        res = _adamw(parts, shards[n].reshape(flat), moments_m[n].reshape(flat), moments_v[n].reshape(flat),
                     f"adamw_{n}")
        out_g[n], out_d[n], out_m[n], out_v[n] = [r.reshape(shp) for r in res]

    small = ("norm1", "b_gate", "norm2", "conv_b")
    vecs = [jnp.stack([grads[l][n] for l in range(depth)]).reshape(-1) for n in small]
    vecs.append(dnf.reshape(-1))
    vecs.append(jnp.stack([grads[l]["conv_w"] for l in range(depth)]).reshape(-1))
    sizes = [v.shape[0] for v in vecs]
    flat = jnp.concatenate(vecs)
    n_small = sum(sizes[:-1])
    pad = (-flat.shape[0]) % 1024
    flat = jnp.pad(flat, (0, pad)).reshape(-1, 128)
    allp = _all_gather(flat, "gather_small_grads").reshape(N_DEV, -1)
    rep_w = jnp.concatenate([norm1.reshape(-1), b_gate.reshape(-1), norm2.reshape(-1), conv_b.reshape(-1), norm_f])
    rep_m = jnp.concatenate([moments_m[n].reshape(-1) for n in small] + [m_norm_f])
    rep_v = jnp.concatenate([moments_v[n].reshape(-1) for n in small] + [v_norm_f])
    rows = n_small // 128
    res = _adamw(allp[:, :n_small].reshape(N_DEV, rows, 128), rep_w.reshape(rows, 128), rep_m.reshape(rows, 128),
                 rep_v.reshape(rows, 128), "adamw_small")
    off = 0
    for n, sz in zip(small + ("norm_f",), sizes[:-1]):
        shp = norm_f.shape if n == "norm_f" else p[n].shape
        out_g[n], out_d[n], out_m[n], out_v[n] = [r.reshape(-1)[off:off + sz].reshape(shp) for r in res]
        off += sz
    f = conv_w.shape[2]
    cwp = allp[:, n_small:n_small + sizes[-1]].reshape(N_DEV, depth * 3, D_FF)
    cwp = lax.dynamic_slice_in_dim(cwp, me * f, f, axis=2)
    res = _adamw(cwp, conv_w.reshape(depth * 3, f), m_conv_w.reshape(depth * 3, f), v_conv_w.reshape(depth * 3, f),
                 "adamw_conv_w")
    out_g["conv_w"], out_d["conv_w"], out_m["conv_w"], out_v["conv_w"] = [r.reshape(conv_w.shape) for r in res]

    order = ("norm1", "w_in", "b_gate", "w_br", "w_o", "norm2", "w_up", "conv_w", "conv_b", "w_down", "norm_f")
    return (loss, grad_x, *[out_g[n] for n in order], *[out_d[n] for n in order],
            *[out_m[n] for n in order], *[out_v[n] for n in order])
```

```python
import functools

import jax
import jax.numpy as jnp
from jax import lax
from jax.experimental import pallas as pl
from jax.experimental.pallas import tpu as pltpu

BF = jnp.bfloat16
F32 = jnp.float32

N_DEV = 8
D_MODEL = 1024
HEAD_DIM = 64
DSW_GROUPS = ((128, 1), (512, 4), (2048, 16))
HEADS_PER_GROUP = 4
N_GROUPS = len(DSW_GROUPS)
DSW_HEADS = HEADS_PER_GROUP * N_GROUPS
SB_HEADS = 4
W_A = DSW_HEADS * HEAD_DIM
W_B = SB_HEADS * HEAD_DIM
OUT_A = HEADS_PER_GROUP * HEAD_DIM
N_IN = 3 * W_A + 3 * W_B + 2 * D_MODEL
GATE_OFF = 3 * W_A + 3 * W_B
D_FF = 2816
SB_BLOCK = 256
RMS_EPS = 1e-6
ATT_SCALE = HEAD_DIM ** -0.5
NEG = -1e30
SB_EXIT = -110.0
FFN_CHUNK = 256

ADAM_LR = 0.001
ADAM_B1 = 0.9
ADAM_B2 = 0.999
ADAM_EPS = 1e-08
ADAM_WD = 0.01
ADAM_STEP = 10

HBM_SPEC = pl.BlockSpec(memory_space=pltpu.HBM)
MESH = pl.DeviceIdType.MESH

_NN = (((1,), (0,)), ((), ()))
_NT = (((1,), (1,)), ((), ()))
_TN = (((0,), (0,)), ((), ()))


def _dot(a, b, dn=_NN):
    return lax.dot_general(a, b, dn, preferred_element_type=F32)


def _pick(dim, pref):
    if dim <= pref:
        return dim
    t = (pref // 128) * 128
    while t >= 128:
        if dim % t == 0:
            return t
        t -= 128
    return dim


def _params(*sem):
    return pltpu.CompilerParams(dimension_semantics=sem)


def _peer(k):
    x, y, c = lax.axis_index("x"), lax.axis_index("y"), lax.axis_index("c")
    px = 1 - x if (k >> 2) & 1 else x
    py = 1 - y if (k >> 1) & 1 else y
    pc = 1 - c if k & 1 else c
    return (px, py, pc), 4 * px + 2 * py + pc


def _exchange(kind, x_ref, out_ref, send_sems, recv_sems, local_sem):
    gather = kind == "gather"
    _, me = _peer(0)

    def src(idx):
        return x_ref if gather else x_ref.at[idx]

    def copy(k, dst_idx):
        peer, pidx = _peer(k)
        return pltpu.make_async_remote_copy(
            src_ref=src(pidx), dst_ref=out_ref.at[dst_idx], send_sem=send_sems.at[k - 1],
            recv_sem=recv_sems.at[k - 1], device_id=peer, device_id_type=MESH)

    mine = pltpu.make_async_copy(src(me), out_ref.at[me], local_sem)

    def start():
        mine.start()
        for k in range(1, N_DEV):
            copy(k, me).start()

    def wait():
        for k in range(1, N_DEV):
            copy(k, _peer(k)[1]).wait_recv()
        for k in range(1, N_DEV):
            copy(k, me).wait_send()
        mine.wait()

    return start, wait


_EXCHANGE_SEMS = [pltpu.SemaphoreType.DMA((N_DEV - 1,)), pltpu.SemaphoreType.DMA((N_DEV - 1,)),
                  pltpu.SemaphoreType.DMA]


def _exchange_shape(kind, x):
    return jax.ShapeDtypeStruct(((N_DEV,) + x.shape) if kind == "gather" else x.shape, x.dtype)


def _exchange_alone(kind, x, name):
    def body(x_ref, out_ref, send_sems, recv_sems, local_sem):
        start, wait = _exchange(kind, x_ref, out_ref, send_sems, recv_sems, local_sem)
        start()
        wait()

    return pl.pallas_call(
        body, name=name, out_shape=_exchange_shape(kind, x),
        in_specs=[HBM_SPEC], out_specs=HBM_SPEC, scratch_shapes=list(_EXCHANGE_SEMS),
    )(x)


def _all_gather(x, name):
    return _exchange_alone("gather", x, name)


def _all_to_all(x, name):
    return _exchange_alone("scatter", x, name)


def _call(body, *, name, grid, in_specs, out_specs, out_shape, args, scratch_shapes=(), sem=None, comm=()):
    single = not isinstance(out_shape, (tuple, list))
    outs = (out_shape,) if single else tuple(out_shape)
    ospecs = (out_specs,) if single else tuple(out_specs)
    if not comm:
        res = pl.pallas_call(
            body, name=name, out_shape=outs, grid=grid, in_specs=list(in_specs), out_specs=ospecs,
            scratch_shapes=list(scratch_shapes), compiler_params=_params(*(sem or ("arbitrary",) * len(grid))),
        )(*args)
        return res
    n_in, n_out, n_scr, nc = len(in_specs), len(outs), len(scratch_shapes), len(comm)

    def wrapped(*refs):
        ins = refs[:n_in]
        cins = refs[n_in:n_in + nc]
        o0 = n_in + nc
        kouts = refs[o0:o0 + n_out]
        couts = refs[o0 + n_out:o0 + n_out + nc]
        s0 = o0 + n_out + nc
        scr = refs[s0:s0 + n_scr]
        sems = refs[s0 + n_scr:]
        ids = [pl.program_id(ax) for ax in range(len(grid))]
        first = functools.reduce(jnp.logical_and, [i == 0 for i in ids])
        last = functools.reduce(jnp.logical_and, [i == g - 1 for i, g in zip(ids, grid)])
        ex = [_exchange(comm[c][0], cins[c], couts[c], *sems[3 * c:3 * c + 3]) for c in range(nc)]

        @pl.when(first)
        def _():
            for start, _ in ex:
                start()

        body(*ins, *kouts, *scr)

        @pl.when(last)
        def _():
            for _, wait in ex:
                wait()

    return pl.pallas_call(
        wrapped, name=name,
        out_shape=outs + tuple(_exchange_shape(k, x) for k, x in comm),
        grid=grid, in_specs=list(in_specs) + [HBM_SPEC] * nc, out_specs=ospecs + (HBM_SPEC,) * nc,
        scratch_shapes=list(scratch_shapes) + list(_EXCHANGE_SEMS) * nc,
        compiler_params=_params(*(("arbitrary",) * len(grid))),
    )(*args, *[x for _, x in comm])


def _matmul(a, b, *, mode, out_dtype, name, tm=512, tn=1024, tk=1024, res=None, comm=()):
    if mode == "nn":
        (M, K), (_, N) = a.shape, b.shape
    elif mode == "nt":
        (M, K), (N, _) = a.shape, b.shape
    else:
        (K, M), (_, N) = a.shape, b.shape
    tm, tn, tk = _pick(M, tm), _pick(N, tn), _pick(K, tk)
    nk = K // tk
    dn = {"nn": _NN, "nt": _NT, "tn": _TN}[mode]

    def body(*refs):
        a_ref, b_ref = refs[0], refs[1]
        r_ref = refs[2] if res is not None else None
        o_ref = refs[3] if res is not None else refs[2]

        def finish(r):
            if res is not None:
                r = r + r_ref[...].astype(F32)
            o_ref[...] = r.astype(out_dtype)

        part = _dot(a_ref[...].astype(BF), b_ref[...].astype(BF), dn)
        if nk == 1:
            finish(part)
            return
        acc = refs[-1]
        k = pl.program_id(2)

        @pl.when(k == 0)
        def _():
            acc[...] = jnp.zeros_like(acc)

        acc[...] += part

        @pl.when(k == nk - 1)
        def _():
            finish(acc[...])

    if mode == "tn":
        a_spec = pl.BlockSpec((tk, tm), lambda j, i, k: (k, i))
    else:
        a_spec = pl.BlockSpec((tm, tk), lambda j, i, k: (i, k))
    if mode == "nt":
        b_spec = pl.BlockSpec((tn, tk), lambda j, i, k: (j, k))
    else:
        b_spec = pl.BlockSpec((tk, tn), lambda j, i, k: (k, j))
    o_spec = pl.BlockSpec((tm, tn), lambda j, i, k: (i, j))
    in_specs = [a_spec, b_spec] + ([o_spec] if res is not None else [])
    args = (a, b) + ((res,) if res is not None else ())
    out = _call(body, name=name, grid=(N // tn, M // tm, nk), in_specs=in_specs, out_specs=o_spec,
                out_shape=jax.ShapeDtypeStruct((M, N), out_dtype), args=args,
                scratch_shapes=[pltpu.VMEM((tm, tn), F32)] if nk > 1 else [],
                sem=("parallel", "parallel", "arbitrary"), comm=comm)
    return out[0], list(out[1:])


def _rms_fwd(x, g, name):
    S, D = x.shape
    tm = _pick(S, 512)

    def body(x_ref, g_ref, h_ref):
        xf = x_ref[...]
        r = lax.rsqrt(jnp.mean(xf * xf, axis=-1, keepdims=True) + RMS_EPS)
        h_ref[...] = (xf * r * g_ref[...]).astype(BF)

    return pl.pallas_call(
        body, name=name,
        out_shape=jax.ShapeDtypeStruct((S, D), BF),
        grid=(S // tm,),
        in_specs=[pl.BlockSpec((tm, D), lambda i: (i, 0)), pl.BlockSpec((1, D), lambda i: (0, 0))],
        out_specs=pl.BlockSpec((tm, D), lambda i: (i, 0)),
        compiler_params=_params("parallel"),
    )(x, g.reshape(1, D))


def _rms_bwd(x, g, dh, dres, name):
    S, D = x.shape
    tm = _pick(S, 512)

    def body(x_ref, g_ref, dh_ref, dres_ref, dx_ref, dg_ref):
        @pl.when(pl.program_id(0) == 0)
        def _():
            dg_ref[...] = jnp.zeros_like(dg_ref)

        xf = x_ref[...]
        r = lax.rsqrt(jnp.mean(xf * xf, axis=-1, keepdims=True) + RMS_EPS)
        xh = xf * r
        dy = dh_ref[...].astype(F32)
        dg_ref[...] += jnp.sum(dy * xh, axis=0, keepdims=True)
        dxh = dy * g_ref[...]
        dx = r * (dxh - xh * jnp.mean(dxh * xh, axis=-1, keepdims=True))
        dx_ref[...] = dres_ref[...] + dx

    row = pl.BlockSpec((tm, D), lambda i: (i, 0))
    vec = pl.BlockSpec((1, D), lambda i: (0, 0))
    return pl.pallas_call(
        body, name=name,
        out_shape=(jax.ShapeDtypeStruct((S, D), F32), jax.ShapeDtypeStruct((1, D), F32)),
        grid=(S // tm,),
        in_specs=[row, vec, row, row], out_specs=(row, vec),
        compiler_params=_params("arbitrary"),
    )(x, g.reshape(1, D), dh, dres)


def _matmul_rms_bwd(dy, w, x, g, dres, *, name, tm=512, comm=()):
    S, K = dy.shape
    D = w.shape[0]
    tm = _pick(S, tm)

    def body(dy_ref, w_ref, x_ref, g_ref, dres_ref, dx_ref, dg_ref):
        @pl.when(pl.program_id(0) == 0)
        def _():
            dg_ref[...] = jnp.zeros_like(dg_ref)

        dh = _dot(dy_ref[...].astype(BF), w_ref[...], _NT)
        xf = x_ref[...]
        r = lax.rsqrt(jnp.mean(xf * xf, axis=-1, keepdims=True) + RMS_EPS)
        xh = xf * r
        dg_ref[...] += jnp.sum(dh * xh, axis=0, keepdims=True)
        dxh = dh * g_ref[...]
        dx_ref[...] = dres_ref[...] + r * (dxh - xh * jnp.mean(dxh * xh, axis=-1, keepdims=True))

    row = pl.BlockSpec((tm, D), lambda i: (i, 0))
    vec = pl.BlockSpec((1, D), lambda i: (0, 0))
    res = _call(body, name=name, grid=(S // tm,),
                in_specs=[pl.BlockSpec((tm, K), lambda i: (i, 0)), pl.BlockSpec((D, K), lambda i: (0, 0)), row, vec, row],
                out_specs=(row, vec),
                out_shape=(jax.ShapeDtypeStruct((S, D), F32), jax.ShapeDtypeStruct((1, D), F32)),
                args=(dy, w, x, g.reshape(1, D), dres), comm=comm)
    return res[0], res[1], list(res[2:])


def _loss_head(x, g, target):
    S, D = x.shape
    tm = _pick(S, 512)

    def body(x_ref, g_ref, t_ref, loss_ref, dx_ref, dg_ref):
        @pl.when(pl.program_id(0) == 0)
        def _():
            dg_ref[...] = jnp.zeros_like(dg_ref)
            loss_ref[...] = jnp.zeros_like(loss_ref)

        xf = x_ref[...]
        gg = g_ref[...]
        r = lax.rsqrt(jnp.mean(xf * xf, axis=-1, keepdims=True) + RMS_EPS)
        xh = xf * r
        err = xh * gg - t_ref[...]
        per_tok = jnp.mean(err * err, axis=-1, keepdims=True)
        loss_ref[...] += 0.5 * jnp.sum(per_tok, axis=0, keepdims=True)
        dy = err * (1.0 / D)
        dg_ref[...] += jnp.sum(dy * xh, axis=0, keepdims=True)
        dxh = dy * gg
        dx_ref[...] = r * (dxh - xh * jnp.mean(dxh * xh, axis=-1, keepdims=True))

    row = pl.BlockSpec((tm, D), lambda i: (i, 0))
    vec = pl.BlockSpec((1, D), lambda i: (0, 0))
    one = pl.BlockSpec((1, 1), lambda i: (0, 0))
    return pl.pallas_call(
        body, name="loss_head",
        out_shape=(jax.ShapeDtypeStruct((1, 1), F32), jax.ShapeDtypeStruct((S, D), F32),
                   jax.ShapeDtypeStruct((1, D), F32)),
        grid=(S // tm,),
        in_specs=[row, vec, row], out_specs=(one, row, vec),
        compiler_params=_params("arbitrary"),
    )(x, g.reshape(1, D), target)


def _slopes(g):
    return [2.0 ** (-8.0 * (HEADS_PER_GROUP * g + j + 1) / DSW_HEADS) for j in range(HEADS_PER_GROUP)]


def _band_masks(W):
    row = lax.broadcasted_iota(jnp.int32, (W, W), 0)
    col = lax.broadcasted_iota(jnp.int32, (W, W), 1)
    d_cur = row - col
    d_prev = d_cur + W
    return d_cur, d_prev, d_cur >= 0, d_cur <= 0


def _band_specs(W, nb, per):
    def cur(c):
        return pl.BlockSpec((None, per * W, OUT_A), lambda r, n: (r, n, c))

    def prev(c):
        return pl.BlockSpec((None, W, OUT_A), lambda r, n: (r, jnp.maximum(per * n - 1, 0), c))

    def nxt(c):
        return pl.BlockSpec((None, W, OUT_A), lambda r, n: (r, jnp.minimum(per * (n + 1), nb - 1), c))

    return cur, prev, nxt


def _blocks_per_step(nb):
    return 2 if nb % 2 == 0 else 1


def _head_stack(W):
    H, hd = HEADS_PER_GROUP, HEAD_DIM
    lane_head = lax.broadcasted_iota(jnp.int32, (W, OUT_A), 1) // hd

    def stack(x):
        return jnp.concatenate([jnp.where(lane_head == h, x, jnp.zeros_like(x)) for h in range(H)], axis=0)

    def unstack(y):
        out = jnp.where(lane_head == 0, y[0:W], 0.0)
        for h in range(1, H):
            out = jnp.where(lane_head == h, y[h * W:(h + 1) * W], out)
        return out

    def column(ref, rows, off=0):
        return jnp.concatenate([ref[rows, h * hd + off:h * hd + off + 1] for h in range(H)], axis=0)

    def tile(x):
        return jnp.concatenate([x] * H, axis=0)

    return stack, unstack, column, tile


def _stacked_bias(W, slopes, dil):
    d_cur, d_prev, m_cur, m_prev = _band_masks(W)
    b_cur = jnp.concatenate([(s * dil) * d_cur.astype(F32) for s in slopes], axis=0)
    b_prev = jnp.concatenate([(s * dil) * d_prev.astype(F32) for s in slopes], axis=0)
    H = len(slopes)
    return b_cur, b_prev, jnp.concatenate([m_cur] * H, axis=0), jnp.concatenate([m_prev] * H, axis=0)


def _attn_a_fwd(qkv, cols, g, comm=()):
    win, dil = DSW_GROUPS[g]
    W = win // dil
    d, L, _ = qkv[0].shape
    nb = L // W
    per = _blocks_per_step(nb)
    slopes = _slopes(g)

    def body(q_ref, kp_ref, kc_ref, vp_ref, vc_ref, o_ref, l_ref):
        n = pl.program_id(1)
        stack, unstack, _, _ = _head_stack(W)
        b_cur, b_prev, m_cur, m_prev = _stacked_bias(W, slopes, dil)
        m_first = jnp.logical_and(m_prev, n > 0)
        for b in range(per):
            rows = slice(b * W, (b + 1) * W)
            before = slice((b - 1) * W, b * W)
            qs = stack(q_ref[rows, :])
            kc, vc = kc_ref[rows, :], vc_ref[rows, :]
            kp, vp = (kp_ref[...], vp_ref[...]) if b == 0 else (kc_ref[before, :], vc_ref[before, :])
            s_c = jnp.where(m_cur, _dot(qs, kc, _NT) * ATT_SCALE - b_cur, NEG)
            s_p = jnp.where(m_first if b == 0 else m_prev, _dot(qs, kp, _NT) * ATT_SCALE - b_prev, NEG)
            m = jnp.maximum(jnp.max(s_c, axis=1, keepdims=True), jnp.max(s_p, axis=1, keepdims=True))
            p_c = jnp.exp(s_c - m)
            p_p = jnp.exp(s_p - m)
            den = jnp.sum(p_c, axis=1, keepdims=True) + jnp.sum(p_p, axis=1, keepdims=True)
            pv = _dot(p_c.astype(BF), vc) + _dot(p_p.astype(BF), vp)
            o_ref[rows, :] = unstack(pv / den)
            l_ref[rows, :] = unstack(jnp.broadcast_to(m + jnp.log(den), pv.shape))

    cur, prev, _ = _band_specs(W, nb, per)
    out = jax.ShapeDtypeStruct((d, L, OUT_A), F32)
    res = _call(body, name=f"attn_a_fwd_g{g}", grid=(d, nb // per),
                in_specs=[cur(cols[0]), prev(cols[1]), cur(cols[1]), prev(cols[2]), cur(cols[2])],
                out_specs=(cur(0), cur(0)), out_shape=(out, out),
                args=(qkv[0], qkv[1], qkv[1], qkv[2], qkv[2]), sem=("parallel", "parallel"), comm=comm)
    return res[0], res[1], list(res[2:])


def _attn_a_bwd(qkv, cols, do, stats, g):
    win, dil = DSW_GROUPS[g]
    W = win // dil
    d, L, _ = qkv[0].shape
    nb = L // W
    per = _blocks_per_step(nb)
    nsteps = nb // per
    slopes = _slopes(g)

    def body(q_ref, qn_ref, kp_ref, kc_ref, vp_ref, vc_ref, do_ref, don_ref, st_ref, stn_ref,
             dq_ref, dk_ref, dv_ref):
        n = pl.program_id(1)
        stack, unstack, column, _ = _head_stack(W)
        b_cur, b_prev, m_cur, m_prev = _stacked_bias(W, slopes, dil)
        m_first = jnp.logical_and(m_prev, n > 0)
        m_last = jnp.logical_and(m_prev, n < nsteps - 1)
        everything = slice(None)
        for b in range(per):
            rows = slice(b * W, (b + 1) * W)
            before = slice((b - 1) * W, b * W)
            after = slice((b + 1) * W, (b + 2) * W)
            first, last = b == 0, b == per - 1
            qs = stack(q_ref[rows, :])
            qn = stack(qn_ref[...] if last else q_ref[after, :])
            dos = stack(do_ref[rows, :])
            don = stack(don_ref[...] if last else do_ref[after, :])
            kc, vc = kc_ref[rows, :], vc_ref[rows, :]
            kp, vp = (kp_ref[...], vp_ref[...]) if first else (kc_ref[before, :], vc_ref[before, :])
            lse_c, dsum_c = column(st_ref, rows), column(st_ref, rows, STAT_OFF)
            lse_n = column(stn_ref, everything) if last else column(st_ref, after)
            dsum_n = column(stn_ref, everything, STAT_OFF) if last else column(st_ref, after, STAT_OFF)
            m_p = m_first if first else m_prev
            m_n = m_last if last else m_prev
            p_cc = jnp.exp(jnp.where(m_cur, _dot(qs, kc, _NT) * ATT_SCALE - b_cur, NEG) - lse_c)
            p_cp = jnp.exp(jnp.where(m_p, _dot(qs, kp, _NT) * ATT_SCALE - b_prev, NEG) - lse_c)
            p_nc = jnp.exp(jnp.where(m_n, _dot(qn, kc, _NT) * ATT_SCALE - b_prev, NEG) - lse_n)
            ds_cc = (p_cc * (_dot(dos, vc, _NT) - dsum_c) * ATT_SCALE).astype(BF)
            ds_cp = (p_cp * (_dot(dos, vp, _NT) - dsum_c) * ATT_SCALE).astype(BF)
            ds_nc = (p_nc * (_dot(don, vc, _NT) - dsum_n) * ATT_SCALE).astype(BF)
            dq_ref[rows, :] = unstack(_dot(ds_cc, kc) + _dot(ds_cp, kp)).astype(BF)
            dk_ref[rows, :] = (_dot(ds_cc, qs, _TN) + _dot(ds_nc, qn, _TN)).astype(BF)
            dv_ref[rows, :] = (_dot(p_cc.astype(BF), dos, _TN) + _dot(p_nc.astype(BF), don, _TN)).astype(BF)

    cur, prev, nxt = _band_specs(W, nb, per)
    out = jax.ShapeDtypeStruct((d, L, OUT_A), BF)
    cq, ck, cv = cols
    return pl.pallas_call(
        body, name=f"attn_a_bwd_g{g}",
        out_shape=(out, out, out),
        grid=(d, nsteps),
        in_specs=[cur(cq), nxt(cq), prev(ck), cur(ck), prev(cv), cur(cv), cur(0), nxt(0), cur(0), nxt(0)],
        out_specs=(cur(0), cur(0), cur(0)),
        compiler_params=_params("parallel", "parallel"),
    )(qkv[0], qkv[0], qkv[1], qkv[1], qkv[2], qkv[2], do, do, stats, stats)


SB_PAIR = 2
SB_QROWS = SB_BLOCK


def _softplus_parts(z):
    e = jnp.exp(-jnp.abs(z))
    log1p_e = jnp.where(e < 1e-4, e, jnp.log(1.0 + e))
    return e, jnp.maximum(z, 0.0) + log1p_e


def _split_dot(x, t):
    hi = x.astype(BF)
    lo = (x - hi.astype(F32)).astype(BF)
    return _dot(hi, t) + _dot(lo, t)


def _sb_block(qh, kk, causal, r_run, tri_incl):
    z = _dot(qh, kk, _NT)
    e, sp = _softplus_parts(z)
    ls = jnp.where(causal, -sp, 0.0)
    cin = _split_dot(ls, tri_incl)
    a = jnp.where(causal, jnp.exp(z + cin + r_run), 0.0)
    return z, e, cin, a


def _sb_specs(S):
    Q, hd = SB_QROWS, HEAD_DIM
    lanes = SB_PAIR * hd
    qc = (3 * W_A) // lanes
    kc = (3 * W_A + W_B) // lanes
    vc = (3 * W_A + 2 * W_B) // lanes
    q_spec = pl.BlockSpec((Q, lanes), lambda p, i: (i, qc + p))
    k_spec = pl.BlockSpec((S, lanes), lambda p, i: (0, kc + p))
    v_spec = pl.BlockSpec((S, lanes), lambda p, i: (0, vc + p))
    o_spec = pl.BlockSpec((Q, lanes), lambda p, i: (i, p))
    full = pl.BlockSpec((S, lanes), lambda p, i: (0, p))
    return q_spec, k_spec, v_spec, o_spec, full


def _sb_stack():
    Q, hd = SB_QROWS, HEAD_DIM
    lane_head = lax.broadcasted_iota(jnp.int32, (Q, SB_PAIR * hd), 1) // hd

    def stack(x):
        return jnp.concatenate([jnp.where(lane_head == h, x, jnp.zeros_like(x)) for h in range(SB_PAIR)], axis=0)

    def unstack(y):
        out = jnp.where(lane_head == 0, y[0:Q], 0.0)
        for h in range(1, SB_PAIR):
            out = jnp.where(lane_head == h, y[h * Q:(h + 1) * Q], out)
        return out

    return stack, unstack


def _sb_iotas(i):
    B, Q = SB_BLOCK, SB_QROWS
    row = lax.broadcasted_iota(jnp.int32, (Q, B), 0) + i * Q
    col = lax.broadcasted_iota(jnp.int32, (Q, B), 1)
    ahead = jnp.concatenate([col - row] * SB_PAIR, axis=0)
    tr = lax.broadcasted_iota(jnp.int32, (B, B), 0)
    tc = lax.broadcasted_iota(jnp.int32, (B, B), 1)
    return ahead, tr, tc


def _sb_fwd(proj, comm=()):
    S = proj.shape[0]
    B, Q, hd = SB_BLOCK, SB_QROWS, HEAD_DIM
    nq = S // Q
    R = SB_PAIR * Q
    q_spec, k_spec, v_spec, o_spec, _ = _sb_specs(S)

    def body(q_ref, k_ref, v_ref, o_ref):
        i = pl.program_id(1)
        ahead, tr, tc = _sb_iotas(i)
        tri_incl = (tr >= tc).astype(BF)
        stack, unstack = _sb_stack()
        qs = stack(q_ref[...] * ATT_SCALE)

        def cond(c):
            return jnp.logical_and(c[0] >= 0, c[-1] > SB_EXIT)

        def step(c):
            kb, r_run, acc, _ = c
            off = pl.multiple_of(kb * B, B)
            causal = ahead < -kb * B
            _, _, cin, a = _sb_block(qs, k_ref[pl.ds(off, B), :], causal, r_run, tri_incl)
            acc = acc + _dot(a.astype(BF), v_ref[pl.ds(off, B), :])
            r_run = r_run + cin[:, 0:1]
            return kb - 1, r_run, acc, jnp.max(r_run)

        init = (i, jnp.zeros((R, 1), F32), jnp.zeros((R, SB_PAIR * hd), F32), jnp.float32(0.0))
        fin = lax.while_loop(cond, step, init)
        o_ref[...] = unstack(fin[2])

    res = _call(body, name="sb_fwd", grid=(SB_HEADS // SB_PAIR, nq), in_specs=[q_spec, k_spec, v_spec],
                out_specs=o_spec, out_shape=jax.ShapeDtypeStruct((S, W_B), F32), args=(proj, proj, proj),
                sem=("parallel", "parallel"), comm=comm)
    return res[0], list(res[1:])


def _sb_bwd(proj, do, o, comm=()):
    S = proj.shape[0]
    B, Q, hd = SB_BLOCK, SB_QROWS, HEAD_DIM
    nq = S // Q
    R = SB_PAIR * Q
    q_spec, k_spec, v_spec, o_spec, full = _sb_specs(S)

    def body(q_ref, k_ref, v_ref, do_ref, o_ref, dq_ref, dk_ref, dv_ref):
        i = pl.program_id(1)

        @pl.when(i == 0)
        def _():
            dk_ref[...] = jnp.zeros_like(dk_ref)
            dv_ref[...] = jnp.zeros_like(dv_ref)

        ahead, tr, tc = _sb_iotas(i)
        tri_incl = (tr >= tc).astype(BF)
        tri_strict = (tr > tc).astype(BF)
        stack, unstack = _sb_stack()
        qs = stack(q_ref[...] * ATT_SCALE)
        dobs = stack(do_ref[...])
        o_all = o_ref[...]
        dsum = jnp.sum(dobs.astype(F32) * jnp.concatenate([o_all] * SB_PAIR, axis=0), axis=1, keepdims=True)

        def cond(c):
            return jnp.logical_and(c[0] >= 0, c[-1] > SB_EXIT)

        def step(c):
            kb, r_run, g_run, dq, _ = c
            off = pl.multiple_of(kb * B, B)
            causal = ahead < -kb * B
            kk = k_ref[pl.ds(off, B), :]
            vv = v_ref[pl.ds(off, B), :]
            z, e, cin, a = _sb_block(qs, kk, causal, r_run, tri_incl)
            a16 = a.astype(BF)
            gmat = a16.astype(F32) * _dot(dobs, vv, _NT)
            later = _split_dot(gmat, tri_strict)
            pfx = dsum - g_run - later
            sig = jnp.where(z >= 0, 1.0, e) / (1.0 + e)
            dz = jnp.where(causal, gmat - sig * pfx, 0.0).astype(BF)
            dq = dq + _dot(dz, kk)
            dk_ref[pl.ds(off, B), :] += _dot(dz, qs, _TN)
            dv_ref[pl.ds(off, B), :] += _dot(a16, dobs, _TN)
            g_run = g_run + jnp.sum(gmat, axis=1, keepdims=True)
            r_run = r_run + cin[:, 0:1]
            return kb - 1, r_run, g_run, dq, jnp.max(r_run)

        init = (i, jnp.zeros((R, 1), F32), jnp.zeros((R, 1), F32), jnp.zeros((R, SB_PAIR * hd), F32),
                jnp.float32(0.0))
        fin = lax.while_loop(cond, step, init)
        dq_ref[...] = unstack(fin[3]) * ATT_SCALE

    out = jax.ShapeDtypeStruct((S, W_B), F32)
    res = _call(body, name="sb_bwd", grid=(SB_HEADS // SB_PAIR, nq), in_specs=[q_spec, k_spec, v_spec, o_spec, o_spec],
                out_specs=(o_spec, full, full), out_shape=(out, out, out), args=(proj, proj, proj, do, o), comm=comm)
    return res[0], res[1], res[2], list(res[3:])


def _merge_fwd(o_g, l_g, o_b, proj, b_gate, w_br):
    S = o_b.shape[0]
    D = D_MODEL
    tm = _pick(S, 256)
    gcol = GATE_OFF // D

    def body(o0, o1, o2, l0, l1, l2, ob_ref, ga_ref, gb_ref, bg_ref, w_ref, mg_ref, oa_ref, lse_ref):
        la, lb, lc = l0[...], l1[...], l2[...]
        mx = jnp.maximum(jnp.maximum(la, lb), lc)
        ea, eb, ec = jnp.exp(la - mx), jnp.exp(lb - mx), jnp.exp(lc - mx)
        den = ea + eb + ec
        oa = (ea * o0[...] + eb * o1[...] + ec * o2[...]) / den
        oa_ref[...] = oa
        lse_ref[...] = mx + jnp.log(den)
        ya = _dot(oa.astype(BF), w_ref[0:OUT_A, :])
        yb = _dot(ob_ref[...].astype(BF), w_ref[OUT_A:OUT_A + W_B, :])
        bg = bg_ref[...]
        g_a = jax.nn.sigmoid(ga_ref[...].astype(F32) + bg[:, 0:D])
        g_b = jax.nn.sigmoid(gb_ref[...].astype(F32) + bg[:, D:2 * D])
        mg_ref[...] = (g_a * ya + g_b * yb).astype(BF)

    nar = pl.BlockSpec((tm, OUT_A), lambda i: (i, 0))
    wide = pl.BlockSpec((tm, D), lambda i: (i, 0))
    return pl.pallas_call(
        body, name="merge_fwd",
        out_shape=(jax.ShapeDtypeStruct((S, D), BF), jax.ShapeDtypeStruct((S, OUT_A), F32),
                   jax.ShapeDtypeStruct((S, OUT_A), F32)),
        grid=(S // tm,),
        in_specs=[nar] * 7 + [pl.BlockSpec((tm, D), lambda i: (i, gcol)),
                              pl.BlockSpec((tm, D), lambda i: (i, gcol + 1)),
                              pl.BlockSpec((1, 2 * D), lambda i: (0, 0)),
                              pl.BlockSpec((OUT_A + W_B, D), lambda i: (0, 0))],
        out_specs=(wide, nar, nar),
        compiler_params=_params("parallel"),
    )(*o_g, *l_g, o_b, proj, proj, b_gate.reshape(1, 2 * D), w_br)


def _merge_wo(o_g, l_g, o_b, proj, b_gate, w_br, w_o, x, name, comm=()):
    S = o_b.shape[0]
    D = D_MODEL
    tm = _pick(S, 256)
    gcol = GATE_OFF // D

    def body(o0, o1, o2, l0, l1, l2, ob_ref, ga_ref, gb_ref, bg_ref, w_ref, wo_ref, x_ref,
             x1_ref, mg_ref, oa_ref, lse_ref):
        la, lb, lc = l0[...], l1[...], l2[...]
        mx = jnp.maximum(jnp.maximum(la, lb), lc)
        ea, eb, ec = jnp.exp(la - mx), jnp.exp(lb - mx), jnp.exp(lc - mx)
        den = ea + eb + ec
        oa = (ea * o0[...] + eb * o1[...] + ec * o2[...]) / den
        oa_ref[...] = oa
        lse_ref[...] = mx + jnp.log(den)
        oa16 = oa.astype(BF)
        ob16 = ob_ref[...].astype(BF)
        acc = x_ref[...]
        for c0 in range(0, D, FFN_CHUNK):
            cs = slice(c0, c0 + FFN_CHUNK)
            ya = _dot(oa16, w_ref[0:OUT_A, cs])
            yb = _dot(ob16, w_ref[OUT_A:OUT_A + W_B, cs])
            g_a = jax.nn.sigmoid(ga_ref[:, cs].astype(F32) + bg_ref[:, cs])
            g_b = jax.nn.sigmoid(gb_ref[:, cs].astype(F32) + bg_ref[:, D + c0:D + c0 + FFN_CHUNK])
            mg = (g_a * ya + g_b * yb).astype(BF)
            mg_ref[:, cs] = mg
            acc = acc + _dot(mg, wo_ref[cs, :])
        x1_ref[...] = acc

    nar = pl.BlockSpec((tm, OUT_A), lambda i: (i, 0))
    wide = pl.BlockSpec((tm, D), lambda i: (i, 0))
    res = _call(body, name=name, grid=(S // tm,),
                in_specs=[nar] * 7 + [pl.BlockSpec((tm, D), lambda i: (i, gcol)),
                                      pl.BlockSpec((tm, D), lambda i: (i, gcol + 1)),
                                      pl.BlockSpec((1, 2 * D), lambda i: (0, 0)),
                                      pl.BlockSpec((OUT_A + W_B, D), lambda i: (0, 0)),
                                      pl.BlockSpec((D, D), lambda i: (0, 0)), wide],
                out_specs=(wide, wide, nar, nar),
                out_shape=(jax.ShapeDtypeStruct((S, D), F32), jax.ShapeDtypeStruct((S, D), BF),
                           jax.ShapeDtypeStruct((S, OUT_A), F32), jax.ShapeDtypeStruct((S, OUT_A), F32)),
                args=(*o_g, *l_g, o_b, proj, proj, b_gate.reshape(1, 2 * D), w_br, w_o, x), comm=comm)
    return res[0], res[1], res[2], res[3], list(res[4:])


STAT_OFF = HEAD_DIM // 2


def _merge_bwd(dm, oa, ob, lse, proj, b_gate, w_br):
    S = ob.shape[0]
    D = D_MODEL
    tm = _pick(S, 256)
    gcol = GATE_OFF // D

    def body(dm_ref, oa_ref, ob_ref, l_ref, ga_ref, gb_ref, bg_ref, w_ref,
             dya_ref, dyb_ref, doa_ref, dob_ref, st_ref, dg_ref, dbg_ref):
        @pl.when(pl.program_id(0) == 0)
        def _():
            dbg_ref[...] = jnp.zeros_like(dbg_ref)

        wa = w_ref[0:OUT_A, :]
        wb = w_ref[OUT_A:OUT_A + W_B, :]
        oa = oa_ref[...]
        ya = _dot(oa.astype(BF), wa)
        yb = _dot(ob_ref[...].astype(BF), wb)
        bg = bg_ref[...]
        g_a = jax.nn.sigmoid(ga_ref[...].astype(F32) + bg[:, 0:D])
        g_b = jax.nn.sigmoid(gb_ref[...].astype(F32) + bg[:, D:2 * D])
        dm = dm_ref[...].astype(F32)
        dga = dm * ya * g_a * (1.0 - g_a)
        dgb = dm * yb * g_b * (1.0 - g_b)
        dg_ref[:, 0:D] = dga.astype(BF)
        dg_ref[:, D:2 * D] = dgb.astype(BF)
        dbg_ref[:, 0:D] += jnp.sum(dga, axis=0, keepdims=True)
        dbg_ref[:, D:2 * D] += jnp.sum(dgb, axis=0, keepdims=True)
        dya = (dm * g_a).astype(BF)
        dyb = (dm * g_b).astype(BF)
        dya_ref[...] = dya
        dyb_ref[...] = dyb
        doa = _dot(dya, wa, _NT).astype(BF)
        doa_ref[...] = doa
        dob_ref[...] = _dot(dyb, wb, _NT).astype(BF)
        r = lax.broadcasted_iota(jnp.int32, (OUT_A, OUT_A), 0) // HEAD_DIM
        c = lax.broadcasted_iota(jnp.int32, (OUT_A, OUT_A), 1) // HEAD_DIM
        dsum = _split_dot(doa.astype(F32) * oa, (r == c).astype(BF))
        lane = lax.broadcasted_iota(jnp.int32, dsum.shape, 1) % HEAD_DIM
        st_ref[...] = jnp.where(lane < STAT_OFF, l_ref[...], dsum)

    nar = pl.BlockSpec((tm, OUT_A), lambda i: (i, 0))
    wide = pl.BlockSpec((tm, D), lambda i: (i, 0))
    wide2 = pl.BlockSpec((tm, 2 * D), lambda i: (i, 0))
    vec2 = pl.BlockSpec((1, 2 * D), lambda i: (0, 0))
    return pl.pallas_call(
        body, name="merge_bwd",
        out_shape=(jax.ShapeDtypeStruct((S, D), BF), jax.ShapeDtypeStruct((S, D), BF),
                   jax.ShapeDtypeStruct((S, OUT_A), BF), jax.ShapeDtypeStruct((S, W_B), BF),
                   jax.ShapeDtypeStruct((S, OUT_A), F32), jax.ShapeDtypeStruct((S, 2 * D), BF),
                   jax.ShapeDtypeStruct((1, 2 * D), F32)),
        grid=(S // tm,),
        in_specs=[wide, nar, nar, nar, pl.BlockSpec((tm, D), lambda i: (i, gcol)),
                  pl.BlockSpec((tm, D), lambda i: (i, gcol + 1)), vec2,
                  pl.BlockSpec((OUT_A + W_B, D), lambda i: (0, 0))],
        out_specs=(wide, wide, nar, nar, nar, wide2, vec2),
        compiler_params=_params("arbitrary"),
    )(dm, oa, ob, lse, proj, proj, b_gate.reshape(1, 2 * D), w_br)


_SQRT_HALF = 0.7071067811865476
_INV_SQRT_2PI = 0.3989422804014327


def _gelu_parts(a):
    cdf = 0.5 * (1.0 + lax.erf(a * _SQRT_HALF))
    pdf = _INV_SQRT_2PI * jnp.exp(-0.5 * a * a)
    return cdf, pdf


def _shift_down(a, halo, k):
    rows = lax.broadcasted_iota(jnp.int32, a.shape, 0)
    out = pltpu.roll(a, k, 0)
    for r in range(k):
        out = jnp.where(rows == r, halo[8 - k + r:8 - k + r + 1, :], out)
    return out


def _shift_up(a, halo, k):
    n = a.shape[0]
    rows = lax.broadcasted_iota(jnp.int32, a.shape, 0)
    out = pltpu.roll(a, n - k, 0)
    for r in range(k):
        out = jnp.where(rows == n - k + r, halo[r:r + 1, :], out)
    return out


def _conv_in(a_ref, h_ref, first):
    a = a_ref[...].astype(F32)
    halo = jnp.where(first, 0.0, h_ref[...].astype(F32))
    return a, _shift_down(a, halo, 1), _shift_down(a, halo, 2)


def _ffn_specs(S, tm):
    F = D_FF
    t8 = tm // 8
    a_spec = pl.BlockSpec((tm, F), lambda i: (i, 0))
    v_spec = pl.BlockSpec((tm, F), lambda i: (i, 1))
    halo_prev = pl.BlockSpec((8, F), lambda i: (jnp.maximum(i * t8 - 1, 0), 0))
    return a_spec, v_spec, halo_prev


def _ffn_act_fwd(up, conv_w, conv_b):
    S = up.shape[0]
    F = D_FF
    tm = _pick(S, 256)
    a_spec, v_spec, halo_prev = _ffn_specs(S, tm)

    def body(a_ref, h_ref, v_ref, w_ref, b_ref, act_ref):
        a0, a1, a2 = _conv_in(a_ref, h_ref, pl.program_id(0) == 0)
        w = w_ref[...]
        ac = b_ref[...] + w[0:1, :] * a2 + w[1:2, :] * a1 + w[2:3, :] * a0
        cdf, _ = _gelu_parts(ac)
        act_ref[...] = (ac * cdf * v_ref[...].astype(F32)).astype(BF)

    return pl.pallas_call(
        body, name="ffn_act_fwd",
        out_shape=jax.ShapeDtypeStruct((S, F), BF),
        grid=(S // tm,),
        in_specs=[a_spec, halo_prev, v_spec, pl.BlockSpec((3, F), lambda i: (0, 0)),
                  pl.BlockSpec((1, F), lambda i: (0, 0))],
        out_specs=a_spec,
        compiler_params=_params("parallel"),
    )(up, up, up, conv_w, conv_b.reshape(1, F))


def _ffn_down(up, conv_w, conv_b, w_down, res, *, name, comm=()):
    S = up.shape[0]
    F = D_FF
    D = w_down.shape[1]
    tm = _pick(S, 256)
    a_spec, v_spec, halo_prev = _ffn_specs(S, tm)

    def body(a_ref, h_ref, v_ref, w_ref, b_ref, wd_ref, r_ref, o_ref, act_ref, ac_ref):
        first = pl.program_id(0) == 0
        acc = r_ref[...]
        for c0 in range(0, F, FFN_CHUNK):
            cs = slice(c0, c0 + FFN_CHUNK)
            a = a_ref[:, cs].astype(F32)
            halo = jnp.where(first, 0.0, h_ref[:, cs].astype(F32))
            w = w_ref[:, cs]
            ac = b_ref[:, cs] + w[0:1, :] * _shift_down(a, halo, 2) + w[1:2, :] * _shift_down(a, halo, 1) + w[2:3, :] * a
            ac_ref[:, cs] = ac.astype(BF)
            cdf, _ = _gelu_parts(ac)
            act = (ac * cdf * v_ref[:, cs].astype(F32)).astype(BF)
            act_ref[:, cs] = act
            acc = acc + _dot(act, wd_ref[cs, :])
        o_ref[...] = acc

    row = pl.BlockSpec((tm, D), lambda i: (i, 0))
    res_ = _call(body, name=name, grid=(S // tm,),
                 in_specs=[a_spec, halo_prev, v_spec, pl.BlockSpec((3, F), lambda i: (0, 0)),
                           pl.BlockSpec((1, F), lambda i: (0, 0)), pl.BlockSpec((F, D), lambda i: (0, 0)), row],
                 out_specs=(row, a_spec, a_spec),
                 out_shape=(jax.ShapeDtypeStruct((S, D), F32), jax.ShapeDtypeStruct((S, F), BF),
                            jax.ShapeDtypeStruct((S, F), BF)),
                 args=(up, up, up, conv_w, conv_b.reshape(1, F), w_down, res), comm=comm)
    return res_[0], res_[1], res_[2], list(res_[3:])


def _shift_up_pair(a, nxt):
    n = a.shape[0]
    r8 = lax.broadcasted_iota(jnp.int32, (8,) + a.shape[1:], 0)
    out = []
    for k in (1, 2):
        rolled = pltpu.roll(a, n - k, 0)
        tail = jnp.where(r8 >= 8 - k, pltpu.roll(nxt, 8 - k, 0), rolled[n - 8:n])
        out.append(jnp.concatenate([rolled[0:n - 8], tail], axis=0))
    return out


def _ffn_bwd(dx, w_down, up, ac, conv_w, comm=()):
    S = up.shape[0]
    F = D_FF
    D = dx.shape[1]
    tm = _pick(S, 256)
    t8 = tm // 8
    nt = S // tm
    a_spec, v_spec, _ = _ffn_specs(S, tm)

    def nxt(width, col):
        return pl.BlockSpec((8, width), lambda i: (jnp.minimum((i + 1) * t8, S // 8 - 1), col))

    def body(dx_ref, dxn_ref, wd_ref, ac_ref, acn_ref, a_ref, v_ref, vn_ref, w_ref, dup_ref, dw_ref, db_ref):
        i = pl.program_id(0)

        @pl.when(i == 0)
        def _():
            dw_ref[...] = jnp.zeros_like(dw_ref)
            db_ref[...] = jnp.zeros_like(db_ref)

        dx16 = dx_ref[...].astype(BF)
        dxn16 = dxn_ref[...].astype(BF)
        last = i == nt - 1

        def dconv(dact, ac, v):
            cdf, pdf = _gelu_parts(ac)
            return cdf, dact * v * (cdf + ac * pdf)

        for c0 in range(0, F, FFN_CHUNK):
            cs = slice(c0, c0 + FFN_CHUNK)
            wd = wd_ref[cs, :]
            dact = _dot(dx16, wd, _NT)
            ac = ac_ref[:, cs].astype(F32)
            cdf, dac = dconv(dact, ac, v_ref[:, cs].astype(F32))
            dup_ref[:, F + c0:F + c0 + FFN_CHUNK] = (dact * ac * cdf).astype(BF)
            _, dac_n = dconv(_dot(dxn16, wd, _NT), acn_ref[:, cs].astype(F32), vn_ref[:, cs].astype(F32))
            d1, d2 = _shift_up_pair(dac, jnp.where(last, 0.0, dac_n))
            w = w_ref[:, cs]
            dup_ref[:, cs] = (w[2:3, :] * dac + w[1:2, :] * d1 + w[0:1, :] * d2).astype(BF)
            a = a_ref[:, cs].astype(F32)
            db_ref[:, cs] += jnp.sum(dac, axis=0, keepdims=True)
            dw_ref[0:1, cs] += jnp.sum(d2 * a, axis=0, keepdims=True)
            dw_ref[1:2, cs] += jnp.sum(d1 * a, axis=0, keepdims=True)
            dw_ref[2:3, cs] += jnp.sum(dac * a, axis=0, keepdims=True)

    w_spec = pl.BlockSpec((3, F), lambda i: (0, 0))
    b_spec = pl.BlockSpec((1, F), lambda i: (0, 0))
    res = _call(body, name="ffn_bwd", grid=(nt,),
                in_specs=[pl.BlockSpec((tm, D), lambda i: (i, 0)), nxt(D, 0), pl.BlockSpec((F, D), lambda i: (0, 0)),
                          a_spec, nxt(F, 0), a_spec, v_spec, nxt(F, 1), w_spec],
                out_specs=(pl.BlockSpec((tm, 2 * F), lambda i: (i, 0)), w_spec, b_spec),
                out_shape=(jax.ShapeDtypeStruct((S, 2 * F), BF), jax.ShapeDtypeStruct((3, F), F32),
                           jax.ShapeDtypeStruct((1, F), F32)),
                args=(dx, dx, w_down, ac, ac, up, up, up, conv_w), comm=comm)
    return res[0], res[1], res[2], list(res[3:])


def _adamw(parts, w, m, v, name, row0=0, prev=None):
    R, C = w.shape
    Rp = parts.shape[1]
    tr = Rp
    for cand in (512, 256, 128, 64, 32, 16):
        if Rp % cand == 0 and row0 % cand == 0 and cand * C * 4 <= (1 << 21):
            tr = cand
            break
    b0 = row0 // tr
    c1 = 1.0 / (1.0 - ADAM_B1 ** ADAM_STEP)
    c2 = 1.0 / (1.0 - ADAM_B2 ** ADAM_STEP)

    def body(p_ref, w_ref, m_ref, v_ref, *rest):
        g_ref, d_ref, nm_ref, nv_ref = rest[-4:]
        g = p_ref[0].astype(F32)
        for j in range(1, N_DEV):
            g = g + p_ref[j].astype(F32)
        nm = ADAM_B1 * m_ref[...] + (1.0 - ADAM_B1) * g
        nv = ADAM_B2 * v_ref[...] + (1.0 - ADAM_B2) * (g * g)
        g_ref[...] = g
        nm_ref[...] = nm
        nv_ref[...] = nv
        d_ref[...] = -ADAM_LR * ((nm * c1) / (jnp.sqrt(nv * c2) + ADAM_EPS) + ADAM_WD * w_ref[...])

    blk = pl.BlockSpec((tr, C), lambda i: (b0 + i, 0))
    out = jax.ShapeDtypeStruct((R, C), F32)
    carried = [] if prev is None else list(prev)
    return pl.pallas_call(
        body, name=name,
        out_shape=(out, out, out, out),
        grid=(Rp // tr,),
        in_specs=[pl.BlockSpec((N_DEV, tr, C), lambda i: (0, i, 0)), blk, blk, blk]
        + [pl.BlockSpec(memory_space=pl.ANY)] * len(carried),
        out_specs=(blk, blk, blk, blk),
        input_output_aliases={4 + k: k for k in range(len(carried))},
        compiler_params=_params("parallel"),
    )(parts, w, m, v, *carried)


def _dil(t, dil):
    S, C = t.shape
    if dil == 1:
        return t.reshape(1, S, C)
    return t.reshape(S // dil, dil, C).transpose(1, 0, 2)


def _undil(t):
    d, L, C = t.shape
    if d == 1:
        return t.reshape(L, C)
    return t.transpose(1, 0, 2).reshape(L * d, C)


def _group_qkv(proj, g):
    dil = DSW_GROUPS[g][1]
    if dil == 1:
        p3 = _dil(proj, 1)
        return (p3, p3, p3), (g, W_A // OUT_A + g, 2 * W_A // OUT_A + g)
    c0 = g * OUT_A
    return tuple(_dil(proj[:, o + c0:o + c0 + OUT_A], dil) for o in (0, W_A, 2 * W_A)), (0, 0, 0)


_COL_SHARDED = ("w_in", "w_br", "w_up")
_ROW_SHARDED = ("w_o", "w_down")


class _Plan:
    def __init__(self):
        self.riders = {}
        self.landed = {}

    def ride(self, slot, key, kind, x):
        self.riders.setdefault(slot, []).append((key, kind, x))

    def run(self, slot, fn, *args, **kw):
        items = self.riders.pop(slot, [])
        res = fn(*args, comm=[(kind, x) for _, kind, x in items], **kw)
        for (key, _, _), r in zip(items, res[-1]):
            self.landed[key] = r
        return res[0] if len(res) == 2 else res[:-1]

    def weight(self, n, l):
        g = self.landed[(n, l)]
        if n in _COL_SHARDED:
            return g.transpose(1, 0, 2).reshape(g.shape[1], -1)
        return g.reshape(-1, g.shape[2])

    def scatter(self, slot, n, l, full, part=0):
        K, N = full.shape
        if n in _COL_SHARDED:
            blocks = full.reshape(K, N_DEV, N // N_DEV).transpose(1, 0, 2)
        else:
            blocks = full.reshape(N_DEV, K // N_DEV, N)
        self.ride(slot, ("d" + n, l, part), "scatter", blocks)


def _layer_fwd(x, p, plan, l):
    h = _rms_fwd(x, p["norm1"][l], f"rms1_fwd_{l}")
    proj = plan.run(f"proj_{l}", _matmul, h, plan.weight("w_in", l), mode="nn", out_dtype=BF, name=f"proj_{l}",
                    tm=1024, tn=1024, tk=1024)
    o_g, l_g, qkv_g = [], [], []
    for g in range(N_GROUPS):
        qkv, cols = _group_qkv(proj, g)
        og, lg = plan.run(f"attn_a_fwd_g{g}_{l}", _attn_a_fwd, qkv, cols, g)
        o_g.append(_undil(og))
        l_g.append(_undil(lg))
        qkv_g.append((qkv, cols))
    ob = plan.run(f"sb_fwd_{l}", _sb_fwd, proj)
    x1, merged, oa, lse = plan.run(f"wo_{l}", _merge_wo, o_g, l_g, ob, proj, p["b_gate"][l], plan.weight("w_br", l),
                                   plan.weight("w_o", l), x, f"wo_{l}")
    h2 = _rms_fwd(x1, p["norm2"][l], f"rms2_fwd_{l}")
    up = plan.run(f"up_{l}", _matmul, h2, plan.weight("w_up", l), mode="nn", out_dtype=BF, name=f"up_{l}",
                  tm=1024, tn=1408, tk=1024)
    x2, act, ac = plan.run(f"down_{l}", _ffn_down, up, p["conv_w"][l], p["conv_b"][l], plan.weight("w_down", l), x1,
                           name=f"down_{l}")
    saved = dict(x=x, h=h, proj=proj, qkv_g=qkv_g, oa=oa, ob=ob, lse=lse, merged=merged, x1=x1, h2=h2, up=up, act=act, ac=ac)
    return x2, saved


def _layer_bwd(dx2, sv, p, plan, l):
    gr = {}
    dwd = plan.run(f"dw_down_{l}", _matmul, sv["act"], dx2, mode="tn", out_dtype=BF, name=f"dw_down_{l}",
                   tm=1408, tn=1024, tk=2048)
    plan.scatter(f"d_h2_{l}", "w_down", l, dwd)
    dup, gr["conv_w"], dcb = plan.run(f"ffn_bwd_{l}", _ffn_bwd, dx2, plan.weight("w_down", l), sv["up"], sv["ac"],
                                      p["conv_w"][l])
    gr["conv_b"] = dcb[0]
    dx1, dn2 = plan.run(f"d_h2_{l}", _matmul_rms_bwd, dup, plan.weight("w_up", l), sv["x1"], p["norm2"][l], dx2,
                        name=f"d_h2_{l}")
    dwu = plan.run(f"dw_up_{l}", _matmul, sv["h2"], dup, mode="tn", out_dtype=BF, name=f"dw_up_{l}",
                   tm=1024, tn=1408, tk=2048)
    plan.scatter(f"sb_bwd_{l}", "w_up", l, dwu)
    gr["norm2"] = dn2[0]
    dm = plan.run(f"d_merged_{l}", _matmul, dx1, plan.weight("w_o", l), mode="nt", out_dtype=BF,
                  name=f"d_merged_{l}", tm=1024, tn=1024, tk=1024)
    dwo = plan.run(f"dw_o_{l}", _matmul, sv["merged"], dx1, mode="tn", out_dtype=BF, name=f"dw_o_{l}",
                   tm=1024, tn=1024, tk=2048)
    plan.scatter(f"dw_in_{l}", "w_o", l, dwo)
    dya, dyb, doa, dob, stats, dgate, dbg = _merge_bwd(dm, sv["oa"], sv["ob"], sv["lse"], sv["proj"], p["b_gate"][l],
                                                       plan.weight("w_br", l))
    gr["b_gate"] = dbg[0]
    dwa = plan.run(f"dw_bra_{l}", _matmul, sv["oa"], dya, mode="tn", out_dtype=BF, name=f"dw_bra_{l}",
                   tm=256, tn=1024, tk=2048)
    dwb = plan.run(f"dw_brb_{l}", _matmul, sv["ob"], dyb, mode="tn", out_dtype=BF, name=f"dw_brb_{l}",
                   tm=256, tn=1024, tk=2048)
    plan.scatter(f"dw_in_{l}", "w_br", l, jnp.concatenate([dwa, dwb], axis=0))
    proj = sv["proj"]
    dq_a, dk_a, dv_a = [], [], []
    for g, (_, dil) in enumerate(DSW_GROUPS):
        qkv, cols = sv["qkv_g"][g]
        dqg, dkg, dvg = _attn_a_bwd(qkv, cols, _dil(doa, dil), _dil(stats, dil), g)
        dq_a.append(_undil(dqg))
        dk_a.append(_undil(dkg))
        dv_a.append(_undil(dvg))
    dqb, dkb, dvb = plan.run(f"sb_bwd_{l}", _sb_bwd, proj, dob, sv["ob"])
    dproj = jnp.concatenate(dq_a + dk_a + dv_a + [dqb.astype(BF), dkb.astype(BF), dvb.astype(BF), dgate], axis=1)
    dx, dn1 = plan.run(f"d_h_{l}", _matmul_rms_bwd, dproj, plan.weight("w_in", l), sv["x"], p["norm1"][l], dx1,
                       name=f"d_h_{l}")
    if l > 0:
        dwi = plan.run(f"dw_in_{l}", _matmul, sv["h"], dproj, mode="tn", out_dtype=BF, name=f"dw_in_{l}",
                       tm=1024, tn=1280, tk=2048)
        plan.scatter(f"ffn_bwd_{l - 1}", "w_in", l, dwi)
    else:
        half = D_MODEL // 2
        for part, slab in enumerate((sv["h"][:, :half], sv["h"][:, half:])):
            name = f"dw_in_{l}" if part == 0 else f"dw_in_{l}_rest"
            dwi = plan.run(name, _matmul, slab, dproj, mode="tn", out_dtype=BF, name=name, tm=half, tn=1280, tk=2048)
            plan.scatter(f"dw_in_{l}_rest" if part == 0 else "alone", "w_in", l, dwi, part)
    gr["norm1"] = dn1[0]
    return dx, gr


def kernel(x, norm1, w_in, b_gate, w_br, w_o, norm2, w_up, conv_w, conv_b, w_down, norm_f, loss_target, m_norm1, m_w_in, m_b_gate, m_w_br, m_w_o, m_norm2, m_w_up, m_conv_w, m_conv_b, m_w_down, m_norm_f, v_norm1, v_w_in, v_b_gate, v_w_br, v_w_o, v_norm2, v_w_up, v_conv_w, v_conv_b, v_w_down, v_norm_f):
    depth = norm1.shape[0]
    me = 4 * lax.axis_index("x") + 2 * lax.axis_index("y") + lax.axis_index("c")
    shards = dict(w_in=w_in, w_br=w_br, w_o=w_o, w_up=w_up, w_down=w_down)
    moments_m = dict(norm1=m_norm1, w_in=m_w_in, b_gate=m_b_gate, w_br=m_w_br, w_o=m_w_o, norm2=m_norm2,
                     w_up=m_w_up, conv_w=m_conv_w, conv_b=m_conv_b, w_down=m_w_down, norm_f=m_norm_f)
    moments_v = dict(norm1=v_norm1, w_in=v_w_in, b_gate=v_b_gate, w_br=v_w_br, w_o=v_w_o, norm2=v_norm2,
                     w_up=v_w_up, conv_w=v_conv_w, conv_b=v_conv_b, w_down=v_w_down, norm_f=v_norm_f)

    plan = _Plan()
    wb = {n: s.astype(BF) for n, s in shards.items()}
    p = dict(norm1=norm1, b_gate=b_gate, norm2=norm2, conv_b=conv_b)
    cw = _all_gather(conv_w, "gather_conv_w")
    p["conv_w"] = cw.transpose(1, 2, 0, 3).reshape(depth, 3, D_FF)
    plan.landed[("w_in", 0)] = _all_gather(wb["w_in"][0], "gather_w_in_0")
    for l in range(depth):
        plan.ride(f"proj_{l}", ("w_down", l), "gather", wb["w_down"][l])
        plan.ride(f"attn_a_fwd_g0_{l}" if l == 0 else f"down_{l - 1}", ("w_br", l), "gather", wb["w_br"][l])
        plan.ride(f"attn_a_fwd_g0_{l}" if l == 0 else f"down_{l - 1}", ("w_o", l), "gather", wb["w_o"][l])
        plan.ride(f"sb_fwd_{l}", ("w_up", l), "gather", wb["w_up"][l])
        if l + 1 < depth:
            plan.ride(f"up_{l}", ("w_in", l + 1), "gather", wb["w_in"][l + 1])

    xs = x[0]
    saved = []
    for l in range(depth):
        xs, sv = _layer_fwd(xs, p, plan, l)
        saved.append(sv)
    loss_part, dx, dnf = _loss_head(xs, norm_f, loss_target[0])
    loss = lax.psum(loss_part[0, 0], ("x", "y", "c"))

    grads = [None] * depth
    for l in reversed(range(depth)):
        dx, grads[l] = _layer_bwd(dx, saved[l], p, plan, l)
    grad_x = dx[None]
    (key, _, last), = plan.riders.pop("alone")
    plan.landed[key] = _all_to_all(last, "scatter_w_in_rest")
    assert not plan.riders, sorted(plan.riders)

    out_g, out_d, out_m, out_v = {}, {}, {}, {}
    for n in _COL_SHARDED + _ROW_SHARDED:
        shp = shards[n].shape
        flat = (shp[0] * shp[1], shp[2])
        res = None
        for l in range(depth):
            row = l * shp[1]
            for key in sorted(k for k in plan.landed if k[:2] == ("d" + n, l)):
                parts = plan.landed[key]
                res = _adamw(parts, shards[n].reshape(flat), moments_m[n].reshape(flat), moments_v[n].reshape(flat),
                             f"adamw_{n}_{l}_{key[2]}", row0=row, prev=res)
                row += parts.shape[1]
        out_g[n], out_d[n], out_m[n], out_v[n] = [r.reshape(shp) for r in res]

    small = ("norm1", "b_gate", "norm2", "conv_b")
    vecs = [jnp.stack([grads[l][n] for l in range(depth)]).reshape(-1) for n in small]
    vecs.append(dnf.reshape(-1))
    vecs.append(jnp.stack([grads[l]["conv_w"] for l in range(depth)]).reshape(-1))
    sizes = [v.shape[0] for v in vecs]
    flat = jnp.concatenate(vecs)
    n_small = sum(sizes[:-1])
    pad = (-flat.shape[0]) % 1024
    flat = jnp.pad(flat, (0, pad)).reshape(-1, 128)
    allp = _all_gather(flat, "gather_small_grads").reshape(N_DEV, -1)
    rep_w = jnp.concatenate([norm1.reshape(-1), b_gate.reshape(-1), norm2.reshape(-1), conv_b.reshape(-1), norm_f])
    rep_m = jnp.concatenate([moments_m[n].reshape(-1) for n in small] + [m_norm_f])
    rep_v = jnp.concatenate([moments_v[n].reshape(-1) for n in small] + [v_norm_f])
    rows = n_small // 128
    res = _adamw(allp[:, :n_small].reshape(N_DEV, rows, 128), rep_w.reshape(rows, 128), rep_m.reshape(rows, 128),
                 rep_v.reshape(rows, 128), "adamw_small")
    off = 0
    for n, sz in zip(small + ("norm_f",), sizes[:-1]):
        shp = norm_f.shape if n == "norm_f" else p[n].shape
        out_g[n], out_d[n], out_m[n], out_v[n] = [r.reshape(-1)[off:off + sz].reshape(shp) for r in res]
        off += sz
    f = conv_w.shape[2]
    cwp = allp[:, n_small:n_small + sizes[-1]].reshape(N_DEV, depth * 3, D_FF)
    cwp = lax.dynamic_slice_in_dim(cwp, me * f, f, axis=2)
    res = _adamw(cwp, conv_w.reshape(depth * 3, f), m_conv_w.reshape(depth * 3, f), v_conv_w.reshape(depth * 3, f),
                 "adamw_conv_w")
    out_g["conv_w"], out_d["conv_w"], out_m["conv_w"], out_v["conv_w"] = [r.reshape(conv_w.shape) for r in res]

    order = ("norm1", "w_in", "b_gate", "w_br", "w_o", "norm2", "w_up", "conv_w", "conv_b", "w_down", "norm_f")
    return (loss, grad_x, *[out_g[n] for n in order], *[out_d[n] for n in order],
            *[out_m[n] for n in order], *[out_v[n] for n in order])
```

```python
import functools

import jax
import jax.numpy as jnp
from jax import lax
from jax.experimental import pallas as pl
from jax.experimental.pallas import tpu as pltpu

BF = jnp.bfloat16
F32 = jnp.float32

N_DEV = 8
D_MODEL = 1024
HEAD_DIM = 64
DSW_GROUPS = ((128, 1), (512, 4), (2048, 16))
HEADS_PER_GROUP = 4
N_GROUPS = len(DSW_GROUPS)
DSW_HEADS = HEADS_PER_GROUP * N_GROUPS
SB_HEADS = 4
W_A = DSW_HEADS * HEAD_DIM
W_B = SB_HEADS * HEAD_DIM
OUT_A = HEADS_PER_GROUP * HEAD_DIM
N_IN = 3 * W_A + 3 * W_B + 2 * D_MODEL
GATE_OFF = 3 * W_A + 3 * W_B
D_FF = 2816
SB_BLOCK = 256
RMS_EPS = 1e-6
ATT_SCALE = HEAD_DIM ** -0.5
NEG = -1e30
SB_EXIT = -110.0
FFN_CHUNK = 256

ADAM_LR = 0.001
ADAM_B1 = 0.9
ADAM_B2 = 0.999
ADAM_EPS = 1e-08
ADAM_WD = 0.01
ADAM_STEP = 10

HBM_SPEC = pl.BlockSpec(memory_space=pltpu.HBM)
MESH = pl.DeviceIdType.MESH

_NN = (((1,), (0,)), ((), ()))
_NT = (((1,), (1,)), ((), ()))
_TN = (((0,), (0,)), ((), ()))


def _dot(a, b, dn=_NN):
    return lax.dot_general(a, b, dn, preferred_element_type=F32)


def _pick(dim, pref):
    if dim <= pref:
        return dim
    t = (pref // 128) * 128
    while t >= 128:
        if dim % t == 0:
            return t
        t -= 128
    return dim


def _params(*sem):
    return pltpu.CompilerParams(dimension_semantics=sem)


def _peer(k):
    x, y, c = lax.axis_index("x"), lax.axis_index("y"), lax.axis_index("c")
    px = 1 - x if (k >> 2) & 1 else x
    py = 1 - y if (k >> 1) & 1 else y
    pc = 1 - c if k & 1 else c
    return (px, py, pc), 4 * px + 2 * py + pc


def _exchange(kind, x_ref, out_ref, send_sems, recv_sems, local_sem):
    gather = kind == "gather"
    _, me = _peer(0)

    def src(idx):
        return x_ref if gather else x_ref.at[idx]

    def copy(k, dst_idx):
        peer, pidx = _peer(k)
        return pltpu.make_async_remote_copy(
            src_ref=src(pidx), dst_ref=out_ref.at[dst_idx], send_sem=send_sems.at[k - 1],
            recv_sem=recv_sems.at[k - 1], device_id=peer, device_id_type=MESH)

    mine = pltpu.make_async_copy(src(me), out_ref.at[me], local_sem)

    def start():
        mine.start()
        for k in range(1, N_DEV):
            copy(k, me).start()

    def wait():
        for k in range(1, N_DEV):
            copy(k, _peer(k)[1]).wait_recv()
        for k in range(1, N_DEV):
            copy(k, me).wait_send()
        mine.wait()

    return start, wait


_EXCHANGE_SEMS = [pltpu.SemaphoreType.DMA((N_DEV - 1,)), pltpu.SemaphoreType.DMA((N_DEV - 1,)),
                  pltpu.SemaphoreType.DMA]


def _exchange_shape(kind, x):
    return jax.ShapeDtypeStruct(((N_DEV,) + x.shape) if kind == "gather" else x.shape, x.dtype)


def _exchange_alone(kind, x, name):
    def body(x_ref, out_ref, send_sems, recv_sems, local_sem):
        start, wait = _exchange(kind, x_ref, out_ref, send_sems, recv_sems, local_sem)
        start()
        wait()

    return pl.pallas_call(
        body, name=name, out_shape=_exchange_shape(kind, x),
        in_specs=[HBM_SPEC], out_specs=HBM_SPEC, scratch_shapes=list(_EXCHANGE_SEMS),
    )(x)


def _all_gather_via_sibling(x, name):
    def body(x_ref, out_ref, send_sems, recv_sems, local_sem):
        x_, y_, c_ = lax.axis_index("x"), lax.axis_index("y"), lax.axis_index("c")
        me, sibling = (x_, y_, c_), (x_, y_, 1 - c_)
        chips = [(1 - x_, y_), (x_, 1 - y_), (1 - x_, 1 - y_)]

        def slot(px, py, pc):
            return out_ref.at[4 * px + 2 * py + pc]

        def copy(k, block, to, src=None):
            return pltpu.make_async_remote_copy(
                src_ref=slot(*block) if src is None else src, dst_ref=slot(*block), send_sem=send_sems.at[k],
                recv_sem=recv_sems.at[k], device_id=to, device_id_type=MESH)

        mine = pltpu.make_async_copy(x_ref, slot(*me), local_sem)
        mine.start()
        first = [copy(0, me, sibling, src=x_ref)]
        first += [copy(1 + j, me, (*chip, c_), src=x_ref) for j, chip in enumerate(chips)]
        for cp in first:
            cp.start()
        passed = [copy(4 + j, (*chip, c_), sibling) for j, chip in enumerate(chips)]
        for j, chip in enumerate(chips):
            copy(1 + j, (*chip, c_), me).wait_recv()
            passed[j].start()
        copy(0, sibling, me).wait_recv()
        for j, chip in enumerate(chips):
            copy(4 + j, (*chip, 1 - c_), me).wait_recv()
        for cp in first + passed:
            cp.wait_send()
        mine.wait()

    return pl.pallas_call(
        body, name=name, out_shape=_exchange_shape("gather", x),
        in_specs=[HBM_SPEC], out_specs=HBM_SPEC,
        scratch_shapes=[pltpu.SemaphoreType.DMA((N_DEV - 1,)), pltpu.SemaphoreType.DMA((N_DEV - 1,)),
                        pltpu.SemaphoreType.DMA],
    )(x)


def _all_gather(x, name):
    return _exchange_alone("gather", x, name)


def _all_to_all(x, name):
    return _exchange_alone("scatter", x, name)


def _call(body, *, name, grid, in_specs, out_specs, out_shape, args, scratch_shapes=(), sem=None, comm=()):
    single = not isinstance(out_shape, (tuple, list))
    outs = (out_shape,) if single else tuple(out_shape)
    ospecs = (out_specs,) if single else tuple(out_specs)
    if not comm:
        res = pl.pallas_call(
            body, name=name, out_shape=outs, grid=grid, in_specs=list(in_specs), out_specs=ospecs,
            scratch_shapes=list(scratch_shapes), compiler_params=_params(*(sem or ("arbitrary",) * len(grid))),
        )(*args)
        return res
    n_in, n_out, n_scr, nc = len(in_specs), len(outs), len(scratch_shapes), len(comm)

    def wrapped(*refs):
        ins = refs[:n_in]
        cins = refs[n_in:n_in + nc]
        o0 = n_in + nc
        kouts = refs[o0:o0 + n_out]
        couts = refs[o0 + n_out:o0 + n_out + nc]
        s0 = o0 + n_out + nc
        scr = refs[s0:s0 + n_scr]
        sems = refs[s0 + n_scr:]
        ids = [pl.program_id(ax) for ax in range(len(grid))]
        first = functools.reduce(jnp.logical_and, [i == 0 for i in ids])
        last = functools.reduce(jnp.logical_and, [i == g - 1 for i, g in zip(ids, grid)])
        ex = [_exchange(comm[c][0], cins[c], couts[c], *sems[3 * c:3 * c + 3]) for c in range(nc)]

        @pl.when(first)
        def _():
            for start, _ in ex:
                start()

        body(*ins, *kouts, *scr)

        @pl.when(last)
        def _():
            for _, wait in ex:
                wait()

    return pl.pallas_call(
        wrapped, name=name,
        out_shape=outs + tuple(_exchange_shape(k, x) for k, x in comm),
        grid=grid, in_specs=list(in_specs) + [HBM_SPEC] * nc, out_specs=ospecs + (HBM_SPEC,) * nc,
        scratch_shapes=list(scratch_shapes) + list(_EXCHANGE_SEMS) * nc,
        compiler_params=_params(*(("arbitrary",) * len(grid))),
    )(*args, *[x for _, x in comm])


def _matmul(a, b, *, mode, out_dtype, name, tm=512, tn=1024, tk=1024, res=None, comm=()):
    if mode == "nn":
        (M, K), (_, N) = a.shape, b.shape
    elif mode == "nt":
        (M, K), (N, _) = a.shape, b.shape
    else:
        (K, M), (_, N) = a.shape, b.shape
    tm, tn, tk = _pick(M, tm), _pick(N, tn), _pick(K, tk)
    nk = K // tk
    dn = {"nn": _NN, "nt": _NT, "tn": _TN}[mode]

    def body(*refs):
        a_ref, b_ref = refs[0], refs[1]
        r_ref = refs[2] if res is not None else None
        o_ref = refs[3] if res is not None else refs[2]

        def finish(r):
            if res is not None:
                r = r + r_ref[...].astype(F32)
            o_ref[...] = r.astype(out_dtype)

        part = _dot(a_ref[...].astype(BF), b_ref[...].astype(BF), dn)
        if nk == 1:
            finish(part)
            return
        acc = refs[-1]
        k = pl.program_id(2)

        @pl.when(k == 0)
        def _():
            acc[...] = jnp.zeros_like(acc)

        acc[...] += part

        @pl.when(k == nk - 1)
        def _():
            finish(acc[...])

    if mode == "tn":
        a_spec = pl.BlockSpec((tk, tm), lambda j, i, k: (k, i))
    else:
        a_spec = pl.BlockSpec((tm, tk), lambda j, i, k: (i, k))
    if mode == "nt":
        b_spec = pl.BlockSpec((tn, tk), lambda j, i, k: (j, k))
    else:
        b_spec = pl.BlockSpec((tk, tn), lambda j, i, k: (k, j))
    o_spec = pl.BlockSpec((tm, tn), lambda j, i, k: (i, j))
    in_specs = [a_spec, b_spec] + ([o_spec] if res is not None else [])
    args = (a, b) + ((res,) if res is not None else ())
    out = _call(body, name=name, grid=(N // tn, M // tm, nk), in_specs=in_specs, out_specs=o_spec,
                out_shape=jax.ShapeDtypeStruct((M, N), out_dtype), args=args,
                scratch_shapes=[pltpu.VMEM((tm, tn), F32)] if nk > 1 else [],
                sem=("parallel", "parallel", "arbitrary"), comm=comm)
    return out[0], list(out[1:])


def _rms_fwd(x, g, name):
    S, D = x.shape
    tm = _pick(S, 512)

    def body(x_ref, g_ref, h_ref):
        xf = x_ref[...]
        r = lax.rsqrt(jnp.mean(xf * xf, axis=-1, keepdims=True) + RMS_EPS)
        h_ref[...] = (xf * r * g_ref[...]).astype(BF)

    return pl.pallas_call(
        body, name=name,
        out_shape=jax.ShapeDtypeStruct((S, D), BF),
        grid=(S // tm,),
        in_specs=[pl.BlockSpec((tm, D), lambda i: (i, 0)), pl.BlockSpec((1, D), lambda i: (0, 0))],
        out_specs=pl.BlockSpec((tm, D), lambda i: (i, 0)),
        compiler_params=_params("parallel"),
    )(x, g.reshape(1, D))


def _rms_bwd(x, g, dh, dres, name):
    S, D = x.shape
    tm = _pick(S, 512)

    def body(x_ref, g_ref, dh_ref, dres_ref, dx_ref, dg_ref):
        @pl.when(pl.program_id(0) == 0)
        def _():
            dg_ref[...] = jnp.zeros_like(dg_ref)

        xf = x_ref[...]
        r = lax.rsqrt(jnp.mean(xf * xf, axis=-1, keepdims=True) + RMS_EPS)
        xh = xf * r
        dy = dh_ref[...].astype(F32)
        dg_ref[...] += jnp.sum(dy * xh, axis=0, keepdims=True)
        dxh = dy * g_ref[...]
        dx = r * (dxh - xh * jnp.mean(dxh * xh, axis=-1, keepdims=True))
        dx_ref[...] = dres_ref[...] + dx

    row = pl.BlockSpec((tm, D), lambda i: (i, 0))
    vec = pl.BlockSpec((1, D), lambda i: (0, 0))
    return pl.pallas_call(
        body, name=name,
        out_shape=(jax.ShapeDtypeStruct((S, D), F32), jax.ShapeDtypeStruct((1, D), F32)),
        grid=(S // tm,),
        in_specs=[row, vec, row, row], out_specs=(row, vec),
        compiler_params=_params("arbitrary"),
    )(x, g.reshape(1, D), dh, dres)


def _matmul_rms_bwd(dy, w, x, g, dres, *, name, tm=512, comm=()):
    S, K = dy.shape
    D = w.shape[0]
    tm = _pick(S, tm)

    def body(dy_ref, w_ref, x_ref, g_ref, dres_ref, dx_ref, dg_ref):
        @pl.when(pl.program_id(0) == 0)
        def _():
            dg_ref[...] = jnp.zeros_like(dg_ref)

        dh = _dot(dy_ref[...].astype(BF), w_ref[...], _NT)
        xf = x_ref[...]
        r = lax.rsqrt(jnp.mean(xf * xf, axis=-1, keepdims=True) + RMS_EPS)
        xh = xf * r
        dg_ref[...] += jnp.sum(dh * xh, axis=0, keepdims=True)
        dxh = dh * g_ref[...]
        dx_ref[...] = dres_ref[...] + r * (dxh - xh * jnp.mean(dxh * xh, axis=-1, keepdims=True))

    row = pl.BlockSpec((tm, D), lambda i: (i, 0))
    vec = pl.BlockSpec((1, D), lambda i: (0, 0))
    res = _call(body, name=name, grid=(S // tm,),
                in_specs=[pl.BlockSpec((tm, K), lambda i: (i, 0)), pl.BlockSpec((D, K), lambda i: (0, 0)), row, vec, row],
                out_specs=(row, vec),
                out_shape=(jax.ShapeDtypeStruct((S, D), F32), jax.ShapeDtypeStruct((1, D), F32)),
                args=(dy, w, x, g.reshape(1, D), dres), comm=comm)
    return res[0], res[1], list(res[2:])


def _loss_head(x, g, target):
    S, D = x.shape
    tm = _pick(S, 512)

    def body(x_ref, g_ref, t_ref, loss_ref, dx_ref, dg_ref):
        @pl.when(pl.program_id(0) == 0)
        def _():
            dg_ref[...] = jnp.zeros_like(dg_ref)
            loss_ref[...] = jnp.zeros_like(loss_ref)

        xf = x_ref[...]
        gg = g_ref[...]
        r = lax.rsqrt(jnp.mean(xf * xf, axis=-1, keepdims=True) + RMS_EPS)
        xh = xf * r
        err = xh * gg - t_ref[...]
        per_tok = jnp.mean(err * err, axis=-1, keepdims=True)
        loss_ref[...] += 0.5 * jnp.sum(per_tok, axis=0, keepdims=True)
        dy = err * (1.0 / D)
        dg_ref[...] += jnp.sum(dy * xh, axis=0, keepdims=True)
        dxh = dy * gg
        dx_ref[...] = r * (dxh - xh * jnp.mean(dxh * xh, axis=-1, keepdims=True))

    row = pl.BlockSpec((tm, D), lambda i: (i, 0))
    vec = pl.BlockSpec((1, D), lambda i: (0, 0))
    one = pl.BlockSpec((1, 1), lambda i: (0, 0))
    return pl.pallas_call(
        body, name="loss_head",
        out_shape=(jax.ShapeDtypeStruct((1, 1), F32), jax.ShapeDtypeStruct((S, D), F32),
                   jax.ShapeDtypeStruct((1, D), F32)),
        grid=(S // tm,),
        in_specs=[row, vec, row], out_specs=(one, row, vec),
        compiler_params=_params("arbitrary"),
    )(x, g.reshape(1, D), target)


def _slopes(g):
    return [2.0 ** (-8.0 * (HEADS_PER_GROUP * g + j + 1) / DSW_HEADS) for j in range(HEADS_PER_GROUP)]


def _band_masks(W):
    row = lax.broadcasted_iota(jnp.int32, (W, W), 0)
    col = lax.broadcasted_iota(jnp.int32, (W, W), 1)
    d_cur = row - col
    d_prev = d_cur + W
    return d_cur, d_prev, d_cur >= 0, d_cur <= 0


def _band_specs(W, nb, per):
    def cur(c):
        return pl.BlockSpec((None, per * W, OUT_A), lambda r, n: (r, n, c))

    def prev(c):
        return pl.BlockSpec((None, W, OUT_A), lambda r, n: (r, jnp.maximum(per * n - 1, 0), c))

    def nxt(c):
        return pl.BlockSpec((None, W, OUT_A), lambda r, n: (r, jnp.minimum(per * (n + 1), nb - 1), c))

    return cur, prev, nxt


def _blocks_per_step(nb):
    return 2 if nb % 2 == 0 else 1


def _head_stack(W):
    H, hd = HEADS_PER_GROUP, HEAD_DIM
    lane_head = lax.broadcasted_iota(jnp.int32, (W, OUT_A), 1) // hd

    def stack(x):
        return jnp.concatenate([jnp.where(lane_head == h, x, jnp.zeros_like(x)) for h in range(H)], axis=0)

    def unstack(y):
        out = jnp.where(lane_head == 0, y[0:W], 0.0)
        for h in range(1, H):
            out = jnp.where(lane_head == h, y[h * W:(h + 1) * W], out)
        return out

    def column(ref, rows, off=0):
        return jnp.concatenate([ref[rows, h * hd + off:h * hd + off + 1] for h in range(H)], axis=0)

    def tile(x):
        return jnp.concatenate([x] * H, axis=0)

    return stack, unstack, column, tile


def _stacked_bias(W, slopes, dil):
    d_cur, d_prev, m_cur, m_prev = _band_masks(W)
    b_cur = jnp.concatenate([(s * dil) * d_cur.astype(F32) for s in slopes], axis=0)
    b_prev = jnp.concatenate([(s * dil) * d_prev.astype(F32) for s in slopes], axis=0)
    H = len(slopes)
    return b_cur, b_prev, jnp.concatenate([m_cur] * H, axis=0), jnp.concatenate([m_prev] * H, axis=0)


def _attn_a_fwd(qkv, cols, g, comm=()):
    win, dil = DSW_GROUPS[g]
    W = win // dil
    d, L, _ = qkv[0].shape
    nb = L // W
    per = _blocks_per_step(nb)
    slopes = _slopes(g)

    def body(q_ref, kp_ref, kc_ref, vp_ref, vc_ref, o_ref, l_ref):
        n = pl.program_id(1)
        stack, unstack, _, _ = _head_stack(W)
        b_cur, b_prev, m_cur, m_prev = _stacked_bias(W, slopes, dil)
        m_first = jnp.logical_and(m_prev, n > 0)
        for b in range(per):
            rows = slice(b * W, (b + 1) * W)
            before = slice((b - 1) * W, b * W)
            qs = stack(q_ref[rows, :])
            kc, vc = kc_ref[rows, :], vc_ref[rows, :]
            kp, vp = (kp_ref[...], vp_ref[...]) if b == 0 else (kc_ref[before, :], vc_ref[before, :])
            s_c = jnp.where(m_cur, _dot(qs, kc, _NT) * ATT_SCALE - b_cur, NEG)
            s_p = jnp.where(m_first if b == 0 else m_prev, _dot(qs, kp, _NT) * ATT_SCALE - b_prev, NEG)
            m = jnp.maximum(jnp.max(s_c, axis=1, keepdims=True), jnp.max(s_p, axis=1, keepdims=True))
            p_c = jnp.exp(s_c - m)
            p_p = jnp.exp(s_p - m)
            den = jnp.sum(p_c, axis=1, keepdims=True) + jnp.sum(p_p, axis=1, keepdims=True)
            pv = _dot(p_c.astype(BF), vc) + _dot(p_p.astype(BF), vp)
            o_ref[rows, :] = unstack(pv / den)
            l_ref[rows, :] = unstack(jnp.broadcast_to(m + jnp.log(den), pv.shape))

    cur, prev, _ = _band_specs(W, nb, per)
    out = jax.ShapeDtypeStruct((d, L, OUT_A), F32)
    res = _call(body, name=f"attn_a_fwd_g{g}", grid=(d, nb // per),
                in_specs=[cur(cols[0]), prev(cols[1]), cur(cols[1]), prev(cols[2]), cur(cols[2])],
                out_specs=(cur(0), cur(0)), out_shape=(out, out),
                args=(qkv[0], qkv[1], qkv[1], qkv[2], qkv[2]), sem=("parallel", "parallel"), comm=comm)
    return res[0], res[1], list(res[2:])


def _attn_a_bwd(qkv, cols, do, stats, g):
    win, dil = DSW_GROUPS[g]
    W = win // dil
    d, L, _ = qkv[0].shape
    nb = L // W
    per = _blocks_per_step(nb)
    nsteps = nb // per
    slopes = _slopes(g)

    def body(q_ref, qn_ref, kp_ref, kc_ref, vp_ref, vc_ref, do_ref, don_ref, st_ref, stn_ref,
             dq_ref, dk_ref, dv_ref):
        n = pl.program_id(1)
        stack, unstack, column, _ = _head_stack(W)
        b_cur, b_prev, m_cur, m_prev = _stacked_bias(W, slopes, dil)
        m_first = jnp.logical_and(m_prev, n > 0)
        m_last = jnp.logical_and(m_prev, n < nsteps - 1)
        everything = slice(None)
        for b in range(per):
            rows = slice(b * W, (b + 1) * W)
            before = slice((b - 1) * W, b * W)
            after = slice((b + 1) * W, (b + 2) * W)
            first, last = b == 0, b == per - 1
            qs = stack(q_ref[rows, :])
            qn = stack(qn_ref[...] if last else q_ref[after, :])
            dos = stack(do_ref[rows, :])
            don = stack(don_ref[...] if last else do_ref[after, :])
            kc, vc = kc_ref[rows, :], vc_ref[rows, :]
            kp, vp = (kp_ref[...], vp_ref[...]) if first else (kc_ref[before, :], vc_ref[before, :])
            lse_c, dsum_c = column(st_ref, rows), column(st_ref, rows, STAT_OFF)
            lse_n = column(stn_ref, everything) if last else column(st_ref, after)
            dsum_n = column(stn_ref, everything, STAT_OFF) if last else column(st_ref, after, STAT_OFF)
            m_p = m_first if first else m_prev
            m_n = m_last if last else m_prev
            p_cc = jnp.exp(jnp.where(m_cur, _dot(qs, kc, _NT) * ATT_SCALE - b_cur, NEG) - lse_c)
            p_cp = jnp.exp(jnp.where(m_p, _dot(qs, kp, _NT) * ATT_SCALE - b_prev, NEG) - lse_c)
            p_nc = jnp.exp(jnp.where(m_n, _dot(qn, kc, _NT) * ATT_SCALE - b_prev, NEG) - lse_n)
            ds_cc = (p_cc * (_dot(dos, vc, _NT) - dsum_c) * ATT_SCALE).astype(BF)
            ds_cp = (p_cp * (_dot(dos, vp, _NT) - dsum_c) * ATT_SCALE).astype(BF)
            ds_nc = (p_nc * (_dot(don, vc, _NT) - dsum_n) * ATT_SCALE).astype(BF)
            dq_ref[rows, :] = unstack(_dot(ds_cc, kc) + _dot(ds_cp, kp)).astype(BF)
            dk_ref[rows, :] = (_dot(ds_cc, qs, _TN) + _dot(ds_nc, qn, _TN)).astype(BF)
            dv_ref[rows, :] = (_dot(p_cc.astype(BF), dos, _TN) + _dot(p_nc.astype(BF), don, _TN)).astype(BF)

    cur, prev, nxt = _band_specs(W, nb, per)
    out = jax.ShapeDtypeStruct((d, L, OUT_A), BF)
    cq, ck, cv = cols
    return pl.pallas_call(
        body, name=f"attn_a_bwd_g{g}",
        out_shape=(out, out, out),
        grid=(d, nsteps),
        in_specs=[cur(cq), nxt(cq), prev(ck), cur(ck), prev(cv), cur(cv), cur(0), nxt(0), cur(0), nxt(0)],
        out_specs=(cur(0), cur(0), cur(0)),
        compiler_params=_params("parallel", "parallel"),
    )(qkv[0], qkv[0], qkv[1], qkv[1], qkv[2], qkv[2], do, do, stats, stats)


SB_PAIR = 2
SB_QROWS = SB_BLOCK


def _softplus_parts(z):
    e = jnp.exp(-jnp.abs(z))
    log1p_e = jnp.where(e < 1e-4, e, jnp.log(1.0 + e))
    return e, jnp.maximum(z, 0.0) + log1p_e


def _split_dot(x, t):
    hi = x.astype(BF)
    lo = (x - hi.astype(F32)).astype(BF)
    return _dot(hi, t) + _dot(lo, t)


def _sb_block(qh, kk, causal, r_run, tri_incl):
    z = _dot(qh, kk, _NT)
    e, sp = _softplus_parts(z)
    ls = jnp.where(causal, -sp, 0.0)
    cin = _split_dot(ls, tri_incl)
    a = jnp.where(causal, jnp.exp(z + cin + r_run), 0.0)
    return z, e, cin, a


def _sb_specs(S):
    Q, hd = SB_QROWS, HEAD_DIM
    lanes = SB_PAIR * hd
    qc = (3 * W_A) // lanes
    kc = (3 * W_A + W_B) // lanes
    vc = (3 * W_A + 2 * W_B) // lanes
    q_spec = pl.BlockSpec((Q, lanes), lambda p, i: (i, qc + p))
    k_spec = pl.BlockSpec((S, lanes), lambda p, i: (0, kc + p))
    v_spec = pl.BlockSpec((S, lanes), lambda p, i: (0, vc + p))
    o_spec = pl.BlockSpec((Q, lanes), lambda p, i: (i, p))
    full = pl.BlockSpec((S, lanes), lambda p, i: (0, p))
    return q_spec, k_spec, v_spec, o_spec, full


def _sb_stack():
    Q, hd = SB_QROWS, HEAD_DIM
    lane_head = lax.broadcasted_iota(jnp.int32, (Q, SB_PAIR * hd), 1) // hd

    def stack(x):
        return jnp.concatenate([jnp.where(lane_head == h, x, jnp.zeros_like(x)) for h in range(SB_PAIR)], axis=0)

    def unstack(y):
        out = jnp.where(lane_head == 0, y[0:Q], 0.0)
        for h in range(1, SB_PAIR):
            out = jnp.where(lane_head == h, y[h * Q:(h + 1) * Q], out)
        return out

    return stack, unstack


def _sb_iotas(i):
    B, Q = SB_BLOCK, SB_QROWS
    row = lax.broadcasted_iota(jnp.int32, (Q, B), 0) + i * Q
    col = lax.broadcasted_iota(jnp.int32, (Q, B), 1)
    ahead = jnp.concatenate([col - row] * SB_PAIR, axis=0)
    tr = lax.broadcasted_iota(jnp.int32, (B, B), 0)
    tc = lax.broadcasted_iota(jnp.int32, (B, B), 1)
    return ahead, tr, tc


def _sb_fwd(proj, comm=()):
    S = proj.shape[0]
    B, Q, hd = SB_BLOCK, SB_QROWS, HEAD_DIM
    nq = S // Q
    R = SB_PAIR * Q
    q_spec, k_spec, v_spec, o_spec, _ = _sb_specs(S)

    def body(q_ref, k_ref, v_ref, o_ref):
        i = pl.program_id(1)
        ahead, tr, tc = _sb_iotas(i)
        tri_incl = (tr >= tc).astype(BF)
        stack, unstack = _sb_stack()
        qs = stack(q_ref[...] * ATT_SCALE)

        def cond(c):
            return jnp.logical_and(c[0] >= 0, c[-1] > SB_EXIT)

        def step(c):
            kb, r_run, acc, _ = c
            off = pl.multiple_of(kb * B, B)
            causal = ahead < -kb * B
            _, _, cin, a = _sb_block(qs, k_ref[pl.ds(off, B), :], causal, r_run, tri_incl)
            acc = acc + _dot(a.astype(BF), v_ref[pl.ds(off, B), :])
            r_run = r_run + cin[:, 0:1]
            return kb - 1, r_run, acc, jnp.max(r_run)

        init = (i, jnp.zeros((R, 1), F32), jnp.zeros((R, SB_PAIR * hd), F32), jnp.float32(0.0))
        fin = lax.while_loop(cond, step, init)
        o_ref[...] = unstack(fin[2])

    res = _call(body, name="sb_fwd", grid=(SB_HEADS // SB_PAIR, nq), in_specs=[q_spec, k_spec, v_spec],
                out_specs=o_spec, out_shape=jax.ShapeDtypeStruct((S, W_B), F32), args=(proj, proj, proj),
                sem=("parallel", "parallel"), comm=comm)
    return res[0], list(res[1:])


def _sb_bwd(proj, do, o, comm=()):
    S = proj.shape[0]
    B, Q, hd = SB_BLOCK, SB_QROWS, HEAD_DIM
    nq = S // Q
    R = SB_PAIR * Q
    q_spec, k_spec, v_spec, o_spec, full = _sb_specs(S)

    def body(q_ref, k_ref, v_ref, do_ref, o_ref, dq_ref, dk_ref, dv_ref):
        i = pl.program_id(1)

        @pl.when(i == 0)
        def _():
            dk_ref[...] = jnp.zeros_like(dk_ref)
            dv_ref[...] = jnp.zeros_like(dv_ref)

        ahead, tr, tc = _sb_iotas(i)
        tri_incl = (tr >= tc).astype(BF)
        tri_strict = (tr > tc).astype(BF)
        stack, unstack = _sb_stack()
        qs = stack(q_ref[...] * ATT_SCALE)
        dobs = stack(do_ref[...])
        o_all = o_ref[...]
        dsum = jnp.sum(dobs.astype(F32) * jnp.concatenate([o_all] * SB_PAIR, axis=0), axis=1, keepdims=True)

        def cond(c):
            return jnp.logical_and(c[0] >= 0, c[-1] > SB_EXIT)

        def step(c):
            kb, r_run, g_run, dq, _ = c
            off = pl.multiple_of(kb * B, B)
            causal = ahead < -kb * B
            kk = k_ref[pl.ds(off, B), :]
            vv = v_ref[pl.ds(off, B), :]
            z, e, cin, a = _sb_block(qs, kk, causal, r_run, tri_incl)
            a16 = a.astype(BF)
            gmat = a16.astype(F32) * _dot(dobs, vv, _NT)
            later = _split_dot(gmat, tri_strict)
            pfx = dsum - g_run - later
            sig = jnp.where(z >= 0, 1.0, e) / (1.0 + e)
            dz = jnp.where(causal, gmat - sig * pfx, 0.0).astype(BF)
            dq = dq + _dot(dz, kk)
            dk_ref[pl.ds(off, B), :] += _dot(dz, qs, _TN)
            dv_ref[pl.ds(off, B), :] += _dot(a16, dobs, _TN)
            g_run = g_run + jnp.sum(gmat, axis=1, keepdims=True)
            r_run = r_run + cin[:, 0:1]
            return kb - 1, r_run, g_run, dq, jnp.max(r_run)

        init = (i, jnp.zeros((R, 1), F32), jnp.zeros((R, 1), F32), jnp.zeros((R, SB_PAIR * hd), F32),
                jnp.float32(0.0))
        fin = lax.while_loop(cond, step, init)
        dq_ref[...] = unstack(fin[3]) * ATT_SCALE

    out = jax.ShapeDtypeStruct((S, W_B), F32)
    res = _call(body, name="sb_bwd", grid=(SB_HEADS // SB_PAIR, nq), in_specs=[q_spec, k_spec, v_spec, o_spec, o_spec],
                out_specs=(o_spec, full, full), out_shape=(out, out, out), args=(proj, proj, proj, do, o), comm=comm)
    return res[0], res[1], res[2], list(res[3:])


def _merge_fwd(o_g, l_g, o_b, proj, b_gate, w_br):
    S = o_b.shape[0]
    D = D_MODEL
    tm = _pick(S, 256)
    gcol = GATE_OFF // D

    def body(o0, o1, o2, l0, l1, l2, ob_ref, ga_ref, gb_ref, bg_ref, w_ref, mg_ref, oa_ref, lse_ref):
        la, lb, lc = l0[...], l1[...], l2[...]
        mx = jnp.maximum(jnp.maximum(la, lb), lc)
        ea, eb, ec = jnp.exp(la - mx), jnp.exp(lb - mx), jnp.exp(lc - mx)
        den = ea + eb + ec
        oa = (ea * o0[...] + eb * o1[...] + ec * o2[...]) / den
        oa_ref[...] = oa
        lse_ref[...] = mx + jnp.log(den)
        ya = _dot(oa.astype(BF), w_ref[0:OUT_A, :])
        yb = _dot(ob_ref[...].astype(BF), w_ref[OUT_A:OUT_A + W_B, :])
        bg = bg_ref[...]
        g_a = jax.nn.sigmoid(ga_ref[...].astype(F32) + bg[:, 0:D])
        g_b = jax.nn.sigmoid(gb_ref[...].astype(F32) + bg[:, D:2 * D])
        mg_ref[...] = (g_a * ya + g_b * yb).astype(BF)

    nar = pl.BlockSpec((tm, OUT_A), lambda i: (i, 0))
    wide = pl.BlockSpec((tm, D), lambda i: (i, 0))
    return pl.pallas_call(
        body, name="merge_fwd",
        out_shape=(jax.ShapeDtypeStruct((S, D), BF), jax.ShapeDtypeStruct((S, OUT_A), F32),
                   jax.ShapeDtypeStruct((S, OUT_A), F32)),
        grid=(S // tm,),
        in_specs=[nar] * 7 + [pl.BlockSpec((tm, D), lambda i: (i, gcol)),
                              pl.BlockSpec((tm, D), lambda i: (i, gcol + 1)),
                              pl.BlockSpec((1, 2 * D), lambda i: (0, 0)),
                              pl.BlockSpec((OUT_A + W_B, D), lambda i: (0, 0))],
        out_specs=(wide, nar, nar),
        compiler_params=_params("parallel"),
    )(*o_g, *l_g, o_b, proj, proj, b_gate.reshape(1, 2 * D), w_br)


def _merge_wo(o_g, l_g, o_b, proj, b_gate, w_br, w_o, x, name, comm=()):
    S = o_b.shape[0]
    D = D_MODEL
    tm = _pick(S, 256)
    gcol = GATE_OFF // D

    def body(o0, o1, o2, l0, l1, l2, ob_ref, ga_ref, gb_ref, bg_ref, w_ref, wo_ref, x_ref,
             x1_ref, mg_ref, oa_ref, lse_ref):
        la, lb, lc = l0[...], l1[...], l2[...]
        mx = jnp.maximum(jnp.maximum(la, lb), lc)
        ea, eb, ec = jnp.exp(la - mx), jnp.exp(lb - mx), jnp.exp(lc - mx)
        den = ea + eb + ec
        oa = (ea * o0[...] + eb * o1[...] + ec * o2[...]) / den
        oa_ref[...] = oa
        lse_ref[...] = mx + jnp.log(den)
        oa16 = oa.astype(BF)
        ob16 = ob_ref[...].astype(BF)
        acc = x_ref[...]
        for c0 in range(0, D, FFN_CHUNK):
            cs = slice(c0, c0 + FFN_CHUNK)
            ya = _dot(oa16, w_ref[0:OUT_A, cs])
            yb = _dot(ob16, w_ref[OUT_A:OUT_A + W_B, cs])
            g_a = jax.nn.sigmoid(ga_ref[:, cs].astype(F32) + bg_ref[:, cs])
            g_b = jax.nn.sigmoid(gb_ref[:, cs].astype(F32) + bg_ref[:, D + c0:D + c0 + FFN_CHUNK])
            mg = (g_a * ya + g_b * yb).astype(BF)
            mg_ref[:, cs] = mg
            acc = acc + _dot(mg, wo_ref[cs, :])
        x1_ref[...] = acc

    nar = pl.BlockSpec((tm, OUT_A), lambda i: (i, 0))
    wide = pl.BlockSpec((tm, D), lambda i: (i, 0))
    res = _call(body, name=name, grid=(S // tm,),
                in_specs=[nar] * 7 + [pl.BlockSpec((tm, D), lambda i: (i, gcol)),
                                      pl.BlockSpec((tm, D), lambda i: (i, gcol + 1)),
                                      pl.BlockSpec((1, 2 * D), lambda i: (0, 0)),
                                      pl.BlockSpec((OUT_A + W_B, D), lambda i: (0, 0)),
                                      pl.BlockSpec((D, D), lambda i: (0, 0)), wide],
                out_specs=(wide, wide, nar, nar),
                out_shape=(jax.ShapeDtypeStruct((S, D), F32), jax.ShapeDtypeStruct((S, D), BF),
                           jax.ShapeDtypeStruct((S, OUT_A), F32), jax.ShapeDtypeStruct((S, OUT_A), F32)),
                args=(*o_g, *l_g, o_b, proj, proj, b_gate.reshape(1, 2 * D), w_br, w_o, x), comm=comm)
    return res[0], res[1], res[2], res[3], list(res[4:])


STAT_OFF = HEAD_DIM // 2


def _merge_bwd(dx, w_o, oa, ob, lse, proj, b_gate, w_br):
    S = ob.shape[0]
    D = D_MODEL
    tm = _pick(S, 256)
    gcol = GATE_OFF // D

    def body(dx_ref, wo_ref, oa_ref, ob_ref, l_ref, ga_ref, gb_ref, bg_ref, w_ref,
             dya_ref, dyb_ref, doa_ref, dob_ref, st_ref, dg_ref, dbg_ref):
        @pl.when(pl.program_id(0) == 0)
        def _():
            dbg_ref[...] = jnp.zeros_like(dbg_ref)

        dx16 = dx_ref[...].astype(BF)
        oa = oa_ref[...]
        oa16 = oa.astype(BF)
        ob16 = ob_ref[...].astype(BF)
        doa = jnp.zeros((tm, OUT_A), F32)
        dob = jnp.zeros((tm, W_B), F32)
        for c0 in range(0, D, FFN_CHUNK):
            cs = slice(c0, c0 + FFN_CHUNK)
            cs2 = slice(D + c0, D + c0 + FFN_CHUNK)
            wa = w_ref[0:OUT_A, cs]
            wb = w_ref[OUT_A:OUT_A + W_B, cs]
            dm = _dot(dx16, wo_ref[cs, :], _NT)
            ya = _dot(oa16, wa)
            yb = _dot(ob16, wb)
            g_a = jax.nn.sigmoid(ga_ref[:, cs].astype(F32) + bg_ref[:, cs])
            g_b = jax.nn.sigmoid(gb_ref[:, cs].astype(F32) + bg_ref[:, cs2])
            dga = dm * ya * g_a * (1.0 - g_a)
            dgb = dm * yb * g_b * (1.0 - g_b)
            dg_ref[:, cs] = dga.astype(BF)
            dg_ref[:, cs2] = dgb.astype(BF)
            dbg_ref[:, cs] += jnp.sum(dga, axis=0, keepdims=True)
            dbg_ref[:, cs2] += jnp.sum(dgb, axis=0, keepdims=True)
            dya = (dm * g_a).astype(BF)
            dyb = (dm * g_b).astype(BF)
            dya_ref[:, cs] = dya
            dyb_ref[:, cs] = dyb
            doa = doa + _dot(dya, wa, _NT)
            dob = dob + _dot(dyb, wb, _NT)
        doa = doa.astype(BF)
        doa_ref[...] = doa
        dob_ref[...] = dob.astype(BF)
        r = lax.broadcasted_iota(jnp.int32, (OUT_A, OUT_A), 0) // HEAD_DIM
        c = lax.broadcasted_iota(jnp.int32, (OUT_A, OUT_A), 1) // HEAD_DIM
        dsum = _split_dot(doa.astype(F32) * oa, (r == c).astype(BF))
        lane = lax.broadcasted_iota(jnp.int32, dsum.shape, 1) % HEAD_DIM
        st_ref[...] = jnp.where(lane < STAT_OFF, l_ref[...], dsum)

    nar = pl.BlockSpec((tm, OUT_A), lambda i: (i, 0))
    wide = pl.BlockSpec((tm, D), lambda i: (i, 0))
    wide2 = pl.BlockSpec((tm, 2 * D), lambda i: (i, 0))
    vec2 = pl.BlockSpec((1, 2 * D), lambda i: (0, 0))
    return pl.pallas_call(
        body, name="merge_bwd",
        out_shape=(jax.ShapeDtypeStruct((S, D), BF), jax.ShapeDtypeStruct((S, D), BF),
                   jax.ShapeDtypeStruct((S, OUT_A), BF), jax.ShapeDtypeStruct((S, W_B), BF),
                   jax.ShapeDtypeStruct((S, OUT_A), F32), jax.ShapeDtypeStruct((S, 2 * D), BF),
                   jax.ShapeDtypeStruct((1, 2 * D), F32)),
        grid=(S // tm,),
        in_specs=[wide, pl.BlockSpec((D, D), lambda i: (0, 0)), nar, nar, nar,
                  pl.BlockSpec((tm, D), lambda i: (i, gcol)), pl.BlockSpec((tm, D), lambda i: (i, gcol + 1)), vec2,
                  pl.BlockSpec((OUT_A + W_B, D), lambda i: (0, 0))],
        out_specs=(wide, wide, nar, nar, nar, wide2, vec2),
        compiler_params=_params("arbitrary"),
    )(dx, w_o, oa, ob, lse, proj, proj, b_gate.reshape(1, 2 * D), w_br)


_SQRT_HALF = 0.7071067811865476
_INV_SQRT_2PI = 0.3989422804014327


def _gelu_parts(a):
    cdf = 0.5 * (1.0 + lax.erf(a * _SQRT_HALF))
    pdf = _INV_SQRT_2PI * jnp.exp(-0.5 * a * a)
    return cdf, pdf


def _shift_down(a, halo, k):
    rows = lax.broadcasted_iota(jnp.int32, a.shape, 0)
    out = pltpu.roll(a, k, 0)
    for r in range(k):
        out = jnp.where(rows == r, halo[8 - k + r:8 - k + r + 1, :], out)
    return out


def _shift_up(a, halo, k):
    n = a.shape[0]
    rows = lax.broadcasted_iota(jnp.int32, a.shape, 0)
    out = pltpu.roll(a, n - k, 0)
    for r in range(k):
        out = jnp.where(rows == n - k + r, halo[r:r + 1, :], out)
    return out


def _conv_in(a_ref, h_ref, first):
    a = a_ref[...].astype(F32)
    halo = jnp.where(first, 0.0, h_ref[...].astype(F32))
    return a, _shift_down(a, halo, 1), _shift_down(a, halo, 2)


def _ffn_specs(S, tm):
    F = D_FF
    t8 = tm // 8
    a_spec = pl.BlockSpec((tm, F), lambda i: (i, 0))
    v_spec = pl.BlockSpec((tm, F), lambda i: (i, 1))
    halo_prev = pl.BlockSpec((8, F), lambda i: (jnp.maximum(i * t8 - 1, 0), 0))
    return a_spec, v_spec, halo_prev


def _ffn_act_fwd(up, conv_w, conv_b):
    S = up.shape[0]
    F = D_FF
    tm = _pick(S, 256)
    a_spec, v_spec, halo_prev = _ffn_specs(S, tm)

    def body(a_ref, h_ref, v_ref, w_ref, b_ref, act_ref):
        a0, a1, a2 = _conv_in(a_ref, h_ref, pl.program_id(0) == 0)
        w = w_ref[...]
        ac = b_ref[...] + w[0:1, :] * a2 + w[1:2, :] * a1 + w[2:3, :] * a0
        cdf, _ = _gelu_parts(ac)
        act_ref[...] = (ac * cdf * v_ref[...].astype(F32)).astype(BF)

    return pl.pallas_call(
        body, name="ffn_act_fwd",
        out_shape=jax.ShapeDtypeStruct((S, F), BF),
        grid=(S // tm,),
        in_specs=[a_spec, halo_prev, v_spec, pl.BlockSpec((3, F), lambda i: (0, 0)),
                  pl.BlockSpec((1, F), lambda i: (0, 0))],
        out_specs=a_spec,
        compiler_params=_params("parallel"),
    )(up, up, up, conv_w, conv_b.reshape(1, F))


def _ffn_down(up, conv_w, conv_b, w_down, res, *, name, comm=()):
    S = up.shape[0]
    F = D_FF
    D = w_down.shape[1]
    tm = _pick(S, 256)
    a_spec, v_spec, halo_prev = _ffn_specs(S, tm)

    def body(a_ref, h_ref, v_ref, w_ref, b_ref, wd_ref, r_ref, o_ref, act_ref, ac_ref):
        first = pl.program_id(0) == 0
        acc = r_ref[...]
        for c0 in range(0, F, FFN_CHUNK):
            cs = slice(c0, c0 + FFN_CHUNK)
            a = a_ref[:, cs].astype(F32)
            halo = jnp.where(first, 0.0, h_ref[:, cs].astype(F32))
            w = w_ref[:, cs]
            ac = b_ref[:, cs] + w[0:1, :] * _shift_down(a, halo, 2) + w[1:2, :] * _shift_down(a, halo, 1) + w[2:3, :] * a
            ac_ref[:, cs] = ac.astype(BF)
            cdf, _ = _gelu_parts(ac)
            act = (ac * cdf * v_ref[:, cs].astype(F32)).astype(BF)
            act_ref[:, cs] = act
            acc = acc + _dot(act, wd_ref[cs, :])
        o_ref[...] = acc

    row = pl.BlockSpec((tm, D), lambda i: (i, 0))
    res_ = _call(body, name=name, grid=(S // tm,),
                 in_specs=[a_spec, halo_prev, v_spec, pl.BlockSpec((3, F), lambda i: (0, 0)),
                           pl.BlockSpec((1, F), lambda i: (0, 0)), pl.BlockSpec((F, D), lambda i: (0, 0)), row],
                 out_specs=(row, a_spec, a_spec),
                 out_shape=(jax.ShapeDtypeStruct((S, D), F32), jax.ShapeDtypeStruct((S, F), BF),
                            jax.ShapeDtypeStruct((S, F), BF)),
                 args=(up, up, up, conv_w, conv_b.reshape(1, F), w_down, res), comm=comm)
    return res_[0], res_[1], res_[2], list(res_[3:])


def _shift_up_pair(a, nxt):
    n = a.shape[0]
    r8 = lax.broadcasted_iota(jnp.int32, (8,) + a.shape[1:], 0)
    out = []
    for k in (1, 2):
        rolled = pltpu.roll(a, n - k, 0)
        tail = jnp.where(r8 >= 8 - k, pltpu.roll(nxt, 8 - k, 0), rolled[n - 8:n])
        out.append(jnp.concatenate([rolled[0:n - 8], tail], axis=0))
    return out


def _ffn_bwd(dx, w_down, up, ac, conv_w, comm=()):
    S = up.shape[0]
    F = D_FF
    D = dx.shape[1]
    tm = _pick(S, 256)
    t8 = tm // 8
    nt = S // tm
    a_spec, v_spec, _ = _ffn_specs(S, tm)

    def nxt(width, col):
        return pl.BlockSpec((8, width), lambda i: (jnp.minimum((i + 1) * t8, S // 8 - 1), col))

    def body(dx_ref, dxn_ref, wd_ref, ac_ref, acn_ref, a_ref, v_ref, vn_ref, w_ref, dup_ref, dw_ref, db_ref):
        i = pl.program_id(0)

        @pl.when(i == 0)
        def _():
            dw_ref[...] = jnp.zeros_like(dw_ref)
            db_ref[...] = jnp.zeros_like(db_ref)

        dx16 = dx_ref[...].astype(BF)
        dxn16 = dxn_ref[...].astype(BF)
        last = i == nt - 1

        def dconv(dact, ac, v):
            cdf, pdf = _gelu_parts(ac)
            return cdf, dact * v * (cdf + ac * pdf)

        for c0 in range(0, F, FFN_CHUNK):
            cs = slice(c0, c0 + FFN_CHUNK)
            wd = wd_ref[cs, :]
            dact = _dot(dx16, wd, _NT)
            ac = ac_ref[:, cs].astype(F32)
            cdf, dac = dconv(dact, ac, v_ref[:, cs].astype(F32))
            dup_ref[:, F + c0:F + c0 + FFN_CHUNK] = (dact * ac * cdf).astype(BF)
            _, dac_n = dconv(_dot(dxn16, wd, _NT), acn_ref[:, cs].astype(F32), vn_ref[:, cs].astype(F32))
            d1, d2 = _shift_up_pair(dac, jnp.where(last, 0.0, dac_n))
            w = w_ref[:, cs]
            dup_ref[:, cs] = (w[2:3, :] * dac + w[1:2, :] * d1 + w[0:1, :] * d2).astype(BF)
            a = a_ref[:, cs].astype(F32)
            db_ref[:, cs] += jnp.sum(dac, axis=0, keepdims=True)
            dw_ref[0:1, cs] += jnp.sum(d2 * a, axis=0, keepdims=True)
            dw_ref[1:2, cs] += jnp.sum(d1 * a, axis=0, keepdims=True)
            dw_ref[2:3, cs] += jnp.sum(dac * a, axis=0, keepdims=True)

    w_spec = pl.BlockSpec((3, F), lambda i: (0, 0))
    b_spec = pl.BlockSpec((1, F), lambda i: (0, 0))
    res = _call(body, name="ffn_bwd", grid=(nt,),
                in_specs=[pl.BlockSpec((tm, D), lambda i: (i, 0)), nxt(D, 0), pl.BlockSpec((F, D), lambda i: (0, 0)),
                          a_spec, nxt(F, 0), a_spec, v_spec, nxt(F, 1), w_spec],
                out_specs=(pl.BlockSpec((tm, 2 * F), lambda i: (i, 0)), w_spec, b_spec),
                out_shape=(jax.ShapeDtypeStruct((S, 2 * F), BF), jax.ShapeDtypeStruct((3, F), F32),
                           jax.ShapeDtypeStruct((1, F), F32)),
                args=(dx, dx, w_down, ac, ac, up, up, up, conv_w), comm=comm)
    return res[0], res[1], res[2], list(res[3:])


def _adamw(parts, w, m, v, name, row0=0, prev=None):
    R, C = w.shape
    Rp = parts.shape[1]
    tr = Rp
    for cand in (512, 256, 128, 64, 32, 16):
        if Rp % cand == 0 and row0 % cand == 0 and cand * C * 4 <= (1 << 21):
            tr = cand
            break
    b0 = row0 // tr
    c1 = 1.0 / (1.0 - ADAM_B1 ** ADAM_STEP)
    c2 = 1.0 / (1.0 - ADAM_B2 ** ADAM_STEP)

    def body(p_ref, w_ref, m_ref, v_ref, *rest):
        g_ref, d_ref, nm_ref, nv_ref = rest[-4:]
        g = p_ref[0].astype(F32)
        for j in range(1, N_DEV):
            g = g + p_ref[j].astype(F32)
        nm = ADAM_B1 * m_ref[...] + (1.0 - ADAM_B1) * g
        nv = ADAM_B2 * v_ref[...] + (1.0 - ADAM_B2) * (g * g)
        g_ref[...] = g
        nm_ref[...] = nm
        nv_ref[...] = nv
        d_ref[...] = -ADAM_LR * ((nm * c1) / (jnp.sqrt(nv * c2) + ADAM_EPS) + ADAM_WD * w_ref[...])

    blk = pl.BlockSpec((tr, C), lambda i: (b0 + i, 0))
    out = jax.ShapeDtypeStruct((R, C), F32)
    carried = [] if prev is None else list(prev)
    return pl.pallas_call(
        body, name=name,
        out_shape=(out, out, out, out),
        grid=(Rp // tr,),
        in_specs=[pl.BlockSpec((N_DEV, tr, C), lambda i: (0, i, 0)), blk, blk, blk]
        + [pl.BlockSpec(memory_space=pl.ANY)] * len(carried),
        out_specs=(blk, blk, blk, blk),
        input_output_aliases={4 + k: k for k in range(len(carried))},
        compiler_params=_params("parallel"),
    )(parts, w, m, v, *carried)


def _dil(t, dil):
    S, C = t.shape
    if dil == 1:
        return t.reshape(1, S, C)
    return t.reshape(S // dil, dil, C).transpose(1, 0, 2)


def _undil(t):
    d, L, C = t.shape
    if d == 1:
        return t.reshape(L, C)
    return t.transpose(1, 0, 2).reshape(L * d, C)


def _group_qkv(proj, g):
    dil = DSW_GROUPS[g][1]
    if dil == 1:
        p3 = _dil(proj, 1)
        return (p3, p3, p3), (g, W_A // OUT_A + g, 2 * W_A // OUT_A + g)
    c0 = g * OUT_A
    return tuple(_dil(proj[:, o + c0:o + c0 + OUT_A], dil) for o in (0, W_A, 2 * W_A)), (0, 0, 0)


_COL_SHARDED = ("w_in", "w_br", "w_up")
_ROW_SHARDED = ("w_o", "w_down")


class _Plan:
    def __init__(self):
        self.riders = {}
        self.landed = {}

    def ride(self, slot, key, kind, x):
        self.riders.setdefault(slot, []).append((key, kind, x))

    def run(self, slot, fn, *args, **kw):
        items = self.riders.pop(slot, [])
        res = fn(*args, comm=[(kind, x) for _, kind, x in items], **kw)
        for (key, _, _), r in zip(items, res[-1]):
            self.landed[key] = r
        return res[0] if len(res) == 2 else res[:-1]

    def weight(self, n, l):
        g = self.landed[(n, l)]
        if n in _COL_SHARDED:
            return g.transpose(1, 0, 2).reshape(g.shape[1], -1)
        return g.reshape(-1, g.shape[2])

    def scatter(self, slot, n, l, full, part=0):
        K, N = full.shape
        if n in _COL_SHARDED:
            blocks = full.reshape(K, N_DEV, N // N_DEV).transpose(1, 0, 2)
        else:
            blocks = full.reshape(N_DEV, K // N_DEV, N)
        self.ride(slot, ("d" + n, l, part), "scatter", blocks)


def _layer_fwd(x, p, plan, l):
    h = _rms_fwd(x, p["norm1"][l], f"rms1_fwd_{l}")
    proj = plan.run(f"proj_{l}", _matmul, h, plan.weight("w_in", l), mode="nn", out_dtype=BF, name=f"proj_{l}",
                    tm=1024, tn=1024, tk=1024)
    o_g, l_g, qkv_g = [], [], []
    for g in range(N_GROUPS):
        qkv, cols = _group_qkv(proj, g)
        og, lg = plan.run(f"attn_a_fwd_g{g}_{l}", _attn_a_fwd, qkv, cols, g)
        o_g.append(_undil(og))
        l_g.append(_undil(lg))
        qkv_g.append((qkv, cols))
    ob = plan.run(f"sb_fwd_{l}", _sb_fwd, proj)
    x1, merged, oa, lse = plan.run(f"wo_{l}", _merge_wo, o_g, l_g, ob, proj, p["b_gate"][l], plan.weight("w_br", l),
                                   plan.weight("w_o", l), x, f"wo_{l}")
    h2 = _rms_fwd(x1, p["norm2"][l], f"rms2_fwd_{l}")
    up = plan.run(f"up_{l}", _matmul, h2, plan.weight("w_up", l), mode="nn", out_dtype=BF, name=f"up_{l}",
                  tm=1024, tn=1408, tk=1024)
    x2, act, ac = plan.run(f"down_{l}", _ffn_down, up, p["conv_w"][l], p["conv_b"][l], plan.weight("w_down", l), x1,
                           name=f"down_{l}")
    saved = dict(x=x, h=h, proj=proj, qkv_g=qkv_g, oa=oa, ob=ob, lse=lse, merged=merged, x1=x1, h2=h2, up=up, act=act, ac=ac)
    return x2, saved


def _layer_bwd(dx2, sv, p, plan, l):
    gr = {}
    dwd = plan.run(f"dw_down_{l}", _matmul, sv["act"], dx2, mode="tn", out_dtype=BF, name=f"dw_down_{l}",
                   tm=1408, tn=1024, tk=2048)
    plan.scatter(f"d_h2_{l}", "w_down", l, dwd)
    dup, gr["conv_w"], dcb = plan.run(f"ffn_bwd_{l}", _ffn_bwd, dx2, plan.weight("w_down", l), sv["up"], sv["ac"],
                                      p["conv_w"][l])
    gr["conv_b"] = dcb[0]
    dx1, dn2 = plan.run(f"d_h2_{l}", _matmul_rms_bwd, dup, plan.weight("w_up", l), sv["x1"], p["norm2"][l], dx2,
                        name=f"d_h2_{l}")
    dwu = plan.run(f"dw_up_{l}", _matmul, sv["h2"], dup, mode="tn", out_dtype=BF, name=f"dw_up_{l}",
                   tm=1024, tn=1408, tk=2048)
    plan.scatter(f"sb_bwd_{l}", "w_up", l, dwu)
    gr["norm2"] = dn2[0]
    dwo = plan.run(f"dw_o_{l}", _matmul, sv["merged"], dx1, mode="tn", out_dtype=BF, name=f"dw_o_{l}",
                   tm=1024, tn=1024, tk=2048)
    plan.scatter(f"dw_in_{l}", "w_o", l, dwo)
    dya, dyb, doa, dob, stats, dgate, dbg = _merge_bwd(dx1, plan.weight("w_o", l), sv["oa"], sv["ob"], sv["lse"],
                                                       sv["proj"], p["b_gate"][l], plan.weight("w_br", l))
    gr["b_gate"] = dbg[0]
    dwa = plan.run(f"dw_bra_{l}", _matmul, sv["oa"], dya, mode="tn", out_dtype=BF, name=f"dw_bra_{l}",
                   tm=256, tn=1024, tk=2048)
    dwb = plan.run(f"dw_brb_{l}", _matmul, sv["ob"], dyb, mode="tn", out_dtype=BF, name=f"dw_brb_{l}",
                   tm=256, tn=1024, tk=2048)
    plan.scatter(f"dw_in_{l}", "w_br", l, jnp.concatenate([dwa, dwb], axis=0))
    proj = sv["proj"]
    dq_a, dk_a, dv_a = [], [], []
    for g, (_, dil) in enumerate(DSW_GROUPS):
        qkv, cols = sv["qkv_g"][g]
        dqg, dkg, dvg = _attn_a_bwd(qkv, cols, _dil(doa, dil), _dil(stats, dil), g)
        dq_a.append(_undil(dqg))
        dk_a.append(_undil(dkg))
        dv_a.append(_undil(dvg))
    dqb, dkb, dvb = plan.run(f"sb_bwd_{l}", _sb_bwd, proj, dob, sv["ob"])
    dproj = jnp.concatenate(dq_a + dk_a + dv_a + [dqb.astype(BF), dkb.astype(BF), dvb.astype(BF), dgate], axis=1)
    dx, dn1 = plan.run(f"d_h_{l}", _matmul_rms_bwd, dproj, plan.weight("w_in", l), sv["x"], p["norm1"][l], dx1,
                       name=f"d_h_{l}")
    if l > 0:
        dwi = plan.run(f"dw_in_{l}", _matmul, sv["h"], dproj, mode="tn", out_dtype=BF, name=f"dw_in_{l}",
                       tm=1024, tn=1280, tk=2048)
        plan.scatter(f"ffn_bwd_{l - 1}", "w_in", l, dwi)
    else:
        half = D_MODEL // 2
        for part, slab in enumerate((sv["h"][:, :half], sv["h"][:, half:])):
            name = f"dw_in_{l}" if part == 0 else f"dw_in_{l}_rest"
            dwi = plan.run(name, _matmul, slab, dproj, mode="tn", out_dtype=BF, name=name, tm=half, tn=1280, tk=2048)
            plan.scatter(f"dw_in_{l}_rest" if part == 0 else "alone", "w_in", l, dwi, part)
    gr["norm1"] = dn1[0]
    return dx, gr


def kernel(x, norm1, w_in, b_gate, w_br, w_o, norm2, w_up, conv_w, conv_b, w_down, norm_f, loss_target, m_norm1, m_w_in, m_b_gate, m_w_br, m_w_o, m_norm2, m_w_up, m_conv_w, m_conv_b, m_w_down, m_norm_f, v_norm1, v_w_in, v_b_gate, v_w_br, v_w_o, v_norm2, v_w_up, v_conv_w, v_conv_b, v_w_down, v_norm_f):
    depth = norm1.shape[0]
    me = 4 * lax.axis_index("x") + 2 * lax.axis_index("y") + lax.axis_index("c")
    shards = dict(w_in=w_in, w_br=w_br, w_o=w_o, w_up=w_up, w_down=w_down)
    moments_m = dict(norm1=m_norm1, w_in=m_w_in, b_gate=m_b_gate, w_br=m_w_br, w_o=m_w_o, norm2=m_norm2,
                     w_up=m_w_up, conv_w=m_conv_w, conv_b=m_conv_b, w_down=m_w_down, norm_f=m_norm_f)
    moments_v = dict(norm1=v_norm1, w_in=v_w_in, b_gate=v_b_gate, w_br=v_w_br, w_o=v_w_o, norm2=v_norm2,
                     w_up=v_w_up, conv_w=v_conv_w, conv_b=v_conv_b, w_down=v_w_down, norm_f=v_norm_f)

    plan = _Plan()
    wb = {n: s.astype(BF) for n, s in shards.items()}
    p = dict(norm1=norm1, b_gate=b_gate, norm2=norm2, conv_b=conv_b)
    cw = _all_gather(conv_w, "gather_conv_w")
    p["conv_w"] = cw.transpose(1, 2, 0, 3).reshape(depth, 3, D_FF)
    plan.landed[("w_in", 0)] = _all_gather_via_sibling(wb["w_in"][0], "gather_w_in_0")
    for l in range(depth):
        plan.ride(f"proj_{l}", ("w_down", l), "gather", wb["w_down"][l])
        plan.ride(f"attn_a_fwd_g0_{l}" if l == 0 else f"down_{l - 1}", ("w_br", l), "gather", wb["w_br"][l])
        plan.ride(f"attn_a_fwd_g0_{l}" if l == 0 else f"down_{l - 1}", ("w_o", l), "gather", wb["w_o"][l])
        plan.ride(f"sb_fwd_{l}", ("w_up", l), "gather", wb["w_up"][l])
        if l + 1 < depth:
            plan.ride(f"up_{l}", ("w_in", l + 1), "gather", wb["w_in"][l + 1])

    xs = x[0]
    saved = []
    for l in range(depth):
        xs, sv = _layer_fwd(xs, p, plan, l)
        saved.append(sv)
    loss_part, dx, dnf = _loss_head(xs, norm_f, loss_target[0])
    loss = lax.psum(loss_part[0, 0], ("x", "y", "c"))

    grads = [None] * depth
    for l in reversed(range(depth)):
        dx, grads[l] = _layer_bwd(dx, saved[l], p, plan, l)
    grad_x = dx[None]
    (key, _, last), = plan.riders.pop("alone")
    plan.landed[key] = _all_to_all(last, "scatter_w_in_rest")
    assert not plan.riders, sorted(plan.riders)

    out_g, out_d, out_m, out_v = {}, {}, {}, {}
    for n in _COL_SHARDED + _ROW_SHARDED:
        shp = shards[n].shape
        flat = (shp[0] * shp[1], shp[2])
        res = None
        for l in range(depth):
            row = l * shp[1]
            for key in sorted(k for k in plan.landed if k[:2] == ("d" + n, l)):
                parts = plan.landed[key]
                res = _adamw(parts, shards[n].reshape(flat), moments_m[n].reshape(flat), moments_v[n].reshape(flat),
                             f"adamw_{n}_{l}_{key[2]}", row0=row, prev=res)
                row += parts.shape[1]
        out_g[n], out_d[n], out_m[n], out_v[n] = [r.reshape(shp) for r in res]

    small = ("norm1", "b_gate", "norm2", "conv_b")
    vecs = [jnp.stack([grads[l][n] for l in range(depth)]).reshape(-1) for n in small]
    vecs.append(dnf.reshape(-1))
    vecs.append(jnp.stack([grads[l]["conv_w"] for l in range(depth)]).reshape(-1))
    sizes = [v.shape[0] for v in vecs]
    flat = jnp.concatenate(vecs)
    n_small = sum(sizes[:-1])
    pad = (-flat.shape[0]) % 1024
    flat = jnp.pad(flat, (0, pad)).reshape(-1, 128)
    allp = _all_gather(flat, "gather_small_grads").reshape(N_DEV, -1)
    rep_w = jnp.concatenate([norm1.reshape(-1), b_gate.reshape(-1), norm2.reshape(-1), conv_b.reshape(-1), norm_f])
    rep_m = jnp.concatenate([moments_m[n].reshape(-1) for n in small] + [m_norm_f])
    rep_v = jnp.concatenate([moments_v[n].reshape(-1) for n in small] + [v_norm_f])
    rows = n_small // 128
    res = _adamw(allp[:, :n_small].reshape(N_DEV, rows, 128), rep_w.reshape(rows, 128), rep_m.reshape(rows, 128),
                 rep_v.reshape(rows, 128), "adamw_small")
    off = 0
    for n, sz in zip(small + ("norm_f",), sizes[:-1]):
        shp = norm_f.shape if n == "norm_f" else p[n].shape
        out_g[n], out_d[n], out_m[n], out_v[n] = [r.reshape(-1)[off:off + sz].reshape(shp) for r in res]
        off += sz
    f = conv_w.shape[2]
    cwp = allp[:, n_small:n_small + sizes[-1]].reshape(N_DEV, depth * 3, D_FF)
    cwp = lax.dynamic_slice_in_dim(cwp, me * f, f, axis=2)
    res = _adamw(cwp, conv_w.reshape(depth * 3, f), m_conv_w.reshape(depth * 3, f), v_conv_w.reshape(depth * 3, f),
                 "adamw_conv_w")
    out_g["conv_w"], out_d["conv_w"], out_m["conv_w"], out_v["conv_w"] = [r.reshape(conv_w.shape) for r in res]

    order = ("norm1", "w_in", "b_gate", "w_br", "w_o", "norm2", "w_up", "conv_w", "conv_b", "w_down", "norm_f")
    return (loss, grad_x, *[out_g[n] for n in order], *[out_d[n] for n in order],
            *[out_m[n] for n in order], *[out_v[n] for n in order])
```

```python
import functools

import jax
import jax.numpy as jnp
from jax import lax
from jax.experimental import pallas as pl
from jax.experimental.pallas import tpu as pltpu

BF = jnp.bfloat16
F32 = jnp.float32

N_DEV = 8
D_MODEL = 1024
HEAD_DIM = 64
DSW_GROUPS = ((128, 1), (512, 4), (2048, 16))
HEADS_PER_GROUP = 4
N_GROUPS = len(DSW_GROUPS)
DSW_HEADS = HEADS_PER_GROUP * N_GROUPS
SB_HEADS = 4
W_A = DSW_HEADS * HEAD_DIM
W_B = SB_HEADS * HEAD_DIM
OUT_A = HEADS_PER_GROUP * HEAD_DIM
N_IN = 3 * W_A + 3 * W_B + 2 * D_MODEL
GATE_OFF = 3 * W_A + 3 * W_B
D_FF = 2816
SB_BLOCK = 256
RMS_EPS = 1e-6
ATT_SCALE = HEAD_DIM ** -0.5
NEG = -1e30
SB_EXIT = -110.0
FFN_CHUNK = 256

ADAM_LR = 0.001
ADAM_B1 = 0.9
ADAM_B2 = 0.999
ADAM_EPS = 1e-08
ADAM_WD = 0.01
ADAM_STEP = 10

HBM_SPEC = pl.BlockSpec(memory_space=pltpu.HBM)
MESH = pl.DeviceIdType.MESH

_NN = (((1,), (0,)), ((), ()))
_NT = (((1,), (1,)), ((), ()))
_TN = (((0,), (0,)), ((), ()))


def _dot(a, b, dn=_NN):
    return lax.dot_general(a, b, dn, preferred_element_type=F32)


def _pick(dim, pref):
    if dim <= pref:
        return dim
    t = (pref // 128) * 128
    while t >= 128:
        if dim % t == 0:
            return t
        t -= 128
    return dim


def _params(*sem):
    return pltpu.CompilerParams(dimension_semantics=sem)


def _peer(k):
    x, y, c = lax.axis_index("x"), lax.axis_index("y"), lax.axis_index("c")
    px = 1 - x if (k >> 2) & 1 else x
    py = 1 - y if (k >> 1) & 1 else y
    pc = 1 - c if k & 1 else c
    return (px, py, pc), 4 * px + 2 * py + pc


def _exchange(kind, x_ref, out_ref, send_sems, recv_sems, local_sem):
    gather = kind == "gather"
    _, me = _peer(0)

    def src(idx):
        return x_ref if gather else x_ref.at[idx]

    def copy(k, dst_idx):
        peer, pidx = _peer(k)
        return pltpu.make_async_remote_copy(
            src_ref=src(pidx), dst_ref=out_ref.at[dst_idx], send_sem=send_sems.at[k - 1],
            recv_sem=recv_sems.at[k - 1], device_id=peer, device_id_type=MESH)

    mine = pltpu.make_async_copy(src(me), out_ref.at[me], local_sem)

    def start():
        mine.start()
        for k in range(1, N_DEV):
            copy(k, me).start()

    def wait():
        for k in range(1, N_DEV):
            copy(k, _peer(k)[1]).wait_recv()
        for k in range(1, N_DEV):
            copy(k, me).wait_send()
        mine.wait()

    return start, wait


_EXCHANGE_SEMS = [pltpu.SemaphoreType.DMA((N_DEV - 1,)), pltpu.SemaphoreType.DMA((N_DEV - 1,)),
                  pltpu.SemaphoreType.DMA]


def _exchange_shape(kind, x):
    return jax.ShapeDtypeStruct(((N_DEV,) + x.shape) if kind == "gather" else x.shape, x.dtype)


def _exchange_alone(kind, x, name):
    def body(x_ref, out_ref, send_sems, recv_sems, local_sem):
        start, wait = _exchange(kind, x_ref, out_ref, send_sems, recv_sems, local_sem)
        start()
        wait()

    return pl.pallas_call(
        body, name=name, out_shape=_exchange_shape(kind, x),
        in_specs=[HBM_SPEC], out_specs=HBM_SPEC, scratch_shapes=list(_EXCHANGE_SEMS),
    )(x)


def _all_gather_via_sibling(x, name):
    def body(x_ref, out_ref, send_sems, recv_sems, local_sem):
        x_, y_, c_ = lax.axis_index("x"), lax.axis_index("y"), lax.axis_index("c")
        me, sibling = (x_, y_, c_), (x_, y_, 1 - c_)
        chips = [(1 - x_, y_), (x_, 1 - y_), (1 - x_, 1 - y_)]

        def slot(px, py, pc):
            return out_ref.at[4 * px + 2 * py + pc]

        def copy(k, block, to, src=None):
            return pltpu.make_async_remote_copy(
                src_ref=slot(*block) if src is None else src, dst_ref=slot(*block), send_sem=send_sems.at[k],
                recv_sem=recv_sems.at[k], device_id=to, device_id_type=MESH)

        mine = pltpu.make_async_copy(x_ref, slot(*me), local_sem)
        mine.start()
        first = [copy(0, me, sibling, src=x_ref)]
        first += [copy(1 + j, me, (*chip, c_), src=x_ref) for j, chip in enumerate(chips)]
        for cp in first:
            cp.start()
        passed = [copy(4 + j, (*chip, c_), sibling) for j, chip in enumerate(chips)]
        for j, chip in enumerate(chips):
            copy(1 + j, (*chip, c_), me).wait_recv()
            passed[j].start()
        copy(0, sibling, me).wait_recv()
        for j, chip in enumerate(chips):
            copy(4 + j, (*chip, 1 - c_), me).wait_recv()
        for cp in first + passed:
            cp.wait_send()
        mine.wait()

    return pl.pallas_call(
        body, name=name, out_shape=_exchange_shape("gather", x),
        in_specs=[HBM_SPEC], out_specs=HBM_SPEC,
        scratch_shapes=[pltpu.SemaphoreType.DMA((N_DEV - 1,)), pltpu.SemaphoreType.DMA((N_DEV - 1,)),
                        pltpu.SemaphoreType.DMA],
    )(x)


def _all_gather(x, name):
    return _exchange_alone("gather", x, name)


def _all_to_all(x, name):
    return _exchange_alone("scatter", x, name)


def _call(body, *, name, grid, in_specs, out_specs, out_shape, args, scratch_shapes=(), sem=None, comm=()):
    single = not isinstance(out_shape, (tuple, list))
    outs = (out_shape,) if single else tuple(out_shape)
    ospecs = (out_specs,) if single else tuple(out_specs)
    if not comm:
        res = pl.pallas_call(
            body, name=name, out_shape=outs, grid=grid, in_specs=list(in_specs), out_specs=ospecs,
            scratch_shapes=list(scratch_shapes), compiler_params=_params(*(sem or ("arbitrary",) * len(grid))),
        )(*args)
        return res
    n_in, n_out, n_scr, nc = len(in_specs), len(outs), len(scratch_shapes), len(comm)

    def wrapped(*refs):
        ins = refs[:n_in]
        cins = refs[n_in:n_in + nc]
        o0 = n_in + nc
        kouts = refs[o0:o0 + n_out]
        couts = refs[o0 + n_out:o0 + n_out + nc]
        s0 = o0 + n_out + nc
        scr = refs[s0:s0 + n_scr]
        sems = refs[s0 + n_scr:]
        ids = [pl.program_id(ax) for ax in range(len(grid))]
        first = functools.reduce(jnp.logical_and, [i == 0 for i in ids])
        last = functools.reduce(jnp.logical_and, [i == g - 1 for i, g in zip(ids, grid)])
        ex = [_exchange(comm[c][0], cins[c], couts[c], *sems[3 * c:3 * c + 3]) for c in range(nc)]

        @pl.when(first)
        def _():
            for start, _ in ex:
                start()

        body(*ins, *kouts, *scr)

        @pl.when(last)
        def _():
            for _, wait in ex:
                wait()

    return pl.pallas_call(
        wrapped, name=name,
        out_shape=outs + tuple(_exchange_shape(k, x) for k, x in comm),
        grid=grid, in_specs=list(in_specs) + [HBM_SPEC] * nc, out_specs=ospecs + (HBM_SPEC,) * nc,
        scratch_shapes=list(scratch_shapes) + list(_EXCHANGE_SEMS) * nc,
        compiler_params=_params(*(("arbitrary",) * len(grid))),
    )(*args, *[x for _, x in comm])


def _matmul(a, b, *, mode, out_dtype, name, tm=512, tn=1024, tk=1024, res=None, comm=()):
    if mode == "nn":
        (M, K), (_, N) = a.shape, b.shape
    elif mode == "nt":
        (M, K), (N, _) = a.shape, b.shape
    else:
        (K, M), (_, N) = a.shape, b.shape
    tm, tn, tk = _pick(M, tm), _pick(N, tn), _pick(K, tk)
    nk = K // tk
    dn = {"nn": _NN, "nt": _NT, "tn": _TN}[mode]

    def body(*refs):
        a_ref, b_ref = refs[0], refs[1]
        r_ref = refs[2] if res is not None else None
        o_ref = refs[3] if res is not None else refs[2]

        def finish(r):
            if res is not None:
                r = r + r_ref[...].astype(F32)
            o_ref[...] = r.astype(out_dtype)

        part = _dot(a_ref[...].astype(BF), b_ref[...].astype(BF), dn)
        if nk == 1:
            finish(part)
            return
        acc = refs[-1]
        k = pl.program_id(2)

        @pl.when(k == 0)
        def _():
            acc[...] = jnp.zeros_like(acc)

        acc[...] += part

        @pl.when(k == nk - 1)
        def _():
            finish(acc[...])

    if mode == "tn":
        a_spec = pl.BlockSpec((tk, tm), lambda j, i, k: (k, i))
    else:
        a_spec = pl.BlockSpec((tm, tk), lambda j, i, k: (i, k))
    if mode == "nt":
        b_spec = pl.BlockSpec((tn, tk), lambda j, i, k: (j, k))
    else:
        b_spec = pl.BlockSpec((tk, tn), lambda j, i, k: (k, j))
    o_spec = pl.BlockSpec((tm, tn), lambda j, i, k: (i, j))
    in_specs = [a_spec, b_spec] + ([o_spec] if res is not None else [])
    args = (a, b) + ((res,) if res is not None else ())
    out = _call(body, name=name, grid=(N // tn, M // tm, nk), in_specs=in_specs, out_specs=o_spec,
                out_shape=jax.ShapeDtypeStruct((M, N), out_dtype), args=args,
                scratch_shapes=[pltpu.VMEM((tm, tn), F32)] if nk > 1 else [],
                sem=("parallel", "parallel", "arbitrary"), comm=comm)
    return out[0], list(out[1:])


def _rms_fwd(x, g, name):
    S, D = x.shape
    tm = _pick(S, 512)

    def body(x_ref, g_ref, h_ref):
        xf = x_ref[...]
        r = lax.rsqrt(jnp.mean(xf * xf, axis=-1, keepdims=True) + RMS_EPS)
        h_ref[...] = (xf * r * g_ref[...]).astype(BF)

    return pl.pallas_call(
        body, name=name,
        out_shape=jax.ShapeDtypeStruct((S, D), BF),
        grid=(S // tm,),
        in_specs=[pl.BlockSpec((tm, D), lambda i: (i, 0)), pl.BlockSpec((1, D), lambda i: (0, 0))],
        out_specs=pl.BlockSpec((tm, D), lambda i: (i, 0)),
        compiler_params=_params("parallel"),
    )(x, g.reshape(1, D))


def _rms_bwd(x, g, dh, dres, name):
    S, D = x.shape
    tm = _pick(S, 512)

    def body(x_ref, g_ref, dh_ref, dres_ref, dx_ref, dg_ref):
        @pl.when(pl.program_id(0) == 0)
        def _():
            dg_ref[...] = jnp.zeros_like(dg_ref)

        xf = x_ref[...]
        r = lax.rsqrt(jnp.mean(xf * xf, axis=-1, keepdims=True) + RMS_EPS)
        xh = xf * r
        dy = dh_ref[...].astype(F32)
        dg_ref[...] += jnp.sum(dy * xh, axis=0, keepdims=True)
        dxh = dy * g_ref[...]
        dx = r * (dxh - xh * jnp.mean(dxh * xh, axis=-1, keepdims=True))
        dx_ref[...] = dres_ref[...] + dx

    row = pl.BlockSpec((tm, D), lambda i: (i, 0))
    vec = pl.BlockSpec((1, D), lambda i: (0, 0))
    return pl.pallas_call(
        body, name=name,
        out_shape=(jax.ShapeDtypeStruct((S, D), F32), jax.ShapeDtypeStruct((1, D), F32)),
        grid=(S // tm,),
        in_specs=[row, vec, row, row], out_specs=(row, vec),
        compiler_params=_params("arbitrary"),
    )(x, g.reshape(1, D), dh, dres)


def _matmul_rms_bwd(dy, w, x, g, dres, *, name, tm=512, comm=()):
    S, K = dy.shape
    D = w.shape[0]
    tm = _pick(S, tm)

    def body(dy_ref, w_ref, x_ref, g_ref, dres_ref, dx_ref, dg_ref):
        @pl.when(pl.program_id(0) == 0)
        def _():
            dg_ref[...] = jnp.zeros_like(dg_ref)

        dh = _dot(dy_ref[...].astype(BF), w_ref[...], _NT)
        xf = x_ref[...]
        r = lax.rsqrt(jnp.mean(xf * xf, axis=-1, keepdims=True) + RMS_EPS)
        xh = xf * r
        dg_ref[...] += jnp.sum(dh * xh, axis=0, keepdims=True)
        dxh = dh * g_ref[...]
        dx_ref[...] = dres_ref[...] + r * (dxh - xh * jnp.mean(dxh * xh, axis=-1, keepdims=True))

    row = pl.BlockSpec((tm, D), lambda i: (i, 0))
    vec = pl.BlockSpec((1, D), lambda i: (0, 0))
    res = _call(body, name=name, grid=(S // tm,),
                in_specs=[pl.BlockSpec((tm, K), lambda i: (i, 0)), pl.BlockSpec((D, K), lambda i: (0, 0)), row, vec, row],
                out_specs=(row, vec),
                out_shape=(jax.ShapeDtypeStruct((S, D), F32), jax.ShapeDtypeStruct((1, D), F32)),
                args=(dy, w, x, g.reshape(1, D), dres), comm=comm)
    return res[0], res[1], list(res[2:])


def _loss_head(x, g, target):
    S, D = x.shape
    tm = _pick(S, 512)

    def body(x_ref, g_ref, t_ref, loss_ref, dx_ref, dg_ref):
        @pl.when(pl.program_id(0) == 0)
        def _():
            dg_ref[...] = jnp.zeros_like(dg_ref)
            loss_ref[...] = jnp.zeros_like(loss_ref)

        xf = x_ref[...]
        gg = g_ref[...]
        r = lax.rsqrt(jnp.mean(xf * xf, axis=-1, keepdims=True) + RMS_EPS)
        xh = xf * r
        err = xh * gg - t_ref[...]
        per_tok = jnp.mean(err * err, axis=-1, keepdims=True)
        loss_ref[...] += 0.5 * jnp.sum(per_tok, axis=0, keepdims=True)
        dy = err * (1.0 / D)
        dg_ref[...] += jnp.sum(dy * xh, axis=0, keepdims=True)
        dxh = dy * gg
        dx_ref[...] = r * (dxh - xh * jnp.mean(dxh * xh, axis=-1, keepdims=True))

    row = pl.BlockSpec((tm, D), lambda i: (i, 0))
    vec = pl.BlockSpec((1, D), lambda i: (0, 0))
    one = pl.BlockSpec((1, 1), lambda i: (0, 0))
    return pl.pallas_call(
        body, name="loss_head",
        out_shape=(jax.ShapeDtypeStruct((1, 1), F32), jax.ShapeDtypeStruct((S, D), F32),
                   jax.ShapeDtypeStruct((1, D), F32)),
        grid=(S // tm,),
        in_specs=[row, vec, row], out_specs=(one, row, vec),
        compiler_params=_params("arbitrary"),
    )(x, g.reshape(1, D), target)


def _slopes(g):
    return [2.0 ** (-8.0 * (HEADS_PER_GROUP * g + j + 1) / DSW_HEADS) for j in range(HEADS_PER_GROUP)]


def _band_masks(W):
    row = lax.broadcasted_iota(jnp.int32, (W, W), 0)
    col = lax.broadcasted_iota(jnp.int32, (W, W), 1)
    d_cur = row - col
    d_prev = d_cur + W
    return d_cur, d_prev, d_cur >= 0, d_cur <= 0


def _band_specs(W, nb, per):
    def cur(c):
        return pl.BlockSpec((None, per * W, OUT_A), lambda r, n: (r, n, c))

    def prev(c):
        return pl.BlockSpec((None, W, OUT_A), lambda r, n: (r, jnp.maximum(per * n - 1, 0), c))

    def nxt(c):
        return pl.BlockSpec((None, W, OUT_A), lambda r, n: (r, jnp.minimum(per * (n + 1), nb - 1), c))

    return cur, prev, nxt


def _blocks_per_step(nb):
    return 4 if nb % 4 == 0 else 2 if nb % 2 == 0 else 1


def _head_stack(W):
    H, hd = HEADS_PER_GROUP, HEAD_DIM
    lane_head = lax.broadcasted_iota(jnp.int32, (W, OUT_A), 1) // hd

    def stack(x):
        return jnp.concatenate([jnp.where(lane_head == h, x, jnp.zeros_like(x)) for h in range(H)], axis=0)

    def unstack(y):
        out = jnp.where(lane_head == 0, y[0:W], 0.0)
        for h in range(1, H):
            out = jnp.where(lane_head == h, y[h * W:(h + 1) * W], out)
        return out

    def column(ref, rows, off=0):
        return jnp.concatenate([ref[rows, h * hd + off:h * hd + off + 1] for h in range(H)], axis=0)

    def tile(x):
        return jnp.concatenate([x] * H, axis=0)

    return stack, unstack, column, tile


def _stacked_bias(W, slopes, dil):
    d_cur, d_prev, m_cur, m_prev = _band_masks(W)
    b_cur = jnp.concatenate([(s * dil) * d_cur.astype(F32) for s in slopes], axis=0)
    b_prev = jnp.concatenate([(s * dil) * d_prev.astype(F32) for s in slopes], axis=0)
    H = len(slopes)
    return b_cur, b_prev, jnp.concatenate([m_cur] * H, axis=0), jnp.concatenate([m_prev] * H, axis=0)


def _attn_a_fwd(qkv, cols, g, comm=()):
    win, dil = DSW_GROUPS[g]
    W = win // dil
    d, L, _ = qkv[0].shape
    nb = L // W
    per = _blocks_per_step(nb)
    slopes = _slopes(g)

    def body(q_ref, kp_ref, kc_ref, vp_ref, vc_ref, o_ref, l_ref):
        n = pl.program_id(1)
        stack, unstack, _, _ = _head_stack(W)
        b_cur, b_prev, m_cur, m_prev = _stacked_bias(W, slopes, dil)
        m_first = jnp.logical_and(m_prev, n > 0)
        for b in range(per):
            rows = slice(b * W, (b + 1) * W)
            before = slice((b - 1) * W, b * W)
            qs = stack(q_ref[rows, :])
            kc, vc = kc_ref[rows, :], vc_ref[rows, :]
            kp, vp = (kp_ref[...], vp_ref[...]) if b == 0 else (kc_ref[before, :], vc_ref[before, :])
            s_c = jnp.where(m_cur, _dot(qs, kc, _NT) * ATT_SCALE - b_cur, NEG)
            s_p = jnp.where(m_first if b == 0 else m_prev, _dot(qs, kp, _NT) * ATT_SCALE - b_prev, NEG)
            m = jnp.maximum(jnp.max(s_c, axis=1, keepdims=True), jnp.max(s_p, axis=1, keepdims=True))
            p_c = jnp.exp(s_c - m)
            p_p = jnp.exp(s_p - m)
            den = jnp.sum(p_c, axis=1, keepdims=True) + jnp.sum(p_p, axis=1, keepdims=True)
            pv = _dot(p_c.astype(BF), vc) + _dot(p_p.astype(BF), vp)
            o_ref[rows, :] = unstack(pv / den)
            l_ref[rows, :] = unstack(jnp.broadcast_to(m + jnp.log(den), pv.shape))

    cur, prev, _ = _band_specs(W, nb, per)
    out = jax.ShapeDtypeStruct((d, L, OUT_A), F32)
    res = _call(body, name=f"attn_a_fwd_g{g}", grid=(d, nb // per),
                in_specs=[cur(cols[0]), prev(cols[1]), cur(cols[1]), prev(cols[2]), cur(cols[2])],
                out_specs=(cur(0), cur(0)), out_shape=(out, out),
                args=(qkv[0], qkv[1], qkv[1], qkv[2], qkv[2]), sem=("parallel", "parallel"), comm=comm)
    return res[0], res[1], list(res[2:])


def _attn_a_bwd(qkv, cols, do, stats, g):
    win, dil = DSW_GROUPS[g]
    W = win // dil
    d, L, _ = qkv[0].shape
    nb = L // W
    per = _blocks_per_step(nb)
    nsteps = nb // per
    slopes = _slopes(g)

    def body(q_ref, qn_ref, kp_ref, kc_ref, vp_ref, vc_ref, do_ref, don_ref, st_ref, stn_ref,
             dq_ref, dk_ref, dv_ref):
        n = pl.program_id(1)
        stack, unstack, column, _ = _head_stack(W)
        b_cur, b_prev, m_cur, m_prev = _stacked_bias(W, slopes, dil)
        m_first = jnp.logical_and(m_prev, n > 0)
        m_last = jnp.logical_and(m_prev, n < nsteps - 1)
        everything = slice(None)
        for b in range(per):
            rows = slice(b * W, (b + 1) * W)
            before = slice((b - 1) * W, b * W)
            after = slice((b + 1) * W, (b + 2) * W)
            first, last = b == 0, b == per - 1
            qs = stack(q_ref[rows, :])
            qn = stack(qn_ref[...] if last else q_ref[after, :])
            dos = stack(do_ref[rows, :])
            don = stack(don_ref[...] if last else do_ref[after, :])
            kc, vc = kc_ref[rows, :], vc_ref[rows, :]
            kp, vp = (kp_ref[...], vp_ref[...]) if first else (kc_ref[before, :], vc_ref[before, :])
            lse_c, dsum_c = column(st_ref, rows), column(st_ref, rows, STAT_OFF)
            lse_n = column(stn_ref, everything) if last else column(st_ref, after)
            dsum_n = column(stn_ref, everything, STAT_OFF) if last else column(st_ref, after, STAT_OFF)
            m_p = m_first if first else m_prev
            m_n = m_last if last else m_prev
            p_cc = jnp.exp(jnp.where(m_cur, _dot(qs, kc, _NT) * ATT_SCALE - b_cur, NEG) - lse_c)
            p_cp = jnp.exp(jnp.where(m_p, _dot(qs, kp, _NT) * ATT_SCALE - b_prev, NEG) - lse_c)
            p_nc = jnp.exp(jnp.where(m_n, _dot(qn, kc, _NT) * ATT_SCALE - b_prev, NEG) - lse_n)
            ds_cc = (p_cc * (_dot(dos, vc, _NT) - dsum_c) * ATT_SCALE).astype(BF)
            ds_cp = (p_cp * (_dot(dos, vp, _NT) - dsum_c) * ATT_SCALE).astype(BF)
            ds_nc = (p_nc * (_dot(don, vc, _NT) - dsum_n) * ATT_SCALE).astype(BF)
            dq_ref[rows, :] = unstack(_dot(ds_cc, kc) + _dot(ds_cp, kp)).astype(BF)
            dk_ref[rows, :] = (_dot(ds_cc, qs, _TN) + _dot(ds_nc, qn, _TN)).astype(BF)
            dv_ref[rows, :] = (_dot(p_cc.astype(BF), dos, _TN) + _dot(p_nc.astype(BF), don, _TN)).astype(BF)

    cur, prev, nxt = _band_specs(W, nb, per)
    out = jax.ShapeDtypeStruct((d, L, OUT_A), BF)
    cq, ck, cv = cols
    return pl.pallas_call(
        body, name=f"attn_a_bwd_g{g}",
        out_shape=(out, out, out),
        grid=(d, nsteps),
        in_specs=[cur(cq), nxt(cq), prev(ck), cur(ck), prev(cv), cur(cv), cur(0), nxt(0), cur(0), nxt(0)],
        out_specs=(cur(0), cur(0), cur(0)),
        compiler_params=_params("parallel", "parallel"),
    )(qkv[0], qkv[0], qkv[1], qkv[1], qkv[2], qkv[2], do, do, stats, stats)


SB_PAIR = 2
SB_QROWS = SB_BLOCK


def _softplus_parts(z):
    e = jnp.exp(-jnp.abs(z))
    log1p_e = jnp.where(e < 1e-4, e, jnp.log(1.0 + e))
    return e, jnp.maximum(z, 0.0) + log1p_e


def _split_dot(x, t):
    hi = x.astype(BF)
    lo = (x - hi.astype(F32)).astype(BF)
    return _dot(hi, t) + _dot(lo, t)


def _sb_block(qh, kk, causal, r_run, tri_incl):
    z = _dot(qh, kk, _NT)
    e, sp = _softplus_parts(z)
    ls = jnp.where(causal, -sp, 0.0)
    cin = _split_dot(ls, tri_incl)
    a = jnp.where(causal, jnp.exp(z + cin + r_run), 0.0)
    return z, e, cin, a


def _sb_specs(S):
    Q, hd = SB_QROWS, HEAD_DIM
    lanes = SB_PAIR * hd
    qc = (3 * W_A) // lanes
    kc = (3 * W_A + W_B) // lanes
    vc = (3 * W_A + 2 * W_B) // lanes
    q_spec = pl.BlockSpec((Q, lanes), lambda p, i: (i, qc + p))
    k_spec = pl.BlockSpec((S, lanes), lambda p, i: (0, kc + p))
    v_spec = pl.BlockSpec((S, lanes), lambda p, i: (0, vc + p))
    o_spec = pl.BlockSpec((Q, lanes), lambda p, i: (i, p))
    full = pl.BlockSpec((S, lanes), lambda p, i: (0, p))
    return q_spec, k_spec, v_spec, o_spec, full


def _sb_stack():
    Q, hd = SB_QROWS, HEAD_DIM
    lane_head = lax.broadcasted_iota(jnp.int32, (Q, SB_PAIR * hd), 1) // hd

    def stack(x):
        return jnp.concatenate([jnp.where(lane_head == h, x, jnp.zeros_like(x)) for h in range(SB_PAIR)], axis=0)

    def unstack(y):
        out = jnp.where(lane_head == 0, y[0:Q], 0.0)
        for h in range(1, SB_PAIR):
            out = jnp.where(lane_head == h, y[h * Q:(h + 1) * Q], out)
        return out

    return stack, unstack


def _sb_iotas(i):
    B, Q = SB_BLOCK, SB_QROWS
    row = lax.broadcasted_iota(jnp.int32, (Q, B), 0) + i * Q
    col = lax.broadcasted_iota(jnp.int32, (Q, B), 1)
    ahead = jnp.concatenate([col - row] * SB_PAIR, axis=0)
    tr = lax.broadcasted_iota(jnp.int32, (B, B), 0)
    tc = lax.broadcasted_iota(jnp.int32, (B, B), 1)
    return ahead, tr, tc


def _sb_fwd(proj, comm=()):
    S = proj.shape[0]
    B, Q, hd = SB_BLOCK, SB_QROWS, HEAD_DIM
    nq = S // Q
    R = SB_PAIR * Q
    q_spec, k_spec, v_spec, o_spec, _ = _sb_specs(S)

    def body(q_ref, k_ref, v_ref, o_ref):
        i = pl.program_id(1)
        ahead, tr, tc = _sb_iotas(i)
        tri_incl = (tr >= tc).astype(BF)
        stack, unstack = _sb_stack()
        qs = stack(q_ref[...] * ATT_SCALE)

        def cond(c):
            return jnp.logical_and(c[0] >= 0, c[-1] > SB_EXIT)

        def step(c):
            kb, r_run, acc, _ = c
            off = pl.multiple_of(kb * B, B)
            causal = ahead < -kb * B
            _, _, cin, a = _sb_block(qs, k_ref[pl.ds(off, B), :], causal, r_run, tri_incl)
            acc = acc + _dot(a.astype(BF), v_ref[pl.ds(off, B), :])
            r_run = r_run + cin[:, 0:1]
            return kb - 1, r_run, acc, jnp.max(r_run)

        init = (i, jnp.zeros((R, 1), F32), jnp.zeros((R, SB_PAIR * hd), F32), jnp.float32(0.0))
        fin = lax.while_loop(cond, step, init)
        o_ref[...] = unstack(fin[2])

    res = _call(body, name="sb_fwd", grid=(SB_HEADS // SB_PAIR, nq), in_specs=[q_spec, k_spec, v_spec],
                out_specs=o_spec, out_shape=jax.ShapeDtypeStruct((S, W_B), F32), args=(proj, proj, proj),
                sem=("parallel", "parallel"), comm=comm)
    return res[0], list(res[1:])


def _sb_bwd(proj, do, o, comm=()):
    S = proj.shape[0]
    B, Q, hd = SB_BLOCK, SB_QROWS, HEAD_DIM
    nq = S // Q
    R = SB_PAIR * Q
    q_spec, k_spec, v_spec, o_spec, full = _sb_specs(S)

    def body(q_ref, k_ref, v_ref, do_ref, o_ref, dq_ref, dk_ref, dv_ref):
        i = pl.program_id(1)

        @pl.when(i == 0)
        def _():
            dk_ref[...] = jnp.zeros_like(dk_ref)
            dv_ref[...] = jnp.zeros_like(dv_ref)

        ahead, tr, tc = _sb_iotas(i)
        tri_incl = (tr >= tc).astype(BF)
        tri_strict = (tr > tc).astype(BF)
        stack, unstack = _sb_stack()
        qs = stack(q_ref[...] * ATT_SCALE)
        dobs = stack(do_ref[...])
        o_all = o_ref[...]
        dsum = jnp.sum(dobs.astype(F32) * jnp.concatenate([o_all] * SB_PAIR, axis=0), axis=1, keepdims=True)

        def cond(c):
            return jnp.logical_and(c[0] >= 0, c[-1] > SB_EXIT)

        def step(c):
            kb, r_run, g_run, dq, _ = c
            off = pl.multiple_of(kb * B, B)
            causal = ahead < -kb * B
            kk = k_ref[pl.ds(off, B), :]
            vv = v_ref[pl.ds(off, B), :]
            z, e, cin, a = _sb_block(qs, kk, causal, r_run, tri_incl)
            a16 = a.astype(BF)
            gmat = a16.astype(F32) * _dot(dobs, vv, _NT)
            later = _split_dot(gmat, tri_strict)
            pfx = dsum - g_run - later
            sig = jnp.where(z >= 0, 1.0, e) / (1.0 + e)
            dz = jnp.where(causal, gmat - sig * pfx, 0.0).astype(BF)
            dq = dq + _dot(dz, kk)
            dk_ref[pl.ds(off, B), :] += _dot(dz, qs, _TN)
            dv_ref[pl.ds(off, B), :] += _dot(a16, dobs, _TN)
            g_run = g_run + jnp.sum(gmat, axis=1, keepdims=True)
            r_run = r_run + cin[:, 0:1]
            return kb - 1, r_run, g_run, dq, jnp.max(r_run)

        init = (i, jnp.zeros((R, 1), F32), jnp.zeros((R, 1), F32), jnp.zeros((R, SB_PAIR * hd), F32),
                jnp.float32(0.0))
        fin = lax.while_loop(cond, step, init)
        dq_ref[...] = unstack(fin[3]) * ATT_SCALE

    out = jax.ShapeDtypeStruct((S, W_B), F32)
    res = _call(body, name="sb_bwd", grid=(SB_HEADS // SB_PAIR, nq), in_specs=[q_spec, k_spec, v_spec, o_spec, o_spec],
                out_specs=(o_spec, full, full), out_shape=(out, out, out), args=(proj, proj, proj, do, o), comm=comm)
    return res[0], res[1], res[2], list(res[3:])


def _merge_fwd(o_g, l_g, o_b, proj, b_gate, w_br):
    S = o_b.shape[0]
    D = D_MODEL
    tm = _pick(S, 256)
    gcol = GATE_OFF // D

    def body(o0, o1, o2, l0, l1, l2, ob_ref, ga_ref, gb_ref, bg_ref, w_ref, mg_ref, oa_ref, lse_ref):
        la, lb, lc = l0[...], l1[...], l2[...]
        mx = jnp.maximum(jnp.maximum(la, lb), lc)
        ea, eb, ec = jnp.exp(la - mx), jnp.exp(lb - mx), jnp.exp(lc - mx)
        den = ea + eb + ec
        oa = (ea * o0[...] + eb * o1[...] + ec * o2[...]) / den
        oa_ref[...] = oa
        lse_ref[...] = mx + jnp.log(den)
        ya = _dot(oa.astype(BF), w_ref[0:OUT_A, :])
        yb = _dot(ob_ref[...].astype(BF), w_ref[OUT_A:OUT_A + W_B, :])
        bg = bg_ref[...]
        g_a = jax.nn.sigmoid(ga_ref[...].astype(F32) + bg[:, 0:D])
        g_b = jax.nn.sigmoid(gb_ref[...].astype(F32) + bg[:, D:2 * D])
        mg_ref[...] = (g_a * ya + g_b * yb).astype(BF)

    nar = pl.BlockSpec((tm, OUT_A), lambda i: (i, 0))
    wide = pl.BlockSpec((tm, D), lambda i: (i, 0))
    return pl.pallas_call(
        body, name="merge_fwd",
        out_shape=(jax.ShapeDtypeStruct((S, D), BF), jax.ShapeDtypeStruct((S, OUT_A), F32),
                   jax.ShapeDtypeStruct((S, OUT_A), F32)),
        grid=(S // tm,),
        in_specs=[nar] * 7 + [pl.BlockSpec((tm, D), lambda i: (i, gcol)),
                              pl.BlockSpec((tm, D), lambda i: (i, gcol + 1)),
                              pl.BlockSpec((1, 2 * D), lambda i: (0, 0)),
                              pl.BlockSpec((OUT_A + W_B, D), lambda i: (0, 0))],
        out_specs=(wide, nar, nar),
        compiler_params=_params("parallel"),
    )(*o_g, *l_g, o_b, proj, proj, b_gate.reshape(1, 2 * D), w_br)


def _merge_wo(o_g, l_g, o_b, proj, b_gate, w_br, w_o, x, name, comm=()):
    S = o_b.shape[0]
    D = D_MODEL
    tm = _pick(S, 256)
    gcol = GATE_OFF // D

    def body(o0, o1, o2, l0, l1, l2, ob_ref, ga_ref, gb_ref, bg_ref, w_ref, wo_ref, x_ref,
             x1_ref, mg_ref, oa_ref, lse_ref):
        la, lb, lc = l0[...], l1[...], l2[...]
        mx = jnp.maximum(jnp.maximum(la, lb), lc)
        ea, eb, ec = jnp.exp(la - mx), jnp.exp(lb - mx), jnp.exp(lc - mx)
        den = ea + eb + ec
        oa = (ea * o0[...] + eb * o1[...] + ec * o2[...]) / den
        oa_ref[...] = oa
        lse_ref[...] = mx + jnp.log(den)
        oa16 = oa.astype(BF)
        ob16 = ob_ref[...].astype(BF)
        acc = x_ref[...]
        for c0 in range(0, D, FFN_CHUNK):
            cs = slice(c0, c0 + FFN_CHUNK)
            ya = _dot(oa16, w_ref[0:OUT_A, cs])
            yb = _dot(ob16, w_ref[OUT_A:OUT_A + W_B, cs])
            g_a = jax.nn.sigmoid(ga_ref[:, cs].astype(F32) + bg_ref[:, cs])
            g_b = jax.nn.sigmoid(gb_ref[:, cs].astype(F32) + bg_ref[:, D + c0:D + c0 + FFN_CHUNK])
            mg = (g_a * ya + g_b * yb).astype(BF)
            mg_ref[:, cs] = mg
            acc = acc + _dot(mg, wo_ref[cs, :])
        x1_ref[...] = acc

    nar = pl.BlockSpec((tm, OUT_A), lambda i: (i, 0))
    wide = pl.BlockSpec((tm, D), lambda i: (i, 0))
    res = _call(body, name=name, grid=(S // tm,),
                in_specs=[nar] * 7 + [pl.BlockSpec((tm, D), lambda i: (i, gcol)),
                                      pl.BlockSpec((tm, D), lambda i: (i, gcol + 1)),
                                      pl.BlockSpec((1, 2 * D), lambda i: (0, 0)),
                                      pl.BlockSpec((OUT_A + W_B, D), lambda i: (0, 0)),
                                      pl.BlockSpec((D, D), lambda i: (0, 0)), wide],
                out_specs=(wide, wide, nar, nar),
                out_shape=(jax.ShapeDtypeStruct((S, D), F32), jax.ShapeDtypeStruct((S, D), BF),
                           jax.ShapeDtypeStruct((S, OUT_A), F32), jax.ShapeDtypeStruct((S, OUT_A), F32)),
                args=(*o_g, *l_g, o_b, proj, proj, b_gate.reshape(1, 2 * D), w_br, w_o, x), comm=comm)
    return res[0], res[1], res[2], res[3], list(res[4:])


STAT_OFF = HEAD_DIM // 2


def _merge_bwd(dx, w_o, oa, ob, lse, proj, b_gate, w_br):
    S = ob.shape[0]
    D = D_MODEL
    tm = _pick(S, 256)
    gcol = GATE_OFF // D

    def body(dx_ref, wo_ref, oa_ref, ob_ref, l_ref, ga_ref, gb_ref, bg_ref, w_ref,
             dya_ref, dyb_ref, doa_ref, dob_ref, st_ref, dg_ref, dbg_ref):
        @pl.when(pl.program_id(0) == 0)
        def _():
            dbg_ref[...] = jnp.zeros_like(dbg_ref)

        dx16 = dx_ref[...].astype(BF)
        oa = oa_ref[...]
        oa16 = oa.astype(BF)
        ob16 = ob_ref[...].astype(BF)
        doa = jnp.zeros((tm, OUT_A), F32)
        dob = jnp.zeros((tm, W_B), F32)
        for c0 in range(0, D, FFN_CHUNK):
            cs = slice(c0, c0 + FFN_CHUNK)
            cs2 = slice(D + c0, D + c0 + FFN_CHUNK)
            wa = w_ref[0:OUT_A, cs]
            wb = w_ref[OUT_A:OUT_A + W_B, cs]
            dm = _dot(dx16, wo_ref[cs, :], _NT)
            ya = _dot(oa16, wa)
            yb = _dot(ob16, wb)
            g_a = jax.nn.sigmoid(ga_ref[:, cs].astype(F32) + bg_ref[:, cs])
            g_b = jax.nn.sigmoid(gb_ref[:, cs].astype(F32) + bg_ref[:, cs2])
            dga = dm * ya * g_a * (1.0 - g_a)
            dgb = dm * yb * g_b * (1.0 - g_b)
            dg_ref[:, cs] = dga.astype(BF)
            dg_ref[:, cs2] = dgb.astype(BF)
            dbg_ref[:, cs] += jnp.sum(dga, axis=0, keepdims=True)
            dbg_ref[:, cs2] += jnp.sum(dgb, axis=0, keepdims=True)
            dya = (dm * g_a).astype(BF)
            dyb = (dm * g_b).astype(BF)
            dya_ref[:, cs] = dya
            dyb_ref[:, cs] = dyb
            doa = doa + _dot(dya, wa, _NT)
            dob = dob + _dot(dyb, wb, _NT)
        doa = doa.astype(BF)
        doa_ref[...] = doa
        dob_ref[...] = dob.astype(BF)
        r = lax.broadcasted_iota(jnp.int32, (OUT_A, OUT_A), 0) // HEAD_DIM
        c = lax.broadcasted_iota(jnp.int32, (OUT_A, OUT_A), 1) // HEAD_DIM
        dsum = _split_dot(doa.astype(F32) * oa, (r == c).astype(BF))
        lane = lax.broadcasted_iota(jnp.int32, dsum.shape, 1) % HEAD_DIM
        st_ref[...] = jnp.where(lane < STAT_OFF, l_ref[...], dsum)

    nar = pl.BlockSpec((tm, OUT_A), lambda i: (i, 0))
    wide = pl.BlockSpec((tm, D), lambda i: (i, 0))
    wide2 = pl.BlockSpec((tm, 2 * D), lambda i: (i, 0))
    vec2 = pl.BlockSpec((1, 2 * D), lambda i: (0, 0))
    return pl.pallas_call(
        body, name="merge_bwd",
        out_shape=(jax.ShapeDtypeStruct((S, D), BF), jax.ShapeDtypeStruct((S, D), BF),
                   jax.ShapeDtypeStruct((S, OUT_A), BF), jax.ShapeDtypeStruct((S, W_B), BF),
                   jax.ShapeDtypeStruct((S, OUT_A), F32), jax.ShapeDtypeStruct((S, 2 * D), BF),
                   jax.ShapeDtypeStruct((1, 2 * D), F32)),
        grid=(S // tm,),
        in_specs=[wide, pl.BlockSpec((D, D), lambda i: (0, 0)), nar, nar, nar,
                  pl.BlockSpec((tm, D), lambda i: (i, gcol)), pl.BlockSpec((tm, D), lambda i: (i, gcol + 1)), vec2,
                  pl.BlockSpec((OUT_A + W_B, D), lambda i: (0, 0))],
        out_specs=(wide, wide, nar, nar, nar, wide2, vec2),
        compiler_params=_params("arbitrary"),
    )(dx, w_o, oa, ob, lse, proj, proj, b_gate.reshape(1, 2 * D), w_br)


_SQRT_HALF = 0.7071067811865476
_INV_SQRT_2PI = 0.3989422804014327


def _gelu_parts(a):
    cdf = 0.5 * (1.0 + lax.erf(a * _SQRT_HALF))
    pdf = _INV_SQRT_2PI * jnp.exp(-0.5 * a * a)
    return cdf, pdf


def _shift_down(a, halo, k):
    rows = lax.broadcasted_iota(jnp.int32, a.shape, 0)
    out = pltpu.roll(a, k, 0)
    for r in range(k):
        out = jnp.where(rows == r, halo[8 - k + r:8 - k + r + 1, :], out)
    return out


def _shift_up(a, halo, k):
    n = a.shape[0]
    rows = lax.broadcasted_iota(jnp.int32, a.shape, 0)
    out = pltpu.roll(a, n - k, 0)
    for r in range(k):
        out = jnp.where(rows == n - k + r, halo[r:r + 1, :], out)
    return out


def _conv_in(a_ref, h_ref, first):
    a = a_ref[...].astype(F32)
    halo = jnp.where(first, 0.0, h_ref[...].astype(F32))
    return a, _shift_down(a, halo, 1), _shift_down(a, halo, 2)


def _ffn_specs(S, tm):
    F = D_FF
    t8 = tm // 8
    a_spec = pl.BlockSpec((tm, F), lambda i: (i, 0))
    v_spec = pl.BlockSpec((tm, F), lambda i: (i, 1))
    halo_prev = pl.BlockSpec((8, F), lambda i: (jnp.maximum(i * t8 - 1, 0), 0))
    return a_spec, v_spec, halo_prev


def _ffn_act_fwd(up, conv_w, conv_b):
    S = up.shape[0]
    F = D_FF
    tm = _pick(S, 256)
    a_spec, v_spec, halo_prev = _ffn_specs(S, tm)

    def body(a_ref, h_ref, v_ref, w_ref, b_ref, act_ref):
        a0, a1, a2 = _conv_in(a_ref, h_ref, pl.program_id(0) == 0)
        w = w_ref[...]
        ac = b_ref[...] + w[0:1, :] * a2 + w[1:2, :] * a1 + w[2:3, :] * a0
        cdf, _ = _gelu_parts(ac)
        act_ref[...] = (ac * cdf * v_ref[...].astype(F32)).astype(BF)

    return pl.pallas_call(
        body, name="ffn_act_fwd",
        out_shape=jax.ShapeDtypeStruct((S, F), BF),
        grid=(S // tm,),
        in_specs=[a_spec, halo_prev, v_spec, pl.BlockSpec((3, F), lambda i: (0, 0)),
                  pl.BlockSpec((1, F), lambda i: (0, 0))],
        out_specs=a_spec,
        compiler_params=_params("parallel"),
    )(up, up, up, conv_w, conv_b.reshape(1, F))


def _ffn_down(up, conv_w, conv_b, w_down, res, *, name, comm=()):
    S = up.shape[0]
    F = D_FF
    D = w_down.shape[1]
    tm = _pick(S, 256)
    a_spec, v_spec, halo_prev = _ffn_specs(S, tm)

    def body(a_ref, h_ref, v_ref, w_ref, b_ref, wd_ref, r_ref, o_ref, act_ref, ac_ref):
        first = pl.program_id(0) == 0
        acc = r_ref[...]
        for c0 in range(0, F, FFN_CHUNK):
            cs = slice(c0, c0 + FFN_CHUNK)
            a = a_ref[:, cs].astype(F32)
            halo = jnp.where(first, 0.0, h_ref[:, cs].astype(F32))
            w = w_ref[:, cs]
            ac = b_ref[:, cs] + w[0:1, :] * _shift_down(a, halo, 2) + w[1:2, :] * _shift_down(a, halo, 1) + w[2:3, :] * a
            ac_ref[:, cs] = ac.astype(BF)
            cdf, _ = _gelu_parts(ac)
            act = (ac * cdf * v_ref[:, cs].astype(F32)).astype(BF)
            act_ref[:, cs] = act
            acc = acc + _dot(act, wd_ref[cs, :])
        o_ref[...] = acc

    row = pl.BlockSpec((tm, D), lambda i: (i, 0))
    res_ = _call(body, name=name, grid=(S // tm,),
                 in_specs=[a_spec, halo_prev, v_spec, pl.BlockSpec((3, F), lambda i: (0, 0)),
                           pl.BlockSpec((1, F), lambda i: (0, 0)), pl.BlockSpec((F, D), lambda i: (0, 0)), row],
                 out_specs=(row, a_spec, a_spec),
                 out_shape=(jax.ShapeDtypeStruct((S, D), F32), jax.ShapeDtypeStruct((S, F), BF),
                            jax.ShapeDtypeStruct((S, F), BF)),
                 args=(up, up, up, conv_w, conv_b.reshape(1, F), w_down, res), comm=comm)
    return res_[0], res_[1], res_[2], list(res_[3:])


def _shift_up_pair(a, nxt):
    n = a.shape[0]
    r8 = lax.broadcasted_iota(jnp.int32, (8,) + a.shape[1:], 0)
    out = []
    for k in (1, 2):
        rolled = pltpu.roll(a, n - k, 0)
        tail = jnp.where(r8 >= 8 - k, pltpu.roll(nxt, 8 - k, 0), rolled[n - 8:n])
        out.append(jnp.concatenate([rolled[0:n - 8], tail], axis=0))
    return out


def _ffn_bwd(dx, w_down, up, ac, conv_w, comm=()):
    S = up.shape[0]
    F = D_FF
    D = dx.shape[1]
    tm = _pick(S, 128)
    t8 = tm // 8
    nt = S // tm
    a_spec, v_spec, _ = _ffn_specs(S, tm)

    def nxt(width, col):
        return pl.BlockSpec((8, width), lambda i: (jnp.minimum((i + 1) * t8, S // 8 - 1), col))

    def body(dx_ref, dxn_ref, wd_ref, ac_ref, acn_ref, a_ref, v_ref, vn_ref, w_ref, dup_ref, dw_ref, db_ref):
        i = pl.program_id(0)

        @pl.when(i == 0)
        def _():
            dw_ref[...] = jnp.zeros_like(dw_ref)
            db_ref[...] = jnp.zeros_like(db_ref)

        dx16 = dx_ref[...].astype(BF)
        dxn16 = dxn_ref[...].astype(BF)
        last = i == nt - 1

        def dconv(dact, ac, v):
            cdf, pdf = _gelu_parts(ac)
            return cdf, dact * v * (cdf + ac * pdf)

        for c0 in range(0, F, FFN_CHUNK):
            cs = slice(c0, c0 + FFN_CHUNK)
            wd = wd_ref[cs, :]
            dact = _dot(dx16, wd, _NT)
            ac = ac_ref[:, cs].astype(F32)
            cdf, dac = dconv(dact, ac, v_ref[:, cs].astype(F32))
            dup_ref[:, F + c0:F + c0 + FFN_CHUNK] = (dact * ac * cdf).astype(BF)
            _, dac_n = dconv(_dot(dxn16, wd, _NT), acn_ref[:, cs].astype(F32), vn_ref[:, cs].astype(F32))
            d1, d2 = _shift_up_pair(dac, jnp.where(last, 0.0, dac_n))
            w = w_ref[:, cs]
            dup_ref[:, cs] = (w[2:3, :] * dac + w[1:2, :] * d1 + w[0:1, :] * d2).astype(BF)
            a = a_ref[:, cs].astype(F32)
            db_ref[:, cs] += jnp.sum(dac, axis=0, keepdims=True)
            dw_ref[0:1, cs] += jnp.sum(d2 * a, axis=0, keepdims=True)
            dw_ref[1:2, cs] += jnp.sum(d1 * a, axis=0, keepdims=True)
            dw_ref[2:3, cs] += jnp.sum(dac * a, axis=0, keepdims=True)

    w_spec = pl.BlockSpec((3, F), lambda i: (0, 0))
    b_spec = pl.BlockSpec((1, F), lambda i: (0, 0))
    res = _call(body, name="ffn_bwd", grid=(nt,),
                in_specs=[pl.BlockSpec((tm, D), lambda i: (i, 0)), nxt(D, 0), pl.BlockSpec((F, D), lambda i: (0, 0)),
                          a_spec, nxt(F, 0), a_spec, v_spec, nxt(F, 1), w_spec],
                out_specs=(pl.BlockSpec((tm, 2 * F), lambda i: (i, 0)), w_spec, b_spec),
                out_shape=(jax.ShapeDtypeStruct((S, 2 * F), BF), jax.ShapeDtypeStruct((3, F), F32),
                           jax.ShapeDtypeStruct((1, F), F32)),
                args=(dx, dx, w_down, ac, ac, up, up, up, conv_w), comm=comm)
    return res[0], res[1], res[2], list(res[3:])


def _adamw(parts, w, m, v, name, row0=0, prev=None):
    R, C = w.shape
    Rp = parts.shape[1]
    tr = Rp
    for cand in (512, 256, 128, 64, 32, 16):
        if Rp % cand == 0 and row0 % cand == 0 and cand * C * 4 <= (1 << 21):
            tr = cand
            break
    b0 = row0 // tr
    c1 = 1.0 / (1.0 - ADAM_B1 ** ADAM_STEP)
    c2 = 1.0 / (1.0 - ADAM_B2 ** ADAM_STEP)

    def body(p_ref, w_ref, m_ref, v_ref, *rest):
        g_ref, d_ref, nm_ref, nv_ref = rest[-4:]
        g = p_ref[0].astype(F32)
        for j in range(1, N_DEV):
            g = g + p_ref[j].astype(F32)
        nm = ADAM_B1 * m_ref[...] + (1.0 - ADAM_B1) * g
        nv = ADAM_B2 * v_ref[...] + (1.0 - ADAM_B2) * (g * g)
        g_ref[...] = g
        nm_ref[...] = nm
        nv_ref[...] = nv
        d_ref[...] = -ADAM_LR * ((nm * c1) / (jnp.sqrt(nv * c2) + ADAM_EPS) + ADAM_WD * w_ref[...])

    blk = pl.BlockSpec((tr, C), lambda i: (b0 + i, 0))
    out = jax.ShapeDtypeStruct((R, C), F32)
    carried = [] if prev is None else list(prev)
    return pl.pallas_call(
        body, name=name,
        out_shape=(out, out, out, out),
        grid=(Rp // tr,),
        in_specs=[pl.BlockSpec((N_DEV, tr, C), lambda i: (0, i, 0)), blk, blk, blk]
        + [pl.BlockSpec(memory_space=pl.ANY)] * len(carried),
        out_specs=(blk, blk, blk, blk),
        input_output_aliases={4 + k: k for k in range(len(carried))},
        compiler_params=_params("parallel"),
    )(parts, w, m, v, *carried)


def _dil(t, dil):
    S, C = t.shape
    if dil == 1:
        return t.reshape(1, S, C)
    return t.reshape(S // dil, dil, C).transpose(1, 0, 2)


def _undil(t):
    d, L, C = t.shape
    if d == 1:
        return t.reshape(L, C)
    return t.transpose(1, 0, 2).reshape(L * d, C)


def _group_qkv(proj, g):
    dil = DSW_GROUPS[g][1]
    if dil == 1:
        p3 = _dil(proj, 1)
        return (p3, p3, p3), (g, W_A // OUT_A + g, 2 * W_A // OUT_A + g)
    c0 = g * OUT_A
    return tuple(_dil(proj[:, o + c0:o + c0 + OUT_A], dil) for o in (0, W_A, 2 * W_A)), (0, 0, 0)


_COL_SHARDED = ("w_in", "w_br", "w_up")
_ROW_SHARDED = ("w_o", "w_down")


class _Plan:
    def __init__(self):
        self.riders = {}
        self.landed = {}

    def ride(self, slot, key, kind, x):
        self.riders.setdefault(slot, []).append((key, kind, x))

    def run(self, slot, fn, *args, **kw):
        items = self.riders.pop(slot, [])
        res = fn(*args, comm=[(kind, x) for _, kind, x in items], **kw)
        for (key, _, _), r in zip(items, res[-1]):
            self.landed[key] = r
        return res[0] if len(res) == 2 else res[:-1]

    def weight(self, n, l):
        g = self.landed[(n, l)]
        if n in _COL_SHARDED:
            return g.transpose(1, 0, 2).reshape(g.shape[1], -1)
        return g.reshape(-1, g.shape[2])

    def scatter(self, slot, n, l, full, part=0):
        K, N = full.shape
        if n in _COL_SHARDED:
            blocks = full.reshape(K, N_DEV, N // N_DEV).transpose(1, 0, 2)
        else:
            blocks = full.reshape(N_DEV, K // N_DEV, N)
        self.ride(slot, ("d" + n, l, part), "scatter", blocks)


def _layer_fwd(x, p, plan, l):
    h = _rms_fwd(x, p["norm1"][l], f"rms1_fwd_{l}")
    proj = plan.run(f"proj_{l}", _matmul, h, plan.weight("w_in", l), mode="nn", out_dtype=BF, name=f"proj_{l}",
                    tm=1024, tn=1024, tk=1024)
    o_g, l_g, qkv_g = [], [], []
    for g in range(N_GROUPS):
        qkv, cols = _group_qkv(proj, g)
        og, lg = plan.run(f"attn_a_fwd_g{g}_{l}", _attn_a_fwd, qkv, cols, g)
        o_g.append(_undil(og))
        l_g.append(_undil(lg))
        qkv_g.append((qkv, cols))
    ob = plan.run(f"sb_fwd_{l}", _sb_fwd, proj)
    x1, merged, oa, lse = plan.run(f"wo_{l}", _merge_wo, o_g, l_g, ob, proj, p["b_gate"][l], plan.weight("w_br", l),
                                   plan.weight("w_o", l), x, f"wo_{l}")
    h2 = _rms_fwd(x1, p["norm2"][l], f"rms2_fwd_{l}")
    up = plan.run(f"up_{l}", _matmul, h2, plan.weight("w_up", l), mode="nn", out_dtype=BF, name=f"up_{l}",
                  tm=1024, tn=1408, tk=1024)
    x2, act, ac = plan.run(f"down_{l}", _ffn_down, up, p["conv_w"][l], p["conv_b"][l], plan.weight("w_down", l), x1,
                           name=f"down_{l}")
    saved = dict(x=x, h=h, proj=proj, qkv_g=qkv_g, oa=oa, ob=ob, lse=lse, merged=merged, x1=x1, h2=h2, up=up, act=act, ac=ac)
    return x2, saved


def _layer_bwd(dx2, sv, p, plan, l):
    gr = {}
    dwd = plan.run(f"dw_down_{l}", _matmul, sv["act"], dx2, mode="tn", out_dtype=BF, name=f"dw_down_{l}",
                   tm=1408, tn=1024, tk=2048)
    plan.scatter(f"d_h2_{l}", "w_down", l, dwd)
    dup, gr["conv_w"], dcb = plan.run(f"ffn_bwd_{l}", _ffn_bwd, dx2, plan.weight("w_down", l), sv["up"], sv["ac"],
                                      p["conv_w"][l])
    gr["conv_b"] = dcb[0]
    dx1, dn2 = plan.run(f"d_h2_{l}", _matmul_rms_bwd, dup, plan.weight("w_up", l), sv["x1"], p["norm2"][l], dx2,
                        name=f"d_h2_{l}")
    dwu = plan.run(f"dw_up_{l}", _matmul, sv["h2"], dup, mode="tn", out_dtype=BF, name=f"dw_up_{l}",
                   tm=1024, tn=1408, tk=2048)
    plan.scatter(f"sb_bwd_{l}", "w_up", l, dwu)
    gr["norm2"] = dn2[0]
    dwo = plan.run(f"dw_o_{l}", _matmul, sv["merged"], dx1, mode="tn", out_dtype=BF, name=f"dw_o_{l}",
                   tm=1024, tn=1024, tk=2048)
    plan.scatter(f"dw_in_{l}", "w_o", l, dwo)
    dya, dyb, doa, dob, stats, dgate, dbg = _merge_bwd(dx1, plan.weight("w_o", l), sv["oa"], sv["ob"], sv["lse"],
                                                       sv["proj"], p["b_gate"][l], plan.weight("w_br", l))
    gr["b_gate"] = dbg[0]
    dwa = plan.run(f"dw_bra_{l}", _matmul, sv["oa"], dya, mode="tn", out_dtype=BF, name=f"dw_bra_{l}",
                   tm=256, tn=1024, tk=2048)
    dwb = plan.run(f"dw_brb_{l}", _matmul, sv["ob"], dyb, mode="tn", out_dtype=BF, name=f"dw_brb_{l}",
                   tm=256, tn=1024, tk=2048)
    plan.scatter(f"dw_in_{l}", "w_br", l, jnp.concatenate([dwa, dwb], axis=0))
    proj = sv["proj"]
    dq_a, dk_a, dv_a = [], [], []
    for g, (_, dil) in enumerate(DSW_GROUPS):
        qkv, cols = sv["qkv_g"][g]
        dqg, dkg, dvg = _attn_a_bwd(qkv, cols, _dil(doa, dil), _dil(stats, dil), g)
        dq_a.append(_undil(dqg))
        dk_a.append(_undil(dkg))
        dv_a.append(_undil(dvg))
    dqb, dkb, dvb = plan.run(f"sb_bwd_{l}", _sb_bwd, proj, dob, sv["ob"])
    dproj = jnp.concatenate(dq_a + dk_a + dv_a + [dqb.astype(BF), dkb.astype(BF), dvb.astype(BF), dgate], axis=1)
    dx, dn1 = plan.run(f"d_h_{l}", _matmul_rms_bwd, dproj, plan.weight("w_in", l), sv["x"], p["norm1"][l], dx1,
                       name=f"d_h_{l}")
    if l > 0:
        dwi = plan.run(f"dw_in_{l}", _matmul, sv["h"], dproj, mode="tn", out_dtype=BF, name=f"dw_in_{l}",
                       tm=1024, tn=1280, tk=2048)
        plan.scatter(f"ffn_bwd_{l - 1}", "w_in", l, dwi)
    else:
        half = D_MODEL // 2
        for part, slab in enumerate((sv["h"][:, :half], sv["h"][:, half:])):
            name = f"dw_in_{l}" if part == 0 else f"dw_in_{l}_rest"
            dwi = plan.run(name, _matmul, slab, dproj, mode="tn", out_dtype=BF, name=name, tm=half, tn=1280, tk=2048)
            plan.scatter(f"dw_in_{l}_rest" if part == 0 else "alone", "w_in", l, dwi, part)
    gr["norm1"] = dn1[0]
    return dx, gr


def kernel(x, norm1, w_in, b_gate, w_br, w_o, norm2, w_up, conv_w, conv_b, w_down, norm_f, loss_target, m_norm1, m_w_in, m_b_gate, m_w_br, m_w_o, m_norm2, m_w_up, m_conv_w, m_conv_b, m_w_down, m_norm_f, v_norm1, v_w_in, v_b_gate, v_w_br, v_w_o, v_norm2, v_w_up, v_conv_w, v_conv_b, v_w_down, v_norm_f):
    depth = norm1.shape[0]
    me = 4 * lax.axis_index("x") + 2 * lax.axis_index("y") + lax.axis_index("c")
    shards = dict(w_in=w_in, w_br=w_br, w_o=w_o, w_up=w_up, w_down=w_down)
    moments_m = dict(norm1=m_norm1, w_in=m_w_in, b_gate=m_b_gate, w_br=m_w_br, w_o=m_w_o, norm2=m_norm2,
                     w_up=m_w_up, conv_w=m_conv_w, conv_b=m_conv_b, w_down=m_w_down, norm_f=m_norm_f)
    moments_v = dict(norm1=v_norm1, w_in=v_w_in, b_gate=v_b_gate, w_br=v_w_br, w_o=v_w_o, norm2=v_norm2,
                     w_up=v_w_up, conv_w=v_conv_w, conv_b=v_conv_b, w_down=v_w_down, norm_f=v_norm_f)

    plan = _Plan()
    wb = {n: s.astype(BF) for n, s in shards.items()}
    p = dict(norm1=norm1, b_gate=b_gate, norm2=norm2, conv_b=conv_b)
    cw = _all_gather(conv_w, "gather_conv_w")
    p["conv_w"] = cw.transpose(1, 2, 0, 3).reshape(depth, 3, D_FF)
    plan.landed[("w_in", 0)] = _all_gather_via_sibling(wb["w_in"][0], "gather_w_in_0")
    for l in range(depth):
        plan.ride(f"proj_{l}", ("w_down", l), "gather", wb["w_down"][l])
        plan.ride(f"attn_a_fwd_g0_{l}" if l == 0 else f"down_{l - 1}", ("w_br", l), "gather", wb["w_br"][l])
        plan.ride(f"attn_a_fwd_g0_{l}" if l == 0 else f"down_{l - 1}", ("w_o", l), "gather", wb["w_o"][l])
        plan.ride(f"sb_fwd_{l}", ("w_up", l), "gather", wb["w_up"][l])
        if l + 1 < depth:
            plan.ride(f"up_{l}", ("w_in", l + 1), "gather", wb["w_in"][l + 1])

    xs = x[0]
    saved = []
    for l in range(depth):
        xs, sv = _layer_fwd(xs, p, plan, l)
        saved.append(sv)
    loss_part, dx, dnf = _loss_head(xs, norm_f, loss_target[0])
    loss = lax.psum(loss_part[0, 0], ("x", "y", "c"))

    grads = [None] * depth
    for l in reversed(range(depth)):
        dx, grads[l] = _layer_bwd(dx, saved[l], p, plan, l)
    grad_x = dx[None]
    (key, _, last), = plan.riders.pop("alone")
    plan.landed[key] = _all_to_all(last, "scatter_w_in_rest")
    assert not plan.riders, sorted(plan.riders)

    out_g, out_d, out_m, out_v = {}, {}, {}, {}
    for n in _COL_SHARDED + _ROW_SHARDED:
        shp = shards[n].shape
        flat = (shp[0] * shp[1], shp[2])
        res = None
        for l in range(depth):
            row = l * shp[1]
            for key in sorted(k for k in plan.landed if k[:2] == ("d" + n, l)):
                parts = plan.landed[key]
                res = _adamw(parts, shards[n].reshape(flat), moments_m[n].reshape(flat), moments_v[n].reshape(flat),
                             f"adamw_{n}_{l}_{key[2]}", row0=row, prev=res)
                row += parts.shape[1]
        out_g[n], out_d[n], out_m[n], out_v[n] = [r.reshape(shp) for r in res]

    small = ("norm1", "b_gate", "norm2", "conv_b")
    vecs = [jnp.stack([grads[l][n] for l in range(depth)]).reshape(-1) for n in small]
    vecs.append(dnf.reshape(-1))
    vecs.append(jnp.stack([grads[l]["conv_w"] for l in range(depth)]).reshape(-1))
    sizes = [v.shape[0] for v in vecs]
    flat = jnp.concatenate(vecs)
    n_small = sum(sizes[:-1])
    pad = (-flat.shape[0]) % 1024
    flat = jnp.pad(flat, (0, pad)).reshape(-1, 128)
    allp = _all_gather(flat, "gather_small_grads").reshape(N_DEV, -1)
    rep_w = jnp.concatenate([norm1.reshape(-1), b_gate.reshape(-1), norm2.reshape(-1), conv_b.reshape(-1), norm_f])
    rep_m = jnp.concatenate([moments_m[n].reshape(-1) for n in small] + [m_norm_f])
    rep_v = jnp.concatenate([moments_v[n].reshape(-1) for n in small] + [v_norm_f])
    rows = n_small // 128
    res = _adamw(allp[:, :n_small].reshape(N_DEV, rows, 128), rep_w.reshape(rows, 128), rep_m.reshape(rows, 128),
                 rep_v.reshape(rows, 128), "adamw_small")
    off = 0
    for n, sz in zip(small + ("norm_f",), sizes[:-1]):
        shp = norm_f.shape if n == "norm_f" else p[n].shape
        out_g[n], out_d[n], out_m[n], out_v[n] = [r.reshape(-1)[off:off + sz].reshape(shp) for r in res]
        off += sz
    f = conv_w.shape[2]
    cwp = allp[:, n_small:n_small + sizes[-1]].reshape(N_DEV, depth * 3, D_FF)
    cwp = lax.dynamic_slice_in_dim(cwp, me * f, f, axis=2)
    res = _adamw(cwp, conv_w.reshape(depth * 3, f), m_conv_w.reshape(depth * 3, f), v_conv_w.reshape(depth * 3, f),
                 "adamw_conv_w")
    out_g["conv_w"], out_d["conv_w"], out_m["conv_w"], out_v["conv_w"] = [r.reshape(conv_w.shape) for r in res]

    order = ("norm1", "w_in", "b_gate", "w_br", "w_o", "norm2", "w_up", "conv_w", "conv_b", "w_down", "norm_f")
    return (loss, grad_x, *[out_g[n] for n in order], *[out_d[n] for n in order],
            *[out_m[n] for n in order], *[out_v[n] for n in order])
```

```python
import functools

import jax
import jax.numpy as jnp
from jax import lax
from jax.experimental import pallas as pl
from jax.experimental.pallas import tpu as pltpu

BF = jnp.bfloat16
F32 = jnp.float32

N_DEV = 8
D_MODEL = 1024
HEAD_DIM = 64
DSW_GROUPS = ((128, 1), (512, 4), (2048, 16))
HEADS_PER_GROUP = 4
N_GROUPS = len(DSW_GROUPS)
DSW_HEADS = HEADS_PER_GROUP * N_GROUPS
SB_HEADS = 4
W_A = DSW_HEADS * HEAD_DIM
W_B = SB_HEADS * HEAD_DIM
OUT_A = HEADS_PER_GROUP * HEAD_DIM
N_IN = 3 * W_A + 3 * W_B + 2 * D_MODEL
GATE_OFF = 3 * W_A + 3 * W_B
D_FF = 2816
SB_BLOCK = 256
RMS_EPS = 1e-6
ATT_SCALE = HEAD_DIM ** -0.5
NEG = -1e30
SB_EXIT = -110.0
FFN_CHUNK = 256

ADAM_LR = 0.001
ADAM_B1 = 0.9
ADAM_B2 = 0.999
ADAM_EPS = 1e-08
ADAM_WD = 0.01
ADAM_STEP = 10

HBM_SPEC = pl.BlockSpec(memory_space=pltpu.HBM)
MESH = pl.DeviceIdType.MESH

_NN = (((1,), (0,)), ((), ()))
_NT = (((1,), (1,)), ((), ()))
_TN = (((0,), (0,)), ((), ()))


def _dot(a, b, dn=_NN):
    return lax.dot_general(a, b, dn, preferred_element_type=F32)


def _pick(dim, pref):
    if dim <= pref:
        return dim
    t = (pref // 128) * 128
    while t >= 128:
        if dim % t == 0:
            return t
        t -= 128
    return dim


def _params(*sem):
    return pltpu.CompilerParams(dimension_semantics=sem)


def _peer(k):
    x, y, c = lax.axis_index("x"), lax.axis_index("y"), lax.axis_index("c")
    px = 1 - x if (k >> 2) & 1 else x
    py = 1 - y if (k >> 1) & 1 else y
    pc = 1 - c if k & 1 else c
    return (px, py, pc), 4 * px + 2 * py + pc


def _exchange(kind, x_ref, out_ref, send_sems, recv_sems, local_sem):
    gather = kind == "gather"
    _, me = _peer(0)

    def src(idx):
        return x_ref if gather else x_ref.at[idx]

    def copy(k, dst_idx):
        peer, pidx = _peer(k)
        return pltpu.make_async_remote_copy(
            src_ref=src(pidx), dst_ref=out_ref.at[dst_idx], send_sem=send_sems.at[k - 1],
            recv_sem=recv_sems.at[k - 1], device_id=peer, device_id_type=MESH)

    mine = pltpu.make_async_copy(src(me), out_ref.at[me], local_sem)

    def start():
        mine.start()
        for k in range(1, N_DEV):
            copy(k, me).start()

    def wait():
        for k in range(1, N_DEV):
            copy(k, _peer(k)[1]).wait_recv()
        for k in range(1, N_DEV):
            copy(k, me).wait_send()
        mine.wait()

    return start, wait


_EXCHANGE_SEMS = [pltpu.SemaphoreType.DMA((N_DEV - 1,)), pltpu.SemaphoreType.DMA((N_DEV - 1,)),
                  pltpu.SemaphoreType.DMA]


def _exchange_shape(kind, x):
    return jax.ShapeDtypeStruct(((N_DEV,) + x.shape) if kind == "gather" else x.shape, x.dtype)


def _exchange_alone(kind, x, name):
    def body(x_ref, out_ref, send_sems, recv_sems, local_sem):
        start, wait = _exchange(kind, x_ref, out_ref, send_sems, recv_sems, local_sem)
        start()
        wait()

    return pl.pallas_call(
        body, name=name, out_shape=_exchange_shape(kind, x),
        in_specs=[HBM_SPEC], out_specs=HBM_SPEC, scratch_shapes=list(_EXCHANGE_SEMS),
    )(x)


def _all_gather_via_sibling(x, name):
    def body(x_ref, out_ref, send_sems, recv_sems, local_sem):
        x_, y_, c_ = lax.axis_index("x"), lax.axis_index("y"), lax.axis_index("c")
        me, sibling = (x_, y_, c_), (x_, y_, 1 - c_)
        chips = [(1 - x_, y_), (x_, 1 - y_), (1 - x_, 1 - y_)]

        def slot(px, py, pc):
            return out_ref.at[4 * px + 2 * py + pc]

        def copy(k, block, to, src=None):
            return pltpu.make_async_remote_copy(
                src_ref=slot(*block) if src is None else src, dst_ref=slot(*block), send_sem=send_sems.at[k],
                recv_sem=recv_sems.at[k], device_id=to, device_id_type=MESH)

        mine = pltpu.make_async_copy(x_ref, slot(*me), local_sem)
        mine.start()
        first = [copy(0, me, sibling, src=x_ref)]
        first += [copy(1 + j, me, (*chip, c_), src=x_ref) for j, chip in enumerate(chips)]
        for cp in first:
            cp.start()
        passed = [copy(4 + j, (*chip, c_), sibling) for j, chip in enumerate(chips)]
        for j, chip in enumerate(chips):
            copy(1 + j, (*chip, c_), me).wait_recv()
            passed[j].start()
        copy(0, sibling, me).wait_recv()
        for j, chip in enumerate(chips):
            copy(4 + j, (*chip, 1 - c_), me).wait_recv()
        for cp in first + passed:
            cp.wait_send()
        mine.wait()

    return pl.pallas_call(
        body, name=name, out_shape=_exchange_shape("gather", x),
        in_specs=[HBM_SPEC], out_specs=HBM_SPEC,
        scratch_shapes=[pltpu.SemaphoreType.DMA((N_DEV - 1,)), pltpu.SemaphoreType.DMA((N_DEV - 1,)),
                        pltpu.SemaphoreType.DMA],
    )(x)


def _all_gather(x, name):
    return _exchange_alone("gather", x, name)


def _all_to_all(x, name):
    return _exchange_alone("scatter", x, name)


def _call(body, *, name, grid, in_specs, out_specs, out_shape, args, scratch_shapes=(), sem=None, comm=()):
    single = not isinstance(out_shape, (tuple, list))
    outs = (out_shape,) if single else tuple(out_shape)
    ospecs = (out_specs,) if single else tuple(out_specs)
    if not comm:
        res = pl.pallas_call(
            body, name=name, out_shape=outs, grid=grid, in_specs=list(in_specs), out_specs=ospecs,
            scratch_shapes=list(scratch_shapes), compiler_params=_params(*(sem or ("arbitrary",) * len(grid))),
        )(*args)
        return res
    n_in, n_out, n_scr, nc = len(in_specs), len(outs), len(scratch_shapes), len(comm)

    def wrapped(*refs):
        ins = refs[:n_in]
        cins = refs[n_in:n_in + nc]
        o0 = n_in + nc
        kouts = refs[o0:o0 + n_out]
        couts = refs[o0 + n_out:o0 + n_out + nc]
        s0 = o0 + n_out + nc
        scr = refs[s0:s0 + n_scr]
        sems = refs[s0 + n_scr:]
        ids = [pl.program_id(ax) for ax in range(len(grid))]
        first = functools.reduce(jnp.logical_and, [i == 0 for i in ids])
        last = functools.reduce(jnp.logical_and, [i == g - 1 for i, g in zip(ids, grid)])
        ex = [_exchange(comm[c][0], cins[c], couts[c], *sems[3 * c:3 * c + 3]) for c in range(nc)]

        @pl.when(first)
        def _():
            for start, _ in ex:
                start()

        body(*ins, *kouts, *scr)

        @pl.when(last)
        def _():
            for _, wait in ex:
                wait()

    return pl.pallas_call(
        wrapped, name=name,
        out_shape=outs + tuple(_exchange_shape(k, x) for k, x in comm),
        grid=grid, in_specs=list(in_specs) + [HBM_SPEC] * nc, out_specs=ospecs + (HBM_SPEC,) * nc,
        scratch_shapes=list(scratch_shapes) + list(_EXCHANGE_SEMS) * nc,
        compiler_params=_params(*(("arbitrary",) * len(grid))),
    )(*args, *[x for _, x in comm])


def _matmul(a, b, *, mode, out_dtype, name, tm=512, tn=1024, tk=1024, res=None, comm=()):
    if mode == "nn":
        (M, K), (_, N) = a.shape, b.shape
    elif mode == "nt":
        (M, K), (N, _) = a.shape, b.shape
    else:
        (K, M), (_, N) = a.shape, b.shape
    tm, tn, tk = _pick(M, tm), _pick(N, tn), _pick(K, tk)
    nk = K // tk
    dn = {"nn": _NN, "nt": _NT, "tn": _TN}[mode]

    def body(*refs):
        a_ref, b_ref = refs[0], refs[1]
        r_ref = refs[2] if res is not None else None
        o_ref = refs[3] if res is not None else refs[2]

        def finish(r):
            if res is not None:
                r = r + r_ref[...].astype(F32)
            o_ref[...] = r.astype(out_dtype)

        part = _dot(a_ref[...].astype(BF), b_ref[...].astype(BF), dn)
        if nk == 1:
            finish(part)
            return
        acc = refs[-1]
        k = pl.program_id(2)

        @pl.when(k == 0)
        def _():
            acc[...] = jnp.zeros_like(acc)

        acc[...] += part

        @pl.when(k == nk - 1)
        def _():
            finish(acc[...])

    if mode == "tn":
        a_spec = pl.BlockSpec((tk, tm), lambda j, i, k: (k, i))
    else:
        a_spec = pl.BlockSpec((tm, tk), lambda j, i, k: (i, k))
    if mode == "nt":
        b_spec = pl.BlockSpec((tn, tk), lambda j, i, k: (j, k))
    else:
        b_spec = pl.BlockSpec((tk, tn), lambda j, i, k: (k, j))
    o_spec = pl.BlockSpec((tm, tn), lambda j, i, k: (i, j))
    in_specs = [a_spec, b_spec] + ([o_spec] if res is not None else [])
    args = (a, b) + ((res,) if res is not None else ())
    out = _call(body, name=name, grid=(N // tn, M // tm, nk), in_specs=in_specs, out_specs=o_spec,
                out_shape=jax.ShapeDtypeStruct((M, N), out_dtype), args=args,
                scratch_shapes=[pltpu.VMEM((tm, tn), F32)] if nk > 1 else [],
                sem=("parallel", "parallel", "arbitrary"), comm=comm)
    return out[0], list(out[1:])


def _rms_fwd(x, g, name):
    S, D = x.shape
    tm = _pick(S, 512)

    def body(x_ref, g_ref, h_ref):
        xf = x_ref[...]
        r = lax.rsqrt(jnp.mean(xf * xf, axis=-1, keepdims=True) + RMS_EPS)
        h_ref[...] = (xf * r * g_ref[...]).astype(BF)

    return pl.pallas_call(
        body, name=name,
        out_shape=jax.ShapeDtypeStruct((S, D), BF),
        grid=(S // tm,),
        in_specs=[pl.BlockSpec((tm, D), lambda i: (i, 0)), pl.BlockSpec((1, D), lambda i: (0, 0))],
        out_specs=pl.BlockSpec((tm, D), lambda i: (i, 0)),
        compiler_params=_params("parallel"),
    )(x, g.reshape(1, D))


def _rms_bwd(x, g, dh, dres, name):
    S, D = x.shape
    tm = _pick(S, 512)

    def body(x_ref, g_ref, dh_ref, dres_ref, dx_ref, dg_ref):
        @pl.when(pl.program_id(0) == 0)
        def _():
            dg_ref[...] = jnp.zeros_like(dg_ref)

        xf = x_ref[...]
        r = lax.rsqrt(jnp.mean(xf * xf, axis=-1, keepdims=True) + RMS_EPS)
        xh = xf * r
        dy = dh_ref[...].astype(F32)
        dg_ref[...] += jnp.sum(dy * xh, axis=0, keepdims=True)
        dxh = dy * g_ref[...]
        dx = r * (dxh - xh * jnp.mean(dxh * xh, axis=-1, keepdims=True))
        dx_ref[...] = dres_ref[...] + dx

    row = pl.BlockSpec((tm, D), lambda i: (i, 0))
    vec = pl.BlockSpec((1, D), lambda i: (0, 0))
    return pl.pallas_call(
        body, name=name,
        out_shape=(jax.ShapeDtypeStruct((S, D), F32), jax.ShapeDtypeStruct((1, D), F32)),
        grid=(S // tm,),
        in_specs=[row, vec, row, row], out_specs=(row, vec),
        compiler_params=_params("arbitrary"),
    )(x, g.reshape(1, D), dh, dres)


def _matmul_rms_bwd(dy, w, x, g, dres, *, name, tm=512, comm=()):
    S, K = dy.shape
    D = w.shape[0]
    tm = _pick(S, tm)

    def body(dy_ref, w_ref, x_ref, g_ref, dres_ref, dx_ref, dg_ref):
        @pl.when(pl.program_id(0) == 0)
        def _():
            dg_ref[...] = jnp.zeros_like(dg_ref)

        dh = _dot(dy_ref[...].astype(BF), w_ref[...], _NT)
        xf = x_ref[...]
        r = lax.rsqrt(jnp.mean(xf * xf, axis=-1, keepdims=True) + RMS_EPS)
        xh = xf * r
        dg_ref[...] += jnp.sum(dh * xh, axis=0, keepdims=True)
        dxh = dh * g_ref[...]
        dx_ref[...] = dres_ref[...] + r * (dxh - xh * jnp.mean(dxh * xh, axis=-1, keepdims=True))

    row = pl.BlockSpec((tm, D), lambda i: (i, 0))
    vec = pl.BlockSpec((1, D), lambda i: (0, 0))
    res = _call(body, name=name, grid=(S // tm,),
                in_specs=[pl.BlockSpec((tm, K), lambda i: (i, 0)), pl.BlockSpec((D, K), lambda i: (0, 0)), row, vec, row],
                out_specs=(row, vec),
                out_shape=(jax.ShapeDtypeStruct((S, D), F32), jax.ShapeDtypeStruct((1, D), F32)),
                args=(dy, w, x, g.reshape(1, D), dres), comm=comm)
    return res[0], res[1], list(res[2:])


def _loss_head(x, g, target):
    S, D = x.shape
    tm = _pick(S, 512)

    def body(x_ref, g_ref, t_ref, loss_ref, dx_ref, dg_ref):
        @pl.when(pl.program_id(0) == 0)
        def _():
            dg_ref[...] = jnp.zeros_like(dg_ref)
            loss_ref[...] = jnp.zeros_like(loss_ref)

        xf = x_ref[...]
        gg = g_ref[...]
        r = lax.rsqrt(jnp.mean(xf * xf, axis=-1, keepdims=True) + RMS_EPS)
        xh = xf * r
        err = xh * gg - t_ref[...]
        per_tok = jnp.mean(err * err, axis=-1, keepdims=True)
        loss_ref[...] += 0.5 * jnp.sum(per_tok, axis=0, keepdims=True)
        dy = err * (1.0 / D)
        dg_ref[...] += jnp.sum(dy * xh, axis=0, keepdims=True)
        dxh = dy * gg
        dx_ref[...] = r * (dxh - xh * jnp.mean(dxh * xh, axis=-1, keepdims=True))

    row = pl.BlockSpec((tm, D), lambda i: (i, 0))
    vec = pl.BlockSpec((1, D), lambda i: (0, 0))
    one = pl.BlockSpec((1, 1), lambda i: (0, 0))
    return pl.pallas_call(
        body, name="loss_head",
        out_shape=(jax.ShapeDtypeStruct((1, 1), F32), jax.ShapeDtypeStruct((S, D), F32),
                   jax.ShapeDtypeStruct((1, D), F32)),
        grid=(S // tm,),
        in_specs=[row, vec, row], out_specs=(one, row, vec),
        compiler_params=_params("arbitrary"),
    )(x, g.reshape(1, D), target)


def _slopes(g):
    return [2.0 ** (-8.0 * (HEADS_PER_GROUP * g + j + 1) / DSW_HEADS) for j in range(HEADS_PER_GROUP)]


def _band_masks(W):
    row = lax.broadcasted_iota(jnp.int32, (W, W), 0)
    col = lax.broadcasted_iota(jnp.int32, (W, W), 1)
    d_cur = row - col
    d_prev = d_cur + W
    return d_cur, d_prev, d_cur >= 0, d_cur <= 0


def _band_specs(W, nb, per):
    def cur(c):
        return pl.BlockSpec((None, per * W, OUT_A), lambda r, n: (r, n, c))

    def prev(c):
        return pl.BlockSpec((None, W, OUT_A), lambda r, n: (r, jnp.maximum(per * n - 1, 0), c))

    def nxt(c):
        return pl.BlockSpec((None, W, OUT_A), lambda r, n: (r, jnp.minimum(per * (n + 1), nb - 1), c))

    return cur, prev, nxt


def _blocks_per_step(nb):
    return 4 if nb % 4 == 0 else 2 if nb % 2 == 0 else 1


def _head_stack(W):
    H, hd = HEADS_PER_GROUP, HEAD_DIM
    lane_head = lax.broadcasted_iota(jnp.int32, (W, OUT_A), 1) // hd

    def stack(x):
        return jnp.concatenate([jnp.where(lane_head == h, x, jnp.zeros_like(x)) for h in range(H)], axis=0)

    def unstack(y):
        out = jnp.where(lane_head == 0, y[0:W], 0.0)
        for h in range(1, H):
            out = jnp.where(lane_head == h, y[h * W:(h + 1) * W], out)
        return out

    def column(ref, rows, off=0):
        return jnp.concatenate([ref[rows, h * hd + off:h * hd + off + 1] for h in range(H)], axis=0)

    def tile(x):
        return jnp.concatenate([x] * H, axis=0)

    return stack, unstack, column, tile


def _stacked_bias(W, slopes, dil):
    d_cur, d_prev, m_cur, m_prev = _band_masks(W)
    b_cur = jnp.concatenate([(s * dil) * d_cur.astype(F32) for s in slopes], axis=0)
    b_prev = jnp.concatenate([(s * dil) * d_prev.astype(F32) for s in slopes], axis=0)
    H = len(slopes)
    return b_cur, b_prev, jnp.concatenate([m_cur] * H, axis=0), jnp.concatenate([m_prev] * H, axis=0)


def _attn_a_fwd(qkv, cols, g, comm=()):
    win, dil = DSW_GROUPS[g]
    W = win // dil
    d, L, _ = qkv[0].shape
    nb = L // W
    per = _blocks_per_step(nb)
    slopes = _slopes(g)

    def body(q_ref, kp_ref, kc_ref, vp_ref, vc_ref, o_ref, l_ref):
        n = pl.program_id(1)
        stack, unstack, _, _ = _head_stack(W)
        b_cur, b_prev, m_cur, m_prev = _stacked_bias(W, slopes, dil)
        m_first = jnp.logical_and(m_prev, n > 0)
        for b in range(per):
            rows = slice(b * W, (b + 1) * W)
            before = slice((b - 1) * W, b * W)
            qs = stack(q_ref[rows, :])
            kc, vc = kc_ref[rows, :], vc_ref[rows, :]
            kp, vp = (kp_ref[...], vp_ref[...]) if b == 0 else (kc_ref[before, :], vc_ref[before, :])
            s_c = jnp.where(m_cur, _dot(qs, kc, _NT) * ATT_SCALE - b_cur, NEG)
            s_p = jnp.where(m_first if b == 0 else m_prev, _dot(qs, kp, _NT) * ATT_SCALE - b_prev, NEG)
            m = jnp.maximum(jnp.max(s_c, axis=1, keepdims=True), jnp.max(s_p, axis=1, keepdims=True))
            p_c = jnp.exp(s_c - m)
            p_p = jnp.exp(s_p - m)
            den = jnp.sum(p_c, axis=1, keepdims=True) + jnp.sum(p_p, axis=1, keepdims=True)
            pv = _dot(p_c.astype(BF), vc) + _dot(p_p.astype(BF), vp)
            o_ref[rows, :] = unstack(pv / den)
            l_ref[rows, :] = unstack(jnp.broadcast_to(m + jnp.log(den), pv.shape))

    cur, prev, _ = _band_specs(W, nb, per)
    out = jax.ShapeDtypeStruct((d, L, OUT_A), F32)
    res = _call(body, name=f"attn_a_fwd_g{g}", grid=(d, nb // per),
                in_specs=[cur(cols[0]), prev(cols[1]), cur(cols[1]), prev(cols[2]), cur(cols[2])],
                out_specs=(cur(0), cur(0)), out_shape=(out, out),
                args=(qkv[0], qkv[1], qkv[1], qkv[2], qkv[2]), sem=("parallel", "parallel"), comm=comm)
    return res[0], res[1], list(res[2:])


def _attn_a_bwd(qkv, cols, do, stats, g):
    win, dil = DSW_GROUPS[g]
    W = win // dil
    d, L, _ = qkv[0].shape
    nb = L // W
    per = _blocks_per_step(nb)
    nsteps = nb // per
    slopes = _slopes(g)

    def body(q_ref, qn_ref, kp_ref, kc_ref, vp_ref, vc_ref, do_ref, don_ref, st_ref, stn_ref,
             dq_ref, dk_ref, dv_ref):
        n = pl.program_id(1)
        stack, unstack, column, _ = _head_stack(W)
        b_cur, b_prev, m_cur, m_prev = _stacked_bias(W, slopes, dil)
        m_first = jnp.logical_and(m_prev, n > 0)
        m_last = jnp.logical_and(m_prev, n < nsteps - 1)
        everything = slice(None)
        for b in range(per):
            rows = slice(b * W, (b + 1) * W)
            before = slice((b - 1) * W, b * W)
            after = slice((b + 1) * W, (b + 2) * W)
            first, last = b == 0, b == per - 1
            qs = stack(q_ref[rows, :])
            qn = stack(qn_ref[...] if last else q_ref[after, :])
            dos = stack(do_ref[rows, :])
            don = stack(don_ref[...] if last else do_ref[after, :])
            kc, vc = kc_ref[rows, :], vc_ref[rows, :]
            kp, vp = (kp_ref[...], vp_ref[...]) if first else (kc_ref[before, :], vc_ref[before, :])
            lse_c, dsum_c = column(st_ref, rows), column(st_ref, rows, STAT_OFF)
            lse_n = column(stn_ref, everything) if last else column(st_ref, after)
            dsum_n = column(stn_ref, everything, STAT_OFF) if last else column(st_ref, after, STAT_OFF)
            m_p = m_first if first else m_prev
            m_n = m_last if last else m_prev
            p_cc = jnp.exp(jnp.where(m_cur, _dot(qs, kc, _NT) * ATT_SCALE - b_cur, NEG) - lse_c)
            p_cp = jnp.exp(jnp.where(m_p, _dot(qs, kp, _NT) * ATT_SCALE - b_prev, NEG) - lse_c)
            p_nc = jnp.exp(jnp.where(m_n, _dot(qn, kc, _NT) * ATT_SCALE - b_prev, NEG) - lse_n)
            ds_cc = (p_cc * (_dot(dos, vc, _NT) - dsum_c) * ATT_SCALE).astype(BF)
            ds_cp = (p_cp * (_dot(dos, vp, _NT) - dsum_c) * ATT_SCALE).astype(BF)
            ds_nc = (p_nc * (_dot(don, vc, _NT) - dsum_n) * ATT_SCALE).astype(BF)
            dq_ref[rows, :] = unstack(_dot(ds_cc, kc) + _dot(ds_cp, kp)).astype(BF)
            dk_ref[rows, :] = (_dot(ds_cc, qs, _TN) + _dot(ds_nc, qn, _TN)).astype(BF)
            dv_ref[rows, :] = (_dot(p_cc.astype(BF), dos, _TN) + _dot(p_nc.astype(BF), don, _TN)).astype(BF)

    cur, prev, nxt = _band_specs(W, nb, per)
    out = jax.ShapeDtypeStruct((d, L, OUT_A), BF)
    cq, ck, cv = cols
    return pl.pallas_call(
        body, name=f"attn_a_bwd_g{g}",
        out_shape=(out, out, out),
        grid=(d, nsteps),
        in_specs=[cur(cq), nxt(cq), prev(ck), cur(ck), prev(cv), cur(cv), cur(0), nxt(0), cur(0), nxt(0)],
        out_specs=(cur(0), cur(0), cur(0)),
        compiler_params=_params("parallel", "parallel"),
    )(qkv[0], qkv[0], qkv[1], qkv[1], qkv[2], qkv[2], do, do, stats, stats)


SB_PAIR = 2
SB_QUAD = 4
SB_QROWS = SB_BLOCK


def _softplus_parts(z):
    e = jnp.exp(-jnp.abs(z))
    log1p_e = jnp.where(e < 1e-4, e, jnp.log(1.0 + e))
    return e, jnp.maximum(z, 0.0) + log1p_e


def _split_dot(x, t):
    hi = x.astype(BF)
    lo = (x - hi.astype(F32)).astype(BF)
    return _dot(hi, t) + _dot(lo, t)


def _sb_block(qh, kk, causal, r_run, tri_incl):
    z = _dot(qh, kk, _NT)
    e, sp = _softplus_parts(z)
    ls = jnp.where(causal, -sp, 0.0)
    cin = _split_dot(ls, tri_incl)
    a = jnp.where(causal, jnp.exp(z + cin + r_run), 0.0)
    return z, e, cin, a


def _sb_specs(S, pair=SB_PAIR):
    Q, hd = SB_QROWS, HEAD_DIM
    lanes = pair * hd
    qc = (3 * W_A) // lanes
    kc = (3 * W_A + W_B) // lanes
    vc = (3 * W_A + 2 * W_B) // lanes
    q_spec = pl.BlockSpec((Q, lanes), lambda p, i: (i, qc + p))
    k_spec = pl.BlockSpec((S, lanes), lambda p, i: (0, kc + p))
    v_spec = pl.BlockSpec((S, lanes), lambda p, i: (0, vc + p))
    o_spec = pl.BlockSpec((Q, lanes), lambda p, i: (i, p))
    full = pl.BlockSpec((S, lanes), lambda p, i: (0, p))
    return q_spec, k_spec, v_spec, o_spec, full


def _sb_stack(pair=SB_PAIR):
    Q, hd = SB_QROWS, HEAD_DIM
    lane_head = lax.broadcasted_iota(jnp.int32, (Q, pair * hd), 1) // hd

    def stack(x):
        return jnp.concatenate([jnp.where(lane_head == h, x, jnp.zeros_like(x)) for h in range(pair)], axis=0)

    def unstack(y):
        out = jnp.where(lane_head == 0, y[0:Q], 0.0)
        for h in range(1, pair):
            out = jnp.where(lane_head == h, y[h * Q:(h + 1) * Q], out)
        return out

    return stack, unstack


def _sb_iotas(i, pair=SB_PAIR):
    B, Q = SB_BLOCK, SB_QROWS
    row = lax.broadcasted_iota(jnp.int32, (Q, B), 0) + i * Q
    col = lax.broadcasted_iota(jnp.int32, (Q, B), 1)
    ahead = jnp.concatenate([col - row] * pair, axis=0)
    tr = lax.broadcasted_iota(jnp.int32, (B, B), 0)
    tc = lax.broadcasted_iota(jnp.int32, (B, B), 1)
    return ahead, tr, tc


def _sb_fwd(proj, comm=()):
    S = proj.shape[0]
    B, Q, hd = SB_BLOCK, SB_QROWS, HEAD_DIM
    nq = S // Q
    P = SB_QUAD
    R = P * Q
    q_spec, k_spec, v_spec, o_spec, _ = _sb_specs(S, P)

    def body(q_ref, k_ref, v_ref, o_ref):
        i = pl.program_id(1)
        ahead, tr, tc = _sb_iotas(i, P)
        tri_incl = (tr >= tc).astype(BF)
        stack, unstack = _sb_stack(P)
        qs = stack(q_ref[...] * ATT_SCALE)

        def cond(c):
            return jnp.logical_and(c[0] >= 0, c[-1] > SB_EXIT)

        def step(c):
            kb, r_run, acc, _ = c
            off = pl.multiple_of(kb * B, B)
            causal = ahead < -kb * B
            _, _, cin, a = _sb_block(qs, k_ref[pl.ds(off, B), :], causal, r_run, tri_incl)
            acc = acc + _dot(a.astype(BF), v_ref[pl.ds(off, B), :])
            r_run = r_run + cin[:, 0:1]
            return kb - 1, r_run, acc, jnp.max(r_run)

        init = (i, jnp.zeros((R, 1), F32), jnp.zeros((R, P * hd), F32), jnp.float32(0.0))
        fin = lax.while_loop(cond, step, init)
        o_ref[...] = unstack(fin[2])

    res = _call(body, name="sb_fwd", grid=(SB_HEADS // P, nq), in_specs=[q_spec, k_spec, v_spec],
                out_specs=o_spec, out_shape=jax.ShapeDtypeStruct((S, W_B), F32), args=(proj, proj, proj),
                sem=("parallel", "parallel"), comm=comm)
    return res[0], list(res[1:])


def _sb_bwd(proj, do, o, comm=()):
    S = proj.shape[0]
    B, Q, hd = SB_BLOCK, SB_QROWS, HEAD_DIM
    nq = S // Q
    R = SB_PAIR * Q
    q_spec, k_spec, v_spec, o_spec, full = _sb_specs(S)

    def body(q_ref, k_ref, v_ref, do_ref, o_ref, dq_ref, dk_ref, dv_ref):
        i = pl.program_id(1)

        @pl.when(i == 0)
        def _():
            dk_ref[...] = jnp.zeros_like(dk_ref)
            dv_ref[...] = jnp.zeros_like(dv_ref)

        ahead, tr, tc = _sb_iotas(i)
        tri_incl = (tr >= tc).astype(BF)
        tri_strict = (tr > tc).astype(BF)
        stack, unstack = _sb_stack()
        qs = stack(q_ref[...] * ATT_SCALE)
        dobs = stack(do_ref[...])
        o_all = o_ref[...]
        dsum = jnp.sum(dobs.astype(F32) * jnp.concatenate([o_all] * SB_PAIR, axis=0), axis=1, keepdims=True)

        def cond(c):
            return jnp.logical_and(c[0] >= 0, c[-1] > SB_EXIT)

        def step(c):
            kb, r_run, g_run, dq, _ = c
            off = pl.multiple_of(kb * B, B)
            causal = ahead < -kb * B
            kk = k_ref[pl.ds(off, B), :]
            vv = v_ref[pl.ds(off, B), :]
            z, e, cin, a = _sb_block(qs, kk, causal, r_run, tri_incl)
            a16 = a.astype(BF)
            gmat = a16.astype(F32) * _dot(dobs, vv, _NT)
            later = _split_dot(gmat, tri_strict)
            pfx = dsum - g_run - later
            sig = jnp.where(z >= 0, 1.0, e) / (1.0 + e)
            dz = jnp.where(causal, gmat - sig * pfx, 0.0).astype(BF)
            dq = dq + _dot(dz, kk)
            dk_ref[pl.ds(off, B), :] += _dot(dz, qs, _TN)
            dv_ref[pl.ds(off, B), :] += _dot(a16, dobs, _TN)
            g_run = g_run + jnp.sum(gmat, axis=1, keepdims=True)
            r_run = r_run + cin[:, 0:1]
            return kb - 1, r_run, g_run, dq, jnp.max(r_run)

        init = (i, jnp.zeros((R, 1), F32), jnp.zeros((R, 1), F32), jnp.zeros((R, SB_PAIR * hd), F32),
                jnp.float32(0.0))
        fin = lax.while_loop(cond, step, init)
        dq_ref[...] = unstack(fin[3]) * ATT_SCALE

    out = jax.ShapeDtypeStruct((S, W_B), F32)
    res = _call(body, name="sb_bwd", grid=(SB_HEADS // SB_PAIR, nq), in_specs=[q_spec, k_spec, v_spec, o_spec, o_spec],
                out_specs=(o_spec, full, full), out_shape=(out, out, out), args=(proj, proj, proj, do, o), comm=comm)
    return res[0], res[1], res[2], list(res[3:])


def _merge_fwd(o_g, l_g, o_b, proj, b_gate, w_br):
    S = o_b.shape[0]
    D = D_MODEL
    tm = _pick(S, 256)
    gcol = GATE_OFF // D

    def body(o0, o1, o2, l0, l1, l2, ob_ref, ga_ref, gb_ref, bg_ref, w_ref, mg_ref, oa_ref, lse_ref):
        la, lb, lc = l0[...], l1[...], l2[...]
        mx = jnp.maximum(jnp.maximum(la, lb), lc)
        ea, eb, ec = jnp.exp(la - mx), jnp.exp(lb - mx), jnp.exp(lc - mx)
        den = ea + eb + ec
        oa = (ea * o0[...] + eb * o1[...] + ec * o2[...]) / den
        oa_ref[...] = oa
        lse_ref[...] = mx + jnp.log(den)
        ya = _dot(oa.astype(BF), w_ref[0:OUT_A, :])
        yb = _dot(ob_ref[...].astype(BF), w_ref[OUT_A:OUT_A + W_B, :])
        bg = bg_ref[...]
        g_a = jax.nn.sigmoid(ga_ref[...].astype(F32) + bg[:, 0:D])
        g_b = jax.nn.sigmoid(gb_ref[...].astype(F32) + bg[:, D:2 * D])
        mg_ref[...] = (g_a * ya + g_b * yb).astype(BF)

    nar = pl.BlockSpec((tm, OUT_A), lambda i: (i, 0))
    wide = pl.BlockSpec((tm, D), lambda i: (i, 0))
    return pl.pallas_call(
        body, name="merge_fwd",
        out_shape=(jax.ShapeDtypeStruct((S, D), BF), jax.ShapeDtypeStruct((S, OUT_A), F32),
                   jax.ShapeDtypeStruct((S, OUT_A), F32)),
        grid=(S // tm,),
        in_specs=[nar] * 7 + [pl.BlockSpec((tm, D), lambda i: (i, gcol)),
                              pl.BlockSpec((tm, D), lambda i: (i, gcol + 1)),
                              pl.BlockSpec((1, 2 * D), lambda i: (0, 0)),
                              pl.BlockSpec((OUT_A + W_B, D), lambda i: (0, 0))],
        out_specs=(wide, nar, nar),
        compiler_params=_params("parallel"),
    )(*o_g, *l_g, o_b, proj, proj, b_gate.reshape(1, 2 * D), w_br)


def _merge_wo(o_g, l_g, o_b, proj, b_gate, w_br, w_o, x, name, comm=()):
    S = o_b.shape[0]
    D = D_MODEL
    tm = _pick(S, 256)
    gcol = GATE_OFF // D

    def body(o0, o1, o2, l0, l1, l2, ob_ref, ga_ref, gb_ref, bg_ref, w_ref, wo_ref, x_ref,
             x1_ref, mg_ref, oa_ref, lse_ref):
        la, lb, lc = l0[...], l1[...], l2[...]
        mx = jnp.maximum(jnp.maximum(la, lb), lc)
        ea, eb, ec = jnp.exp(la - mx), jnp.exp(lb - mx), jnp.exp(lc - mx)
        den = ea + eb + ec
        oa = (ea * o0[...] + eb * o1[...] + ec * o2[...]) / den
        oa_ref[...] = oa
        lse_ref[...] = mx + jnp.log(den)
        oa16 = oa.astype(BF)
        ob16 = ob_ref[...].astype(BF)
        acc = x_ref[...]
        for c0 in range(0, D, FFN_CHUNK):
            cs = slice(c0, c0 + FFN_CHUNK)
            ya = _dot(oa16, w_ref[0:OUT_A, cs])
            yb = _dot(ob16, w_ref[OUT_A:OUT_A + W_B, cs])
            g_a = jax.nn.sigmoid(ga_ref[:, cs].astype(F32) + bg_ref[:, cs])
            g_b = jax.nn.sigmoid(gb_ref[:, cs].astype(F32) + bg_ref[:, D + c0:D + c0 + FFN_CHUNK])
            mg = (g_a * ya + g_b * yb).astype(BF)
            mg_ref[:, cs] = mg
            acc = acc + _dot(mg, wo_ref[cs, :])
        x1_ref[...] = acc

    nar = pl.BlockSpec((tm, OUT_A), lambda i: (i, 0))
    wide = pl.BlockSpec((tm, D), lambda i: (i, 0))
    res = _call(body, name=name, grid=(S // tm,),
                in_specs=[nar] * 7 + [pl.BlockSpec((tm, D), lambda i: (i, gcol)),
                                      pl.BlockSpec((tm, D), lambda i: (i, gcol + 1)),
                                      pl.BlockSpec((1, 2 * D), lambda i: (0, 0)),
                                      pl.BlockSpec((OUT_A + W_B, D), lambda i: (0, 0)),
                                      pl.BlockSpec((D, D), lambda i: (0, 0)), wide],
                out_specs=(wide, wide, nar, nar),
                out_shape=(jax.ShapeDtypeStruct((S, D), F32), jax.ShapeDtypeStruct((S, D), BF),
                           jax.ShapeDtypeStruct((S, OUT_A), F32), jax.ShapeDtypeStruct((S, OUT_A), F32)),
                args=(*o_g, *l_g, o_b, proj, proj, b_gate.reshape(1, 2 * D), w_br, w_o, x), comm=comm)
    return res[0], res[1], res[2], res[3], list(res[4:])


STAT_OFF = HEAD_DIM // 2


def _merge_bwd(dx, w_o, oa, ob, lse, proj, b_gate, w_br):
    S = ob.shape[0]
    D = D_MODEL
    tm = _pick(S, 256)
    gcol = GATE_OFF // D

    def body(dx_ref, wo_ref, oa_ref, ob_ref, l_ref, ga_ref, gb_ref, bg_ref, w_ref,
             dya_ref, dyb_ref, doa_ref, dob_ref, st_ref, dg_ref, dbg_ref):
        @pl.when(pl.program_id(0) == 0)
        def _():
            dbg_ref[...] = jnp.zeros_like(dbg_ref)

        dx16 = dx_ref[...].astype(BF)
        oa = oa_ref[...]
        oa16 = oa.astype(BF)
        ob16 = ob_ref[...].astype(BF)
        doa = jnp.zeros((tm, OUT_A), F32)
        dob = jnp.zeros((tm, W_B), F32)
        for c0 in range(0, D, FFN_CHUNK):
            cs = slice(c0, c0 + FFN_CHUNK)
            cs2 = slice(D + c0, D + c0 + FFN_CHUNK)
            wa = w_ref[0:OUT_A, cs]
            wb = w_ref[OUT_A:OUT_A + W_B, cs]
            dm = _dot(dx16, wo_ref[cs, :], _NT)
            ya = _dot(oa16, wa)
            yb = _dot(ob16, wb)
            g_a = jax.nn.sigmoid(ga_ref[:, cs].astype(F32) + bg_ref[:, cs])
            g_b = jax.nn.sigmoid(gb_ref[:, cs].astype(F32) + bg_ref[:, cs2])
            dga = dm * ya * g_a * (1.0 - g_a)
            dgb = dm * yb * g_b * (1.0 - g_b)
            dg_ref[:, cs] = dga.astype(BF)
            dg_ref[:, cs2] = dgb.astype(BF)
            dbg_ref[:, cs] += jnp.sum(dga, axis=0, keepdims=True)
            dbg_ref[:, cs2] += jnp.sum(dgb, axis=0, keepdims=True)
            dya = (dm * g_a).astype(BF)
            dyb = (dm * g_b).astype(BF)
            dya_ref[:, cs] = dya
            dyb_ref[:, cs] = dyb
            doa = doa + _dot(dya, wa, _NT)
            dob = dob + _dot(dyb, wb, _NT)
        doa = doa.astype(BF)
        doa_ref[...] = doa
        dob_ref[...] = dob.astype(BF)
        r = lax.broadcasted_iota(jnp.int32, (OUT_A, OUT_A), 0) // HEAD_DIM
        c = lax.broadcasted_iota(jnp.int32, (OUT_A, OUT_A), 1) // HEAD_DIM
        dsum = _split_dot(doa.astype(F32) * oa, (r == c).astype(BF))
        lane = lax.broadcasted_iota(jnp.int32, dsum.shape, 1) % HEAD_DIM
        st_ref[...] = jnp.where(lane < STAT_OFF, l_ref[...], dsum)

    nar = pl.BlockSpec((tm, OUT_A), lambda i: (i, 0))
    wide = pl.BlockSpec((tm, D), lambda i: (i, 0))
    wide2 = pl.BlockSpec((tm, 2 * D), lambda i: (i, 0))
    vec2 = pl.BlockSpec((1, 2 * D), lambda i: (0, 0))
    return pl.pallas_call(
        body, name="merge_bwd",
        out_shape=(jax.ShapeDtypeStruct((S, D), BF), jax.ShapeDtypeStruct((S, D), BF),
                   jax.ShapeDtypeStruct((S, OUT_A), BF), jax.ShapeDtypeStruct((S, W_B), BF),
                   jax.ShapeDtypeStruct((S, OUT_A), F32), jax.ShapeDtypeStruct((S, 2 * D), BF),
                   jax.ShapeDtypeStruct((1, 2 * D), F32)),
        grid=(S // tm,),
        in_specs=[wide, pl.BlockSpec((D, D), lambda i: (0, 0)), nar, nar, nar,
                  pl.BlockSpec((tm, D), lambda i: (i, gcol)), pl.BlockSpec((tm, D), lambda i: (i, gcol + 1)), vec2,
                  pl.BlockSpec((OUT_A + W_B, D), lambda i: (0, 0))],
        out_specs=(wide, wide, nar, nar, nar, wide2, vec2),
        compiler_params=_params("arbitrary"),
    )(dx, w_o, oa, ob, lse, proj, proj, b_gate.reshape(1, 2 * D), w_br)


_SQRT_HALF = 0.7071067811865476
_INV_SQRT_2PI = 0.3989422804014327


def _gelu_parts(a):
    cdf = 0.5 * (1.0 + lax.erf(a * _SQRT_HALF))
    pdf = _INV_SQRT_2PI * jnp.exp(-0.5 * a * a)
    return cdf, pdf


def _shift_down(a, halo, k):
    rows = lax.broadcasted_iota(jnp.int32, a.shape, 0)
    out = pltpu.roll(a, k, 0)
    for r in range(k):
        out = jnp.where(rows == r, halo[8 - k + r:8 - k + r + 1, :], out)
    return out


def _shift_up(a, halo, k):
    n = a.shape[0]
    rows = lax.broadcasted_iota(jnp.int32, a.shape, 0)
    out = pltpu.roll(a, n - k, 0)
    for r in range(k):
        out = jnp.where(rows == n - k + r, halo[r:r + 1, :], out)
    return out


def _conv_in(a_ref, h_ref, first):
    a = a_ref[...].astype(F32)
    halo = jnp.where(first, 0.0, h_ref[...].astype(F32))
    return a, _shift_down(a, halo, 1), _shift_down(a, halo, 2)


def _ffn_specs(S, tm):
    F = D_FF
    t8 = tm // 8
    a_spec = pl.BlockSpec((tm, F), lambda i: (i, 0))
    v_spec = pl.BlockSpec((tm, F), lambda i: (i, 1))
    halo_prev = pl.BlockSpec((8, F), lambda i: (jnp.maximum(i * t8 - 1, 0), 0))
    return a_spec, v_spec, halo_prev


def _ffn_act_fwd(up, conv_w, conv_b):
    S = up.shape[0]
    F = D_FF
    tm = _pick(S, 256)
    a_spec, v_spec, halo_prev = _ffn_specs(S, tm)

    def body(a_ref, h_ref, v_ref, w_ref, b_ref, act_ref):
        a0, a1, a2 = _conv_in(a_ref, h_ref, pl.program_id(0) == 0)
        w = w_ref[...]
        ac = b_ref[...] + w[0:1, :] * a2 + w[1:2, :] * a1 + w[2:3, :] * a0
        cdf, _ = _gelu_parts(ac)
        act_ref[...] = (ac * cdf * v_ref[...].astype(F32)).astype(BF)

    return pl.pallas_call(
        body, name="ffn_act_fwd",
        out_shape=jax.ShapeDtypeStruct((S, F), BF),
        grid=(S // tm,),
        in_specs=[a_spec, halo_prev, v_spec, pl.BlockSpec((3, F), lambda i: (0, 0)),
                  pl.BlockSpec((1, F), lambda i: (0, 0))],
        out_specs=a_spec,
        compiler_params=_params("parallel"),
    )(up, up, up, conv_w, conv_b.reshape(1, F))


def _ffn_down(up, conv_w, conv_b, w_down, res, *, name, comm=()):
    S = up.shape[0]
    F = D_FF
    D = w_down.shape[1]
    tm = _pick(S, 256)
    a_spec, v_spec, halo_prev = _ffn_specs(S, tm)

    def body(a_ref, h_ref, v_ref, w_ref, b_ref, wd_ref, r_ref, o_ref, act_ref, ac_ref):
        first = pl.program_id(0) == 0
        acc = r_ref[...]
        for c0 in range(0, F, FFN_CHUNK):
            cs = slice(c0, c0 + FFN_CHUNK)
            a = a_ref[:, cs].astype(F32)
            halo = jnp.where(first, 0.0, h_ref[:, cs].astype(F32))
            w = w_ref[:, cs]
            ac = b_ref[:, cs] + w[0:1, :] * _shift_down(a, halo, 2) + w[1:2, :] * _shift_down(a, halo, 1) + w[2:3, :] * a
            ac_ref[:, cs] = ac.astype(BF)
            cdf, _ = _gelu_parts(ac)
            act = (ac * cdf * v_ref[:, cs].astype(F32)).astype(BF)
            act_ref[:, cs] = act
            acc = acc + _dot(act, wd_ref[cs, :])
        o_ref[...] = acc

    row = pl.BlockSpec((tm, D), lambda i: (i, 0))
    res_ = _call(body, name=name, grid=(S // tm,),
                 in_specs=[a_spec, halo_prev, v_spec, pl.BlockSpec((3, F), lambda i: (0, 0)),
                           pl.BlockSpec((1, F), lambda i: (0, 0)), pl.BlockSpec((F, D), lambda i: (0, 0)), row],
                 out_specs=(row, a_spec, a_spec),
                 out_shape=(jax.ShapeDtypeStruct((S, D), F32), jax.ShapeDtypeStruct((S, F), BF),
                            jax.ShapeDtypeStruct((S, F), BF)),
                 args=(up, up, up, conv_w, conv_b.reshape(1, F), w_down, res), comm=comm)
    return res_[0], res_[1], res_[2], list(res_[3:])


def _shift_up_pair(a, nxt):
    n = a.shape[0]
    r8 = lax.broadcasted_iota(jnp.int32, (8,) + a.shape[1:], 0)
    out = []
    for k in (1, 2):
        rolled = pltpu.roll(a, n - k, 0)
        tail = jnp.where(r8 >= 8 - k, pltpu.roll(nxt, 8 - k, 0), rolled[n - 8:n])
        out.append(jnp.concatenate([rolled[0:n - 8], tail], axis=0))
    return out


def _ffn_bwd(dx, w_down, up, ac, conv_w, comm=()):
    S = up.shape[0]
    F = D_FF
    D = dx.shape[1]
    tm = _pick(S, 256)
    t8 = tm // 8
    nt = S // tm
    a_spec, v_spec, _ = _ffn_specs(S, tm)

    def nxt(width, col):
        return pl.BlockSpec((8, width), lambda i: (jnp.minimum((i + 1) * t8, S // 8 - 1), col))

    def body(dx_ref, dxn_ref, wd_ref, ac_ref, acn_ref, a_ref, v_ref, vn_ref, w_ref, dup_ref, dw_ref, db_ref):
        i = pl.program_id(0)

        @pl.when(i == 0)
        def _():
            dw_ref[...] = jnp.zeros_like(dw_ref)
            db_ref[...] = jnp.zeros_like(db_ref)

        dx16 = dx_ref[...].astype(BF)
        dxn16 = dxn_ref[...].astype(BF)
        last = i == nt - 1

        def dconv(dact, ac, v):
            cdf, pdf = _gelu_parts(ac)
            return cdf, dact * v * (cdf + ac * pdf)

        for c0 in range(0, F, FFN_CHUNK):
            cs = slice(c0, c0 + FFN_CHUNK)
            wd = wd_ref[cs, :]
            dact = _dot(dx16, wd, _NT)
            ac = ac_ref[:, cs].astype(F32)
            cdf, dac = dconv(dact, ac, v_ref[:, cs].astype(F32))
            dup_ref[:, F + c0:F + c0 + FFN_CHUNK] = (dact * ac * cdf).astype(BF)
            _, dac_n = dconv(_dot(dxn16, wd, _NT), acn_ref[:, cs].astype(F32), vn_ref[:, cs].astype(F32))
            d1, d2 = _shift_up_pair(dac, jnp.where(last, 0.0, dac_n))
            w = w_ref[:, cs]
            dup_ref[:, cs] = (w[2:3, :] * dac + w[1:2, :] * d1 + w[0:1, :] * d2).astype(BF)
            a = a_ref[:, cs].astype(F32)
            db_ref[:, cs] += jnp.sum(dac, axis=0, keepdims=True)
            dw_ref[0:1, cs] += jnp.sum(d2 * a, axis=0, keepdims=True)
            dw_ref[1:2, cs] += jnp.sum(d1 * a, axis=0, keepdims=True)
            dw_ref[2:3, cs] += jnp.sum(dac * a, axis=0, keepdims=True)

    w_spec = pl.BlockSpec((3, F), lambda i: (0, 0))
    b_spec = pl.BlockSpec((1, F), lambda i: (0, 0))
    res = _call(body, name="ffn_bwd", grid=(nt,),
                in_specs=[pl.BlockSpec((tm, D), lambda i: (i, 0)), nxt(D, 0), pl.BlockSpec((F, D), lambda i: (0, 0)),
                          a_spec, nxt(F, 0), a_spec, v_spec, nxt(F, 1), w_spec],
                out_specs=(pl.BlockSpec((tm, 2 * F), lambda i: (i, 0)), w_spec, b_spec),
                out_shape=(jax.ShapeDtypeStruct((S, 2 * F), BF), jax.ShapeDtypeStruct((3, F), F32),
                           jax.ShapeDtypeStruct((1, F), F32)),
                args=(dx, dx, w_down, ac, ac, up, up, up, conv_w), comm=comm)
    return res[0], res[1], res[2], list(res[3:])


def _adamw(parts, w, m, v, name, row0=0, prev=None):
    R, C = w.shape
    Rp = parts.shape[1]
    tr = Rp
    for cand in (512, 256, 128, 64, 32, 16):
        if Rp % cand == 0 and row0 % cand == 0 and cand * C * 4 <= (1 << 21):
            tr = cand
            break
    b0 = row0 // tr
    c1 = 1.0 / (1.0 - ADAM_B1 ** ADAM_STEP)
    c2 = 1.0 / (1.0 - ADAM_B2 ** ADAM_STEP)

    def body(p_ref, w_ref, m_ref, v_ref, *rest):
        g_ref, d_ref, nm_ref, nv_ref = rest[-4:]
        g = p_ref[0].astype(F32)
        for j in range(1, N_DEV):
            g = g + p_ref[j].astype(F32)
        nm = ADAM_B1 * m_ref[...] + (1.0 - ADAM_B1) * g
        nv = ADAM_B2 * v_ref[...] + (1.0 - ADAM_B2) * (g * g)
        g_ref[...] = g
        nm_ref[...] = nm
        nv_ref[...] = nv
        d_ref[...] = -ADAM_LR * ((nm * c1) / (jnp.sqrt(nv * c2) + ADAM_EPS) + ADAM_WD * w_ref[...])

    blk = pl.BlockSpec((tr, C), lambda i: (b0 + i, 0))
    out = jax.ShapeDtypeStruct((R, C), F32)
    carried = [] if prev is None else list(prev)
    return pl.pallas_call(
        body, name=name,
        out_shape=(out, out, out, out),
        grid=(Rp // tr,),
        in_specs=[pl.BlockSpec((N_DEV, tr, C), lambda i: (0, i, 0)), blk, blk, blk]
        + [pl.BlockSpec(memory_space=pl.ANY)] * len(carried),
        out_specs=(blk, blk, blk, blk),
        input_output_aliases={4 + k: k for k in range(len(carried))},
        compiler_params=_params("parallel"),
    )(parts, w, m, v, *carried)


def _dil(t, dil):
    S, C = t.shape
    if dil == 1:
        return t.reshape(1, S, C)
    return t.reshape(S // dil, dil, C).transpose(1, 0, 2)


def _undil(t):
    d, L, C = t.shape
    if d == 1:
        return t.reshape(L, C)
    return t.transpose(1, 0, 2).reshape(L * d, C)


def _group_qkv(proj, g):
    dil = DSW_GROUPS[g][1]
    if dil == 1:
        p3 = _dil(proj, 1)
        return (p3, p3, p3), (g, W_A // OUT_A + g, 2 * W_A // OUT_A + g)
    c0 = g * OUT_A
    return tuple(_dil(proj[:, o + c0:o + c0 + OUT_A], dil) for o in (0, W_A, 2 * W_A)), (0, 0, 0)


_COL_SHARDED = ("w_in", "w_br", "w_up")
_ROW_SHARDED = ("w_o", "w_down")


class _Plan:
    def __init__(self):
        self.riders = {}
        self.landed = {}

    def ride(self, slot, key, kind, x):
        self.riders.setdefault(slot, []).append((key, kind, x))

    def run(self, slot, fn, *args, **kw):
        items = self.riders.pop(slot, [])
        res = fn(*args, comm=[(kind, x) for _, kind, x in items], **kw)
        for (key, _, _), r in zip(items, res[-1]):
            self.landed[key] = r
        return res[0] if len(res) == 2 else res[:-1]

    def weight(self, n, l):
        g = self.landed[(n, l)]
        if n in _COL_SHARDED:
            return g.transpose(1, 0, 2).reshape(g.shape[1], -1)
        return g.reshape(-1, g.shape[2])

    def scatter(self, slot, n, l, full, part=0):
        K, N = full.shape
        if n in _COL_SHARDED:
            blocks = full.reshape(K, N_DEV, N // N_DEV).transpose(1, 0, 2)
        else:
            blocks = full.reshape(N_DEV, K // N_DEV, N)
        self.ride(slot, ("d" + n, l, part), "scatter", blocks)


def _layer_fwd(x, p, plan, l):
    h = _rms_fwd(x, p["norm1"][l], f"rms1_fwd_{l}")
    proj = plan.run(f"proj_{l}", _matmul, h, plan.weight("w_in", l), mode="nn", out_dtype=BF, name=f"proj_{l}",
                    tm=1024, tn=1024, tk=1024)
    o_g, l_g, qkv_g = [], [], []
    for g in range(N_GROUPS):
        qkv, cols = _group_qkv(proj, g)
        og, lg = plan.run(f"attn_a_fwd_g{g}_{l}", _attn_a_fwd, qkv, cols, g)
        o_g.append(_undil(og))
        l_g.append(_undil(lg))
        qkv_g.append((qkv, cols))
    ob = plan.run(f"sb_fwd_{l}", _sb_fwd, proj)
    x1, merged, oa, lse = plan.run(f"wo_{l}", _merge_wo, o_g, l_g, ob, proj, p["b_gate"][l], plan.weight("w_br", l),
                                   plan.weight("w_o", l), x, f"wo_{l}")
    h2 = _rms_fwd(x1, p["norm2"][l], f"rms2_fwd_{l}")
    up = plan.run(f"up_{l}", _matmul, h2, plan.weight("w_up", l), mode="nn", out_dtype=BF, name=f"up_{l}",
                  tm=1024, tn=1408, tk=1024)
    x2, act, ac = plan.run(f"down_{l}", _ffn_down, up, p["conv_w"][l], p["conv_b"][l], plan.weight("w_down", l), x1,
                           name=f"down_{l}")
    saved = dict(x=x, h=h, proj=proj, qkv_g=qkv_g, oa=oa, ob=ob, lse=lse, merged=merged, x1=x1, h2=h2, up=up, act=act, ac=ac)
    return x2, saved


def _layer_bwd(dx2, sv, p, plan, l):
    gr = {}
    dwd = plan.run(f"dw_down_{l}", _matmul, sv["act"], dx2, mode="tn", out_dtype=BF, name=f"dw_down_{l}",
                   tm=1408, tn=1024, tk=2048)
    plan.scatter(f"d_h2_{l}", "w_down", l, dwd)
    dup, gr["conv_w"], dcb = plan.run(f"ffn_bwd_{l}", _ffn_bwd, dx2, plan.weight("w_down", l), sv["up"], sv["ac"],
                                      p["conv_w"][l])
    gr["conv_b"] = dcb[0]
    dx1, dn2 = plan.run(f"d_h2_{l}", _matmul_rms_bwd, dup, plan.weight("w_up", l), sv["x1"], p["norm2"][l], dx2,
                        name=f"d_h2_{l}")
    dwu = plan.run(f"dw_up_{l}", _matmul, sv["h2"], dup, mode="tn", out_dtype=BF, name=f"dw_up_{l}",
                   tm=1024, tn=1408, tk=2048)
    plan.scatter(f"sb_bwd_{l}", "w_up", l, dwu)
    gr["norm2"] = dn2[0]
    dwo = plan.run(f"dw_o_{l}", _matmul, sv["merged"], dx1, mode="tn", out_dtype=BF, name=f"dw_o_{l}",
                   tm=1024, tn=1024, tk=2048)
    plan.scatter(f"dw_in_{l}", "w_o", l, dwo)
    dya, dyb, doa, dob, stats, dgate, dbg = _merge_bwd(dx1, plan.weight("w_o", l), sv["oa"], sv["ob"], sv["lse"],
                                                       sv["proj"], p["b_gate"][l], plan.weight("w_br", l))
    gr["b_gate"] = dbg[0]
    dwa = plan.run(f"dw_bra_{l}", _matmul, sv["oa"], dya, mode="tn", out_dtype=BF, name=f"dw_bra_{l}",
                   tm=256, tn=1024, tk=2048)
    dwb = plan.run(f"dw_brb_{l}", _matmul, sv["ob"], dyb, mode="tn", out_dtype=BF, name=f"dw_brb_{l}",
                   tm=256, tn=1024, tk=2048)
    plan.scatter(f"dw_in_{l}", "w_br", l, jnp.concatenate([dwa, dwb], axis=0))
    proj = sv["proj"]
    dq_a, dk_a, dv_a = [], [], []
    for g, (_, dil) in enumerate(DSW_GROUPS):
        qkv, cols = sv["qkv_g"][g]
        dqg, dkg, dvg = _attn_a_bwd(qkv, cols, _dil(doa, dil), _dil(stats, dil), g)
        dq_a.append(_undil(dqg))
        dk_a.append(_undil(dkg))
        dv_a.append(_undil(dvg))
    dqb, dkb, dvb = plan.run(f"sb_bwd_{l}", _sb_bwd, proj, dob, sv["ob"])
    dproj = jnp.concatenate(dq_a + dk_a + dv_a + [dqb.astype(BF), dkb.astype(BF), dvb.astype(BF), dgate], axis=1)
    dx, dn1 = plan.run(f"d_h_{l}", _matmul_rms_bwd, dproj, plan.weight("w_in", l), sv["x"], p["norm1"][l], dx1,
                       name=f"d_h_{l}")
    if l > 0:
        dwi = plan.run(f"dw_in_{l}", _matmul, sv["h"], dproj, mode="tn", out_dtype=BF, name=f"dw_in_{l}",
                       tm=1024, tn=1280, tk=2048)
        plan.scatter(f"ffn_bwd_{l - 1}", "w_in", l, dwi)
    else:
        half = D_MODEL // 2
        for part, slab in enumerate((sv["h"][:, :half], sv["h"][:, half:])):
            name = f"dw_in_{l}" if part == 0 else f"dw_in_{l}_rest"
            dwi = plan.run(name, _matmul, slab, dproj, mode="tn", out_dtype=BF, name=name, tm=half, tn=1280, tk=2048)
            plan.scatter(f"dw_in_{l}_rest" if part == 0 else "alone", "w_in", l, dwi, part)
    gr["norm1"] = dn1[0]
    return dx, gr


def kernel(x, norm1, w_in, b_gate, w_br, w_o, norm2, w_up, conv_w, conv_b, w_down, norm_f, loss_target, m_norm1, m_w_in, m_b_gate, m_w_br, m_w_o, m_norm2, m_w_up, m_conv_w, m_conv_b, m_w_down, m_norm_f, v_norm1, v_w_in, v_b_gate, v_w_br, v_w_o, v_norm2, v_w_up, v_conv_w, v_conv_b, v_w_down, v_norm_f):
    depth = norm1.shape[0]
    me = 4 * lax.axis_index("x") + 2 * lax.axis_index("y") + lax.axis_index("c")
    shards = dict(w_in=w_in, w_br=w_br, w_o=w_o, w_up=w_up, w_down=w_down)
    moments_m = dict(norm1=m_norm1, w_in=m_w_in, b_gate=m_b_gate, w_br=m_w_br, w_o=m_w_o, norm2=m_norm2,
                     w_up=m_w_up, conv_w=m_conv_w, conv_b=m_conv_b, w_down=m_w_down, norm_f=m_norm_f)
    moments_v = dict(norm1=v_norm1, w_in=v_w_in, b_gate=v_b_gate, w_br=v_w_br, w_o=v_w_o, norm2=v_norm2,
                     w_up=v_w_up, conv_w=v_conv_w, conv_b=v_conv_b, w_down=v_w_down, norm_f=v_norm_f)

    plan = _Plan()
    wb = {n: s.astype(BF) for n, s in shards.items()}
    p = dict(norm1=norm1, b_gate=b_gate, norm2=norm2, conv_b=conv_b)
    cw = _all_gather(conv_w, "gather_conv_w")
    p["conv_w"] = cw.transpose(1, 2, 0, 3).reshape(depth, 3, D_FF)
    plan.landed[("w_in", 0)] = _all_gather_via_sibling(wb["w_in"][0], "gather_w_in_0")
    for l in range(depth):
        plan.ride(f"proj_{l}", ("w_down", l), "gather", wb["w_down"][l])
        plan.ride(f"attn_a_fwd_g0_{l}" if l == 0 else f"down_{l - 1}", ("w_br", l), "gather", wb["w_br"][l])
        plan.ride(f"attn_a_fwd_g0_{l}" if l == 0 else f"down_{l - 1}", ("w_o", l), "gather", wb["w_o"][l])
        plan.ride(f"sb_fwd_{l}", ("w_up", l), "gather", wb["w_up"][l])
        if l + 1 < depth:
            plan.ride(f"up_{l}", ("w_in", l + 1), "gather", wb["w_in"][l + 1])

    xs = x[0]
    saved = []
    for l in range(depth):
        xs, sv = _layer_fwd(xs, p, plan, l)
        saved.append(sv)
    loss_part, dx, dnf = _loss_head(xs, norm_f, loss_target[0])
    loss = lax.psum(loss_part[0, 0], ("x", "y", "c"))

    grads = [None] * depth
    for l in reversed(range(depth)):
        dx, grads[l] = _layer_bwd(dx, saved[l], p, plan, l)
    grad_x = dx[None]
    (key, _, last), = plan.riders.pop("alone")
    plan.landed[key] = _all_to_all(last, "scatter_w_in_rest")
    assert not plan.riders, sorted(plan.riders)

    out_g, out_d, out_m, out_v = {}, {}, {}, {}
    for n in _COL_SHARDED + _ROW_SHARDED:
        shp = shards[n].shape
        flat = (shp[0] * shp[1], shp[2])
        res = None
        for l in range(depth):
            row = l * shp[1]
            for key in sorted(k for k in plan.landed if k[:2] == ("d" + n, l)):
                parts = plan.landed[key]
                res = _adamw(parts, shards[n].reshape(flat), moments_m[n].reshape(flat), moments_v[n].reshape(flat),
                             f"adamw_{n}_{l}_{key[2]}", row0=row, prev=res)
                row += parts.shape[1]
        out_g[n], out_d[n], out_m[n], out_v[n] = [r.reshape(shp) for r in res]

    small = ("norm1", "b_gate", "norm2", "conv_b")
    vecs = [jnp.stack([grads[l][n] for l in range(depth)]).reshape(-1) for n in small]
    vecs.append(dnf.reshape(-1))
    vecs.append(jnp.stack([grads[l]["conv_w"] for l in range(depth)]).reshape(-1))
    sizes = [v.shape[0] for v in vecs]
    flat = jnp.concatenate(vecs)
    n_small = sum(sizes[:-1])
    pad = (-flat.shape[0]) % 1024
    flat = jnp.pad(flat, (0, pad)).reshape(-1, 128)
    allp = _all_gather(flat, "gather_small_grads").reshape(N_DEV, -1)
    rep_w = jnp.concatenate([norm1.reshape(-1), b_gate.reshape(-1), norm2.reshape(-1), conv_b.reshape(-1), norm_f])
    rep_m = jnp.concatenate([moments_m[n].reshape(-1) for n in small] + [m_norm_f])
    rep_v = jnp.concatenate([moments_v[n].reshape(-1) for n in small] + [v_norm_f])
    rows = n_small // 128
    res = _adamw(allp[:, :n_small].reshape(N_DEV, rows, 128), rep_w.reshape(rows, 128), rep_m.reshape(rows, 128),
                 rep_v.reshape(rows, 128), "adamw_small")
    off = 0
    for n, sz in zip(small + ("norm_f",), sizes[:-1]):
        shp = norm_f.shape if n == "norm_f" else p[n].shape
        out_g[n], out_d[n], out_m[n], out_v[n] = [r.reshape(-1)[off:off + sz].reshape(shp) for r in res]
        off += sz
    f = conv_w.shape[2]
    cwp = allp[:, n_small:n_small + sizes[-1]].reshape(N_DEV, depth * 3, D_FF)
    cwp = lax.dynamic_slice_in_dim(cwp, me * f, f, axis=2)
    res = _adamw(cwp, conv_w.reshape(depth * 3, f), m_conv_w.reshape(depth * 3, f), v_conv_w.reshape(depth * 3, f),
                 "adamw_conv_w")
    out_g["conv_w"], out_d["conv_w"], out_m["conv_w"], out_v["conv_w"] = [r.reshape(conv_w.shape) for r in res]

    order = ("norm1", "w_in", "b_gate", "w_br", "w_o", "norm2", "w_up", "conv_w", "conv_b", "w_down", "norm_f")
    return (loss, grad_x, *[out_g[n] for n in order], *[out_d[n] for n in order],
            *[out_m[n] for n in order], *[out_v[n] for n in order])
```

```python
import functools

import jax
import jax.numpy as jnp
from jax import lax
from jax.experimental import pallas as pl
from jax.experimental.pallas import tpu as pltpu

BF = jnp.bfloat16
F32 = jnp.float32

N_DEV = 8
D_MODEL = 1024
HEAD_DIM = 64
DSW_GROUPS = ((128, 1), (512, 4), (2048, 16))
HEADS_PER_GROUP = 4
N_GROUPS = len(DSW_GROUPS)
DSW_HEADS = HEADS_PER_GROUP * N_GROUPS
SB_HEADS = 4
W_A = DSW_HEADS * HEAD_DIM
W_B = SB_HEADS * HEAD_DIM
OUT_A = HEADS_PER_GROUP * HEAD_DIM
N_IN = 3 * W_A + 3 * W_B + 2 * D_MODEL
GATE_OFF = 3 * W_A + 3 * W_B
D_FF = 2816
SB_BLOCK = 256
RMS_EPS = 1e-6
ATT_SCALE = HEAD_DIM ** -0.5
NEG = -1e30
SB_EXIT = -110.0
FFN_CHUNK = 256

ADAM_LR = 0.001
ADAM_B1 = 0.9
ADAM_B2 = 0.999
ADAM_EPS = 1e-08
ADAM_WD = 0.01
ADAM_STEP = 10

HBM_SPEC = pl.BlockSpec(memory_space=pltpu.HBM)
MESH = pl.DeviceIdType.MESH

_NN = (((1,), (0,)), ((), ()))
_NT = (((1,), (1,)), ((), ()))
_TN = (((0,), (0,)), ((), ()))


def _dot(a, b, dn=_NN):
    return lax.dot_general(a, b, dn, preferred_element_type=F32)


def _pick(dim, pref):
    if dim <= pref:
        return dim
    t = (pref // 128) * 128
    while t >= 128:
        if dim % t == 0:
            return t
        t -= 128
    return dim


def _params(*sem):
    return pltpu.CompilerParams(dimension_semantics=sem)


def _peer(k):
    x, y, c = lax.axis_index("x"), lax.axis_index("y"), lax.axis_index("c")
    px = 1 - x if (k >> 2) & 1 else x
    py = 1 - y if (k >> 1) & 1 else y
    pc = 1 - c if k & 1 else c
    return (px, py, pc), 4 * px + 2 * py + pc


def _exchange(kind, x_ref, out_ref, send_sems, recv_sems, local_sem):
    gather = kind == "gather"
    _, me = _peer(0)

    def src(idx):
        return x_ref if gather else x_ref.at[idx]

    def copy(k, dst_idx):
        peer, pidx = _peer(k)
        return pltpu.make_async_remote_copy(
            src_ref=src(pidx), dst_ref=out_ref.at[dst_idx], send_sem=send_sems.at[k - 1],
            recv_sem=recv_sems.at[k - 1], device_id=peer, device_id_type=MESH)

    mine = pltpu.make_async_copy(src(me), out_ref.at[me], local_sem)

    def start():
        mine.start()
        for k in range(1, N_DEV):
            copy(k, me).start()

    def wait():
        for k in range(1, N_DEV):
            copy(k, _peer(k)[1]).wait_recv()
        for k in range(1, N_DEV):
            copy(k, me).wait_send()
        mine.wait()

    return start, wait


_EXCHANGE_SEMS = [pltpu.SemaphoreType.DMA((N_DEV - 1,)), pltpu.SemaphoreType.DMA((N_DEV - 1,)),
                  pltpu.SemaphoreType.DMA]


def _exchange_shape(kind, x):
    return jax.ShapeDtypeStruct(((N_DEV,) + x.shape) if kind == "gather" else x.shape, x.dtype)


def _exchange_alone(kind, x, name):
    def body(x_ref, out_ref, send_sems, recv_sems, local_sem):
        start, wait = _exchange(kind, x_ref, out_ref, send_sems, recv_sems, local_sem)
        start()
        wait()

    return pl.pallas_call(
        body, name=name, out_shape=_exchange_shape(kind, x),
        in_specs=[HBM_SPEC], out_specs=HBM_SPEC, scratch_shapes=list(_EXCHANGE_SEMS),
    )(x)


def _all_gather_via_sibling(x, name):
    def body(x_ref, out_ref, send_sems, recv_sems, local_sem):
        x_, y_, c_ = lax.axis_index("x"), lax.axis_index("y"), lax.axis_index("c")
        me, sibling = (x_, y_, c_), (x_, y_, 1 - c_)
        chips = [(1 - x_, y_), (x_, 1 - y_), (1 - x_, 1 - y_)]

        def slot(px, py, pc):
            return out_ref.at[4 * px + 2 * py + pc]

        def copy(k, block, to, src=None):
            return pltpu.make_async_remote_copy(
                src_ref=slot(*block) if src is None else src, dst_ref=slot(*block), send_sem=send_sems.at[k],
                recv_sem=recv_sems.at[k], device_id=to, device_id_type=MESH)

        mine = pltpu.make_async_copy(x_ref, slot(*me), local_sem)
        mine.start()
        first = [copy(0, me, sibling, src=x_ref)]
        first += [copy(1 + j, me, (*chip, c_), src=x_ref) for j, chip in enumerate(chips)]
        for cp in first:
            cp.start()
        passed = [copy(4 + j, (*chip, c_), sibling) for j, chip in enumerate(chips)]
        for j, chip in enumerate(chips):
            copy(1 + j, (*chip, c_), me).wait_recv()
            passed[j].start()
        copy(0, sibling, me).wait_recv()
        for j, chip in enumerate(chips):
            copy(4 + j, (*chip, 1 - c_), me).wait_recv()
        for cp in first + passed:
            cp.wait_send()
        mine.wait()

    return pl.pallas_call(
        body, name=name, out_shape=_exchange_shape("gather", x),
        in_specs=[HBM_SPEC], out_specs=HBM_SPEC,
        scratch_shapes=[pltpu.SemaphoreType.DMA((N_DEV - 1,)), pltpu.SemaphoreType.DMA((N_DEV - 1,)),
                        pltpu.SemaphoreType.DMA],
    )(x)


def _all_gather(x, name):
    return _exchange_alone("gather", x, name)


def _all_to_all(x, name):
    return _exchange_alone("scatter", x, name)


def _call(body, *, name, grid, in_specs, out_specs, out_shape, args, scratch_shapes=(), sem=None, comm=()):
    single = not isinstance(out_shape, (tuple, list))
    outs = (out_shape,) if single else tuple(out_shape)
    ospecs = (out_specs,) if single else tuple(out_specs)
    if not comm:
        res = pl.pallas_call(
            body, name=name, out_shape=outs, grid=grid, in_specs=list(in_specs), out_specs=ospecs,
            scratch_shapes=list(scratch_shapes), compiler_params=_params(*(sem or ("arbitrary",) * len(grid))),
        )(*args)
        return res
    n_in, n_out, n_scr, nc = len(in_specs), len(outs), len(scratch_shapes), len(comm)

    def wrapped(*refs):
        ins = refs[:n_in]
        cins = refs[n_in:n_in + nc]
        o0 = n_in + nc
        kouts = refs[o0:o0 + n_out]
        couts = refs[o0 + n_out:o0 + n_out + nc]
        s0 = o0 + n_out + nc
        scr = refs[s0:s0 + n_scr]
        sems = refs[s0 + n_scr:]
        ids = [pl.program_id(ax) for ax in range(len(grid))]
        first = functools.reduce(jnp.logical_and, [i == 0 for i in ids])
        last = functools.reduce(jnp.logical_and, [i == g - 1 for i, g in zip(ids, grid)])
        ex = [_exchange(comm[c][0], cins[c], couts[c], *sems[3 * c:3 * c + 3]) for c in range(nc)]

        @pl.when(first)
        def _():
            for start, _ in ex:
                start()

        body(*ins, *kouts, *scr)

        @pl.when(last)
        def _():
            for _, wait in ex:
                wait()

    return pl.pallas_call(
        wrapped, name=name,
        out_shape=outs + tuple(_exchange_shape(k, x) for k, x in comm),
        grid=grid, in_specs=list(in_specs) + [HBM_SPEC] * nc, out_specs=ospecs + (HBM_SPEC,) * nc,
        scratch_shapes=list(scratch_shapes) + list(_EXCHANGE_SEMS) * nc,
        compiler_params=_params(*(("arbitrary",) * len(grid))),
    )(*args, *[x for _, x in comm])


def _matmul(a, b, *, mode, out_dtype, name, tm=512, tn=1024, tk=1024, res=None, comm=(), m_cols=None, m_off=0):
    if mode == "nn":
        (M, K), (_, N) = a.shape, b.shape
    elif mode == "nt":
        (M, K), (N, _) = a.shape, b.shape
    else:
        (K, M), (_, N) = a.shape, b.shape
        M = M if m_cols is None else m_cols
    tm, tn, tk = _pick(M, tm), _pick(N, tn), _pick(K, tk)
    i0 = m_off // tm
    nk = K // tk
    dn = {"nn": _NN, "nt": _NT, "tn": _TN}[mode]

    def body(*refs):
        a_ref, b_ref = refs[0], refs[1]
        r_ref = refs[2] if res is not None else None
        o_ref = refs[3] if res is not None else refs[2]

        def finish(r):
            if res is not None:
                r = r + r_ref[...].astype(F32)
            o_ref[...] = r.astype(out_dtype)

        part = _dot(a_ref[...].astype(BF), b_ref[...].astype(BF), dn)
        if nk == 1:
            finish(part)
            return
        acc = refs[-1]
        k = pl.program_id(2)

        @pl.when(k == 0)
        def _():
            acc[...] = jnp.zeros_like(acc)

        acc[...] += part

        @pl.when(k == nk - 1)
        def _():
            finish(acc[...])

    if mode == "tn":
        a_spec = pl.BlockSpec((tk, tm), lambda j, i, k: (k, i0 + i))
    else:
        a_spec = pl.BlockSpec((tm, tk), lambda j, i, k: (i, k))
    if mode == "nt":
        b_spec = pl.BlockSpec((tn, tk), lambda j, i, k: (j, k))
    else:
        b_spec = pl.BlockSpec((tk, tn), lambda j, i, k: (k, j))
    o_spec = pl.BlockSpec((tm, tn), lambda j, i, k: (i, j))
    in_specs = [a_spec, b_spec] + ([o_spec] if res is not None else [])
    args = (a, b) + ((res,) if res is not None else ())
    out = _call(body, name=name, grid=(N // tn, M // tm, nk), in_specs=in_specs, out_specs=o_spec,
                out_shape=jax.ShapeDtypeStruct((M, N), out_dtype), args=args,
                scratch_shapes=[pltpu.VMEM((tm, tn), F32)] if nk > 1 else [],
                sem=("parallel", "parallel", "arbitrary"), comm=comm)
    return out[0], list(out[1:])


def _rms_fwd(x, g, name):
    S, D = x.shape
    tm = _pick(S, 512)

    def body(x_ref, g_ref, h_ref):
        xf = x_ref[...]
        r = lax.rsqrt(jnp.mean(xf * xf, axis=-1, keepdims=True) + RMS_EPS)
        h_ref[...] = (xf * r * g_ref[...]).astype(BF)

    return pl.pallas_call(
        body, name=name,
        out_shape=jax.ShapeDtypeStruct((S, D), BF),
        grid=(S // tm,),
        in_specs=[pl.BlockSpec((tm, D), lambda i: (i, 0)), pl.BlockSpec((1, D), lambda i: (0, 0))],
        out_specs=pl.BlockSpec((tm, D), lambda i: (i, 0)),
        compiler_params=_params("parallel"),
    )(x, g.reshape(1, D))


def _matmul_rms_bwd(dy, w, x, g, dres, *, name, tm=512, comm=()):
    S, K = dy.shape
    D = w.shape[0]
    tm = _pick(S, tm)

    def body(dy_ref, w_ref, x_ref, g_ref, dres_ref, dx_ref, dg_ref):
        @pl.when(pl.program_id(0) == 0)
        def _():
            dg_ref[...] = jnp.zeros_like(dg_ref)

        dh = _dot(dy_ref[...].astype(BF), w_ref[...], _NT)
        xf = x_ref[...]
        r = lax.rsqrt(jnp.mean(xf * xf, axis=-1, keepdims=True) + RMS_EPS)
        xh = xf * r
        dg_ref[...] += jnp.sum(dh * xh, axis=0, keepdims=True)
        dxh = dh * g_ref[...]
        dx_ref[...] = dres_ref[...] + r * (dxh - xh * jnp.mean(dxh * xh, axis=-1, keepdims=True))

    row = pl.BlockSpec((tm, D), lambda i: (i, 0))
    vec = pl.BlockSpec((1, D), lambda i: (0, 0))
    res = _call(body, name=name, grid=(S // tm,),
                in_specs=[pl.BlockSpec((tm, K), lambda i: (i, 0)), pl.BlockSpec((D, K), lambda i: (0, 0)), row, vec, row],
                out_specs=(row, vec),
                out_shape=(jax.ShapeDtypeStruct((S, D), F32), jax.ShapeDtypeStruct((1, D), F32)),
                args=(dy, w, x, g.reshape(1, D), dres), comm=comm)
    return res[0], res[1], list(res[2:])


def _loss_head(x, g, target):
    S, D = x.shape
    tm = _pick(S, 512)

    def body(x_ref, g_ref, t_ref, loss_ref, dx_ref, dg_ref):
        @pl.when(pl.program_id(0) == 0)
        def _():
            dg_ref[...] = jnp.zeros_like(dg_ref)
            loss_ref[...] = jnp.zeros_like(loss_ref)

        xf = x_ref[...]
        gg = g_ref[...]
        r = lax.rsqrt(jnp.mean(xf * xf, axis=-1, keepdims=True) + RMS_EPS)
        xh = xf * r
        err = xh * gg - t_ref[...]
        per_tok = jnp.mean(err * err, axis=-1, keepdims=True)
        loss_ref[...] += 0.5 * jnp.sum(per_tok, axis=0, keepdims=True)
        dy = err * (1.0 / D)
        dg_ref[...] += jnp.sum(dy * xh, axis=0, keepdims=True)
        dxh = dy * gg
        dx_ref[...] = r * (dxh - xh * jnp.mean(dxh * xh, axis=-1, keepdims=True))

    row = pl.BlockSpec((tm, D), lambda i: (i, 0))
    vec = pl.BlockSpec((1, D), lambda i: (0, 0))
    one = pl.BlockSpec((1, 1), lambda i: (0, 0))
    return pl.pallas_call(
        body, name="loss_head",
        out_shape=(jax.ShapeDtypeStruct((1, 1), F32), jax.ShapeDtypeStruct((S, D), F32),
                   jax.ShapeDtypeStruct((1, D), F32)),
        grid=(S // tm,),
        in_specs=[row, vec, row], out_specs=(one, row, vec),
        compiler_params=_params("arbitrary"),
    )(x, g.reshape(1, D), target)


def _slopes(g):
    return [2.0 ** (-8.0 * (HEADS_PER_GROUP * g + j + 1) / DSW_HEADS) for j in range(HEADS_PER_GROUP)]


def _band_masks(W):
    row = lax.broadcasted_iota(jnp.int32, (W, W), 0)
    col = lax.broadcasted_iota(jnp.int32, (W, W), 1)
    d_cur = row - col
    d_prev = d_cur + W
    return d_cur, d_prev, d_cur >= 0, d_cur <= 0


def _band_specs(W, nb, per):
    def cur(c):
        return pl.BlockSpec((None, per * W, OUT_A), lambda r, n: (r, n, c))

    def prev(c):
        return pl.BlockSpec((None, W, OUT_A), lambda r, n: (r, jnp.maximum(per * n - 1, 0), c))

    def nxt(c):
        return pl.BlockSpec((None, W, OUT_A), lambda r, n: (r, jnp.minimum(per * (n + 1), nb - 1), c))

    return cur, prev, nxt


def _blocks_per_step(nb):
    return 4 if nb % 4 == 0 else 2 if nb % 2 == 0 else 1


def _head_stack(W):
    H, hd = HEADS_PER_GROUP, HEAD_DIM
    lane_head = lax.broadcasted_iota(jnp.int32, (W, OUT_A), 1) // hd

    def stack(x):
        return jnp.concatenate([jnp.where(lane_head == h, x, jnp.zeros_like(x)) for h in range(H)], axis=0)

    def unstack(y):
        out = jnp.where(lane_head == 0, y[0:W], 0.0)
        for h in range(1, H):
            out = jnp.where(lane_head == h, y[h * W:(h + 1) * W], out)
        return out

    def column(ref, rows, off=0):
        return jnp.concatenate([ref[rows, h * hd + off:h * hd + off + 1] for h in range(H)], axis=0)

    return stack, unstack, column


def _stacked_bias(W, slopes, dil):
    d_cur, d_prev, m_cur, m_prev = _band_masks(W)
    b_cur = jnp.concatenate([(s * dil) * d_cur.astype(F32) for s in slopes], axis=0)
    b_prev = jnp.concatenate([(s * dil) * d_prev.astype(F32) for s in slopes], axis=0)
    H = len(slopes)
    return b_cur, b_prev, jnp.concatenate([m_cur] * H, axis=0), jnp.concatenate([m_prev] * H, axis=0)


def _attn_a_fwd(qkv, cols, g, comm=()):
    win, dil = DSW_GROUPS[g]
    W = win // dil
    d, L, _ = qkv[0].shape
    nb = L // W
    per = _blocks_per_step(nb)
    slopes = _slopes(g)

    def body(q_ref, kp_ref, kc_ref, vp_ref, vc_ref, o_ref, l_ref):
        n = pl.program_id(1)
        stack, unstack, _ = _head_stack(W)
        b_cur, b_prev, m_cur, m_prev = _stacked_bias(W, slopes, dil)
        m_first = jnp.logical_and(m_prev, n > 0)
        for b in range(per):
            rows = slice(b * W, (b + 1) * W)
            before = slice((b - 1) * W, b * W)
            qs = stack(q_ref[rows, :])
            kc, vc = kc_ref[rows, :], vc_ref[rows, :]
            kp, vp = (kp_ref[...], vp_ref[...]) if b == 0 else (kc_ref[before, :], vc_ref[before, :])
            s_c = jnp.where(m_cur, _dot(qs, kc, _NT) * ATT_SCALE - b_cur, NEG)
            s_p = jnp.where(m_first if b == 0 else m_prev, _dot(qs, kp, _NT) * ATT_SCALE - b_prev, NEG)
            m = jnp.maximum(jnp.max(s_c, axis=1, keepdims=True), jnp.max(s_p, axis=1, keepdims=True))
            p_c = jnp.exp(s_c - m)
            p_p = jnp.exp(s_p - m)
            den = jnp.sum(p_c, axis=1, keepdims=True) + jnp.sum(p_p, axis=1, keepdims=True)
            pv = _dot(p_c.astype(BF), vc) + _dot(p_p.astype(BF), vp)
            o_ref[rows, :] = unstack(pv / den)
            l_ref[rows, :] = unstack(jnp.broadcast_to(m + jnp.log(den), pv.shape))

    cur, prev, _ = _band_specs(W, nb, per)
    out = jax.ShapeDtypeStruct((d, L, OUT_A), F32)
    res = _call(body, name=f"attn_a_fwd_g{g}", grid=(d, nb // per),
                in_specs=[cur(cols[0]), prev(cols[1]), cur(cols[1]), prev(cols[2]), cur(cols[2])],
                out_specs=(cur(0), cur(0)), out_shape=(out, out),
                args=(qkv[0], qkv[1], qkv[1], qkv[2], qkv[2]), sem=("parallel", "parallel"), comm=comm)
    return res[0], res[1], list(res[2:])


def _attn_a_bwd(qkv, cols, do, stats, g):
    win, dil = DSW_GROUPS[g]
    W = win // dil
    d, L, _ = qkv[0].shape
    nb = L // W
    per = _blocks_per_step(nb)
    nsteps = nb // per
    slopes = _slopes(g)

    def body(q_ref, qn_ref, kp_ref, kc_ref, vp_ref, vc_ref, do_ref, don_ref, st_ref, stn_ref,
             dq_ref, dk_ref, dv_ref):
        n = pl.program_id(1)
        stack, unstack, column = _head_stack(W)
        b_cur, b_prev, m_cur, m_prev = _stacked_bias(W, slopes, dil)
        m_first = jnp.logical_and(m_prev, n > 0)
        m_last = jnp.logical_and(m_prev, n < nsteps - 1)
        everything = slice(None)
        for b in range(per):
            rows = slice(b * W, (b + 1) * W)
            before = slice((b - 1) * W, b * W)
            after = slice((b + 1) * W, (b + 2) * W)
            first, last = b == 0, b == per - 1
            qs = stack(q_ref[rows, :])
            qn = stack(qn_ref[...] if last else q_ref[after, :])
            dos = stack(do_ref[rows, :])
            don = stack(don_ref[...] if last else do_ref[after, :])
            kc, vc = kc_ref[rows, :], vc_ref[rows, :]
            kp, vp = (kp_ref[...], vp_ref[...]) if first else (kc_ref[before, :], vc_ref[before, :])
            lse_c, dsum_c = column(st_ref, rows), column(st_ref, rows, STAT_OFF)
            lse_n = column(stn_ref, everything) if last else column(st_ref, after)
            dsum_n = column(stn_ref, everything, STAT_OFF) if last else column(st_ref, after, STAT_OFF)
            m_p = m_first if first else m_prev
            m_n = m_last if last else m_prev
            p_cc = jnp.exp(jnp.where(m_cur, _dot(qs, kc, _NT) * ATT_SCALE - b_cur, NEG) - lse_c)
            p_cp = jnp.exp(jnp.where(m_p, _dot(qs, kp, _NT) * ATT_SCALE - b_prev, NEG) - lse_c)
            p_nc = jnp.exp(jnp.where(m_n, _dot(qn, kc, _NT) * ATT_SCALE - b_prev, NEG) - lse_n)
            ds_cc = (p_cc * (_dot(dos, vc, _NT) - dsum_c) * ATT_SCALE).astype(BF)
            ds_cp = (p_cp * (_dot(dos, vp, _NT) - dsum_c) * ATT_SCALE).astype(BF)
            ds_nc = (p_nc * (_dot(don, vc, _NT) - dsum_n) * ATT_SCALE).astype(BF)
            dq_ref[rows, :] = unstack(_dot(ds_cc, kc) + _dot(ds_cp, kp)).astype(BF)
            dk_ref[rows, :] = (_dot(ds_cc, qs, _TN) + _dot(ds_nc, qn, _TN)).astype(BF)
            dv_ref[rows, :] = (_dot(p_cc.astype(BF), dos, _TN) + _dot(p_nc.astype(BF), don, _TN)).astype(BF)

    cur, prev, nxt = _band_specs(W, nb, per)
    out = jax.ShapeDtypeStruct((d, L, OUT_A), BF)
    cq, ck, cv = cols
    return pl.pallas_call(
        body, name=f"attn_a_bwd_g{g}",
        out_shape=(out, out, out),
        grid=(d, nsteps),
        in_specs=[cur(cq), nxt(cq), prev(ck), cur(ck), prev(cv), cur(cv), cur(0), nxt(0), cur(0), nxt(0)],
        out_specs=(cur(0), cur(0), cur(0)),
        compiler_params=_params("parallel", "parallel"),
    )(qkv[0], qkv[0], qkv[1], qkv[1], qkv[2], qkv[2], do, do, stats, stats)


SB_PAIR = 2
SB_QUAD = 4
SB_QROWS = SB_BLOCK


def _softplus_parts(z):
    e = jnp.exp(-jnp.abs(z))
    log1p_e = jnp.where(e < 1e-4, e, jnp.log(1.0 + e))
    return e, jnp.maximum(z, 0.0) + log1p_e


def _split_dot(x, t):
    hi = x.astype(BF)
    lo = (x - hi.astype(F32)).astype(BF)
    return _dot(hi, t) + _dot(lo, t)


def _sb_block(qh, kk, causal, r_run, tri_incl):
    z = _dot(qh, kk, _NT)
    e, sp = _softplus_parts(z)
    ls = jnp.where(causal, -sp, 0.0)
    cin = _split_dot(ls, tri_incl)
    a = jnp.where(causal, jnp.exp(z + cin + r_run), 0.0)
    return z, e, cin, a


def _sb_specs(S, pair=SB_PAIR):
    Q, hd = SB_QROWS, HEAD_DIM
    lanes = pair * hd
    qc = (3 * W_A) // lanes
    kc = (3 * W_A + W_B) // lanes
    vc = (3 * W_A + 2 * W_B) // lanes
    q_spec = pl.BlockSpec((Q, lanes), lambda p, i: (i, qc + p))
    k_spec = pl.BlockSpec((S, lanes), lambda p, i: (0, kc + p))
    v_spec = pl.BlockSpec((S, lanes), lambda p, i: (0, vc + p))
    o_spec = pl.BlockSpec((Q, lanes), lambda p, i: (i, p))
    full = pl.BlockSpec((S, lanes), lambda p, i: (0, p))
    return q_spec, k_spec, v_spec, o_spec, full


def _sb_stack(pair=SB_PAIR):
    Q, hd = SB_QROWS, HEAD_DIM
    lane_head = lax.broadcasted_iota(jnp.int32, (Q, pair * hd), 1) // hd

    def stack(x):
        return jnp.concatenate([jnp.where(lane_head == h, x, jnp.zeros_like(x)) for h in range(pair)], axis=0)

    def unstack(y):
        out = jnp.where(lane_head == 0, y[0:Q], 0.0)
        for h in range(1, pair):
            out = jnp.where(lane_head == h, y[h * Q:(h + 1) * Q], out)
        return out

    return stack, unstack


def _sb_iotas(i, pair=SB_PAIR):
    B, Q = SB_BLOCK, SB_QROWS
    row = lax.broadcasted_iota(jnp.int32, (Q, B), 0) + i * Q
    col = lax.broadcasted_iota(jnp.int32, (Q, B), 1)
    ahead = jnp.concatenate([col - row] * pair, axis=0)
    tr = lax.broadcasted_iota(jnp.int32, (B, B), 0)
    tc = lax.broadcasted_iota(jnp.int32, (B, B), 1)
    return ahead, tr, tc


def _sb_fwd(proj, comm=()):
    S = proj.shape[0]
    B, Q, hd = SB_BLOCK, SB_QROWS, HEAD_DIM
    nq = S // Q
    P = SB_QUAD
    R = P * Q
    q_spec, k_spec, v_spec, o_spec, _ = _sb_specs(S, P)

    def body(q_ref, k_ref, v_ref, o_ref):
        i = pl.program_id(1)
        ahead, tr, tc = _sb_iotas(i, P)
        tri_incl = (tr >= tc).astype(BF)
        stack, unstack = _sb_stack(P)
        qs = stack(q_ref[...] * ATT_SCALE)

        def cond(c):
            return jnp.logical_and(c[0] >= 0, c[-1] > SB_EXIT)

        def step(c):
            kb, r_run, acc, _ = c
            off = pl.multiple_of(kb * B, B)
            causal = ahead < -kb * B
            _, _, cin, a = _sb_block(qs, k_ref[pl.ds(off, B), :], causal, r_run, tri_incl)
            acc = acc + _dot(a.astype(BF), v_ref[pl.ds(off, B), :])
            r_run = r_run + cin[:, 0:1]
            return kb - 1, r_run, acc, jnp.max(r_run)

        init = (i, jnp.zeros((R, 1), F32), jnp.zeros((R, P * hd), F32), jnp.float32(0.0))
        fin = lax.while_loop(cond, step, init)
        o_ref[...] = unstack(fin[2])

    res = _call(body, name="sb_fwd", grid=(SB_HEADS // P, nq), in_specs=[q_spec, k_spec, v_spec],
                out_specs=o_spec, out_shape=jax.ShapeDtypeStruct((S, W_B), F32), args=(proj, proj, proj),
                sem=("parallel", "parallel"), comm=comm)
    return res[0], list(res[1:])


def _sb_bwd(proj, do, o, comm=()):
    S = proj.shape[0]
    B, Q, hd = SB_BLOCK, SB_QROWS, HEAD_DIM
    nq = S // Q
    R = SB_PAIR * Q
    q_spec, k_spec, v_spec, o_spec, full = _sb_specs(S)

    def body(q_ref, k_ref, v_ref, do_ref, o_ref, dq_ref, dk_ref, dv_ref):
        i = pl.program_id(1)

        @pl.when(i == 0)
        def _():
            dk_ref[...] = jnp.zeros_like(dk_ref)
            dv_ref[...] = jnp.zeros_like(dv_ref)

        ahead, tr, tc = _sb_iotas(i)
        tri_incl = (tr >= tc).astype(BF)
        tri_strict = (tr > tc).astype(BF)
        stack, unstack = _sb_stack()
        qs = stack(q_ref[...] * ATT_SCALE)
        dobs = stack(do_ref[...])
        o_all = o_ref[...]
        dsum = jnp.sum(dobs.astype(F32) * jnp.concatenate([o_all] * SB_PAIR, axis=0), axis=1, keepdims=True)

        def cond(c):
            return jnp.logical_and(c[0] >= 0, c[-1] > SB_EXIT)

        def step(c):
            kb, r_run, g_run, dq, _ = c
            off = pl.multiple_of(kb * B, B)
            causal = ahead < -kb * B
            kk = k_ref[pl.ds(off, B), :]
            vv = v_ref[pl.ds(off, B), :]
            z, e, cin, a = _sb_block(qs, kk, causal, r_run, tri_incl)
            a16 = a.astype(BF)
            gmat = a16.astype(F32) * _dot(dobs, vv, _NT)
            later = _split_dot(gmat, tri_strict)
            pfx = dsum - g_run - later
            sig = jnp.where(z >= 0, 1.0, e) / (1.0 + e)
            dz = jnp.where(causal, gmat - sig * pfx, 0.0).astype(BF)
            dq = dq + _dot(dz, kk)
            dk_ref[pl.ds(off, B), :] += _dot(dz, qs, _TN)
            dv_ref[pl.ds(off, B), :] += _dot(a16, dobs, _TN)
            g_run = g_run + jnp.sum(gmat, axis=1, keepdims=True)
            r_run = r_run + cin[:, 0:1]
            return kb - 1, r_run, g_run, dq, jnp.max(r_run)

        init = (i, jnp.zeros((R, 1), F32), jnp.zeros((R, 1), F32), jnp.zeros((R, SB_PAIR * hd), F32),
                jnp.float32(0.0))
        fin = lax.while_loop(cond, step, init)
        dq_ref[...] = unstack(fin[3]) * ATT_SCALE

    out = jax.ShapeDtypeStruct((S, W_B), F32)
    res = _call(body, name="sb_bwd", grid=(SB_HEADS // SB_PAIR, nq), in_specs=[q_spec, k_spec, v_spec, o_spec, o_spec],
                out_specs=(o_spec, full, full), out_shape=(out, out, out), args=(proj, proj, proj, do, o), comm=comm)
    return res[0], res[1], res[2], list(res[3:])


def _merge_wo(o_g, l_g, o_b, proj, b_gate, w_br, w_o, x, name, comm=()):
    S = o_b.shape[0]
    D = D_MODEL
    tm = _pick(S, 256)
    gcol = GATE_OFF // D

    def body(o0, o1, o2, l0, l1, l2, ob_ref, ga_ref, gb_ref, bg_ref, w_ref, wo_ref, x_ref,
             x1_ref, mg_ref, oa_ref, lse_ref):
        la, lb, lc = l0[...], l1[...], l2[...]
        mx = jnp.maximum(jnp.maximum(la, lb), lc)
        ea, eb, ec = jnp.exp(la - mx), jnp.exp(lb - mx), jnp.exp(lc - mx)
        den = ea + eb + ec
        oa = (ea * o0[...] + eb * o1[...] + ec * o2[...]) / den
        oa_ref[...] = oa
        lse_ref[...] = mx + jnp.log(den)
        oa16 = oa.astype(BF)
        ob16 = ob_ref[...].astype(BF)
        acc = x_ref[...]
        for c0 in range(0, D, FFN_CHUNK):
            cs = slice(c0, c0 + FFN_CHUNK)
            ya = _dot(oa16, w_ref[0:OUT_A, cs])
            yb = _dot(ob16, w_ref[OUT_A:OUT_A + W_B, cs])
            g_a = jax.nn.sigmoid(ga_ref[:, cs].astype(F32) + bg_ref[:, cs])
            g_b = jax.nn.sigmoid(gb_ref[:, cs].astype(F32) + bg_ref[:, D + c0:D + c0 + FFN_CHUNK])
            mg = (g_a * ya + g_b * yb).astype(BF)
            mg_ref[:, cs] = mg
            acc = acc + _dot(mg, wo_ref[cs, :])
        x1_ref[...] = acc

    nar = pl.BlockSpec((tm, OUT_A), lambda i: (i, 0))
    wide = pl.BlockSpec((tm, D), lambda i: (i, 0))
    res = _call(body, name=name, grid=(S // tm,),
                in_specs=[nar] * 7 + [pl.BlockSpec((tm, D), lambda i: (i, gcol)),
                                      pl.BlockSpec((tm, D), lambda i: (i, gcol + 1)),
                                      pl.BlockSpec((1, 2 * D), lambda i: (0, 0)),
                                      pl.BlockSpec((OUT_A + W_B, D), lambda i: (0, 0)),
                                      pl.BlockSpec((D, D), lambda i: (0, 0)), wide],
                out_specs=(wide, wide, nar, nar),
                out_shape=(jax.ShapeDtypeStruct((S, D), F32), jax.ShapeDtypeStruct((S, D), BF),
                           jax.ShapeDtypeStruct((S, OUT_A), F32), jax.ShapeDtypeStruct((S, OUT_A), F32)),
                args=(*o_g, *l_g, o_b, proj, proj, b_gate.reshape(1, 2 * D), w_br, w_o, x), comm=comm)
    return res[0], res[1], res[2], res[3], list(res[4:])


STAT_OFF = HEAD_DIM // 2


def _merge_bwd(dx, w_o, oa, ob, lse, proj, b_gate, w_br):
    S = ob.shape[0]
    D = D_MODEL
    tm = _pick(S, 256)
    gcol = GATE_OFF // D

    def body(dx_ref, wo_ref, oa_ref, ob_ref, l_ref, ga_ref, gb_ref, bg_ref, w_ref,
             dya_ref, dyb_ref, doa_ref, dob_ref, st_ref, dg_ref, dbg_ref):
        @pl.when(pl.program_id(0) == 0)
        def _():
            dbg_ref[...] = jnp.zeros_like(dbg_ref)

        dx16 = dx_ref[...].astype(BF)
        oa = oa_ref[...]
        oa16 = oa.astype(BF)
        ob16 = ob_ref[...].astype(BF)
        doa = jnp.zeros((tm, OUT_A), F32)
        dob = jnp.zeros((tm, W_B), F32)
        for c0 in range(0, D, FFN_CHUNK):
            cs = slice(c0, c0 + FFN_CHUNK)
            cs2 = slice(D + c0, D + c0 + FFN_CHUNK)
            wa = w_ref[0:OUT_A, cs]
            wb = w_ref[OUT_A:OUT_A + W_B, cs]
            dm = _dot(dx16, wo_ref[cs, :], _NT)
            ya = _dot(oa16, wa)
            yb = _dot(ob16, wb)
            g_a = jax.nn.sigmoid(ga_ref[:, cs].astype(F32) + bg_ref[:, cs])
            g_b = jax.nn.sigmoid(gb_ref[:, cs].astype(F32) + bg_ref[:, cs2])
            dga = dm * ya * g_a * (1.0 - g_a)
            dgb = dm * yb * g_b * (1.0 - g_b)
            dg_ref[:, cs] = dga.astype(BF)
            dg_ref[:, cs2] = dgb.astype(BF)
            dbg_ref[:, cs] += jnp.sum(dga, axis=0, keepdims=True)
            dbg_ref[:, cs2] += jnp.sum(dgb, axis=0, keepdims=True)
            dya = (dm * g_a).astype(BF)
            dyb = (dm * g_b).astype(BF)
            dya_ref[:, cs] = dya
            dyb_ref[:, cs] = dyb
            doa = doa + _dot(dya, wa, _NT)
            dob = dob + _dot(dyb, wb, _NT)
        doa = doa.astype(BF)
        doa_ref[...] = doa
        dob_ref[...] = dob.astype(BF)
        r = lax.broadcasted_iota(jnp.int32, (OUT_A, OUT_A), 0) // HEAD_DIM
        c = lax.broadcasted_iota(jnp.int32, (OUT_A, OUT_A), 1) // HEAD_DIM
        dsum = _split_dot(doa.astype(F32) * oa, (r == c).astype(BF))
        lane = lax.broadcasted_iota(jnp.int32, dsum.shape, 1) % HEAD_DIM
        st_ref[...] = jnp.where(lane < STAT_OFF, l_ref[...], dsum)

    nar = pl.BlockSpec((tm, OUT_A), lambda i: (i, 0))
    wide = pl.BlockSpec((tm, D), lambda i: (i, 0))
    wide2 = pl.BlockSpec((tm, 2 * D), lambda i: (i, 0))
    vec2 = pl.BlockSpec((1, 2 * D), lambda i: (0, 0))
    return pl.pallas_call(
        body, name="merge_bwd",
        out_shape=(jax.ShapeDtypeStruct((S, D), BF), jax.ShapeDtypeStruct((S, D), BF),
                   jax.ShapeDtypeStruct((S, OUT_A), BF), jax.ShapeDtypeStruct((S, W_B), BF),
                   jax.ShapeDtypeStruct((S, OUT_A), F32), jax.ShapeDtypeStruct((S, 2 * D), BF),
                   jax.ShapeDtypeStruct((1, 2 * D), F32)),
        grid=(S // tm,),
        in_specs=[wide, pl.BlockSpec((D, D), lambda i: (0, 0)), nar, nar, nar,
                  pl.BlockSpec((tm, D), lambda i: (i, gcol)), pl.BlockSpec((tm, D), lambda i: (i, gcol + 1)), vec2,
                  pl.BlockSpec((OUT_A + W_B, D), lambda i: (0, 0))],
        out_specs=(wide, wide, nar, nar, nar, wide2, vec2),
        compiler_params=_params("arbitrary"),
    )(dx, w_o, oa, ob, lse, proj, proj, b_gate.reshape(1, 2 * D), w_br)


_SQRT_HALF = 0.7071067811865476
_INV_SQRT_2PI = 0.3989422804014327


def _gelu_parts(a):
    cdf = 0.5 * (1.0 + lax.erf(a * _SQRT_HALF))
    pdf = _INV_SQRT_2PI * jnp.exp(-0.5 * a * a)
    return cdf, pdf


def _shift_down(a, halo, k):
    rows = lax.broadcasted_iota(jnp.int32, a.shape, 0)
    out = pltpu.roll(a, k, 0)
    for r in range(k):
        out = jnp.where(rows == r, halo[8 - k + r:8 - k + r + 1, :], out)
    return out


def _ffn_specs(S, tm):
    F = D_FF
    t8 = tm // 8
    a_spec = pl.BlockSpec((tm, F), lambda i: (i, 0))
    v_spec = pl.BlockSpec((tm, F), lambda i: (i, 1))
    halo_prev = pl.BlockSpec((8, F), lambda i: (jnp.maximum(i * t8 - 1, 0), 0))
    return a_spec, v_spec, halo_prev


def _ffn_down(up, conv_w, conv_b, w_down, res, *, name, comm=()):
    S = up.shape[0]
    F = D_FF
    D = w_down.shape[1]
    tm = _pick(S, 256)
    a_spec, v_spec, halo_prev = _ffn_specs(S, tm)

    def body(a_ref, h_ref, v_ref, w_ref, b_ref, wd_ref, r_ref, o_ref, act_ref, ac_ref):
        first = pl.program_id(0) == 0
        acc = r_ref[...]
        for c0 in range(0, F, FFN_CHUNK):
            cs = slice(c0, c0 + FFN_CHUNK)
            a = a_ref[:, cs].astype(F32)
            halo = jnp.where(first, 0.0, h_ref[:, cs].astype(F32))
            w = w_ref[:, cs]
            ac = b_ref[:, cs] + w[0:1, :] * _shift_down(a, halo, 2) + w[1:2, :] * _shift_down(a, halo, 1) + w[2:3, :] * a
            ac_ref[:, cs] = ac.astype(BF)
            cdf, _ = _gelu_parts(ac)
            act = (ac * cdf * v_ref[:, cs].astype(F32)).astype(BF)
            act_ref[:, cs] = act
            acc = acc + _dot(act, wd_ref[cs, :])
        o_ref[...] = acc

    row = pl.BlockSpec((tm, D), lambda i: (i, 0))
    res_ = _call(body, name=name, grid=(S // tm,),
                 in_specs=[a_spec, halo_prev, v_spec, pl.BlockSpec((3, F), lambda i: (0, 0)),
                           pl.BlockSpec((1, F), lambda i: (0, 0)), pl.BlockSpec((F, D), lambda i: (0, 0)), row],
                 out_specs=(row, a_spec, a_spec),
                 out_shape=(jax.ShapeDtypeStruct((S, D), F32), jax.ShapeDtypeStruct((S, F), BF),
                            jax.ShapeDtypeStruct((S, F), BF)),
                 args=(up, up, up, conv_w, conv_b.reshape(1, F), w_down, res), comm=comm)
    return res_[0], res_[1], res_[2], list(res_[3:])


def _shift_up_pair(a, nxt):
    n = a.shape[0]
    r8 = lax.broadcasted_iota(jnp.int32, (8,) + a.shape[1:], 0)
    out = []
    for k in (1, 2):
        rolled = pltpu.roll(a, n - k, 0)
        tail = jnp.where(r8 >= 8 - k, pltpu.roll(nxt, 8 - k, 0), rolled[n - 8:n])
        out.append(jnp.concatenate([rolled[0:n - 8], tail], axis=0))
    return out


def _ffn_bwd(dx, w_down, up, ac, conv_w, comm=()):
    S = up.shape[0]
    F = D_FF
    D = dx.shape[1]
    tm = _pick(S, 256)
    t8 = tm // 8
    nt = S // tm
    a_spec, v_spec, _ = _ffn_specs(S, tm)

    def nxt(width, col):
        return pl.BlockSpec((8, width), lambda i: (jnp.minimum((i + 1) * t8, S // 8 - 1), col))

    def body(dx_ref, dxn_ref, wd_ref, ac_ref, acn_ref, a_ref, v_ref, vn_ref, w_ref, dup_ref, dw_ref, db_ref):
        i = pl.program_id(0)

        @pl.when(i == 0)
        def _():
            dw_ref[...] = jnp.zeros_like(dw_ref)
            db_ref[...] = jnp.zeros_like(db_ref)

        dx16 = dx_ref[...].astype(BF)
        dxn16 = dxn_ref[...].astype(BF)
        last = i == nt - 1

        def dconv(dact, ac, v):
            cdf, pdf = _gelu_parts(ac)
            return cdf, dact * v * (cdf + ac * pdf)

        for c0 in range(0, F, FFN_CHUNK):
            cs = slice(c0, c0 + FFN_CHUNK)
            wd = wd_ref[cs, :]
            dact = _dot(dx16, wd, _NT)
            ac = ac_ref[:, cs].astype(F32)
            cdf, dac = dconv(dact, ac, v_ref[:, cs].astype(F32))
            dup_ref[:, F + c0:F + c0 + FFN_CHUNK] = (dact * ac * cdf).astype(BF)
            _, dac_n = dconv(_dot(dxn16, wd, _NT), acn_ref[:, cs].astype(F32), vn_ref[:, cs].astype(F32))
            d1, d2 = _shift_up_pair(dac, jnp.where(last, 0.0, dac_n))
            w = w_ref[:, cs]
            dup_ref[:, cs] = (w[2:3, :] * dac + w[1:2, :] * d1 + w[0:1, :] * d2).astype(BF)
            a = a_ref[:, cs].astype(F32)
            db_ref[:, cs] += jnp.sum(dac, axis=0, keepdims=True)
            dw_ref[0:1, cs] += jnp.sum(d2 * a, axis=0, keepdims=True)
            dw_ref[1:2, cs] += jnp.sum(d1 * a, axis=0, keepdims=True)
            dw_ref[2:3, cs] += jnp.sum(dac * a, axis=0, keepdims=True)

    w_spec = pl.BlockSpec((3, F), lambda i: (0, 0))
    b_spec = pl.BlockSpec((1, F), lambda i: (0, 0))
    res = _call(body, name="ffn_bwd", grid=(nt,),
                in_specs=[pl.BlockSpec((tm, D), lambda i: (i, 0)), nxt(D, 0), pl.BlockSpec((F, D), lambda i: (0, 0)),
                          a_spec, nxt(F, 0), a_spec, v_spec, nxt(F, 1), w_spec],
                out_specs=(pl.BlockSpec((tm, 2 * F), lambda i: (i, 0)), w_spec, b_spec),
                out_shape=(jax.ShapeDtypeStruct((S, 2 * F), BF), jax.ShapeDtypeStruct((3, F), F32),
                           jax.ShapeDtypeStruct((1, F), F32)),
                args=(dx, dx, w_down, ac, ac, up, up, up, conv_w), comm=comm)
    return res[0], res[1], res[2], list(res[3:])


def _adamw(parts, w, m, v, name, row0=0, prev=None):
    R, C = w.shape
    Rp = parts.shape[1]
    tr = Rp
    for cand in (512, 256, 128, 64, 32, 16):
        if Rp % cand == 0 and row0 % cand == 0 and cand * C * 4 <= (1 << 21):
            tr = cand
            break
    b0 = row0 // tr
    c1 = 1.0 / (1.0 - ADAM_B1 ** ADAM_STEP)
    c2 = 1.0 / (1.0 - ADAM_B2 ** ADAM_STEP)

    def body(p_ref, w_ref, m_ref, v_ref, *rest):
        g_ref, d_ref, nm_ref, nv_ref = rest[-4:]
        g = p_ref[0].astype(F32)
        for j in range(1, N_DEV):
            g = g + p_ref[j].astype(F32)
        nm = ADAM_B1 * m_ref[...] + (1.0 - ADAM_B1) * g
        nv = ADAM_B2 * v_ref[...] + (1.0 - ADAM_B2) * (g * g)
        g_ref[...] = g
        nm_ref[...] = nm
        nv_ref[...] = nv
        d_ref[...] = -ADAM_LR * ((nm * c1) / (jnp.sqrt(nv * c2) + ADAM_EPS) + ADAM_WD * w_ref[...])

    blk = pl.BlockSpec((tr, C), lambda i: (b0 + i, 0))
    out = jax.ShapeDtypeStruct((R, C), F32)
    carried = [] if prev is None else list(prev)
    return pl.pallas_call(
        body, name=name,
        out_shape=(out, out, out, out),
        grid=(Rp // tr,),
        in_specs=[pl.BlockSpec((N_DEV, tr, C), lambda i: (0, i, 0)), blk, blk, blk]
        + [pl.BlockSpec(memory_space=pl.ANY)] * len(carried),
        out_specs=(blk, blk, blk, blk),
        input_output_aliases={4 + k: k for k in range(len(carried))},
        compiler_params=_params("parallel"),
    )(parts, w, m, v, *carried)


def _dil(t, dil):
    S, C = t.shape
    if dil == 1:
        return t.reshape(1, S, C)
    return t.reshape(S // dil, dil, C).transpose(1, 0, 2)


def _undil(t):
    d, L, C = t.shape
    if d == 1:
        return t.reshape(L, C)
    return t.transpose(1, 0, 2).reshape(L * d, C)


def _group_qkv(proj, g):
    dil = DSW_GROUPS[g][1]
    if dil == 1:
        p3 = _dil(proj, 1)
        return (p3, p3, p3), (g, W_A // OUT_A + g, 2 * W_A // OUT_A + g)
    c0 = g * OUT_A
    return tuple(_dil(proj[:, o + c0:o + c0 + OUT_A], dil) for o in (0, W_A, 2 * W_A)), (0, 0, 0)


_COL_SHARDED = ("w_in", "w_br", "w_up")
_ROW_SHARDED = ("w_o", "w_down")


class _Plan:
    def __init__(self):
        self.riders = {}
        self.landed = {}

    def ride(self, slot, key, kind, x):
        self.riders.setdefault(slot, []).append((key, kind, x))

    def run(self, slot, fn, *args, **kw):
        items = self.riders.pop(slot, [])
        res = fn(*args, comm=[(kind, x) for _, kind, x in items], **kw)
        for (key, _, _), r in zip(items, res[-1]):
            self.landed[key] = r
        return res[0] if len(res) == 2 else res[:-1]

    def weight(self, n, l):
        g = self.landed[(n, l)]
        if n in _COL_SHARDED:
            return g.transpose(1, 0, 2).reshape(g.shape[1], -1)
        return g.reshape(-1, g.shape[2])

    def scatter(self, slot, n, l, full, part=0):
        K, N = full.shape
        if n in _COL_SHARDED:
            blocks = full.reshape(K, N_DEV, N // N_DEV).transpose(1, 0, 2)
        else:
            blocks = full.reshape(N_DEV, K // N_DEV, N)
        self.ride(slot, ("d" + n, l, part), "scatter", blocks)


def _layer_fwd(x, p, plan, l):
    h = _rms_fwd(x, p["norm1"][l], f"rms1_fwd_{l}")
    proj = plan.run(f"proj_{l}", _matmul, h, plan.weight("w_in", l), mode="nn", out_dtype=BF, name=f"proj_{l}",
                    tm=1024, tn=1024, tk=1024)
    o_g, l_g, qkv_g = [], [], []
    for g in range(N_GROUPS):
        qkv, cols = _group_qkv(proj, g)
        og, lg = plan.run(f"attn_a_fwd_g{g}_{l}", _attn_a_fwd, qkv, cols, g)
        o_g.append(_undil(og))
        l_g.append(_undil(lg))
        qkv_g.append((qkv, cols))
    ob = plan.run(f"sb_fwd_{l}", _sb_fwd, proj)
    x1, merged, oa, lse = plan.run(f"wo_{l}", _merge_wo, o_g, l_g, ob, proj, p["b_gate"][l], plan.weight("w_br", l),
                                   plan.weight("w_o", l), x, f"wo_{l}")
    h2 = _rms_fwd(x1, p["norm2"][l], f"rms2_fwd_{l}")
    up = plan.run(f"up_{l}", _matmul, h2, plan.weight("w_up", l), mode="nn", out_dtype=BF, name=f"up_{l}",
                  tm=1024, tn=1408, tk=1024)
    x2, act, ac = plan.run(f"down_{l}", _ffn_down, up, p["conv_w"][l], p["conv_b"][l], plan.weight("w_down", l), x1,
                           name=f"down_{l}")
    saved = dict(x=x, h=h, proj=proj, qkv_g=qkv_g, oa=oa, ob=ob, lse=lse, merged=merged, x1=x1, h2=h2, up=up, act=act, ac=ac)
    return x2, saved


def _layer_bwd(dx2, sv, p, plan, l):
    gr = {}
    dwd = plan.run(f"dw_down_{l}", _matmul, sv["act"], dx2, mode="tn", out_dtype=BF, name=f"dw_down_{l}",
                   tm=1408, tn=1024, tk=2048)
    plan.scatter(f"d_h2_{l}", "w_down", l, dwd)
    dup, gr["conv_w"], dcb = plan.run(f"ffn_bwd_{l}", _ffn_bwd, dx2, plan.weight("w_down", l), sv["up"], sv["ac"],
                                      p["conv_w"][l])
    gr["conv_b"] = dcb[0]
    dx1, dn2 = plan.run(f"d_h2_{l}", _matmul_rms_bwd, dup, plan.weight("w_up", l), sv["x1"], p["norm2"][l], dx2,
                        name=f"d_h2_{l}")
    dwu = plan.run(f"dw_up_{l}", _matmul, sv["h2"], dup, mode="tn", out_dtype=BF, name=f"dw_up_{l}",
                   tm=1024, tn=1408, tk=2048)
    plan.scatter(f"sb_bwd_{l}", "w_up", l, dwu)
    gr["norm2"] = dn2[0]
    dwo = plan.run(f"dw_o_{l}", _matmul, sv["merged"], dx1, mode="tn", out_dtype=BF, name=f"dw_o_{l}",
                   tm=1024, tn=1024, tk=2048)
    plan.scatter(f"dw_in_{l}", "w_o", l, dwo)
    dya, dyb, doa, dob, stats, dgate, dbg = _merge_bwd(dx1, plan.weight("w_o", l), sv["oa"], sv["ob"], sv["lse"],
                                                       sv["proj"], p["b_gate"][l], plan.weight("w_br", l))
    gr["b_gate"] = dbg[0]
    dwa = plan.run(f"dw_bra_{l}", _matmul, sv["oa"], dya, mode="tn", out_dtype=BF, name=f"dw_bra_{l}",
                   tm=256, tn=1024, tk=2048)
    dwb = plan.run(f"dw_brb_{l}", _matmul, sv["ob"], dyb, mode="tn", out_dtype=BF, name=f"dw_brb_{l}",
                   tm=256, tn=1024, tk=2048)
    plan.scatter(f"dw_in_{l}", "w_br", l, jnp.concatenate([dwa, dwb], axis=0))
    proj = sv["proj"]
    dq_a, dk_a, dv_a = [], [], []
    for g, (_, dil) in enumerate(DSW_GROUPS):
        qkv, cols = sv["qkv_g"][g]
        dqg, dkg, dvg = _attn_a_bwd(qkv, cols, _dil(doa, dil), _dil(stats, dil), g)
        dq_a.append(_undil(dqg))
        dk_a.append(_undil(dkg))
        dv_a.append(_undil(dvg))
    dqb, dkb, dvb = plan.run(f"sb_bwd_{l}", _sb_bwd, proj, dob, sv["ob"])
    dproj = jnp.concatenate(dq_a + dk_a + dv_a + [dqb.astype(BF), dkb.astype(BF), dvb.astype(BF), dgate], axis=1)
    dx, dn1 = plan.run(f"d_h_{l}", _matmul_rms_bwd, dproj, plan.weight("w_in", l), sv["x"], p["norm1"][l], dx1,
                       name=f"d_h_{l}")
    if l > 0:
        dwi = plan.run(f"dw_in_{l}", _matmul, sv["h"], dproj, mode="tn", out_dtype=BF, name=f"dw_in_{l}",
                       tm=1024, tn=1280, tk=2048)
        plan.scatter(f"ffn_bwd_{l - 1}", "w_in", l, dwi)
    else:
        half = D_MODEL // 2
        for part in range(2):
            name = f"dw_in_{l}" if part == 0 else f"dw_in_{l}_rest"
            dwi = plan.run(name, _matmul, sv["h"], dproj, mode="tn", out_dtype=BF, name=name, tm=half, tn=1280,
                           tk=2048, m_cols=half, m_off=part * half)
            plan.scatter(f"dw_in_{l}_rest" if part == 0 else "alone", "w_in", l, dwi, part)
    gr["norm1"] = dn1[0]
    return dx, gr


def kernel(x, norm1, w_in, b_gate, w_br, w_o, norm2, w_up, conv_w, conv_b, w_down, norm_f, loss_target, m_norm1, m_w_in, m_b_gate, m_w_br, m_w_o, m_norm2, m_w_up, m_conv_w, m_conv_b, m_w_down, m_norm_f, v_norm1, v_w_in, v_b_gate, v_w_br, v_w_o, v_norm2, v_w_up, v_conv_w, v_conv_b, v_w_down, v_norm_f):
    depth = norm1.shape[0]
    me = 4 * lax.axis_index("x") + 2 * lax.axis_index("y") + lax.axis_index("c")
    shards = dict(w_in=w_in, w_br=w_br, w_o=w_o, w_up=w_up, w_down=w_down)
    moments_m = dict(norm1=m_norm1, w_in=m_w_in, b_gate=m_b_gate, w_br=m_w_br, w_o=m_w_o, norm2=m_norm2,
                     w_up=m_w_up, conv_w=m_conv_w, conv_b=m_conv_b, w_down=m_w_down, norm_f=m_norm_f)
    moments_v = dict(norm1=v_norm1, w_in=v_w_in, b_gate=v_b_gate, w_br=v_w_br, w_o=v_w_o, norm2=v_norm2,
                     w_up=v_w_up, conv_w=v_conv_w, conv_b=v_conv_b, w_down=v_w_down, norm_f=v_norm_f)

    plan = _Plan()
    wb = {n: s.astype(BF) for n, s in shards.items()}
    p = dict(norm1=norm1, b_gate=b_gate, norm2=norm2, conv_b=conv_b)
    cw = _all_gather(conv_w, "gather_conv_w")
    p["conv_w"] = cw.transpose(1, 2, 0, 3).reshape(depth, 3, D_FF)
    plan.landed[("w_in", 0)] = _all_gather_via_sibling(wb["w_in"][0], "gather_w_in_0")
    for l in range(depth):
        plan.ride(f"proj_{l}", ("w_down", l), "gather", wb["w_down"][l])
        plan.ride(f"attn_a_fwd_g0_{l}" if l == 0 else f"down_{l - 1}", ("w_br", l), "gather", wb["w_br"][l])
        plan.ride(f"attn_a_fwd_g0_{l}" if l == 0 else f"down_{l - 1}", ("w_o", l), "gather", wb["w_o"][l])
        plan.ride(f"sb_fwd_{l}", ("w_up", l), "gather", wb["w_up"][l])
        if l + 1 < depth:
            plan.ride(f"up_{l}", ("w_in", l + 1), "gather", wb["w_in"][l + 1])

    xs = x[0]
    saved = []
    for l in range(depth):
        xs, sv = _layer_fwd(xs, p, plan, l)
        saved.append(sv)
    loss_part, dx, dnf = _loss_head(xs, norm_f, loss_target[0])
    loss = lax.psum(loss_part[0, 0], ("x", "y", "c"))

    grads = [None] * depth
    for l in reversed(range(depth)):
        dx, grads[l] = _layer_bwd(dx, saved[l], p, plan, l)
    grad_x = dx[None]
    (key, _, last), = plan.riders.pop("alone")
    plan.landed[key] = _all_to_all(last, "scatter_w_in_rest")
    assert not plan.riders, sorted(plan.riders)

    out_g, out_d, out_m, out_v = {}, {}, {}, {}
    for n in _COL_SHARDED + _ROW_SHARDED:
        shp = shards[n].shape
        flat = (shp[0] * shp[1], shp[2])
        res = None
        for l in range(depth):
            row = l * shp[1]
            for key in sorted(k for k in plan.landed if k[:2] == ("d" + n, l)):
                parts = plan.landed[key]
                res = _adamw(parts, shards[n].reshape(flat), moments_m[n].reshape(flat), moments_v[n].reshape(flat),
                             f"adamw_{n}_{l}_{key[2]}", row0=row, prev=res)
                row += parts.shape[1]
        out_g[n], out_d[n], out_m[n], out_v[n] = [r.reshape(shp) for r in res]

    small = ("norm1", "b_gate", "norm2", "conv_b")
    vecs = [jnp.stack([grads[l][n] for l in range(depth)]).reshape(-1) for n in small]
    vecs.append(dnf.reshape(-1))
    vecs.append(jnp.stack([grads[l]["conv_w"] for l in range(depth)]).reshape(-1))
    sizes = [v.shape[0] for v in vecs]
    flat = jnp.concatenate(vecs)
    n_small = sum(sizes[:-1])
    pad = (-flat.shape[0]) % 1024
    flat = jnp.pad(flat, (0, pad)).reshape(-1, 128)
    allp = _all_gather(flat, "gather_small_grads").reshape(N_DEV, -1)
    rep_w = jnp.concatenate([norm1.reshape(-1), b_gate.reshape(-1), norm2.reshape(-1), conv_b.reshape(-1), norm_f])
    rep_m = jnp.concatenate([moments_m[n].reshape(-1) for n in small] + [m_norm_f])
    rep_v = jnp.concatenate([moments_v[n].reshape(-1) for n in small] + [v_norm_f])
    rows = n_small // 128
    res = _adamw(allp[:, :n_small].reshape(N_DEV, rows, 128), rep_w.reshape(rows, 128), rep_m.reshape(rows, 128),
                 rep_v.reshape(rows, 128), "adamw_small")
    off = 0
    for n, sz in zip(small + ("norm_f",), sizes[:-1]):
        shp = norm_f.shape if n == "norm_f" else p[n].shape
        out_g[n], out_d[n], out_m[n], out_v[n] = [r.reshape(-1)[off:off + sz].reshape(shp) for r in res]
        off += sz
    f = conv_w.shape[2]
    cwp = allp[:, n_small:n_small + sizes[-1]].reshape(N_DEV, depth * 3, D_FF)
    cwp = lax.dynamic_slice_in_dim(cwp, me * f, f, axis=2)
    res = _adamw(cwp, conv_w.reshape(depth * 3, f), m_conv_w.reshape(depth * 3, f), v_conv_w.reshape(depth * 3, f),
                 "adamw_conv_w")
    out_g["conv_w"], out_d["conv_w"], out_m["conv_w"], out_v["conv_w"] = [r.reshape(conv_w.shape) for r in res]

    order = ("norm1", "w_in", "b_gate", "w_br", "w_o", "norm2", "w_up", "conv_w", "conv_b", "w_down", "norm_f")
    return (loss, grad_x, *[out_g[n] for n in order], *[out_d[n] for n in order],
            *[out_m[n] for n in order], *[out_v[n] for n in order])
```

```python
import functools

import jax
import jax.numpy as jnp
from jax import lax
from jax.experimental import pallas as pl
from jax.experimental.pallas import tpu as pltpu

BF = jnp.bfloat16
F32 = jnp.float32

N_DEV = 8
D_MODEL = 1024
HEAD_DIM = 64
DSW_GROUPS = ((128, 1), (512, 4), (2048, 16))
HEADS_PER_GROUP = 4
N_GROUPS = len(DSW_GROUPS)
DSW_HEADS = HEADS_PER_GROUP * N_GROUPS
SB_HEADS = 4
W_A = DSW_HEADS * HEAD_DIM
W_B = SB_HEADS * HEAD_DIM
OUT_A = HEADS_PER_GROUP * HEAD_DIM
N_IN = 3 * W_A + 3 * W_B + 2 * D_MODEL
GATE_OFF = 3 * W_A + 3 * W_B
D_FF = 2816
SB_BLOCK = 256
RMS_EPS = 1e-6
ATT_SCALE = HEAD_DIM ** -0.5
NEG = -1e30
SB_EXIT = -110.0
FFN_CHUNK = 256

ADAM_LR = 0.001
ADAM_B1 = 0.9
ADAM_B2 = 0.999
ADAM_EPS = 1e-08
ADAM_WD = 0.01
ADAM_STEP = 10

HBM_SPEC = pl.BlockSpec(memory_space=pltpu.HBM)
MESH = pl.DeviceIdType.MESH

_NN = (((1,), (0,)), ((), ()))
_NT = (((1,), (1,)), ((), ()))
_TN = (((0,), (0,)), ((), ()))


def _dot(a, b, dn=_NN):
    return lax.dot_general(a, b, dn, preferred_element_type=F32)


def _pick(dim, pref):
    if dim <= pref:
        return dim
    t = (pref // 128) * 128
    while t >= 128:
        if dim % t == 0:
            return t
        t -= 128
    return dim


def _params(*sem):
    return pltpu.CompilerParams(dimension_semantics=sem)


def _peer(k):
    x, y, c = lax.axis_index("x"), lax.axis_index("y"), lax.axis_index("c")
    px = 1 - x if (k >> 2) & 1 else x
    py = 1 - y if (k >> 1) & 1 else y
    pc = 1 - c if k & 1 else c
    return (px, py, pc), 4 * px + 2 * py + pc


def _exchange(kind, x_ref, out_ref, send_sems, recv_sems, local_sem):
    gather = kind == "gather"
    _, me = _peer(0)

    def src(idx):
        return x_ref if gather else x_ref.at[idx]

    def copy(k, dst_idx):
        peer, pidx = _peer(k)
        return pltpu.make_async_remote_copy(
            src_ref=src(pidx), dst_ref=out_ref.at[dst_idx], send_sem=send_sems.at[k - 1],
            recv_sem=recv_sems.at[k - 1], device_id=peer, device_id_type=MESH)

    mine = pltpu.make_async_copy(src(me), out_ref.at[me], local_sem)

    def start():
        mine.start()
        for k in range(1, N_DEV):
            copy(k, me).start()

    def wait():
        for k in range(1, N_DEV):
            copy(k, _peer(k)[1]).wait_recv()
        for k in range(1, N_DEV):
            copy(k, me).wait_send()
        mine.wait()

    return start, wait


_EXCHANGE_SEMS = [pltpu.SemaphoreType.DMA((N_DEV - 1,)), pltpu.SemaphoreType.DMA((N_DEV - 1,)),
                  pltpu.SemaphoreType.DMA]


def _exchange_shape(kind, x):
    return jax.ShapeDtypeStruct(((N_DEV,) + x.shape) if kind == "gather" else x.shape, x.dtype)


def _exchange_alone(kind, x, name):
    def body(x_ref, out_ref, send_sems, recv_sems, local_sem):
        start, wait = _exchange(kind, x_ref, out_ref, send_sems, recv_sems, local_sem)
        start()
        wait()

    return pl.pallas_call(
        body, name=name, out_shape=_exchange_shape(kind, x),
        in_specs=[HBM_SPEC], out_specs=HBM_SPEC, scratch_shapes=list(_EXCHANGE_SEMS),
    )(x)


def _all_gather_via_sibling(x, name):
    def body(x_ref, out_ref, send_sems, recv_sems, local_sem):
        x_, y_, c_ = lax.axis_index("x"), lax.axis_index("y"), lax.axis_index("c")
        me, sibling = (x_, y_, c_), (x_, y_, 1 - c_)
        chips = [(1 - x_, y_), (x_, 1 - y_), (1 - x_, 1 - y_)]

        def slot(px, py, pc):
            return out_ref.at[4 * px + 2 * py + pc]

        def copy(k, block, to, src=None):
            return pltpu.make_async_remote_copy(
                src_ref=slot(*block) if src is None else src, dst_ref=slot(*block), send_sem=send_sems.at[k],
                recv_sem=recv_sems.at[k], device_id=to, device_id_type=MESH)

        mine = pltpu.make_async_copy(x_ref, slot(*me), local_sem)
        mine.start()
        first = [copy(0, me, sibling, src=x_ref)]
        first += [copy(1 + j, me, (*chip, c_), src=x_ref) for j, chip in enumerate(chips)]
        for cp in first:
            cp.start()
        passed = [copy(4 + j, (*chip, c_), sibling) for j, chip in enumerate(chips)]
        for j, chip in enumerate(chips):
            copy(1 + j, (*chip, c_), me).wait_recv()
            passed[j].start()
        copy(0, sibling, me).wait_recv()
        for j, chip in enumerate(chips):
            copy(4 + j, (*chip, 1 - c_), me).wait_recv()
        for cp in first + passed:
            cp.wait_send()
        mine.wait()

    return pl.pallas_call(
        body, name=name, out_shape=_exchange_shape("gather", x),
        in_specs=[HBM_SPEC], out_specs=HBM_SPEC,
        scratch_shapes=[pltpu.SemaphoreType.DMA((N_DEV - 1,)), pltpu.SemaphoreType.DMA((N_DEV - 1,)),
                        pltpu.SemaphoreType.DMA],
    )(x)


def _all_gather(x, name):
    return _exchange_alone("gather", x, name)


def _all_to_all(x, name):
    return _exchange_alone("scatter", x, name)


def _call(body, *, name, grid, in_specs, out_specs, out_shape, args, scratch_shapes=(), sem=None, comm=()):
    single = not isinstance(out_shape, (tuple, list))
    outs = (out_shape,) if single else tuple(out_shape)
    ospecs = (out_specs,) if single else tuple(out_specs)
    if not comm:
        res = pl.pallas_call(
            body, name=name, out_shape=outs, grid=grid, in_specs=list(in_specs), out_specs=ospecs,
            scratch_shapes=list(scratch_shapes), compiler_params=_params(*(sem or ("arbitrary",) * len(grid))),
        )(*args)
        return res
    n_in, n_out, n_scr, nc = len(in_specs), len(outs), len(scratch_shapes), len(comm)

    def wrapped(*refs):
        ins = refs[:n_in]
        cins = refs[n_in:n_in + nc]
        o0 = n_in + nc
        kouts = refs[o0:o0 + n_out]
        couts = refs[o0 + n_out:o0 + n_out + nc]
        s0 = o0 + n_out + nc
        scr = refs[s0:s0 + n_scr]
        sems = refs[s0 + n_scr:]
        ids = [pl.program_id(ax) for ax in range(len(grid))]
        first = functools.reduce(jnp.logical_and, [i == 0 for i in ids])
        last = functools.reduce(jnp.logical_and, [i == g - 1 for i, g in zip(ids, grid)])
        ex = [_exchange(comm[c][0], cins[c], couts[c], *sems[3 * c:3 * c + 3]) for c in range(nc)]

        @pl.when(first)
        def _():
            for start, _ in ex:
                start()

        body(*ins, *kouts, *scr)

        @pl.when(last)
        def _():
            for _, wait in ex:
                wait()

    return pl.pallas_call(
        wrapped, name=name,
        out_shape=outs + tuple(_exchange_shape(k, x) for k, x in comm),
        grid=grid, in_specs=list(in_specs) + [HBM_SPEC] * nc, out_specs=ospecs + (HBM_SPEC,) * nc,
        scratch_shapes=list(scratch_shapes) + list(_EXCHANGE_SEMS) * nc,
        compiler_params=_params(*(("arbitrary",) * len(grid))),
    )(*args, *[x for _, x in comm])


def _matmul(a, b, *, mode, out_dtype, name, tm=512, tn=1024, tk=1024, res=None, comm=(), m_cols=None, m_off=0):
    if mode == "nn":
        (M, K), (_, N) = a.shape, b.shape
    elif mode == "nt":
        (M, K), (N, _) = a.shape, b.shape
    else:
        (K, M), (_, N) = a.shape, b.shape
        M = M if m_cols is None else m_cols
    tm, tn, tk = _pick(M, tm), _pick(N, tn), _pick(K, tk)
    i0 = m_off // tm
    nk = K // tk
    dn = {"nn": _NN, "nt": _NT, "tn": _TN}[mode]

    def body(*refs):
        a_ref, b_ref = refs[0], refs[1]
        r_ref = refs[2] if res is not None else None
        o_ref = refs[3] if res is not None else refs[2]

        def finish(r):
            if res is not None:
                r = r + r_ref[...].astype(F32)
            o_ref[...] = r.astype(out_dtype)

        part = _dot(a_ref[...].astype(BF), b_ref[...].astype(BF), dn)
        if nk == 1:
            finish(part)
            return
        acc = refs[-1]
        k = pl.program_id(2)

        @pl.when(k == 0)
        def _():
            acc[...] = jnp.zeros_like(acc)

        acc[...] += part

        @pl.when(k == nk - 1)
        def _():
            finish(acc[...])

    if mode == "tn":
        a_spec = pl.BlockSpec((tk, tm), lambda j, i, k: (k, i0 + i))
    else:
        a_spec = pl.BlockSpec((tm, tk), lambda j, i, k: (i, k))
    if mode == "nt":
        b_spec = pl.BlockSpec((tn, tk), lambda j, i, k: (j, k))
    else:
        b_spec = pl.BlockSpec((tk, tn), lambda j, i, k: (k, j))
    o_spec = pl.BlockSpec((tm, tn), lambda j, i, k: (i, j))
    in_specs = [a_spec, b_spec] + ([o_spec] if res is not None else [])
    args = (a, b) + ((res,) if res is not None else ())
    out = _call(body, name=name, grid=(N // tn, M // tm, nk), in_specs=in_specs, out_specs=o_spec,
                out_shape=jax.ShapeDtypeStruct((M, N), out_dtype), args=args,
                scratch_shapes=[pltpu.VMEM((tm, tn), F32)] if nk > 1 else [],
                sem=("parallel", "parallel", "arbitrary"), comm=comm)
    return out[0], list(out[1:])


def _norm_matmul(x, g, w, *, name, tm=512, tn=1024, comm=()):
    S, K = x.shape
    N = w.shape[1]
    tm, tn = _pick(S, tm), _pick(N, tn)
    slab = min(tm, 256)

    def body(x_ref, g_ref, w_ref, o_ref, h_ref):
        gg = g_ref[...]
        for r0 in range(0, tm, slab):
            rs = slice(r0, r0 + slab)
            xf = x_ref[rs, :]
            r = lax.rsqrt(jnp.mean(xf * xf, axis=-1, keepdims=True) + RMS_EPS)
            hh = (xf * r * gg).astype(BF)
            h_ref[rs, :] = hh
            for c0 in range(0, N, tn):
                o_ref[rs, c0:c0 + tn] = _dot(hh, w_ref[:, c0:c0 + tn]).astype(BF)

    res = _call(body, name=name, grid=(S // tm,),
                in_specs=[pl.BlockSpec((tm, K), lambda i: (i, 0)), pl.BlockSpec((1, K), lambda i: (0, 0)),
                          pl.BlockSpec((K, N), lambda i: (0, 0))],
                out_specs=(pl.BlockSpec((tm, N), lambda i: (i, 0)), pl.BlockSpec((tm, K), lambda i: (i, 0))),
                out_shape=(jax.ShapeDtypeStruct((S, N), BF), jax.ShapeDtypeStruct((S, K), BF)),
                args=(x, g.reshape(1, K), w), comm=comm)
    return res[0], res[1], list(res[2:])


def _matmul_rms_bwd(dy, w, x, g, dres, *, name, tm=512, comm=()):
    S, K = dy.shape
    D = w.shape[0]
    tm = _pick(S, tm)

    def body(dy_ref, w_ref, x_ref, g_ref, dres_ref, dx_ref, dg_ref):
        @pl.when(pl.program_id(0) == 0)
        def _():
            dg_ref[...] = jnp.zeros_like(dg_ref)

        dh = _dot(dy_ref[...].astype(BF), w_ref[...], _NT)
        xf = x_ref[...]
        r = lax.rsqrt(jnp.mean(xf * xf, axis=-1, keepdims=True) + RMS_EPS)
        xh = xf * r
        dg_ref[...] += jnp.sum(dh * xh, axis=0, keepdims=True)
        dxh = dh * g_ref[...]
        dx_ref[...] = dres_ref[...] + r * (dxh - xh * jnp.mean(dxh * xh, axis=-1, keepdims=True))

    row = pl.BlockSpec((tm, D), lambda i: (i, 0))
    vec = pl.BlockSpec((1, D), lambda i: (0, 0))
    res = _call(body, name=name, grid=(S // tm,),
                in_specs=[pl.BlockSpec((tm, K), lambda i: (i, 0)), pl.BlockSpec((D, K), lambda i: (0, 0)), row, vec, row],
                out_specs=(row, vec),
                out_shape=(jax.ShapeDtypeStruct((S, D), F32), jax.ShapeDtypeStruct((1, D), F32)),
                args=(dy, w, x, g.reshape(1, D), dres), comm=comm)
    return res[0], res[1], list(res[2:])


def _loss_head(x, g, target):
    S, D = x.shape
    tm = _pick(S, 512)

    def body(x_ref, g_ref, t_ref, loss_ref, dx_ref, dg_ref):
        @pl.when(pl.program_id(0) == 0)
        def _():
            dg_ref[...] = jnp.zeros_like(dg_ref)
            loss_ref[...] = jnp.zeros_like(loss_ref)

        xf = x_ref[...]
        gg = g_ref[...]
        r = lax.rsqrt(jnp.mean(xf * xf, axis=-1, keepdims=True) + RMS_EPS)
        xh = xf * r
        err = xh * gg - t_ref[...]
        per_tok = jnp.mean(err * err, axis=-1, keepdims=True)
        loss_ref[...] += 0.5 * jnp.sum(per_tok, axis=0, keepdims=True)
        dy = err * (1.0 / D)
        dg_ref[...] += jnp.sum(dy * xh, axis=0, keepdims=True)
        dxh = dy * gg
        dx_ref[...] = r * (dxh - xh * jnp.mean(dxh * xh, axis=-1, keepdims=True))

    row = pl.BlockSpec((tm, D), lambda i: (i, 0))
    vec = pl.BlockSpec((1, D), lambda i: (0, 0))
    one = pl.BlockSpec((1, 1), lambda i: (0, 0))
    return pl.pallas_call(
        body, name="loss_head",
        out_shape=(jax.ShapeDtypeStruct((1, 1), F32), jax.ShapeDtypeStruct((S, D), F32),
                   jax.ShapeDtypeStruct((1, D), F32)),
        grid=(S // tm,),
        in_specs=[row, vec, row], out_specs=(one, row, vec),
        compiler_params=_params("arbitrary"),
    )(x, g.reshape(1, D), target)


def _slopes(g):
    return [2.0 ** (-8.0 * (HEADS_PER_GROUP * g + j + 1) / DSW_HEADS) for j in range(HEADS_PER_GROUP)]


def _band_masks(W):
    row = lax.broadcasted_iota(jnp.int32, (W, W), 0)
    col = lax.broadcasted_iota(jnp.int32, (W, W), 1)
    d_cur = row - col
    d_prev = d_cur + W
    return d_cur, d_prev, d_cur >= 0, d_cur <= 0


def _band_specs(W, nb, per):
    def cur(c):
        return pl.BlockSpec((None, per * W, OUT_A), lambda r, n: (r, n, c))

    def prev(c):
        return pl.BlockSpec((None, W, OUT_A), lambda r, n: (r, jnp.maximum(per * n - 1, 0), c))

    def nxt(c):
        return pl.BlockSpec((None, W, OUT_A), lambda r, n: (r, jnp.minimum(per * (n + 1), nb - 1), c))

    return cur, prev, nxt


def _blocks_per_step(nb):
    return 4 if nb % 4 == 0 else 2 if nb % 2 == 0 else 1


def _head_stack(W):
    H, hd = HEADS_PER_GROUP, HEAD_DIM
    lane_head = lax.broadcasted_iota(jnp.int32, (W, OUT_A), 1) // hd

    def stack(x):
        return jnp.concatenate([jnp.where(lane_head == h, x, jnp.zeros_like(x)) for h in range(H)], axis=0)

    def unstack(y):
        out = jnp.where(lane_head == 0, y[0:W], 0.0)
        for h in range(1, H):
            out = jnp.where(lane_head == h, y[h * W:(h + 1) * W], out)
        return out

    def column(ref, rows, off=0):
        return jnp.concatenate([ref[rows, h * hd + off:h * hd + off + 1] for h in range(H)], axis=0)

    return stack, unstack, column


def _stacked_bias(W, slopes, dil):
    d_cur, d_prev, m_cur, m_prev = _band_masks(W)
    b_cur = jnp.concatenate([(s * dil) * d_cur.astype(F32) for s in slopes], axis=0)
    b_prev = jnp.concatenate([(s * dil) * d_prev.astype(F32) for s in slopes], axis=0)
    H = len(slopes)
    return b_cur, b_prev, jnp.concatenate([m_cur] * H, axis=0), jnp.concatenate([m_prev] * H, axis=0)


def _attn_a_fwd(qkv, cols, g, comm=()):
    win, dil = DSW_GROUPS[g]
    W = win // dil
    d, L, _ = qkv[0].shape
    nb = L // W
    per = _blocks_per_step(nb)
    slopes = _slopes(g)

    def body(q_ref, kp_ref, kc_ref, vp_ref, vc_ref, o_ref, l_ref):
        n = pl.program_id(1)
        stack, unstack, _ = _head_stack(W)
        b_cur, b_prev, m_cur, m_prev = _stacked_bias(W, slopes, dil)
        m_first = jnp.logical_and(m_prev, n > 0)
        for b in range(per):
            rows = slice(b * W, (b + 1) * W)
            before = slice((b - 1) * W, b * W)
            qs = stack(q_ref[rows, :])
            kc, vc = kc_ref[rows, :], vc_ref[rows, :]
            kp, vp = (kp_ref[...], vp_ref[...]) if b == 0 else (kc_ref[before, :], vc_ref[before, :])
            s_c = jnp.where(m_cur, _dot(qs, kc, _NT) * ATT_SCALE - b_cur, NEG)
            s_p = jnp.where(m_first if b == 0 else m_prev, _dot(qs, kp, _NT) * ATT_SCALE - b_prev, NEG)
            m = jnp.maximum(jnp.max(s_c, axis=1, keepdims=True), jnp.max(s_p, axis=1, keepdims=True))
            p_c = jnp.exp(s_c - m)
            p_p = jnp.exp(s_p - m)
            den = jnp.sum(p_c, axis=1, keepdims=True) + jnp.sum(p_p, axis=1, keepdims=True)
            pv = _dot(p_c.astype(BF), vc) + _dot(p_p.astype(BF), vp)
            o_ref[rows, :] = unstack(pv / den)
            l_ref[rows, :] = unstack(jnp.broadcast_to(m + jnp.log(den), pv.shape))

    cur, prev, _ = _band_specs(W, nb, per)
    out = jax.ShapeDtypeStruct((d, L, OUT_A), F32)
    res = _call(body, name=f"attn_a_fwd_g{g}", grid=(d, nb // per),
                in_specs=[cur(cols[0]), prev(cols[1]), cur(cols[1]), prev(cols[2]), cur(cols[2])],
                out_specs=(cur(0), cur(0)), out_shape=(out, out),
                args=(qkv[0], qkv[1], qkv[1], qkv[2], qkv[2]), sem=("parallel", "parallel"), comm=comm)
    return res[0], res[1], list(res[2:])


def _attn_a_bwd(qkv, cols, do, stats, g):
    win, dil = DSW_GROUPS[g]
    W = win // dil
    d, L, _ = qkv[0].shape
    nb = L // W
    per = _blocks_per_step(nb)
    nsteps = nb // per
    slopes = _slopes(g)

    def body(q_ref, qn_ref, kp_ref, kc_ref, vp_ref, vc_ref, do_ref, don_ref, st_ref, stn_ref,
             dq_ref, dk_ref, dv_ref):
        n = pl.program_id(1)
        stack, unstack, column = _head_stack(W)
        b_cur, b_prev, m_cur, m_prev = _stacked_bias(W, slopes, dil)
        m_first = jnp.logical_and(m_prev, n > 0)
        m_last = jnp.logical_and(m_prev, n < nsteps - 1)
        everything = slice(None)
        for b in range(per):
            rows = slice(b * W, (b + 1) * W)
            before = slice((b - 1) * W, b * W)
            after = slice((b + 1) * W, (b + 2) * W)
            first, last = b == 0, b == per - 1
            qs = stack(q_ref[rows, :])
            qn = stack(qn_ref[...] if last else q_ref[after, :])
            dos = stack(do_ref[rows, :])
            don = stack(don_ref[...] if last else do_ref[after, :])
            kc, vc = kc_ref[rows, :], vc_ref[rows, :]
            kp, vp = (kp_ref[...], vp_ref[...]) if first else (kc_ref[before, :], vc_ref[before, :])
            lse_c, dsum_c = column(st_ref, rows), column(st_ref, rows, STAT_OFF)
            lse_n = column(stn_ref, everything) if last else column(st_ref, after)
            dsum_n = column(stn_ref, everything, STAT_OFF) if last else column(st_ref, after, STAT_OFF)
            m_p = m_first if first else m_prev
            m_n = m_last if last else m_prev
            p_cc = jnp.exp(jnp.where(m_cur, _dot(qs, kc, _NT) * ATT_SCALE - b_cur, NEG) - lse_c)
            p_cp = jnp.exp(jnp.where(m_p, _dot(qs, kp, _NT) * ATT_SCALE - b_prev, NEG) - lse_c)
            p_nc = jnp.exp(jnp.where(m_n, _dot(qn, kc, _NT) * ATT_SCALE - b_prev, NEG) - lse_n)
            ds_cc = (p_cc * (_dot(dos, vc, _NT) - dsum_c) * ATT_SCALE).astype(BF)
            ds_cp = (p_cp * (_dot(dos, vp, _NT) - dsum_c) * ATT_SCALE).astype(BF)
            ds_nc = (p_nc * (_dot(don, vc, _NT) - dsum_n) * ATT_SCALE).astype(BF)
            dq_ref[rows, :] = unstack(_dot(ds_cc, kc) + _dot(ds_cp, kp)).astype(BF)
            dk_ref[rows, :] = (_dot(ds_cc, qs, _TN) + _dot(ds_nc, qn, _TN)).astype(BF)
            dv_ref[rows, :] = (_dot(p_cc.astype(BF), dos, _TN) + _dot(p_nc.astype(BF), don, _TN)).astype(BF)

    cur, prev, nxt = _band_specs(W, nb, per)
    out = jax.ShapeDtypeStruct((d, L, OUT_A), BF)
    cq, ck, cv = cols
    return pl.pallas_call(
        body, name=f"attn_a_bwd_g{g}",
        out_shape=(out, out, out),
        grid=(d, nsteps),
        in_specs=[cur(cq), nxt(cq), prev(ck), cur(ck), prev(cv), cur(cv), cur(0), nxt(0), cur(0), nxt(0)],
        out_specs=(cur(0), cur(0), cur(0)),
        compiler_params=_params("parallel", "parallel"),
    )(qkv[0], qkv[0], qkv[1], qkv[1], qkv[2], qkv[2], do, do, stats, stats)


SB_PAIR = 2
SB_QUAD = 4
SB_QROWS = SB_BLOCK


def _softplus_parts(z):
    e = jnp.exp(-jnp.abs(z))
    log1p_e = jnp.where(e < 1e-4, e, jnp.log(1.0 + e))
    return e, jnp.maximum(z, 0.0) + log1p_e


def _split_dot(x, t):
    hi = x.astype(BF)
    lo = (x - hi.astype(F32)).astype(BF)
    return _dot(hi, t) + _dot(lo, t)


def _sb_block(qh, kk, causal, r_run, tri_incl):
    z = _dot(qh, kk, _NT)
    e, sp = _softplus_parts(z)
    ls = jnp.where(causal, -sp, 0.0)
    cin = _split_dot(ls, tri_incl)
    a = jnp.where(causal, jnp.exp(z + cin + r_run), 0.0)
    return z, e, cin, a


def _sb_specs(S, pair=SB_PAIR):
    Q, hd = SB_QROWS, HEAD_DIM
    lanes = pair * hd
    qc = (3 * W_A) // lanes
    kc = (3 * W_A + W_B) // lanes
    vc = (3 * W_A + 2 * W_B) // lanes
    q_spec = pl.BlockSpec((Q, lanes), lambda p, i: (i, qc + p))
    k_spec = pl.BlockSpec((S, lanes), lambda p, i: (0, kc + p))
    v_spec = pl.BlockSpec((S, lanes), lambda p, i: (0, vc + p))
    o_spec = pl.BlockSpec((Q, lanes), lambda p, i: (i, p))
    full = pl.BlockSpec((S, lanes), lambda p, i: (0, p))
    return q_spec, k_spec, v_spec, o_spec, full


def _sb_stack(pair=SB_PAIR):
    Q, hd = SB_QROWS, HEAD_DIM
    lane_head = lax.broadcasted_iota(jnp.int32, (Q, pair * hd), 1) // hd

    def stack(x):
        return jnp.concatenate([jnp.where(lane_head == h, x, jnp.zeros_like(x)) for h in range(pair)], axis=0)

    def unstack(y):
        out = jnp.where(lane_head == 0, y[0:Q], 0.0)
        for h in range(1, pair):
            out = jnp.where(lane_head == h, y[h * Q:(h + 1) * Q], out)
        return out

    return stack, unstack


def _sb_iotas(i, pair=SB_PAIR):
    B, Q = SB_BLOCK, SB_QROWS
    row = lax.broadcasted_iota(jnp.int32, (Q, B), 0) + i * Q
    col = lax.broadcasted_iota(jnp.int32, (Q, B), 1)
    ahead = jnp.concatenate([col - row] * pair, axis=0)
    tr = lax.broadcasted_iota(jnp.int32, (B, B), 0)
    tc = lax.broadcasted_iota(jnp.int32, (B, B), 1)
    return ahead, tr, tc


def _sb_fwd(proj, comm=()):
    S = proj.shape[0]
    B, Q, hd = SB_BLOCK, SB_QROWS, HEAD_DIM
    nq = S // Q
    P = SB_QUAD
    R = P * Q
    q_spec, k_spec, v_spec, o_spec, _ = _sb_specs(S, P)

    def body(q_ref, k_ref, v_ref, o_ref):
        i = pl.program_id(1)
        ahead, tr, tc = _sb_iotas(i, P)
        tri_incl = (tr >= tc).astype(BF)
        stack, unstack = _sb_stack(P)
        qs = stack(q_ref[...] * ATT_SCALE)

        def cond(c):
            return jnp.logical_and(c[0] >= 0, c[-1] > SB_EXIT)

        def step(c):
            kb, r_run, acc, _ = c
            off = pl.multiple_of(kb * B, B)
            causal = ahead < -kb * B
            _, _, cin, a = _sb_block(qs, k_ref[pl.ds(off, B), :], causal, r_run, tri_incl)
            acc = acc + _dot(a.astype(BF), v_ref[pl.ds(off, B), :])
            r_run = r_run + cin[:, 0:1]
            return kb - 1, r_run, acc, jnp.max(r_run)

        init = (i, jnp.zeros((R, 1), F32), jnp.zeros((R, P * hd), F32), jnp.float32(0.0))
        fin = lax.while_loop(cond, step, init)
        o_ref[...] = unstack(fin[2])

    res = _call(body, name="sb_fwd", grid=(SB_HEADS // P, nq), in_specs=[q_spec, k_spec, v_spec],
                out_specs=o_spec, out_shape=jax.ShapeDtypeStruct((S, W_B), F32), args=(proj, proj, proj),
                sem=("parallel", "parallel"), comm=comm)
    return res[0], list(res[1:])


def _sb_bwd(proj, do, o, comm=()):
    S = proj.shape[0]
    B, Q, hd = SB_BLOCK, SB_QROWS, HEAD_DIM
    nq = S // Q
    R = SB_PAIR * Q
    q_spec, k_spec, v_spec, o_spec, full = _sb_specs(S)

    def body(q_ref, k_ref, v_ref, do_ref, o_ref, dq_ref, dk_ref, dv_ref):
        i = pl.program_id(1)

        @pl.when(i == 0)
        def _():
            dk_ref[...] = jnp.zeros_like(dk_ref)
            dv_ref[...] = jnp.zeros_like(dv_ref)

        ahead, tr, tc = _sb_iotas(i)
        tri_incl = (tr >= tc).astype(BF)
        tri_strict = (tr > tc).astype(BF)
        stack, unstack = _sb_stack()
        qs = stack(q_ref[...] * ATT_SCALE)
        dobs = stack(do_ref[...])
        o_all = o_ref[...]
        dsum = jnp.sum(dobs.astype(F32) * jnp.concatenate([o_all] * SB_PAIR, axis=0), axis=1, keepdims=True)

        def cond(c):
            return jnp.logical_and(c[0] >= 0, c[-1] > SB_EXIT)

        def step(c):
            kb, r_run, g_run, dq, _ = c
            off = pl.multiple_of(kb * B, B)
            causal = ahead < -kb * B
            kk = k_ref[pl.ds(off, B), :]
            vv = v_ref[pl.ds(off, B), :]
            z, e, cin, a = _sb_block(qs, kk, causal, r_run, tri_incl)
            a16 = a.astype(BF)
            gmat = a16.astype(F32) * _dot(dobs, vv, _NT)
            later = _split_dot(gmat, tri_strict)
            pfx = dsum - g_run - later
            sig = jnp.where(z >= 0, 1.0, e) / (1.0 + e)
            dz = jnp.where(causal, gmat - sig * pfx, 0.0).astype(BF)
            dq = dq + _dot(dz, kk)
            dk_ref[pl.ds(off, B), :] += _dot(dz, qs, _TN)
            dv_ref[pl.ds(off, B), :] += _dot(a16, dobs, _TN)
            g_run = g_run + jnp.sum(gmat, axis=1, keepdims=True)
            r_run = r_run + cin[:, 0:1]
            return kb - 1, r_run, g_run, dq, jnp.max(r_run)

        init = (i, jnp.zeros((R, 1), F32), jnp.zeros((R, 1), F32), jnp.zeros((R, SB_PAIR * hd), F32),
                jnp.float32(0.0))
        fin = lax.while_loop(cond, step, init)
        dq_ref[...] = unstack(fin[3]) * ATT_SCALE

    out = jax.ShapeDtypeStruct((S, W_B), F32)
    res = _call(body, name="sb_bwd", grid=(SB_HEADS // SB_PAIR, nq), in_specs=[q_spec, k_spec, v_spec, o_spec, o_spec],
                out_specs=(o_spec, full, full), out_shape=(out, out, out), args=(proj, proj, proj, do, o), comm=comm)
    return res[0], res[1], res[2], list(res[3:])


def _merge_wo(o_g, l_g, o_b, proj, b_gate, w_br, w_o, x, name, comm=()):
    S = o_b.shape[0]
    D = D_MODEL
    tm = _pick(S, 256)
    gcol = GATE_OFF // D

    def body(o0, o1, o2, l0, l1, l2, ob_ref, ga_ref, gb_ref, bg_ref, w_ref, wo_ref, x_ref,
             x1_ref, mg_ref, oa_ref, lse_ref):
        la, lb, lc = l0[...], l1[...], l2[...]
        mx = jnp.maximum(jnp.maximum(la, lb), lc)
        ea, eb, ec = jnp.exp(la - mx), jnp.exp(lb - mx), jnp.exp(lc - mx)
        den = ea + eb + ec
        oa = (ea * o0[...] + eb * o1[...] + ec * o2[...]) / den
        oa_ref[...] = oa
        lse_ref[...] = mx + jnp.log(den)
        oa16 = oa.astype(BF)
        ob16 = ob_ref[...].astype(BF)
        acc = x_ref[...]
        for c0 in range(0, D, FFN_CHUNK):
            cs = slice(c0, c0 + FFN_CHUNK)
            ya = _dot(oa16, w_ref[0:OUT_A, cs])
            yb = _dot(ob16, w_ref[OUT_A:OUT_A + W_B, cs])
            g_a = jax.nn.sigmoid(ga_ref[:, cs].astype(F32) + bg_ref[:, cs])
            g_b = jax.nn.sigmoid(gb_ref[:, cs].astype(F32) + bg_ref[:, D + c0:D + c0 + FFN_CHUNK])
            mg = (g_a * ya + g_b * yb).astype(BF)
            mg_ref[:, cs] = mg
            acc = acc + _dot(mg, wo_ref[cs, :])
        x1_ref[...] = acc

    nar = pl.BlockSpec((tm, OUT_A), lambda i: (i, 0))
    wide = pl.BlockSpec((tm, D), lambda i: (i, 0))
    res = _call(body, name=name, grid=(S // tm,),
                in_specs=[nar] * 7 + [pl.BlockSpec((tm, D), lambda i: (i, gcol)),
                                      pl.BlockSpec((tm, D), lambda i: (i, gcol + 1)),
                                      pl.BlockSpec((1, 2 * D), lambda i: (0, 0)),
                                      pl.BlockSpec((OUT_A + W_B, D), lambda i: (0, 0)),
                                      pl.BlockSpec((D, D), lambda i: (0, 0)), wide],
                out_specs=(wide, wide, nar, nar),
                out_shape=(jax.ShapeDtypeStruct((S, D), F32), jax.ShapeDtypeStruct((S, D), BF),
                           jax.ShapeDtypeStruct((S, OUT_A), F32), jax.ShapeDtypeStruct((S, OUT_A), F32)),
                args=(*o_g, *l_g, o_b, proj, proj, b_gate.reshape(1, 2 * D), w_br, w_o, x), comm=comm)
    return res[0], res[1], res[2], res[3], list(res[4:])


STAT_OFF = HEAD_DIM // 2


def _merge_bwd(dx, w_o, oa, ob, lse, proj, b_gate, w_br):
    S = ob.shape[0]
    D = D_MODEL
    tm = _pick(S, 256)
    gcol = GATE_OFF // D

    def body(dx_ref, wo_ref, oa_ref, ob_ref, l_ref, ga_ref, gb_ref, bg_ref, w_ref,
             dya_ref, dyb_ref, doa_ref, dob_ref, st_ref, dg_ref, dbg_ref):
        @pl.when(pl.program_id(0) == 0)
        def _():
            dbg_ref[...] = jnp.zeros_like(dbg_ref)

        dx16 = dx_ref[...].astype(BF)
        oa = oa_ref[...]
        oa16 = oa.astype(BF)
        ob16 = ob_ref[...].astype(BF)
        doa = jnp.zeros((tm, OUT_A), F32)
        dob = jnp.zeros((tm, W_B), F32)
        for c0 in range(0, D, FFN_CHUNK):
            cs = slice(c0, c0 + FFN_CHUNK)
            cs2 = slice(D + c0, D + c0 + FFN_CHUNK)
            wa = w_ref[0:OUT_A, cs]
            wb = w_ref[OUT_A:OUT_A + W_B, cs]
            dm = _dot(dx16, wo_ref[cs, :], _NT)
            ya = _dot(oa16, wa)
            yb = _dot(ob16, wb)
            g_a = jax.nn.sigmoid(ga_ref[:, cs].astype(F32) + bg_ref[:, cs])
            g_b = jax.nn.sigmoid(gb_ref[:, cs].astype(F32) + bg_ref[:, cs2])
            dga = dm * ya * g_a * (1.0 - g_a)
            dgb = dm * yb * g_b * (1.0 - g_b)
            dg_ref[:, cs] = dga.astype(BF)
            dg_ref[:, cs2] = dgb.astype(BF)
            dbg_ref[:, cs] += jnp.sum(dga, axis=0, keepdims=True)
            dbg_ref[:, cs2] += jnp.sum(dgb, axis=0, keepdims=True)
            dya = (dm * g_a).astype(BF)
            dyb = (dm * g_b).astype(BF)
            dya_ref[:, cs] = dya
            dyb_ref[:, cs] = dyb
            doa = doa + _dot(dya, wa, _NT)
            dob = dob + _dot(dyb, wb, _NT)
        doa = doa.astype(BF)
        doa_ref[...] = doa
        dob_ref[...] = dob.astype(BF)
        r = lax.broadcasted_iota(jnp.int32, (OUT_A, OUT_A), 0) // HEAD_DIM
        c = lax.broadcasted_iota(jnp.int32, (OUT_A, OUT_A), 1) // HEAD_DIM
        dsum = _split_dot(doa.astype(F32) * oa, (r == c).astype(BF))
        lane = lax.broadcasted_iota(jnp.int32, dsum.shape, 1) % HEAD_DIM
        st_ref[...] = jnp.where(lane < STAT_OFF, l_ref[...], dsum)

    nar = pl.BlockSpec((tm, OUT_A), lambda i: (i, 0))
    wide = pl.BlockSpec((tm, D), lambda i: (i, 0))
    wide2 = pl.BlockSpec((tm, 2 * D), lambda i: (i, 0))
    vec2 = pl.BlockSpec((1, 2 * D), lambda i: (0, 0))
    return pl.pallas_call(
        body, name="merge_bwd",
        out_shape=(jax.ShapeDtypeStruct((S, D), BF), jax.ShapeDtypeStruct((S, D), BF),
                   jax.ShapeDtypeStruct((S, OUT_A), BF), jax.ShapeDtypeStruct((S, W_B), BF),
                   jax.ShapeDtypeStruct((S, OUT_A), F32), jax.ShapeDtypeStruct((S, 2 * D), BF),
                   jax.ShapeDtypeStruct((1, 2 * D), F32)),
        grid=(S // tm,),
        in_specs=[wide, pl.BlockSpec((D, D), lambda i: (0, 0)), nar, nar, nar,
                  pl.BlockSpec((tm, D), lambda i: (i, gcol)), pl.BlockSpec((tm, D), lambda i: (i, gcol + 1)), vec2,
                  pl.BlockSpec((OUT_A + W_B, D), lambda i: (0, 0))],
        out_specs=(wide, wide, nar, nar, nar, wide2, vec2),
        compiler_params=_params("arbitrary"),
    )(dx, w_o, oa, ob, lse, proj, proj, b_gate.reshape(1, 2 * D), w_br)


_SQRT_HALF = 0.7071067811865476
_INV_SQRT_2PI = 0.3989422804014327


def _gelu_parts(a):
    cdf = 0.5 * (1.0 + lax.erf(a * _SQRT_HALF))
    pdf = _INV_SQRT_2PI * jnp.exp(-0.5 * a * a)
    return cdf, pdf


def _shift_down(a, halo, k):
    rows = lax.broadcasted_iota(jnp.int32, a.shape, 0)
    out = pltpu.roll(a, k, 0)
    for r in range(k):
        out = jnp.where(rows == r, halo[8 - k + r:8 - k + r + 1, :], out)
    return out


def _ffn_specs(S, tm):
    F = D_FF
    t8 = tm // 8
    a_spec = pl.BlockSpec((tm, F), lambda i: (i, 0))
    v_spec = pl.BlockSpec((tm, F), lambda i: (i, 1))
    halo_prev = pl.BlockSpec((8, F), lambda i: (jnp.maximum(i * t8 - 1, 0), 0))
    return a_spec, v_spec, halo_prev


def _ffn_down(up, conv_w, conv_b, w_down, res, *, name, comm=()):
    S = up.shape[0]
    F = D_FF
    D = w_down.shape[1]
    tm = _pick(S, 256)
    a_spec, v_spec, halo_prev = _ffn_specs(S, tm)

    def body(a_ref, h_ref, v_ref, w_ref, b_ref, wd_ref, r_ref, o_ref, act_ref, ac_ref):
        first = pl.program_id(0) == 0
        acc = r_ref[...]
        for c0 in range(0, F, FFN_CHUNK):
            cs = slice(c0, c0 + FFN_CHUNK)
            a = a_ref[:, cs].astype(F32)
            halo = jnp.where(first, 0.0, h_ref[:, cs].astype(F32))
            w = w_ref[:, cs]
            ac = b_ref[:, cs] + w[0:1, :] * _shift_down(a, halo, 2) + w[1:2, :] * _shift_down(a, halo, 1) + w[2:3, :] * a
            ac_ref[:, cs] = ac.astype(BF)
            cdf, _ = _gelu_parts(ac)
            act = (ac * cdf * v_ref[:, cs].astype(F32)).astype(BF)
            act_ref[:, cs] = act
            acc = acc + _dot(act, wd_ref[cs, :])
        o_ref[...] = acc

    row = pl.BlockSpec((tm, D), lambda i: (i, 0))
    res_ = _call(body, name=name, grid=(S // tm,),
                 in_specs=[a_spec, halo_prev, v_spec, pl.BlockSpec((3, F), lambda i: (0, 0)),
                           pl.BlockSpec((1, F), lambda i: (0, 0)), pl.BlockSpec((F, D), lambda i: (0, 0)), row],
                 out_specs=(row, a_spec, a_spec),
                 out_shape=(jax.ShapeDtypeStruct((S, D), F32), jax.ShapeDtypeStruct((S, F), BF),
                            jax.ShapeDtypeStruct((S, F), BF)),
                 args=(up, up, up, conv_w, conv_b.reshape(1, F), w_down, res), comm=comm)
    return res_[0], res_[1], res_[2], list(res_[3:])


def _shift_up_pair(a, nxt):
    n = a.shape[0]
    r8 = lax.broadcasted_iota(jnp.int32, (8,) + a.shape[1:], 0)
    out = []
    for k in (1, 2):
        rolled = pltpu.roll(a, n - k, 0)
        tail = jnp.where(r8 >= 8 - k, pltpu.roll(nxt, 8 - k, 0), rolled[n - 8:n])
        out.append(jnp.concatenate([rolled[0:n - 8], tail], axis=0))
    return out


def _ffn_bwd(dx, w_down, up, ac, conv_w, comm=()):
    S = up.shape[0]
    F = D_FF
    D = dx.shape[1]
    tm = _pick(S, 256)
    t8 = tm // 8
    nt = S // tm
    a_spec, v_spec, _ = _ffn_specs(S, tm)

    def nxt(width, col):
        return pl.BlockSpec((8, width), lambda i: (jnp.minimum((i + 1) * t8, S // 8 - 1), col))

    def body(dx_ref, dxn_ref, wd_ref, ac_ref, acn_ref, a_ref, v_ref, vn_ref, w_ref, dup_ref, dw_ref, db_ref):
        i = pl.program_id(0)

        @pl.when(i == 0)
        def _():
            dw_ref[...] = jnp.zeros_like(dw_ref)
            db_ref[...] = jnp.zeros_like(db_ref)

        dx16 = dx_ref[...].astype(BF)
        dxn16 = dxn_ref[...].astype(BF)
        last = i == nt - 1

        def dconv(dact, ac, v):
            cdf, pdf = _gelu_parts(ac)
            return cdf, dact * v * (cdf + ac * pdf)

        for c0 in range(0, F, FFN_CHUNK):
            cs = slice(c0, c0 + FFN_CHUNK)
            wd = wd_ref[cs, :]
            dact = _dot(dx16, wd, _NT)
            ac = ac_ref[:, cs].astype(F32)
            cdf, dac = dconv(dact, ac, v_ref[:, cs].astype(F32))
            dup_ref[:, F + c0:F + c0 + FFN_CHUNK] = (dact * ac * cdf).astype(BF)
            _, dac_n = dconv(_dot(dxn16, wd, _NT), acn_ref[:, cs].astype(F32), vn_ref[:, cs].astype(F32))
            d1, d2 = _shift_up_pair(dac, jnp.where(last, 0.0, dac_n))
            w = w_ref[:, cs]
            dup_ref[:, cs] = (w[2:3, :] * dac + w[1:2, :] * d1 + w[0:1, :] * d2).astype(BF)
            a = a_ref[:, cs].astype(F32)
            db_ref[:, cs] += jnp.sum(dac, axis=0, keepdims=True)
            dw_ref[0:1, cs] += jnp.sum(d2 * a, axis=0, keepdims=True)
            dw_ref[1:2, cs] += jnp.sum(d1 * a, axis=0, keepdims=True)
            dw_ref[2:3, cs] += jnp.sum(dac * a, axis=0, keepdims=True)

    w_spec = pl.BlockSpec((3, F), lambda i: (0, 0))
    b_spec = pl.BlockSpec((1, F), lambda i: (0, 0))
    res = _call(body, name="ffn_bwd", grid=(nt,),
                in_specs=[pl.BlockSpec((tm, D), lambda i: (i, 0)), nxt(D, 0), pl.BlockSpec((F, D), lambda i: (0, 0)),
                          a_spec, nxt(F, 0), a_spec, v_spec, nxt(F, 1), w_spec],
                out_specs=(pl.BlockSpec((tm, 2 * F), lambda i: (i, 0)), w_spec, b_spec),
                out_shape=(jax.ShapeDtypeStruct((S, 2 * F), BF), jax.ShapeDtypeStruct((3, F), F32),
                           jax.ShapeDtypeStruct((1, F), F32)),
                args=(dx, dx, w_down, ac, ac, up, up, up, conv_w), comm=comm)
    return res[0], res[1], res[2], list(res[3:])


def _adamw(parts, w, m, v, name, row0=0, prev=None):
    R, C = w.shape
    Rp = parts.shape[1]
    tr = Rp
    for cand in (512, 256, 128, 64, 32, 16):
        if Rp % cand == 0 and row0 % cand == 0 and cand * C * 4 <= (1 << 21):
            tr = cand
            break
    b0 = row0 // tr
    c1 = 1.0 / (1.0 - ADAM_B1 ** ADAM_STEP)
    c2 = 1.0 / (1.0 - ADAM_B2 ** ADAM_STEP)

    def body(p_ref, w_ref, m_ref, v_ref, *rest):
        g_ref, d_ref, nm_ref, nv_ref = rest[-4:]
        g = p_ref[0].astype(F32)
        for j in range(1, N_DEV):
            g = g + p_ref[j].astype(F32)
        nm = ADAM_B1 * m_ref[...] + (1.0 - ADAM_B1) * g
        nv = ADAM_B2 * v_ref[...] + (1.0 - ADAM_B2) * (g * g)
        g_ref[...] = g
        nm_ref[...] = nm
        nv_ref[...] = nv
        d_ref[...] = -ADAM_LR * ((nm * c1) / (jnp.sqrt(nv * c2) + ADAM_EPS) + ADAM_WD * w_ref[...])

    blk = pl.BlockSpec((tr, C), lambda i: (b0 + i, 0))
    out = jax.ShapeDtypeStruct((R, C), F32)
    carried = [] if prev is None else list(prev)
    return pl.pallas_call(
        body, name=name,
        out_shape=(out, out, out, out),
        grid=(Rp // tr,),
        in_specs=[pl.BlockSpec((N_DEV, tr, C), lambda i: (0, i, 0)), blk, blk, blk]
        + [pl.BlockSpec(memory_space=pl.ANY)] * len(carried),
        out_specs=(blk, blk, blk, blk),
        input_output_aliases={4 + k: k for k in range(len(carried))},
        compiler_params=_params("parallel"),
    )(parts, w, m, v, *carried)


def _dil(t, dil):
    S, C = t.shape
    if dil == 1:
        return t.reshape(1, S, C)
    return t.reshape(S // dil, dil, C).transpose(1, 0, 2)


def _undil(t):
    d, L, C = t.shape
    if d == 1:
        return t.reshape(L, C)
    return t.transpose(1, 0, 2).reshape(L * d, C)


def _group_qkv(proj, g):
    dil = DSW_GROUPS[g][1]
    if dil == 1:
        p3 = _dil(proj, 1)
        return (p3, p3, p3), (g, W_A // OUT_A + g, 2 * W_A // OUT_A + g)
    c0 = g * OUT_A
    return tuple(_dil(proj[:, o + c0:o + c0 + OUT_A], dil) for o in (0, W_A, 2 * W_A)), (0, 0, 0)


_COL_SHARDED = ("w_in", "w_br", "w_up")
_ROW_SHARDED = ("w_o", "w_down")


class _Plan:
    def __init__(self):
        self.riders = {}
        self.landed = {}

    def ride(self, slot, key, kind, x):
        self.riders.setdefault(slot, []).append((key, kind, x))

    def run(self, slot, fn, *args, **kw):
        items = self.riders.pop(slot, [])
        res = fn(*args, comm=[(kind, x) for _, kind, x in items], **kw)
        for (key, _, _), r in zip(items, res[-1]):
            self.landed[key] = r
        return res[0] if len(res) == 2 else res[:-1]

    def weight(self, n, l):
        g = self.landed[(n, l)]
        if n in _COL_SHARDED:
            return g.transpose(1, 0, 2).reshape(g.shape[1], -1)
        return g.reshape(-1, g.shape[2])

    def scatter(self, slot, n, l, full, part=0):
        K, N = full.shape
        if n in _COL_SHARDED:
            blocks = full.reshape(K, N_DEV, N // N_DEV).transpose(1, 0, 2)
        else:
            blocks = full.reshape(N_DEV, K // N_DEV, N)
        self.ride(slot, ("d" + n, l, part), "scatter", blocks)


def _layer_fwd(x, p, plan, l):
    proj, h = plan.run(f"proj_{l}", _norm_matmul, x, p["norm1"][l], plan.weight("w_in", l), name=f"proj_{l}",
                       tm=512, tn=1024)
    o_g, l_g, qkv_g = [], [], []
    for g in range(N_GROUPS):
        qkv, cols = _group_qkv(proj, g)
        og, lg = plan.run(f"attn_a_fwd_g{g}_{l}", _attn_a_fwd, qkv, cols, g)
        o_g.append(_undil(og))
        l_g.append(_undil(lg))
        qkv_g.append((qkv, cols))
    ob = plan.run(f"sb_fwd_{l}", _sb_fwd, proj)
    x1, merged, oa, lse = plan.run(f"wo_{l}", _merge_wo, o_g, l_g, ob, proj, p["b_gate"][l], plan.weight("w_br", l),
                                   plan.weight("w_o", l), x, f"wo_{l}")
    up, h2 = plan.run(f"up_{l}", _norm_matmul, x1, p["norm2"][l], plan.weight("w_up", l), name=f"up_{l}",
                      tm=512, tn=1408)
    x2, act, ac = plan.run(f"down_{l}", _ffn_down, up, p["conv_w"][l], p["conv_b"][l], plan.weight("w_down", l), x1,
                           name=f"down_{l}")
    saved = dict(x=x, h=h, proj=proj, qkv_g=qkv_g, oa=oa, ob=ob, lse=lse, merged=merged, x1=x1, h2=h2, up=up, act=act, ac=ac)
    return x2, saved


def _layer_bwd(dx2, sv, p, plan, l):
    gr = {}
    dwd = plan.run(f"dw_down_{l}", _matmul, sv["act"], dx2, mode="tn", out_dtype=BF, name=f"dw_down_{l}",
                   tm=1408, tn=1024, tk=2048)
    plan.scatter(f"d_h2_{l}", "w_down", l, dwd)
    dup, gr["conv_w"], dcb = plan.run(f"ffn_bwd_{l}", _ffn_bwd, dx2, plan.weight("w_down", l), sv["up"], sv["ac"],
                                      p["conv_w"][l])
    gr["conv_b"] = dcb[0]
    dx1, dn2 = plan.run(f"d_h2_{l}", _matmul_rms_bwd, dup, plan.weight("w_up", l), sv["x1"], p["norm2"][l], dx2,
                        name=f"d_h2_{l}")
    dwu = plan.run(f"dw_up_{l}", _matmul, sv["h2"], dup, mode="tn", out_dtype=BF, name=f"dw_up_{l}",
                   tm=1024, tn=1408, tk=2048)
    plan.scatter(f"sb_bwd_{l}", "w_up", l, dwu)
    gr["norm2"] = dn2[0]
    dwo = plan.run(f"dw_o_{l}", _matmul, sv["merged"], dx1, mode="tn", out_dtype=BF, name=f"dw_o_{l}",
                   tm=1024, tn=1024, tk=2048)
    plan.scatter(f"dw_in_{l}", "w_o", l, dwo)
    dya, dyb, doa, dob, stats, dgate, dbg = _merge_bwd(dx1, plan.weight("w_o", l), sv["oa"], sv["ob"], sv["lse"],
                                                       sv["proj"], p["b_gate"][l], plan.weight("w_br", l))
    gr["b_gate"] = dbg[0]
    dwa = plan.run(f"dw_bra_{l}", _matmul, sv["oa"], dya, mode="tn", out_dtype=BF, name=f"dw_bra_{l}",
                   tm=256, tn=1024, tk=2048)
    dwb = plan.run(f"dw_brb_{l}", _matmul, sv["ob"], dyb, mode="tn", out_dtype=BF, name=f"dw_brb_{l}",
                   tm=256, tn=1024, tk=2048)
    plan.scatter(f"dw_in_{l}", "w_br", l, jnp.concatenate([dwa, dwb], axis=0))
    proj = sv["proj"]
    dq_a, dk_a, dv_a = [], [], []
    for g, (_, dil) in enumerate(DSW_GROUPS):
        qkv, cols = sv["qkv_g"][g]
        dqg, dkg, dvg = _attn_a_bwd(qkv, cols, _dil(doa, dil), _dil(stats, dil), g)
        dq_a.append(_undil(dqg))
        dk_a.append(_undil(dkg))
        dv_a.append(_undil(dvg))
    dqb, dkb, dvb = plan.run(f"sb_bwd_{l}", _sb_bwd, proj, dob, sv["ob"])
    dproj = jnp.concatenate(dq_a + dk_a + dv_a + [dqb.astype(BF), dkb.astype(BF), dvb.astype(BF), dgate], axis=1)
    dx, dn1 = plan.run(f"d_h_{l}", _matmul_rms_bwd, dproj, plan.weight("w_in", l), sv["x"], p["norm1"][l], dx1,
                       name=f"d_h_{l}")
    if l > 0:
        dwi = plan.run(f"dw_in_{l}", _matmul, sv["h"], dproj, mode="tn", out_dtype=BF, name=f"dw_in_{l}",
                       tm=1024, tn=1280, tk=2048)
        plan.scatter(f"ffn_bwd_{l - 1}", "w_in", l, dwi)
    else:
        half = D_MODEL // 2
        for part in range(2):
            name = f"dw_in_{l}" if part == 0 else f"dw_in_{l}_rest"
            dwi = plan.run(name, _matmul, sv["h"], dproj, mode="tn", out_dtype=BF, name=name, tm=half, tn=1280,
                           tk=2048, m_cols=half, m_off=part * half)
            plan.scatter(f"dw_in_{l}_rest" if part == 0 else "alone", "w_in", l, dwi, part)
    gr["norm1"] = dn1[0]
    return dx, gr


def kernel(x, norm1, w_in, b_gate, w_br, w_o, norm2, w_up, conv_w, conv_b, w_down, norm_f, loss_target, m_norm1, m_w_in, m_b_gate, m_w_br, m_w_o, m_norm2, m_w_up, m_conv_w, m_conv_b, m_w_down, m_norm_f, v_norm1, v_w_in, v_b_gate, v_w_br, v_w_o, v_norm2, v_w_up, v_conv_w, v_conv_b, v_w_down, v_norm_f):
    depth = norm1.shape[0]
    me = 4 * lax.axis_index("x") + 2 * lax.axis_index("y") + lax.axis_index("c")
    shards = dict(w_in=w_in, w_br=w_br, w_o=w_o, w_up=w_up, w_down=w_down)
    moments_m = dict(norm1=m_norm1, w_in=m_w_in, b_gate=m_b_gate, w_br=m_w_br, w_o=m_w_o, norm2=m_norm2,
                     w_up=m_w_up, conv_w=m_conv_w, conv_b=m_conv_b, w_down=m_w_down, norm_f=m_norm_f)
    moments_v = dict(norm1=v_norm1, w_in=v_w_in, b_gate=v_b_gate, w_br=v_w_br, w_o=v_w_o, norm2=v_norm2,
                     w_up=v_w_up, conv_w=v_conv_w, conv_b=v_conv_b, w_down=v_w_down, norm_f=v_norm_f)

    plan = _Plan()
    wb = {n: s.astype(BF) for n, s in shards.items()}
    p = dict(norm1=norm1, b_gate=b_gate, norm2=norm2, conv_b=conv_b)
    cw = _all_gather(conv_w, "gather_conv_w")
    p["conv_w"] = cw.transpose(1, 2, 0, 3).reshape(depth, 3, D_FF)
    plan.landed[("w_in", 0)] = _all_gather_via_sibling(wb["w_in"][0], "gather_w_in_0")
    for l in range(depth):
        plan.ride(f"proj_{l}", ("w_down", l), "gather", wb["w_down"][l])
        plan.ride(f"attn_a_fwd_g0_{l}" if l == 0 else f"down_{l - 1}", ("w_br", l), "gather", wb["w_br"][l])
        plan.ride(f"attn_a_fwd_g0_{l}" if l == 0 else f"down_{l - 1}", ("w_o", l), "gather", wb["w_o"][l])
        plan.ride(f"sb_fwd_{l}", ("w_up", l), "gather", wb["w_up"][l])
        if l + 1 < depth:
            plan.ride(f"up_{l}", ("w_in", l + 1), "gather", wb["w_in"][l + 1])

    xs = x[0]
    saved = []
    for l in range(depth):
        xs, sv = _layer_fwd(xs, p, plan, l)
        saved.append(sv)
    loss_part, dx, dnf = _loss_head(xs, norm_f, loss_target[0])
    loss = lax.psum(loss_part[0, 0], ("x", "y", "c"))

    grads = [None] * depth
    for l in reversed(range(depth)):
        dx, grads[l] = _layer_bwd(dx, saved[l], p, plan, l)
    grad_x = dx[None]
    (key, _, last), = plan.riders.pop("alone")
    plan.landed[key] = _all_to_all(last, "scatter_w_in_rest")
    assert not plan.riders, sorted(plan.riders)

    out_g, out_d, out_m, out_v = {}, {}, {}, {}
    for n in _COL_SHARDED + _ROW_SHARDED:
        shp = shards[n].shape
        flat = (shp[0] * shp[1], shp[2])
        res = None
        for l in range(depth):
            row = l * shp[1]
            for key in sorted(k for k in plan.landed if k[:2] == ("d" + n, l)):
                parts = plan.landed[key]
                res = _adamw(parts, shards[n].reshape(flat), moments_m[n].reshape(flat), moments_v[n].reshape(flat),
                             f"adamw_{n}_{l}_{key[2]}", row0=row, prev=res)
                row += parts.shape[1]
        out_g[n], out_d[n], out_m[n], out_v[n] = [r.reshape(shp) for r in res]

    small = ("norm1", "b_gate", "norm2", "conv_b")
    vecs = [jnp.stack([grads[l][n] for l in range(depth)]).reshape(-1) for n in small]
    vecs.append(dnf.reshape(-1))
    vecs.append(jnp.stack([grads[l]["conv_w"] for l in range(depth)]).reshape(-1))
    sizes = [v.shape[0] for v in vecs]
    flat = jnp.concatenate(vecs)
    n_small = sum(sizes[:-1])
    pad = (-flat.shape[0]) % 1024
    flat = jnp.pad(flat, (0, pad)).reshape(-1, 128)
    allp = _all_gather(flat, "gather_small_grads").reshape(N_DEV, -1)
    rep_w = jnp.concatenate([norm1.reshape(-1), b_gate.reshape(-1), norm2.reshape(-1), conv_b.reshape(-1), norm_f])
    rep_m = jnp.concatenate([moments_m[n].reshape(-1) for n in small] + [m_norm_f])
    rep_v = jnp.concatenate([moments_v[n].reshape(-1) for n in small] + [v_norm_f])
    rows = n_small // 128
    res = _adamw(allp[:, :n_small].reshape(N_DEV, rows, 128), rep_w.reshape(rows, 128), rep_m.reshape(rows, 128),
                 rep_v.reshape(rows, 128), "adamw_small")
    off = 0
    for n, sz in zip(small + ("norm_f",), sizes[:-1]):
        shp = norm_f.shape if n == "norm_f" else p[n].shape
        out_g[n], out_d[n], out_m[n], out_v[n] = [r.reshape(-1)[off:off + sz].reshape(shp) for r in res]
        off += sz
    f = conv_w.shape[2]
    cwp = allp[:, n_small:n_small + sizes[-1]].reshape(N_DEV, depth * 3, D_FF)
    cwp = lax.dynamic_slice_in_dim(cwp, me * f, f, axis=2)
    res = _adamw(cwp, conv_w.reshape(depth * 3, f), m_conv_w.reshape(depth * 3, f), v_conv_w.reshape(depth * 3, f),
                 "adamw_conv_w")
    out_g["conv_w"], out_d["conv_w"], out_m["conv_w"], out_v["conv_w"] = [r.reshape(conv_w.shape) for r in res]

    order = ("norm1", "w_in", "b_gate", "w_br", "w_o", "norm2", "w_up", "conv_w", "conv_b", "w_down", "norm_f")
    return (loss, grad_x, *[out_g[n] for n in order], *[out_d[n] for n in order],
            *[out_m[n] for n in order], *[out_v[n] for n in order])
```

```python
import functools

import jax
import jax.numpy as jnp
from jax import lax
from jax.experimental import pallas as pl
from jax.experimental.pallas import tpu as pltpu

BF = jnp.bfloat16
F32 = jnp.float32

N_DEV = 8
D_MODEL = 1024
HEAD_DIM = 64
DSW_GROUPS = ((128, 1), (512, 4), (2048, 16))
HEADS_PER_GROUP = 4
N_GROUPS = len(DSW_GROUPS)
DSW_HEADS = HEADS_PER_GROUP * N_GROUPS
SB_HEADS = 4
W_A = DSW_HEADS * HEAD_DIM
W_B = SB_HEADS * HEAD_DIM
OUT_A = HEADS_PER_GROUP * HEAD_DIM
N_IN = 3 * W_A + 3 * W_B + 2 * D_MODEL
GATE_OFF = 3 * W_A + 3 * W_B
D_FF = 2816
SB_BLOCK = 256
RMS_EPS = 1e-6
ATT_SCALE = HEAD_DIM ** -0.5
NEG = -1e30
SB_EXIT = -110.0
FFN_CHUNK = 256

ADAM_LR = 0.001
ADAM_B1 = 0.9
ADAM_B2 = 0.999
ADAM_EPS = 1e-08
ADAM_WD = 0.01
ADAM_STEP = 10

HBM_SPEC = pl.BlockSpec(memory_space=pltpu.HBM)
MESH = pl.DeviceIdType.MESH

_NN = (((1,), (0,)), ((), ()))
_NT = (((1,), (1,)), ((), ()))
_TN = (((0,), (0,)), ((), ()))


def _dot(a, b, dn=_NN):
    return lax.dot_general(a, b, dn, preferred_element_type=F32)


def _pick(dim, pref):
    if dim <= pref:
        return dim
    t = (pref // 128) * 128
    while t >= 128:
        if dim % t == 0:
            return t
        t -= 128
    return dim


def _params(*sem):
    return pltpu.CompilerParams(dimension_semantics=sem)


def _peer(k):
    x, y, c = lax.axis_index("x"), lax.axis_index("y"), lax.axis_index("c")
    px = 1 - x if (k >> 2) & 1 else x
    py = 1 - y if (k >> 1) & 1 else y
    pc = 1 - c if k & 1 else c
    return (px, py, pc), 4 * px + 2 * py + pc


def _exchange(kind, x_ref, out_ref, send_sems, recv_sems, local_sem):
    gather = kind == "gather"
    _, me = _peer(0)

    def src(idx):
        return x_ref if gather else x_ref.at[idx]

    def copy(k, dst_idx):
        peer, pidx = _peer(k)
        return pltpu.make_async_remote_copy(
            src_ref=src(pidx), dst_ref=out_ref.at[dst_idx], send_sem=send_sems.at[k - 1],
            recv_sem=recv_sems.at[k - 1], device_id=peer, device_id_type=MESH)

    mine = pltpu.make_async_copy(src(me), out_ref.at[me], local_sem)

    def start():
        mine.start()
        for k in range(1, N_DEV):
            copy(k, me).start()

    def wait():
        for k in range(1, N_DEV):
            copy(k, _peer(k)[1]).wait_recv()
        for k in range(1, N_DEV):
            copy(k, me).wait_send()
        mine.wait()

    return start, wait


_EXCHANGE_SEMS = [pltpu.SemaphoreType.DMA((N_DEV - 1,)), pltpu.SemaphoreType.DMA((N_DEV - 1,)),
                  pltpu.SemaphoreType.DMA]


def _exchange_shape(kind, x):
    return jax.ShapeDtypeStruct(((N_DEV,) + x.shape) if kind == "gather" else x.shape, x.dtype)


def _exchange_alone(kind, x, name):
    def body(x_ref, out_ref, send_sems, recv_sems, local_sem):
        start, wait = _exchange(kind, x_ref, out_ref, send_sems, recv_sems, local_sem)
        start()
        wait()

    return pl.pallas_call(
        body, name=name, out_shape=_exchange_shape(kind, x),
        in_specs=[HBM_SPEC], out_specs=HBM_SPEC, scratch_shapes=list(_EXCHANGE_SEMS),
    )(x)


def _all_gather_via_sibling(x, name):
    def body(x_ref, out_ref, send_sems, recv_sems, local_sem):
        x_, y_, c_ = lax.axis_index("x"), lax.axis_index("y"), lax.axis_index("c")
        me, sibling = (x_, y_, c_), (x_, y_, 1 - c_)
        chips = [(1 - x_, y_), (x_, 1 - y_), (1 - x_, 1 - y_)]

        def slot(px, py, pc):
            return out_ref.at[4 * px + 2 * py + pc]

        def copy(k, block, to, src=None):
            return pltpu.make_async_remote_copy(
                src_ref=slot(*block) if src is None else src, dst_ref=slot(*block), send_sem=send_sems.at[k],
                recv_sem=recv_sems.at[k], device_id=to, device_id_type=MESH)

        mine = pltpu.make_async_copy(x_ref, slot(*me), local_sem)
        mine.start()
        first = [copy(0, me, sibling, src=x_ref)]
        first += [copy(1 + j, me, (*chip, c_), src=x_ref) for j, chip in enumerate(chips)]
        for cp in first:
            cp.start()
        passed = [copy(4 + j, (*chip, c_), sibling) for j, chip in enumerate(chips)]
        for j, chip in enumerate(chips):
            copy(1 + j, (*chip, c_), me).wait_recv()
            passed[j].start()
        copy(0, sibling, me).wait_recv()
        for j, chip in enumerate(chips):
            copy(4 + j, (*chip, 1 - c_), me).wait_recv()
        for cp in first + passed:
            cp.wait_send()
        mine.wait()

    return pl.pallas_call(
        body, name=name, out_shape=_exchange_shape("gather", x),
        in_specs=[HBM_SPEC], out_specs=HBM_SPEC,
        scratch_shapes=[pltpu.SemaphoreType.DMA((N_DEV - 1,)), pltpu.SemaphoreType.DMA((N_DEV - 1,)),
                        pltpu.SemaphoreType.DMA],
    )(x)


def _all_gather(x, name):
    return _exchange_alone("gather", x, name)


def _all_to_all(x, name):
    return _exchange_alone("scatter", x, name)


def _call(body, *, name, grid, in_specs, out_specs, out_shape, args, scratch_shapes=(), sem=None, comm=()):
    single = not isinstance(out_shape, (tuple, list))
    outs = (out_shape,) if single else tuple(out_shape)
    ospecs = (out_specs,) if single else tuple(out_specs)
    if not comm:
        res = pl.pallas_call(
            body, name=name, out_shape=outs, grid=grid, in_specs=list(in_specs), out_specs=ospecs,
            scratch_shapes=list(scratch_shapes), compiler_params=_params(*(sem or ("arbitrary",) * len(grid))),
        )(*args)
        return res
    n_in, n_out, n_scr, nc = len(in_specs), len(outs), len(scratch_shapes), len(comm)

    def wrapped(*refs):
        ins = refs[:n_in]
        cins = refs[n_in:n_in + nc]
        o0 = n_in + nc
        kouts = refs[o0:o0 + n_out]
        couts = refs[o0 + n_out:o0 + n_out + nc]
        s0 = o0 + n_out + nc
        scr = refs[s0:s0 + n_scr]
        sems = refs[s0 + n_scr:]
        ids = [pl.program_id(ax) for ax in range(len(grid))]
        first = functools.reduce(jnp.logical_and, [i == 0 for i in ids])
        last = functools.reduce(jnp.logical_and, [i == g - 1 for i, g in zip(ids, grid)])
        ex = [_exchange(comm[c][0], cins[c], couts[c], *sems[3 * c:3 * c + 3]) for c in range(nc)]

        @pl.when(first)
        def _():
            for start, _ in ex:
                start()

        body(*ins, *kouts, *scr)

        @pl.when(last)
        def _():
            for _, wait in ex:
                wait()

    return pl.pallas_call(
        wrapped, name=name,
        out_shape=outs + tuple(_exchange_shape(k, x) for k, x in comm),
        grid=grid, in_specs=list(in_specs) + [HBM_SPEC] * nc, out_specs=ospecs + (HBM_SPEC,) * nc,
        scratch_shapes=list(scratch_shapes) + list(_EXCHANGE_SEMS) * nc,
        compiler_params=_params(*(("arbitrary",) * len(grid))),
    )(*args, *[x for _, x in comm])


def _matmul(a, b, *, mode, out_dtype, name, tm=512, tn=1024, tk=1024, res=None, comm=(), m_cols=None, m_off=0):
    if mode == "nn":
        (M, K), (_, N) = a.shape, b.shape
    elif mode == "nt":
        (M, K), (N, _) = a.shape, b.shape
    else:
        (K, M), (_, N) = a.shape, b.shape
        M = M if m_cols is None else m_cols
    tm, tn, tk = _pick(M, tm), _pick(N, tn), _pick(K, tk)
    i0 = m_off // tm
    nk = K // tk
    dn = {"nn": _NN, "nt": _NT, "tn": _TN}[mode]

    def body(*refs):
        a_ref, b_ref = refs[0], refs[1]
        r_ref = refs[2] if res is not None else None
        o_ref = refs[3] if res is not None else refs[2]

        def finish(r):
            if res is not None:
                r = r + r_ref[...].astype(F32)
            o_ref[...] = r.astype(out_dtype)

        part = _dot(a_ref[...].astype(BF), b_ref[...].astype(BF), dn)
        if nk == 1:
            finish(part)
            return
        acc = refs[-1]
        k = pl.program_id(2)

        @pl.when(k == 0)
        def _():
            acc[...] = jnp.zeros_like(acc)

        acc[...] += part

        @pl.when(k == nk - 1)
        def _():
            finish(acc[...])

    if mode == "tn":
        a_spec = pl.BlockSpec((tk, tm), lambda j, i, k: (k, i0 + i))
    else:
        a_spec = pl.BlockSpec((tm, tk), lambda j, i, k: (i, k))
    if mode == "nt":
        b_spec = pl.BlockSpec((tn, tk), lambda j, i, k: (j, k))
    else:
        b_spec = pl.BlockSpec((tk, tn), lambda j, i, k: (k, j))
    o_spec = pl.BlockSpec((tm, tn), lambda j, i, k: (i, j))
    in_specs = [a_spec, b_spec] + ([o_spec] if res is not None else [])
    args = (a, b) + ((res,) if res is not None else ())
    out = _call(body, name=name, grid=(N // tn, M // tm, nk), in_specs=in_specs, out_specs=o_spec,
                out_shape=jax.ShapeDtypeStruct((M, N), out_dtype), args=args,
                scratch_shapes=[pltpu.VMEM((tm, tn), F32)] if nk > 1 else [],
                sem=("parallel", "parallel", "arbitrary"), comm=comm)
    return out[0], list(out[1:])


def _norm_matmul(x, g, w, *, name, tm=512, tn=1024, comm=()):
    S, K = x.shape
    N = w.shape[1]
    tm, tn = _pick(S, tm), _pick(N, tn)
    slab = min(tm, 256)

    def body(x_ref, g_ref, w_ref, o_ref, h_ref):
        gg = g_ref[...]
        for r0 in range(0, tm, slab):
            rs = slice(r0, r0 + slab)
            xf = x_ref[rs, :]
            r = lax.rsqrt(jnp.mean(xf * xf, axis=-1, keepdims=True) + RMS_EPS)
            hh = (xf * r * gg).astype(BF)
            h_ref[rs, :] = hh
            for c0 in range(0, N, tn):
                o_ref[rs, c0:c0 + tn] = _dot(hh, w_ref[:, c0:c0 + tn]).astype(BF)

    res = _call(body, name=name, grid=(S // tm,),
                in_specs=[pl.BlockSpec((tm, K), lambda i: (i, 0)), pl.BlockSpec((1, K), lambda i: (0, 0)),
                          pl.BlockSpec((K, N), lambda i: (0, 0))],
                out_specs=(pl.BlockSpec((tm, N), lambda i: (i, 0)), pl.BlockSpec((tm, K), lambda i: (i, 0))),
                out_shape=(jax.ShapeDtypeStruct((S, N), BF), jax.ShapeDtypeStruct((S, K), BF)),
                args=(x, g.reshape(1, K), w), comm=comm)
    return res[0], res[1], list(res[2:])


def _matmul_rms_bwd(dy, w, x, g, dres, *, name, tm=512, comm=()):
    S, K = dy.shape
    D = w.shape[0]
    tm = _pick(S, tm)
    slab = min(tm, 256)

    def body(dy_ref, w_ref, x_ref, g_ref, dres_ref, dx_ref, dg_ref):
        @pl.when(pl.program_id(0) == 0)
        def _():
            dg_ref[...] = jnp.zeros_like(dg_ref)

        gg = g_ref[...]
        for r0 in range(0, tm, slab):
            rs = slice(r0, r0 + slab)
            dh = _dot(dy_ref[rs, :].astype(BF), w_ref[...], _NT)
            xf = x_ref[rs, :]
            r = lax.rsqrt(jnp.mean(xf * xf, axis=-1, keepdims=True) + RMS_EPS)
            xh = xf * r
            dg_ref[...] += jnp.sum(dh * xh, axis=0, keepdims=True)
            dxh = dh * gg
            dx_ref[rs, :] = dres_ref[rs, :] + r * (dxh - xh * jnp.mean(dxh * xh, axis=-1, keepdims=True))

    row = pl.BlockSpec((tm, D), lambda i: (i, 0))
    vec = pl.BlockSpec((1, D), lambda i: (0, 0))
    res = _call(body, name=name, grid=(S // tm,),
                in_specs=[pl.BlockSpec((tm, K), lambda i: (i, 0)), pl.BlockSpec((D, K), lambda i: (0, 0)), row, vec, row],
                out_specs=(row, vec),
                out_shape=(jax.ShapeDtypeStruct((S, D), F32), jax.ShapeDtypeStruct((1, D), F32)),
                args=(dy, w, x, g.reshape(1, D), dres), comm=comm)
    return res[0], res[1], list(res[2:])


def _loss_head(x, g, target):
    S, D = x.shape
    tm = _pick(S, 512)

    def body(x_ref, g_ref, t_ref, loss_ref, dx_ref, dg_ref):
        @pl.when(pl.program_id(0) == 0)
        def _():
            dg_ref[...] = jnp.zeros_like(dg_ref)
            loss_ref[...] = jnp.zeros_like(loss_ref)

        xf = x_ref[...]
        gg = g_ref[...]
        r = lax.rsqrt(jnp.mean(xf * xf, axis=-1, keepdims=True) + RMS_EPS)
        xh = xf * r
        err = xh * gg - t_ref[...]
        per_tok = jnp.mean(err * err, axis=-1, keepdims=True)
        loss_ref[...] += 0.5 * jnp.sum(per_tok, axis=0, keepdims=True)
        dy = err * (1.0 / D)
        dg_ref[...] += jnp.sum(dy * xh, axis=0, keepdims=True)
        dxh = dy * gg
        dx_ref[...] = r * (dxh - xh * jnp.mean(dxh * xh, axis=-1, keepdims=True))

    row = pl.BlockSpec((tm, D), lambda i: (i, 0))
    vec = pl.BlockSpec((1, D), lambda i: (0, 0))
    one = pl.BlockSpec((1, 1), lambda i: (0, 0))
    return pl.pallas_call(
        body, name="loss_head",
        out_shape=(jax.ShapeDtypeStruct((1, 1), F32), jax.ShapeDtypeStruct((S, D), F32),
                   jax.ShapeDtypeStruct((1, D), F32)),
        grid=(S // tm,),
        in_specs=[row, vec, row], out_specs=(one, row, vec),
        compiler_params=_params("arbitrary"),
    )(x, g.reshape(1, D), target)


def _slopes(g):
    return [2.0 ** (-8.0 * (HEADS_PER_GROUP * g + j + 1) / DSW_HEADS) for j in range(HEADS_PER_GROUP)]


def _band_masks(W):
    row = lax.broadcasted_iota(jnp.int32, (W, W), 0)
    col = lax.broadcasted_iota(jnp.int32, (W, W), 1)
    d_cur = row - col
    d_prev = d_cur + W
    return d_cur, d_prev, d_cur >= 0, d_cur <= 0


def _band_specs(W, nb, per):
    def cur(c):
        return pl.BlockSpec((None, per * W, OUT_A), lambda r, n: (r, n, c))

    def prev(c):
        return pl.BlockSpec((None, W, OUT_A), lambda r, n: (r, jnp.maximum(per * n - 1, 0), c))

    def nxt(c):
        return pl.BlockSpec((None, W, OUT_A), lambda r, n: (r, jnp.minimum(per * (n + 1), nb - 1), c))

    return cur, prev, nxt


def _blocks_per_step(nb):
    return 4 if nb % 4 == 0 else 2 if nb % 2 == 0 else 1


def _head_stack(W):
    H, hd = HEADS_PER_GROUP, HEAD_DIM
    lane_head = lax.broadcasted_iota(jnp.int32, (W, OUT_A), 1) // hd

    def stack(x):
        return jnp.concatenate([jnp.where(lane_head == h, x, jnp.zeros_like(x)) for h in range(H)], axis=0)

    def unstack(y):
        out = jnp.where(lane_head == 0, y[0:W], 0.0)
        for h in range(1, H):
            out = jnp.where(lane_head == h, y[h * W:(h + 1) * W], out)
        return out

    def column(ref, rows, off=0):
        return jnp.concatenate([ref[rows, h * hd + off:h * hd + off + 1] for h in range(H)], axis=0)

    return stack, unstack, column


def _stacked_bias(W, slopes, dil):
    d_cur, d_prev, m_cur, m_prev = _band_masks(W)
    b_cur = jnp.concatenate([(s * dil) * d_cur.astype(F32) for s in slopes], axis=0)
    b_prev = jnp.concatenate([(s * dil) * d_prev.astype(F32) for s in slopes], axis=0)
    H = len(slopes)
    return b_cur, b_prev, jnp.concatenate([m_cur] * H, axis=0), jnp.concatenate([m_prev] * H, axis=0)


def _attn_a_fwd(qkv, cols, g, comm=()):
    win, dil = DSW_GROUPS[g]
    W = win // dil
    d, L, _ = qkv[0].shape
    nb = L // W
    per = _blocks_per_step(nb)
    slopes = _slopes(g)

    def body(q_ref, kp_ref, kc_ref, vp_ref, vc_ref, o_ref, l_ref):
        n = pl.program_id(1)
        stack, unstack, _ = _head_stack(W)
        b_cur, b_prev, m_cur, m_prev = _stacked_bias(W, slopes, dil)
        m_first = jnp.logical_and(m_prev, n > 0)
        for b in range(per):
            rows = slice(b * W, (b + 1) * W)
            before = slice((b - 1) * W, b * W)
            qs = stack(q_ref[rows, :])
            kc, vc = kc_ref[rows, :], vc_ref[rows, :]
            kp, vp = (kp_ref[...], vp_ref[...]) if b == 0 else (kc_ref[before, :], vc_ref[before, :])
            s_c = jnp.where(m_cur, _dot(qs, kc, _NT) * ATT_SCALE - b_cur, NEG)
            s_p = jnp.where(m_first if b == 0 else m_prev, _dot(qs, kp, _NT) * ATT_SCALE - b_prev, NEG)
            m = jnp.maximum(jnp.max(s_c, axis=1, keepdims=True), jnp.max(s_p, axis=1, keepdims=True))
            p_c = jnp.exp(s_c - m)
            p_p = jnp.exp(s_p - m)
            den = jnp.sum(p_c, axis=1, keepdims=True) + jnp.sum(p_p, axis=1, keepdims=True)
            pv = _dot(p_c.astype(BF), vc) + _dot(p_p.astype(BF), vp)
            o_ref[rows, :] = unstack(pv / den)
            l_ref[rows, :] = unstack(jnp.broadcast_to(m + jnp.log(den), pv.shape))

    cur, prev, _ = _band_specs(W, nb, per)
    out = jax.ShapeDtypeStruct((d, L, OUT_A), F32)
    res = _call(body, name=f"attn_a_fwd_g{g}", grid=(d, nb // per),
                in_specs=[cur(cols[0]), prev(cols[1]), cur(cols[1]), prev(cols[2]), cur(cols[2])],
                out_specs=(cur(0), cur(0)), out_shape=(out, out),
                args=(qkv[0], qkv[1], qkv[1], qkv[2], qkv[2]), sem=("parallel", "parallel"), comm=comm)
    return res[0], res[1], list(res[2:])


def _attn_a_bwd(qkv, cols, do, stats, g):
    win, dil = DSW_GROUPS[g]
    W = win // dil
    d, L, _ = qkv[0].shape
    nb = L // W
    per = _blocks_per_step(nb)
    nsteps = nb // per
    slopes = _slopes(g)

    def body(q_ref, qn_ref, kp_ref, kc_ref, vp_ref, vc_ref, do_ref, don_ref, st_ref, stn_ref,
             dq_ref, dk_ref, dv_ref):
        n = pl.program_id(1)
        stack, unstack, column = _head_stack(W)
        b_cur, b_prev, m_cur, m_prev = _stacked_bias(W, slopes, dil)
        m_first = jnp.logical_and(m_prev, n > 0)
        m_last = jnp.logical_and(m_prev, n < nsteps - 1)
        everything = slice(None)
        for b in range(per):
            rows = slice(b * W, (b + 1) * W)
            before = slice((b - 1) * W, b * W)
            after = slice((b + 1) * W, (b + 2) * W)
            first, last = b == 0, b == per - 1
            qs = stack(q_ref[rows, :])
            qn = stack(qn_ref[...] if last else q_ref[after, :])
            dos = stack(do_ref[rows, :])
            don = stack(don_ref[...] if last else do_ref[after, :])
            kc, vc = kc_ref[rows, :], vc_ref[rows, :]
            kp, vp = (kp_ref[...], vp_ref[...]) if first else (kc_ref[before, :], vc_ref[before, :])
            lse_c, dsum_c = column(st_ref, rows), column(st_ref, rows, STAT_OFF)
            lse_n = column(stn_ref, everything) if last else column(st_ref, after)
            dsum_n = column(stn_ref, everything, STAT_OFF) if last else column(st_ref, after, STAT_OFF)
            m_p = m_first if first else m_prev
            m_n = m_last if last else m_prev
            p_cc = jnp.exp(jnp.where(m_cur, _dot(qs, kc, _NT) * ATT_SCALE - b_cur, NEG) - lse_c)
            p_cp = jnp.exp(jnp.where(m_p, _dot(qs, kp, _NT) * ATT_SCALE - b_prev, NEG) - lse_c)
            p_nc = jnp.exp(jnp.where(m_n, _dot(qn, kc, _NT) * ATT_SCALE - b_prev, NEG) - lse_n)
            ds_cc = (p_cc * (_dot(dos, vc, _NT) - dsum_c) * ATT_SCALE).astype(BF)
            ds_cp = (p_cp * (_dot(dos, vp, _NT) - dsum_c) * ATT_SCALE).astype(BF)
            ds_nc = (p_nc * (_dot(don, vc, _NT) - dsum_n) * ATT_SCALE).astype(BF)
            dq_ref[rows, :] = unstack(_dot(ds_cc, kc) + _dot(ds_cp, kp)).astype(BF)
            dk_ref[rows, :] = (_dot(ds_cc, qs, _TN) + _dot(ds_nc, qn, _TN)).astype(BF)
            dv_ref[rows, :] = (_dot(p_cc.astype(BF), dos, _TN) + _dot(p_nc.astype(BF), don, _TN)).astype(BF)

    cur, prev, nxt = _band_specs(W, nb, per)
    out = jax.ShapeDtypeStruct((d, L, OUT_A), BF)
    cq, ck, cv = cols
    return pl.pallas_call(
        body, name=f"attn_a_bwd_g{g}",
        out_shape=(out, out, out),
        grid=(d, nsteps),
        in_specs=[cur(cq), nxt(cq), prev(ck), cur(ck), prev(cv), cur(cv), cur(0), nxt(0), cur(0), nxt(0)],
        out_specs=(cur(0), cur(0), cur(0)),
        compiler_params=_params("parallel", "parallel"),
    )(qkv[0], qkv[0], qkv[1], qkv[1], qkv[2], qkv[2], do, do, stats, stats)


SB_PAIR = 2
SB_QUAD = 4
SB_QROWS = SB_BLOCK


def _softplus_parts(z):
    e = jnp.exp(-jnp.abs(z))
    log1p_e = jnp.where(e < 1e-4, e, jnp.log(1.0 + e))
    return e, jnp.maximum(z, 0.0) + log1p_e


def _split_dot(x, t):
    hi = x.astype(BF)
    lo = (x - hi.astype(F32)).astype(BF)
    return _dot(hi, t) + _dot(lo, t)


def _sb_block(qh, kk, causal, r_run, tri_incl):
    z = _dot(qh, kk, _NT)
    e, sp = _softplus_parts(z)
    ls = jnp.where(causal, -sp, 0.0)
    cin = _split_dot(ls, tri_incl)
    a = jnp.where(causal, jnp.exp(z + cin + r_run), 0.0)
    return z, e, cin, a


def _sb_specs(S, pair=SB_PAIR):
    Q, hd = SB_QROWS, HEAD_DIM
    lanes = pair * hd
    qc = (3 * W_A) // lanes
    kc = (3 * W_A + W_B) // lanes
    vc = (3 * W_A + 2 * W_B) // lanes
    q_spec = pl.BlockSpec((Q, lanes), lambda p, i: (i, qc + p))
    k_spec = pl.BlockSpec((S, lanes), lambda p, i: (0, kc + p))
    v_spec = pl.BlockSpec((S, lanes), lambda p, i: (0, vc + p))
    o_spec = pl.BlockSpec((Q, lanes), lambda p, i: (i, p))
    full = pl.BlockSpec((S, lanes), lambda p, i: (0, p))
    return q_spec, k_spec, v_spec, o_spec, full


def _sb_stack(pair=SB_PAIR):
    Q, hd = SB_QROWS, HEAD_DIM
    lane_head = lax.broadcasted_iota(jnp.int32, (Q, pair * hd), 1) // hd

    def stack(x):
        return jnp.concatenate([jnp.where(lane_head == h, x, jnp.zeros_like(x)) for h in range(pair)], axis=0)

    def unstack(y):
        out = jnp.where(lane_head == 0, y[0:Q], 0.0)
        for h in range(1, pair):
            out = jnp.where(lane_head == h, y[h * Q:(h + 1) * Q], out)
        return out

    return stack, unstack


def _sb_iotas(i, pair=SB_PAIR):
    B, Q = SB_BLOCK, SB_QROWS
    row = lax.broadcasted_iota(jnp.int32, (Q, B), 0) + i * Q
    col = lax.broadcasted_iota(jnp.int32, (Q, B), 1)
    ahead = jnp.concatenate([col - row] * pair, axis=0)
    tr = lax.broadcasted_iota(jnp.int32, (B, B), 0)
    tc = lax.broadcasted_iota(jnp.int32, (B, B), 1)
    return ahead, tr, tc


def _sb_fwd(proj, comm=()):
    S = proj.shape[0]
    B, Q, hd = SB_BLOCK, SB_QROWS, HEAD_DIM
    nq = S // Q
    P = SB_QUAD
    R = P * Q
    q_spec, k_spec, v_spec, o_spec, _ = _sb_specs(S, P)

    def body(q_ref, k_ref, v_ref, o_ref):
        i = pl.program_id(1)
        ahead, tr, tc = _sb_iotas(i, P)
        tri_incl = (tr >= tc).astype(BF)
        stack, unstack = _sb_stack(P)
        qs = stack(q_ref[...] * ATT_SCALE)

        def cond(c):
            return jnp.logical_and(c[0] >= 0, c[-1] > SB_EXIT)

        def step(c):
            kb, r_run, acc, _ = c
            off = pl.multiple_of(kb * B, B)
            causal = ahead < -kb * B
            _, _, cin, a = _sb_block(qs, k_ref[pl.ds(off, B), :], causal, r_run, tri_incl)
            acc = acc + _dot(a.astype(BF), v_ref[pl.ds(off, B), :])
            r_run = r_run + cin[:, 0:1]
            return kb - 1, r_run, acc, jnp.max(r_run)

        init = (i, jnp.zeros((R, 1), F32), jnp.zeros((R, P * hd), F32), jnp.float32(0.0))
        fin = lax.while_loop(cond, step, init)
        o_ref[...] = unstack(fin[2])

    res = _call(body, name="sb_fwd", grid=(SB_HEADS // P, nq), in_specs=[q_spec, k_spec, v_spec],
                out_specs=o_spec, out_shape=jax.ShapeDtypeStruct((S, W_B), F32), args=(proj, proj, proj),
                sem=("parallel", "parallel"), comm=comm)
    return res[0], list(res[1:])


def _sb_bwd(proj, do, o, comm=()):
    S = proj.shape[0]
    B, Q, hd = SB_BLOCK, SB_QROWS, HEAD_DIM
    nq = S // Q
    R = SB_PAIR * Q
    q_spec, k_spec, v_spec, o_spec, full = _sb_specs(S)

    def body(q_ref, k_ref, v_ref, do_ref, o_ref, dq_ref, dk_ref, dv_ref):
        i = pl.program_id(1)

        @pl.when(i == 0)
        def _():
            dk_ref[...] = jnp.zeros_like(dk_ref)
            dv_ref[...] = jnp.zeros_like(dv_ref)

        ahead, tr, tc = _sb_iotas(i)
        tri_incl = (tr >= tc).astype(BF)
        tri_strict = (tr > tc).astype(BF)
        stack, unstack = _sb_stack()
        qs = stack(q_ref[...] * ATT_SCALE)
        dobs = stack(do_ref[...])
        o_all = o_ref[...]
        dsum = jnp.sum(dobs.astype(F32) * jnp.concatenate([o_all] * SB_PAIR, axis=0), axis=1, keepdims=True)

        def cond(c):
            return jnp.logical_and(c[0] >= 0, c[-1] > SB_EXIT)

        def step(c):
            kb, r_run, g_run, dq, _ = c
            off = pl.multiple_of(kb * B, B)
            causal = ahead < -kb * B
            kk = k_ref[pl.ds(off, B), :]
            vv = v_ref[pl.ds(off, B), :]
            z, e, cin, a = _sb_block(qs, kk, causal, r_run, tri_incl)
            a16 = a.astype(BF)
            gmat = a16.astype(F32) * _dot(dobs, vv, _NT)
            later = _split_dot(gmat, tri_strict)
            pfx = dsum - g_run - later
            sig = jnp.where(z >= 0, 1.0, e) / (1.0 + e)
            dz = jnp.where(causal, gmat - sig * pfx, 0.0).astype(BF)
            dq = dq + _dot(dz, kk)
            dk_ref[pl.ds(off, B), :] += _dot(dz, qs, _TN)
            dv_ref[pl.ds(off, B), :] += _dot(a16, dobs, _TN)
            g_run = g_run + jnp.sum(gmat, axis=1, keepdims=True)
            r_run = r_run + cin[:, 0:1]
            return kb - 1, r_run, g_run, dq, jnp.max(r_run)

        init = (i, jnp.zeros((R, 1), F32), jnp.zeros((R, 1), F32), jnp.zeros((R, SB_PAIR * hd), F32),
                jnp.float32(0.0))
        fin = lax.while_loop(cond, step, init)
        dq_ref[...] = unstack(fin[3]) * ATT_SCALE

    out = jax.ShapeDtypeStruct((S, W_B), F32)
    res = _call(body, name="sb_bwd", grid=(SB_HEADS // SB_PAIR, nq), in_specs=[q_spec, k_spec, v_spec, o_spec, o_spec],
                out_specs=(o_spec, full, full), out_shape=(out, out, out), args=(proj, proj, proj, do, o), comm=comm)
    return res[0], res[1], res[2], list(res[3:])


def _merge_wo(o_g, l_g, o_b, proj, b_gate, w_br, w_o, x, name, comm=()):
    S = o_b.shape[0]
    D = D_MODEL
    tm = _pick(S, 256)
    gcol = GATE_OFF // D

    def body(o0, o1, o2, l0, l1, l2, ob_ref, ga_ref, gb_ref, bg_ref, w_ref, wo_ref, x_ref,
             x1_ref, mg_ref, oa_ref, lse_ref):
        la, lb, lc = l0[...], l1[...], l2[...]
        mx = jnp.maximum(jnp.maximum(la, lb), lc)
        ea, eb, ec = jnp.exp(la - mx), jnp.exp(lb - mx), jnp.exp(lc - mx)
        den = ea + eb + ec
        oa = (ea * o0[...] + eb * o1[...] + ec * o2[...]) / den
        oa_ref[...] = oa
        lse_ref[...] = mx + jnp.log(den)
        oa16 = oa.astype(BF)
        ob16 = ob_ref[...].astype(BF)
        acc = x_ref[...]
        for c0 in range(0, D, FFN_CHUNK):
            cs = slice(c0, c0 + FFN_CHUNK)
            ya = _dot(oa16, w_ref[0:OUT_A, cs])
            yb = _dot(ob16, w_ref[OUT_A:OUT_A + W_B, cs])
            g_a = jax.nn.sigmoid(ga_ref[:, cs].astype(F32) + bg_ref[:, cs])
            g_b = jax.nn.sigmoid(gb_ref[:, cs].astype(F32) + bg_ref[:, D + c0:D + c0 + FFN_CHUNK])
            mg = (g_a * ya + g_b * yb).astype(BF)
            mg_ref[:, cs] = mg
            acc = acc + _dot(mg, wo_ref[cs, :])
        x1_ref[...] = acc

    nar = pl.BlockSpec((tm, OUT_A), lambda i: (i, 0))
    wide = pl.BlockSpec((tm, D), lambda i: (i, 0))
    res = _call(body, name=name, grid=(S // tm,),
                in_specs=[nar] * 7 + [pl.BlockSpec((tm, D), lambda i: (i, gcol)),
                                      pl.BlockSpec((tm, D), lambda i: (i, gcol + 1)),
                                      pl.BlockSpec((1, 2 * D), lambda i: (0, 0)),
                                      pl.BlockSpec((OUT_A + W_B, D), lambda i: (0, 0)),
                                      pl.BlockSpec((D, D), lambda i: (0, 0)), wide],
                out_specs=(wide, wide, nar, nar),
                out_shape=(jax.ShapeDtypeStruct((S, D), F32), jax.ShapeDtypeStruct((S, D), BF),
                           jax.ShapeDtypeStruct((S, OUT_A), F32), jax.ShapeDtypeStruct((S, OUT_A), F32)),
                args=(*o_g, *l_g, o_b, proj, proj, b_gate.reshape(1, 2 * D), w_br, w_o, x), comm=comm)
    return res[0], res[1], res[2], res[3], list(res[4:])


STAT_OFF = HEAD_DIM // 2


def _merge_bwd(dx, w_o, oa, ob, lse, proj, b_gate, w_br):
    S = ob.shape[0]
    D = D_MODEL
    tm = _pick(S, 256)
    gcol = GATE_OFF // D

    def body(dx_ref, wo_ref, oa_ref, ob_ref, l_ref, ga_ref, gb_ref, bg_ref, w_ref,
             dya_ref, dyb_ref, doa_ref, dob_ref, st_ref, dg_ref, dbg_ref):
        @pl.when(pl.program_id(0) == 0)
        def _():
            dbg_ref[...] = jnp.zeros_like(dbg_ref)

        dx16 = dx_ref[...].astype(BF)
        oa = oa_ref[...]
        oa16 = oa.astype(BF)
        ob16 = ob_ref[...].astype(BF)
        doa = jnp.zeros((tm, OUT_A), F32)
        dob = jnp.zeros((tm, W_B), F32)
        for c0 in range(0, D, FFN_CHUNK):
            cs = slice(c0, c0 + FFN_CHUNK)
            cs2 = slice(D + c0, D + c0 + FFN_CHUNK)
            wa = w_ref[0:OUT_A, cs]
            wb = w_ref[OUT_A:OUT_A + W_B, cs]
            dm = _dot(dx16, wo_ref[cs, :], _NT)
            ya = _dot(oa16, wa)
            yb = _dot(ob16, wb)
            g_a = jax.nn.sigmoid(ga_ref[:, cs].astype(F32) + bg_ref[:, cs])
            g_b = jax.nn.sigmoid(gb_ref[:, cs].astype(F32) + bg_ref[:, cs2])
            dga = dm * ya * g_a * (1.0 - g_a)
            dgb = dm * yb * g_b * (1.0 - g_b)
            dg_ref[:, cs] = dga.astype(BF)
            dg_ref[:, cs2] = dgb.astype(BF)
            dbg_ref[:, cs] += jnp.sum(dga, axis=0, keepdims=True)
            dbg_ref[:, cs2] += jnp.sum(dgb, axis=0, keepdims=True)
            dya = (dm * g_a).astype(BF)
            dyb = (dm * g_b).astype(BF)
            dya_ref[:, cs] = dya
            dyb_ref[:, cs] = dyb
            doa = doa + _dot(dya, wa, _NT)
            dob = dob + _dot(dyb, wb, _NT)
        doa = doa.astype(BF)
        doa_ref[...] = doa
        dob_ref[...] = dob.astype(BF)
        r = lax.broadcasted_iota(jnp.int32, (OUT_A, OUT_A), 0) // HEAD_DIM
        c = lax.broadcasted_iota(jnp.int32, (OUT_A, OUT_A), 1) // HEAD_DIM
        dsum = _split_dot(doa.astype(F32) * oa, (r == c).astype(BF))
        lane = lax.broadcasted_iota(jnp.int32, dsum.shape, 1) % HEAD_DIM
        st_ref[...] = jnp.where(lane < STAT_OFF, l_ref[...], dsum)

    nar = pl.BlockSpec((tm, OUT_A), lambda i: (i, 0))
    wide = pl.BlockSpec((tm, D), lambda i: (i, 0))
    wide2 = pl.BlockSpec((tm, 2 * D), lambda i: (i, 0))
    vec2 = pl.BlockSpec((1, 2 * D), lambda i: (0, 0))
    return pl.pallas_call(
        body, name="merge_bwd",
        out_shape=(jax.ShapeDtypeStruct((S, D), BF), jax.ShapeDtypeStruct((S, D), BF),
                   jax.ShapeDtypeStruct((S, OUT_A), BF), jax.ShapeDtypeStruct((S, W_B), BF),
                   jax.ShapeDtypeStruct((S, OUT_A), F32), jax.ShapeDtypeStruct((S, 2 * D), BF),
                   jax.ShapeDtypeStruct((1, 2 * D), F32)),
        grid=(S // tm,),
        in_specs=[wide, pl.BlockSpec((D, D), lambda i: (0, 0)), nar, nar, nar,
                  pl.BlockSpec((tm, D), lambda i: (i, gcol)), pl.BlockSpec((tm, D), lambda i: (i, gcol + 1)), vec2,
                  pl.BlockSpec((OUT_A + W_B, D), lambda i: (0, 0))],
        out_specs=(wide, wide, nar, nar, nar, wide2, vec2),
        compiler_params=_params("arbitrary"),
    )(dx, w_o, oa, ob, lse, proj, proj, b_gate.reshape(1, 2 * D), w_br)


_SQRT_HALF = 0.7071067811865476
_INV_SQRT_2PI = 0.3989422804014327


def _gelu_parts(a):
    cdf = 0.5 * (1.0 + lax.erf(a * _SQRT_HALF))
    pdf = _INV_SQRT_2PI * jnp.exp(-0.5 * a * a)
    return cdf, pdf


def _shift_down(a, halo, k):
    rows = lax.broadcasted_iota(jnp.int32, a.shape, 0)
    out = pltpu.roll(a, k, 0)
    for r in range(k):
        out = jnp.where(rows == r, halo[8 - k + r:8 - k + r + 1, :], out)
    return out


def _ffn_specs(S, tm):
    F = D_FF
    t8 = tm // 8
    a_spec = pl.BlockSpec((tm, F), lambda i: (i, 0))
    v_spec = pl.BlockSpec((tm, F), lambda i: (i, 1))
    halo_prev = pl.BlockSpec((8, F), lambda i: (jnp.maximum(i * t8 - 1, 0), 0))
    return a_spec, v_spec, halo_prev


def _ffn_down(up, conv_w, conv_b, w_down, res, *, name, comm=()):
    S = up.shape[0]
    F = D_FF
    D = w_down.shape[1]
    tm = _pick(S, 256)
    a_spec, v_spec, halo_prev = _ffn_specs(S, tm)

    def body(a_ref, h_ref, v_ref, w_ref, b_ref, wd_ref, r_ref, o_ref, act_ref, ac_ref):
        first = pl.program_id(0) == 0
        acc = r_ref[...]
        for c0 in range(0, F, FFN_CHUNK):
            cs = slice(c0, c0 + FFN_CHUNK)
            a = a_ref[:, cs].astype(F32)
            halo = jnp.where(first, 0.0, h_ref[:, cs].astype(F32))
            w = w_ref[:, cs]
            ac = b_ref[:, cs] + w[0:1, :] * _shift_down(a, halo, 2) + w[1:2, :] * _shift_down(a, halo, 1) + w[2:3, :] * a
            ac_ref[:, cs] = ac.astype(BF)
            cdf, _ = _gelu_parts(ac)
            act = (ac * cdf * v_ref[:, cs].astype(F32)).astype(BF)
            act_ref[:, cs] = act
            acc = acc + _dot(act, wd_ref[cs, :])
        o_ref[...] = acc

    row = pl.BlockSpec((tm, D), lambda i: (i, 0))
    res_ = _call(body, name=name, grid=(S // tm,),
                 in_specs=[a_spec, halo_prev, v_spec, pl.BlockSpec((3, F), lambda i: (0, 0)),
                           pl.BlockSpec((1, F), lambda i: (0, 0)), pl.BlockSpec((F, D), lambda i: (0, 0)), row],
                 out_specs=(row, a_spec, a_spec),
                 out_shape=(jax.ShapeDtypeStruct((S, D), F32), jax.ShapeDtypeStruct((S, F), BF),
                            jax.ShapeDtypeStruct((S, F), BF)),
                 args=(up, up, up, conv_w, conv_b.reshape(1, F), w_down, res), comm=comm)
    return res_[0], res_[1], res_[2], list(res_[3:])


def _shift_up_pair(a, nxt):
    n = a.shape[0]
    r8 = lax.broadcasted_iota(jnp.int32, (8,) + a.shape[1:], 0)
    out = []
    for k in (1, 2):
        rolled = pltpu.roll(a, n - k, 0)
        tail = jnp.where(r8 >= 8 - k, pltpu.roll(nxt, 8 - k, 0), rolled[n - 8:n])
        out.append(jnp.concatenate([rolled[0:n - 8], tail], axis=0))
    return out


def _ffn_bwd(dx, w_down, up, ac, conv_w, comm=()):
    S = up.shape[0]
    F = D_FF
    D = dx.shape[1]
    tm = _pick(S, 256)
    t8 = tm // 8
    nt = S // tm
    a_spec, v_spec, _ = _ffn_specs(S, tm)

    def nxt(width, col):
        return pl.BlockSpec((8, width), lambda i: (jnp.minimum((i + 1) * t8, S // 8 - 1), col))

    def body(dx_ref, dxn_ref, wd_ref, ac_ref, acn_ref, a_ref, v_ref, vn_ref, w_ref, dup_ref, dw_ref, db_ref):
        i = pl.program_id(0)

        @pl.when(i == 0)
        def _():
            dw_ref[...] = jnp.zeros_like(dw_ref)
            db_ref[...] = jnp.zeros_like(db_ref)

        dx16 = dx_ref[...].astype(BF)
        dxn16 = dxn_ref[...].astype(BF)
        last = i == nt - 1

        def dconv(dact, ac, v):
            cdf, pdf = _gelu_parts(ac)
            return cdf, dact * v * (cdf + ac * pdf)

        for c0 in range(0, F, FFN_CHUNK):
            cs = slice(c0, c0 + FFN_CHUNK)
            wd = wd_ref[cs, :]
            dact = _dot(dx16, wd, _NT)
            ac = ac_ref[:, cs].astype(F32)
            cdf, dac = dconv(dact, ac, v_ref[:, cs].astype(F32))
            dup_ref[:, F + c0:F + c0 + FFN_CHUNK] = (dact * ac * cdf).astype(BF)
            _, dac_n = dconv(_dot(dxn16, wd, _NT), acn_ref[:, cs].astype(F32), vn_ref[:, cs].astype(F32))
            d1, d2 = _shift_up_pair(dac, jnp.where(last, 0.0, dac_n))
            w = w_ref[:, cs]
            dup_ref[:, cs] = (w[2:3, :] * dac + w[1:2, :] * d1 + w[0:1, :] * d2).astype(BF)
            a = a_ref[:, cs].astype(F32)
            db_ref[:, cs] += jnp.sum(dac, axis=0, keepdims=True)
            dw_ref[0:1, cs] += jnp.sum(d2 * a, axis=0, keepdims=True)
            dw_ref[1:2, cs] += jnp.sum(d1 * a, axis=0, keepdims=True)
            dw_ref[2:3, cs] += jnp.sum(dac * a, axis=0, keepdims=True)

    w_spec = pl.BlockSpec((3, F), lambda i: (0, 0))
    b_spec = pl.BlockSpec((1, F), lambda i: (0, 0))
    res = _call(body, name="ffn_bwd", grid=(nt,),
                in_specs=[pl.BlockSpec((tm, D), lambda i: (i, 0)), nxt(D, 0), pl.BlockSpec((F, D), lambda i: (0, 0)),
                          a_spec, nxt(F, 0), a_spec, v_spec, nxt(F, 1), w_spec],
                out_specs=(pl.BlockSpec((tm, 2 * F), lambda i: (i, 0)), w_spec, b_spec),
                out_shape=(jax.ShapeDtypeStruct((S, 2 * F), BF), jax.ShapeDtypeStruct((3, F), F32),
                           jax.ShapeDtypeStruct((1, F), F32)),
                args=(dx, dx, w_down, ac, ac, up, up, up, conv_w), comm=comm)
    return res[0], res[1], res[2], list(res[3:])


def _adamw(parts, w, m, v, name, row0=0, prev=None):
    R, C = w.shape
    Rp = parts.shape[1]
    tr = Rp
    for cand in (512, 256, 128, 64, 32, 16):
        if Rp % cand == 0 and row0 % cand == 0 and cand * C * 4 <= (1 << 21):
            tr = cand
            break
    b0 = row0 // tr
    c1 = 1.0 / (1.0 - ADAM_B1 ** ADAM_STEP)
    c2 = 1.0 / (1.0 - ADAM_B2 ** ADAM_STEP)

    def body(p_ref, w_ref, m_ref, v_ref, *rest):
        g_ref, d_ref, nm_ref, nv_ref = rest[-4:]
        g = p_ref[0].astype(F32)
        for j in range(1, N_DEV):
            g = g + p_ref[j].astype(F32)
        nm = ADAM_B1 * m_ref[...] + (1.0 - ADAM_B1) * g
        nv = ADAM_B2 * v_ref[...] + (1.0 - ADAM_B2) * (g * g)
        g_ref[...] = g
        nm_ref[...] = nm
        nv_ref[...] = nv
        d_ref[...] = -ADAM_LR * ((nm * c1) / (jnp.sqrt(nv * c2) + ADAM_EPS) + ADAM_WD * w_ref[...])

    blk = pl.BlockSpec((tr, C), lambda i: (b0 + i, 0))
    out = jax.ShapeDtypeStruct((R, C), F32)
    carried = [] if prev is None else list(prev)
    return pl.pallas_call(
        body, name=name,
        out_shape=(out, out, out, out),
        grid=(Rp // tr,),
        in_specs=[pl.BlockSpec((N_DEV, tr, C), lambda i: (0, i, 0)), blk, blk, blk]
        + [pl.BlockSpec(memory_space=pl.ANY)] * len(carried),
        out_specs=(blk, blk, blk, blk),
        input_output_aliases={4 + k: k for k in range(len(carried))},
        compiler_params=_params("parallel"),
    )(parts, w, m, v, *carried)


def _dil(t, dil):
    S, C = t.shape
    if dil == 1:
        return t.reshape(1, S, C)
    return t.reshape(S // dil, dil, C).transpose(1, 0, 2)


def _undil(t):
    d, L, C = t.shape
    if d == 1:
        return t.reshape(L, C)
    return t.transpose(1, 0, 2).reshape(L * d, C)


def _group_qkv(proj, g):
    dil = DSW_GROUPS[g][1]
    if dil == 1:
        p3 = _dil(proj, 1)
        return (p3, p3, p3), (g, W_A // OUT_A + g, 2 * W_A // OUT_A + g)
    c0 = g * OUT_A
    return tuple(_dil(proj[:, o + c0:o + c0 + OUT_A], dil) for o in (0, W_A, 2 * W_A)), (0, 0, 0)


_COL_SHARDED = ("w_in", "w_br", "w_up")
_ROW_SHARDED = ("w_o", "w_down")


class _Plan:
    def __init__(self):
        self.riders = {}
        self.landed = {}

    def ride(self, slot, key, kind, x):
        self.riders.setdefault(slot, []).append((key, kind, x))

    def run(self, slot, fn, *args, **kw):
        items = self.riders.pop(slot, [])
        res = fn(*args, comm=[(kind, x) for _, kind, x in items], **kw)
        for (key, _, _), r in zip(items, res[-1]):
            self.landed[key] = r
        return res[0] if len(res) == 2 else res[:-1]

    def weight(self, n, l):
        g = self.landed[(n, l)]
        if n in _COL_SHARDED:
            return g.transpose(1, 0, 2).reshape(g.shape[1], -1)
        return g.reshape(-1, g.shape[2])

    def scatter(self, slot, n, l, full, part=0):
        K, N = full.shape
        if n in _COL_SHARDED:
            blocks = full.reshape(K, N_DEV, N // N_DEV).transpose(1, 0, 2)
        else:
            blocks = full.reshape(N_DEV, K // N_DEV, N)
        self.ride(slot, ("d" + n, l, part), "scatter", blocks)


def _layer_fwd(x, p, plan, l):
    proj, h = plan.run(f"proj_{l}", _norm_matmul, x, p["norm1"][l], plan.weight("w_in", l), name=f"proj_{l}",
                       tm=512, tn=1024)
    o_g, l_g, qkv_g = [], [], []
    for g in range(N_GROUPS):
        qkv, cols = _group_qkv(proj, g)
        og, lg = plan.run(f"attn_a_fwd_g{g}_{l}", _attn_a_fwd, qkv, cols, g)
        o_g.append(_undil(og))
        l_g.append(_undil(lg))
        qkv_g.append((qkv, cols))
    ob = plan.run(f"sb_fwd_{l}", _sb_fwd, proj)
    x1, merged, oa, lse = plan.run(f"wo_{l}", _merge_wo, o_g, l_g, ob, proj, p["b_gate"][l], plan.weight("w_br", l),
                                   plan.weight("w_o", l), x, f"wo_{l}")
    up, h2 = plan.run(f"up_{l}", _norm_matmul, x1, p["norm2"][l], plan.weight("w_up", l), name=f"up_{l}",
                      tm=512, tn=1408)
    x2, act, ac = plan.run(f"down_{l}", _ffn_down, up, p["conv_w"][l], p["conv_b"][l], plan.weight("w_down", l), x1,
                           name=f"down_{l}")
    saved = dict(x=x, h=h, proj=proj, qkv_g=qkv_g, oa=oa, ob=ob, lse=lse, merged=merged, x1=x1, h2=h2, up=up, act=act, ac=ac)
    return x2, saved


def _layer_bwd(dx2, sv, p, plan, l):
    gr = {}
    dwd = plan.run(f"dw_down_{l}", _matmul, sv["act"], dx2, mode="tn", out_dtype=BF, name=f"dw_down_{l}",
                   tm=1408, tn=1024, tk=2048)
    plan.scatter(f"d_h2_{l}", "w_down", l, dwd)
    dup, gr["conv_w"], dcb = plan.run(f"ffn_bwd_{l}", _ffn_bwd, dx2, plan.weight("w_down", l), sv["up"], sv["ac"],
                                      p["conv_w"][l])
    gr["conv_b"] = dcb[0]
    dx1, dn2 = plan.run(f"d_h2_{l}", _matmul_rms_bwd, dup, plan.weight("w_up", l), sv["x1"], p["norm2"][l], dx2,
                        name=f"d_h2_{l}")
    dwu = plan.run(f"dw_up_{l}", _matmul, sv["h2"], dup, mode="tn", out_dtype=BF, name=f"dw_up_{l}",
                   tm=1024, tn=1408, tk=2048)
    plan.scatter(f"sb_bwd_{l}", "w_up", l, dwu)
    gr["norm2"] = dn2[0]
    dwo = plan.run(f"dw_o_{l}", _matmul, sv["merged"], dx1, mode="tn", out_dtype=BF, name=f"dw_o_{l}",
                   tm=1024, tn=1024, tk=2048)
    plan.scatter(f"dw_in_{l}", "w_o", l, dwo)
    dya, dyb, doa, dob, stats, dgate, dbg = _merge_bwd(dx1, plan.weight("w_o", l), sv["oa"], sv["ob"], sv["lse"],
                                                       sv["proj"], p["b_gate"][l], plan.weight("w_br", l))
    gr["b_gate"] = dbg[0]
    dwa = plan.run(f"dw_bra_{l}", _matmul, sv["oa"], dya, mode="tn", out_dtype=BF, name=f"dw_bra_{l}",
                   tm=256, tn=1024, tk=2048)
    dwb = plan.run(f"dw_brb_{l}", _matmul, sv["ob"], dyb, mode="tn", out_dtype=BF, name=f"dw_brb_{l}",
                   tm=256, tn=1024, tk=2048)
    plan.scatter(f"dw_in_{l}", "w_br", l, jnp.concatenate([dwa, dwb], axis=0))
    proj = sv["proj"]
    dq_a, dk_a, dv_a = [], [], []
    for g, (_, dil) in enumerate(DSW_GROUPS):
        qkv, cols = sv["qkv_g"][g]
        dqg, dkg, dvg = _attn_a_bwd(qkv, cols, _dil(doa, dil), _dil(stats, dil), g)
        dq_a.append(_undil(dqg))
        dk_a.append(_undil(dkg))
        dv_a.append(_undil(dvg))
    dqb, dkb, dvb = plan.run(f"sb_bwd_{l}", _sb_bwd, proj, dob, sv["ob"])
    dproj = jnp.concatenate(dq_a + dk_a + dv_a + [dqb.astype(BF), dkb.astype(BF), dvb.astype(BF), dgate], axis=1)
    dx, dn1 = plan.run(f"d_h_{l}", _matmul_rms_bwd, dproj, plan.weight("w_in", l), sv["x"], p["norm1"][l], dx1,
                       name=f"d_h_{l}")
    if l > 0:
        dwi = plan.run(f"dw_in_{l}", _matmul, sv["h"], dproj, mode="tn", out_dtype=BF, name=f"dw_in_{l}",
                       tm=1024, tn=1280, tk=2048)
        plan.scatter(f"ffn_bwd_{l - 1}", "w_in", l, dwi)
    else:
        slabs = 4
        rows = D_MODEL // slabs
        for part in range(slabs):
            name = f"dw_in_{l}" if part == 0 else f"dw_in_{l}_slab{part}"
            dwi = plan.run(name, _matmul, sv["h"], dproj, mode="tn", out_dtype=BF, name=name, tm=rows, tn=1280,
                           tk=2048, m_cols=rows, m_off=part * rows)
            plan.scatter(f"dw_in_{l}_slab{part + 1}" if part + 1 < slabs else "alone", "w_in", l, dwi, part)
    gr["norm1"] = dn1[0]
    return dx, gr


def kernel(x, norm1, w_in, b_gate, w_br, w_o, norm2, w_up, conv_w, conv_b, w_down, norm_f, loss_target, m_norm1, m_w_in, m_b_gate, m_w_br, m_w_o, m_norm2, m_w_up, m_conv_w, m_conv_b, m_w_down, m_norm_f, v_norm1, v_w_in, v_b_gate, v_w_br, v_w_o, v_norm2, v_w_up, v_conv_w, v_conv_b, v_w_down, v_norm_f):
    depth = norm1.shape[0]
    me = 4 * lax.axis_index("x") + 2 * lax.axis_index("y") + lax.axis_index("c")
    shards = dict(w_in=w_in, w_br=w_br, w_o=w_o, w_up=w_up, w_down=w_down)
    moments_m = dict(norm1=m_norm1, w_in=m_w_in, b_gate=m_b_gate, w_br=m_w_br, w_o=m_w_o, norm2=m_norm2,
                     w_up=m_w_up, conv_w=m_conv_w, conv_b=m_conv_b, w_down=m_w_down, norm_f=m_norm_f)
    moments_v = dict(norm1=v_norm1, w_in=v_w_in, b_gate=v_b_gate, w_br=v_w_br, w_o=v_w_o, norm2=v_norm2,
                     w_up=v_w_up, conv_w=v_conv_w, conv_b=v_conv_b, w_down=v_w_down, norm_f=v_norm_f)

    plan = _Plan()
    wb = {n: s.astype(BF) for n, s in shards.items()}
    p = dict(norm1=norm1, b_gate=b_gate, norm2=norm2, conv_b=conv_b)
    cw = _all_gather(conv_w, "gather_conv_w")
    p["conv_w"] = cw.transpose(1, 2, 0, 3).reshape(depth, 3, D_FF)
    plan.landed[("w_in", 0)] = _all_gather_via_sibling(wb["w_in"][0], "gather_w_in_0")
    for l in range(depth):
        plan.ride(f"proj_{l}", ("w_down", l), "gather", wb["w_down"][l])
        plan.ride(f"attn_a_fwd_g0_{l}" if l == 0 else f"down_{l - 1}", ("w_br", l), "gather", wb["w_br"][l])
        plan.ride(f"attn_a_fwd_g0_{l}" if l == 0 else f"down_{l - 1}", ("w_o", l), "gather", wb["w_o"][l])
        plan.ride(f"sb_fwd_{l}", ("w_up", l), "gather", wb["w_up"][l])
        if l + 1 < depth:
            plan.ride(f"up_{l}", ("w_in", l + 1), "gather", wb["w_in"][l + 1])

    xs = x[0]
    saved = []
    for l in range(depth):
        xs, sv = _layer_fwd(xs, p, plan, l)
        saved.append(sv)
    loss_part, dx, dnf = _loss_head(xs, norm_f, loss_target[0])
    loss = lax.psum(loss_part[0, 0], ("x", "y", "c"))

    grads = [None] * depth
    for l in reversed(range(depth)):
        dx, grads[l] = _layer_bwd(dx, saved[l], p, plan, l)
    grad_x = dx[None]
    (key, _, last), = plan.riders.pop("alone")
    plan.landed[key] = _all_to_all(last, "scatter_w_in_rest")
    assert not plan.riders, sorted(plan.riders)

    out_g, out_d, out_m, out_v = {}, {}, {}, {}
    for n in _COL_SHARDED + _ROW_SHARDED:
        shp = shards[n].shape
        flat = (shp[0] * shp[1], shp[2])
        res = None
        for l in range(depth):
            row = l * shp[1]
            for key in sorted(k for k in plan.landed if k[:2] == ("d" + n, l)):
                parts = plan.landed[key]
                res = _adamw(parts, shards[n].reshape(flat), moments_m[n].reshape(flat), moments_v[n].reshape(flat),
                             f"adamw_{n}_{l}_{key[2]}", row0=row, prev=res)
                row += parts.shape[1]
        out_g[n], out_d[n], out_m[n], out_v[n] = [r.reshape(shp) for r in res]

    small = ("norm1", "b_gate", "norm2", "conv_b")
    vecs = [jnp.stack([grads[l][n] for l in range(depth)]).reshape(-1) for n in small]
    vecs.append(dnf.reshape(-1))
    vecs.append(jnp.stack([grads[l]["conv_w"] for l in range(depth)]).reshape(-1))
    sizes = [v.shape[0] for v in vecs]
    flat = jnp.concatenate(vecs)
    n_small = sum(sizes[:-1])
    pad = (-flat.shape[0]) % 1024
    flat = jnp.pad(flat, (0, pad)).reshape(-1, 128)
    allp = _all_gather(flat, "gather_small_grads").reshape(N_DEV, -1)
    rep_w = jnp.concatenate([norm1.reshape(-1), b_gate.reshape(-1), norm2.reshape(-1), conv_b.reshape(-1), norm_f])
    rep_m = jnp.concatenate([moments_m[n].reshape(-1) for n in small] + [m_norm_f])
    rep_v = jnp.concatenate([moments_v[n].reshape(-1) for n in small] + [v_norm_f])
    rows = n_small // 128
    res = _adamw(allp[:, :n_small].reshape(N_DEV, rows, 128), rep_w.reshape(rows, 128), rep_m.reshape(rows, 128),
                 rep_v.reshape(rows, 128), "adamw_small")
    off = 0
    for n, sz in zip(small + ("norm_f",), sizes[:-1]):
        shp = norm_f.shape if n == "norm_f" else p[n].shape
        out_g[n], out_d[n], out_m[n], out_v[n] = [r.reshape(-1)[off:off + sz].reshape(shp) for r in res]
        off += sz
    f = conv_w.shape[2]
    cwp = allp[:, n_small:n_small + sizes[-1]].reshape(N_DEV, depth * 3, D_FF)
    cwp = lax.dynamic_slice_in_dim(cwp, me * f, f, axis=2)
    res = _adamw(cwp, conv_w.reshape(depth * 3, f), m_conv_w.reshape(depth * 3, f), v_conv_w.reshape(depth * 3, f),
                 "adamw_conv_w")
    out_g["conv_w"], out_d["conv_w"], out_m["conv_w"], out_v["conv_w"] = [r.reshape(conv_w.shape) for r in res]

    order = ("norm1", "w_in", "b_gate", "w_br", "w_o", "norm2", "w_up", "conv_w", "conv_b", "w_down", "norm_f")
    return (loss, grad_x, *[out_g[n] for n in order], *[out_d[n] for n in order],
            *[out_m[n] for n in order], *[out_v[n] for n in order])
```

```python
import functools

import jax
import jax.numpy as jnp
from jax import lax
from jax.experimental import pallas as pl
from jax.experimental.pallas import tpu as pltpu

BF = jnp.bfloat16
F32 = jnp.float32

N_DEV = 8
D_MODEL = 1024
HEAD_DIM = 64
DSW_GROUPS = ((128, 1), (512, 4), (2048, 16))
HEADS_PER_GROUP = 4
N_GROUPS = len(DSW_GROUPS)
DSW_HEADS = HEADS_PER_GROUP * N_GROUPS
SB_HEADS = 4
W_A = DSW_HEADS * HEAD_DIM
W_B = SB_HEADS * HEAD_DIM
OUT_A = HEADS_PER_GROUP * HEAD_DIM
N_IN = 3 * W_A + 3 * W_B + 2 * D_MODEL
GATE_OFF = 3 * W_A + 3 * W_B
D_FF = 2816
SB_BLOCK = 256
RMS_EPS = 1e-6
ATT_SCALE = HEAD_DIM ** -0.5
NEG = -1e30
SB_EXIT = -110.0
FFN_CHUNK = 256

ADAM_LR = 0.001
ADAM_B1 = 0.9
ADAM_B2 = 0.999
ADAM_EPS = 1e-08
ADAM_WD = 0.01
ADAM_STEP = 10

HBM_SPEC = pl.BlockSpec(memory_space=pltpu.HBM)
MESH = pl.DeviceIdType.MESH

_NN = (((1,), (0,)), ((), ()))
_NT = (((1,), (1,)), ((), ()))
_TN = (((0,), (0,)), ((), ()))


def _dot(a, b, dn=_NN):
    return lax.dot_general(a, b, dn, preferred_element_type=F32)


def _pick(dim, pref):
    if dim <= pref:
        return dim
    t = (pref // 128) * 128
    while t >= 128:
        if dim % t == 0:
            return t
        t -= 128
    return dim


def _params(*sem):
    return pltpu.CompilerParams(dimension_semantics=sem)


def _peer(k):
    x, y, c = lax.axis_index("x"), lax.axis_index("y"), lax.axis_index("c")
    px = 1 - x if (k >> 2) & 1 else x
    py = 1 - y if (k >> 1) & 1 else y
    pc = 1 - c if k & 1 else c
    return (px, py, pc), 4 * px + 2 * py + pc


def _exchange(kind, x_ref, out_ref, send_sems, recv_sems, local_sem):
    gather = kind == "gather"
    _, me = _peer(0)

    def src(idx):
        return x_ref if gather else x_ref.at[idx]

    def copy(k, dst_idx):
        peer, pidx = _peer(k)
        return pltpu.make_async_remote_copy(
            src_ref=src(pidx), dst_ref=out_ref.at[dst_idx], send_sem=send_sems.at[k - 1],
            recv_sem=recv_sems.at[k - 1], device_id=peer, device_id_type=MESH)

    mine = pltpu.make_async_copy(src(me), out_ref.at[me], local_sem)

    def start():
        mine.start()
        for k in range(1, N_DEV):
            copy(k, me).start()

    def wait():
        for k in range(1, N_DEV):
            copy(k, _peer(k)[1]).wait_recv()
        for k in range(1, N_DEV):
            copy(k, me).wait_send()
        mine.wait()

    return start, wait


_EXCHANGE_SEMS = [pltpu.SemaphoreType.DMA((N_DEV - 1,)), pltpu.SemaphoreType.DMA((N_DEV - 1,)),
                  pltpu.SemaphoreType.DMA]


def _exchange_shape(kind, x):
    return jax.ShapeDtypeStruct(((N_DEV,) + x.shape) if kind == "gather" else x.shape, x.dtype)


def _exchange_alone(kind, x, name):
    def body(x_ref, out_ref, send_sems, recv_sems, local_sem):
        start, wait = _exchange(kind, x_ref, out_ref, send_sems, recv_sems, local_sem)
        start()
        wait()

    return pl.pallas_call(
        body, name=name, out_shape=_exchange_shape(kind, x),
        in_specs=[HBM_SPEC], out_specs=HBM_SPEC, scratch_shapes=list(_EXCHANGE_SEMS),
    )(x)


def _all_gather_via_sibling(x, name):
    def body(x_ref, out_ref, send_sems, recv_sems, local_sem):
        x_, y_, c_ = lax.axis_index("x"), lax.axis_index("y"), lax.axis_index("c")
        me, sibling = (x_, y_, c_), (x_, y_, 1 - c_)
        chips = [(1 - x_, y_), (x_, 1 - y_), (1 - x_, 1 - y_)]

        def slot(px, py, pc):
            return out_ref.at[4 * px + 2 * py + pc]

        def copy(k, block, to, src=None):
            return pltpu.make_async_remote_copy(
                src_ref=slot(*block) if src is None else src, dst_ref=slot(*block), send_sem=send_sems.at[k],
                recv_sem=recv_sems.at[k], device_id=to, device_id_type=MESH)

        mine = pltpu.make_async_copy(x_ref, slot(*me), local_sem)
        mine.start()
        first = [copy(0, me, sibling, src=x_ref)]
        first += [copy(1 + j, me, (*chip, c_), src=x_ref) for j, chip in enumerate(chips)]
        for cp in first:
            cp.start()
        passed = [copy(4 + j, (*chip, c_), sibling) for j, chip in enumerate(chips)]
        for j, chip in enumerate(chips):
            copy(1 + j, (*chip, c_), me).wait_recv()
            passed[j].start()
        copy(0, sibling, me).wait_recv()
        for j, chip in enumerate(chips):
            copy(4 + j, (*chip, 1 - c_), me).wait_recv()
        for cp in first + passed:
            cp.wait_send()
        mine.wait()

    return pl.pallas_call(
        body, name=name, out_shape=_exchange_shape("gather", x),
        in_specs=[HBM_SPEC], out_specs=HBM_SPEC,
        scratch_shapes=[pltpu.SemaphoreType.DMA((N_DEV - 1,)), pltpu.SemaphoreType.DMA((N_DEV - 1,)),
                        pltpu.SemaphoreType.DMA],
    )(x)


def _all_gather(x, name):
    return _exchange_alone("gather", x, name)


def _all_to_all(x, name):
    return _exchange_alone("scatter", x, name)


def _call(body, *, name, grid, in_specs, out_specs, out_shape, args, scratch_shapes=(), sem=None, comm=()):
    single = not isinstance(out_shape, (tuple, list))
    outs = (out_shape,) if single else tuple(out_shape)
    ospecs = (out_specs,) if single else tuple(out_specs)
    if not comm:
        res = pl.pallas_call(
            body, name=name, out_shape=outs, grid=grid, in_specs=list(in_specs), out_specs=ospecs,
            scratch_shapes=list(scratch_shapes), compiler_params=_params(*(sem or ("arbitrary",) * len(grid))),
        )(*args)
        return res
    n_in, n_out, n_scr, nc = len(in_specs), len(outs), len(scratch_shapes), len(comm)

    def wrapped(*refs):
        ins = refs[:n_in]
        cins = refs[n_in:n_in + nc]
        o0 = n_in + nc
        kouts = refs[o0:o0 + n_out]
        couts = refs[o0 + n_out:o0 + n_out + nc]
        s0 = o0 + n_out + nc
        scr = refs[s0:s0 + n_scr]
        sems = refs[s0 + n_scr:]
        ids = [pl.program_id(ax) for ax in range(len(grid))]
        first = functools.reduce(jnp.logical_and, [i == 0 for i in ids])
        last = functools.reduce(jnp.logical_and, [i == g - 1 for i, g in zip(ids, grid)])
        ex = [_exchange(comm[c][0], cins[c], couts[c], *sems[3 * c:3 * c + 3]) for c in range(nc)]

        @pl.when(first)
        def _():
            for start, _ in ex:
                start()

        body(*ins, *kouts, *scr)

        @pl.when(last)
        def _():
            for _, wait in ex:
                wait()

    return pl.pallas_call(
        wrapped, name=name,
        out_shape=outs + tuple(_exchange_shape(k, x) for k, x in comm),
        grid=grid, in_specs=list(in_specs) + [HBM_SPEC] * nc, out_specs=ospecs + (HBM_SPEC,) * nc,
        scratch_shapes=list(scratch_shapes) + list(_EXCHANGE_SEMS) * nc,
        compiler_params=_params(*(("arbitrary",) * len(grid))),
    )(*args, *[x for _, x in comm])


def _matmul(a, b, *, mode, out_dtype, name, tm=512, tn=1024, tk=1024, res=None, comm=(), m_cols=None, m_off=0):
    if mode == "nn":
        (M, K), (_, N) = a.shape, b.shape
    elif mode == "nt":
        (M, K), (N, _) = a.shape, b.shape
    else:
        (K, M), (_, N) = a.shape, b.shape
        M = M if m_cols is None else m_cols
    tm, tn, tk = _pick(M, tm), _pick(N, tn), _pick(K, tk)
    i0 = m_off // tm
    nk = K // tk
    dn = {"nn": _NN, "nt": _NT, "tn": _TN}[mode]

    def body(*refs):
        a_ref, b_ref = refs[0], refs[1]
        r_ref = refs[2] if res is not None else None
        o_ref = refs[3] if res is not None else refs[2]

        def finish(r):
            if res is not None:
                r = r + r_ref[...].astype(F32)
            o_ref[...] = r.astype(out_dtype)

        part = _dot(a_ref[...].astype(BF), b_ref[...].astype(BF), dn)
        if nk == 1:
            finish(part)
            return
        acc = refs[-1]
        k = pl.program_id(2)

        @pl.when(k == 0)
        def _():
            acc[...] = jnp.zeros_like(acc)

        acc[...] += part

        @pl.when(k == nk - 1)
        def _():
            finish(acc[...])

    if mode == "tn":
        a_spec = pl.BlockSpec((tk, tm), lambda j, i, k: (k, i0 + i))
    else:
        a_spec = pl.BlockSpec((tm, tk), lambda j, i, k: (i, k))
    if mode == "nt":
        b_spec = pl.BlockSpec((tn, tk), lambda j, i, k: (j, k))
    else:
        b_spec = pl.BlockSpec((tk, tn), lambda j, i, k: (k, j))
    o_spec = pl.BlockSpec((tm, tn), lambda j, i, k: (i, j))
    in_specs = [a_spec, b_spec] + ([o_spec] if res is not None else [])
    args = (a, b) + ((res,) if res is not None else ())
    out = _call(body, name=name, grid=(N // tn, M // tm, nk), in_specs=in_specs, out_specs=o_spec,
                out_shape=jax.ShapeDtypeStruct((M, N), out_dtype), args=args,
                scratch_shapes=[pltpu.VMEM((tm, tn), F32)] if nk > 1 else [],
                sem=("parallel", "parallel", "arbitrary"), comm=comm)
    return out[0], list(out[1:])


def _norm_matmul(x, g, w, *, name, tm=512, tn=1024, comm=()):
    S, K = x.shape
    N = w.shape[1]
    tm, tn = _pick(S, tm), _pick(N, tn)
    slab = min(tm, 256)

    def body(x_ref, g_ref, w_ref, o_ref, h_ref):
        gg = g_ref[...]
        for r0 in range(0, tm, slab):
            rs = slice(r0, r0 + slab)
            xf = x_ref[rs, :]
            r = lax.rsqrt(jnp.mean(xf * xf, axis=-1, keepdims=True) + RMS_EPS)
            hh = (xf * r * gg).astype(BF)
            h_ref[rs, :] = hh
            for c0 in range(0, N, tn):
                o_ref[rs, c0:c0 + tn] = _dot(hh, w_ref[:, c0:c0 + tn]).astype(BF)

    res = _call(body, name=name, grid=(S // tm,),
                in_specs=[pl.BlockSpec((tm, K), lambda i: (i, 0)), pl.BlockSpec((1, K), lambda i: (0, 0)),
                          pl.BlockSpec((K, N), lambda i: (0, 0))],
                out_specs=(pl.BlockSpec((tm, N), lambda i: (i, 0)), pl.BlockSpec((tm, K), lambda i: (i, 0))),
                out_shape=(jax.ShapeDtypeStruct((S, N), BF), jax.ShapeDtypeStruct((S, K), BF)),
                args=(x, g.reshape(1, K), w), comm=comm)
    return res[0], res[1], list(res[2:])


def _matmul_rms_bwd(dy, w, x, g, dres, *, name, tm=512, comm=()):
    S, K = dy.shape
    D = w.shape[0]
    tm = _pick(S, tm)

    def body(dy_ref, w_ref, x_ref, g_ref, dres_ref, dx_ref, dg_ref):
        @pl.when(pl.program_id(0) == 0)
        def _():
            dg_ref[...] = jnp.zeros_like(dg_ref)

        dh = _dot(dy_ref[...].astype(BF), w_ref[...], _NT)
        xf = x_ref[...]
        r = lax.rsqrt(jnp.mean(xf * xf, axis=-1, keepdims=True) + RMS_EPS)
        xh = xf * r
        dg_ref[...] += jnp.sum(dh * xh, axis=0, keepdims=True)
        dxh = dh * g_ref[...]
        dx_ref[...] = dres_ref[...] + r * (dxh - xh * jnp.mean(dxh * xh, axis=-1, keepdims=True))

    row = pl.BlockSpec((tm, D), lambda i: (i, 0))
    vec = pl.BlockSpec((1, D), lambda i: (0, 0))
    res = _call(body, name=name, grid=(S // tm,),
                in_specs=[pl.BlockSpec((tm, K), lambda i: (i, 0)), pl.BlockSpec((D, K), lambda i: (0, 0)), row, vec, row],
                out_specs=(row, vec),
                out_shape=(jax.ShapeDtypeStruct((S, D), F32), jax.ShapeDtypeStruct((1, D), F32)),
                args=(dy, w, x, g.reshape(1, D), dres), comm=comm)
    return res[0], res[1], list(res[2:])


def _loss_head(x, g, target):
    S, D = x.shape
    tm = _pick(S, 512)

    def body(x_ref, g_ref, t_ref, loss_ref, dx_ref, dg_ref):
        @pl.when(pl.program_id(0) == 0)
        def _():
            dg_ref[...] = jnp.zeros_like(dg_ref)
            loss_ref[...] = jnp.zeros_like(loss_ref)

        xf = x_ref[...]
        gg = g_ref[...]
        r = lax.rsqrt(jnp.mean(xf * xf, axis=-1, keepdims=True) + RMS_EPS)
        xh = xf * r
        err = xh * gg - t_ref[...]
        per_tok = jnp.mean(err * err, axis=-1, keepdims=True)
        loss_ref[...] += 0.5 * jnp.sum(per_tok, axis=0, keepdims=True)
        dy = err * (1.0 / D)
        dg_ref[...] += jnp.sum(dy * xh, axis=0, keepdims=True)
        dxh = dy * gg
        dx_ref[...] = r * (dxh - xh * jnp.mean(dxh * xh, axis=-1, keepdims=True))

    row = pl.BlockSpec((tm, D), lambda i: (i, 0))
    vec = pl.BlockSpec((1, D), lambda i: (0, 0))
    one = pl.BlockSpec((1, 1), lambda i: (0, 0))
    return pl.pallas_call(
        body, name="loss_head",
        out_shape=(jax.ShapeDtypeStruct((1, 1), F32), jax.ShapeDtypeStruct((S, D), F32),
                   jax.ShapeDtypeStruct((1, D), F32)),
        grid=(S // tm,),
        in_specs=[row, vec, row], out_specs=(one, row, vec),
        compiler_params=_params("arbitrary"),
    )(x, g.reshape(1, D), target)


def _slopes(g):
    return [2.0 ** (-8.0 * (HEADS_PER_GROUP * g + j + 1) / DSW_HEADS) for j in range(HEADS_PER_GROUP)]


def _band_masks(W):
    row = lax.broadcasted_iota(jnp.int32, (W, W), 0)
    col = lax.broadcasted_iota(jnp.int32, (W, W), 1)
    d_cur = row - col
    d_prev = d_cur + W
    return d_cur, d_prev, d_cur >= 0, d_cur <= 0


def _band_specs(W, nb, per):
    def cur(c):
        return pl.BlockSpec((None, per * W, OUT_A), lambda r, n: (r, n, c))

    def prev(c):
        return pl.BlockSpec((None, W, OUT_A), lambda r, n: (r, jnp.maximum(per * n - 1, 0), c))

    def nxt(c):
        return pl.BlockSpec((None, W, OUT_A), lambda r, n: (r, jnp.minimum(per * (n + 1), nb - 1), c))

    return cur, prev, nxt


def _blocks_per_step(nb):
    return 8 if nb % 8 == 0 else 4 if nb % 4 == 0 else 2 if nb % 2 == 0 else 1


def _head_stack(W):
    H, hd = HEADS_PER_GROUP, HEAD_DIM
    lane_head = lax.broadcasted_iota(jnp.int32, (W, OUT_A), 1) // hd

    def stack(x):
        return jnp.concatenate([jnp.where(lane_head == h, x, jnp.zeros_like(x)) for h in range(H)], axis=0)

    def unstack(y):
        out = jnp.where(lane_head == 0, y[0:W], 0.0)
        for h in range(1, H):
            out = jnp.where(lane_head == h, y[h * W:(h + 1) * W], out)
        return out

    def column(ref, rows, off=0):
        return jnp.concatenate([ref[rows, h * hd + off:h * hd + off + 1] for h in range(H)], axis=0)

    return stack, unstack, column


def _stacked_bias(W, slopes, dil):
    d_cur, d_prev, m_cur, m_prev = _band_masks(W)
    b_cur = jnp.concatenate([(s * dil) * d_cur.astype(F32) for s in slopes], axis=0)
    b_prev = jnp.concatenate([(s * dil) * d_prev.astype(F32) for s in slopes], axis=0)
    H = len(slopes)
    return b_cur, b_prev, jnp.concatenate([m_cur] * H, axis=0), jnp.concatenate([m_prev] * H, axis=0)


def _attn_a_fwd(qkv, cols, g, comm=()):
    win, dil = DSW_GROUPS[g]
    W = win // dil
    d, L, _ = qkv[0].shape
    nb = L // W
    per = _blocks_per_step(nb)
    slopes = _slopes(g)

    def body(q_ref, kp_ref, kc_ref, vp_ref, vc_ref, o_ref, l_ref):
        n = pl.program_id(1)
        stack, unstack, _ = _head_stack(W)
        b_cur, b_prev, m_cur, m_prev = _stacked_bias(W, slopes, dil)
        m_first = jnp.logical_and(m_prev, n > 0)
        for b in range(per):
            rows = slice(b * W, (b + 1) * W)
            before = slice((b - 1) * W, b * W)
            qs = stack(q_ref[rows, :])
            kc, vc = kc_ref[rows, :], vc_ref[rows, :]
            kp, vp = (kp_ref[...], vp_ref[...]) if b == 0 else (kc_ref[before, :], vc_ref[before, :])
            s_c = jnp.where(m_cur, _dot(qs, kc, _NT) * ATT_SCALE - b_cur, NEG)
            s_p = jnp.where(m_first if b == 0 else m_prev, _dot(qs, kp, _NT) * ATT_SCALE - b_prev, NEG)
            m = jnp.maximum(jnp.max(s_c, axis=1, keepdims=True), jnp.max(s_p, axis=1, keepdims=True))
            p_c = jnp.exp(s_c - m)
            p_p = jnp.exp(s_p - m)
            den = jnp.sum(p_c, axis=1, keepdims=True) + jnp.sum(p_p, axis=1, keepdims=True)
            pv = _dot(p_c.astype(BF), vc) + _dot(p_p.astype(BF), vp)
            o_ref[rows, :] = unstack(pv / den)
            l_ref[rows, :] = unstack(jnp.broadcast_to(m + jnp.log(den), pv.shape))

    cur, prev, _ = _band_specs(W, nb, per)
    out = jax.ShapeDtypeStruct((d, L, OUT_A), F32)
    res = _call(body, name=f"attn_a_fwd_g{g}", grid=(d, nb // per),
                in_specs=[cur(cols[0]), prev(cols[1]), cur(cols[1]), prev(cols[2]), cur(cols[2])],
                out_specs=(cur(0), cur(0)), out_shape=(out, out),
                args=(qkv[0], qkv[1], qkv[1], qkv[2], qkv[2]), sem=("parallel", "parallel"), comm=comm)
    return res[0], res[1], list(res[2:])


def _attn_a_bwd(qkv, cols, do, stats, g):
    win, dil = DSW_GROUPS[g]
    W = win // dil
    d, L, _ = qkv[0].shape
    nb = L // W
    per = _blocks_per_step(nb)
    nsteps = nb // per
    slopes = _slopes(g)

    def body(q_ref, qn_ref, kp_ref, kc_ref, vp_ref, vc_ref, do_ref, don_ref, st_ref, stn_ref,
             dq_ref, dk_ref, dv_ref):
        n = pl.program_id(1)
        stack, unstack, column = _head_stack(W)
        b_cur, b_prev, m_cur, m_prev = _stacked_bias(W, slopes, dil)
        m_first = jnp.logical_and(m_prev, n > 0)
        m_last = jnp.logical_and(m_prev, n < nsteps - 1)
        everything = slice(None)
        for b in range(per):
            rows = slice(b * W, (b + 1) * W)
            before = slice((b - 1) * W, b * W)
            after = slice((b + 1) * W, (b + 2) * W)
            first, last = b == 0, b == per - 1
            qs = stack(q_ref[rows, :])
            qn = stack(qn_ref[...] if last else q_ref[after, :])
            dos = stack(do_ref[rows, :])
            don = stack(don_ref[...] if last else do_ref[after, :])
            kc, vc = kc_ref[rows, :], vc_ref[rows, :]
            kp, vp = (kp_ref[...], vp_ref[...]) if first else (kc_ref[before, :], vc_ref[before, :])
            lse_c, dsum_c = column(st_ref, rows), column(st_ref, rows, STAT_OFF)
            lse_n = column(stn_ref, everything) if last else column(st_ref, after)
            dsum_n = column(stn_ref, everything, STAT_OFF) if last else column(st_ref, after, STAT_OFF)
            m_p = m_first if first else m_prev
            m_n = m_last if last else m_prev
            p_cc = jnp.exp(jnp.where(m_cur, _dot(qs, kc, _NT) * ATT_SCALE - b_cur, NEG) - lse_c)
            p_cp = jnp.exp(jnp.where(m_p, _dot(qs, kp, _NT) * ATT_SCALE - b_prev, NEG) - lse_c)
            p_nc = jnp.exp(jnp.where(m_n, _dot(qn, kc, _NT) * ATT_SCALE - b_prev, NEG) - lse_n)
            ds_cc = (p_cc * (_dot(dos, vc, _NT) - dsum_c) * ATT_SCALE).astype(BF)
            ds_cp = (p_cp * (_dot(dos, vp, _NT) - dsum_c) * ATT_SCALE).astype(BF)
            ds_nc = (p_nc * (_dot(don, vc, _NT) - dsum_n) * ATT_SCALE).astype(BF)
            dq_ref[rows, :] = unstack(_dot(ds_cc, kc) + _dot(ds_cp, kp)).astype(BF)
            dk_ref[rows, :] = (_dot(ds_cc, qs, _TN) + _dot(ds_nc, qn, _TN)).astype(BF)
            dv_ref[rows, :] = (_dot(p_cc.astype(BF), dos, _TN) + _dot(p_nc.astype(BF), don, _TN)).astype(BF)

    cur, prev, nxt = _band_specs(W, nb, per)
    out = jax.ShapeDtypeStruct((d, L, OUT_A), BF)
    cq, ck, cv = cols
    return pl.pallas_call(
        body, name=f"attn_a_bwd_g{g}",
        out_shape=(out, out, out),
        grid=(d, nsteps),
        in_specs=[cur(cq), nxt(cq), prev(ck), cur(ck), prev(cv), cur(cv), cur(0), nxt(0), cur(0), nxt(0)],
        out_specs=(cur(0), cur(0), cur(0)),
        compiler_params=_params("parallel", "parallel"),
    )(qkv[0], qkv[0], qkv[1], qkv[1], qkv[2], qkv[2], do, do, stats, stats)


SB_PAIR = 2
SB_QUAD = 4
SB_QROWS = SB_BLOCK


def _softplus_parts(z):
    e = jnp.exp(-jnp.abs(z))
    log1p_e = jnp.where(e < 1e-4, e, jnp.log(1.0 + e))
    return e, jnp.maximum(z, 0.0) + log1p_e


def _split_dot(x, t):
    hi = x.astype(BF)
    lo = (x - hi.astype(F32)).astype(BF)
    return _dot(hi, t) + _dot(lo, t)


def _sb_block(qh, kk, causal, r_run, tri_incl):
    z = _dot(qh, kk, _NT)
    e, sp = _softplus_parts(z)
    ls = jnp.where(causal, -sp, 0.0)
    cin = _split_dot(ls, tri_incl)
    a = jnp.where(causal, jnp.exp(z + cin + r_run), 0.0)
    return z, e, cin, a


def _sb_specs(S, pair=SB_PAIR):
    Q, hd = SB_QROWS, HEAD_DIM
    lanes = pair * hd
    qc = (3 * W_A) // lanes
    kc = (3 * W_A + W_B) // lanes
    vc = (3 * W_A + 2 * W_B) // lanes
    q_spec = pl.BlockSpec((Q, lanes), lambda p, i: (i, qc + p))
    k_spec = pl.BlockSpec((S, lanes), lambda p, i: (0, kc + p))
    v_spec = pl.BlockSpec((S, lanes), lambda p, i: (0, vc + p))
    o_spec = pl.BlockSpec((Q, lanes), lambda p, i: (i, p))
    full = pl.BlockSpec((S, lanes), lambda p, i: (0, p))
    return q_spec, k_spec, v_spec, o_spec, full


def _sb_stack(pair=SB_PAIR):
    Q, hd = SB_QROWS, HEAD_DIM
    lane_head = lax.broadcasted_iota(jnp.int32, (Q, pair * hd), 1) // hd

    def stack(x):
        return jnp.concatenate([jnp.where(lane_head == h, x, jnp.zeros_like(x)) for h in range(pair)], axis=0)

    def unstack(y):
        out = jnp.where(lane_head == 0, y[0:Q], 0.0)
        for h in range(1, pair):
            out = jnp.where(lane_head == h, y[h * Q:(h + 1) * Q], out)
        return out

    return stack, unstack


def _sb_iotas(i, pair=SB_PAIR):
    B, Q = SB_BLOCK, SB_QROWS
    row = lax.broadcasted_iota(jnp.int32, (Q, B), 0) + i * Q
    col = lax.broadcasted_iota(jnp.int32, (Q, B), 1)
    ahead = jnp.concatenate([col - row] * pair, axis=0)
    tr = lax.broadcasted_iota(jnp.int32, (B, B), 0)
    tc = lax.broadcasted_iota(jnp.int32, (B, B), 1)
    return ahead, tr, tc


def _sb_fwd(proj, comm=()):
    S = proj.shape[0]
    B, Q, hd = SB_BLOCK, SB_QROWS, HEAD_DIM
    nq = S // Q
    P = SB_QUAD
    R = P * Q
    q_spec, k_spec, v_spec, o_spec, _ = _sb_specs(S, P)

    def body(q_ref, k_ref, v_ref, o_ref):
        i = pl.program_id(1)
        ahead, tr, tc = _sb_iotas(i, P)
        tri_incl = (tr >= tc).astype(BF)
        stack, unstack = _sb_stack(P)
        qs = stack(q_ref[...] * ATT_SCALE)

        def cond(c):
            return jnp.logical_and(c[0] >= 0, c[-1] > SB_EXIT)

        def step(c):
            kb, r_run, acc, _ = c
            off = pl.multiple_of(kb * B, B)
            causal = ahead < -kb * B
            _, _, cin, a = _sb_block(qs, k_ref[pl.ds(off, B), :], causal, r_run, tri_incl)
            acc = acc + _dot(a.astype(BF), v_ref[pl.ds(off, B), :])
            r_run = r_run + cin[:, 0:1]
            return kb - 1, r_run, acc, jnp.max(r_run)

        init = (i, jnp.zeros((R, 1), F32), jnp.zeros((R, P * hd), F32), jnp.float32(0.0))
        fin = lax.while_loop(cond, step, init)
        o_ref[...] = unstack(fin[2])

    res = _call(body, name="sb_fwd", grid=(SB_HEADS // P, nq), in_specs=[q_spec, k_spec, v_spec],
                out_specs=o_spec, out_shape=jax.ShapeDtypeStruct((S, W_B), F32), args=(proj, proj, proj),
                sem=("parallel", "parallel"), comm=comm)
    return res[0], list(res[1:])


def _sb_bwd(proj, do, o, comm=()):
    S = proj.shape[0]
    B, Q, hd = SB_BLOCK, SB_QROWS, HEAD_DIM
    nq = S // Q
    R = SB_PAIR * Q
    q_spec, k_spec, v_spec, o_spec, full = _sb_specs(S)

    def body(q_ref, k_ref, v_ref, do_ref, o_ref, dq_ref, dk_ref, dv_ref):
        i = pl.program_id(1)

        @pl.when(i == 0)
        def _():
            dk_ref[...] = jnp.zeros_like(dk_ref)
            dv_ref[...] = jnp.zeros_like(dv_ref)

        ahead, tr, tc = _sb_iotas(i)
        tri_incl = (tr >= tc).astype(BF)
        tri_strict = (tr > tc).astype(BF)
        stack, unstack = _sb_stack()
        qs = stack(q_ref[...] * ATT_SCALE)
        dobs = stack(do_ref[...])
        o_all = o_ref[...]
        dsum = jnp.sum(dobs.astype(F32) * jnp.concatenate([o_all] * SB_PAIR, axis=0), axis=1, keepdims=True)

        def cond(c):
            return jnp.logical_and(c[0] >= 0, c[-1] > SB_EXIT)

        def step(c):
            kb, r_run, g_run, dq, _ = c
            off = pl.multiple_of(kb * B, B)
            causal = ahead < -kb * B
            kk = k_ref[pl.ds(off, B), :]
            vv = v_ref[pl.ds(off, B), :]
            z, e, cin, a = _sb_block(qs, kk, causal, r_run, tri_incl)
            a16 = a.astype(BF)
            gmat = a16.astype(F32) * _dot(dobs, vv, _NT)
            later = _split_dot(gmat, tri_strict)
            pfx = dsum - g_run - later
            sig = jnp.where(z >= 0, 1.0, e) / (1.0 + e)
            dz = jnp.where(causal, gmat - sig * pfx, 0.0).astype(BF)
            dq = dq + _dot(dz, kk)
            dk_ref[pl.ds(off, B), :] += _dot(dz, qs, _TN)
            dv_ref[pl.ds(off, B), :] += _dot(a16, dobs, _TN)
            g_run = g_run + jnp.sum(gmat, axis=1, keepdims=True)
            r_run = r_run + cin[:, 0:1]
            return kb - 1, r_run, g_run, dq, jnp.max(r_run)

        init = (i, jnp.zeros((R, 1), F32), jnp.zeros((R, 1), F32), jnp.zeros((R, SB_PAIR * hd), F32),
                jnp.float32(0.0))
        fin = lax.while_loop(cond, step, init)
        dq_ref[...] = unstack(fin[3]) * ATT_SCALE

    out = jax.ShapeDtypeStruct((S, W_B), F32)
    res = _call(body, name="sb_bwd", grid=(SB_HEADS // SB_PAIR, nq), in_specs=[q_spec, k_spec, v_spec, o_spec, o_spec],
                out_specs=(o_spec, full, full), out_shape=(out, out, out), args=(proj, proj, proj, do, o), comm=comm)
    return res[0], res[1], res[2], list(res[3:])


def _merge_wo(o_g, l_g, o_b, proj, b_gate, w_br, w_o, x, name, comm=()):
    S = o_b.shape[0]
    D = D_MODEL
    tm = _pick(S, 256)
    gcol = GATE_OFF // D

    def body(o0, o1, o2, l0, l1, l2, ob_ref, ga_ref, gb_ref, bg_ref, w_ref, wo_ref, x_ref,
             x1_ref, mg_ref, oa_ref, lse_ref):
        la, lb, lc = l0[...], l1[...], l2[...]
        mx = jnp.maximum(jnp.maximum(la, lb), lc)
        ea, eb, ec = jnp.exp(la - mx), jnp.exp(lb - mx), jnp.exp(lc - mx)
        den = ea + eb + ec
        oa = (ea * o0[...] + eb * o1[...] + ec * o2[...]) / den
        oa_ref[...] = oa
        lse_ref[...] = mx + jnp.log(den)
        oa16 = oa.astype(BF)
        ob16 = ob_ref[...].astype(BF)
        acc = x_ref[...]
        for c0 in range(0, D, FFN_CHUNK):
            cs = slice(c0, c0 + FFN_CHUNK)
            ya = _dot(oa16, w_ref[0:OUT_A, cs])
            yb = _dot(ob16, w_ref[OUT_A:OUT_A + W_B, cs])
            g_a = jax.nn.sigmoid(ga_ref[:, cs].astype(F32) + bg_ref[:, cs])
            g_b = jax.nn.sigmoid(gb_ref[:, cs].astype(F32) + bg_ref[:, D + c0:D + c0 + FFN_CHUNK])
            mg = (g_a * ya + g_b * yb).astype(BF)
            mg_ref[:, cs] = mg
            acc = acc + _dot(mg, wo_ref[cs, :])
        x1_ref[...] = acc

    nar = pl.BlockSpec((tm, OUT_A), lambda i: (i, 0))
    wide = pl.BlockSpec((tm, D), lambda i: (i, 0))
    res = _call(body, name=name, grid=(S // tm,),
                in_specs=[nar] * 7 + [pl.BlockSpec((tm, D), lambda i: (i, gcol)),
                                      pl.BlockSpec((tm, D), lambda i: (i, gcol + 1)),
                                      pl.BlockSpec((1, 2 * D), lambda i: (0, 0)),
                                      pl.BlockSpec((OUT_A + W_B, D), lambda i: (0, 0)),
                                      pl.BlockSpec((D, D), lambda i: (0, 0)), wide],
                out_specs=(wide, wide, nar, nar),
                out_shape=(jax.ShapeDtypeStruct((S, D), F32), jax.ShapeDtypeStruct((S, D), BF),
                           jax.ShapeDtypeStruct((S, OUT_A), F32), jax.ShapeDtypeStruct((S, OUT_A), F32)),
                args=(*o_g, *l_g, o_b, proj, proj, b_gate.reshape(1, 2 * D), w_br, w_o, x), comm=comm)
    return res[0], res[1], res[2], res[3], list(res[4:])


STAT_OFF = HEAD_DIM // 2


def _merge_bwd(dx, w_o, oa, ob, lse, proj, b_gate, w_br):
    S = ob.shape[0]
    D = D_MODEL
    tm = _pick(S, 256)
    gcol = GATE_OFF // D

    def body(dx_ref, wo_ref, oa_ref, ob_ref, l_ref, ga_ref, gb_ref, bg_ref, w_ref,
             dya_ref, dyb_ref, doa_ref, dob_ref, st_ref, dg_ref, dbg_ref):
        @pl.when(pl.program_id(0) == 0)
        def _():
            dbg_ref[...] = jnp.zeros_like(dbg_ref)

        dx16 = dx_ref[...].astype(BF)
        oa = oa_ref[...]
        oa16 = oa.astype(BF)
        ob16 = ob_ref[...].astype(BF)
        doa = jnp.zeros((tm, OUT_A), F32)
        dob = jnp.zeros((tm, W_B), F32)
        for c0 in range(0, D, FFN_CHUNK):
            cs = slice(c0, c0 + FFN_CHUNK)
            cs2 = slice(D + c0, D + c0 + FFN_CHUNK)
            wa = w_ref[0:OUT_A, cs]
            wb = w_ref[OUT_A:OUT_A + W_B, cs]
            dm = _dot(dx16, wo_ref[cs, :], _NT)
            ya = _dot(oa16, wa)
            yb = _dot(ob16, wb)
            g_a = jax.nn.sigmoid(ga_ref[:, cs].astype(F32) + bg_ref[:, cs])
            g_b = jax.nn.sigmoid(gb_ref[:, cs].astype(F32) + bg_ref[:, cs2])
            dga = dm * ya * g_a * (1.0 - g_a)
            dgb = dm * yb * g_b * (1.0 - g_b)
            dg_ref[:, cs] = dga.astype(BF)
            dg_ref[:, cs2] = dgb.astype(BF)
            dbg_ref[:, cs] += jnp.sum(dga, axis=0, keepdims=True)
            dbg_ref[:, cs2] += jnp.sum(dgb, axis=0, keepdims=True)
            dya = (dm * g_a).astype(BF)
            dyb = (dm * g_b).astype(BF)
            dya_ref[:, cs] = dya
            dyb_ref[:, cs] = dyb
            doa = doa + _dot(dya, wa, _NT)
            dob = dob + _dot(dyb, wb, _NT)
        doa = doa.astype(BF)
        doa_ref[...] = doa
        dob_ref[...] = dob.astype(BF)
        r = lax.broadcasted_iota(jnp.int32, (OUT_A, OUT_A), 0) // HEAD_DIM
        c = lax.broadcasted_iota(jnp.int32, (OUT_A, OUT_A), 1) // HEAD_DIM
        dsum = _split_dot(doa.astype(F32) * oa, (r == c).astype(BF))
        lane = lax.broadcasted_iota(jnp.int32, dsum.shape, 1) % HEAD_DIM
        st_ref[...] = jnp.where(lane < STAT_OFF, l_ref[...], dsum)

    nar = pl.BlockSpec((tm, OUT_A), lambda i: (i, 0))
    wide = pl.BlockSpec((tm, D), lambda i: (i, 0))
    wide2 = pl.BlockSpec((tm, 2 * D), lambda i: (i, 0))
    vec2 = pl.BlockSpec((1, 2 * D), lambda i: (0, 0))
    return pl.pallas_call(
        body, name="merge_bwd",
        out_shape=(jax.ShapeDtypeStruct((S, D), BF), jax.ShapeDtypeStruct((S, D), BF),
                   jax.ShapeDtypeStruct((S, OUT_A), BF), jax.ShapeDtypeStruct((S, W_B), BF),
                   jax.ShapeDtypeStruct((S, OUT_A), F32), jax.ShapeDtypeStruct((S, 2 * D), BF),
                   jax.ShapeDtypeStruct((1, 2 * D), F32)),
        grid=(S // tm,),
        in_specs=[wide, pl.BlockSpec((D, D), lambda i: (0, 0)), nar, nar, nar,
                  pl.BlockSpec((tm, D), lambda i: (i, gcol)), pl.BlockSpec((tm, D), lambda i: (i, gcol + 1)), vec2,
                  pl.BlockSpec((OUT_A + W_B, D), lambda i: (0, 0))],
        out_specs=(wide, wide, nar, nar, nar, wide2, vec2),
        compiler_params=_params("arbitrary"),
    )(dx, w_o, oa, ob, lse, proj, proj, b_gate.reshape(1, 2 * D), w_br)


_SQRT_HALF = 0.7071067811865476
_INV_SQRT_2PI = 0.3989422804014327


def _gelu_parts(a):
    cdf = 0.5 * (1.0 + lax.erf(a * _SQRT_HALF))
    pdf = _INV_SQRT_2PI * jnp.exp(-0.5 * a * a)
    return cdf, pdf


def _shift_down(a, halo, k):
    rows = lax.broadcasted_iota(jnp.int32, a.shape, 0)
    out = pltpu.roll(a, k, 0)
    for r in range(k):
        out = jnp.where(rows == r, halo[8 - k + r:8 - k + r + 1, :], out)
    return out


def _ffn_specs(S, tm):
    F = D_FF
    t8 = tm // 8
    a_spec = pl.BlockSpec((tm, F), lambda i: (i, 0))
    v_spec = pl.BlockSpec((tm, F), lambda i: (i, 1))
    halo_prev = pl.BlockSpec((8, F), lambda i: (jnp.maximum(i * t8 - 1, 0), 0))
    return a_spec, v_spec, halo_prev


def _ffn_down(up, conv_w, conv_b, w_down, res, *, name, comm=()):
    S = up.shape[0]
    F = D_FF
    D = w_down.shape[1]
    tm = _pick(S, 256)
    a_spec, v_spec, halo_prev = _ffn_specs(S, tm)

    def body(a_ref, h_ref, v_ref, w_ref, b_ref, wd_ref, r_ref, o_ref, act_ref, ac_ref):
        first = pl.program_id(0) == 0
        acc = r_ref[...]
        for c0 in range(0, F, FFN_CHUNK):
            cs = slice(c0, c0 + FFN_CHUNK)
            a = a_ref[:, cs].astype(F32)
            halo = jnp.where(first, 0.0, h_ref[:, cs].astype(F32))
            w = w_ref[:, cs]
            ac = b_ref[:, cs] + w[0:1, :] * _shift_down(a, halo, 2) + w[1:2, :] * _shift_down(a, halo, 1) + w[2:3, :] * a
            ac_ref[:, cs] = ac.astype(BF)
            cdf, _ = _gelu_parts(ac)
            act = (ac * cdf * v_ref[:, cs].astype(F32)).astype(BF)
            act_ref[:, cs] = act
            acc = acc + _dot(act, wd_ref[cs, :])
        o_ref[...] = acc

    row = pl.BlockSpec((tm, D), lambda i: (i, 0))
    res_ = _call(body, name=name, grid=(S // tm,),
                 in_specs=[a_spec, halo_prev, v_spec, pl.BlockSpec((3, F), lambda i: (0, 0)),
                           pl.BlockSpec((1, F), lambda i: (0, 0)), pl.BlockSpec((F, D), lambda i: (0, 0)), row],
                 out_specs=(row, a_spec, a_spec),
                 out_shape=(jax.ShapeDtypeStruct((S, D), F32), jax.ShapeDtypeStruct((S, F), BF),
                            jax.ShapeDtypeStruct((S, F), BF)),
                 args=(up, up, up, conv_w, conv_b.reshape(1, F), w_down, res), comm=comm)
    return res_[0], res_[1], res_[2], list(res_[3:])


def _shift_up_pair(a, nxt):
    n = a.shape[0]
    r8 = lax.broadcasted_iota(jnp.int32, (8,) + a.shape[1:], 0)
    out = []
    for k in (1, 2):
        rolled = pltpu.roll(a, n - k, 0)
        tail = jnp.where(r8 >= 8 - k, pltpu.roll(nxt, 8 - k, 0), rolled[n - 8:n])
        out.append(jnp.concatenate([rolled[0:n - 8], tail], axis=0))
    return out


def _ffn_bwd(dx, w_down, up, ac, conv_w, comm=()):
    S = up.shape[0]
    F = D_FF
    D = dx.shape[1]
    tm = _pick(S, 256)
    t8 = tm // 8
    nt = S // tm
    a_spec, v_spec, _ = _ffn_specs(S, tm)

    def nxt(width, col):
        return pl.BlockSpec((8, width), lambda i: (jnp.minimum((i + 1) * t8, S // 8 - 1), col))

    def body(dx_ref, dxn_ref, wd_ref, ac_ref, acn_ref, a_ref, v_ref, vn_ref, w_ref, dup_ref, dw_ref, db_ref):
        i = pl.program_id(0)

        @pl.when(i == 0)
        def _():
            dw_ref[...] = jnp.zeros_like(dw_ref)
            db_ref[...] = jnp.zeros_like(db_ref)

        dx16 = dx_ref[...].astype(BF)
        dxn16 = dxn_ref[...].astype(BF)
        last = i == nt - 1

        def dconv(dact, ac, v):
            cdf, pdf = _gelu_parts(ac)
            return cdf, dact * v * (cdf + ac * pdf)

        for c0 in range(0, F, FFN_CHUNK):
            cs = slice(c0, c0 + FFN_CHUNK)
            wd = wd_ref[cs, :]
            dact = _dot(dx16, wd, _NT)
            ac = ac_ref[:, cs].astype(F32)
            cdf, dac = dconv(dact, ac, v_ref[:, cs].astype(F32))
            dup_ref[:, F + c0:F + c0 + FFN_CHUNK] = (dact * ac * cdf).astype(BF)
            _, dac_n = dconv(_dot(dxn16, wd, _NT), acn_ref[:, cs].astype(F32), vn_ref[:, cs].astype(F32))
            d1, d2 = _shift_up_pair(dac, jnp.where(last, 0.0, dac_n))
            w = w_ref[:, cs]
            dup_ref[:, cs] = (w[2:3, :] * dac + w[1:2, :] * d1 + w[0:1, :] * d2).astype(BF)
            a = a_ref[:, cs].astype(F32)
            db_ref[:, cs] += jnp.sum(dac, axis=0, keepdims=True)
            dw_ref[0:1, cs] += jnp.sum(d2 * a, axis=0, keepdims=True)
            dw_ref[1:2, cs] += jnp.sum(d1 * a, axis=0, keepdims=True)
            dw_ref[2:3, cs] += jnp.sum(dac * a, axis=0, keepdims=True)

    w_spec = pl.BlockSpec((3, F), lambda i: (0, 0))
    b_spec = pl.BlockSpec((1, F), lambda i: (0, 0))
    res = _call(body, name="ffn_bwd", grid=(nt,),
                in_specs=[pl.BlockSpec((tm, D), lambda i: (i, 0)), nxt(D, 0), pl.BlockSpec((F, D), lambda i: (0, 0)),
                          a_spec, nxt(F, 0), a_spec, v_spec, nxt(F, 1), w_spec],
                out_specs=(pl.BlockSpec((tm, 2 * F), lambda i: (i, 0)), w_spec, b_spec),
                out_shape=(jax.ShapeDtypeStruct((S, 2 * F), BF), jax.ShapeDtypeStruct((3, F), F32),
                           jax.ShapeDtypeStruct((1, F), F32)),
                args=(dx, dx, w_down, ac, ac, up, up, up, conv_w), comm=comm)
    return res[0], res[1], res[2], list(res[3:])


def _adamw(parts, w, m, v, name, row0=0, prev=None):
    R, C = w.shape
    Rp = parts.shape[1]
    tr = Rp
    for cand in (512, 256, 128, 64, 32, 16):
        if Rp % cand == 0 and row0 % cand == 0 and cand * C * 4 <= (1 << 21):
            tr = cand
            break
    b0 = row0 // tr
    c1 = 1.0 / (1.0 - ADAM_B1 ** ADAM_STEP)
    c2 = 1.0 / (1.0 - ADAM_B2 ** ADAM_STEP)

    def body(p_ref, w_ref, m_ref, v_ref, *rest):
        g_ref, d_ref, nm_ref, nv_ref = rest[-4:]
        g = p_ref[0].astype(F32)
        for j in range(1, N_DEV):
            g = g + p_ref[j].astype(F32)
        nm = ADAM_B1 * m_ref[...] + (1.0 - ADAM_B1) * g
        nv = ADAM_B2 * v_ref[...] + (1.0 - ADAM_B2) * (g * g)
        g_ref[...] = g
        nm_ref[...] = nm
        nv_ref[...] = nv
        d_ref[...] = -ADAM_LR * ((nm * c1) / (jnp.sqrt(nv * c2) + ADAM_EPS) + ADAM_WD * w_ref[...])

    blk = pl.BlockSpec((tr, C), lambda i: (b0 + i, 0))
    out = jax.ShapeDtypeStruct((R, C), F32)
    carried = [] if prev is None else list(prev)
    return pl.pallas_call(
        body, name=name,
        out_shape=(out, out, out, out),
        grid=(Rp // tr,),
        in_specs=[pl.BlockSpec((N_DEV, tr, C), lambda i: (0, i, 0)), blk, blk, blk]
        + [pl.BlockSpec(memory_space=pl.ANY)] * len(carried),
        out_specs=(blk, blk, blk, blk),
        input_output_aliases={4 + k: k for k in range(len(carried))},
        compiler_params=_params("parallel"),
    )(parts, w, m, v, *carried)


def _dil(t, dil):
    S, C = t.shape
    if dil == 1:
        return t.reshape(1, S, C)
    return t.reshape(S // dil, dil, C).transpose(1, 0, 2)


def _undil(t):
    d, L, C = t.shape
    if d == 1:
        return t.reshape(L, C)
    return t.transpose(1, 0, 2).reshape(L * d, C)


def _group_qkv(proj, g):
    dil = DSW_GROUPS[g][1]
    if dil == 1:
        p3 = _dil(proj, 1)
        return (p3, p3, p3), (g, W_A // OUT_A + g, 2 * W_A // OUT_A + g)
    c0 = g * OUT_A
    return tuple(_dil(proj[:, o + c0:o + c0 + OUT_A], dil) for o in (0, W_A, 2 * W_A)), (0, 0, 0)


_COL_SHARDED = ("w_in", "w_br", "w_up")
_ROW_SHARDED = ("w_o", "w_down")


class _Plan:
    def __init__(self):
        self.riders = {}
        self.landed = {}

    def ride(self, slot, key, kind, x):
        self.riders.setdefault(slot, []).append((key, kind, x))

    def run(self, slot, fn, *args, **kw):
        items = self.riders.pop(slot, [])
        res = fn(*args, comm=[(kind, x) for _, kind, x in items], **kw)
        for (key, _, _), r in zip(items, res[-1]):
            self.landed[key] = r
        return res[0] if len(res) == 2 else res[:-1]

    def weight(self, n, l):
        g = self.landed[(n, l)]
        if n in _COL_SHARDED:
            return g.transpose(1, 0, 2).reshape(g.shape[1], -1)
        return g.reshape(-1, g.shape[2])

    def scatter(self, slot, n, l, full, part=0):
        K, N = full.shape
        if n in _COL_SHARDED:
            blocks = full.reshape(K, N_DEV, N // N_DEV).transpose(1, 0, 2)
        else:
            blocks = full.reshape(N_DEV, K // N_DEV, N)
        self.ride(slot, ("d" + n, l, part), "scatter", blocks)


def _layer_fwd(x, p, plan, l):
    proj, h = plan.run(f"proj_{l}", _norm_matmul, x, p["norm1"][l], plan.weight("w_in", l), name=f"proj_{l}",
                       tm=512, tn=1024)
    o_g, l_g, qkv_g = [], [], []
    for g in range(N_GROUPS):
        qkv, cols = _group_qkv(proj, g)
        og, lg = plan.run(f"attn_a_fwd_g{g}_{l}", _attn_a_fwd, qkv, cols, g)
        o_g.append(_undil(og))
        l_g.append(_undil(lg))
        qkv_g.append((qkv, cols))
    ob = plan.run(f"sb_fwd_{l}", _sb_fwd, proj)
    x1, merged, oa, lse = plan.run(f"wo_{l}", _merge_wo, o_g, l_g, ob, proj, p["b_gate"][l], plan.weight("w_br", l),
                                   plan.weight("w_o", l), x, f"wo_{l}")
    up, h2 = plan.run(f"up_{l}", _norm_matmul, x1, p["norm2"][l], plan.weight("w_up", l), name=f"up_{l}",
                      tm=512, tn=1408)
    x2, act, ac = plan.run(f"down_{l}", _ffn_down, up, p["conv_w"][l], p["conv_b"][l], plan.weight("w_down", l), x1,
                           name=f"down_{l}")
    saved = dict(x=x, h=h, proj=proj, qkv_g=qkv_g, oa=oa, ob=ob, lse=lse, merged=merged, x1=x1, h2=h2, up=up, act=act, ac=ac)
    return x2, saved


def _layer_bwd(dx2, sv, p, plan, l):
    gr = {}
    dwd = plan.run(f"dw_down_{l}", _matmul, sv["act"], dx2, mode="tn", out_dtype=BF, name=f"dw_down_{l}",
                   tm=1408, tn=1024, tk=2048)
    plan.scatter(f"d_h2_{l}", "w_down", l, dwd)
    dup, gr["conv_w"], dcb = plan.run(f"ffn_bwd_{l}", _ffn_bwd, dx2, plan.weight("w_down", l), sv["up"], sv["ac"],
                                      p["conv_w"][l])
    gr["conv_b"] = dcb[0]
    dx1, dn2 = plan.run(f"d_h2_{l}", _matmul_rms_bwd, dup, plan.weight("w_up", l), sv["x1"], p["norm2"][l], dx2,
                        name=f"d_h2_{l}")
    dwu = plan.run(f"dw_up_{l}", _matmul, sv["h2"], dup, mode="tn", out_dtype=BF, name=f"dw_up_{l}",
                   tm=1024, tn=1408, tk=2048)
    plan.scatter(f"sb_bwd_{l}", "w_up", l, dwu)
    gr["norm2"] = dn2[0]
    dwo = plan.run(f"dw_o_{l}", _matmul, sv["merged"], dx1, mode="tn", out_dtype=BF, name=f"dw_o_{l}",
                   tm=1024, tn=1024, tk=2048)
    plan.scatter(f"dw_in_{l}", "w_o", l, dwo)
    dya, dyb, doa, dob, stats, dgate, dbg = _merge_bwd(dx1, plan.weight("w_o", l), sv["oa"], sv["ob"], sv["lse"],
                                                       sv["proj"], p["b_gate"][l], plan.weight("w_br", l))
    gr["b_gate"] = dbg[0]
    dwa = plan.run(f"dw_bra_{l}", _matmul, sv["oa"], dya, mode="tn", out_dtype=BF, name=f"dw_bra_{l}",
                   tm=256, tn=1024, tk=2048)
    dwb = plan.run(f"dw_brb_{l}", _matmul, sv["ob"], dyb, mode="tn", out_dtype=BF, name=f"dw_brb_{l}",
                   tm=256, tn=1024, tk=2048)
    plan.scatter(f"dw_in_{l}", "w_br", l, jnp.concatenate([dwa, dwb], axis=0))
    proj = sv["proj"]
    dq_a, dk_a, dv_a = [], [], []
    for g, (_, dil) in enumerate(DSW_GROUPS):
        qkv, cols = sv["qkv_g"][g]
        dqg, dkg, dvg = _attn_a_bwd(qkv, cols, _dil(doa, dil), _dil(stats, dil), g)
        dq_a.append(_undil(dqg))
        dk_a.append(_undil(dkg))
        dv_a.append(_undil(dvg))
    dqb, dkb, dvb = plan.run(f"sb_bwd_{l}", _sb_bwd, proj, dob, sv["ob"])
    dproj = jnp.concatenate(dq_a + dk_a + dv_a + [dqb.astype(BF), dkb.astype(BF), dvb.astype(BF), dgate], axis=1)
    dx, dn1 = plan.run(f"d_h_{l}", _matmul_rms_bwd, dproj, plan.weight("w_in", l), sv["x"], p["norm1"][l], dx1,
                       name=f"d_h_{l}")
    if l > 0:
        dwi = plan.run(f"dw_in_{l}", _matmul, sv["h"], dproj, mode="tn", out_dtype=BF, name=f"dw_in_{l}",
                       tm=1024, tn=1280, tk=2048)
        plan.scatter(f"ffn_bwd_{l - 1}", "w_in", l, dwi)
    else:
        half = D_MODEL // 2
        for part in range(2):
            name = f"dw_in_{l}" if part == 0 else f"dw_in_{l}_rest"
            dwi = plan.run(name, _matmul, sv["h"], dproj, mode="tn", out_dtype=BF, name=name, tm=half, tn=1280,
                           tk=2048, m_cols=half, m_off=part * half)
            plan.scatter(f"dw_in_{l}_rest" if part == 0 else "alone", "w_in", l, dwi, part)
    gr["norm1"] = dn1[0]
    return dx, gr


def kernel(x, norm1, w_in, b_gate, w_br, w_o, norm2, w_up, conv_w, conv_b, w_down, norm_f, loss_target, m_norm1, m_w_in, m_b_gate, m_w_br, m_w_o, m_norm2, m_w_up, m_conv_w, m_conv_b, m_w_down, m_norm_f, v_norm1, v_w_in, v_b_gate, v_w_br, v_w_o, v_norm2, v_w_up, v_conv_w, v_conv_b, v_w_down, v_norm_f):
    depth = norm1.shape[0]
    me = 4 * lax.axis_index("x") + 2 * lax.axis_index("y") + lax.axis_index("c")
    shards = dict(w_in=w_in, w_br=w_br, w_o=w_o, w_up=w_up, w_down=w_down)
    moments_m = dict(norm1=m_norm1, w_in=m_w_in, b_gate=m_b_gate, w_br=m_w_br, w_o=m_w_o, norm2=m_norm2,
                     w_up=m_w_up, conv_w=m_conv_w, conv_b=m_conv_b, w_down=m_w_down, norm_f=m_norm_f)
    moments_v = dict(norm1=v_norm1, w_in=v_w_in, b_gate=v_b_gate, w_br=v_w_br, w_o=v_w_o, norm2=v_norm2,
                     w_up=v_w_up, conv_w=v_conv_w, conv_b=v_conv_b, w_down=v_w_down, norm_f=v_norm_f)

    plan = _Plan()
    wb = {n: s.astype(BF) for n, s in shards.items()}
    p = dict(norm1=norm1, b_gate=b_gate, norm2=norm2, conv_b=conv_b)
    cw = _all_gather(conv_w, "gather_conv_w")
    p["conv_w"] = cw.transpose(1, 2, 0, 3).reshape(depth, 3, D_FF)
    plan.landed[("w_in", 0)] = _all_gather_via_sibling(wb["w_in"][0], "gather_w_in_0")
    for l in range(depth):
        plan.ride(f"proj_{l}", ("w_down", l), "gather", wb["w_down"][l])
        plan.ride(f"attn_a_fwd_g0_{l}" if l == 0 else f"down_{l - 1}", ("w_br", l), "gather", wb["w_br"][l])
        plan.ride(f"attn_a_fwd_g0_{l}" if l == 0 else f"down_{l - 1}", ("w_o", l), "gather", wb["w_o"][l])
        plan.ride(f"sb_fwd_{l}", ("w_up", l), "gather", wb["w_up"][l])
        if l + 1 < depth:
            plan.ride(f"up_{l}", ("w_in", l + 1), "gather", wb["w_in"][l + 1])

    xs = x[0]
    saved = []
    for l in range(depth):
        xs, sv = _layer_fwd(xs, p, plan, l)
        saved.append(sv)
    loss_part, dx, dnf = _loss_head(xs, norm_f, loss_target[0])
    loss = lax.psum(loss_part[0, 0], ("x", "y", "c"))

    grads = [None] * depth
    for l in reversed(range(depth)):
        dx, grads[l] = _layer_bwd(dx, saved[l], p, plan, l)
    grad_x = dx[None]
    (key, _, last), = plan.riders.pop("alone")
    plan.landed[key] = _all_to_all(last, "scatter_w_in_rest")
    assert not plan.riders, sorted(plan.riders)

    out_g, out_d, out_m, out_v = {}, {}, {}, {}
    for n in _COL_SHARDED + _ROW_SHARDED:
        shp = shards[n].shape
        flat = (shp[0] * shp[1], shp[2])
        res = None
        for l in range(depth):
            row = l * shp[1]
            for key in sorted(k for k in plan.landed if k[:2] == ("d" + n, l)):
                parts = plan.landed[key]
                res = _adamw(parts, shards[n].reshape(flat), moments_m[n].reshape(flat), moments_v[n].reshape(flat),
                             f"adamw_{n}_{l}_{key[2]}", row0=row, prev=res)
                row += parts.shape[1]
        out_g[n], out_d[n], out_m[n], out_v[n] = [r.reshape(shp) for r in res]

    small = ("norm1", "b_gate", "norm2", "conv_b")
    vecs = [jnp.stack([grads[l][n] for l in range(depth)]).reshape(-1) for n in small]
    vecs.append(dnf.reshape(-1))
    vecs.append(jnp.stack([grads[l]["conv_w"] for l in range(depth)]).reshape(-1))
    sizes = [v.shape[0] for v in vecs]
    flat = jnp.concatenate(vecs)
    n_small = sum(sizes[:-1])
    pad = (-flat.shape[0]) % 1024
    flat = jnp.pad(flat, (0, pad)).reshape(-1, 128)
    allp = _all_gather(flat, "gather_small_grads").reshape(N_DEV, -1)
    rep_w = jnp.concatenate([norm1.reshape(-1), b_gate.reshape(-1), norm2.reshape(-1), conv_b.reshape(-1), norm_f])
    rep_m = jnp.concatenate([moments_m[n].reshape(-1) for n in small] + [m_norm_f])
    rep_v = jnp.concatenate([moments_v[n].reshape(-1) for n in small] + [v_norm_f])
    rows = n_small // 128
    res = _adamw(allp[:, :n_small].reshape(N_DEV, rows, 128), rep_w.reshape(rows, 128), rep_m.reshape(rows, 128),
                 rep_v.reshape(rows, 128), "adamw_small")
    off = 0
    for n, sz in zip(small + ("norm_f",), sizes[:-1]):
        shp = norm_f.shape if n == "norm_f" else p[n].shape
        out_g[n], out_d[n], out_m[n], out_v[n] = [r.reshape(-1)[off:off + sz].reshape(shp) for r in res]
        off += sz
    f = conv_w.shape[2]
    cwp = allp[:, n_small:n_small + sizes[-1]].reshape(N_DEV, depth * 3, D_FF)
    cwp = lax.dynamic_slice_in_dim(cwp, me * f, f, axis=2)
    res = _adamw(cwp, conv_w.reshape(depth * 3, f), m_conv_w.reshape(depth * 3, f), v_conv_w.reshape(depth * 3, f),
                 "adamw_conv_w")
    out_g["conv_w"], out_d["conv_w"], out_m["conv_w"], out_v["conv_w"] = [r.reshape(conv_w.shape) for r in res]

    order = ("norm1", "w_in", "b_gate", "w_br", "w_o", "norm2", "w_up", "conv_w", "conv_b", "w_down", "norm_f")
    return (loss, grad_x, *[out_g[n] for n in order], *[out_d[n] for n in order],
            *[out_m[n] for n in order], *[out_v[n] for n in order])
```

```python
import functools

import jax
import jax.numpy as jnp
from jax import lax
from jax.experimental import pallas as pl
from jax.experimental.pallas import tpu as pltpu

BF = jnp.bfloat16
F32 = jnp.float32

N_DEV = 8
D_MODEL = 1024
HEAD_DIM = 64
DSW_GROUPS = ((128, 1), (512, 4), (2048, 16))
HEADS_PER_GROUP = 4
N_GROUPS = len(DSW_GROUPS)
DSW_HEADS = HEADS_PER_GROUP * N_GROUPS
SB_HEADS = 4
W_A = DSW_HEADS * HEAD_DIM
W_B = SB_HEADS * HEAD_DIM
OUT_A = HEADS_PER_GROUP * HEAD_DIM
N_IN = 3 * W_A + 3 * W_B + 2 * D_MODEL
GATE_OFF = 3 * W_A + 3 * W_B
D_FF = 2816
SB_BLOCK = 256
RMS_EPS = 1e-6
ATT_SCALE = HEAD_DIM ** -0.5
NEG = -1e30
SB_EXIT = -110.0
FFN_CHUNK = 256
FFN_BWD_CHUNK = 128

ADAM_LR = 0.001
ADAM_B1 = 0.9
ADAM_B2 = 0.999
ADAM_EPS = 1e-08
ADAM_WD = 0.01
ADAM_STEP = 10

HBM_SPEC = pl.BlockSpec(memory_space=pltpu.HBM)
MESH = pl.DeviceIdType.MESH

_NN = (((1,), (0,)), ((), ()))
_NT = (((1,), (1,)), ((), ()))
_TN = (((0,), (0,)), ((), ()))


def _dot(a, b, dn=_NN):
    return lax.dot_general(a, b, dn, preferred_element_type=F32)


def _pick(dim, pref):
    if dim <= pref:
        return dim
    t = (pref // 128) * 128
    while t >= 128:
        if dim % t == 0:
            return t
        t -= 128
    return dim


def _params(*sem):
    return pltpu.CompilerParams(dimension_semantics=sem)


def _peer(k):
    x, y, c = lax.axis_index("x"), lax.axis_index("y"), lax.axis_index("c")
    px = 1 - x if (k >> 2) & 1 else x
    py = 1 - y if (k >> 1) & 1 else y
    pc = 1 - c if k & 1 else c
    return (px, py, pc), 4 * px + 2 * py + pc


def _exchange(kind, x_ref, out_ref, send_sems, recv_sems, local_sem):
    gather = kind == "gather"
    _, me = _peer(0)

    def src(idx):
        return x_ref if gather else x_ref.at[idx]

    def copy(k, dst_idx):
        peer, pidx = _peer(k)
        return pltpu.make_async_remote_copy(
            src_ref=src(pidx), dst_ref=out_ref.at[dst_idx], send_sem=send_sems.at[k - 1],
            recv_sem=recv_sems.at[k - 1], device_id=peer, device_id_type=MESH)

    mine = pltpu.make_async_copy(src(me), out_ref.at[me], local_sem)

    def start():
        mine.start()
        for k in range(1, N_DEV):
            copy(k, me).start()

    def wait():
        for k in range(1, N_DEV):
            copy(k, _peer(k)[1]).wait_recv()
        for k in range(1, N_DEV):
            copy(k, me).wait_send()
        mine.wait()

    return start, wait


_EXCHANGE_SEMS = [pltpu.SemaphoreType.DMA((N_DEV - 1,)), pltpu.SemaphoreType.DMA((N_DEV - 1,)),
                  pltpu.SemaphoreType.DMA]


def _exchange_shape(kind, x):
    return jax.ShapeDtypeStruct(((N_DEV,) + x.shape) if kind == "gather" else x.shape, x.dtype)


def _exchange_alone(kind, x, name):
    def body(x_ref, out_ref, send_sems, recv_sems, local_sem):
        start, wait = _exchange(kind, x_ref, out_ref, send_sems, recv_sems, local_sem)
        start()
        wait()

    return pl.pallas_call(
        body, name=name, out_shape=_exchange_shape(kind, x),
        in_specs=[HBM_SPEC], out_specs=HBM_SPEC, scratch_shapes=list(_EXCHANGE_SEMS),
    )(x)


def _all_gather_via_sibling(x, name):
    def body(x_ref, out_ref, send_sems, recv_sems, local_sem):
        x_, y_, c_ = lax.axis_index("x"), lax.axis_index("y"), lax.axis_index("c")
        me, sibling = (x_, y_, c_), (x_, y_, 1 - c_)
        chips = [(1 - x_, y_), (x_, 1 - y_), (1 - x_, 1 - y_)]

        def slot(px, py, pc):
            return out_ref.at[4 * px + 2 * py + pc]

        def copy(k, block, to, src=None):
            return pltpu.make_async_remote_copy(
                src_ref=slot(*block) if src is None else src, dst_ref=slot(*block), send_sem=send_sems.at[k],
                recv_sem=recv_sems.at[k], device_id=to, device_id_type=MESH)

        mine = pltpu.make_async_copy(x_ref, slot(*me), local_sem)
        mine.start()
        first = [copy(0, me, sibling, src=x_ref)]
        first += [copy(1 + j, me, (*chip, c_), src=x_ref) for j, chip in enumerate(chips)]
        for cp in first:
            cp.start()
        passed = [copy(4 + j, (*chip, c_), sibling) for j, chip in enumerate(chips)]
        for j, chip in enumerate(chips):
            copy(1 + j, (*chip, c_), me).wait_recv()
            passed[j].start()
        copy(0, sibling, me).wait_recv()
        for j, chip in enumerate(chips):
            copy(4 + j, (*chip, 1 - c_), me).wait_recv()
        for cp in first + passed:
            cp.wait_send()
        mine.wait()

    return pl.pallas_call(
        body, name=name, out_shape=_exchange_shape("gather", x),
        in_specs=[HBM_SPEC], out_specs=HBM_SPEC,
        scratch_shapes=[pltpu.SemaphoreType.DMA((N_DEV - 1,)), pltpu.SemaphoreType.DMA((N_DEV - 1,)),
                        pltpu.SemaphoreType.DMA],
    )(x)


def _all_gather(x, name):
    return _exchange_alone("gather", x, name)


def _all_to_all(x, name):
    return _exchange_alone("scatter", x, name)


def _call(body, *, name, grid, in_specs, out_specs, out_shape, args, scratch_shapes=(), sem=None, comm=()):
    single = not isinstance(out_shape, (tuple, list))
    outs = (out_shape,) if single else tuple(out_shape)
    ospecs = (out_specs,) if single else tuple(out_specs)
    if not comm:
        res = pl.pallas_call(
            body, name=name, out_shape=outs, grid=grid, in_specs=list(in_specs), out_specs=ospecs,
            scratch_shapes=list(scratch_shapes), compiler_params=_params(*(sem or ("arbitrary",) * len(grid))),
        )(*args)
        return res
    n_in, n_out, n_scr, nc = len(in_specs), len(outs), len(scratch_shapes), len(comm)

    def wrapped(*refs):
        ins = refs[:n_in]
        cins = refs[n_in:n_in + nc]
        o0 = n_in + nc
        kouts = refs[o0:o0 + n_out]
        couts = refs[o0 + n_out:o0 + n_out + nc]
        s0 = o0 + n_out + nc
        scr = refs[s0:s0 + n_scr]
        sems = refs[s0 + n_scr:]
        ids = [pl.program_id(ax) for ax in range(len(grid))]
        first = functools.reduce(jnp.logical_and, [i == 0 for i in ids])
        last = functools.reduce(jnp.logical_and, [i == g - 1 for i, g in zip(ids, grid)])
        ex = [_exchange(comm[c][0], cins[c], couts[c], *sems[3 * c:3 * c + 3]) for c in range(nc)]

        @pl.when(first)
        def _():
            for start, _ in ex:
                start()

        body(*ins, *kouts, *scr)

        @pl.when(last)
        def _():
            for _, wait in ex:
                wait()

    return pl.pallas_call(
        wrapped, name=name,
        out_shape=outs + tuple(_exchange_shape(k, x) for k, x in comm),
        grid=grid, in_specs=list(in_specs) + [HBM_SPEC] * nc, out_specs=ospecs + (HBM_SPEC,) * nc,
        scratch_shapes=list(scratch_shapes) + list(_EXCHANGE_SEMS) * nc,
        compiler_params=_params(*(("arbitrary",) * len(grid))),
    )(*args, *[x for _, x in comm])


def _matmul(a, b, *, mode, out_dtype, name, tm=512, tn=1024, tk=1024, res=None, comm=(), m_cols=None, m_off=0):
    if mode == "nn":
        (M, K), (_, N) = a.shape, b.shape
    elif mode == "nt":
        (M, K), (N, _) = a.shape, b.shape
    else:
        (K, M), (_, N) = a.shape, b.shape
        M = M if m_cols is None else m_cols
    tm, tn, tk = _pick(M, tm), _pick(N, tn), _pick(K, tk)
    i0 = m_off // tm
    nk = K // tk
    dn = {"nn": _NN, "nt": _NT, "tn": _TN}[mode]

    def body(*refs):
        a_ref, b_ref = refs[0], refs[1]
        r_ref = refs[2] if res is not None else None
        o_ref = refs[3] if res is not None else refs[2]

        def finish(r):
            if res is not None:
                r = r + r_ref[...].astype(F32)
            o_ref[...] = r.astype(out_dtype)

        part = _dot(a_ref[...].astype(BF), b_ref[...].astype(BF), dn)
        if nk == 1:
            finish(part)
            return
        acc = refs[-1]
        k = pl.program_id(2)

        @pl.when(k == 0)
        def _():
            acc[...] = jnp.zeros_like(acc)

        acc[...] += part

        @pl.when(k == nk - 1)
        def _():
            finish(acc[...])

    if mode == "tn":
        a_spec = pl.BlockSpec((tk, tm), lambda j, i, k: (k, i0 + i))
    else:
        a_spec = pl.BlockSpec((tm, tk), lambda j, i, k: (i, k))
    if mode == "nt":
        b_spec = pl.BlockSpec((tn, tk), lambda j, i, k: (j, k))
    else:
        b_spec = pl.BlockSpec((tk, tn), lambda j, i, k: (k, j))
    o_spec = pl.BlockSpec((tm, tn), lambda j, i, k: (i, j))
    in_specs = [a_spec, b_spec] + ([o_spec] if res is not None else [])
    args = (a, b) + ((res,) if res is not None else ())
    out = _call(body, name=name, grid=(N // tn, M // tm, nk), in_specs=in_specs, out_specs=o_spec,
                out_shape=jax.ShapeDtypeStruct((M, N), out_dtype), args=args,
                scratch_shapes=[pltpu.VMEM((tm, tn), F32)] if nk > 1 else [],
                sem=("parallel", "parallel", "arbitrary"), comm=comm)
    return out[0], list(out[1:])


def _norm_matmul(x, g, w, *, name, tm=512, tn=1024, comm=()):
    S, K = x.shape
    N = w.shape[1]
    tm, tn = _pick(S, tm), _pick(N, tn)
    slab = min(tm, 256)

    def body(x_ref, g_ref, w_ref, o_ref, h_ref):
        gg = g_ref[...]
        for r0 in range(0, tm, slab):
            rs = slice(r0, r0 + slab)
            xf = x_ref[rs, :]
            r = lax.rsqrt(jnp.mean(xf * xf, axis=-1, keepdims=True) + RMS_EPS)
            hh = (xf * r * gg).astype(BF)
            h_ref[rs, :] = hh
            for c0 in range(0, N, tn):
                o_ref[rs, c0:c0 + tn] = _dot(hh, w_ref[:, c0:c0 + tn]).astype(BF)

    res = _call(body, name=name, grid=(S // tm,),
                in_specs=[pl.BlockSpec((tm, K), lambda i: (i, 0)), pl.BlockSpec((1, K), lambda i: (0, 0)),
                          pl.BlockSpec((K, N), lambda i: (0, 0))],
                out_specs=(pl.BlockSpec((tm, N), lambda i: (i, 0)), pl.BlockSpec((tm, K), lambda i: (i, 0))),
                out_shape=(jax.ShapeDtypeStruct((S, N), BF), jax.ShapeDtypeStruct((S, K), BF)),
                args=(x, g.reshape(1, K), w), comm=comm)
    return res[0], res[1], list(res[2:])


def _matmul_rms_bwd(dy, w, x, g, dres, *, name, tm=512, comm=()):
    S, K = dy.shape
    D = w.shape[0]
    tm = _pick(S, tm)

    def body(dy_ref, w_ref, x_ref, g_ref, dres_ref, dx_ref, dg_ref):
        @pl.when(pl.program_id(0) == 0)
        def _():
            dg_ref[...] = jnp.zeros_like(dg_ref)

        dh = _dot(dy_ref[...].astype(BF), w_ref[...], _NT)
        xf = x_ref[...]
        r = lax.rsqrt(jnp.mean(xf * xf, axis=-1, keepdims=True) + RMS_EPS)
        xh = xf * r
        dg_ref[...] += jnp.sum(dh * xh, axis=0, keepdims=True)
        dxh = dh * g_ref[...]
        dx_ref[...] = dres_ref[...] + r * (dxh - xh * jnp.mean(dxh * xh, axis=-1, keepdims=True))

    row = pl.BlockSpec((tm, D), lambda i: (i, 0))
    vec = pl.BlockSpec((1, D), lambda i: (0, 0))
    res = _call(body, name=name, grid=(S // tm,),
                in_specs=[pl.BlockSpec((tm, K), lambda i: (i, 0)), pl.BlockSpec((D, K), lambda i: (0, 0)), row, vec, row],
                out_specs=(row, vec),
                out_shape=(jax.ShapeDtypeStruct((S, D), F32), jax.ShapeDtypeStruct((1, D), F32)),
                args=(dy, w, x, g.reshape(1, D), dres), comm=comm)
    return res[0], res[1], list(res[2:])


def _loss_head(x, g, target):
    S, D = x.shape
    tm = _pick(S, 512)

    def body(x_ref, g_ref, t_ref, loss_ref, dx_ref, dg_ref):
        @pl.when(pl.program_id(0) == 0)
        def _():
            dg_ref[...] = jnp.zeros_like(dg_ref)
            loss_ref[...] = jnp.zeros_like(loss_ref)

        xf = x_ref[...]
        gg = g_ref[...]
        r = lax.rsqrt(jnp.mean(xf * xf, axis=-1, keepdims=True) + RMS_EPS)
        xh = xf * r
        err = xh * gg - t_ref[...]
        per_tok = jnp.mean(err * err, axis=-1, keepdims=True)
        loss_ref[...] += 0.5 * jnp.sum(per_tok, axis=0, keepdims=True)
        dy = err * (1.0 / D)
        dg_ref[...] += jnp.sum(dy * xh, axis=0, keepdims=True)
        dxh = dy * gg
        dx_ref[...] = r * (dxh - xh * jnp.mean(dxh * xh, axis=-1, keepdims=True))

    row = pl.BlockSpec((tm, D), lambda i: (i, 0))
    vec = pl.BlockSpec((1, D), lambda i: (0, 0))
    one = pl.BlockSpec((1, 1), lambda i: (0, 0))
    return pl.pallas_call(
        body, name="loss_head",
        out_shape=(jax.ShapeDtypeStruct((1, 1), F32), jax.ShapeDtypeStruct((S, D), F32),
                   jax.ShapeDtypeStruct((1, D), F32)),
        grid=(S // tm,),
        in_specs=[row, vec, row], out_specs=(one, row, vec),
        compiler_params=_params("arbitrary"),
    )(x, g.reshape(1, D), target)


def _slopes(g):
    return [2.0 ** (-8.0 * (HEADS_PER_GROUP * g + j + 1) / DSW_HEADS) for j in range(HEADS_PER_GROUP)]


def _band_masks(W):
    row = lax.broadcasted_iota(jnp.int32, (W, W), 0)
    col = lax.broadcasted_iota(jnp.int32, (W, W), 1)
    d_cur = row - col
    d_prev = d_cur + W
    return d_cur, d_prev, d_cur >= 0, d_cur <= 0


def _band_specs(W, nb, per):
    def cur(c):
        return pl.BlockSpec((None, per * W, OUT_A), lambda r, n: (r, n, c))

    def prev(c):
        return pl.BlockSpec((None, W, OUT_A), lambda r, n: (r, jnp.maximum(per * n - 1, 0), c))

    def nxt(c):
        return pl.BlockSpec((None, W, OUT_A), lambda r, n: (r, jnp.minimum(per * (n + 1), nb - 1), c))

    return cur, prev, nxt


def _blocks_per_step(nb):
    return 8 if nb % 8 == 0 else 4 if nb % 4 == 0 else 2 if nb % 2 == 0 else 1


def _head_stack(W):
    H, hd = HEADS_PER_GROUP, HEAD_DIM
    lane_head = lax.broadcasted_iota(jnp.int32, (W, OUT_A), 1) // hd

    def stack(x):
        return jnp.concatenate([jnp.where(lane_head == h, x, jnp.zeros_like(x)) for h in range(H)], axis=0)

    def unstack(y):
        out = jnp.where(lane_head == 0, y[0:W], 0.0)
        for h in range(1, H):
            out = jnp.where(lane_head == h, y[h * W:(h + 1) * W], out)
        return out

    def column(ref, rows, off=0):
        return jnp.concatenate([ref[rows, h * hd + off:h * hd + off + 1] for h in range(H)], axis=0)

    return stack, unstack, column


def _stacked_bias(W, slopes, dil):
    d_cur, d_prev, m_cur, m_prev = _band_masks(W)
    b_cur = jnp.concatenate([(s * dil) * d_cur.astype(F32) for s in slopes], axis=0)
    b_prev = jnp.concatenate([(s * dil) * d_prev.astype(F32) for s in slopes], axis=0)
    H = len(slopes)
    return b_cur, b_prev, jnp.concatenate([m_cur] * H, axis=0), jnp.concatenate([m_prev] * H, axis=0)


def _attn_a_fwd(qkv, cols, g, comm=()):
    win, dil = DSW_GROUPS[g]
    W = win // dil
    d, L, _ = qkv[0].shape
    nb = L // W
    per = _blocks_per_step(nb)
    slopes = _slopes(g)

    def body(q_ref, kp_ref, kc_ref, vp_ref, vc_ref, o_ref, l_ref):
        n = pl.program_id(1)
        stack, unstack, _ = _head_stack(W)
        b_cur, b_prev, m_cur, m_prev = _stacked_bias(W, slopes, dil)
        m_first = jnp.logical_and(m_prev, n > 0)
        for b in range(per):
            rows = slice(b * W, (b + 1) * W)
            before = slice((b - 1) * W, b * W)
            qs = stack(q_ref[rows, :])
            kc, vc = kc_ref[rows, :], vc_ref[rows, :]
            kp, vp = (kp_ref[...], vp_ref[...]) if b == 0 else (kc_ref[before, :], vc_ref[before, :])
            s_c = jnp.where(m_cur, _dot(qs, kc, _NT) * ATT_SCALE - b_cur, NEG)
            s_p = jnp.where(m_first if b == 0 else m_prev, _dot(qs, kp, _NT) * ATT_SCALE - b_prev, NEG)
            m = jnp.maximum(jnp.max(s_c, axis=1, keepdims=True), jnp.max(s_p, axis=1, keepdims=True))
            p_c = jnp.exp(s_c - m)
            p_p = jnp.exp(s_p - m)
            den = jnp.sum(p_c, axis=1, keepdims=True) + jnp.sum(p_p, axis=1, keepdims=True)
            pv = _dot(p_c.astype(BF), vc) + _dot(p_p.astype(BF), vp)
            o_ref[rows, :] = unstack(pv / den)
            l_ref[rows, :] = unstack(jnp.broadcast_to(m + jnp.log(den), pv.shape))

    cur, prev, _ = _band_specs(W, nb, per)
    out = jax.ShapeDtypeStruct((d, L, OUT_A), F32)
    res = _call(body, name=f"attn_a_fwd_g{g}", grid=(d, nb // per),
                in_specs=[cur(cols[0]), prev(cols[1]), cur(cols[1]), prev(cols[2]), cur(cols[2])],
                out_specs=(cur(0), cur(0)), out_shape=(out, out),
                args=(qkv[0], qkv[1], qkv[1], qkv[2], qkv[2]), sem=("parallel", "parallel"), comm=comm)
    return res[0], res[1], list(res[2:])


def _attn_a_bwd(qkv, cols, do, stats, g):
    win, dil = DSW_GROUPS[g]
    W = win // dil
    d, L, _ = qkv[0].shape
    nb = L // W
    per = _blocks_per_step(nb)
    nsteps = nb // per
    slopes = _slopes(g)

    def body(q_ref, qn_ref, kp_ref, kc_ref, vp_ref, vc_ref, do_ref, don_ref, st_ref, stn_ref,
             dq_ref, dk_ref, dv_ref):
        n = pl.program_id(1)
        stack, unstack, column = _head_stack(W)
        b_cur, b_prev, m_cur, m_prev = _stacked_bias(W, slopes, dil)
        m_first = jnp.logical_and(m_prev, n > 0)
        m_last = jnp.logical_and(m_prev, n < nsteps - 1)
        everything = slice(None)
        for b in range(per):
            rows = slice(b * W, (b + 1) * W)
            before = slice((b - 1) * W, b * W)
            after = slice((b + 1) * W, (b + 2) * W)
            first, last = b == 0, b == per - 1
            qs = stack(q_ref[rows, :])
            qn = stack(qn_ref[...] if last else q_ref[after, :])
            dos = stack(do_ref[rows, :])
            don = stack(don_ref[...] if last else do_ref[after, :])
            kc, vc = kc_ref[rows, :], vc_ref[rows, :]
            kp, vp = (kp_ref[...], vp_ref[...]) if first else (kc_ref[before, :], vc_ref[before, :])
            lse_c, dsum_c = column(st_ref, rows), column(st_ref, rows, STAT_OFF)
            lse_n = column(stn_ref, everything) if last else column(st_ref, after)
            dsum_n = column(stn_ref, everything, STAT_OFF) if last else column(st_ref, after, STAT_OFF)
            m_p = m_first if first else m_prev
            m_n = m_last if last else m_prev
            p_cc = jnp.exp(jnp.where(m_cur, _dot(qs, kc, _NT) * ATT_SCALE - b_cur, NEG) - lse_c)
            p_cp = jnp.exp(jnp.where(m_p, _dot(qs, kp, _NT) * ATT_SCALE - b_prev, NEG) - lse_c)
            p_nc = jnp.exp(jnp.where(m_n, _dot(qn, kc, _NT) * ATT_SCALE - b_prev, NEG) - lse_n)
            ds_cc = (p_cc * (_dot(dos, vc, _NT) - dsum_c) * ATT_SCALE).astype(BF)
            ds_cp = (p_cp * (_dot(dos, vp, _NT) - dsum_c) * ATT_SCALE).astype(BF)
            ds_nc = (p_nc * (_dot(don, vc, _NT) - dsum_n) * ATT_SCALE).astype(BF)
            dq_ref[rows, :] = unstack(_dot(ds_cc, kc) + _dot(ds_cp, kp)).astype(BF)
            dk_ref[rows, :] = (_dot(ds_cc, qs, _TN) + _dot(ds_nc, qn, _TN)).astype(BF)
            dv_ref[rows, :] = (_dot(p_cc.astype(BF), dos, _TN) + _dot(p_nc.astype(BF), don, _TN)).astype(BF)

    cur, prev, nxt = _band_specs(W, nb, per)
    out = jax.ShapeDtypeStruct((d, L, OUT_A), BF)
    cq, ck, cv = cols
    return pl.pallas_call(
        body, name=f"attn_a_bwd_g{g}",
        out_shape=(out, out, out),
        grid=(d, nsteps),
        in_specs=[cur(cq), nxt(cq), prev(ck), cur(ck), prev(cv), cur(cv), cur(0), nxt(0), cur(0), nxt(0)],
        out_specs=(cur(0), cur(0), cur(0)),
        compiler_params=_params("parallel", "parallel"),
    )(qkv[0], qkv[0], qkv[1], qkv[1], qkv[2], qkv[2], do, do, stats, stats)


SB_PAIR = 2
SB_QUAD = 4
SB_QROWS = SB_BLOCK


def _softplus_parts(z):
    e = jnp.exp(-jnp.abs(z))
    log1p_e = jnp.where(e < 1e-4, e, jnp.log(1.0 + e))
    return e, jnp.maximum(z, 0.0) + log1p_e


def _split_dot(x, t):
    hi = x.astype(BF)
    lo = (x - hi.astype(F32)).astype(BF)
    return _dot(hi, t) + _dot(lo, t)


def _sb_block(qh, kk, causal, r_run, tri_incl):
    z = _dot(qh, kk, _NT)
    e, sp = _softplus_parts(z)
    ls = jnp.where(causal, -sp, 0.0)
    cin = _split_dot(ls, tri_incl)
    a = jnp.where(causal, jnp.exp(z + cin + r_run), 0.0)
    return z, e, cin, a


def _sb_specs(S, pair=SB_PAIR):
    Q, hd = SB_QROWS, HEAD_DIM
    lanes = pair * hd
    qc = (3 * W_A) // lanes
    kc = (3 * W_A + W_B) // lanes
    vc = (3 * W_A + 2 * W_B) // lanes
    q_spec = pl.BlockSpec((Q, lanes), lambda p, i: (i, qc + p))
    k_spec = pl.BlockSpec((S, lanes), lambda p, i: (0, kc + p))
    v_spec = pl.BlockSpec((S, lanes), lambda p, i: (0, vc + p))
    o_spec = pl.BlockSpec((Q, lanes), lambda p, i: (i, p))
    full = pl.BlockSpec((S, lanes), lambda p, i: (0, p))
    return q_spec, k_spec, v_spec, o_spec, full


def _sb_stack(pair=SB_PAIR):
    Q, hd = SB_QROWS, HEAD_DIM
    lane_head = lax.broadcasted_iota(jnp.int32, (Q, pair * hd), 1) // hd

    def stack(x):
        return jnp.concatenate([jnp.where(lane_head == h, x, jnp.zeros_like(x)) for h in range(pair)], axis=0)

    def unstack(y):
        out = jnp.where(lane_head == 0, y[0:Q], 0.0)
        for h in range(1, pair):
            out = jnp.where(lane_head == h, y[h * Q:(h + 1) * Q], out)
        return out

    return stack, unstack


def _sb_iotas(i, pair=SB_PAIR):
    B, Q = SB_BLOCK, SB_QROWS
    row = lax.broadcasted_iota(jnp.int32, (Q, B), 0) + i * Q
    col = lax.broadcasted_iota(jnp.int32, (Q, B), 1)
    ahead = jnp.concatenate([col - row] * pair, axis=0)
    tr = lax.broadcasted_iota(jnp.int32, (B, B), 0)
    tc = lax.broadcasted_iota(jnp.int32, (B, B), 1)
    return ahead, tr, tc


def _sb_fwd(proj, comm=()):
    S = proj.shape[0]
    B, Q, hd = SB_BLOCK, SB_QROWS, HEAD_DIM
    nq = S // Q
    P = SB_QUAD
    R = P * Q
    q_spec, k_spec, v_spec, o_spec, _ = _sb_specs(S, P)

    def body(q_ref, k_ref, v_ref, o_ref):
        i = pl.program_id(1)
        ahead, tr, tc = _sb_iotas(i, P)
        tri_incl = (tr >= tc).astype(BF)
        stack, unstack = _sb_stack(P)
        qs = stack(q_ref[...] * ATT_SCALE)

        def cond(c):
            return jnp.logical_and(c[0] >= 0, c[-1] > SB_EXIT)

        def step(c):
            kb, r_run, acc, _ = c
            off = pl.multiple_of(kb * B, B)
            causal = ahead < -kb * B
            _, _, cin, a = _sb_block(qs, k_ref[pl.ds(off, B), :], causal, r_run, tri_incl)
            acc = acc + _dot(a.astype(BF), v_ref[pl.ds(off, B), :])
            r_run = r_run + cin[:, 0:1]
            return kb - 1, r_run, acc, jnp.max(r_run)

        init = (i, jnp.zeros((R, 1), F32), jnp.zeros((R, P * hd), F32), jnp.float32(0.0))
        fin = lax.while_loop(cond, step, init)
        o_ref[...] = unstack(fin[2])

    res = _call(body, name="sb_fwd", grid=(SB_HEADS // P, nq), in_specs=[q_spec, k_spec, v_spec],
                out_specs=o_spec, out_shape=jax.ShapeDtypeStruct((S, W_B), F32), args=(proj, proj, proj),
                sem=("parallel", "parallel"), comm=comm)
    return res[0], list(res[1:])


def _sb_bwd(proj, do, o, comm=()):
    S = proj.shape[0]
    B, Q, hd = SB_BLOCK, SB_QROWS, HEAD_DIM
    nq = S // Q
    R = SB_PAIR * Q
    q_spec, k_spec, v_spec, o_spec, full = _sb_specs(S)

    def body(q_ref, k_ref, v_ref, do_ref, o_ref, dq_ref, dk_ref, dv_ref):
        i = pl.program_id(1)

        @pl.when(i == 0)
        def _():
            dk_ref[...] = jnp.zeros_like(dk_ref)
            dv_ref[...] = jnp.zeros_like(dv_ref)

        ahead, tr, tc = _sb_iotas(i)
        tri_incl = (tr >= tc).astype(BF)
        tri_strict = (tr > tc).astype(BF)
        stack, unstack = _sb_stack()
        qs = stack(q_ref[...] * ATT_SCALE)
        dobs = stack(do_ref[...])
        o_all = o_ref[...]
        dsum = jnp.sum(dobs.astype(F32) * jnp.concatenate([o_all] * SB_PAIR, axis=0), axis=1, keepdims=True)

        def cond(c):
            return jnp.logical_and(c[0] >= 0, c[-1] > SB_EXIT)

        def step(c):
            kb, r_run, g_run, dq, _ = c
            off = pl.multiple_of(kb * B, B)
            causal = ahead < -kb * B
            kk = k_ref[pl.ds(off, B), :]
            vv = v_ref[pl.ds(off, B), :]
            z, e, cin, a = _sb_block(qs, kk, causal, r_run, tri_incl)
            a16 = a.astype(BF)
            gmat = a16.astype(F32) * _dot(dobs, vv, _NT)
            later = _split_dot(gmat, tri_strict)
            pfx = dsum - g_run - later
            sig = jnp.where(z >= 0, 1.0, e) / (1.0 + e)
            dz = jnp.where(causal, gmat - sig * pfx, 0.0).astype(BF)
            dq = dq + _dot(dz, kk)
            dk_ref[pl.ds(off, B), :] += _dot(dz, qs, _TN)
            dv_ref[pl.ds(off, B), :] += _dot(a16, dobs, _TN)
            g_run = g_run + jnp.sum(gmat, axis=1, keepdims=True)
            r_run = r_run + cin[:, 0:1]
            return kb - 1, r_run, g_run, dq, jnp.max(r_run)

        init = (i, jnp.zeros((R, 1), F32), jnp.zeros((R, 1), F32), jnp.zeros((R, SB_PAIR * hd), F32),
                jnp.float32(0.0))
        fin = lax.while_loop(cond, step, init)
        dq_ref[...] = unstack(fin[3]) * ATT_SCALE

    out = jax.ShapeDtypeStruct((S, W_B), F32)
    res = _call(body, name="sb_bwd", grid=(SB_HEADS // SB_PAIR, nq), in_specs=[q_spec, k_spec, v_spec, o_spec, o_spec],
                out_specs=(o_spec, full, full), out_shape=(out, out, out), args=(proj, proj, proj, do, o), comm=comm)
    return res[0], res[1], res[2], list(res[3:])


def _merge_wo(o_g, l_g, o_b, proj, b_gate, w_br, w_o, x, name, comm=()):
    S = o_b.shape[0]
    D = D_MODEL
    tm = _pick(S, 256)
    gcol = GATE_OFF // D

    def body(o0, o1, o2, l0, l1, l2, ob_ref, ga_ref, gb_ref, bg_ref, w_ref, wo_ref, x_ref,
             x1_ref, mg_ref, oa_ref, lse_ref):
        la, lb, lc = l0[...], l1[...], l2[...]
        mx = jnp.maximum(jnp.maximum(la, lb), lc)
        ea, eb, ec = jnp.exp(la - mx), jnp.exp(lb - mx), jnp.exp(lc - mx)
        den = ea + eb + ec
        oa = (ea * o0[...] + eb * o1[...] + ec * o2[...]) / den
        oa_ref[...] = oa
        lse_ref[...] = mx + jnp.log(den)
        oa16 = oa.astype(BF)
        ob16 = ob_ref[...].astype(BF)
        acc = x_ref[...]
        for c0 in range(0, D, FFN_CHUNK):
            cs = slice(c0, c0 + FFN_CHUNK)
            ya = _dot(oa16, w_ref[0:OUT_A, cs])
            yb = _dot(ob16, w_ref[OUT_A:OUT_A + W_B, cs])
            g_a = jax.nn.sigmoid(ga_ref[:, cs].astype(F32) + bg_ref[:, cs])
            g_b = jax.nn.sigmoid(gb_ref[:, cs].astype(F32) + bg_ref[:, D + c0:D + c0 + FFN_CHUNK])
            mg = (g_a * ya + g_b * yb).astype(BF)
            mg_ref[:, cs] = mg
            acc = acc + _dot(mg, wo_ref[cs, :])
        x1_ref[...] = acc

    nar = pl.BlockSpec((tm, OUT_A), lambda i: (i, 0))
    wide = pl.BlockSpec((tm, D), lambda i: (i, 0))
    res = _call(body, name=name, grid=(S // tm,),
                in_specs=[nar] * 7 + [pl.BlockSpec((tm, D), lambda i: (i, gcol)),
                                      pl.BlockSpec((tm, D), lambda i: (i, gcol + 1)),
                                      pl.BlockSpec((1, 2 * D), lambda i: (0, 0)),
                                      pl.BlockSpec((OUT_A + W_B, D), lambda i: (0, 0)),
                                      pl.BlockSpec((D, D), lambda i: (0, 0)), wide],
                out_specs=(wide, wide, nar, nar),
                out_shape=(jax.ShapeDtypeStruct((S, D), F32), jax.ShapeDtypeStruct((S, D), BF),
                           jax.ShapeDtypeStruct((S, OUT_A), F32), jax.ShapeDtypeStruct((S, OUT_A), F32)),
                args=(*o_g, *l_g, o_b, proj, proj, b_gate.reshape(1, 2 * D), w_br, w_o, x), comm=comm)
    return res[0], res[1], res[2], res[3], list(res[4:])


STAT_OFF = HEAD_DIM // 2


def _merge_bwd(dx, w_o, oa, ob, lse, proj, b_gate, w_br):
    S = ob.shape[0]
    D = D_MODEL
    tm = _pick(S, 256)
    gcol = GATE_OFF // D

    def body(dx_ref, wo_ref, oa_ref, ob_ref, l_ref, ga_ref, gb_ref, bg_ref, w_ref,
             dya_ref, dyb_ref, doa_ref, dob_ref, st_ref, dg_ref, dbg_ref):
        @pl.when(pl.program_id(0) == 0)
        def _():
            dbg_ref[...] = jnp.zeros_like(dbg_ref)

        dx16 = dx_ref[...].astype(BF)
        oa = oa_ref[...]
        oa16 = oa.astype(BF)
        ob16 = ob_ref[...].astype(BF)
        doa = jnp.zeros((tm, OUT_A), F32)
        dob = jnp.zeros((tm, W_B), F32)
        for c0 in range(0, D, FFN_CHUNK):
            cs = slice(c0, c0 + FFN_CHUNK)
            cs2 = slice(D + c0, D + c0 + FFN_CHUNK)
            wa = w_ref[0:OUT_A, cs]
            wb = w_ref[OUT_A:OUT_A + W_B, cs]
            dm = _dot(dx16, wo_ref[cs, :], _NT)
            ya = _dot(oa16, wa)
            yb = _dot(ob16, wb)
            g_a = jax.nn.sigmoid(ga_ref[:, cs].astype(F32) + bg_ref[:, cs])
            g_b = jax.nn.sigmoid(gb_ref[:, cs].astype(F32) + bg_ref[:, cs2])
            dga = dm * ya * g_a * (1.0 - g_a)
            dgb = dm * yb * g_b * (1.0 - g_b)
            dg_ref[:, cs] = dga.astype(BF)
            dg_ref[:, cs2] = dgb.astype(BF)
            dbg_ref[:, cs] += jnp.sum(dga, axis=0, keepdims=True)
            dbg_ref[:, cs2] += jnp.sum(dgb, axis=0, keepdims=True)
            dya = (dm * g_a).astype(BF)
            dyb = (dm * g_b).astype(BF)
            dya_ref[:, cs] = dya
            dyb_ref[:, cs] = dyb
            doa = doa + _dot(dya, wa, _NT)
            dob = dob + _dot(dyb, wb, _NT)
        doa = doa.astype(BF)
        doa_ref[...] = doa
        dob_ref[...] = dob.astype(BF)
        r = lax.broadcasted_iota(jnp.int32, (OUT_A, OUT_A), 0) // HEAD_DIM
        c = lax.broadcasted_iota(jnp.int32, (OUT_A, OUT_A), 1) // HEAD_DIM
        dsum = _split_dot(doa.astype(F32) * oa, (r == c).astype(BF))
        lane = lax.broadcasted_iota(jnp.int32, dsum.shape, 1) % HEAD_DIM
        st_ref[...] = jnp.where(lane < STAT_OFF, l_ref[...], dsum)

    nar = pl.BlockSpec((tm, OUT_A), lambda i: (i, 0))
    wide = pl.BlockSpec((tm, D), lambda i: (i, 0))
    wide2 = pl.BlockSpec((tm, 2 * D), lambda i: (i, 0))
    vec2 = pl.BlockSpec((1, 2 * D), lambda i: (0, 0))
    return pl.pallas_call(
        body, name="merge_bwd",
        out_shape=(jax.ShapeDtypeStruct((S, D), BF), jax.ShapeDtypeStruct((S, D), BF),
                   jax.ShapeDtypeStruct((S, OUT_A), BF), jax.ShapeDtypeStruct((S, W_B), BF),
                   jax.ShapeDtypeStruct((S, OUT_A), F32), jax.ShapeDtypeStruct((S, 2 * D), BF),
                   jax.ShapeDtypeStruct((1, 2 * D), F32)),
        grid=(S // tm,),
        in_specs=[wide, pl.BlockSpec((D, D), lambda i: (0, 0)), nar, nar, nar,
                  pl.BlockSpec((tm, D), lambda i: (i, gcol)), pl.BlockSpec((tm, D), lambda i: (i, gcol + 1)), vec2,
                  pl.BlockSpec((OUT_A + W_B, D), lambda i: (0, 0))],
        out_specs=(wide, wide, nar, nar, nar, wide2, vec2),
        compiler_params=_params("arbitrary"),
    )(dx, w_o, oa, ob, lse, proj, proj, b_gate.reshape(1, 2 * D), w_br)


_SQRT_HALF = 0.7071067811865476
_INV_SQRT_2PI = 0.3989422804014327


def _gelu_parts(a):
    cdf = 0.5 * (1.0 + lax.erf(a * _SQRT_HALF))
    pdf = _INV_SQRT_2PI * jnp.exp(-0.5 * a * a)
    return cdf, pdf


def _shift_down(a, halo, k):
    rows = lax.broadcasted_iota(jnp.int32, a.shape, 0)
    out = pltpu.roll(a, k, 0)
    for r in range(k):
        out = jnp.where(rows == r, halo[8 - k + r:8 - k + r + 1, :], out)
    return out


def _ffn_specs(S, tm):
    F = D_FF
    t8 = tm // 8
    a_spec = pl.BlockSpec((tm, F), lambda i: (i, 0))
    v_spec = pl.BlockSpec((tm, F), lambda i: (i, 1))
    halo_prev = pl.BlockSpec((8, F), lambda i: (jnp.maximum(i * t8 - 1, 0), 0))
    return a_spec, v_spec, halo_prev


def _ffn_down(up, conv_w, conv_b, w_down, res, *, name, comm=()):
    S = up.shape[0]
    F = D_FF
    D = w_down.shape[1]
    tm = _pick(S, 256)
    a_spec, v_spec, halo_prev = _ffn_specs(S, tm)

    def body(a_ref, h_ref, v_ref, w_ref, b_ref, wd_ref, r_ref, o_ref, act_ref, ac_ref):
        first = pl.program_id(0) == 0
        acc = r_ref[...]
        for c0 in range(0, F, FFN_CHUNK):
            cs = slice(c0, c0 + FFN_CHUNK)
            a = a_ref[:, cs].astype(F32)
            halo = jnp.where(first, 0.0, h_ref[:, cs].astype(F32))
            w = w_ref[:, cs]
            ac = b_ref[:, cs] + w[0:1, :] * _shift_down(a, halo, 2) + w[1:2, :] * _shift_down(a, halo, 1) + w[2:3, :] * a
            ac_ref[:, cs] = ac.astype(BF)
            cdf, _ = _gelu_parts(ac)
            act = (ac * cdf * v_ref[:, cs].astype(F32)).astype(BF)
            act_ref[:, cs] = act
            acc = acc + _dot(act, wd_ref[cs, :])
        o_ref[...] = acc

    row = pl.BlockSpec((tm, D), lambda i: (i, 0))
    res_ = _call(body, name=name, grid=(S // tm,),
                 in_specs=[a_spec, halo_prev, v_spec, pl.BlockSpec((3, F), lambda i: (0, 0)),
                           pl.BlockSpec((1, F), lambda i: (0, 0)), pl.BlockSpec((F, D), lambda i: (0, 0)), row],
                 out_specs=(row, a_spec, a_spec),
                 out_shape=(jax.ShapeDtypeStruct((S, D), F32), jax.ShapeDtypeStruct((S, F), BF),
                            jax.ShapeDtypeStruct((S, F), BF)),
                 args=(up, up, up, conv_w, conv_b.reshape(1, F), w_down, res), comm=comm)
    return res_[0], res_[1], res_[2], list(res_[3:])


def _shift_up_pair(a, nxt):
    n = a.shape[0]
    r8 = lax.broadcasted_iota(jnp.int32, (8,) + a.shape[1:], 0)
    out = []
    for k in (1, 2):
        rolled = pltpu.roll(a, n - k, 0)
        tail = jnp.where(r8 >= 8 - k, pltpu.roll(nxt, 8 - k, 0), rolled[n - 8:n])
        out.append(jnp.concatenate([rolled[0:n - 8], tail], axis=0))
    return out


def _ffn_bwd(dx, w_down, up, ac, conv_w, comm=()):
    S = up.shape[0]
    F = D_FF
    D = dx.shape[1]
    tm = _pick(S, 256)
    t8 = tm // 8
    nt = S // tm
    a_spec, v_spec, _ = _ffn_specs(S, tm)

    def nxt(width, col):
        return pl.BlockSpec((8, width), lambda i: (jnp.minimum((i + 1) * t8, S // 8 - 1), col))

    def body(dx_ref, dxn_ref, wd_ref, ac_ref, acn_ref, a_ref, v_ref, vn_ref, w_ref, dup_ref, dw_ref, db_ref):
        i = pl.program_id(0)

        @pl.when(i == 0)
        def _():
            dw_ref[...] = jnp.zeros_like(dw_ref)
            db_ref[...] = jnp.zeros_like(db_ref)

        dx16 = dx_ref[...].astype(BF)
        dxn16 = dxn_ref[...].astype(BF)
        last = i == nt - 1

        def dconv(dact, ac, v):
            cdf, pdf = _gelu_parts(ac)
            return cdf, dact * v * (cdf + ac * pdf)

        for c0 in range(0, F, FFN_BWD_CHUNK):
            cs = slice(c0, c0 + FFN_BWD_CHUNK)
            wd = wd_ref[cs, :]
            dact = _dot(dx16, wd, _NT)
            ac = ac_ref[:, cs].astype(F32)
            cdf, dac = dconv(dact, ac, v_ref[:, cs].astype(F32))
            dup_ref[:, F + c0:F + c0 + FFN_BWD_CHUNK] = (dact * ac * cdf).astype(BF)
            _, dac_n = dconv(_dot(dxn16, wd, _NT), acn_ref[:, cs].astype(F32), vn_ref[:, cs].astype(F32))
            d1, d2 = _shift_up_pair(dac, jnp.where(last, 0.0, dac_n))
            w = w_ref[:, cs]
            dup_ref[:, cs] = (w[2:3, :] * dac + w[1:2, :] * d1 + w[0:1, :] * d2).astype(BF)
            a = a_ref[:, cs].astype(F32)
            db_ref[:, cs] += jnp.sum(dac, axis=0, keepdims=True)
            dw_ref[0:1, cs] += jnp.sum(d2 * a, axis=0, keepdims=True)
            dw_ref[1:2, cs] += jnp.sum(d1 * a, axis=0, keepdims=True)
            dw_ref[2:3, cs] += jnp.sum(dac * a, axis=0, keepdims=True)

    w_spec = pl.BlockSpec((3, F), lambda i: (0, 0))
    b_spec = pl.BlockSpec((1, F), lambda i: (0, 0))
    res = _call(body, name="ffn_bwd", grid=(nt,),
                in_specs=[pl.BlockSpec((tm, D), lambda i: (i, 0)), nxt(D, 0), pl.BlockSpec((F, D), lambda i: (0, 0)),
                          a_spec, nxt(F, 0), a_spec, v_spec, nxt(F, 1), w_spec],
                out_specs=(pl.BlockSpec((tm, 2 * F), lambda i: (i, 0)), w_spec, b_spec),
                out_shape=(jax.ShapeDtypeStruct((S, 2 * F), BF), jax.ShapeDtypeStruct((3, F), F32),
                           jax.ShapeDtypeStruct((1, F), F32)),
                args=(dx, dx, w_down, ac, ac, up, up, up, conv_w), comm=comm)
    return res[0], res[1], res[2], list(res[3:])


def _adamw(parts, w, m, v, name, row0=0, prev=None):
    R, C = w.shape
    Rp = parts.shape[1]
    tr = Rp
    for cand in (512, 256, 128, 64, 32, 16):
        if Rp % cand == 0 and row0 % cand == 0 and cand * C * 4 <= (1 << 21):
            tr = cand
            break
    b0 = row0 // tr
    c1 = 1.0 / (1.0 - ADAM_B1 ** ADAM_STEP)
    c2 = 1.0 / (1.0 - ADAM_B2 ** ADAM_STEP)

    def body(p_ref, w_ref, m_ref, v_ref, *rest):
        g_ref, d_ref, nm_ref, nv_ref = rest[-4:]
        g = p_ref[0].astype(F32)
        for j in range(1, N_DEV):
            g = g + p_ref[j].astype(F32)
        nm = ADAM_B1 * m_ref[...] + (1.0 - ADAM_B1) * g
        nv = ADAM_B2 * v_ref[...] + (1.0 - ADAM_B2) * (g * g)
        g_ref[...] = g
        nm_ref[...] = nm
        nv_ref[...] = nv
        d_ref[...] = -ADAM_LR * ((nm * c1) / (jnp.sqrt(nv * c2) + ADAM_EPS) + ADAM_WD * w_ref[...])

    blk = pl.BlockSpec((tr, C), lambda i: (b0 + i, 0))
    out = jax.ShapeDtypeStruct((R, C), F32)
    carried = [] if prev is None else list(prev)
    return pl.pallas_call(
        body, name=name,
        out_shape=(out, out, out, out),
        grid=(Rp // tr,),
        in_specs=[pl.BlockSpec((N_DEV, tr, C), lambda i: (0, i, 0)), blk, blk, blk]
        + [pl.BlockSpec(memory_space=pl.ANY)] * len(carried),
        out_specs=(blk, blk, blk, blk),
        input_output_aliases={4 + k: k for k in range(len(carried))},
        compiler_params=_params("parallel"),
    )(parts, w, m, v, *carried)


def _dil(t, dil):
    S, C = t.shape
    if dil == 1:
        return t.reshape(1, S, C)
    return t.reshape(S // dil, dil, C).transpose(1, 0, 2)


def _undil(t):
    d, L, C = t.shape
    if d == 1:
        return t.reshape(L, C)
    return t.transpose(1, 0, 2).reshape(L * d, C)


def _group_qkv(proj, g):
    dil = DSW_GROUPS[g][1]
    if dil == 1:
        p3 = _dil(proj, 1)
        return (p3, p3, p3), (g, W_A // OUT_A + g, 2 * W_A // OUT_A + g)
    c0 = g * OUT_A
    return tuple(_dil(proj[:, o + c0:o + c0 + OUT_A], dil) for o in (0, W_A, 2 * W_A)), (0, 0, 0)


_COL_SHARDED = ("w_in", "w_br", "w_up")
_ROW_SHARDED = ("w_o", "w_down")


class _Plan:
    def __init__(self):
        self.riders = {}
        self.landed = {}

    def ride(self, slot, key, kind, x):
        self.riders.setdefault(slot, []).append((key, kind, x))

    def run(self, slot, fn, *args, **kw):
        items = self.riders.pop(slot, [])
        res = fn(*args, comm=[(kind, x) for _, kind, x in items], **kw)
        for (key, _, _), r in zip(items, res[-1]):
            self.landed[key] = r
        return res[0] if len(res) == 2 else res[:-1]

    def weight(self, n, l):
        g = self.landed[(n, l)]
        if n in _COL_SHARDED:
            return g.transpose(1, 0, 2).reshape(g.shape[1], -1)
        return g.reshape(-1, g.shape[2])

    def scatter(self, slot, n, l, full, part=0):
        K, N = full.shape
        if n in _COL_SHARDED:
            blocks = full.reshape(K, N_DEV, N // N_DEV).transpose(1, 0, 2)
        else:
            blocks = full.reshape(N_DEV, K // N_DEV, N)
        self.ride(slot, ("d" + n, l, part), "scatter", blocks)


def _layer_fwd(x, p, plan, l):
    proj, h = plan.run(f"proj_{l}", _norm_matmul, x, p["norm1"][l], plan.weight("w_in", l), name=f"proj_{l}",
                       tm=512, tn=1024)
    o_g, l_g, qkv_g = [], [], []
    for g in range(N_GROUPS):
        qkv, cols = _group_qkv(proj, g)
        og, lg = plan.run(f"attn_a_fwd_g{g}_{l}", _attn_a_fwd, qkv, cols, g)
        o_g.append(_undil(og))
        l_g.append(_undil(lg))
        qkv_g.append((qkv, cols))
    ob = plan.run(f"sb_fwd_{l}", _sb_fwd, proj)
    x1, merged, oa, lse = plan.run(f"wo_{l}", _merge_wo, o_g, l_g, ob, proj, p["b_gate"][l], plan.weight("w_br", l),
                                   plan.weight("w_o", l), x, f"wo_{l}")
    up, h2 = plan.run(f"up_{l}", _norm_matmul, x1, p["norm2"][l], plan.weight("w_up", l), name=f"up_{l}",
                      tm=512, tn=1408)
    x2, act, ac = plan.run(f"down_{l}", _ffn_down, up, p["conv_w"][l], p["conv_b"][l], plan.weight("w_down", l), x1,
                           name=f"down_{l}")
    saved = dict(x=x, h=h, proj=proj, qkv_g=qkv_g, oa=oa, ob=ob, lse=lse, merged=merged, x1=x1, h2=h2, up=up, act=act, ac=ac)
    return x2, saved


def _layer_bwd(dx2, sv, p, plan, l):
    gr = {}
    dwd = plan.run(f"dw_down_{l}", _matmul, sv["act"], dx2, mode="tn", out_dtype=BF, name=f"dw_down_{l}",
                   tm=1408, tn=1024, tk=2048)
    plan.scatter(f"d_h2_{l}", "w_down", l, dwd)
    dup, gr["conv_w"], dcb = plan.run(f"ffn_bwd_{l}", _ffn_bwd, dx2, plan.weight("w_down", l), sv["up"], sv["ac"],
                                      p["conv_w"][l])
    gr["conv_b"] = dcb[0]
    dx1, dn2 = plan.run(f"d_h2_{l}", _matmul_rms_bwd, dup, plan.weight("w_up", l), sv["x1"], p["norm2"][l], dx2,
                        name=f"d_h2_{l}")
    dwu = plan.run(f"dw_up_{l}", _matmul, sv["h2"], dup, mode="tn", out_dtype=BF, name=f"dw_up_{l}",
                   tm=1024, tn=1408, tk=2048)
    plan.scatter(f"sb_bwd_{l}", "w_up", l, dwu)
    gr["norm2"] = dn2[0]
    dwo = plan.run(f"dw_o_{l}", _matmul, sv["merged"], dx1, mode="tn", out_dtype=BF, name=f"dw_o_{l}",
                   tm=1024, tn=1024, tk=2048)
    plan.scatter(f"dw_in_{l}", "w_o", l, dwo)
    dya, dyb, doa, dob, stats, dgate, dbg = _merge_bwd(dx1, plan.weight("w_o", l), sv["oa"], sv["ob"], sv["lse"],
                                                       sv["proj"], p["b_gate"][l], plan.weight("w_br", l))
    gr["b_gate"] = dbg[0]
    dwa = plan.run(f"dw_bra_{l}", _matmul, sv["oa"], dya, mode="tn", out_dtype=BF, name=f"dw_bra_{l}",
                   tm=256, tn=1024, tk=2048)
    dwb = plan.run(f"dw_brb_{l}", _matmul, sv["ob"], dyb, mode="tn", out_dtype=BF, name=f"dw_brb_{l}",
                   tm=256, tn=1024, tk=2048)
    plan.scatter(f"dw_in_{l}", "w_br", l, jnp.concatenate([dwa, dwb], axis=0))
    proj = sv["proj"]
    dq_a, dk_a, dv_a = [], [], []
    for g, (_, dil) in enumerate(DSW_GROUPS):
        qkv, cols = sv["qkv_g"][g]
        dqg, dkg, dvg = _attn_a_bwd(qkv, cols, _dil(doa, dil), _dil(stats, dil), g)
        dq_a.append(_undil(dqg))
        dk_a.append(_undil(dkg))
        dv_a.append(_undil(dvg))
    dqb, dkb, dvb = plan.run(f"sb_bwd_{l}", _sb_bwd, proj, dob, sv["ob"])
    dproj = jnp.concatenate(dq_a + dk_a + dv_a + [dqb.astype(BF), dkb.astype(BF), dvb.astype(BF), dgate], axis=1)
    dx, dn1 = plan.run(f"d_h_{l}", _matmul_rms_bwd, dproj, plan.weight("w_in", l), sv["x"], p["norm1"][l], dx1,
                       name=f"d_h_{l}")
    if l > 0:
        dwi = plan.run(f"dw_in_{l}", _matmul, sv["h"], dproj, mode="tn", out_dtype=BF, name=f"dw_in_{l}",
                       tm=1024, tn=1280, tk=2048)
        plan.scatter(f"ffn_bwd_{l - 1}", "w_in", l, dwi)
    else:
        half = D_MODEL // 2
        for part in range(2):
            name = f"dw_in_{l}" if part == 0 else f"dw_in_{l}_rest"
            dwi = plan.run(name, _matmul, sv["h"], dproj, mode="tn", out_dtype=BF, name=name, tm=half, tn=1280,
                           tk=2048, m_cols=half, m_off=part * half)
            plan.scatter(f"dw_in_{l}_rest" if part == 0 else "alone", "w_in", l, dwi, part)
    gr["norm1"] = dn1[0]
    return dx, gr


def kernel(x, norm1, w_in, b_gate, w_br, w_o, norm2, w_up, conv_w, conv_b, w_down, norm_f, loss_target, m_norm1, m_w_in, m_b_gate, m_w_br, m_w_o, m_norm2, m_w_up, m_conv_w, m_conv_b, m_w_down, m_norm_f, v_norm1, v_w_in, v_b_gate, v_w_br, v_w_o, v_norm2, v_w_up, v_conv_w, v_conv_b, v_w_down, v_norm_f):
    depth = norm1.shape[0]
    me = 4 * lax.axis_index("x") + 2 * lax.axis_index("y") + lax.axis_index("c")
    shards = dict(w_in=w_in, w_br=w_br, w_o=w_o, w_up=w_up, w_down=w_down)
    moments_m = dict(norm1=m_norm1, w_in=m_w_in, b_gate=m_b_gate, w_br=m_w_br, w_o=m_w_o, norm2=m_norm2,
                     w_up=m_w_up, conv_w=m_conv_w, conv_b=m_conv_b, w_down=m_w_down, norm_f=m_norm_f)
    moments_v = dict(norm1=v_norm1, w_in=v_w_in, b_gate=v_b_gate, w_br=v_w_br, w_o=v_w_o, norm2=v_norm2,
                     w_up=v_w_up, conv_w=v_conv_w, conv_b=v_conv_b, w_down=v_w_down, norm_f=v_norm_f)

    plan = _Plan()
    wb = {n: s.astype(BF) for n, s in shards.items()}
    p = dict(norm1=norm1, b_gate=b_gate, norm2=norm2, conv_b=conv_b)
    cw = _all_gather(conv_w, "gather_conv_w")
    p["conv_w"] = cw.transpose(1, 2, 0, 3).reshape(depth, 3, D_FF)
    plan.landed[("w_in", 0)] = _all_gather_via_sibling(wb["w_in"][0], "gather_w_in_0")
    for l in range(depth):
        plan.ride(f"proj_{l}", ("w_down", l), "gather", wb["w_down"][l])
        plan.ride(f"attn_a_fwd_g0_{l}" if l == 0 else f"down_{l - 1}", ("w_br", l), "gather", wb["w_br"][l])
        plan.ride(f"attn_a_fwd_g0_{l}" if l == 0 else f"down_{l - 1}", ("w_o", l), "gather", wb["w_o"][l])
        plan.ride(f"sb_fwd_{l}", ("w_up", l), "gather", wb["w_up"][l])
        if l + 1 < depth:
            plan.ride(f"up_{l}", ("w_in", l + 1), "gather", wb["w_in"][l + 1])

    xs = x[0]
    saved = []
    for l in range(depth):
        xs, sv = _layer_fwd(xs, p, plan, l)
        saved.append(sv)
    loss_part, dx, dnf = _loss_head(xs, norm_f, loss_target[0])
    loss = lax.psum(loss_part[0, 0], ("x", "y", "c"))

    grads = [None] * depth
    for l in reversed(range(depth)):
        dx, grads[l] = _layer_bwd(dx, saved[l], p, plan, l)
    grad_x = dx[None]
    (key, _, last), = plan.riders.pop("alone")
    plan.landed[key] = _all_to_all(last, "scatter_w_in_rest")
    assert not plan.riders, sorted(plan.riders)

    out_g, out_d, out_m, out_v = {}, {}, {}, {}
    for n in _COL_SHARDED + _ROW_SHARDED:
        shp = shards[n].shape
        flat = (shp[0] * shp[1], shp[2])
        res = None
        for l in range(depth):
            row = l * shp[1]
            for key in sorted(k for k in plan.landed if k[:2] == ("d" + n, l)):
                parts = plan.landed[key]
                res = _adamw(parts, shards[n].reshape(flat), moments_m[n].reshape(flat), moments_v[n].reshape(flat),
                             f"adamw_{n}_{l}_{key[2]}", row0=row, prev=res)
                row += parts.shape[1]
        out_g[n], out_d[n], out_m[n], out_v[n] = [r.reshape(shp) for r in res]

    small = ("norm1", "b_gate", "norm2", "conv_b")
    vecs = [jnp.stack([grads[l][n] for l in range(depth)]).reshape(-1) for n in small]
    vecs.append(dnf.reshape(-1))
    vecs.append(jnp.stack([grads[l]["conv_w"] for l in range(depth)]).reshape(-1))
    sizes = [v.shape[0] for v in vecs]
    flat = jnp.concatenate(vecs)
    n_small = sum(sizes[:-1])
    pad = (-flat.shape[0]) % 1024
    flat = jnp.pad(flat, (0, pad)).reshape(-1, 128)
    allp = _all_gather(flat, "gather_small_grads").reshape(N_DEV, -1)
    rep_w = jnp.concatenate([norm1.reshape(-1), b_gate.reshape(-1), norm2.reshape(-1), conv_b.reshape(-1), norm_f])
    rep_m = jnp.concatenate([moments_m[n].reshape(-1) for n in small] + [m_norm_f])
    rep_v = jnp.concatenate([moments_v[n].reshape(-1) for n in small] + [v_norm_f])
    rows = n_small // 128
    res = _adamw(allp[:, :n_small].reshape(N_DEV, rows, 128), rep_w.reshape(rows, 128), rep_m.reshape(rows, 128),
                 rep_v.reshape(rows, 128), "adamw_small")
    off = 0
    for n, sz in zip(small + ("norm_f",), sizes[:-1]):
        shp = norm_f.shape if n == "norm_f" else p[n].shape
        out_g[n], out_d[n], out_m[n], out_v[n] = [r.reshape(-1)[off:off + sz].reshape(shp) for r in res]
        off += sz
    f = conv_w.shape[2]
    cwp = allp[:, n_small:n_small + sizes[-1]].reshape(N_DEV, depth * 3, D_FF)
    cwp = lax.dynamic_slice_in_dim(cwp, me * f, f, axis=2)
    res = _adamw(cwp, conv_w.reshape(depth * 3, f), m_conv_w.reshape(depth * 3, f), v_conv_w.reshape(depth * 3, f),
                 "adamw_conv_w")
    out_g["conv_w"], out_d["conv_w"], out_m["conv_w"], out_v["conv_w"] = [r.reshape(conv_w.shape) for r in res]

    order = ("norm1", "w_in", "b_gate", "w_br", "w_o", "norm2", "w_up", "conv_w", "conv_b", "w_down", "norm_f")
    return (loss, grad_x, *[out_g[n] for n in order], *[out_d[n] for n in order],
            *[out_m[n] for n in order], *[out_v[n] for n in order])
```

```python
import functools

import jax
import jax.numpy as jnp
from jax import lax
from jax.experimental import pallas as pl
from jax.experimental.pallas import tpu as pltpu

BF = jnp.bfloat16
F32 = jnp.float32

N_DEV = 8
D_MODEL = 1024
HEAD_DIM = 64
DSW_GROUPS = ((128, 1), (512, 4), (2048, 16))
HEADS_PER_GROUP = 4
N_GROUPS = len(DSW_GROUPS)
DSW_HEADS = HEADS_PER_GROUP * N_GROUPS
SB_HEADS = 4
W_A = DSW_HEADS * HEAD_DIM
W_B = SB_HEADS * HEAD_DIM
OUT_A = HEADS_PER_GROUP * HEAD_DIM
N_IN = 3 * W_A + 3 * W_B + 2 * D_MODEL
GATE_OFF = 3 * W_A + 3 * W_B
D_FF = 2816
SB_BLOCK = 256
RMS_EPS = 1e-6
ATT_SCALE = HEAD_DIM ** -0.5
NEG = -1e30
SB_EXIT = -110.0
FFN_CHUNK = 256

ADAM_LR = 0.001
ADAM_B1 = 0.9
ADAM_B2 = 0.999
ADAM_EPS = 1e-08
ADAM_WD = 0.01
ADAM_STEP = 10

HBM_SPEC = pl.BlockSpec(memory_space=pltpu.HBM)
MESH = pl.DeviceIdType.MESH

_NN = (((1,), (0,)), ((), ()))
_NT = (((1,), (1,)), ((), ()))
_TN = (((0,), (0,)), ((), ()))


def _dot(a, b, dn=_NN):
    return lax.dot_general(a, b, dn, preferred_element_type=F32)


def _pick(dim, pref):
    if dim <= pref:
        return dim
    t = (pref // 128) * 128
    while t >= 128:
        if dim % t == 0:
            return t
        t -= 128
    return dim


def _params(*sem):
    return pltpu.CompilerParams(dimension_semantics=sem)


def _peer(k):
    x, y, c = lax.axis_index("x"), lax.axis_index("y"), lax.axis_index("c")
    px = 1 - x if (k >> 2) & 1 else x
    py = 1 - y if (k >> 1) & 1 else y
    pc = 1 - c if k & 1 else c
    return (px, py, pc), 4 * px + 2 * py + pc


def _exchange(kind, x_ref, out_ref, send_sems, recv_sems, local_sem):
    gather = kind == "gather"
    _, me = _peer(0)

    def src(idx):
        return x_ref if gather else x_ref.at[idx]

    def copy(k, dst_idx):
        peer, pidx = _peer(k)
        return pltpu.make_async_remote_copy(
            src_ref=src(pidx), dst_ref=out_ref.at[dst_idx], send_sem=send_sems.at[k - 1],
            recv_sem=recv_sems.at[k - 1], device_id=peer, device_id_type=MESH)

    mine = pltpu.make_async_copy(src(me), out_ref.at[me], local_sem)

    def start():
        mine.start()
        for k in range(1, N_DEV):
            copy(k, me).start()

    def wait():
        for k in range(1, N_DEV):
            copy(k, _peer(k)[1]).wait_recv()
        for k in range(1, N_DEV):
            copy(k, me).wait_send()
        mine.wait()

    return start, wait


_EXCHANGE_SEMS = [pltpu.SemaphoreType.DMA((N_DEV - 1,)), pltpu.SemaphoreType.DMA((N_DEV - 1,)),
                  pltpu.SemaphoreType.DMA]


def _exchange_shape(kind, x):
    return jax.ShapeDtypeStruct(((N_DEV,) + x.shape) if kind == "gather" else x.shape, x.dtype)


def _exchange_alone(kind, x, name):
    def body(x_ref, out_ref, send_sems, recv_sems, local_sem):
        start, wait = _exchange(kind, x_ref, out_ref, send_sems, recv_sems, local_sem)
        start()
        wait()

    return pl.pallas_call(
        body, name=name, out_shape=_exchange_shape(kind, x),
        in_specs=[HBM_SPEC], out_specs=HBM_SPEC, scratch_shapes=list(_EXCHANGE_SEMS),
    )(x)


def _all_gather_via_sibling(x, name):
    def body(x_ref, out_ref, send_sems, recv_sems, local_sem):
        x_, y_, c_ = lax.axis_index("x"), lax.axis_index("y"), lax.axis_index("c")
        me, sibling = (x_, y_, c_), (x_, y_, 1 - c_)
        chips = [(1 - x_, y_), (x_, 1 - y_), (1 - x_, 1 - y_)]

        def slot(px, py, pc):
            return out_ref.at[4 * px + 2 * py + pc]

        def copy(k, block, to, src=None):
            return pltpu.make_async_remote_copy(
                src_ref=slot(*block) if src is None else src, dst_ref=slot(*block), send_sem=send_sems.at[k],
                recv_sem=recv_sems.at[k], device_id=to, device_id_type=MESH)

        mine = pltpu.make_async_copy(x_ref, slot(*me), local_sem)
        mine.start()
        first = [copy(0, me, sibling, src=x_ref)]
        first += [copy(1 + j, me, (*chip, c_), src=x_ref) for j, chip in enumerate(chips)]
        for cp in first:
            cp.start()
        passed = [copy(4 + j, (*chip, c_), sibling) for j, chip in enumerate(chips)]
        for j, chip in enumerate(chips):
            copy(1 + j, (*chip, c_), me).wait_recv()
            passed[j].start()
        copy(0, sibling, me).wait_recv()
        for j, chip in enumerate(chips):
            copy(4 + j, (*chip, 1 - c_), me).wait_recv()
        for cp in first + passed:
            cp.wait_send()
        mine.wait()

    return pl.pallas_call(
        body, name=name, out_shape=_exchange_shape("gather", x),
        in_specs=[HBM_SPEC], out_specs=HBM_SPEC,
        scratch_shapes=[pltpu.SemaphoreType.DMA((N_DEV - 1,)), pltpu.SemaphoreType.DMA((N_DEV - 1,)),
                        pltpu.SemaphoreType.DMA],
    )(x)


def _all_gather(x, name):
    return _exchange_alone("gather", x, name)


def _all_to_all(x, name):
    return _exchange_alone("scatter", x, name)


def _call(body, *, name, grid, in_specs, out_specs, out_shape, args, scratch_shapes=(), sem=None, comm=()):
    single = not isinstance(out_shape, (tuple, list))
    outs = (out_shape,) if single else tuple(out_shape)
    ospecs = (out_specs,) if single else tuple(out_specs)
    if not comm:
        res = pl.pallas_call(
            body, name=name, out_shape=outs, grid=grid, in_specs=list(in_specs), out_specs=ospecs,
            scratch_shapes=list(scratch_shapes), compiler_params=_params(*(sem or ("arbitrary",) * len(grid))),
        )(*args)
        return res
    n_in, n_out, n_scr, nc = len(in_specs), len(outs), len(scratch_shapes), len(comm)

    def wrapped(*refs):
        ins = refs[:n_in]
        cins = refs[n_in:n_in + nc]
        o0 = n_in + nc
        kouts = refs[o0:o0 + n_out]
        couts = refs[o0 + n_out:o0 + n_out + nc]
        s0 = o0 + n_out + nc
        scr = refs[s0:s0 + n_scr]
        sems = refs[s0 + n_scr:]
        ids = [pl.program_id(ax) for ax in range(len(grid))]
        first = functools.reduce(jnp.logical_and, [i == 0 for i in ids])
        last = functools.reduce(jnp.logical_and, [i == g - 1 for i, g in zip(ids, grid)])
        ex = [_exchange(comm[c][0], cins[c], couts[c], *sems[3 * c:3 * c + 3]) for c in range(nc)]

        @pl.when(first)
        def _():
            for start, _ in ex:
                start()

        body(*ins, *kouts, *scr)

        @pl.when(last)
        def _():
            for _, wait in ex:
                wait()

    return pl.pallas_call(
        wrapped, name=name,
        out_shape=outs + tuple(_exchange_shape(k, x) for k, x in comm),
        grid=grid, in_specs=list(in_specs) + [HBM_SPEC] * nc, out_specs=ospecs + (HBM_SPEC,) * nc,
        scratch_shapes=list(scratch_shapes) + list(_EXCHANGE_SEMS) * nc,
        compiler_params=_params(*(("arbitrary",) * len(grid))),
    )(*args, *[x for _, x in comm])


def _matmul(a, b, *, mode, out_dtype, name, tm=512, tn=1024, tk=1024, res=None, comm=(), m_cols=None, m_off=0):
    if mode == "nn":
        (M, K), (_, N) = a.shape, b.shape
    elif mode == "nt":
        (M, K), (N, _) = a.shape, b.shape
    else:
        (K, M), (_, N) = a.shape, b.shape
        M = M if m_cols is None else m_cols
    tm, tn, tk = _pick(M, tm), _pick(N, tn), _pick(K, tk)
    i0 = m_off // tm
    nk = K // tk
    dn = {"nn": _NN, "nt": _NT, "tn": _TN}[mode]

    def body(*refs):
        a_ref, b_ref = refs[0], refs[1]
        r_ref = refs[2] if res is not None else None
        o_ref = refs[3] if res is not None else refs[2]

        def finish(r):
            if res is not None:
                r = r + r_ref[...].astype(F32)
            o_ref[...] = r.astype(out_dtype)

        part = _dot(a_ref[...].astype(BF), b_ref[...].astype(BF), dn)
        if nk == 1:
            finish(part)
            return
        acc = refs[-1]
        k = pl.program_id(2)

        @pl.when(k == 0)
        def _():
            acc[...] = jnp.zeros_like(acc)

        acc[...] += part

        @pl.when(k == nk - 1)
        def _():
            finish(acc[...])

    if mode == "tn":
        a_spec = pl.BlockSpec((tk, tm), lambda j, i, k: (k, i0 + i))
    else:
        a_spec = pl.BlockSpec((tm, tk), lambda j, i, k: (i, k))
    if mode == "nt":
        b_spec = pl.BlockSpec((tn, tk), lambda j, i, k: (j, k))
    else:
        b_spec = pl.BlockSpec((tk, tn), lambda j, i, k: (k, j))
    o_spec = pl.BlockSpec((tm, tn), lambda j, i, k: (i, j))
    in_specs = [a_spec, b_spec] + ([o_spec] if res is not None else [])
    args = (a, b) + ((res,) if res is not None else ())
    out = _call(body, name=name, grid=(N // tn, M // tm, nk), in_specs=in_specs, out_specs=o_spec,
                out_shape=jax.ShapeDtypeStruct((M, N), out_dtype), args=args,
                scratch_shapes=[pltpu.VMEM((tm, tn), F32)] if nk > 1 else [],
                sem=("parallel", "parallel", "arbitrary"), comm=comm)
    return out[0], list(out[1:])


def _norm_matmul(x, g, w, *, name, tm=512, tn=1024, comm=()):
    S, K = x.shape
    N = w.shape[1]
    tm, tn = _pick(S, tm), _pick(N, tn)
    slab = min(tm, 256)

    def body(x_ref, g_ref, w_ref, o_ref, h_ref):
        gg = g_ref[...]
        for r0 in range(0, tm, slab):
            rs = slice(r0, r0 + slab)
            xf = x_ref[rs, :]
            r = lax.rsqrt(jnp.mean(xf * xf, axis=-1, keepdims=True) + RMS_EPS)
            hh = (xf * r * gg).astype(BF)
            h_ref[rs, :] = hh
            for c0 in range(0, N, tn):
                o_ref[rs, c0:c0 + tn] = _dot(hh, w_ref[:, c0:c0 + tn]).astype(BF)

    res = _call(body, name=name, grid=(S // tm,),
                in_specs=[pl.BlockSpec((tm, K), lambda i: (i, 0)), pl.BlockSpec((1, K), lambda i: (0, 0)),
                          pl.BlockSpec((K, N), lambda i: (0, 0))],
                out_specs=(pl.BlockSpec((tm, N), lambda i: (i, 0)), pl.BlockSpec((tm, K), lambda i: (i, 0))),
                out_shape=(jax.ShapeDtypeStruct((S, N), BF), jax.ShapeDtypeStruct((S, K), BF)),
                args=(x, g.reshape(1, K), w), comm=comm)
    return res[0], res[1], list(res[2:])


def _matmul_rms_bwd(dy, w, x, g, dres, *, name, tm=512, comm=()):
    S, K = dy.shape
    D = w.shape[0]
    tm = _pick(S, tm)

    def body(dy_ref, w_ref, x_ref, g_ref, dres_ref, dx_ref, dg_ref):
        @pl.when(pl.program_id(0) == 0)
        def _():
            dg_ref[...] = jnp.zeros_like(dg_ref)

        dh = _dot(dy_ref[...].astype(BF), w_ref[...], _NT)
        xf = x_ref[...]
        r = lax.rsqrt(jnp.mean(xf * xf, axis=-1, keepdims=True) + RMS_EPS)
        xh = xf * r
        dg_ref[...] += jnp.sum(dh * xh, axis=0, keepdims=True)
        dxh = dh * g_ref[...]
        dx_ref[...] = dres_ref[...] + r * (dxh - xh * jnp.mean(dxh * xh, axis=-1, keepdims=True))

    row = pl.BlockSpec((tm, D), lambda i: (i, 0))
    vec = pl.BlockSpec((1, D), lambda i: (0, 0))
    res = _call(body, name=name, grid=(S // tm,),
                in_specs=[pl.BlockSpec((tm, K), lambda i: (i, 0)), pl.BlockSpec((D, K), lambda i: (0, 0)), row, vec, row],
                out_specs=(row, vec),
                out_shape=(jax.ShapeDtypeStruct((S, D), F32), jax.ShapeDtypeStruct((1, D), F32)),
                args=(dy, w, x, g.reshape(1, D), dres), comm=comm)
    return res[0], res[1], list(res[2:])


def _loss_head(x, g, target):
    S, D = x.shape
    tm = _pick(S, 512)

    def body(x_ref, g_ref, t_ref, loss_ref, dx_ref, dg_ref):
        @pl.when(pl.program_id(0) == 0)
        def _():
            dg_ref[...] = jnp.zeros_like(dg_ref)
            loss_ref[...] = jnp.zeros_like(loss_ref)

        xf = x_ref[...]
        gg = g_ref[...]
        r = lax.rsqrt(jnp.mean(xf * xf, axis=-1, keepdims=True) + RMS_EPS)
        xh = xf * r
        err = xh * gg - t_ref[...]
        per_tok = jnp.mean(err * err, axis=-1, keepdims=True)
        loss_ref[...] += 0.5 * jnp.sum(per_tok, axis=0, keepdims=True)
        dy = err * (1.0 / D)
        dg_ref[...] += jnp.sum(dy * xh, axis=0, keepdims=True)
        dxh = dy * gg
        dx_ref[...] = r * (dxh - xh * jnp.mean(dxh * xh, axis=-1, keepdims=True))

    row = pl.BlockSpec((tm, D), lambda i: (i, 0))
    vec = pl.BlockSpec((1, D), lambda i: (0, 0))
    one = pl.BlockSpec((1, 1), lambda i: (0, 0))
    return pl.pallas_call(
        body, name="loss_head",
        out_shape=(jax.ShapeDtypeStruct((1, 1), F32), jax.ShapeDtypeStruct((S, D), F32),
                   jax.ShapeDtypeStruct((1, D), F32)),
        grid=(S // tm,),
        in_specs=[row, vec, row], out_specs=(one, row, vec),
        compiler_params=_params("arbitrary"),
    )(x, g.reshape(1, D), target)


def _slopes(g):
    return [2.0 ** (-8.0 * (HEADS_PER_GROUP * g + j + 1) / DSW_HEADS) for j in range(HEADS_PER_GROUP)]


def _band_masks(W):
    row = lax.broadcasted_iota(jnp.int32, (W, W), 0)
    col = lax.broadcasted_iota(jnp.int32, (W, W), 1)
    d_cur = row - col
    d_prev = d_cur + W
    return d_cur, d_prev, d_cur >= 0, d_cur <= 0


def _band_specs(W, nb, per):
    def cur(c):
        return pl.BlockSpec((None, per * W, OUT_A), lambda r, n: (r, n, c))

    def prev(c):
        return pl.BlockSpec((None, W, OUT_A), lambda r, n: (r, jnp.maximum(per * n - 1, 0), c))

    def nxt(c):
        return pl.BlockSpec((None, W, OUT_A), lambda r, n: (r, jnp.minimum(per * (n + 1), nb - 1), c))

    return cur, prev, nxt


def _blocks_per_step(nb):
    return 8 if nb % 8 == 0 else 4 if nb % 4 == 0 else 2 if nb % 2 == 0 else 1


def _head_stack(W):
    H, hd = HEADS_PER_GROUP, HEAD_DIM
    lane_head = lax.broadcasted_iota(jnp.int32, (W, OUT_A), 1) // hd

    def stack(x):
        return jnp.concatenate([jnp.where(lane_head == h, x, jnp.zeros_like(x)) for h in range(H)], axis=0)

    def unstack(y):
        out = jnp.where(lane_head == 0, y[0:W], 0.0)
        for h in range(1, H):
            out = jnp.where(lane_head == h, y[h * W:(h + 1) * W], out)
        return out

    def column(ref, rows, off=0):
        return jnp.concatenate([ref[rows, h * hd + off:h * hd + off + 1] for h in range(H)], axis=0)

    return stack, unstack, column


def _stacked_bias(W, slopes, dil):
    d_cur, d_prev, m_cur, m_prev = _band_masks(W)
    b_cur = jnp.concatenate([(s * dil) * d_cur.astype(F32) for s in slopes], axis=0)
    b_prev = jnp.concatenate([(s * dil) * d_prev.astype(F32) for s in slopes], axis=0)
    H = len(slopes)
    return b_cur, b_prev, jnp.concatenate([m_cur] * H, axis=0), jnp.concatenate([m_prev] * H, axis=0)


def _attn_a_fwd(qkv, cols, g, comm=()):
    win, dil = DSW_GROUPS[g]
    W = win // dil
    d, L, _ = qkv[0].shape
    nb = L // W
    per = _blocks_per_step(nb)
    slopes = _slopes(g)

    def body(q_ref, kp_ref, kc_ref, vp_ref, vc_ref, o_ref, l_ref):
        n = pl.program_id(1)
        stack, unstack, _ = _head_stack(W)
        b_cur, b_prev, m_cur, m_prev = _stacked_bias(W, slopes, dil)
        m_first = jnp.logical_and(m_prev, n > 0)
        for b in range(per):
            rows = slice(b * W, (b + 1) * W)
            before = slice((b - 1) * W, b * W)
            qs = stack(q_ref[rows, :])
            kc, vc = kc_ref[rows, :], vc_ref[rows, :]
            kp, vp = (kp_ref[...], vp_ref[...]) if b == 0 else (kc_ref[before, :], vc_ref[before, :])
            s_c = jnp.where(m_cur, _dot(qs, kc, _NT) * ATT_SCALE - b_cur, NEG)
            s_p = jnp.where(m_first if b == 0 else m_prev, _dot(qs, kp, _NT) * ATT_SCALE - b_prev, NEG)
            m = jnp.maximum(jnp.max(s_c, axis=1, keepdims=True), jnp.max(s_p, axis=1, keepdims=True))
            p_c = jnp.exp(s_c - m)
            p_p = jnp.exp(s_p - m)
            den = jnp.sum(p_c, axis=1, keepdims=True) + jnp.sum(p_p, axis=1, keepdims=True)
            pv = _dot(p_c.astype(BF), vc) + _dot(p_p.astype(BF), vp)
            o_ref[rows, :] = unstack(pv / den)
            l_ref[rows, :] = unstack(jnp.broadcast_to(m + jnp.log(den), pv.shape))

    cur, prev, _ = _band_specs(W, nb, per)
    out = jax.ShapeDtypeStruct((d, L, OUT_A), F32)
    res = _call(body, name=f"attn_a_fwd_g{g}", grid=(d, nb // per),
                in_specs=[cur(cols[0]), prev(cols[1]), cur(cols[1]), prev(cols[2]), cur(cols[2])],
                out_specs=(cur(0), cur(0)), out_shape=(out, out),
                args=(qkv[0], qkv[1], qkv[1], qkv[2], qkv[2]), sem=("parallel", "parallel"), comm=comm)
    return res[0], res[1], list(res[2:])


def _attn_a_bwd(qkv, cols, do, stats, g):
    win, dil = DSW_GROUPS[g]
    W = win // dil
    d, L, _ = qkv[0].shape
    nb = L // W
    per = _blocks_per_step(nb)
    nsteps = nb // per
    slopes = _slopes(g)

    def body(q_ref, qn_ref, kp_ref, kc_ref, vp_ref, vc_ref, do_ref, don_ref, st_ref, stn_ref,
             dq_ref, dk_ref, dv_ref):
        n = pl.program_id(1)
        stack, unstack, column = _head_stack(W)
        b_cur, b_prev, m_cur, m_prev = _stacked_bias(W, slopes, dil)
        m_first = jnp.logical_and(m_prev, n > 0)
        m_last = jnp.logical_and(m_prev, n < nsteps - 1)
        everything = slice(None)
        for b in range(per):
            rows = slice(b * W, (b + 1) * W)
            before = slice((b - 1) * W, b * W)
            after = slice((b + 1) * W, (b + 2) * W)
            first, last = b == 0, b == per - 1
            qs = stack(q_ref[rows, :])
            qn = stack(qn_ref[...] if last else q_ref[after, :])
            dos = stack(do_ref[rows, :])
            don = stack(don_ref[...] if last else do_ref[after, :])
            kc, vc = kc_ref[rows, :], vc_ref[rows, :]
            kp, vp = (kp_ref[...], vp_ref[...]) if first else (kc_ref[before, :], vc_ref[before, :])
            lse_c, dsum_c = column(st_ref, rows), column(st_ref, rows, STAT_OFF)
            lse_n = column(stn_ref, everything) if last else column(st_ref, after)
            dsum_n = column(stn_ref, everything, STAT_OFF) if last else column(st_ref, after, STAT_OFF)
            m_p = m_first if first else m_prev
            m_n = m_last if last else m_prev
            p_cc = jnp.exp(jnp.where(m_cur, _dot(qs, kc, _NT) * ATT_SCALE - b_cur, NEG) - lse_c)
            p_cp = jnp.exp(jnp.where(m_p, _dot(qs, kp, _NT) * ATT_SCALE - b_prev, NEG) - lse_c)
            p_nc = jnp.exp(jnp.where(m_n, _dot(qn, kc, _NT) * ATT_SCALE - b_prev, NEG) - lse_n)
            ds_cc = (p_cc * (_dot(dos, vc, _NT) - dsum_c) * ATT_SCALE).astype(BF)
            ds_cp = (p_cp * (_dot(dos, vp, _NT) - dsum_c) * ATT_SCALE).astype(BF)
            ds_nc = (p_nc * (_dot(don, vc, _NT) - dsum_n) * ATT_SCALE).astype(BF)
            dq_ref[rows, :] = unstack(_dot(ds_cc, kc) + _dot(ds_cp, kp)).astype(BF)
            dk_ref[rows, :] = (_dot(ds_cc, qs, _TN) + _dot(ds_nc, qn, _TN)).astype(BF)
            dv_ref[rows, :] = (_dot(p_cc.astype(BF), dos, _TN) + _dot(p_nc.astype(BF), don, _TN)).astype(BF)

    cur, prev, nxt = _band_specs(W, nb, per)
    out = jax.ShapeDtypeStruct((d, L, OUT_A), BF)
    cq, ck, cv = cols
    return pl.pallas_call(
        body, name=f"attn_a_bwd_g{g}",
        out_shape=(out, out, out),
        grid=(d, nsteps),
        in_specs=[cur(cq), nxt(cq), prev(ck), cur(ck), prev(cv), cur(cv), cur(0), nxt(0), cur(0), nxt(0)],
        out_specs=(cur(0), cur(0), cur(0)),
        compiler_params=_params("parallel", "parallel"),
    )(qkv[0], qkv[0], qkv[1], qkv[1], qkv[2], qkv[2], do, do, stats, stats)


SB_PAIR = 2
SB_QUAD = 4
SB_QROWS = SB_BLOCK


def _softplus_parts(z):
    e = jnp.exp(-jnp.abs(z))
    log1p_e = jnp.where(e < 1e-4, e, jnp.log(1.0 + e))
    return e, jnp.maximum(z, 0.0) + log1p_e


def _split_dot(x, t):
    hi = x.astype(BF)
    lo = (x - hi.astype(F32)).astype(BF)
    return _dot(hi, t) + _dot(lo, t)


def _sb_block(qh, kk, causal, r_run, tri_incl):
    z = _dot(qh, kk, _NT)
    e, sp = _softplus_parts(z)
    ls = -sp if causal is None else jnp.where(causal, -sp, 0.0)
    cin = _split_dot(ls, tri_incl)
    a = jnp.exp(z + cin + r_run)
    if causal is not None:
        a = jnp.where(causal, a, 0.0)
    return z, e, cin, a


def _sb_specs(S, pair=SB_PAIR):
    Q, hd = SB_QROWS, HEAD_DIM
    lanes = pair * hd
    qc = (3 * W_A) // lanes
    kc = (3 * W_A + W_B) // lanes
    vc = (3 * W_A + 2 * W_B) // lanes
    q_spec = pl.BlockSpec((Q, lanes), lambda p, i: (i, qc + p))
    k_spec = pl.BlockSpec((S, lanes), lambda p, i: (0, kc + p))
    v_spec = pl.BlockSpec((S, lanes), lambda p, i: (0, vc + p))
    o_spec = pl.BlockSpec((Q, lanes), lambda p, i: (i, p))
    full = pl.BlockSpec((S, lanes), lambda p, i: (0, p))
    return q_spec, k_spec, v_spec, o_spec, full


def _sb_stack(pair=SB_PAIR):
    Q, hd = SB_QROWS, HEAD_DIM
    lane_head = lax.broadcasted_iota(jnp.int32, (Q, pair * hd), 1) // hd

    def stack(x):
        return jnp.concatenate([jnp.where(lane_head == h, x, jnp.zeros_like(x)) for h in range(pair)], axis=0)

    def unstack(y):
        out = jnp.where(lane_head == 0, y[0:Q], 0.0)
        for h in range(1, pair):
            out = jnp.where(lane_head == h, y[h * Q:(h + 1) * Q], out)
        return out

    return stack, unstack


def _sb_iotas(i, pair=SB_PAIR):
    B, Q = SB_BLOCK, SB_QROWS
    row = lax.broadcasted_iota(jnp.int32, (Q, B), 0) + i * Q
    col = lax.broadcasted_iota(jnp.int32, (Q, B), 1)
    ahead = jnp.concatenate([col - row] * pair, axis=0)
    tr = lax.broadcasted_iota(jnp.int32, (B, B), 0)
    tc = lax.broadcasted_iota(jnp.int32, (B, B), 1)
    return ahead, tr, tc


def _sb_fwd(proj, comm=()):
    S = proj.shape[0]
    B, Q, hd = SB_BLOCK, SB_QROWS, HEAD_DIM
    nq = S // Q
    P = SB_QUAD
    R = P * Q
    q_spec, k_spec, v_spec, o_spec, _ = _sb_specs(S, P)

    def body(q_ref, k_ref, v_ref, o_ref):
        i = pl.program_id(1)
        ahead, tr, tc = _sb_iotas(i, P)
        tri_incl = (tr >= tc).astype(BF)
        stack, unstack = _sb_stack(P)
        qs = stack(q_ref[...] * ATT_SCALE)

        def cond(c):
            return jnp.logical_and(c[0] >= 0, c[-1] > SB_EXIT)

        def step(c, diagonal):
            kb, r_run, acc, _ = c
            off = pl.multiple_of(kb * B, B)
            causal = ahead < -kb * B if diagonal else None
            _, _, cin, a = _sb_block(qs, k_ref[pl.ds(off, B), :], causal, r_run, tri_incl)
            acc = acc + _dot(a.astype(BF), v_ref[pl.ds(off, B), :])
            r_run = r_run + cin[:, 0:1]
            return kb - 1, r_run, acc, jnp.max(r_run)

        init = (i, jnp.zeros((R, 1), F32), jnp.zeros((R, P * hd), F32), jnp.float32(0.0))
        fin = lax.while_loop(cond, functools.partial(step, diagonal=False), step(init, True))
        o_ref[...] = unstack(fin[2])

    res = _call(body, name="sb_fwd", grid=(SB_HEADS // P, nq), in_specs=[q_spec, k_spec, v_spec],
                out_specs=o_spec, out_shape=jax.ShapeDtypeStruct((S, W_B), F32), args=(proj, proj, proj),
                sem=("parallel", "parallel"), comm=comm)
    return res[0], list(res[1:])


def _sb_bwd(proj, do, o, comm=()):
    S = proj.shape[0]
    B, Q, hd = SB_BLOCK, SB_QROWS, HEAD_DIM
    nq = S // Q
    R = SB_PAIR * Q
    q_spec, k_spec, v_spec, o_spec, full = _sb_specs(S)

    def body(q_ref, k_ref, v_ref, do_ref, o_ref, dq_ref, dk_ref, dv_ref):
        i = pl.program_id(1)

        @pl.when(i == 0)
        def _():
            dk_ref[...] = jnp.zeros_like(dk_ref)
            dv_ref[...] = jnp.zeros_like(dv_ref)

        ahead, tr, tc = _sb_iotas(i)
        tri_incl = (tr >= tc).astype(BF)
        tri_strict = (tr > tc).astype(BF)
        stack, unstack = _sb_stack()
        qs = stack(q_ref[...] * ATT_SCALE)
        dobs = stack(do_ref[...])
        o_all = o_ref[...]
        dsum = jnp.sum(dobs.astype(F32) * jnp.concatenate([o_all] * SB_PAIR, axis=0), axis=1, keepdims=True)

        def cond(c):
            return jnp.logical_and(c[0] >= 0, c[-1] > SB_EXIT)

        def step(c, diagonal):
            kb, r_run, g_run, dq, _ = c
            off = pl.multiple_of(kb * B, B)
            causal = ahead < -kb * B if diagonal else None
            kk = k_ref[pl.ds(off, B), :]
            vv = v_ref[pl.ds(off, B), :]
            z, e, cin, a = _sb_block(qs, kk, causal, r_run, tri_incl)
            a16 = a.astype(BF)
            gmat = a16.astype(F32) * _dot(dobs, vv, _NT)
            later = _split_dot(gmat, tri_strict)
            pfx = dsum - g_run - later
            sig = jnp.where(z >= 0, 1.0, e) / (1.0 + e)
            dz = gmat - sig * pfx
            dz = (dz if causal is None else jnp.where(causal, dz, 0.0)).astype(BF)
            dq = dq + _dot(dz, kk)
            dk_ref[pl.ds(off, B), :] += _dot(dz, qs, _TN)
            dv_ref[pl.ds(off, B), :] += _dot(a16, dobs, _TN)
            g_run = g_run + jnp.sum(gmat, axis=1, keepdims=True)
            r_run = r_run + cin[:, 0:1]
            return kb - 1, r_run, g_run, dq, jnp.max(r_run)

        init = (i, jnp.zeros((R, 1), F32), jnp.zeros((R, 1), F32), jnp.zeros((R, SB_PAIR * hd), F32),
                jnp.float32(0.0))
        fin = lax.while_loop(cond, functools.partial(step, diagonal=False), step(init, True))
        dq_ref[...] = unstack(fin[3]) * ATT_SCALE

    out = jax.ShapeDtypeStruct((S, W_B), F32)
    res = _call(body, name="sb_bwd", grid=(SB_HEADS // SB_PAIR, nq), in_specs=[q_spec, k_spec, v_spec, o_spec, o_spec],
                out_specs=(o_spec, full, full), out_shape=(out, out, out), args=(proj, proj, proj, do, o), comm=comm)
    return res[0], res[1], res[2], list(res[3:])


def _merge_wo(o_g, l_g, o_b, proj, b_gate, w_br, w_o, x, name, comm=()):
    S = o_b.shape[0]
    D = D_MODEL
    tm = _pick(S, 256)
    gcol = GATE_OFF // D

    def body(o0, o1, o2, l0, l1, l2, ob_ref, ga_ref, gb_ref, bg_ref, w_ref, wo_ref, x_ref,
             x1_ref, mg_ref, oa_ref, lse_ref):
        la, lb, lc = l0[...], l1[...], l2[...]
        mx = jnp.maximum(jnp.maximum(la, lb), lc)
        ea, eb, ec = jnp.exp(la - mx), jnp.exp(lb - mx), jnp.exp(lc - mx)
        den = ea + eb + ec
        oa = (ea * o0[...] + eb * o1[...] + ec * o2[...]) / den
        oa_ref[...] = oa
        lse_ref[...] = mx + jnp.log(den)
        oa16 = oa.astype(BF)
        ob16 = ob_ref[...].astype(BF)
        acc = x_ref[...]
        for c0 in range(0, D, FFN_CHUNK):
            cs = slice(c0, c0 + FFN_CHUNK)
            ya = _dot(oa16, w_ref[0:OUT_A, cs])
            yb = _dot(ob16, w_ref[OUT_A:OUT_A + W_B, cs])
            g_a = jax.nn.sigmoid(ga_ref[:, cs].astype(F32) + bg_ref[:, cs])
            g_b = jax.nn.sigmoid(gb_ref[:, cs].astype(F32) + bg_ref[:, D + c0:D + c0 + FFN_CHUNK])
            mg = (g_a * ya + g_b * yb).astype(BF)
            mg_ref[:, cs] = mg
            acc = acc + _dot(mg, wo_ref[cs, :])
        x1_ref[...] = acc

    nar = pl.BlockSpec((tm, OUT_A), lambda i: (i, 0))
    wide = pl.BlockSpec((tm, D), lambda i: (i, 0))
    res = _call(body, name=name, grid=(S // tm,),
                in_specs=[nar] * 7 + [pl.BlockSpec((tm, D), lambda i: (i, gcol)),
                                      pl.BlockSpec((tm, D), lambda i: (i, gcol + 1)),
                                      pl.BlockSpec((1, 2 * D), lambda i: (0, 0)),
                                      pl.BlockSpec((OUT_A + W_B, D), lambda i: (0, 0)),
                                      pl.BlockSpec((D, D), lambda i: (0, 0)), wide],
                out_specs=(wide, wide, nar, nar),
                out_shape=(jax.ShapeDtypeStruct((S, D), F32), jax.ShapeDtypeStruct((S, D), BF),
                           jax.ShapeDtypeStruct((S, OUT_A), F32), jax.ShapeDtypeStruct((S, OUT_A), F32)),
                args=(*o_g, *l_g, o_b, proj, proj, b_gate.reshape(1, 2 * D), w_br, w_o, x), comm=comm)
    return res[0], res[1], res[2], res[3], list(res[4:])


STAT_OFF = HEAD_DIM // 2


def _merge_bwd(dx, w_o, oa, ob, lse, proj, b_gate, w_br):
    S = ob.shape[0]
    D = D_MODEL
    tm = _pick(S, 256)
    gcol = GATE_OFF // D

    def body(dx_ref, wo_ref, oa_ref, ob_ref, l_ref, ga_ref, gb_ref, bg_ref, w_ref,
             dya_ref, dyb_ref, doa_ref, dob_ref, st_ref, dg_ref, dbg_ref):
        @pl.when(pl.program_id(0) == 0)
        def _():
            dbg_ref[...] = jnp.zeros_like(dbg_ref)

        dx16 = dx_ref[...].astype(BF)
        oa = oa_ref[...]
        oa16 = oa.astype(BF)
        ob16 = ob_ref[...].astype(BF)
        doa = jnp.zeros((tm, OUT_A), F32)
        dob = jnp.zeros((tm, W_B), F32)
        for c0 in range(0, D, FFN_CHUNK):
            cs = slice(c0, c0 + FFN_CHUNK)
            cs2 = slice(D + c0, D + c0 + FFN_CHUNK)
            wa = w_ref[0:OUT_A, cs]
            wb = w_ref[OUT_A:OUT_A + W_B, cs]
            dm = _dot(dx16, wo_ref[cs, :], _NT)
            ya = _dot(oa16, wa)
            yb = _dot(ob16, wb)
            g_a = jax.nn.sigmoid(ga_ref[:, cs].astype(F32) + bg_ref[:, cs])
            g_b = jax.nn.sigmoid(gb_ref[:, cs].astype(F32) + bg_ref[:, cs2])
            dga = dm * ya * g_a * (1.0 - g_a)
            dgb = dm * yb * g_b * (1.0 - g_b)
            dg_ref[:, cs] = dga.astype(BF)
            dg_ref[:, cs2] = dgb.astype(BF)
            dbg_ref[:, cs] += jnp.sum(dga, axis=0, keepdims=True)
            dbg_ref[:, cs2] += jnp.sum(dgb, axis=0, keepdims=True)
            dya = (dm * g_a).astype(BF)
            dyb = (dm * g_b).astype(BF)
            dya_ref[:, cs] = dya
            dyb_ref[:, cs] = dyb
            doa = doa + _dot(dya, wa, _NT)
            dob = dob + _dot(dyb, wb, _NT)
        doa = doa.astype(BF)
        doa_ref[...] = doa
        dob_ref[...] = dob.astype(BF)
        r = lax.broadcasted_iota(jnp.int32, (OUT_A, OUT_A), 0) // HEAD_DIM
        c = lax.broadcasted_iota(jnp.int32, (OUT_A, OUT_A), 1) // HEAD_DIM
        dsum = _split_dot(doa.astype(F32) * oa, (r == c).astype(BF))
        lane = lax.broadcasted_iota(jnp.int32, dsum.shape, 1) % HEAD_DIM
        st_ref[...] = jnp.where(lane < STAT_OFF, l_ref[...], dsum)

    nar = pl.BlockSpec((tm, OUT_A), lambda i: (i, 0))
    wide = pl.BlockSpec((tm, D), lambda i: (i, 0))
    wide2 = pl.BlockSpec((tm, 2 * D), lambda i: (i, 0))
    vec2 = pl.BlockSpec((1, 2 * D), lambda i: (0, 0))
    return pl.pallas_call(
        body, name="merge_bwd",
        out_shape=(jax.ShapeDtypeStruct((S, D), BF), jax.ShapeDtypeStruct((S, D), BF),
                   jax.ShapeDtypeStruct((S, OUT_A), BF), jax.ShapeDtypeStruct((S, W_B), BF),
                   jax.ShapeDtypeStruct((S, OUT_A), F32), jax.ShapeDtypeStruct((S, 2 * D), BF),
                   jax.ShapeDtypeStruct((1, 2 * D), F32)),
        grid=(S // tm,),
        in_specs=[wide, pl.BlockSpec((D, D), lambda i: (0, 0)), nar, nar, nar,
                  pl.BlockSpec((tm, D), lambda i: (i, gcol)), pl.BlockSpec((tm, D), lambda i: (i, gcol + 1)), vec2,
                  pl.BlockSpec((OUT_A + W_B, D), lambda i: (0, 0))],
        out_specs=(wide, wide, nar, nar, nar, wide2, vec2),
        compiler_params=_params("arbitrary"),
    )(dx, w_o, oa, ob, lse, proj, proj, b_gate.reshape(1, 2 * D), w_br)


_SQRT_HALF = 0.7071067811865476
_INV_SQRT_2PI = 0.3989422804014327


def _gelu_parts(a):
    cdf = 0.5 * (1.0 + lax.erf(a * _SQRT_HALF))
    pdf = _INV_SQRT_2PI * jnp.exp(-0.5 * a * a)
    return cdf, pdf


def _shift_down(a, halo, k):
    rows = lax.broadcasted_iota(jnp.int32, a.shape, 0)
    out = pltpu.roll(a, k, 0)
    for r in range(k):
        out = jnp.where(rows == r, halo[8 - k + r:8 - k + r + 1, :], out)
    return out


def _ffn_specs(S, tm):
    F = D_FF
    t8 = tm // 8
    a_spec = pl.BlockSpec((tm, F), lambda i: (i, 0))
    v_spec = pl.BlockSpec((tm, F), lambda i: (i, 1))
    halo_prev = pl.BlockSpec((8, F), lambda i: (jnp.maximum(i * t8 - 1, 0), 0))
    return a_spec, v_spec, halo_prev


def _ffn_down(up, conv_w, conv_b, w_down, res, *, name, comm=()):
    S = up.shape[0]
    F = D_FF
    D = w_down.shape[1]
    tm = _pick(S, 256)
    a_spec, v_spec, halo_prev = _ffn_specs(S, tm)

    def body(a_ref, h_ref, v_ref, w_ref, b_ref, wd_ref, r_ref, o_ref, act_ref, ac_ref):
        first = pl.program_id(0) == 0
        acc = r_ref[...]
        for c0 in range(0, F, FFN_CHUNK):
            cs = slice(c0, c0 + FFN_CHUNK)
            a = a_ref[:, cs].astype(F32)
            halo = jnp.where(first, 0.0, h_ref[:, cs].astype(F32))
            w = w_ref[:, cs]
            ac = b_ref[:, cs] + w[0:1, :] * _shift_down(a, halo, 2) + w[1:2, :] * _shift_down(a, halo, 1) + w[2:3, :] * a
            ac_ref[:, cs] = ac.astype(BF)
            cdf, _ = _gelu_parts(ac)
            act = (ac * cdf * v_ref[:, cs].astype(F32)).astype(BF)
            act_ref[:, cs] = act
            acc = acc + _dot(act, wd_ref[cs, :])
        o_ref[...] = acc

    row = pl.BlockSpec((tm, D), lambda i: (i, 0))
    res_ = _call(body, name=name, grid=(S // tm,),
                 in_specs=[a_spec, halo_prev, v_spec, pl.BlockSpec((3, F), lambda i: (0, 0)),
                           pl.BlockSpec((1, F), lambda i: (0, 0)), pl.BlockSpec((F, D), lambda i: (0, 0)), row],
                 out_specs=(row, a_spec, a_spec),
                 out_shape=(jax.ShapeDtypeStruct((S, D), F32), jax.ShapeDtypeStruct((S, F), BF),
                            jax.ShapeDtypeStruct((S, F), BF)),
                 args=(up, up, up, conv_w, conv_b.reshape(1, F), w_down, res), comm=comm)
    return res_[0], res_[1], res_[2], list(res_[3:])


def _shift_up_pair(a, nxt):
    n = a.shape[0]
    r8 = lax.broadcasted_iota(jnp.int32, (8,) + a.shape[1:], 0)
    out = []
    for k in (1, 2):
        rolled = pltpu.roll(a, n - k, 0)
        tail = jnp.where(r8 >= 8 - k, pltpu.roll(nxt, 8 - k, 0), rolled[n - 8:n])
        out.append(jnp.concatenate([rolled[0:n - 8], tail], axis=0))
    return out


def _ffn_bwd(dx, w_down, up, ac, conv_w, comm=()):
    S = up.shape[0]
    F = D_FF
    D = dx.shape[1]
    tm = _pick(S, 256)
    t8 = tm // 8
    nt = S // tm
    a_spec, v_spec, _ = _ffn_specs(S, tm)

    def nxt(width, col):
        return pl.BlockSpec((8, width), lambda i: (jnp.minimum((i + 1) * t8, S // 8 - 1), col))

    def body(dx_ref, dxn_ref, wd_ref, ac_ref, acn_ref, a_ref, v_ref, vn_ref, w_ref, dup_ref, dw_ref, db_ref):
        i = pl.program_id(0)

        @pl.when(i == 0)
        def _():
            dw_ref[...] = jnp.zeros_like(dw_ref)
            db_ref[...] = jnp.zeros_like(db_ref)

        dx16 = dx_ref[...].astype(BF)
        dxn16 = dxn_ref[...].astype(BF)
        last = i == nt - 1

        def dconv(dact, ac, v):
            cdf, pdf = _gelu_parts(ac)
            return cdf, dact * v * (cdf + ac * pdf)

        for c0 in range(0, F, FFN_CHUNK):
            cs = slice(c0, c0 + FFN_CHUNK)
            wd = wd_ref[cs, :]
            dact = _dot(dx16, wd, _NT)
            ac = ac_ref[:, cs].astype(F32)
            cdf, dac = dconv(dact, ac, v_ref[:, cs].astype(F32))
            dup_ref[:, F + c0:F + c0 + FFN_CHUNK] = (dact * ac * cdf).astype(BF)
            _, dac_n = dconv(_dot(dxn16, wd, _NT), acn_ref[:, cs].astype(F32), vn_ref[:, cs].astype(F32))
            d1, d2 = _shift_up_pair(dac, jnp.where(last, 0.0, dac_n))
            w = w_ref[:, cs]
            dup_ref[:, cs] = (w[2:3, :] * dac + w[1:2, :] * d1 + w[0:1, :] * d2).astype(BF)
            a = a_ref[:, cs].astype(F32)
            db_ref[:, cs] += jnp.sum(dac, axis=0, keepdims=True)
            dw_ref[0:1, cs] += jnp.sum(d2 * a, axis=0, keepdims=True)
            dw_ref[1:2, cs] += jnp.sum(d1 * a, axis=0, keepdims=True)
            dw_ref[2:3, cs] += jnp.sum(dac * a, axis=0, keepdims=True)

    w_spec = pl.BlockSpec((3, F), lambda i: (0, 0))
    b_spec = pl.BlockSpec((1, F), lambda i: (0, 0))
    res = _call(body, name="ffn_bwd", grid=(nt,),
                in_specs=[pl.BlockSpec((tm, D), lambda i: (i, 0)), nxt(D, 0), pl.BlockSpec((F, D), lambda i: (0, 0)),
                          a_spec, nxt(F, 0), a_spec, v_spec, nxt(F, 1), w_spec],
                out_specs=(pl.BlockSpec((tm, 2 * F), lambda i: (i, 0)), w_spec, b_spec),
                out_shape=(jax.ShapeDtypeStruct((S, 2 * F), BF), jax.ShapeDtypeStruct((3, F), F32),
                           jax.ShapeDtypeStruct((1, F), F32)),
                args=(dx, dx, w_down, ac, ac, up, up, up, conv_w), comm=comm)
    return res[0], res[1], res[2], list(res[3:])


def _adamw(parts, w, m, v, name, row0=0, prev=None):
    R, C = w.shape
    Rp = parts.shape[1]
    tr = Rp
    for cand in (512, 256, 128, 64, 32, 16):
        if Rp % cand == 0 and row0 % cand == 0 and cand * C * 4 <= (1 << 21):
            tr = cand
            break
    b0 = row0 // tr
    c1 = 1.0 / (1.0 - ADAM_B1 ** ADAM_STEP)
    c2 = 1.0 / (1.0 - ADAM_B2 ** ADAM_STEP)

    def body(p_ref, w_ref, m_ref, v_ref, *rest):
        g_ref, d_ref, nm_ref, nv_ref = rest[-4:]
        g = p_ref[0].astype(F32)
        for j in range(1, N_DEV):
            g = g + p_ref[j].astype(F32)
        nm = ADAM_B1 * m_ref[...] + (1.0 - ADAM_B1) * g
        nv = ADAM_B2 * v_ref[...] + (1.0 - ADAM_B2) * (g * g)
        g_ref[...] = g
        nm_ref[...] = nm
        nv_ref[...] = nv
        d_ref[...] = -ADAM_LR * ((nm * c1) / (jnp.sqrt(nv * c2) + ADAM_EPS) + ADAM_WD * w_ref[...])

    blk = pl.BlockSpec((tr, C), lambda i: (b0 + i, 0))
    out = jax.ShapeDtypeStruct((R, C), F32)
    carried = [] if prev is None else list(prev)
    return pl.pallas_call(
        body, name=name,
        out_shape=(out, out, out, out),
        grid=(Rp // tr,),
        in_specs=[pl.BlockSpec((N_DEV, tr, C), lambda i: (0, i, 0)), blk, blk, blk]
        + [pl.BlockSpec(memory_space=pl.ANY)] * len(carried),
        out_specs=(blk, blk, blk, blk),
        input_output_aliases={4 + k: k for k in range(len(carried))},
        compiler_params=_params("parallel"),
    )(parts, w, m, v, *carried)


def _dil(t, dil):
    S, C = t.shape
    if dil == 1:
        return t.reshape(1, S, C)
    return t.reshape(S // dil, dil, C).transpose(1, 0, 2)


def _undil(t):
    d, L, C = t.shape
    if d == 1:
        return t.reshape(L, C)
    return t.transpose(1, 0, 2).reshape(L * d, C)


def _group_qkv(proj, g):
    dil = DSW_GROUPS[g][1]
    if dil == 1:
        p3 = _dil(proj, 1)
        return (p3, p3, p3), (g, W_A // OUT_A + g, 2 * W_A // OUT_A + g)
    c0 = g * OUT_A
    return tuple(_dil(proj[:, o + c0:o + c0 + OUT_A], dil) for o in (0, W_A, 2 * W_A)), (0, 0, 0)


_COL_SHARDED = ("w_in", "w_br", "w_up")
_ROW_SHARDED = ("w_o", "w_down")


class _Plan:
    def __init__(self):
        self.riders = {}
        self.landed = {}

    def ride(self, slot, key, kind, x):
        self.riders.setdefault(slot, []).append((key, kind, x))

    def run(self, slot, fn, *args, **kw):
        items = self.riders.pop(slot, [])
        res = fn(*args, comm=[(kind, x) for _, kind, x in items], **kw)
        for (key, _, _), r in zip(items, res[-1]):
            self.landed[key] = r
        return res[0] if len(res) == 2 else res[:-1]

    def weight(self, n, l):
        g = self.landed[(n, l)]
        if n in _COL_SHARDED:
            return g.transpose(1, 0, 2).reshape(g.shape[1], -1)
        return g.reshape(-1, g.shape[2])

    def scatter(self, slot, n, l, full, part=0):
        K, N = full.shape
        if n in _COL_SHARDED:
            blocks = full.reshape(K, N_DEV, N // N_DEV).transpose(1, 0, 2)
        else:
            blocks = full.reshape(N_DEV, K // N_DEV, N)
        self.ride(slot, ("d" + n, l, part), "scatter", blocks)


def _layer_fwd(x, p, plan, l):
    proj, h = plan.run(f"proj_{l}", _norm_matmul, x, p["norm1"][l], plan.weight("w_in", l), name=f"proj_{l}",
                       tm=512, tn=1024)
    o_g, l_g, qkv_g = [], [], []
    for g in range(N_GROUPS):
        qkv, cols = _group_qkv(proj, g)
        og, lg = plan.run(f"attn_a_fwd_g{g}_{l}", _attn_a_fwd, qkv, cols, g)
        o_g.append(_undil(og))
        l_g.append(_undil(lg))
        qkv_g.append((qkv, cols))
    ob = plan.run(f"sb_fwd_{l}", _sb_fwd, proj)
    x1, merged, oa, lse = plan.run(f"wo_{l}", _merge_wo, o_g, l_g, ob, proj, p["b_gate"][l], plan.weight("w_br", l),
                                   plan.weight("w_o", l), x, f"wo_{l}")
    up, h2 = plan.run(f"up_{l}", _norm_matmul, x1, p["norm2"][l], plan.weight("w_up", l), name=f"up_{l}",
                      tm=512, tn=1408)
    x2, act, ac = plan.run(f"down_{l}", _ffn_down, up, p["conv_w"][l], p["conv_b"][l], plan.weight("w_down", l), x1,
                           name=f"down_{l}")
    saved = dict(x=x, h=h, proj=proj, qkv_g=qkv_g, oa=oa, ob=ob, lse=lse, merged=merged, x1=x1, h2=h2, up=up, act=act, ac=ac)
    return x2, saved


def _layer_bwd(dx2, sv, p, plan, l):
    gr = {}
    dwd = plan.run(f"dw_down_{l}", _matmul, sv["act"], dx2, mode="tn", out_dtype=BF, name=f"dw_down_{l}",
                   tm=1408, tn=1024, tk=2048)
    plan.scatter(f"d_h2_{l}", "w_down", l, dwd)
    dup, gr["conv_w"], dcb = plan.run(f"ffn_bwd_{l}", _ffn_bwd, dx2, plan.weight("w_down", l), sv["up"], sv["ac"],
                                      p["conv_w"][l])
    gr["conv_b"] = dcb[0]
    dx1, dn2 = plan.run(f"d_h2_{l}", _matmul_rms_bwd, dup, plan.weight("w_up", l), sv["x1"], p["norm2"][l], dx2,
                        name=f"d_h2_{l}")
    dwu = plan.run(f"dw_up_{l}", _matmul, sv["h2"], dup, mode="tn", out_dtype=BF, name=f"dw_up_{l}",
                   tm=1024, tn=1408, tk=2048)
    plan.scatter(f"sb_bwd_{l}", "w_up", l, dwu)
    gr["norm2"] = dn2[0]
    dwo = plan.run(f"dw_o_{l}", _matmul, sv["merged"], dx1, mode="tn", out_dtype=BF, name=f"dw_o_{l}",
                   tm=1024, tn=1024, tk=2048)
    plan.scatter(f"dw_in_{l}", "w_o", l, dwo)
    dya, dyb, doa, dob, stats, dgate, dbg = _merge_bwd(dx1, plan.weight("w_o", l), sv["oa"], sv["ob"], sv["lse"],
                                                       sv["proj"], p["b_gate"][l], plan.weight("w_br", l))
    gr["b_gate"] = dbg[0]
    dwa = plan.run(f"dw_bra_{l}", _matmul, sv["oa"], dya, mode="tn", out_dtype=BF, name=f"dw_bra_{l}",
                   tm=256, tn=1024, tk=2048)
    dwb = plan.run(f"dw_brb_{l}", _matmul, sv["ob"], dyb, mode="tn", out_dtype=BF, name=f"dw_brb_{l}",
                   tm=256, tn=1024, tk=2048)
    plan.scatter(f"dw_in_{l}", "w_br", l, jnp.concatenate([dwa, dwb], axis=0))
    proj = sv["proj"]
    dq_a, dk_a, dv_a = [], [], []
    for g, (_, dil) in enumerate(DSW_GROUPS):
        qkv, cols = sv["qkv_g"][g]
        dqg, dkg, dvg = _attn_a_bwd(qkv, cols, _dil(doa, dil), _dil(stats, dil), g)
        dq_a.append(_undil(dqg))
        dk_a.append(_undil(dkg))
        dv_a.append(_undil(dvg))
    dqb, dkb, dvb = plan.run(f"sb_bwd_{l}", _sb_bwd, proj, dob, sv["ob"])
    dproj = jnp.concatenate(dq_a + dk_a + dv_a + [dqb.astype(BF), dkb.astype(BF), dvb.astype(BF), dgate], axis=1)
    dx, dn1 = plan.run(f"d_h_{l}", _matmul_rms_bwd, dproj, plan.weight("w_in", l), sv["x"], p["norm1"][l], dx1,
                       name=f"d_h_{l}")
    if l > 0:
        dwi = plan.run(f"dw_in_{l}", _matmul, sv["h"], dproj, mode="tn", out_dtype=BF, name=f"dw_in_{l}",
                       tm=1024, tn=1280, tk=2048)
        plan.scatter(f"ffn_bwd_{l - 1}", "w_in", l, dwi)
    else:
        half = D_MODEL // 2
        for part in range(2):
            name = f"dw_in_{l}" if part == 0 else f"dw_in_{l}_rest"
            dwi = plan.run(name, _matmul, sv["h"], dproj, mode="tn", out_dtype=BF, name=name, tm=half, tn=1280,
                           tk=2048, m_cols=half, m_off=part * half)
            plan.scatter(f"dw_in_{l}_rest" if part == 0 else "alone", "w_in", l, dwi, part)
    gr["norm1"] = dn1[0]
    return dx, gr


def kernel(x, norm1, w_in, b_gate, w_br, w_o, norm2, w_up, conv_w, conv_b, w_down, norm_f, loss_target, m_norm1, m_w_in, m_b_gate, m_w_br, m_w_o, m_norm2, m_w_up, m_conv_w, m_conv_b, m_w_down, m_norm_f, v_norm1, v_w_in, v_b_gate, v_w_br, v_w_o, v_norm2, v_w_up, v_conv_w, v_conv_b, v_w_down, v_norm_f):
    depth = norm1.shape[0]
    me = 4 * lax.axis_index("x") + 2 * lax.axis_index("y") + lax.axis_index("c")
    shards = dict(w_in=w_in, w_br=w_br, w_o=w_o, w_up=w_up, w_down=w_down)
    moments_m = dict(norm1=m_norm1, w_in=m_w_in, b_gate=m_b_gate, w_br=m_w_br, w_o=m_w_o, norm2=m_norm2,
                     w_up=m_w_up, conv_w=m_conv_w, conv_b=m_conv_b, w_down=m_w_down, norm_f=m_norm_f)
    moments_v = dict(norm1=v_norm1, w_in=v_w_in, b_gate=v_b_gate, w_br=v_w_br, w_o=v_w_o, norm2=v_norm2,
                     w_up=v_w_up, conv_w=v_conv_w, conv_b=v_conv_b, w_down=v_w_down, norm_f=v_norm_f)

    plan = _Plan()
    wb = {n: s.astype(BF) for n, s in shards.items()}
    p = dict(norm1=norm1, b_gate=b_gate, norm2=norm2, conv_b=conv_b)
    cw = _all_gather(conv_w, "gather_conv_w")
    p["conv_w"] = cw.transpose(1, 2, 0, 3).reshape(depth, 3, D_FF)
    plan.landed[("w_in", 0)] = _all_gather_via_sibling(wb["w_in"][0], "gather_w_in_0")
    for l in range(depth):
        plan.ride(f"proj_{l}", ("w_down", l), "gather", wb["w_down"][l])
        plan.ride(f"attn_a_fwd_g0_{l}" if l == 0 else f"down_{l - 1}", ("w_br", l), "gather", wb["w_br"][l])
        plan.ride(f"attn_a_fwd_g0_{l}" if l == 0 else f"down_{l - 1}", ("w_o", l), "gather", wb["w_o"][l])
        plan.ride(f"sb_fwd_{l}", ("w_up", l), "gather", wb["w_up"][l])
        if l + 1 < depth:
            plan.ride(f"up_{l}", ("w_in", l + 1), "gather", wb["w_in"][l + 1])

    xs = x[0]
    saved = []
    for l in range(depth):
        xs, sv = _layer_fwd(xs, p, plan, l)
        saved.append(sv)
    loss_part, dx, dnf = _loss_head(xs, norm_f, loss_target[0])
    loss = lax.psum(loss_part[0, 0], ("x", "y", "c"))

    grads = [None] * depth
    for l in reversed(range(depth)):
        dx, grads[l] = _layer_bwd(dx, saved[l], p, plan, l)
    grad_x = dx[None]
    (key, _, last), = plan.riders.pop("alone")
    plan.landed[key] = _all_to_all(last, "scatter_w_in_rest")
    assert not plan.riders, sorted(plan.riders)

    out_g, out_d, out_m, out_v = {}, {}, {}, {}
    for n in _COL_SHARDED + _ROW_SHARDED:
        shp = shards[n].shape
        flat = (shp[0] * shp[1], shp[2])
        res = None
        for l in range(depth):
            row = l * shp[1]
            for key in sorted(k for k in plan.landed if k[:2] == ("d" + n, l)):
                parts = plan.landed[key]
                res = _adamw(parts, shards[n].reshape(flat), moments_m[n].reshape(flat), moments_v[n].reshape(flat),
                             f"adamw_{n}_{l}_{key[2]}", row0=row, prev=res)
                row += parts.shape[1]
        out_g[n], out_d[n], out_m[n], out_v[n] = [r.reshape(shp) for r in res]

    small = ("norm1", "b_gate", "norm2", "conv_b")
    vecs = [jnp.stack([grads[l][n] for l in range(depth)]).reshape(-1) for n in small]
    vecs.append(dnf.reshape(-1))
    vecs.append(jnp.stack([grads[l]["conv_w"] for l in range(depth)]).reshape(-1))
    sizes = [v.shape[0] for v in vecs]
    flat = jnp.concatenate(vecs)
    n_small = sum(sizes[:-1])
    pad = (-flat.shape[0]) % 1024
    flat = jnp.pad(flat, (0, pad)).reshape(-1, 128)
    allp = _all_gather(flat, "gather_small_grads").reshape(N_DEV, -1)
    rep_w = jnp.concatenate([norm1.reshape(-1), b_gate.reshape(-1), norm2.reshape(-1), conv_b.reshape(-1), norm_f])
    rep_m = jnp.concatenate([moments_m[n].reshape(-1) for n in small] + [m_norm_f])
    rep_v = jnp.concatenate([moments_v[n].reshape(-1) for n in small] + [v_norm_f])
    rows = n_small // 128
    res = _adamw(allp[:, :n_small].reshape(N_DEV, rows, 128), rep_w.reshape(rows, 128), rep_m.reshape(rows, 128),
                 rep_v.reshape(rows, 128), "adamw_small")
    off = 0
    for n, sz in zip(small + ("norm_f",), sizes[:-1]):
        shp = norm_f.shape if n == "norm_f" else p[n].shape
        out_g[n], out_d[n], out_m[n], out_v[n] = [r.reshape(-1)[off:off + sz].reshape(shp) for r in res]
        off += sz
    f = conv_w.shape[2]
    cwp = allp[:, n_small:n_small + sizes[-1]].reshape(N_DEV, depth * 3, D_FF)
    cwp = lax.dynamic_slice_in_dim(cwp, me * f, f, axis=2)
    res = _adamw(cwp, conv_w.reshape(depth * 3, f), m_conv_w.reshape(depth * 3, f), v_conv_w.reshape(depth * 3, f),
                 "adamw_conv_w")
    out_g["conv_w"], out_d["conv_w"], out_m["conv_w"], out_v["conv_w"] = [r.reshape(conv_w.shape) for r in res]

    order = ("norm1", "w_in", "b_gate", "w_br", "w_o", "norm2", "w_up", "conv_w", "conv_b", "w_down", "norm_f")
    return (loss, grad_x, *[out_g[n] for n in order], *[out_d[n] for n in order],
            *[out_m[n] for n in order], *[out_v[n] for n in order])
```
